```python
import jax, jax.numpy as jnp
from jax import lax
import numpy as np

D_MODEL = 1024
BATCH = 16
SEQ = 2048
DEPTH = 1
DEC_BATCH = 32
DEC_SEQ = 32
PAST_LEN = 2048

CHUNK = 64

RWKV_HEADS = 8
RWKV_HEAD_DIM = 64
RWKV_WIDTH = RWKV_HEADS * RWKV_HEAD_DIM
DECAY_LORA = 64
ICLR_LORA = 64
GATE_LORA = 128
RWKV_GN_EPS = RWKV_HEAD_DIM * 1e-5
L2_EPS = 1e-12

FOX_HEADS = 8
FOX_HEAD_DIM = 64
FOX_WIDTH = FOX_HEADS * FOX_HEAD_DIM
Q_BLOCK = 128
FORGET_BIAS_INIT = 2.0

N_EXPERTS = 256
TOP_K = 8
N_GROUPS = 8
TOPK_GROUPS = 4
EXPERTS_PER_GROUP = N_EXPERTS // N_GROUPS
D_EXPERT = 256
D_SHARED = 256
ROUTED_SCALE = 2.5
EXPERT_BLOCK = 128

RMS_EPS = 1e-6

RWKV_SPLITS = [RWKV_WIDTH, 2 * RWKV_WIDTH, 3 * RWKV_WIDTH, 3 * RWKV_WIDTH + DECAY_LORA, 3 * RWKV_WIDTH + DECAY_LORA + ICLR_LORA]
RWKV_COLS = 3 * RWKV_WIDTH + DECAY_LORA + ICLR_LORA + GATE_LORA
FOX_SPLITS = [FOX_WIDTH, 2 * FOX_WIDTH, 3 * FOX_WIDTH, 4 * FOX_WIDTH]
FOX_COLS = 4 * FOX_WIDTH + FOX_HEADS
GATE_COLS = 2 * D_MODEL
IN_COLS = RWKV_COLS + FOX_COLS + GATE_COLS

kernel_name = 'hybrid_rwkv7_fox_moe_stream_step'


def _rmsnorm(x, g):
    xf = x.astype(jnp.float32)
    y = xf * lax.rsqrt(jnp.mean(xf * xf, axis=-1, keepdims=True) + RMS_EPS)
    return (y * g.astype(jnp.float32)).astype(x.dtype)


def _swiglu(h, w_gate, w_up, w_down):
    return (jax.nn.silu(h @ w_gate) * (h @ w_up)) @ w_down


def _rwkv7_mixer(p, shift0, wkv0, lp):
    B, T, _ = p.shape
    H, N = RWKV_HEADS, RWKV_HEAD_DIM
    f32 = jnp.float32
    prev = jnp.concatenate([shift0[:, None, :].astype(p.dtype), p[:, :-1]], axis=1)
    pm = p + (prev - p) * lp['rwkv_mu']
    r, k, v, wd, ad, gd = jnp.split(pm, RWKV_SPLITS, axis=-1)
    w = -jax.nn.softplus(-(lp['rwkv_w0'] + jnp.tanh(wd) @ lp['rwkv_w_lora_b'])) - 0.5
    decay = jnp.exp(-jnp.exp(w.astype(f32)))
    a = jax.nn.sigmoid((lp['rwkv_a0'] + ad @ lp['rwkv_a_lora_b']).astype(f32))
    g = jax.nn.sigmoid(gd) @ lp['rwkv_g_lora_b']
    heads = lambda t: t.astype(f32).reshape(B, T, H, N)
    kk = heads(k * lp['rwkv_k_k'])
    kk = kk / jnp.maximum(jnp.linalg.norm(kk, axis=-1, keepdims=True), L2_EPS)
    kf = k.astype(f32) * (1.0 + (a - 1.0) * lp['rwkv_k_a'].astype(f32))
    rh, kh, vh, ah, wh = heads(r), heads(kf), heads(v), heads(a), heads(decay)

    def step(S, inp):
        r_t, w_t, k_t, v_t, kk_t, a_t = inp
        s_kk = jnp.einsum('bhvk,bhk->bhv', S, kk_t)
        S = (S * w_t[:, :, None, :]
             - s_kk[..., None] * (kk_t * a_t)[:, :, None, :]
             + v_t[..., None] * k_t[:, :, None, :])
        return S, jnp.einsum('bhvk,bhk->bhv', S, r_t)

    seq_first = lambda t: jnp.swapaxes(t, 0, 1)
    S_T, o = lax.scan(step, wkv0.astype(f32), tuple(seq_first(t) for t in (rh, wh, kh, vh, kk, ah)))
    o = seq_first(o)
    o_mean = jnp.mean(o, axis=-1, keepdims=True)
    o_var = jnp.mean(jnp.square(o - o_mean), axis=-1, keepdims=True)
    o = ((o - o_mean) * lax.rsqrt(o_var + RWKV_GN_EPS)).reshape(B, T, RWKV_WIDTH)
    o = o * lp['rwkv_ln_w'].astype(f32) + lp['rwkv_ln_b'].astype(f32)
    bonus = jnp.sum(rh * kh * lp['rwkv_r_k'].astype(f32).reshape(H, N), axis=-1, keepdims=True) * vh
    y = (o + bonus.reshape(B, T, RWKV_WIDTH)).astype(p.dtype) * g
    return y, S_T.astype(p.dtype), p[:, -1]


def _fox_block(q, k, v, f_q, f_k, q_pos, k_pos):
    s = jnp.einsum('bqhd,bkhd->bhqk', q, k).astype(jnp.float32) * (FOX_HEAD_DIM ** -0.5)
    s = s + jnp.swapaxes(f_q, 1, 2)[..., :, None] - jnp.swapaxes(f_k, 1, 2)[..., None, :]
    s = jnp.where(k_pos[None, None, None, :] <= q_pos[None, None, :, None], s, -jnp.inf)
    pr = jax.nn.softmax(s, axis=-1)
    return jnp.einsum('bhqk,bkhd->bqhd', pr.astype(v.dtype), v)


def _fox_mixer(p, past_k, past_v, past_logf, lp):
    B, T, _ = p.shape
    H, Dh = FOX_HEADS, FOX_HEAD_DIM
    q, k, v, og, fl = jnp.split(p, FOX_SPLITS, axis=-1)
    q = _rmsnorm(q.reshape(B, T, H, Dh), lp['fox_q_norm'])
    k = _rmsnorm(k.reshape(B, T, H, Dh), lp['fox_k_norm'])
    v = v.reshape(B, T, H, Dh)
    logf = jax.nn.log_sigmoid((fl + lp['fox_f_bias']).astype(jnp.float32))
    k_all = jnp.concatenate([past_k.astype(k.dtype), k], axis=1)
    v_all = jnp.concatenate([past_v.astype(v.dtype), v], axis=1)
    f_all = jnp.cumsum(jnp.concatenate([past_logf.astype(jnp.float32), logf], axis=1), axis=1)
    L = k_all.shape[1]
    P0 = L - T
    k_pos = jnp.arange(L)
    q_pos = P0 + jnp.arange(T)
    f_q = f_all[:, P0:]
    if T <= Q_BLOCK:
        o = _fox_block(q, k_all, v_all, f_q, f_all, q_pos, k_pos)
    else:
        nb = T // Q_BLOCK
        qb = jnp.swapaxes(q.reshape(B, nb, Q_BLOCK, H, Dh), 0, 1)
        fqb = jnp.swapaxes(f_q.reshape(B, nb, Q_BLOCK, H), 0, 1)
        pb = q_pos.reshape(nb, Q_BLOCK)
        o = lax.map(lambda blk: _fox_block(blk[0], k_all, v_all, blk[1], f_all, blk[2], k_pos), (qb, fqb, pb))
        o = jnp.swapaxes(o, 0, 1).reshape(B, T, H, Dh)
    y = o.reshape(B, T, FOX_WIDTH) * jax.nn.sigmoid(og)
    return y, k, v, logf


def _route(h, w_router, router_bias):
    n = h.shape[0]
    scores = jax.nn.sigmoid((h @ w_router).astype(jnp.float32))
    sel = scores + router_bias.astype(jnp.float32)
    grp_score = jnp.sum(lax.top_k(sel.reshape(n, N_GROUPS, EXPERTS_PER_GROUP), 2)[0], axis=-1)
    _, top_grp = lax.top_k(grp_score, TOPK_GROUPS)
    grp_mask = jnp.sum(jax.nn.one_hot(top_grp, N_GROUPS, dtype=jnp.float32), axis=1) > 0
    exp_mask = jnp.repeat(grp_mask, EXPERTS_PER_GROUP, axis=1)
    _, idx = lax.top_k(jnp.where(exp_mask, sel, -jnp.inf), TOP_K)
    w = jnp.take_along_axis(scores, idx, axis=1)
    w = w / jnp.sum(w, axis=-1, keepdims=True) * ROUTED_SCALE
    return idx, w


def _routed_experts(h, idx, wts, w_gate, w_up, w_down):
    n, d = h.shape
    m = n * TOP_K
    flat_e = idx.reshape(m)
    flat_tok = jnp.arange(m, dtype=jnp.int32) // TOP_K
    flat_w = wts.reshape(m).astype(h.dtype)
    counts = jnp.bincount(flat_e, length=N_EXPERTS)
    padded = (counts + EXPERT_BLOCK - 1) // EXPERT_BLOCK * EXPERT_BLOCK
    pad_end = jnp.cumsum(padded)
    pad_start = pad_end - padded
    grp_start = jnp.cumsum(counts) - counts
    order = jnp.argsort(flat_e)
    se = flat_e[order]
    dest = pad_start[se] + jnp.arange(m, dtype=jnp.int32) - grp_start[se]
    n_blocks = (m + N_EXPERTS * (EXPERT_BLOCK - 1) + EXPERT_BLOCK - 1) // EXPERT_BLOCK
    n_slots = n_blocks * EXPERT_BLOCK
    slot_tok = jnp.zeros((n_slots,), jnp.int32).at[dest].set(flat_tok[order])
    slot_w = jnp.zeros((n_slots,), h.dtype).at[dest].set(flat_w[order])
    block_e = jnp.minimum(jnp.searchsorted(pad_end, jnp.arange(n_blocks, dtype=jnp.int32) * EXPERT_BLOCK, side='right'), N_EXPERTS - 1)

    def body(out, blk):
        tok, sw, e = blk
        xb = h[tok]
        yb = _swiglu(xb, w_gate[e], w_up[e], w_down[e])
        return out.at[tok].add(yb * sw[:, None]), None

    out, _ = lax.scan(body, jnp.zeros_like(h),
                      (slot_tok.reshape(n_blocks, EXPERT_BLOCK), slot_w.reshape(n_blocks, EXPERT_BLOCK), block_e))
    return out


def _layer(x, c, shift0, wkv0, past_k, past_v, past_logf, lp):
    B, T, D = x.shape
    mod = jax.nn.silu(c) @ lp['w_ada'] + lp['b_ada']
    sh1, sc1, gt1, sh2, sc2, gt2 = (m_[:, None, :] for m_ in jnp.split(mod, 6, axis=-1))
    h = _rmsnorm(x, lp['norm1_g']) * (1.0 + sc1) + sh1
    proj = h @ lp['w_in']
    p_rwkv, p_fox, p_gate = jnp.split(proj, [RWKV_COLS, RWKV_COLS + FOX_COLS], axis=-1)
    y_a, wkv_new, shift_new = _rwkv7_mixer(p_rwkv, shift0, wkv0, lp)
    y_b, k_new, v_new, logf_new = _fox_mixer(p_fox, past_k, past_v, past_logf, lp)
    g_a, g_b = jnp.split(jax.nn.sigmoid(p_gate), 2, axis=-1)
    merged = g_a * (y_a @ lp['w_out_rwkv']) + g_b * (y_b @ lp['w_out_fox'])
    x = x + gt1 * (merged @ lp['w_out'])
    h2 = (_rmsnorm(x, lp['norm2_g']) * (1.0 + sc2) + sh2).reshape(B * T, D)
    idx, wts = _route(h2, lp['w_router'], lp['router_bias'])
    ffn = (_routed_experts(h2, idx, wts, lp['w_exp_gate'], lp['w_exp_up'], lp['w_exp_down'])
           + _swiglu(h2, lp['w_sh_gate'], lp['w_sh_up'], lp['w_sh_down']))
    x = x + gt2 * ffn.reshape(B, T, D)
    return x, wkv_new, shift_new, k_new, v_new, logf_new


def setup_inputs(seed: int = 0) -> dict:
    key = jax.random.key(seed)
    keys = jax.random.split(key, 64)
    cnt = [0]

    def nxt():
        cnt[0] += 1
        return keys[cnt[0] - 1]

    def nrm(shape, scale):
        return jax.random.normal(nxt(), shape, jnp.float32) * scale

    def unif(shape, lo, hi):
        return jax.random.uniform(nxt(), shape, jnp.float32, lo, hi)

    L, D = DEPTH, D_MODEL
    return {
        'x_prompt': nrm((BATCH, SEQ, D), 1.0),
        'x_sample': nrm((DEC_BATCH, DEC_SEQ, D), 1.0),
        'c_prompt': nrm((BATCH, D), 1.0),
        'c_sample': nrm((DEC_BATCH, D), 1.0),
        'state_rwkv_wkv': nrm((L, DEC_BATCH, RWKV_HEADS, RWKV_HEAD_DIM, RWKV_HEAD_DIM), 0.5),
        'state_rwkv_shift': nrm((L, DEC_BATCH, RWKV_COLS), 1.0),
        'cache_fox_k': nrm((L, DEC_BATCH, PAST_LEN, FOX_HEADS, FOX_HEAD_DIM), 1.0),
        'cache_fox_v': nrm((L, DEC_BATCH, PAST_LEN, FOX_HEADS, FOX_HEAD_DIM), 1.0),
        'cache_fox_logf': jax.nn.log_sigmoid(FORGET_BIAS_INIT + nrm((L, DEC_BATCH, PAST_LEN, FOX_HEADS), 1.0)),
        'w_ada': nrm((L, D, 6 * D), 0.5 * D ** -0.5),
        'b_ada': nrm((L, 6 * D), 0.02),
        'norm1_g': 1.0 + nrm((L, D), 0.02),
        'norm2_g': 1.0 + nrm((L, D), 0.02),
        'w_in': nrm((L, D, IN_COLS), D ** -0.5),
        'rwkv_mu': unif((L, RWKV_COLS), 0.0, 1.0),
        'rwkv_w0': unif((L, RWKV_WIDTH), -6.0, 0.0),
        'rwkv_w_lora_b': nrm((L, DECAY_LORA, RWKV_WIDTH), 0.5 * DECAY_LORA ** -0.5),
        'rwkv_a0': nrm((L, RWKV_WIDTH), 0.1),
        'rwkv_a_lora_b': nrm((L, ICLR_LORA, RWKV_WIDTH), 0.5 * ICLR_LORA ** -0.5),
        'rwkv_g_lora_b': nrm((L, GATE_LORA, RWKV_WIDTH), GATE_LORA ** -0.5),
        'rwkv_k_k': 0.85 + nrm((L, RWKV_WIDTH), 0.05),
        'rwkv_k_a': 1.0 + nrm((L, RWKV_WIDTH), 0.05),
        'rwkv_r_k': nrm((L, RWKV_WIDTH), 0.1),
        'rwkv_ln_w': 1.0 + nrm((L, RWKV_WIDTH), 0.02),
        'rwkv_ln_b': nrm((L, RWKV_WIDTH), 0.02),
        'fox_q_norm': 1.0 + nrm((L, FOX_HEAD_DIM), 0.02),
        'fox_k_norm': 1.0 + nrm((L, FOX_HEAD_DIM), 0.02),
        'fox_f_bias': FORGET_BIAS_INIT + nrm((L, FOX_HEADS), 0.5),
        'w_out_rwkv': nrm((L, RWKV_WIDTH, D), RWKV_WIDTH ** -0.5),
        'w_out_fox': nrm((L, FOX_WIDTH, D), FOX_WIDTH ** -0.5),
        'w_out': nrm((L, D, D), D ** -0.5),
        'w_router': nrm((L, D, N_EXPERTS), D ** -0.5),
        'router_bias': nrm((L, N_EXPERTS), 0.01),
        'w_exp_gate': nrm((L, N_EXPERTS, D, D_EXPERT), D ** -0.5),
        'w_exp_up': nrm((L, N_EXPERTS, D, D_EXPERT), D ** -0.5),
        'w_exp_down': nrm((L, N_EXPERTS, D_EXPERT, D), D_EXPERT ** -0.5),
        'w_sh_gate': nrm((L, D, D_SHARED), D ** -0.5),
        'w_sh_up': nrm((L, D, D_SHARED), D ** -0.5),
        'w_sh_down': nrm((L, D_SHARED, D), D_SHARED ** -0.5),
    }


def reference(x_prompt, x_sample, c_prompt, c_sample, state_rwkv_wkv, state_rwkv_shift,
              cache_fox_k, cache_fox_v, cache_fox_logf,
              w_ada, b_ada, norm1_g, norm2_g, w_in, rwkv_mu, rwkv_w0, rwkv_w_lora_b, rwkv_a0,
              rwkv_a_lora_b, rwkv_g_lora_b, rwkv_k_k, rwkv_k_a, rwkv_r_k, rwkv_ln_w, rwkv_ln_b,
              fox_q_norm, fox_k_norm, fox_f_bias, w_out_rwkv, w_out_fox, w_out, w_router, router_bias,
              w_exp_gate, w_exp_up, w_exp_down, w_sh_gate, w_sh_up, w_sh_down):
    params = dict(w_ada=w_ada, b_ada=b_ada, norm1_g=norm1_g, norm2_g=norm2_g, w_in=w_in,
                  rwkv_mu=rwkv_mu, rwkv_w0=rwkv_w0, rwkv_w_lora_b=rwkv_w_lora_b, rwkv_a0=rwkv_a0,
                  rwkv_a_lora_b=rwkv_a_lora_b, rwkv_g_lora_b=rwkv_g_lora_b, rwkv_k_k=rwkv_k_k,
                  rwkv_k_a=rwkv_k_a, rwkv_r_k=rwkv_r_k, rwkv_ln_w=rwkv_ln_w, rwkv_ln_b=rwkv_ln_b,
                  fox_q_norm=fox_q_norm, fox_k_norm=fox_k_norm, fox_f_bias=fox_f_bias,
                  w_out_rwkv=w_out_rwkv, w_out_fox=w_out_fox, w_out=w_out, w_router=w_router,
                  router_bias=router_bias, w_exp_gate=w_exp_gate, w_exp_up=w_exp_up,
                  w_exp_down=w_exp_down, w_sh_gate=w_sh_gate, w_sh_up=w_sh_up, w_sh_down=w_sh_down)
    nb = x_prompt.shape[0]
    y_p, y_s = x_prompt, x_sample
    st_p = [[] for _ in range(5)]
    st_s = [[] for _ in range(5)]
    for l in range(DEPTH):
        lp = {name: arr[l] for name, arr in params.items()}
        y_p, *new_p = _layer(y_p, c_prompt,
                             jnp.zeros((nb, RWKV_COLS), x_prompt.dtype),
                             jnp.zeros((nb, RWKV_HEADS, RWKV_HEAD_DIM, RWKV_HEAD_DIM), jnp.float32),
                             jnp.zeros((nb, 0, FOX_HEADS, FOX_HEAD_DIM), x_prompt.dtype),
                             jnp.zeros((nb, 0, FOX_HEADS, FOX_HEAD_DIM), x_prompt.dtype),
                             jnp.zeros((nb, 0, FOX_HEADS), jnp.float32), lp)
        y_s, *new_s = _layer(y_s, c_sample, state_rwkv_shift[l], state_rwkv_wkv[l],
                             cache_fox_k[l], cache_fox_v[l], cache_fox_logf[l], lp)
        for i in range(5):
            st_p[i].append(new_p[i])
            st_s[i].append(new_s[i])
    wkv_p, shift_p, k_p, v_p, logf_p = (jnp.stack(s, axis=0) for s in st_p)
    wkv_s, shift_s, k_s, v_s, logf_s = (jnp.stack(s, axis=0) for s in st_s)
    return (y_p, y_s, wkv_p, shift_p, k_p, v_p, logf_p, wkv_s, shift_s, k_s, v_s, logf_s)
```

```python
import functools
import math

import jax
import jax.numpy as jnp
from jax import lax
from jax.experimental import pallas as pl
from jax.experimental.pallas import tpu as pltpu

F32 = jnp.float32
BF16 = jnp.bfloat16
I32 = jnp.int32

D_MODEL = 1024
N_HEADS = 8
HEAD_DIM = 64
WIDTH = N_HEADS * HEAD_DIM
DECAY_LORA = 64
ICLR_LORA = 64
GATE_LORA = 128
RWKV_COLS = 3 * WIDTH + DECAY_LORA + ICLR_LORA + GATE_LORA
FOX_MAIN_COLS = 4 * WIDTH
GATE_COLS = 2 * D_MODEL
RWKV_GN_EPS = HEAD_DIM * 1e-5
L2_EPS = 1e-12
RMS_EPS = 1e-6
N_EXPERTS = 256
TOP_K = 8
N_GROUPS = 8
TOPK_GROUPS = 4
EXPERTS_PER_GROUP = N_EXPERTS // N_GROUPS
D_EXPERT = 256
ROUTED_SCALE = 2.5

LANES = 128
VMEM_LIMIT = 56 * 1024 * 1024
NEG_BIG = -1e30

NN = (((1,), (0,)), ((), ()))
NT = (((1,), (1,)), ((), ()))
TN = (((0,), (0,)), ((), ()))


def _cparams(sem):
    return pltpu.CompilerParams(dimension_semantics=sem, vmem_limit_bytes=VMEM_LIMIT)


def _dot(a, b, dims=NN):
    return lax.dot_general(a.astype(BF16), b.astype(BF16), dims, preferred_element_type=F32)


def _split2(a):
    hi = a.astype(BF16)
    lo = (a - hi.astype(F32)).astype(BF16)
    return hi, lo


def _split3(a):
    hi = a.astype(BF16)
    r1 = a - hi.astype(F32)
    mid = r1.astype(BF16)
    lo = (r1 - mid.astype(F32)).astype(BF16)
    return hi, mid, lo


def _dot3(a, b, dims=NN):
    ah, al = _split2(a)
    bh, bl = _split2(b)
    d = functools.partial(lax.dot_general, dimension_numbers=dims, preferred_element_type=F32)
    return d(ah, bh) + (d(ah, bl) + d(al, bh))


def _dot_exact_rhs(a_exact, b, dims=NN):
    ab = a_exact.astype(BF16)
    bh, bm, bl = _split3(b)
    d = functools.partial(lax.dot_general, dimension_numbers=dims, preferred_element_type=F32)
    return d(ab, bh) + (d(ab, bm) + d(ab, bl))


def _gsum(x, g_ref):
    hi, mid, lo = _split3(x)
    g = g_ref[...]
    d = functools.partial(jnp.dot, preferred_element_type=F32)
    return d(hi, g) + (d(mid, g) + d(lo, g))


def _sigmoid(x):
    return 1.0 / (1.0 + jnp.exp(-x))


def _softplus(x):
    return jnp.maximum(x, 0.0) + jnp.log1p(jnp.exp(-jnp.abs(x)))


def _silu(x):
    return x * _sigmoid(x)


def _group_ones():
    h = jnp.arange(WIDTH, dtype=I32) // HEAD_DIM
    return (h[:, None] == h[None, :]).astype(BF16)


def _ada_kernel(c_ref, w_ref, b_ref, o_ref):
    o_ref[...] = _dot(_silu(c_ref[...]), w_ref[...]) + b_ref[...]


def _ada(c, w_ada, b_ada):
    nb = c.shape[0]
    n_out = w_ada.shape[1]
    blk = D_MODEL
    return pl.pallas_call(
        _ada_kernel,
        grid=(n_out // blk,),
        in_specs=[pl.BlockSpec((nb, D_MODEL), lambda j: (0, 0)),
                  pl.BlockSpec((D_MODEL, blk), lambda j: (0, j)),
                  pl.BlockSpec((1, blk), lambda j: (0, j))],
        out_specs=pl.BlockSpec((nb, blk), lambda j: (0, j)),
        out_shape=jax.ShapeDtypeStruct((nb, n_out), F32),
        compiler_params=_cparams(("parallel",)),
        name="ada_mod",
    )(c, w_ada, b_ada.reshape(1, n_out))


def _inproj_kernel(x_ref, mod_ref, g1_ref, wr_ref, wf_ref, wfl_ref, wg_ref, qn_ref, kn_ref, fb_ref, gm_ref,
                   pr_ref, q_ref, k_ref, v_ref, sg_ref, lf_ref, gate_ref):
    bb, tt, d = x_ref.shape
    m = bb * tt
    x = x_ref[...]
    ms = jnp.mean(x * x, axis=-1, keepdims=True)
    h = x * lax.rsqrt(ms + RMS_EPS) * g1_ref[...]
    h = h * (1.0 + mod_ref[:, 1:2, :]) + mod_ref[:, 0:1, :]
    hb = h.reshape(m, d).astype(BF16)

    pr_ref[...] = jnp.dot(hb, wr_ref[...], preferred_element_type=F32).reshape(bb, tt, RWKV_COLS)

    f = jnp.dot(hb, wf_ref[...], preferred_element_type=F32)
    q = f[:, 0:WIDTH]
    k = f[:, WIDTH:2 * WIDTH]
    v = f[:, 2 * WIDTH:3 * WIDTH]
    og = f[:, 3 * WIDTH:4 * WIDTH]
    inv_hd = 1.0 / HEAD_DIM
    q = q * lax.rsqrt(_gsum(q * q, gm_ref) * inv_hd + RMS_EPS) * qn_ref[...]
    k = k * lax.rsqrt(_gsum(k * k, gm_ref) * inv_hd + RMS_EPS) * kn_ref[...]
    q_ref[...] = (q * (HEAD_DIM ** -0.5)).astype(BF16).reshape(bb, tt, WIDTH)
    k_ref[...] = k.reshape(bb, tt, WIDTH)
    v_ref[...] = v.reshape(bb, tt, WIDTH)
    sg_ref[...] = _sigmoid(og).reshape(bb, tt, WIDTH)

    fl = jnp.dot(hb, wfl_ref[...], preferred_element_type=F32)[:, 0:N_HEADS] + fb_ref[...]
    lf_ref[...] = (-_softplus(-fl)).reshape(bb, tt, N_HEADS)

    gate_ref[...] = _sigmoid(jnp.dot(hb, wg_ref[...], preferred_element_type=F32)).reshape(bb, tt, GATE_COLS)


def _const_spec(shape):
    nd = len(shape)
    return pl.BlockSpec(shape, lambda *_: (0,) * nd)


def _inproj(x, mod, g1, wr, wf, wfl, wg, qn, kn, fb, gmat, bb, tt):
    b, t, d = x.shape
    grid = (b // bb, t // tt)

    def tok(cols):
        return pl.BlockSpec((bb, tt, cols), lambda i, j: (i, j, 0))

    out_cols = [(RWKV_COLS, F32), (WIDTH, BF16), (WIDTH, F32), (WIDTH, F32), (WIDTH, F32), (N_HEADS, F32),
                (GATE_COLS, F32)]
    return pl.pallas_call(
        _inproj_kernel,
        grid=grid,
        in_specs=[tok(d),
                  pl.BlockSpec((bb, 6, d), lambda i, j: (i, 0, 0)),
                  _const_spec((1, d)),
                  _const_spec(wr.shape), _const_spec(wf.shape), _const_spec(wfl.shape), _const_spec(wg.shape),
                  _const_spec((1, WIDTH)), _const_spec((1, WIDTH)), _const_spec((1, N_HEADS)),
                  _const_spec((WIDTH, WIDTH))],
        out_specs=[tok(c) for c, _ in out_cols],
        out_shape=[jax.ShapeDtypeStruct((b, t, c), dt) for c, dt in out_cols],
        compiler_params=_cparams(("parallel", "parallel")),
        name="norm1_inproj",
    )(x, mod, g1, wr, wf, wfl, wg, qn, kn, fb, gmat)


def _cumsum_kernel(x_ref, init_ref, o_ref, carry):
    @pl.when(pl.program_id(1) == 0)
    def _():
        carry[...] = init_ref[0]

    x = x_ref[0]
    tt = x.shape[0]
    r = lax.broadcasted_iota(I32, (tt, tt), 0)
    c = lax.broadcasted_iota(I32, (tt, tt), 1)
    tri = (r >= c).astype(F32)
    cs = _dot_exact_rhs(tri, x) + carry[...]
    o_ref[0] = cs
    carry[...] = cs[tt - 1:tt, :]


def _cumsum_time(x, init, tt):
    b, t, h = x.shape
    return pl.pallas_call(
        _cumsum_kernel,
        grid=(b, t // tt),
        in_specs=[pl.BlockSpec((1, tt, h), lambda i, j: (i, j, 0)),
                  pl.BlockSpec((1, 1, h), lambda i, j: (i, 0, 0))],
        out_specs=pl.BlockSpec((1, tt, h), lambda i, j: (i, j, 0)),
        out_shape=jax.ShapeDtypeStruct((b, t, h), F32),
        scratch_shapes=[pltpu.VMEM((1, h), F32)],
        compiler_params=_cparams(("parallel", "arbitrary")),
        name="logf_cumsum",
    )(x, init)


def _fox_kernel(*refs, n_past_blocks, tq):
    if n_past_blocks:
        (q_ref, fq_ref, sg_ref, kp_ref, vp_ref, fkp_ref, kn_ref, vn_ref, fkn_ref,
         o_ref, m_scr, l_scr, acc_scr) = refs
    else:
        q_ref, fq_ref, sg_ref, kn_ref, vn_ref, fkn_ref, o_ref, m_scr, l_scr, acc_scr = refs
    qi = pl.program_id(1)
    ki = pl.program_id(2)
    nk = pl.num_programs(2)

    @pl.when(ki == 0)
    def _():
        m_scr[...] = jnp.full(m_scr.shape, NEG_BIG, F32)
        l_scr[...] = jnp.zeros(l_scr.shape, F32)
        acc_scr[...] = jnp.zeros(acc_scr.shape, F32)

    lane_a = lax.broadcasted_iota(I32, (tq, LANES), 1) < HEAD_DIM

    def step(k_ref, v_ref, fk_ref, diag):
        tk = k_ref.shape[1]
        if diag:
            rq = lax.broadcasted_iota(I32, (tq, tk), 0)
            ck = lax.broadcasted_iota(I32, (tq, tk), 1)
            visible = ck <= rq
        fq_all = fq_ref[0]
        for j in range(N_HEADS // 2):
            cols = slice(j * LANES, (j + 1) * LANES)
            qj = q_ref[0, :, cols]
            kb = k_ref[0, :, cols].astype(BF16)
            vb = v_ref[0, :, cols].astype(BF16)
            alphas, pvs = [], []
            for hh in range(2):
                h = 2 * j + hh
                qm = jnp.where(lane_a if hh == 0 else jnp.logical_not(lane_a), qj, jnp.zeros_like(qj))
                s = lax.dot_general(qm, kb, NT, preferred_element_type=F32)
                s = s + fq_all[:, h:h + 1] - fk_ref[0, h:h + 1, :]
                if diag:
                    s = jnp.where(visible, s, NEG_BIG)
                m_old = m_scr[h]
                m_new = jnp.maximum(m_old, jnp.max(s, axis=-1, keepdims=True))
                alpha = jnp.exp(m_old - m_new)
                p = jnp.exp(s - m_new)
                l_scr[h] = alpha * l_scr[h] + jnp.sum(p, axis=-1, keepdims=True)
                m_scr[h] = m_new
                alphas.append(alpha)
                pvs.append(jnp.dot(p.astype(BF16), vb, preferred_element_type=F32))
            acc_scr[:, cols] = (acc_scr[:, cols] * jnp.where(lane_a, alphas[0], alphas[1])
                                + jnp.where(lane_a, pvs[0], pvs[1]))

    if n_past_blocks:
        @pl.when(ki < n_past_blocks)
        def _():
            step(kp_ref, vp_ref, fkp_ref, False)

    kn = ki - n_past_blocks

    @pl.when(jnp.logical_and(kn >= 0, kn < qi))
    def _():
        step(kn_ref, vn_ref, fkn_ref, False)

    @pl.when(kn == qi)
    def _():
        step(kn_ref, vn_ref, fkn_ref, True)

    @pl.when(ki == nk - 1)
    def _():
        for j in range(N_HEADS // 2):
            cols = slice(j * LANES, (j + 1) * LANES)
            l = jnp.where(lane_a, l_scr[2 * j], l_scr[2 * j + 1])
            o_ref[0, :, cols] = acc_scr[:, cols] / l * sg_ref[0, :, cols]


def _fox_attention(q, fq, sg, k_new, v_new, fk_new_t, past=None, tq=512, tk_past=512):
    b, t, _ = q.shape
    nq = t // tq
    n_past_blocks = 0 if past is None else past[0].shape[1] // tk_past
    nk = n_past_blocks + nq

    def new_idx(i, qi, ki):
        return jnp.clip(ki - n_past_blocks, 0, qi)

    in_specs = [pl.BlockSpec((1, tq, WIDTH), lambda i, qi, ki: (i, qi, 0)),
                pl.BlockSpec((1, tq, N_HEADS), lambda i, qi, ki: (i, qi, 0)),
                pl.BlockSpec((1, tq, WIDTH), lambda i, qi, ki: (i, qi, 0))]
    args = [q, fq, sg]
    if n_past_blocks:
        def past_idx(i, qi, ki):
            return jnp.minimum(ki, n_past_blocks - 1)
        in_specs += [pl.BlockSpec((1, tk_past, WIDTH), lambda i, qi, ki: (i, past_idx(i, qi, ki), 0)),
                     pl.BlockSpec((1, tk_past, WIDTH), lambda i, qi, ki: (i, past_idx(i, qi, ki), 0)),
                     pl.BlockSpec((1, N_HEADS, tk_past), lambda i, qi, ki: (i, 0, past_idx(i, qi, ki)))]
        args += list(past)
    in_specs += [pl.BlockSpec((1, tq, WIDTH), lambda i, qi, ki: (i, new_idx(i, qi, ki), 0)),
                 pl.BlockSpec((1, tq, WIDTH), lambda i, qi, ki: (i, new_idx(i, qi, ki), 0)),
                 pl.BlockSpec((1, N_HEADS, tq), lambda i, qi, ki: (i, 0, new_idx(i, qi, ki)))]
    args += [k_new, v_new, fk_new_t]
    return pl.pallas_call(
        functools.partial(_fox_kernel, n_past_blocks=n_past_blocks, tq=tq),
        grid=(b, nq, nk),
        in_specs=in_specs,
        out_specs=pl.BlockSpec((1, tq, WIDTH), lambda i, qi, ki: (i, qi, 0)),
        out_shape=jax.ShapeDtypeStruct((b, t, WIDTH), F32),
        scratch_shapes=[pltpu.VMEM((N_HEADS, tq, 1), F32), pltpu.VMEM((N_HEADS, tq, 1), F32),
                        pltpu.VMEM((tq, WIDTH), F32)],
        compiler_params=_cparams(("parallel", "parallel", "arbitrary")),
        name="fox_attention",
    )(*args)


def _rwkv_kernel(p_ref, sh0_ref, s0_ref, mu_ref, w0_ref, wb_ref, a0_ref, ab_ref, gb_ref, kk_ref, ka_ref, rk_ref,
                 lnw_ref, lnb_ref, gm_ref, y_ref, st_ref, sht_ref, s_scr, prev_scr, o_scr):
    t = pl.program_id(1)
    nt = pl.num_programs(1)
    c = p_ref.shape[1]

    @pl.when(t == 0)
    def _():
        s_scr[...] = s0_ref[0]
        prev_scr[...] = sh0_ref[0]

    p = p_ref[0]
    row = lax.broadcasted_iota(I32, p.shape, 0)
    prev = jnp.where(row == 0, prev_scr[...], pltpu.roll(p, 1, 0))
    last = p[c - 1:c, :]
    prev_scr[...] = last
    sht_ref[0] = last

    pm = p + (prev - p) * mu_ref[...]
    r = pm[:, 0:WIDTH]
    k = pm[:, WIDTH:2 * WIDTH]
    v = pm[:, 2 * WIDTH:3 * WIDTH]
    o1 = 3 * WIDTH
    wd = pm[:, o1:o1 + DECAY_LORA]
    ad = pm[:, o1 + DECAY_LORA:o1 + DECAY_LORA + ICLR_LORA]
    gd = pm[:, o1 + DECAY_LORA + ICLR_LORA:RWKV_COLS]

    w = -_softplus(-(w0_ref[...] + _dot(jnp.tanh(wd), wb_ref[...]))) - 0.5
    lw = -jnp.exp(w)
    a = _sigmoid(a0_ref[...] + _dot(ad, ab_ref[...]))
    g = _dot(_sigmoid(gd), gb_ref[...])
    kk = k * kk_ref[...]
    kk = kk / jnp.maximum(jnp.sqrt(_gsum(kk * kk, gm_ref)), L2_EPS)
    kf = k * (1.0 + (a - 1.0) * ka_ref[...])

    ri = lax.broadcasted_iota(I32, (c, c), 0)
    ci = lax.broadcasted_iota(I32, (c, c), 1)
    lower = ri >= ci
    strict = ri > ci
    cum = _dot_exact_rhs(lower.astype(F32), lw)
    cum_last = cum[c - 1:c, :]
    r_t = r * jnp.exp(cum)
    a_t = -kk * jnp.exp(cum - lw)
    inv = jnp.exp(-cum)
    b_t = kk * a * inv
    k_t = kf * inv
    to_end = jnp.exp(cum_last - cum)
    b_e = kk * a * to_end
    k_e = kf * to_end
    g_end = jnp.exp(cum_last)
    eye = (ri == ci).astype(F32)

    for h in range(N_HEADS):
        sl = slice(h * HEAD_DIM, (h + 1) * HEAD_DIM)
        rh, ah, bh, kh, vh = r_t[:, sl], a_t[:, sl], b_t[:, sl], k_t[:, sl], v[:, sl]
        l_ab = jnp.where(strict, _dot3(ah, bh, NT), 0.0)
        l_ak = jnp.where(strict, _dot3(ah, kh, NT), 0.0)
        l_rb = jnp.where(lower, _dot3(rh, bh, NT), 0.0)
        l_rk = jnp.where(lower, _dot3(rh, kh, NT), 0.0)
        tinv = eye + l_ab
        lp = l_ab
        for _ in range(int(math.log2(c)) - 1):
            lp = _dot3(lp, lp)
            tinv = tinv + _dot3(tinv, lp)
        s0 = s_scr[h]
        u = _dot3(tinv, _dot3(ah, s0, NT) + _dot3(l_ak, vh))
        o_scr[:, sl] = _dot3(rh, s0, NT) + _dot3(l_rb, u) + _dot3(l_rk, vh)
        s_scr[h] = s0 * g_end[:, sl] + _dot3(u, b_e[:, sl], TN) + _dot3(vh, k_e[:, sl], TN)

    o = o_scr[...]
    inv_hd = 1.0 / HEAD_DIM
    dlt = o - _gsum(o, gm_ref) * inv_hd
    var = _gsum(dlt * dlt, gm_ref) * inv_hd
    on = dlt * lax.rsqrt(var + RWKV_GN_EPS) * lnw_ref[...] + lnb_ref[...]
    bonus = _gsum(r * kf * rk_ref[...], gm_ref) * v
    y_ref[0] = (on + bonus) * g

    @pl.when(t == nt - 1)
    def _():
        st_ref[0] = s_scr[...]


def _rwkv(p, shift0, s0, prm, gmat, chunk):
    b, t, _ = p.shape
    row = lambda n: _const_spec((1, n))
    return pl.pallas_call(
        _rwkv_kernel,
        grid=(b, t // chunk),
        in_specs=[pl.BlockSpec((1, chunk, RWKV_COLS), lambda i, j: (i, j, 0)),
                  pl.BlockSpec((1, 1, RWKV_COLS), lambda i, j: (i, 0, 0)),
                  pl.BlockSpec((1, N_HEADS, HEAD_DIM, HEAD_DIM), lambda i, j: (i, 0, 0, 0)),
                  row(RWKV_COLS), row(WIDTH), _const_spec((DECAY_LORA, WIDTH)), row(WIDTH),
                  _const_spec((ICLR_LORA, WIDTH)), _const_spec((GATE_LORA, WIDTH)),
                  row(WIDTH), row(WIDTH), row(WIDTH), row(WIDTH), row(WIDTH), _const_spec((WIDTH, WIDTH))],
        out_specs=[pl.BlockSpec((1, chunk, WIDTH), lambda i, j: (i, j, 0)),
                   pl.BlockSpec((1, N_HEADS, HEAD_DIM, HEAD_DIM), lambda i, j: (i, 0, 0, 0)),
                   pl.BlockSpec((1, 1, RWKV_COLS), lambda i, j: (i, 0, 0))],
        out_shape=[jax.ShapeDtypeStruct((b, t, WIDTH), F32),
                   jax.ShapeDtypeStruct((b, N_HEADS, HEAD_DIM, HEAD_DIM), F32),
                   jax.ShapeDtypeStruct((b, 1, RWKV_COLS), F32)],
        scratch_shapes=[pltpu.VMEM((N_HEADS, HEAD_DIM, HEAD_DIM), F32), pltpu.VMEM((1, RWKV_COLS), F32),
                        pltpu.VMEM((chunk, WIDTH), F32)],
        compiler_params=_cparams(("parallel", "arbitrary")),
        name="rwkv7_mix",
    )(p, shift0, s0, prm["mu"], prm["w0"], prm["wb"], prm["a0"], prm["ab"], prm["gb"], prm["kk"], prm["ka"],
      prm["rk"], prm["lnw"], prm["lnb"], gmat)


def _merge_kernel(x_ref, ya_ref, yb_ref, gate_ref, mod_ref, g2_ref, woa_ref, wob_ref, wo_ref, wrh_ref, wrl_ref,
                  x1_ref, h2_ref, lg_ref):
    bb, tt, d = x_ref.shape
    m = bb * tt
    gate = gate_ref[...].reshape(m, GATE_COLS)
    merged = (gate[:, 0:d] * _dot(ya_ref[...].reshape(m, WIDTH), woa_ref[...])
              + gate[:, d:2 * d] * _dot(yb_ref[...].reshape(m, WIDTH), wob_ref[...]))
    x1 = x_ref[...] + mod_ref[:, 2:3, :] * _dot(merged, wo_ref[...]).reshape(bb, tt, d)
    x1_ref[...] = x1
    ms = jnp.mean(x1 * x1, axis=-1, keepdims=True)
    h2 = x1 * lax.rsqrt(ms + RMS_EPS) * g2_ref[...]
    h2 = (h2 * (1.0 + mod_ref[:, 4:5, :]) + mod_ref[:, 3:4, :]).reshape(m, d)
    h2_ref[...] = h2
    hh, hl = _split2(h2)
    d2 = functools.partial(jnp.dot, preferred_element_type=F32)
    lg_ref[...] = d2(hh, wrh_ref[...]) + (d2(hh, wrl_ref[...]) + d2(hl, wrh_ref[...]))


def _merge(x, ya, yb, gate, mod, w):
    b, t, d = x.shape
    bb, tt = _token_blocks(b, t)
    nt = t // tt
    m = bb * tt

    def tok(cols):
        return pl.BlockSpec((bb, tt, cols), lambda i, j: (i, j, 0))

    def flat(cols):
        return pl.BlockSpec((m, cols), lambda i, j: (i * nt + j, 0))

    return pl.pallas_call(
        _merge_kernel,
        grid=(b // bb, nt),
        in_specs=[tok(d), tok(WIDTH), tok(WIDTH), tok(GATE_COLS),
                  pl.BlockSpec((bb, 6, d), lambda i, j: (i, 0, 0)),
                  _const_spec((1, d)), _const_spec((WIDTH, d)), _const_spec((WIDTH, d)), _const_spec((d, d)),
                  _const_spec((d, N_EXPERTS)), _const_spec((d, N_EXPERTS))],
        out_specs=[tok(d), flat(d), flat(N_EXPERTS)],
        out_shape=[jax.ShapeDtypeStruct((b, t, d), F32), jax.ShapeDtypeStruct((b * t, d), F32),
                   jax.ShapeDtypeStruct((b * t, N_EXPERTS), F32)],
        compiler_params=_cparams(("parallel", "parallel")),
        name="merge_norm2_router",
    )(x, ya, yb, gate, mod.reshape(b, 6, d), w["g2"], w["w_oa"], w["w_ob"], w["w_o"], w["wr_hi"], w["wr_lo"])


def _route_kernel(lg_ref, bias_ref, idx_ref, wt_ref, rank_ref, cnt_ref, carry):
    @pl.when(pl.program_id(0) == 0)
    def _():
        carry[...] = jnp.zeros(carry.shape, F32)

    tm = lg_ref.shape[0]
    scores = _sigmoid(lg_ref[...])
    sel = scores + bias_ref[...]
    lane = lax.broadcasted_iota(I32, (tm, N_EXPERTS), 1)
    grp = lane // EXPERTS_PER_GROUP
    neg_inf = -jnp.inf

    def first_argmax(vals):
        mx = jnp.max(vals, axis=-1, keepdims=True)
        return mx, jnp.min(jnp.where(vals == mx, lane, N_EXPERTS), axis=-1, keepdims=True)

    gs = []
    for g in range(N_GROUPS):
        mg = jnp.where(grp == g, sel, neg_inf)
        m1, i1 = first_argmax(mg)
        m2 = jnp.max(jnp.where(lane == i1, neg_inf, mg), axis=-1, keepdims=True)
        gs.append(m1 + m2)
    own = gs[0]
    for g in range(1, N_GROUPS):
        own = jnp.where(grp == g, gs[g], own)
    beaten = jnp.zeros((tm, N_EXPERTS), I32)
    for g in range(N_GROUPS):
        wins = jnp.where(gs[g] > own, 1, jnp.where(gs[g] == own, jnp.where(g < grp, 1, 0), 0))
        beaten = beaten + wins
    cur = jnp.where(beaten < TOPK_GROUPS, sel, neg_inf)

    idxs, ws = [], []
    picked = jnp.zeros((tm, N_EXPERTS), F32)
    for _ in range(TOP_K):
        _, ik = first_argmax(cur)
        hit = lane == ik
        idxs.append(ik)
        ws.append(jnp.sum(jnp.where(hit, scores, 0.0), axis=-1, keepdims=True))
        cur = jnp.where(hit, neg_inf, cur)
        picked = jnp.where(hit, 1.0, picked)
    wsum = ws[0]
    for k in range(1, TOP_K):
        wsum = wsum + ws[k]

    r = lax.broadcasted_iota(I32, (tm, tm), 0)
    c = lax.broadcasted_iota(I32, (tm, tm), 1)
    before = jnp.dot((r > c).astype(BF16), picked.astype(BF16), preferred_element_type=F32) + carry[...]
    carry[...] = carry[...] + jnp.sum(picked, axis=0, keepdims=True)
    cnt_ref[...] = carry[...]

    col = lax.broadcasted_iota(I32, (tm, TOP_K), 1)
    idx_o = jnp.zeros((tm, TOP_K), I32)
    wt_o = jnp.zeros((tm, TOP_K), F32)
    rank_o = jnp.zeros((tm, TOP_K), F32)
    for k in range(TOP_K):
        rk = jnp.sum(jnp.where(lane == idxs[k], before, 0.0), axis=-1, keepdims=True)
        idx_o = jnp.where(col == k, idxs[k], idx_o)
        wt_o = jnp.where(col == k, ws[k] / wsum * ROUTED_SCALE, wt_o)
        rank_o = jnp.where(col == k, rk, rank_o)
    idx_ref[...] = idx_o
    wt_ref[...] = wt_o
    rank_ref[...] = rank_o.astype(I32)


def _route(logits, bias, tm):
    n = logits.shape[0]
    tokk = pl.BlockSpec((tm, TOP_K), lambda i: (i, 0))
    return pl.pallas_call(
        _route_kernel,
        grid=(n // tm,),
        in_specs=[pl.BlockSpec((tm, N_EXPERTS), lambda i: (i, 0)), _const_spec((1, N_EXPERTS))],
        out_specs=[tokk, tokk, tokk, _const_spec((1, N_EXPERTS))],
        out_shape=[jax.ShapeDtypeStruct((n, TOP_K), I32), jax.ShapeDtypeStruct((n, TOP_K), F32),
                   jax.ShapeDtypeStruct((n, TOP_K), I32), jax.ShapeDtypeStruct((1, N_EXPERTS), F32)],
        scratch_shapes=[pltpu.VMEM((1, N_EXPERTS), F32)],
        compiler_params=_cparams(("arbitrary",)),
        name="route_topk",
    )(logits, bias)


def _plan_kernel(cnt_ref, start_ref, be_ref, valid_ref, nu_ref, *, blk):
    cnt = cnt_ref[...]
    padded = jnp.ceil(cnt * (1.0 / blk)) * blk
    e_r = lax.broadcasted_iota(I32, (N_EXPERTS, N_EXPERTS), 0)
    e_c = lax.broadcasted_iota(I32, (N_EXPERTS, N_EXPERTS), 1)
    incl = (e_r <= e_c).astype(BF16)
    ph, pm, plo = _split3(jnp.broadcast_to(padded, (8, N_EXPERTS)))
    d2 = functools.partial(jnp.dot, preferred_element_type=F32)
    pad_end = (d2(ph, incl) + (d2(pm, incl) + d2(plo, incl)))[0:1, :]
    pad_start = pad_end - padded
    start_ref[...] = pad_start.astype(I32)
    total = jnp.max(pad_end, axis=-1, keepdims=True)
    nu_ref[...] = jnp.broadcast_to(total * (1.0 / blk), (1, N_EXPERTS)).astype(I32)
    nb = be_ref.shape[0]
    first = (lax.broadcasted_iota(I32, (nb, N_EXPERTS), 0) * blk).astype(F32)
    lane = lax.broadcasted_iota(I32, (nb, N_EXPERTS), 1)
    inside = jnp.logical_and(pad_start <= first, first < pad_end)
    be_ref[...] = jnp.sum(jnp.where(inside, lane, 0), axis=-1, keepdims=True)
    rows = jnp.minimum(pad_start + cnt - first, float(blk))
    valid_ref[...] = jnp.sum(jnp.where(inside, rows, 0.0), axis=-1, keepdims=True).astype(I32)


def _plan(counts, n_blocks, blk):
    return pl.pallas_call(
        functools.partial(_plan_kernel, blk=blk),
        out_shape=[jax.ShapeDtypeStruct((1, N_EXPERTS), I32), jax.ShapeDtypeStruct((n_blocks, 1), I32),
                   jax.ShapeDtypeStruct((n_blocks, 1), I32), jax.ShapeDtypeStruct((1, N_EXPERTS), I32)],
        compiler_params=pltpu.CompilerParams(vmem_limit_bytes=VMEM_LIMIT),
        name="dispatch_plan",
    )(counts)


def _dest_kernel(idx_ref, rank_ref, start_ref, dest_ref):
    tm = idx_ref.shape[0]
    lane = lax.broadcasted_iota(I32, (tm, N_EXPERTS), 1)
    col = lax.broadcasted_iota(I32, (tm, TOP_K), 1)
    idx = idx_ref[...]
    base = jnp.zeros((tm, TOP_K), I32)
    for k in range(TOP_K):
        bk = jnp.sum(jnp.where(lane == idx[:, k:k + 1], start_ref[...], 0), axis=-1, keepdims=True)
        base = jnp.where(col == k, bk, base)
    dest_ref[...] = base + rank_ref[...]


def _dest(idx, rank, pad_start, tm):
    n = idx.shape[0]
    tokk = pl.BlockSpec((tm, TOP_K), lambda i: (i, 0))
    return pl.pallas_call(
        _dest_kernel,
        grid=(n // tm,),
        in_specs=[tokk, tokk, _const_spec((1, N_EXPERTS))],
        out_specs=tokk,
        out_shape=jax.ShapeDtypeStruct((n, TOP_K), I32),
        compiler_params=_cparams(("parallel",)),
        name="dispatch_dest",
    )(idx, rank, pad_start)


def _row_copy(src_ref, src_row, dst_ref, dst_row, sem):
    return pltpu.make_async_copy(src_ref.at[pl.ds(src_row, 1)], dst_ref.at[pl.ds(dst_row, 1)], sem)


def _dispatch_kernel(dest_ref, h2_ref, xs_ref, sem):
    tm = h2_ref.shape[0]

    def issue(r, carry):
        for k in range(TOP_K):
            _row_copy(h2_ref, r, xs_ref, dest_ref[k, r], sem).start()
        return carry

    lax.fori_loop(0, tm, issue, 0)

    def drain(r, carry):
        for k in range(TOP_K):
            _row_copy(h2_ref, 0, xs_ref, 0, sem).wait()
        return carry

    lax.fori_loop(0, tm, drain, 0)


def _dispatch(h2, dest_t, n_slots, tm):
    n, d = h2.shape
    return pl.pallas_call(
        _dispatch_kernel,
        grid=(n // tm,),
        in_specs=[pl.BlockSpec((TOP_K, tm), lambda i: (0, i), memory_space=pltpu.SMEM),
                  pl.BlockSpec((tm, d), lambda i: (i, 0))],
        out_specs=pl.BlockSpec(memory_space=pl.ANY),
        out_shape=jax.ShapeDtypeStruct((n_slots, d), F32),
        scratch_shapes=[pltpu.SemaphoreType.DMA(())],
        compiler_params=_cparams(("arbitrary",)),
        name="moe_dispatch",
    )(dest_t, h2)


def _expert_kernel(be_ref, valid_ref, nu_ref, x_ref, wg_ref, wu_ref, wd_ref, y_ref):
    nv = valid_ref[pl.program_id(0)]

    @pl.when(nv > 0)
    def _():
        blk = x_ref.shape[0]
        rows = lax.broadcasted_iota(I32, (blk, 1), 0)
        x = jnp.where(rows < nv, x_ref[...], 0.0).astype(BF16)
        hg = jnp.dot(x, wg_ref[0].astype(BF16), preferred_element_type=F32)
        hu = jnp.dot(x, wu_ref[0].astype(BF16), preferred_element_type=F32)
        y_ref[...] = _dot(_silu(hg) * hu, wd_ref[0])


def _experts(xs, block_e, valid, n_used, w_eg, w_eu, w_ed, blk):
    n_slots, d = xs.shape
    n_blocks = n_slots // blk

    def row_blk(i, be, valid, nu):
        return (jnp.minimum(i, nu[0] - 1), 0)

    def w_blk(i, be, valid, nu):
        return (be[i], 0, 0)

    return pl.pallas_call(
        _expert_kernel,
        grid_spec=pltpu.PrefetchScalarGridSpec(
            num_scalar_prefetch=3,
            grid=(n_blocks,),
            in_specs=[pl.BlockSpec((blk, d), row_blk),
                      pl.BlockSpec((1, d, D_EXPERT), w_blk), pl.BlockSpec((1, d, D_EXPERT), w_blk),
                      pl.BlockSpec((1, D_EXPERT, d), w_blk)],
            out_specs=pl.BlockSpec((blk, d), row_blk)),
        out_shape=jax.ShapeDtypeStruct((n_slots, d), F32),
        compiler_params=_cparams(("arbitrary",)),
        name="moe_experts",
    )(block_e, valid, n_used, xs, w_eg, w_eu, w_ed)


def _final_kernel(dest_ref, x1_ref, h2_ref, wt_ref, mod_ref, wsg_ref, wsu_ref, wsd_ref, ys_ref, o_ref, ybuf, sem):
    bb, tt, d = x1_ref.shape
    m = bb * tt

    def issue(r, carry):
        for k in range(TOP_K):
            _row_copy(ys_ref, dest_ref[k, r], ybuf.at[k], r, sem).start()
        return carry

    lax.fori_loop(0, m, issue, 0)

    hb = h2_ref[...].astype(BF16)
    hg = jnp.dot(hb, wsg_ref[...], preferred_element_type=F32)
    hu = jnp.dot(hb, wsu_ref[...], preferred_element_type=F32)
    ffn = _dot(_silu(hg) * hu, wsd_ref[...])

    def drain(r, carry):
        for k in range(TOP_K):
            _row_copy(ys_ref, 0, ybuf.at[k], 0, sem).wait()
        return carry

    lax.fori_loop(0, m, drain, 0)

    wt = wt_ref[...]
    for k in range(TOP_K):
        ffn = ffn + wt[:, k:k + 1] * ybuf[k]
    o_ref[...] = x1_ref[...] + mod_ref[:, 5:6, :] * ffn.reshape(bb, tt, d)


def _final(x1, h2_all, wts_all, dest_t, ys, mod, w, row_offset):
    b, t, d = x1.shape
    bb, tt = _token_blocks(b, t)
    nt = t // tt
    m = bb * tt
    off = row_offset // m

    def flat_idx(i, j):
        return off + i * nt + j

    return pl.pallas_call(
        _final_kernel,
        grid=(b // bb, nt),
        in_specs=[pl.BlockSpec((TOP_K, m), lambda i, j: (0, flat_idx(i, j)), memory_space=pltpu.SMEM),
                  pl.BlockSpec((bb, tt, d), lambda i, j: (i, j, 0)),
                  pl.BlockSpec((m, d), lambda i, j: (flat_idx(i, j), 0)),
                  pl.BlockSpec((m, TOP_K), lambda i, j: (flat_idx(i, j), 0)),
                  pl.BlockSpec((bb, 6, d), lambda i, j: (i, 0, 0)),
                  _const_spec((d, D_EXPERT)), _const_spec((d, D_EXPERT)), _const_spec((D_EXPERT, d)),
                  pl.BlockSpec(memory_space=pl.ANY)],
        out_specs=pl.BlockSpec((bb, tt, d), lambda i, j: (i, j, 0)),
        out_shape=jax.ShapeDtypeStruct((b, t, d), F32),
        scratch_shapes=[pltpu.VMEM((TOP_K, m, d), F32), pltpu.SemaphoreType.DMA(())],
        compiler_params=_cparams(("arbitrary", "arbitrary")),
        name="moe_combine_final",
    )(dest_t, x1, h2_all, wts_all, mod.reshape(b, 6, d), w["w_sg"], w["w_su"], w["w_sd"], ys)


def _moe_routed(h2_all, logits_all, w, blk=256, tm=256):
    n = h2_all.shape[0]
    n_blocks = (n * TOP_K + N_EXPERTS * (blk - 1)) // blk + 1
    n_blocks = (n_blocks + 7) // 8 * 8
    idx, wts, rank, counts = _route(logits_all, w["router_bias"], tm)
    pad_start, block_e, valid, n_used = _plan(counts, n_blocks, blk)
    block_e = block_e.reshape(n_blocks)
    valid = valid.reshape(n_blocks)
    n_used = n_used[0, 0:1]
    dest_t = jnp.transpose(_dest(idx, rank, pad_start, tm))
    xs = _dispatch(h2_all, dest_t, n_blocks * blk, tm)
    ys = _experts(xs, block_e, valid, n_used, w["w_eg"], w["w_eu"], w["w_ed"], blk)
    return ys, dest_t, wts


def _prep(raw):
    p = {k: v[0] for k, v in raw.items()}
    w_in = p["w_in"]
    o_fox = RWKV_COLS
    o_fl = o_fox + FOX_MAIN_COLS
    o_gate = o_fl + N_HEADS
    row = lambda a: a.reshape(1, -1)
    return dict(
        w_ada=p["w_ada"], b_ada=p["b_ada"],
        g1=row(p["norm1_g"]), g2=row(p["norm2_g"]),
        wr=w_in[:, :o_fox].astype(BF16),
        wf=w_in[:, o_fox:o_fl].astype(BF16),
        wfl=jnp.pad(w_in[:, o_fl:o_gate], ((0, 0), (0, LANES - N_HEADS))).astype(BF16),
        wg=w_in[:, o_gate:].astype(BF16),
        qn=row(jnp.tile(p["fox_q_norm"], N_HEADS)), kn=row(jnp.tile(p["fox_k_norm"], N_HEADS)),
        fb=row(p["fox_f_bias"]),
        gmat=_group_ones(),
        rwkv=dict(mu=row(p["rwkv_mu"]), w0=row(p["rwkv_w0"]), wb=p["rwkv_w_lora_b"], a0=row(p["rwkv_a0"]),
                  ab=p["rwkv_a_lora_b"], gb=p["rwkv_g_lora_b"], kk=row(p["rwkv_k_k"]), ka=row(p["rwkv_k_a"]),
                  rk=row(p["rwkv_r_k"]), lnw=row(p["rwkv_ln_w"]), lnb=row(p["rwkv_ln_b"])),
        w_oa=p["w_out_rwkv"].astype(BF16), w_ob=p["w_out_fox"].astype(BF16), w_o=p["w_out"].astype(BF16),
        wr_hi=p["w_router"].astype(BF16),
        wr_lo=(p["w_router"] - p["w_router"].astype(BF16).astype(F32)).astype(BF16),
        router_bias=row(p["router_bias"]),
        w_eg=p["w_exp_gate"], w_eu=p["w_exp_up"], w_ed=p["w_exp_down"],
        w_sg=p["w_sh_gate"].astype(BF16), w_su=p["w_sh_up"].astype(BF16), w_sd=p["w_sh_down"].astype(BF16),
    )


def _token_blocks(b, t):
    if t >= 256:
        return 1, 256
    bb = max(1, min(b, 256 // t))
    while b % bb:
        bb -= 1
    return bb, t


def _mix_path(x, mod, shift0, wkv0, past_k, past_v, past_logf, w):
    b, t, d = x.shape
    bb, tt = _token_blocks(b, t)
    pr, q, k, v, sg, logf, gate = _inproj(x, mod.reshape(b, 6, d), w["g1"], w["wr"], w["wf"], w["wfl"], w["wg"],
                                          w["qn"], w["kn"], w["fb"], w["gmat"], bb, tt)
    n_past = past_k.shape[1]
    ct = min(t, 256)
    if n_past:
        f_past = _cumsum_time(past_logf, jnp.zeros((b, 1, N_HEADS), F32), min(n_past, 256))
        init = f_past[:, n_past - 1:n_past, :]
        past = (past_k, past_v, jnp.swapaxes(f_past, 1, 2))
    else:
        init = jnp.zeros((b, 1, N_HEADS), F32)
        past = None
    f_new = _cumsum_time(logf, init, ct)
    y_fox = _fox_attention(q, f_new, sg, k, v, jnp.swapaxes(f_new, 1, 2), past=past, tq=min(t, 512),
                           tk_past=min(max(n_past, 1), 512))
    y_rwkv, wkv_new, shift_new = _rwkv(pr, shift0.reshape(b, 1, RWKV_COLS), wkv0, w["rwkv"], w["gmat"],
                                       min(t, 64))
    return y_rwkv, y_fox, gate, wkv_new, shift_new, k, v, logf


def _layer(paths, w):
    n_b = [p[0].shape[0] for p in paths]
    mod_all = _ada(jnp.concatenate([p[1] for p in paths], axis=0), w["w_ada"], w["b_ada"])
    mods, o = [], 0
    for nb in n_b:
        mods.append(mod_all[o:o + nb])
        o += nb
    mixed, x1s, h2s, lgs = [], [], [], []
    for (x, _, shift0, wkv0, pk, pv, plf), mod in zip(paths, mods):
        ya, yb, gate, wkv_new, shift_new, k, v, logf = _mix_path(x, mod, shift0, wkv0, pk, pv, plf, w)
        x1, h2, lg = _merge(x, ya, yb, gate, mod, w)
        mixed.append((wkv_new, shift_new, k, v, logf))
        x1s.append(x1)
        h2s.append(h2)
        lgs.append(lg)
    h2_all = jnp.concatenate(h2s, axis=0)
    ys, dest_t, wts = _moe_routed(h2_all, jnp.concatenate(lgs, axis=0), w)
    outs, row = [], 0
    for x1, mod, st in zip(x1s, mods, mixed):
        y = _final(x1, h2_all, wts, dest_t, ys, mod, w, row)
        row += x1.shape[0] * x1.shape[1]
        outs.append((y,) + st)
    return outs


def kernel(x_prompt, x_sample, c_prompt, c_sample, state_rwkv_wkv, state_rwkv_shift, cache_fox_k, cache_fox_v,
           cache_fox_logf, w_ada, b_ada, norm1_g, norm2_g, w_in, rwkv_mu, rwkv_w0, rwkv_w_lora_b, rwkv_a0,
           rwkv_a_lora_b, rwkv_g_lora_b, rwkv_k_k, rwkv_k_a, rwkv_r_k, rwkv_ln_w, rwkv_ln_b, fox_q_norm,
           fox_k_norm, fox_f_bias, w_out_rwkv, w_out_fox, w_out, w_router, router_bias, w_exp_gate, w_exp_up,
           w_exp_down, w_sh_gate, w_sh_up, w_sh_down):
    raw = dict(w_ada=w_ada, b_ada=b_ada, norm1_g=norm1_g, norm2_g=norm2_g, w_in=w_in, rwkv_mu=rwkv_mu,
               rwkv_w0=rwkv_w0, rwkv_w_lora_b=rwkv_w_lora_b, rwkv_a0=rwkv_a0, rwkv_a_lora_b=rwkv_a_lora_b,
               rwkv_g_lora_b=rwkv_g_lora_b, rwkv_k_k=rwkv_k_k, rwkv_k_a=rwkv_k_a, rwkv_r_k=rwkv_r_k,
               rwkv_ln_w=rwkv_ln_w, rwkv_ln_b=rwkv_ln_b, fox_q_norm=fox_q_norm, fox_k_norm=fox_k_norm,
               fox_f_bias=fox_f_bias, w_out_rwkv=w_out_rwkv, w_out_fox=w_out_fox, w_out=w_out,
               w_router=w_router, router_bias=router_bias, w_exp_gate=w_exp_gate, w_exp_up=w_exp_up,
               w_exp_down=w_exp_down, w_sh_gate=w_sh_gate, w_sh_up=w_sh_up, w_sh_down=w_sh_down)
    assert w_in.shape[0] == 1, "single-layer stack"
    w = _prep(raw)
    bp, tp, _ = x_prompt.shape
    bs, ts, _ = x_sample.shape
    n_past = cache_fox_k.shape[2]
    prompt = (x_prompt, c_prompt, jnp.zeros((bp, RWKV_COLS), F32),
              jnp.zeros((bp, N_HEADS, HEAD_DIM, HEAD_DIM), F32),
              jnp.zeros((bp, 0, WIDTH), F32), jnp.zeros((bp, 0, WIDTH), F32), jnp.zeros((bp, 0, N_HEADS), F32))
    sample = (x_sample, c_sample, state_rwkv_shift[0], state_rwkv_wkv[0],
              cache_fox_k[0].reshape(bs, n_past, WIDTH), cache_fox_v[0].reshape(bs, n_past, WIDTH),
              cache_fox_logf[0])
    (yp, wkv_p, sh_p, k_p, v_p, lf_p), (ysm, wkv_s, sh_s, k_s, v_s, lf_s) = _layer([prompt, sample], w)

    def heads(a):
        return a.reshape((1,) + a.shape[:2] + (N_HEADS, HEAD_DIM))

    return (yp, ysm,
            wkv_p[None], sh_p.reshape(1, bp, RWKV_COLS), heads(k_p), heads(v_p), lf_p[None],
            wkv_s[None], sh_s.reshape(1, bs, RWKV_COLS), heads(k_s), heads(v_s), lf_s[None])
```

```python
import functools
import math

import jax
import jax.numpy as jnp
from jax import lax
from jax.experimental import pallas as pl
from jax.experimental.pallas import tpu as pltpu

F32 = jnp.float32
BF16 = jnp.bfloat16
I32 = jnp.int32

D_MODEL = 1024
N_HEADS = 8
HEAD_DIM = 64
WIDTH = N_HEADS * HEAD_DIM
HEADS_PER_GROUP = 4
DECAY_LORA = 64
ICLR_LORA = 64
GATE_LORA = 128
RWKV_COLS = 3 * WIDTH + DECAY_LORA + ICLR_LORA + GATE_LORA
FOX_MAIN_COLS = 4 * WIDTH
GATE_COLS = 2 * D_MODEL
RWKV_GN_EPS = HEAD_DIM * 1e-5
L2_EPS = 1e-12
RMS_EPS = 1e-6
N_EXPERTS = 256
TOP_K = 8
N_GROUPS = 8
TOPK_GROUPS = 4
EXPERTS_PER_GROUP = N_EXPERTS // N_GROUPS
D_EXPERT = 256
ROUTED_SCALE = 2.5

LANES = 128
VMEM_LIMIT = 56 * 1024 * 1024
NEG_BIG = -1e30

NN = (((1,), (0,)), ((), ()))
NT = (((1,), (1,)), ((), ()))
TN = (((0,), (0,)), ((), ()))


def _cparams(sem):
    return pltpu.CompilerParams(dimension_semantics=sem, vmem_limit_bytes=VMEM_LIMIT)


def _dot(a, b, dims=NN):
    return lax.dot_general(a.astype(BF16), b.astype(BF16), dims, preferred_element_type=F32)


def _split2(a):
    hi = a.astype(BF16)
    lo = (a - hi.astype(F32)).astype(BF16)
    return hi, lo


def _split3(a):
    hi = a.astype(BF16)
    r1 = a - hi.astype(F32)
    mid = r1.astype(BF16)
    lo = (r1 - mid.astype(F32)).astype(BF16)
    return hi, mid, lo


def _dot3(a, b, dims=NN):
    ah, al = _split2(a)
    bh, bl = _split2(b)
    d = functools.partial(lax.dot_general, dimension_numbers=dims, preferred_element_type=F32)
    return d(ah, bh) + (d(ah, bl) + d(al, bh))


def _mm3(a, b, dims):
    d = functools.partial(lax.dot_general, dimension_numbers=dims, preferred_element_type=F32)
    return d(a[0], b[0]) + (d(a[0], b[1]) + d(a[1], b[0]))


def _bd_parts(x, mask):
    out = []
    for part in _split2(x):
        tiled = jnp.concatenate([part] * HEADS_PER_GROUP, axis=0)
        out.append(jnp.where(mask, tiled, jnp.zeros_like(tiled)))
    return tuple(out)


def _dot_exact_rhs(a_exact, b, dims=NN):
    ab = a_exact.astype(BF16)
    bh, bm, bl = _split3(b)
    d = functools.partial(lax.dot_general, dimension_numbers=dims, preferred_element_type=F32)
    return d(ab, bh) + (d(ab, bm) + d(ab, bl))


def _gsum(x, g_ref):
    hi, mid, lo = _split3(x)
    g = g_ref[...]
    d = functools.partial(jnp.dot, preferred_element_type=F32)
    return d(hi, g) + (d(mid, g) + d(lo, g))


def _sigmoid(x):
    return 1.0 / (1.0 + jnp.exp(-x))


def _softplus(x):
    return jnp.maximum(x, 0.0) + jnp.log1p(jnp.exp(-jnp.abs(x)))


def _silu(x):
    return x * _sigmoid(x)


def _group_ones():
    h = jnp.arange(WIDTH, dtype=I32) // HEAD_DIM
    return (h[:, None] == h[None, :]).astype(BF16)


def _ada_kernel(c_ref, w_ref, b_ref, o_ref):
    o_ref[...] = _dot(_silu(c_ref[...]), w_ref[...]) + b_ref[...]


def _ada(c, w_ada, b_ada):
    nb = c.shape[0]
    n_out = w_ada.shape[1]
    blk = D_MODEL
    return pl.pallas_call(
        _ada_kernel,
        grid=(n_out // blk,),
        in_specs=[pl.BlockSpec((nb, D_MODEL), lambda j: (0, 0)),
                  pl.BlockSpec((D_MODEL, blk), lambda j: (0, j)),
                  pl.BlockSpec((1, blk), lambda j: (0, j))],
        out_specs=pl.BlockSpec((nb, blk), lambda j: (0, j)),
        out_shape=jax.ShapeDtypeStruct((nb, n_out), F32),
        compiler_params=_cparams(("parallel",)),
        name="ada_mod",
    )(c, w_ada, b_ada.reshape(1, n_out))


def _inproj_kernel(x_ref, mod_ref, g1_ref, wr_ref, wf_ref, wfl_ref, wg_ref, qn_ref, kn_ref, fb_ref, gm_ref,
                   pr_ref, q_ref, k_ref, v_ref, sg_ref, lf_ref, gate_ref):
    bb, tt, d = x_ref.shape
    m = bb * tt
    x = x_ref[...]
    ms = jnp.mean(x * x, axis=-1, keepdims=True)
    h = x * lax.rsqrt(ms + RMS_EPS) * g1_ref[...]
    h = h * (1.0 + mod_ref[:, 1:2, :]) + mod_ref[:, 0:1, :]
    hb = h.reshape(m, d).astype(BF16)

    pr_ref[...] = jnp.dot(hb, wr_ref[...], preferred_element_type=F32).reshape(bb, tt, RWKV_COLS)

    f = jnp.dot(hb, wf_ref[...], preferred_element_type=F32)
    q = f[:, 0:WIDTH]
    k = f[:, WIDTH:2 * WIDTH]
    v = f[:, 2 * WIDTH:3 * WIDTH]
    og = f[:, 3 * WIDTH:4 * WIDTH]
    inv_hd = 1.0 / HEAD_DIM
    q = q * lax.rsqrt(_gsum(q * q, gm_ref) * inv_hd + RMS_EPS) * qn_ref[...]
    k = k * lax.rsqrt(_gsum(k * k, gm_ref) * inv_hd + RMS_EPS) * kn_ref[...]
    q_ref[...] = (q * (HEAD_DIM ** -0.5)).astype(BF16).reshape(bb, tt, WIDTH)
    k_ref[...] = k.reshape(bb, tt, WIDTH)
    v_ref[...] = v.reshape(bb, tt, WIDTH)
    sg_ref[...] = _sigmoid(og).reshape(bb, tt, WIDTH)

    fl = jnp.dot(hb, wfl_ref[...], preferred_element_type=F32)[:, 0:N_HEADS] + fb_ref[...]
    lf_ref[...] = (-_softplus(-fl)).reshape(bb, tt, N_HEADS)

    gate_ref[...] = _sigmoid(jnp.dot(hb, wg_ref[...], preferred_element_type=F32)).reshape(bb, tt, GATE_COLS)


def _const_spec(shape):
    nd = len(shape)
    return pl.BlockSpec(shape, lambda *_: (0,) * nd)


def _inproj(x, mod, g1, wr, wf, wfl, wg, qn, kn, fb, gmat, bb, tt):
    b, t, d = x.shape
    grid = (b // bb, t // tt)

    def tok(cols):
        return pl.BlockSpec((bb, tt, cols), lambda i, j: (i, j, 0))

    out_cols = [(RWKV_COLS, F32), (WIDTH, BF16), (WIDTH, F32), (WIDTH, F32), (WIDTH, F32), (N_HEADS, F32),
                (GATE_COLS, F32)]
    return pl.pallas_call(
        _inproj_kernel,
        grid=grid,
        in_specs=[tok(d),
                  pl.BlockSpec((bb, 6, d), lambda i, j: (i, 0, 0)),
                  _const_spec((1, d)),
                  _const_spec(wr.shape), _const_spec(wf.shape), _const_spec(wfl.shape), _const_spec(wg.shape),
                  _const_spec((1, WIDTH)), _const_spec((1, WIDTH)), _const_spec((1, N_HEADS)),
                  _const_spec((WIDTH, WIDTH))],
        out_specs=[tok(c) for c, _ in out_cols],
        out_shape=[jax.ShapeDtypeStruct((b, t, c), dt) for c, dt in out_cols],
        compiler_params=_cparams(("parallel", "parallel")),
        name="norm1_inproj",
    )(x, mod, g1, wr, wf, wfl, wg, qn, kn, fb, gmat)


def _cumsum_kernel(x_ref, init_ref, o_ref, carry):
    @pl.when(pl.program_id(1) == 0)
    def _():
        carry[...] = init_ref[0]

    x = x_ref[0]
    tt = x.shape[0]
    r = lax.broadcasted_iota(I32, (tt, tt), 0)
    c = lax.broadcasted_iota(I32, (tt, tt), 1)
    tri = (r >= c).astype(F32)
    cs = _dot_exact_rhs(tri, x) + carry[...]
    o_ref[0] = cs
    carry[...] = cs[tt - 1:tt, :]


def _cumsum_time(x, init, tt):
    b, t, h = x.shape
    return pl.pallas_call(
        _cumsum_kernel,
        grid=(b, t // tt),
        in_specs=[pl.BlockSpec((1, tt, h), lambda i, j: (i, j, 0)),
                  pl.BlockSpec((1, 1, h), lambda i, j: (i, 0, 0))],
        out_specs=pl.BlockSpec((1, tt, h), lambda i, j: (i, j, 0)),
        out_shape=jax.ShapeDtypeStruct((b, t, h), F32),
        scratch_shapes=[pltpu.VMEM((1, h), F32)],
        compiler_params=_cparams(("parallel", "arbitrary")),
        name="logf_cumsum",
    )(x, init)


def _fox_kernel(*refs, n_past_blocks, tq):
    if n_past_blocks:
        (q_ref, fq_ref, sg_ref, kp_ref, vp_ref, fkp_ref, kn_ref, vn_ref, fkn_ref,
         o_ref, m_scr, l_scr, acc_scr) = refs
    else:
        q_ref, fq_ref, sg_ref, kn_ref, vn_ref, fkn_ref, o_ref, m_scr, l_scr, acc_scr = refs
    qi = pl.program_id(1)
    ki = pl.program_id(2)
    nk = pl.num_programs(2)

    @pl.when(ki == 0)
    def _():
        m_scr[...] = jnp.full(m_scr.shape, NEG_BIG, F32)
        l_scr[...] = jnp.zeros(l_scr.shape, F32)
        acc_scr[...] = jnp.zeros(acc_scr.shape, F32)

    lane_a = lax.broadcasted_iota(I32, (tq, LANES), 1) < HEAD_DIM

    def step(k_ref, v_ref, fk_ref, diag):
        tk = k_ref.shape[1]
        if diag:
            rq = lax.broadcasted_iota(I32, (tq, tk), 0)
            ck = lax.broadcasted_iota(I32, (tq, tk), 1)
            visible = ck <= rq
        fq_all = fq_ref[0]
        pairs = range(N_HEADS // 2)
        cols = [slice(j * LANES, (j + 1) * LANES) for j in pairs]
        scores = []
        for j in pairs:
            qj = q_ref[0, :, cols[j]]
            kb = k_ref[0, :, cols[j]].astype(BF16)
            for hh in range(2):
                h = 2 * j + hh
                qm = jnp.where(lane_a if hh == 0 else jnp.logical_not(lane_a), qj, jnp.zeros_like(qj))
                s = lax.dot_general(qm, kb, NT, preferred_element_type=F32)
                s = s + fq_all[:, h:h + 1] - fk_ref[0, h:h + 1, :]
                if diag:
                    s = jnp.where(visible, s, NEG_BIG)
                scores.append(s)
        alphas, probs = [], []
        for h in range(N_HEADS):
            m_old = m_scr[h]
            m_new = jnp.maximum(m_old, jnp.max(scores[h], axis=-1, keepdims=True))
            alpha = jnp.exp(m_old - m_new)
            p = jnp.exp(scores[h] - m_new)
            l_scr[h] = alpha * l_scr[h] + jnp.sum(p, axis=-1, keepdims=True)
            m_scr[h] = m_new
            alphas.append(alpha)
            probs.append(p.astype(BF16))
        for j in pairs:
            vb = v_ref[0, :, cols[j]].astype(BF16)
            pv0 = jnp.dot(probs[2 * j], vb, preferred_element_type=F32)
            pv1 = jnp.dot(probs[2 * j + 1], vb, preferred_element_type=F32)
            acc_scr[:, cols[j]] = (acc_scr[:, cols[j]] * jnp.where(lane_a, alphas[2 * j], alphas[2 * j + 1])
                                   + jnp.where(lane_a, pv0, pv1))

    if n_past_blocks:
        @pl.when(ki < n_past_blocks)
        def _():
            step(kp_ref, vp_ref, fkp_ref, False)

    kn = ki - n_past_blocks

    @pl.when(jnp.logical_and(kn >= 0, kn < qi))
    def _():
        step(kn_ref, vn_ref, fkn_ref, False)

    @pl.when(kn == qi)
    def _():
        step(kn_ref, vn_ref, fkn_ref, True)

    @pl.when(ki == nk - 1)
    def _():
        for j in range(N_HEADS // 2):
            cols = slice(j * LANES, (j + 1) * LANES)
            l = jnp.where(lane_a, l_scr[2 * j], l_scr[2 * j + 1])
            o_ref[0, :, cols] = acc_scr[:, cols] / l * sg_ref[0, :, cols]


def _fox_attention(q, fq, sg, k_new, v_new, fk_new_t, past=None, tq=512, tk_past=512):
    b, t, _ = q.shape
    nq = t // tq
    n_past_blocks = 0 if past is None else past[0].shape[1] // tk_past
    nk = n_past_blocks + nq

    def new_idx(i, qi, ki):
        return jnp.clip(ki - n_past_blocks, 0, qi)

    in_specs = [pl.BlockSpec((1, tq, WIDTH), lambda i, qi, ki: (i, qi, 0)),
                pl.BlockSpec((1, tq, N_HEADS), lambda i, qi, ki: (i, qi, 0)),
                pl.BlockSpec((1, tq, WIDTH), lambda i, qi, ki: (i, qi, 0))]
    args = [q, fq, sg]
    if n_past_blocks:
        def past_idx(i, qi, ki):
            return jnp.minimum(ki, n_past_blocks - 1)
        in_specs += [pl.BlockSpec((1, tk_past, WIDTH), lambda i, qi, ki: (i, past_idx(i, qi, ki), 0)),
                     pl.BlockSpec((1, tk_past, WIDTH), lambda i, qi, ki: (i, past_idx(i, qi, ki), 0)),
                     pl.BlockSpec((1, N_HEADS, tk_past), lambda i, qi, ki: (i, 0, past_idx(i, qi, ki)))]
        args += list(past)
    in_specs += [pl.BlockSpec((1, tq, WIDTH), lambda i, qi, ki: (i, new_idx(i, qi, ki), 0)),
                 pl.BlockSpec((1, tq, WIDTH), lambda i, qi, ki: (i, new_idx(i, qi, ki), 0)),
                 pl.BlockSpec((1, N_HEADS, tq), lambda i, qi, ki: (i, 0, new_idx(i, qi, ki)))]
    args += [k_new, v_new, fk_new_t]
    return pl.pallas_call(
        functools.partial(_fox_kernel, n_past_blocks=n_past_blocks, tq=tq),
        grid=(b, nq, nk),
        in_specs=in_specs,
        out_specs=pl.BlockSpec((1, tq, WIDTH), lambda i, qi, ki: (i, qi, 0)),
        out_shape=jax.ShapeDtypeStruct((b, t, WIDTH), F32),
        scratch_shapes=[pltpu.VMEM((N_HEADS, tq, 1), F32), pltpu.VMEM((N_HEADS, tq, 1), F32),
                        pltpu.VMEM((tq, WIDTH), F32)],
        compiler_params=_cparams(("parallel", "parallel", "arbitrary")),
        name="fox_attention",
    )(*args)


def _rwkv_kernel(p_ref, sh0_ref, s0_ref, mu_ref, w0_ref, wb_ref, a0_ref, ab_ref, gb_ref, kk_ref, ka_ref, rk_ref,
                 lnw_ref, lnb_ref, gm_ref, y_ref, st_ref, sht_ref, z_scr, prev_scr):
    t = pl.program_id(1)
    nt = pl.num_programs(1)
    c = p_ref.shape[1]

    def head_block(h):
        lo = (h % HEADS_PER_GROUP) * HEAD_DIM
        return h // HEADS_PER_GROUP, slice(lo, lo + HEAD_DIM)

    @pl.when(t == 0)
    def _():
        z_scr[...] = jnp.zeros(z_scr.shape, F32)
        for h in range(N_HEADS):
            i, blk = head_block(h)
            z_scr[i, blk, blk] = s0_ref[0, h]
        prev_scr[...] = sh0_ref[0]

    p = p_ref[0]
    row = lax.broadcasted_iota(I32, p.shape, 0)
    prev = jnp.where(row == 0, prev_scr[...], pltpu.roll(p, 1, 0))
    last = p[c - 1:c, :]
    prev_scr[...] = last
    sht_ref[0] = last

    pm = p + (prev - p) * mu_ref[...]
    r = pm[:, 0:WIDTH]
    k = pm[:, WIDTH:2 * WIDTH]
    v = pm[:, 2 * WIDTH:3 * WIDTH]
    o1 = 3 * WIDTH
    wd = pm[:, o1:o1 + DECAY_LORA]
    ad = pm[:, o1 + DECAY_LORA:o1 + DECAY_LORA + ICLR_LORA]
    gd = pm[:, o1 + DECAY_LORA + ICLR_LORA:RWKV_COLS]

    w = -_softplus(-(w0_ref[...] + _dot(jnp.tanh(wd), wb_ref[...]))) - 0.5
    lw = -jnp.exp(w)
    a = _sigmoid(a0_ref[...] + _dot(ad, ab_ref[...]))
    g = _dot(_sigmoid(gd), gb_ref[...])
    kk = k * kk_ref[...]
    kk = kk / jnp.maximum(jnp.sqrt(_gsum(kk * kk, gm_ref)), L2_EPS)
    kf = k * (1.0 + (a - 1.0) * ka_ref[...])

    ri = lax.broadcasted_iota(I32, (c, c), 0)
    ci = lax.broadcasted_iota(I32, (c, c), 1)
    cum = _dot_exact_rhs((ri >= ci).astype(F32), lw)
    cum_last = cum[c - 1:c, :]
    r_t = r * jnp.exp(cum)
    a_t = -kk * jnp.exp(cum - lw)
    inv = jnp.exp(-cum)
    b_t = kk * a * inv
    k_t = kf * inv
    to_end = jnp.exp(cum_last - cum)
    b_e = kk * a * to_end
    k_e = kf * to_end
    g_end = jnp.exp(cum_last)

    hg = HEADS_PER_GROUP
    gw = hg * HEAD_DIM
    log_c = int(math.log2(c))
    t_idx = lax.broadcasted_iota(I32, (c, hg * c), 0)
    s_idx = lax.broadcasted_iota(I32, (c, hg * c), 1) & (c - 1)
    strict = s_idx < t_idx
    lower = s_idx <= t_idx
    eye = (s_idx == t_idx).astype(F32)
    rb = lax.broadcasted_iota(I32, (hg * c, gw), 0) >> log_c
    mask_kv = rb == (lax.broadcasted_iota(I32, (hg * c, gw), 1) >> int(math.log2(HEAD_DIM)))
    rs = lax.broadcasted_iota(I32, (hg * c, hg * c), 0) >> log_c
    mask_ss = rs == (lax.broadcasted_iota(I32, (hg * c, hg * c), 1) >> log_c)
    groups = [slice(i * gw, (i + 1) * gw) for i in range(N_HEADS // hg)]
    ng = len(groups)
    cat = functools.partial(jnp.concatenate, axis=0)

    ar = [_split2(cat([a_t[:, s], r_t[:, s]])) for s in groups]
    ab = [_mm3(ar[i], _bd_parts(b_t[:, groups[i]], mask_kv), NT) for i in range(ng)]
    ak = [_mm3(ar[i], _bd_parts(k_t[:, groups[i]], mask_kv), NT) for i in range(ng)]
    l_ab = [jnp.where(strict, m[:c], 0.0) for m in ab]
    l_rb = [jnp.where(lower, m[c:], 0.0) for m in ab]
    l_ak = [jnp.where(strict, m[:c], 0.0) for m in ak]
    l_rk = [jnp.where(lower, m[c:], 0.0) for m in ak]
    tinv = [eye + m for m in l_ab]
    pw = [_mm3(_split2(m), _bd_parts(m, mask_ss), NN) for m in l_ab]
    for _ in range(1, log_c - 1):
        res = [_mm3(_split2(cat([tinv[i], pw[i]])), _bd_parts(pw[i], mask_ss), NN) for i in range(ng)]
        tinv = [tinv[i] + res[i][:c] for i in range(ng)]
        pw = [m[c:] for m in res]
    tinv = [tinv[i] + _mm3(_split2(tinv[i]), _bd_parts(pw[i], mask_ss), NN) for i in range(ng)]

    z0 = [z_scr[i] for i in range(ng)]
    ws = [_mm3(ar[i], _split2(z0[i]), NT) for i in range(ng)]
    av = [_mm3(_split2(cat([l_ak[i], l_rk[i]])), _bd_parts(v[:, groups[i]], mask_kv), NN) for i in range(ng)]
    u = [_mm3(_split2(tinv[i]), _bd_parts(ws[i][:c] + av[i][:c], mask_kv), NN) for i in range(ng)]
    o = [ws[i][c:] + av[i][c:] + _mm3(_split2(l_rb[i]), _bd_parts(u[i], mask_kv), NN) for i in range(ng)]
    zr = lax.broadcasted_iota(I32, (gw, gw), 0) >> int(math.log2(HEAD_DIM))
    zmask = zr == (lax.broadcasted_iota(I32, (gw, gw), 1) >> int(math.log2(HEAD_DIM)))
    for i, s in enumerate(groups):
        upd = _mm3(_split2(cat([u[i], v[:, s]])), _split2(cat([b_e[:, s], k_e[:, s]])), TN)
        z_scr[i] = z0[i] * g_end[:, s] + jnp.where(zmask, upd, 0.0)

    o = jnp.concatenate(o, axis=1)
    inv_hd = 1.0 / HEAD_DIM
    dlt = o - _gsum(o, gm_ref) * inv_hd
    var = _gsum(dlt * dlt, gm_ref) * inv_hd
    on = dlt * lax.rsqrt(var + RWKV_GN_EPS) * lnw_ref[...] + lnb_ref[...]
    bonus = _gsum(r * kf * rk_ref[...], gm_ref) * v
    y_ref[0] = (on + bonus) * g

    @pl.when(t == nt - 1)
    def _():
        for h in range(N_HEADS):
            i, blk = head_block(h)
            st_ref[0, h] = z_scr[i, blk, blk]


def _rwkv(p, shift0, s0, prm, gmat, chunk):
    b, t, _ = p.shape
    row = lambda n: _const_spec((1, n))
    return pl.pallas_call(
        _rwkv_kernel,
        grid=(b, t // chunk),
        in_specs=[pl.BlockSpec((1, chunk, RWKV_COLS), lambda i, j: (i, j, 0)),
                  pl.BlockSpec((1, 1, RWKV_COLS), lambda i, j: (i, 0, 0)),
                  pl.BlockSpec((1, N_HEADS, HEAD_DIM, HEAD_DIM), lambda i, j: (i, 0, 0, 0)),
                  row(RWKV_COLS), row(WIDTH), _const_spec((DECAY_LORA, WIDTH)), row(WIDTH),
                  _const_spec((ICLR_LORA, WIDTH)), _const_spec((GATE_LORA, WIDTH)),
                  row(WIDTH), row(WIDTH), row(WIDTH), row(WIDTH), row(WIDTH), _const_spec((WIDTH, WIDTH))],
        out_specs=[pl.BlockSpec((1, chunk, WIDTH), lambda i, j: (i, j, 0)),
                   pl.BlockSpec((1, N_HEADS, HEAD_DIM, HEAD_DIM), lambda i, j: (i, 0, 0, 0)),
                   pl.BlockSpec((1, 1, RWKV_COLS), lambda i, j: (i, 0, 0))],
        out_shape=[jax.ShapeDtypeStruct((b, t, WIDTH), F32),
                   jax.ShapeDtypeStruct((b, N_HEADS, HEAD_DIM, HEAD_DIM), F32),
                   jax.ShapeDtypeStruct((b, 1, RWKV_COLS), F32)],
        scratch_shapes=[pltpu.VMEM((N_HEADS // HEADS_PER_GROUP, HEADS_PER_GROUP * HEAD_DIM,
                                    HEADS_PER_GROUP * HEAD_DIM), F32),
                        pltpu.VMEM((1, RWKV_COLS), F32)],
        compiler_params=_cparams(("parallel", "arbitrary")),
        name="rwkv7_mix",
    )(p, shift0, s0, prm["mu"], prm["w0"], prm["wb"], prm["a0"], prm["ab"], prm["gb"], prm["kk"], prm["ka"],
      prm["rk"], prm["lnw"], prm["lnb"], gmat)


def _merge_kernel(x_ref, ya_ref, yb_ref, gate_ref, mod_ref, g2_ref, woa_ref, wob_ref, wo_ref, wrh_ref, wrl_ref,
                  *rest):
    x1_ref, h2_ref, lg_ref = rest[-3:]
    bb, tt, d = x_ref.shape
    m = bb * tt
    gate = gate_ref[...].reshape(m, GATE_COLS)
    merged = (gate[:, 0:d] * _dot(ya_ref[...].reshape(m, WIDTH), woa_ref[...])
              + gate[:, d:2 * d] * _dot(yb_ref[...].reshape(m, WIDTH), wob_ref[...]))
    x1 = x_ref[...] + mod_ref[:, 2:3, :] * _dot(merged, wo_ref[...]).reshape(bb, tt, d)
    x1_ref[...] = x1
    ms = jnp.mean(x1 * x1, axis=-1, keepdims=True)
    h2 = x1 * lax.rsqrt(ms + RMS_EPS) * g2_ref[...]
    h2 = (h2 * (1.0 + mod_ref[:, 4:5, :]) + mod_ref[:, 3:4, :]).reshape(m, d)
    h2_ref[...] = h2
    hh, hl = _split2(h2)
    d2 = functools.partial(jnp.dot, preferred_element_type=F32)
    lg_ref[...] = d2(hh, wrh_ref[...]) + (d2(hh, wrl_ref[...]) + d2(hl, wrh_ref[...]))


def _merge(x, ya, yb, gate, mod, w, n_total, row_offset, shared=None):
    b, t, d = x.shape
    bb, tt = _token_blocks(b, t)
    nt = t // tt
    m = bb * tt
    off = row_offset // m

    def tok(cols):
        return pl.BlockSpec((bb, tt, cols), lambda i, j: (i, j, 0))

    def flat(cols):
        return pl.BlockSpec((m, cols), lambda i, j: (off + i * nt + j, 0))

    in_specs = [tok(d), tok(WIDTH), tok(WIDTH), tok(GATE_COLS),
                pl.BlockSpec((bb, 6, d), lambda i, j: (i, 0, 0)),
                _const_spec((1, d)), _const_spec((WIDTH, d)), _const_spec((WIDTH, d)), _const_spec((d, d)),
                _const_spec((d, N_EXPERTS)), _const_spec((d, N_EXPERTS))]
    args = [x, ya, yb, gate, mod.reshape(b, 6, d), w["g2"], w["w_oa"], w["w_ob"], w["w_o"], w["wr_hi"], w["wr_lo"]]
    aliases = {}
    if shared is not None:
        aliases = {len(args): 1, len(args) + 1: 2}
        in_specs += [pl.BlockSpec(memory_space=pl.ANY), pl.BlockSpec(memory_space=pl.ANY)]
        args += list(shared)
    return pl.pallas_call(
        _merge_kernel,
        grid=(b // bb, nt),
        in_specs=in_specs,
        out_specs=[tok(d), flat(d), flat(N_EXPERTS)],
        out_shape=[jax.ShapeDtypeStruct((b, t, d), F32), jax.ShapeDtypeStruct((n_total, d), F32),
                   jax.ShapeDtypeStruct((n_total, N_EXPERTS), F32)],
        input_output_aliases=aliases,
        compiler_params=_cparams(("parallel", "parallel")),
        name="merge_norm2_router",
    )(*args)


def _route_kernel(lg_ref, bias_ref, idx_ref, wt_ref, rank_ref, cnt_ref, carry):
    @pl.when(pl.program_id(0) == 0)
    def _():
        carry[...] = jnp.zeros(carry.shape, F32)

    tm = lg_ref.shape[0]
    scores = _sigmoid(lg_ref[...])
    sel = scores + bias_ref[...]
    lane = lax.broadcasted_iota(I32, (tm, N_EXPERTS), 1)
    grp = lane // EXPERTS_PER_GROUP
    neg_inf = -jnp.inf

    def first_argmax(vals):
        mx = jnp.max(vals, axis=-1, keepdims=True)
        return mx, jnp.min(jnp.where(vals == mx, lane, N_EXPERTS), axis=-1, keepdims=True)

    gs = []
    for g in range(N_GROUPS):
        mg = jnp.where(grp == g, sel, neg_inf)
        m1, i1 = first_argmax(mg)
        m2 = jnp.max(jnp.where(lane == i1, neg_inf, mg), axis=-1, keepdims=True)
        gs.append(m1 + m2)
    own = gs[0]
    for g in range(1, N_GROUPS):
        own = jnp.where(grp == g, gs[g], own)
    beaten = jnp.zeros((tm, N_EXPERTS), I32)
    for g in range(N_GROUPS):
        wins = jnp.where(gs[g] > own, 1, jnp.where(gs[g] == own, jnp.where(g < grp, 1, 0), 0))
        beaten = beaten + wins
    cur = jnp.where(beaten < TOPK_GROUPS, sel, neg_inf)

    idxs, ws = [], []
    picked = jnp.zeros((tm, N_EXPERTS), F32)
    for _ in range(TOP_K):
        _, ik = first_argmax(cur)
        hit = lane == ik
        idxs.append(ik)
        ws.append(jnp.sum(jnp.where(hit, scores, 0.0), axis=-1, keepdims=True))
        cur = jnp.where(hit, neg_inf, cur)
        picked = jnp.where(hit, 1.0, picked)
    wsum = ws[0]
    for k in range(1, TOP_K):
        wsum = wsum + ws[k]

    r = lax.broadcasted_iota(I32, (tm, tm), 0)
    c = lax.broadcasted_iota(I32, (tm, tm), 1)
    before = jnp.dot((r > c).astype(BF16), picked.astype(BF16), preferred_element_type=F32) + carry[...]
    carry[...] = carry[...] + jnp.sum(picked, axis=0, keepdims=True)
    cnt_ref[...] = carry[...]

    col = lax.broadcasted_iota(I32, (tm, TOP_K), 1)
    idx_o = jnp.zeros((tm, TOP_K), I32)
    wt_o = jnp.zeros((tm, TOP_K), F32)
    rank_o = jnp.zeros((tm, TOP_K), F32)
    for k in range(TOP_K):
        rk = jnp.sum(jnp.where(lane == idxs[k], before, 0.0), axis=-1, keepdims=True)
        idx_o = jnp.where(col == k, idxs[k], idx_o)
        wt_o = jnp.where(col == k, ws[k] / wsum * ROUTED_SCALE, wt_o)
        rank_o = jnp.where(col == k, rk, rank_o)
    idx_ref[...] = idx_o
    wt_ref[...] = wt_o
    rank_ref[...] = rank_o.astype(I32)


def _route(logits, bias, tm):
    n = logits.shape[0]
    tokk = pl.BlockSpec((tm, TOP_K), lambda i: (i, 0))
    return pl.pallas_call(
        _route_kernel,
        grid=(n // tm,),
        in_specs=[pl.BlockSpec((tm, N_EXPERTS), lambda i: (i, 0)), _const_spec((1, N_EXPERTS))],
        out_specs=[tokk, tokk, tokk, _const_spec((1, N_EXPERTS))],
        out_shape=[jax.ShapeDtypeStruct((n, TOP_K), I32), jax.ShapeDtypeStruct((n, TOP_K), F32),
                   jax.ShapeDtypeStruct((n, TOP_K), I32), jax.ShapeDtypeStruct((1, N_EXPERTS), F32)],
        scratch_shapes=[pltpu.VMEM((1, N_EXPERTS), F32)],
        compiler_params=_cparams(("arbitrary",)),
        name="route_topk",
    )(logits, bias)


def _plan_kernel(cnt_ref, start_ref, be_ref, valid_ref, nu_ref, *, blk):
    cnt = cnt_ref[...]
    padded = jnp.ceil(cnt * (1.0 / blk)) * blk
    e_r = lax.broadcasted_iota(I32, (N_EXPERTS, N_EXPERTS), 0)
    e_c = lax.broadcasted_iota(I32, (N_EXPERTS, N_EXPERTS), 1)
    incl = (e_r <= e_c).astype(BF16)
    ph, pm, plo = _split3(jnp.broadcast_to(padded, (8, N_EXPERTS)))
    d2 = functools.partial(jnp.dot, preferred_element_type=F32)
    pad_end = (d2(ph, incl) + (d2(pm, incl) + d2(plo, incl)))[0:1, :]
    pad_start = pad_end - padded
    start_ref[...] = pad_start.astype(I32)
    total = jnp.max(pad_end, axis=-1, keepdims=True)
    nu_ref[...] = jnp.broadcast_to(total * (1.0 / blk), (1, N_EXPERTS)).astype(I32)
    nb = be_ref.shape[0]
    first = (lax.broadcasted_iota(I32, (nb, N_EXPERTS), 0) * blk).astype(F32)
    lane = lax.broadcasted_iota(I32, (nb, N_EXPERTS), 1)
    inside = jnp.logical_and(pad_start <= first, first < pad_end)
    be_ref[...] = jnp.sum(jnp.where(inside, lane, 0), axis=-1, keepdims=True)
    rows = jnp.minimum(pad_start + cnt - first, float(blk))
    valid_ref[...] = jnp.sum(jnp.where(inside, rows, 0.0), axis=-1, keepdims=True).astype(I32)


def _plan(counts, n_blocks, blk):
    return pl.pallas_call(
        functools.partial(_plan_kernel, blk=blk),
        out_shape=[jax.ShapeDtypeStruct((1, N_EXPERTS), I32), jax.ShapeDtypeStruct((n_blocks, 1), I32),
                   jax.ShapeDtypeStruct((n_blocks, 1), I32), jax.ShapeDtypeStruct((1, N_EXPERTS), I32)],
        compiler_params=pltpu.CompilerParams(vmem_limit_bytes=VMEM_LIMIT),
        name="dispatch_plan",
    )(counts)


def _dest_kernel(idx_ref, rank_ref, start_ref, dest_ref):
    tm = idx_ref.shape[0]
    lane = lax.broadcasted_iota(I32, (tm, N_EXPERTS), 1)
    col = lax.broadcasted_iota(I32, (tm, TOP_K), 1)
    idx = idx_ref[...]
    base = jnp.zeros((tm, TOP_K), I32)
    for k in range(TOP_K):
        bk = jnp.sum(jnp.where(lane == idx[:, k:k + 1], start_ref[...], 0), axis=-1, keepdims=True)
        base = jnp.where(col == k, bk, base)
    dest_ref[...] = base + rank_ref[...]


def _dest(idx, rank, pad_start, tm):
    n = idx.shape[0]
    tokk = pl.BlockSpec((tm, TOP_K), lambda i: (i, 0))
    return pl.pallas_call(
        _dest_kernel,
        grid=(n // tm,),
        in_specs=[tokk, tokk, _const_spec((1, N_EXPERTS))],
        out_specs=tokk,
        out_shape=jax.ShapeDtypeStruct((n, TOP_K), I32),
        compiler_params=_cparams(("parallel",)),
        name="dispatch_dest",
    )(idx, rank, pad_start)


def _row_copy(src_ref, src_row, dst_ref, dst_row, sem):
    return pltpu.make_async_copy(src_ref.at[pl.ds(src_row, 1)], dst_ref.at[pl.ds(dst_row, 1)], sem)


def _dispatch_kernel(dest_ref, h2_ref, xs_ref, sem):
    tm = h2_ref.shape[0]

    def issue(r, carry):
        for k in range(TOP_K):
            _row_copy(h2_ref, r, xs_ref, dest_ref[k, r], sem).start()
        return carry

    lax.fori_loop(0, tm, issue, 0)

    def drain(r, carry):
        for k in range(TOP_K):
            _row_copy(h2_ref, 0, xs_ref, 0, sem).wait()
        return carry

    lax.fori_loop(0, tm, drain, 0)


def _dispatch(h2, dest_t, n_slots, tm):
    n, d = h2.shape
    return pl.pallas_call(
        _dispatch_kernel,
        grid=(n // tm,),
        in_specs=[pl.BlockSpec((TOP_K, tm), lambda i: (0, i), memory_space=pltpu.SMEM),
                  pl.BlockSpec((tm, d), lambda i: (i, 0))],
        out_specs=pl.BlockSpec(memory_space=pl.ANY),
        out_shape=jax.ShapeDtypeStruct((n_slots, d), F32),
        scratch_shapes=[pltpu.SemaphoreType.DMA(())],
        compiler_params=_cparams(("arbitrary",)),
        name="moe_dispatch",
    )(dest_t, h2)


def _expert_kernel(be_ref, valid_ref, nu_ref, x_ref, wg_ref, wu_ref, wd_ref, y_ref, wg_b, wu_b, wd_b):
    i = pl.program_id(0)
    nv = valid_ref[i]
    new_expert = jnp.logical_or(i == 0, be_ref[i] != be_ref[jnp.maximum(i - 1, 0)])

    @pl.when(jnp.logical_and(nv > 0, new_expert))
    def _():
        wg_b[...] = wg_ref[0].astype(BF16)
        wu_b[...] = wu_ref[0].astype(BF16)
        wd_b[...] = wd_ref[0].astype(BF16)

    @pl.when(nv > 0)
    def _():
        blk = x_ref.shape[0]
        rows = lax.broadcasted_iota(I32, (blk, 1), 0)
        x = jnp.where(rows < nv, x_ref[...], 0.0).astype(BF16)
        hg = jnp.dot(x, wg_b[...], preferred_element_type=F32)
        hu = jnp.dot(x, wu_b[...], preferred_element_type=F32)
        y_ref[...] = jnp.dot((_silu(hg) * hu).astype(BF16), wd_b[...], preferred_element_type=F32)


def _experts(xs, block_e, valid, n_used, w_eg, w_eu, w_ed, blk):
    n_slots, d = xs.shape
    n_blocks = n_slots // blk

    def row_blk(i, be, valid, nu):
        return (jnp.minimum(i, nu[0] - 1), 0)

    def w_blk(i, be, valid, nu):
        return (be[i], 0, 0)

    return pl.pallas_call(
        _expert_kernel,
        grid_spec=pltpu.PrefetchScalarGridSpec(
            num_scalar_prefetch=3,
            grid=(n_blocks,),
            in_specs=[pl.BlockSpec((blk, d), row_blk),
                      pl.BlockSpec((1, d, D_EXPERT), w_blk), pl.BlockSpec((1, d, D_EXPERT), w_blk),
                      pl.BlockSpec((1, D_EXPERT, d), w_blk)],
            out_specs=pl.BlockSpec((blk, d), row_blk),
            scratch_shapes=[pltpu.VMEM((d, D_EXPERT), BF16), pltpu.VMEM((d, D_EXPERT), BF16),
                            pltpu.VMEM((D_EXPERT, d), BF16)]),
        out_shape=jax.ShapeDtypeStruct((n_slots, d), F32),
        compiler_params=_cparams(("arbitrary",)),
        name="moe_experts",
    )(block_e, valid, n_used, xs, w_eg, w_eu, w_ed)


def _final_kernel(dest_ref, x1_ref, h2_ref, wt_ref, mod_ref, wsg_ref, wsu_ref, wsd_ref, ys_ref, o_ref, ybuf, sem):
    bb, tt, d = x1_ref.shape
    m = bb * tt

    def issue(r, carry):
        for k in range(TOP_K):
            _row_copy(ys_ref, dest_ref[k, r], ybuf.at[k], r, sem).start()
        return carry

    lax.fori_loop(0, m, issue, 0)

    hb = h2_ref[...].astype(BF16)
    hg = jnp.dot(hb, wsg_ref[...], preferred_element_type=F32)
    hu = jnp.dot(hb, wsu_ref[...], preferred_element_type=F32)
    ffn = _dot(_silu(hg) * hu, wsd_ref[...])

    def drain(r, carry):
        for k in range(TOP_K):
            _row_copy(ys_ref, 0, ybuf.at[k], 0, sem).wait()
        return carry

    lax.fori_loop(0, m, drain, 0)

    wt = wt_ref[...]
    for k in range(TOP_K):
        ffn = ffn + wt[:, k:k + 1] * ybuf[k]
    o_ref[...] = x1_ref[...] + mod_ref[:, 5:6, :] * ffn.reshape(bb, tt, d)


def _final(x1, h2_all, wts_all, dest_t, ys, mod, w, row_offset):
    b, t, d = x1.shape
    bb, tt = _token_blocks(b, t)
    nt = t // tt
    m = bb * tt
    off = row_offset // m

    def flat_idx(i, j):
        return off + i * nt + j

    return pl.pallas_call(
        _final_kernel,
        grid=(b // bb, nt),
        in_specs=[pl.BlockSpec((TOP_K, m), lambda i, j: (0, flat_idx(i, j)), memory_space=pltpu.SMEM),
                  pl.BlockSpec((bb, tt, d), lambda i, j: (i, j, 0)),
                  pl.BlockSpec((m, d), lambda i, j: (flat_idx(i, j), 0)),
                  pl.BlockSpec((m, TOP_K), lambda i, j: (flat_idx(i, j), 0)),
                  pl.BlockSpec((bb, 6, d), lambda i, j: (i, 0, 0)),
                  _const_spec((d, D_EXPERT)), _const_spec((d, D_EXPERT)), _const_spec((D_EXPERT, d)),
                  pl.BlockSpec(memory_space=pl.ANY)],
        out_specs=pl.BlockSpec((bb, tt, d), lambda i, j: (i, j, 0)),
        out_shape=jax.ShapeDtypeStruct((b, t, d), F32),
        scratch_shapes=[pltpu.VMEM((TOP_K, m, d), F32), pltpu.SemaphoreType.DMA(())],
        compiler_params=_cparams(("arbitrary", "arbitrary")),
        name="moe_combine_final",
    )(dest_t, x1, h2_all, wts_all, mod.reshape(b, 6, d), w["w_sg"], w["w_su"], w["w_sd"], ys)


def _moe_routed(h2_all, logits_all, w, blk=256, tm=256):
    n = h2_all.shape[0]
    n_blocks = (n * TOP_K + N_EXPERTS * (blk - 1)) // blk + 1
    n_blocks = (n_blocks + 7) // 8 * 8
    idx, wts, rank, counts = _route(logits_all, w["router_bias"], tm)
    pad_start, block_e, valid, n_used = _plan(counts, n_blocks, blk)
    block_e = block_e.reshape(n_blocks)
    valid = valid.reshape(n_blocks)
    n_used = n_used[0, 0:1]
    dest_t = jnp.transpose(_dest(idx, rank, pad_start, tm))
    xs = _dispatch(h2_all, dest_t, n_blocks * blk, tm)
    ys = _experts(xs, block_e, valid, n_used, w["w_eg"], w["w_eu"], w["w_ed"], blk)
    return ys, dest_t, wts


def _prep(raw):
    p = {k: v[0] for k, v in raw.items()}
    w_in = p["w_in"]
    o_fox = RWKV_COLS
    o_fl = o_fox + FOX_MAIN_COLS
    o_gate = o_fl + N_HEADS
    row = lambda a: a.reshape(1, -1)
    return dict(
        w_ada=p["w_ada"], b_ada=p["b_ada"],
        g1=row(p["norm1_g"]), g2=row(p["norm2_g"]),
        wr=w_in[:, :o_fox].astype(BF16),
        wf=w_in[:, o_fox:o_fl].astype(BF16),
        wfl=jnp.pad(w_in[:, o_fl:o_gate], ((0, 0), (0, LANES - N_HEADS))).astype(BF16),
        wg=w_in[:, o_gate:].astype(BF16),
        qn=row(jnp.tile(p["fox_q_norm"], N_HEADS)), kn=row(jnp.tile(p["fox_k_norm"], N_HEADS)),
        fb=row(p["fox_f_bias"]),
        gmat=_group_ones(),
        rwkv=dict(mu=row(p["rwkv_mu"]), w0=row(p["rwkv_w0"]), wb=p["rwkv_w_lora_b"], a0=row(p["rwkv_a0"]),
                  ab=p["rwkv_a_lora_b"], gb=p["rwkv_g_lora_b"], kk=row(p["rwkv_k_k"]), ka=row(p["rwkv_k_a"]),
                  rk=row(p["rwkv_r_k"]), lnw=row(p["rwkv_ln_w"]), lnb=row(p["rwkv_ln_b"])),
        w_oa=p["w_out_rwkv"].astype(BF16), w_ob=p["w_out_fox"].astype(BF16), w_o=p["w_out"].astype(BF16),
        wr_hi=p["w_router"].astype(BF16),
        wr_lo=(p["w_router"] - p["w_router"].astype(BF16).astype(F32)).astype(BF16),
        router_bias=row(p["router_bias"]),
        w_eg=p["w_exp_gate"], w_eu=p["w_exp_up"], w_ed=p["w_exp_down"],
        w_sg=p["w_sh_gate"].astype(BF16), w_su=p["w_sh_up"].astype(BF16), w_sd=p["w_sh_down"].astype(BF16),
    )


def _token_blocks(b, t):
    if t >= 256:
        return 1, 256
    bb = max(1, min(b, 256 // t))
    while b % bb:
        bb -= 1
    return bb, t


def _mix_path(x, mod, shift0, wkv0, past_k, past_v, past_logf, w):
    b, t, d = x.shape
    bb, tt = _token_blocks(b, t)
    pr, q, k, v, sg, logf, gate = _inproj(x, mod.reshape(b, 6, d), w["g1"], w["wr"], w["wf"], w["wfl"], w["wg"],
                                          w["qn"], w["kn"], w["fb"], w["gmat"], bb, tt)
    n_past = past_k.shape[1]
    ct = min(t, 256)
    if n_past:
        f_past = _cumsum_time(past_logf, jnp.zeros((b, 1, N_HEADS), F32), min(n_past, 256))
        init = f_past[:, n_past - 1:n_past, :]
        past = (past_k, past_v, jnp.swapaxes(f_past, 1, 2))
    else:
        init = jnp.zeros((b, 1, N_HEADS), F32)
        past = None
    f_new = _cumsum_time(logf, init, ct)
    y_fox = _fox_attention(q, f_new, sg, k, v, jnp.swapaxes(f_new, 1, 2), past=past, tq=min(t, 512),
                           tk_past=min(max(n_past, 1), 512))
    y_rwkv, wkv_new, shift_new = _rwkv(pr, shift0.reshape(b, 1, RWKV_COLS), wkv0, w["rwkv"], w["gmat"],
                                       min(t, 64))
    return y_rwkv, y_fox, gate, wkv_new, shift_new, k, v, logf


def _layer(paths, w):
    n_b = [p[0].shape[0] for p in paths]
    mod_all = _ada(jnp.concatenate([p[1] for p in paths], axis=0), w["w_ada"], w["b_ada"])
    mods, o = [], 0
    for nb in n_b:
        mods.append(mod_all[o:o + nb])
        o += nb
    n_total = sum(p[0].shape[0] * p[0].shape[1] for p in paths)
    mixed, x1s = [], []
    shared, row = None, 0
    for (x, _, shift0, wkv0, pk, pv, plf), mod in zip(paths, mods):
        ya, yb, gate, wkv_new, shift_new, k, v, logf = _mix_path(x, mod, shift0, wkv0, pk, pv, plf, w)
        x1, h2_all, lg_all = _merge(x, ya, yb, gate, mod, w, n_total, row, shared)
        shared = (h2_all, lg_all)
        row += x.shape[0] * x.shape[1]
        mixed.append((wkv_new, shift_new, k, v, logf))
        x1s.append(x1)
    ys, dest_t, wts = _moe_routed(h2_all, lg_all, w)
    outs, row = [], 0
    for x1, mod, st in zip(x1s, mods, mixed):
        y = _final(x1, h2_all, wts, dest_t, ys, mod, w, row)
        row += x1.shape[0] * x1.shape[1]
        outs.append((y,) + st)
    return outs


def kernel(x_prompt, x_sample, c_prompt, c_sample, state_rwkv_wkv, state_rwkv_shift, cache_fox_k, cache_fox_v,
           cache_fox_logf, w_ada, b_ada, norm1_g, norm2_g, w_in, rwkv_mu, rwkv_w0, rwkv_w_lora_b, rwkv_a0,
           rwkv_a_lora_b, rwkv_g_lora_b, rwkv_k_k, rwkv_k_a, rwkv_r_k, rwkv_ln_w, rwkv_ln_b, fox_q_norm,
           fox_k_norm, fox_f_bias, w_out_rwkv, w_out_fox, w_out, w_router, router_bias, w_exp_gate, w_exp_up,
           w_exp_down, w_sh_gate, w_sh_up, w_sh_down):
    raw = dict(w_ada=w_ada, b_ada=b_ada, norm1_g=norm1_g, norm2_g=norm2_g, w_in=w_in, rwkv_mu=rwkv_mu,
               rwkv_w0=rwkv_w0, rwkv_w_lora_b=rwkv_w_lora_b, rwkv_a0=rwkv_a0, rwkv_a_lora_b=rwkv_a_lora_b,
               rwkv_g_lora_b=rwkv_g_lora_b, rwkv_k_k=rwkv_k_k, rwkv_k_a=rwkv_k_a, rwkv_r_k=rwkv_r_k,
               rwkv_ln_w=rwkv_ln_w, rwkv_ln_b=rwkv_ln_b, fox_q_norm=fox_q_norm, fox_k_norm=fox_k_norm,
               fox_f_bias=fox_f_bias, w_out_rwkv=w_out_rwkv, w_out_fox=w_out_fox, w_out=w_out,
               w_router=w_router, router_bias=router_bias, w_exp_gate=w_exp_gate, w_exp_up=w_exp_up,
               w_exp_down=w_exp_down, w_sh_gate=w_sh_gate, w_sh_up=w_sh_up, w_sh_down=w_sh_down)
    assert w_in.shape[0] == 1, "single-layer stack"
    w = _prep(raw)
    bp, tp, _ = x_prompt.shape
    bs, ts, _ = x_sample.shape
    n_past = cache_fox_k.shape[2]
    prompt = (x_prompt, c_prompt, jnp.zeros((bp, RWKV_COLS), F32),
              jnp.zeros((bp, N_HEADS, HEAD_DIM, HEAD_DIM), F32),
              jnp.zeros((bp, 0, WIDTH), F32), jnp.zeros((bp, 0, WIDTH), F32), jnp.zeros((bp, 0, N_HEADS), F32))
    sample = (x_sample, c_sample, state_rwkv_shift[0], state_rwkv_wkv[0],
              cache_fox_k[0].reshape(bs, n_past, WIDTH), cache_fox_v[0].reshape(bs, n_past, WIDTH),
              cache_fox_logf[0])
    (yp, wkv_p, sh_p, k_p, v_p, lf_p), (ysm, wkv_s, sh_s, k_s, v_s, lf_s) = _layer([prompt, sample], w)

    def heads(a):
        return a.reshape((1,) + a.shape[:2] + (N_HEADS, HEAD_DIM))

    return (yp, ysm,
            wkv_p[None], sh_p.reshape(1, bp, RWKV_COLS), heads(k_p), heads(v_p), lf_p[None],
            wkv_s[None], sh_s.reshape(1, bs, RWKV_COLS), heads(k_s), heads(v_s), lf_s[None])
```

```python
import functools
import math

import jax
import jax.numpy as jnp
from jax import lax
from jax.experimental import pallas as pl
from jax.experimental.pallas import tpu as pltpu

F32 = jnp.float32
BF16 = jnp.bfloat16
I32 = jnp.int32

D_MODEL = 1024
N_HEADS = 8
HEAD_DIM = 64
WIDTH = N_HEADS * HEAD_DIM
HEADS_PER_GROUP = 4
RWKV_CHUNK = 64
RWKV_CHUNKS_PER_STEP = 4
DECAY_LORA = 64
ICLR_LORA = 64
GATE_LORA = 128
RWKV_COLS = 3 * WIDTH + DECAY_LORA + ICLR_LORA + GATE_LORA
FOX_MAIN_COLS = 4 * WIDTH
GATE_COLS = 2 * D_MODEL
RWKV_GN_EPS = HEAD_DIM * 1e-5
L2_EPS = 1e-12
RMS_EPS = 1e-6
N_EXPERTS = 256
TOP_K = 8
N_GROUPS = 8
TOPK_GROUPS = 4
EXPERTS_PER_GROUP = N_EXPERTS // N_GROUPS
D_EXPERT = 256
ROUTED_SCALE = 2.5

LANES = 128
VMEM_LIMIT = 56 * 1024 * 1024
NEG_BIG = -1e30

NN = (((1,), (0,)), ((), ()))
NT = (((1,), (1,)), ((), ()))
TN = (((0,), (0,)), ((), ()))


def _cparams(sem):
    return pltpu.CompilerParams(dimension_semantics=sem, vmem_limit_bytes=VMEM_LIMIT)


def _dot(a, b, dims=NN):
    return lax.dot_general(a.astype(BF16), b.astype(BF16), dims, preferred_element_type=F32)


def _split2(a):
    hi = a.astype(BF16)
    lo = (a - hi.astype(F32)).astype(BF16)
    return hi, lo


def _split3(a):
    hi = a.astype(BF16)
    r1 = a - hi.astype(F32)
    mid = r1.astype(BF16)
    lo = (r1 - mid.astype(F32)).astype(BF16)
    return hi, mid, lo


def _dot3(a, b, dims=NN):
    ah, al = _split2(a)
    bh, bl = _split2(b)
    d = functools.partial(lax.dot_general, dimension_numbers=dims, preferred_element_type=F32)
    return d(ah, bh) + (d(ah, bl) + d(al, bh))


def _mm3(a, b, dims):
    d = functools.partial(lax.dot_general, dimension_numbers=dims, preferred_element_type=F32)
    return d(a[0], b[0]) + (d(a[0], b[1]) + d(a[1], b[0]))


def _bd_parts(x, mask):
    out = []
    for part in _split2(x):
        tiled = jnp.concatenate([part] * HEADS_PER_GROUP, axis=0)
        out.append(jnp.where(mask, tiled, jnp.zeros_like(tiled)))
    return tuple(out)


def _dot_exact_rhs(a_exact, b, dims=NN):
    ab = a_exact.astype(BF16)
    bh, bm, bl = _split3(b)
    d = functools.partial(lax.dot_general, dimension_numbers=dims, preferred_element_type=F32)
    return d(ab, bh) + (d(ab, bm) + d(ab, bl))


def _gsum(x, g_ref):
    hi, mid, lo = _split3(x)
    g = g_ref[...]
    d = functools.partial(jnp.dot, preferred_element_type=F32)
    return d(hi, g) + (d(mid, g) + d(lo, g))


def _sigmoid(x):
    return 1.0 / (1.0 + jnp.exp(-x))


def _softplus(x):
    return jnp.maximum(x, 0.0) + jnp.log1p(jnp.exp(-jnp.abs(x)))


def _silu(x):
    return x * _sigmoid(x)


def _group_ones():
    h = jnp.arange(WIDTH, dtype=I32) // HEAD_DIM
    return (h[:, None] == h[None, :]).astype(BF16)


def _ada_kernel(c_ref, w_ref, b_ref, o_ref):
    o_ref[...] = _dot(_silu(c_ref[...]), w_ref[...]) + b_ref[...]


def _ada(c, w_ada, b_ada):
    nb = c.shape[0]
    n_out = w_ada.shape[1]
    blk = D_MODEL
    return pl.pallas_call(
        _ada_kernel,
        grid=(n_out // blk,),
        in_specs=[pl.BlockSpec((nb, D_MODEL), lambda j: (0, 0)),
                  pl.BlockSpec((D_MODEL, blk), lambda j: (0, j)),
                  pl.BlockSpec((1, blk), lambda j: (0, j))],
        out_specs=pl.BlockSpec((nb, blk), lambda j: (0, j)),
        out_shape=jax.ShapeDtypeStruct((nb, n_out), F32),
        compiler_params=_cparams(("parallel",)),
        name="ada_mod",
    )(c, w_ada, b_ada.reshape(1, n_out))


def _inproj_kernel(x_ref, mod_ref, g1_ref, wr_ref, wf_ref, wfl_ref, wg_ref, qn_ref, kn_ref, fb_ref, gm_ref, f0_ref,
                   pr_ref, q_ref, k_ref, v_ref, sg_ref, lf_ref, cf_ref, gate_ref, carry):
    bb, tt, d = x_ref.shape
    m = bb * tt
    x = x_ref[...]
    ms = jnp.mean(x * x, axis=-1, keepdims=True)
    h = x * lax.rsqrt(ms + RMS_EPS) * g1_ref[...]
    h = h * (1.0 + mod_ref[:, 1:2, :]) + mod_ref[:, 0:1, :]
    hb = h.reshape(m, d).astype(BF16)

    pr_ref[...] = jnp.dot(hb, wr_ref[...], preferred_element_type=F32).reshape(bb, tt, RWKV_COLS)

    f = jnp.dot(hb, wf_ref[...], preferred_element_type=F32)
    q = f[:, 0:WIDTH]
    k = f[:, WIDTH:2 * WIDTH]
    v = f[:, 2 * WIDTH:3 * WIDTH]
    og = f[:, 3 * WIDTH:4 * WIDTH]
    inv_hd = 1.0 / HEAD_DIM
    q = q * lax.rsqrt(_gsum(q * q, gm_ref) * inv_hd + RMS_EPS) * qn_ref[...]
    k = k * lax.rsqrt(_gsum(k * k, gm_ref) * inv_hd + RMS_EPS) * kn_ref[...]
    q_ref[...] = (q * (HEAD_DIM ** -0.5)).astype(BF16).reshape(bb, tt, WIDTH)
    k_ref[...] = k.reshape(bb, tt, WIDTH)
    v_ref[...] = v.reshape(bb, tt, WIDTH)
    sg_ref[...] = _sigmoid(og).reshape(bb, tt, WIDTH)

    fl = jnp.dot(hb, wfl_ref[...], preferred_element_type=F32)[:, 0:N_HEADS] + fb_ref[...]
    lf = -_softplus(-fl)
    lf_ref[...] = lf.reshape(bb, tt, N_HEADS)

    @pl.when(pl.program_id(1) == 0)
    def _():
        carry[...] = f0_ref[...]

    r = lax.broadcasted_iota(I32, (m, m), 0)
    c = lax.broadcasted_iota(I32, (m, m), 1)
    tri = jnp.logical_and(r // tt == c // tt, r >= c).astype(F32)
    cf = _dot_exact_rhs(tri, lf).reshape(bb, tt, N_HEADS) + carry[...]
    cf_ref[...] = cf
    carry[...] = cf[:, tt - 1:tt, :]

    gate_ref[...] =_sigmoid(jnp.dot(hb, wg_ref[...], preferred_element_type=F32)).reshape(bb, tt, GATE_COLS)


def _const_spec(shape):
    nd = len(shape)
    return pl.BlockSpec(shape, lambda *_: (0,) * nd)


def _inproj(x, mod, g1, wr, wf, wfl, wg, qn, kn, fb, gmat, f0, bb, tt):
    b, t, d = x.shape
    grid = (b // bb, t // tt)

    def tok(cols):
        return pl.BlockSpec((bb, tt, cols), lambda i, j: (i, j, 0))

    out_cols = [(RWKV_COLS, F32), (WIDTH, BF16), (WIDTH, F32), (WIDTH, F32), (WIDTH, F32), (N_HEADS, F32),
                (N_HEADS, F32), (GATE_COLS, F32)]
    return pl.pallas_call(
        _inproj_kernel,
        grid=grid,
        in_specs=[tok(d),
                  pl.BlockSpec((bb, 6, d), lambda i, j: (i, 0, 0)),
                  _const_spec((1, d)),
                  _const_spec(wr.shape), _const_spec(wf.shape), _const_spec(wfl.shape), _const_spec(wg.shape),
                  _const_spec((1, WIDTH)), _const_spec((1, WIDTH)), _const_spec((1, N_HEADS)),
                  _const_spec((WIDTH, WIDTH)),
                  pl.BlockSpec((bb, 1, N_HEADS), lambda i, j: (i, 0, 0))],
        out_specs=[tok(c) for c, _ in out_cols],
        out_shape=[jax.ShapeDtypeStruct((b, t, c), dt) for c, dt in out_cols],
        scratch_shapes=[pltpu.VMEM((bb, 1, N_HEADS), F32)],
        compiler_params=_cparams(("parallel", "arbitrary")),
        name="norm1_inproj",
    )(x, mod, g1, wr, wf, wfl, wg, qn, kn, fb, gmat, f0)


def _past_cumsum_kernel(x_ref, o_ref):
    x = x_ref[0]
    rows = x.shape[0]
    li = lax.broadcasted_iota(I32, (LANES, LANES), 0)
    lj = lax.broadcasted_iota(I32, (LANES, LANES), 1)
    same_head = (li % N_HEADS) == (lj % N_HEADS)
    within = jnp.logical_and(same_head, li // N_HEADS <= lj // N_HEADS).astype(BF16)
    xh, xm, xl = _split3(x)
    d2 = functools.partial(jnp.dot, preferred_element_type=F32)
    in_row = d2(xh, within) + (d2(xm, within) + d2(xl, within))
    sh = same_head.astype(BF16)
    row_tot = d2(xh, sh) + (d2(xm, sh) + d2(xl, sh))
    ri = lax.broadcasted_iota(I32, (rows, rows), 0)
    ci = lax.broadcasted_iota(I32, (rows, rows), 1)
    o_ref[0] = in_row + _dot_exact_rhs((ri > ci).astype(F32), row_tot)


def _past_cumsum(past_logf):
    b, p, h = past_logf.shape
    rows = p * h // LANES
    flat = past_logf.reshape(b, rows, LANES)
    out = pl.pallas_call(
        _past_cumsum_kernel,
        grid=(b,),
        in_specs=[pl.BlockSpec((1, rows, LANES), lambda i: (i, 0, 0))],
        out_specs=pl.BlockSpec((1, rows, LANES), lambda i: (i, 0, 0)),
        out_shape=jax.ShapeDtypeStruct((b, rows, LANES), F32),
        compiler_params=_cparams(("parallel",)),
        name="cache_logf_cumsum",
    )(flat)
    return out.reshape(b, p, h)


def _fox_kernel(*refs, n_past_blocks, tq):
    if n_past_blocks:
        (q_ref, fq_ref, sg_ref, kp_ref, vp_ref, fkp_ref, kn_ref, vn_ref, fkn_ref,
         o_ref, m_scr, l_scr, acc_scr) = refs
    else:
        q_ref, fq_ref, sg_ref, kn_ref, vn_ref, fkn_ref, o_ref, m_scr, l_scr, acc_scr = refs
    qi = pl.program_id(1)
    ki = pl.program_id(2)
    nk = pl.num_programs(2)

    @pl.when(ki == 0)
    def _():
        m_scr[...] = jnp.full(m_scr.shape, NEG_BIG, F32)
        l_scr[...] = jnp.zeros(l_scr.shape, F32)
        acc_scr[...] = jnp.zeros(acc_scr.shape, F32)

    lane_a = lax.broadcasted_iota(I32, (tq, LANES), 1) < HEAD_DIM

    def step(k_ref, v_ref, fk_ref, diag):
        tk = k_ref.shape[1]
        if diag:
            rq = lax.broadcasted_iota(I32, (tq, tk), 0)
            ck = lax.broadcasted_iota(I32, (tq, tk), 1)
            visible = ck <= rq
        fq_all = fq_ref[0]
        pairs = range(N_HEADS // 2)
        cols = [slice(j * LANES, (j + 1) * LANES) for j in pairs]
        scores = []
        for j in pairs:
            qj = q_ref[0, :, cols[j]]
            kb = k_ref[0, :, cols[j]].astype(BF16)
            for hh in range(2):
                h = 2 * j + hh
                qm = jnp.where(lane_a if hh == 0 else jnp.logical_not(lane_a), qj, jnp.zeros_like(qj))
                s = lax.dot_general(qm, kb, NT, preferred_element_type=F32)
                s = s + fq_all[:, h:h + 1] - fk_ref[0, h:h + 1, :]
                if diag:
                    s = jnp.where(visible, s, NEG_BIG)
                scores.append(s)
        alphas, probs = [], []
        for h in range(N_HEADS):
            m_old = m_scr[h]
            m_new = jnp.maximum(m_old, jnp.max(scores[h], axis=-1, keepdims=True))
            alpha = jnp.exp(m_old - m_new)
            p = jnp.exp(scores[h] - m_new)
            l_scr[h] = alpha * l_scr[h] + jnp.sum(p, axis=-1, keepdims=True)
            m_scr[h] = m_new
            alphas.append(alpha)
            probs.append(p.astype(BF16))
        for j in pairs:
            vb = v_ref[0, :, cols[j]].astype(BF16)
            pv0 = jnp.dot(probs[2 * j], vb, preferred_element_type=F32)
            pv1 = jnp.dot(probs[2 * j + 1], vb, preferred_element_type=F32)
            acc_scr[:, cols[j]] = (acc_scr[:, cols[j]] * jnp.where(lane_a, alphas[2 * j], alphas[2 * j + 1])
                                   + jnp.where(lane_a, pv0, pv1))

    if n_past_blocks:
        @pl.when(ki < n_past_blocks)
        def _():
            step(kp_ref, vp_ref, fkp_ref, False)

    kn = ki - n_past_blocks

    @pl.when(jnp.logical_and(kn >= 0, kn < qi))
    def _():
        step(kn_ref, vn_ref, fkn_ref, False)

    @pl.when(kn == qi)
    def _():
        step(kn_ref, vn_ref, fkn_ref, True)

    @pl.when(ki == nk - 1)
    def _():
        for j in range(N_HEADS // 2):
            cols = slice(j * LANES, (j + 1) * LANES)
            l = jnp.where(lane_a, l_scr[2 * j], l_scr[2 * j + 1])
            o_ref[0, :, cols] = acc_scr[:, cols] / l * sg_ref[0, :, cols]


def _fox_attention(q, fq, sg, k_new, v_new, fk_new_t, past=None, tq=512, tk_past=512):
    b, t, _ = q.shape
    nq = t // tq
    n_past_blocks = 0 if past is None else past[0].shape[1] // tk_past
    nk = n_past_blocks + nq

    def new_idx(i, qi, ki):
        return jnp.clip(ki - n_past_blocks, 0, qi)

    in_specs = [pl.BlockSpec((1, tq, WIDTH), lambda i, qi, ki: (i, qi, 0)),
                pl.BlockSpec((1, tq, N_HEADS), lambda i, qi, ki: (i, qi, 0)),
                pl.BlockSpec((1, tq, WIDTH), lambda i, qi, ki: (i, qi, 0))]
    args = [q, fq, sg]
    if n_past_blocks:
        def past_idx(i, qi, ki):
            return jnp.minimum(ki, n_past_blocks - 1)
        in_specs += [pl.BlockSpec((1, tk_past, WIDTH), lambda i, qi, ki: (i, past_idx(i, qi, ki), 0)),
                     pl.BlockSpec((1, tk_past, WIDTH), lambda i, qi, ki: (i, past_idx(i, qi, ki), 0)),
                     pl.BlockSpec((1, N_HEADS, tk_past), lambda i, qi, ki: (i, 0, past_idx(i, qi, ki)))]
        args += list(past)
    in_specs += [pl.BlockSpec((1, tq, WIDTH), lambda i, qi, ki: (i, new_idx(i, qi, ki), 0)),
                 pl.BlockSpec((1, tq, WIDTH), lambda i, qi, ki: (i, new_idx(i, qi, ki), 0)),
                 pl.BlockSpec((1, N_HEADS, tq), lambda i, qi, ki: (i, 0, new_idx(i, qi, ki)))]
    args += [k_new, v_new, fk_new_t]
    return pl.pallas_call(
        functools.partial(_fox_kernel, n_past_blocks=n_past_blocks, tq=tq),
        grid=(b, nq, nk),
        in_specs=in_specs,
        out_specs=pl.BlockSpec((1, tq, WIDTH), lambda i, qi, ki: (i, qi, 0)),
        out_shape=jax.ShapeDtypeStruct((b, t, WIDTH), F32),
        scratch_shapes=[pltpu.VMEM((N_HEADS, tq, 1), F32), pltpu.VMEM((N_HEADS, tq, 1), F32),
                        pltpu.VMEM((tq, WIDTH), F32)],
        compiler_params=_cparams(("parallel", "parallel", "arbitrary")),
        name="fox_attention",
    )(*args)


def _rwkv_kernel(p_ref, sh0_ref, s0_ref, mu_ref, w0_ref, wb_ref, a0_ref, ab_ref, gb_ref, kk_ref, ka_ref, rk_ref,
                 lnw_ref, lnb_ref, gm_ref, y_ref, st_ref, sht_ref, z_scr, prev_scr, *, c):
    t = pl.program_id(1)
    nt = pl.num_programs(1)
    n_rows = p_ref.shape[1]
    n_chunks = n_rows // c

    def head_block(h):
        lo = (h % HEADS_PER_GROUP) * HEAD_DIM
        return h // HEADS_PER_GROUP, slice(lo, lo + HEAD_DIM)

    @pl.when(t == 0)
    def _():
        z_scr[...] = jnp.zeros(z_scr.shape, F32)
        for h in range(N_HEADS):
            i, blk = head_block(h)
            z_scr[i, blk, blk] = s0_ref[0, h]
        prev_scr[...] = sh0_ref[0]

    p = p_ref[0]
    row = lax.broadcasted_iota(I32, p.shape, 0)
    prev = jnp.where(row == 0, prev_scr[...], pltpu.roll(p, 1, 0))
    last = p[n_rows - 1:n_rows, :]
    prev_scr[...] = last
    sht_ref[0] = last

    pm = p + (prev - p) * mu_ref[...]
    r = pm[:, 0:WIDTH]
    k = pm[:, WIDTH:2 * WIDTH]
    v = pm[:, 2 * WIDTH:3 * WIDTH]
    o1 = 3 * WIDTH
    wd = pm[:, o1:o1 + DECAY_LORA]
    ad = pm[:, o1 + DECAY_LORA:o1 + DECAY_LORA + ICLR_LORA]
    gd = pm[:, o1 + DECAY_LORA + ICLR_LORA:RWKV_COLS]

    w = -_softplus(-(w0_ref[...] + _dot(jnp.tanh(wd), wb_ref[...]))) - 0.5
    lw = -jnp.exp(w)
    a = _sigmoid(a0_ref[...] + _dot(ad, ab_ref[...]))
    g = _dot(_sigmoid(gd), gb_ref[...])
    kk = k * kk_ref[...]
    kk = kk / jnp.maximum(jnp.sqrt(_gsum(kk * kk, gm_ref)), L2_EPS)
    kf = k * (1.0 + (a - 1.0) * ka_ref[...])

    ri = lax.broadcasted_iota(I32, (n_rows, n_rows), 0)
    ci = lax.broadcasted_iota(I32, (n_rows, n_rows), 1)
    same_chunk = (ri // c) == (ci // c)
    cum = _dot_exact_rhs(jnp.logical_and(same_chunk, ri >= ci).astype(F32), lw)
    cum_last = _dot_exact_rhs(same_chunk.astype(F32), lw)
    r_t = r * jnp.exp(cum)
    a_t = -kk * jnp.exp(cum - lw)
    inv = jnp.exp(-cum)
    b_t = kk * a * inv
    k_t = kf * inv
    to_end = jnp.exp(cum_last - cum)
    b_e = kk * a * to_end
    k_e = kf * to_end
    g_end = jnp.exp(cum_last)

    hg = HEADS_PER_GROUP
    gw = hg * HEAD_DIM
    log_c = int(math.log2(c))
    t_idx = lax.broadcasted_iota(I32, (c, hg * c), 0)
    s_idx = lax.broadcasted_iota(I32, (c, hg * c), 1) & (c - 1)
    strict = s_idx < t_idx
    lower = s_idx <= t_idx
    eye = (s_idx == t_idx).astype(F32)
    rb = lax.broadcasted_iota(I32, (hg * c, gw), 0) >> log_c
    mask_kv = rb == (lax.broadcasted_iota(I32, (hg * c, gw), 1) >> int(math.log2(HEAD_DIM)))
    rs = lax.broadcasted_iota(I32, (hg * c, hg * c), 0) >> log_c
    mask_ss = rs == (lax.broadcasted_iota(I32, (hg * c, hg * c), 1) >> log_c)
    ng = N_HEADS // hg
    cat = functools.partial(jnp.concatenate, axis=0)
    units = [(slice(j * c, (j + 1) * c), slice(i * gw, (i + 1) * gw)) for j in range(n_chunks) for i in range(ng)]
    nu = len(units)

    ar = [_split2(cat([a_t[rs_, s], r_t[rs_, s]])) for rs_, s in units]
    ab = [_mm3(ar[n], _bd_parts(b_t[units[n]], mask_kv), NT) for n in range(nu)]
    ak = [_mm3(ar[n], _bd_parts(k_t[units[n]], mask_kv), NT) for n in range(nu)]
    l_ab = [jnp.where(strict, m[:c], 0.0) for m in ab]
    l_rb = [jnp.where(lower, m[c:], 0.0) for m in ab]
    l_ak = [jnp.where(strict, m[:c], 0.0) for m in ak]
    l_rk = [jnp.where(lower, m[c:], 0.0) for m in ak]
    def mm1(a, b_bd):
        return jnp.dot(a.astype(BF16), b_bd, preferred_element_type=F32)

    def bd1(x, mask):
        tiled = jnp.concatenate([x.astype(BF16)] * hg, axis=0)
        return jnp.where(mask, tiled, jnp.zeros_like(tiled))

    tinv = [eye + m for m in l_ab]
    pw = [mm1(m, bd1(m, mask_ss)) for m in l_ab]
    for _ in range(1, log_c - 1):
        res = [mm1(cat([tinv[n], pw[n]]), bd1(pw[n], mask_ss)) for n in range(nu)]
        tinv = [tinv[n] + res[n][:c] for n in range(nu)]
        pw = [m[c:] for m in res]
    tinv = [tinv[n] + mm1(tinv[n], bd1(pw[n], mask_ss)) for n in range(nu)]
    av = [_mm3(_split2(cat([l_ak[n], l_rk[n]])), _bd_parts(v[units[n]], mask_kv), NN) for n in range(nu)]
    ue = [_split2(cat([b_e[units[n]], k_e[units[n]]])) for n in range(nu)]

    def wide(fn, x):
        return [fn(x[:, :gw]), fn(x[:, gw:])]

    def bd1w(x):
        return jnp.concatenate(wide(lambda h_: bd1(h_, mask_kv), x), axis=1)

    def bd3w(x):
        parts = wide(lambda h_: _bd_parts(h_, mask_kv), x)
        return tuple(jnp.concatenate([parts[0][q], parts[1][q]], axis=1) for q in range(2))

    rhs = [jnp.concatenate([a_t[units[n]], av[n][:c]], axis=1) for n in range(nu)]
    x0 = [mm1(tinv[n], bd1w(rhs[n])) for n in range(nu)]
    resid = [rhs[n] - (x0[n] - _mm3(_split2(l_ab[n]), bd3w(x0[n]), NN)) for n in range(nu)]
    sol = [x0[n] + mm1(tinv[n], bd1w(resid[n])) for n in range(nu)]
    lift = [_mm3(_split2(l_rb[n]), bd3w(sol[n]), NN) for n in range(nu)]
    lhs_s = [_split2(cat([sol[n][:, :gw], r_t[units[n]] + lift[n][:, :gw]])) for n in range(nu)]
    u_loc = [sol[n][:, gw:] for n in range(nu)]
    o_loc = [av[n][c:] + lift[n][:, gw:] for n in range(nu)]

    zr = lax.broadcasted_iota(I32, (gw, gw), 0) >> int(math.log2(HEAD_DIM))
    zmask = zr == (lax.broadcasted_iota(I32, (gw, gw), 1) >> int(math.log2(HEAD_DIM)))
    z = [z_scr[i] for i in range(ng)]
    o_rows = []
    for j in range(n_chunks):
        o_grp = []
        for i in range(ng):
            n = j * ng + i
            rs_, s = units[n]
            sz = _mm3(lhs_s[n], _split2(z[i]), NT)
            u = sz[:c] + u_loc[n]
            o_grp.append(sz[c:] + o_loc[n])
            upd = _mm3(_split2(cat([u, v[rs_, s]])), ue[n], TN)
            z[i] = z[i] * g_end[j * c:j * c + 1, s] + jnp.where(zmask, upd, 0.0)
        o_rows.append(jnp.concatenate(o_grp, axis=1))
    for i in range(ng):
        z_scr[i] = z[i]

    o = cat(o_rows)
    inv_hd = 1.0 / HEAD_DIM
    dlt = o - _gsum(o, gm_ref) * inv_hd
    var = _gsum(dlt * dlt, gm_ref) * inv_hd
    on = dlt * lax.rsqrt(var + RWKV_GN_EPS) * lnw_ref[...] + lnb_ref[...]
    bonus = _gsum(r * kf * rk_ref[...], gm_ref) * v
    y_ref[0] = (on + bonus) * g

    @pl.when(t == nt - 1)
    def _():
        for h in range(N_HEADS):
            i, blk = head_block(h)
            st_ref[0, h] = z_scr[i, blk, blk]


def _rwkv(p, shift0, s0, prm, gmat, chunk, chunks_per_step):
    b, t, _ = p.shape
    row = lambda n: _const_spec((1, n))
    rows = chunk * chunks_per_step
    return pl.pallas_call(
        functools.partial(_rwkv_kernel, c=chunk),
        grid=(b, t // rows),
        in_specs=[pl.BlockSpec((1, rows, RWKV_COLS), lambda i, j: (i, j, 0)),
                  pl.BlockSpec((1, 1, RWKV_COLS), lambda i, j: (i, 0, 0)),
                  pl.BlockSpec((1, N_HEADS, HEAD_DIM, HEAD_DIM), lambda i, j: (i, 0, 0, 0)),
                  row(RWKV_COLS), row(WIDTH), _const_spec((DECAY_LORA, WIDTH)), row(WIDTH),
                  _const_spec((ICLR_LORA, WIDTH)), _const_spec((GATE_LORA, WIDTH)),
                  row(WIDTH), row(WIDTH), row(WIDTH), row(WIDTH), row(WIDTH), _const_spec((WIDTH, WIDTH))],
        out_specs=[pl.BlockSpec((1, rows, WIDTH), lambda i, j: (i, j, 0)),
                   pl.BlockSpec((1, N_HEADS, HEAD_DIM, HEAD_DIM), lambda i, j: (i, 0, 0, 0)),
                   pl.BlockSpec((1, 1, RWKV_COLS), lambda i, j: (i, 0, 0))],
        out_shape=[jax.ShapeDtypeStruct((b, t, WIDTH), F32),
                   jax.ShapeDtypeStruct((b, N_HEADS, HEAD_DIM, HEAD_DIM), F32),
                   jax.ShapeDtypeStruct((b, 1, RWKV_COLS), F32)],
        scratch_shapes=[pltpu.VMEM((N_HEADS // HEADS_PER_GROUP, HEADS_PER_GROUP * HEAD_DIM,
                                    HEADS_PER_GROUP * HEAD_DIM), F32),
                        pltpu.VMEM((1, RWKV_COLS), F32)],
        compiler_params=_cparams(("parallel", "arbitrary")),
        name="rwkv7_mix",
    )(p, shift0, s0, prm["mu"], prm["w0"], prm["wb"], prm["a0"], prm["ab"], prm["gb"], prm["kk"], prm["ka"],
      prm["rk"], prm["lnw"], prm["lnb"], gmat)


def _merge_kernel(x_ref, ya_ref, yb_ref, gate_ref, mod_ref, g2_ref, woa_ref, wob_ref, wo_ref, wrh_ref, wrl_ref,
                  *rest):
    x1_ref, h2_ref, lg_ref = rest[-3:]
    bb, tt, d = x_ref.shape
    m = bb * tt
    gate = gate_ref[...].reshape(m, GATE_COLS)
    merged = (gate[:, 0:d] * _dot(ya_ref[...].reshape(m, WIDTH), woa_ref[...])
              + gate[:, d:2 * d] * _dot(yb_ref[...].reshape(m, WIDTH), wob_ref[...]))
    x1 = x_ref[...] + mod_ref[:, 2:3, :] * _dot(merged, wo_ref[...]).reshape(bb, tt, d)
    x1_ref[...] = x1
    ms = jnp.mean(x1 * x1, axis=-1, keepdims=True)
    h2 = x1 * lax.rsqrt(ms + RMS_EPS) * g2_ref[...]
    h2 = (h2 * (1.0 + mod_ref[:, 4:5, :]) + mod_ref[:, 3:4, :]).reshape(m, d)
    h2_ref[...] = h2
    lg_ref[...] = _mm3((wrh_ref[...], wrl_ref[...]), _split2(h2), NT)


def _merge(x, ya, yb, gate, mod, w, n_total, row_offset, shared=None):
    b, t, d = x.shape
    bb, tt = _token_blocks(b, t)
    nt = t // tt
    m = bb * tt
    off = row_offset // m

    def tok(cols):
        return pl.BlockSpec((bb, tt, cols), lambda i, j: (i, j, 0))

    def flat(cols):
        return pl.BlockSpec((m, cols), lambda i, j: (off + i * nt + j, 0))

    in_specs = [tok(d), tok(WIDTH), tok(WIDTH), tok(GATE_COLS),
                pl.BlockSpec((bb, 6, d), lambda i, j: (i, 0, 0)),
                _const_spec((1, d)), _const_spec((WIDTH, d)), _const_spec((WIDTH, d)), _const_spec((d, d)),
                _const_spec((N_EXPERTS, d)), _const_spec((N_EXPERTS, d))]
    args = [x, ya, yb, gate, mod.reshape(b, 6, d), w["g2"], w["w_oa"], w["w_ob"], w["w_o"], w["wr_hi"], w["wr_lo"]]
    aliases = {}
    if shared is not None:
        aliases = {len(args): 1, len(args) + 1: 2}
        in_specs += [pl.BlockSpec(memory_space=pl.ANY), pl.BlockSpec(memory_space=pl.ANY)]
        args += list(shared)
    return pl.pallas_call(
        _merge_kernel,
        grid=(b // bb, nt),
        in_specs=in_specs,
        out_specs=[tok(d), flat(d), pl.BlockSpec((N_EXPERTS, m), lambda i, j: (0, off + i * nt + j))],
        out_shape=[jax.ShapeDtypeStruct((b, t, d), F32), jax.ShapeDtypeStruct((n_total, d), F32),
                   jax.ShapeDtypeStruct((N_EXPERTS, n_total), F32)],
        input_output_aliases=aliases,
        compiler_params=_cparams(("parallel", "parallel")),
        name="merge_norm2_router",
    )(*args)


def _route_kernel(lg_ref, bias_ref, idx_ref, wt_ref, rank_ref, cnt_ref, carry):
    @pl.when(pl.program_id(0) == 0)
    def _():
        carry[...] = jnp.zeros(carry.shape, F32)

    tm = lg_ref.shape[1]
    scores = _sigmoid(lg_ref[...])
    sel = scores + bias_ref[...]
    row = lax.broadcasted_iota(I32, (N_EXPERTS, tm), 0)
    neg_inf = -jnp.inf

    def first_argmax(vals, rows):
        mx = jnp.max(vals, axis=0, keepdims=True)
        return mx, jnp.min(jnp.where(vals == mx, rows, N_EXPERTS), axis=0, keepdims=True)

    gslices = [slice(g * EXPERTS_PER_GROUP, (g + 1) * EXPERTS_PER_GROUP) for g in range(N_GROUPS)]
    gs = []
    row_g = lax.broadcasted_iota(I32, (EXPERTS_PER_GROUP, tm), 0)
    for sl in gslices:
        m1, i1 = first_argmax(sel[sl], row_g)
        m2 = jnp.max(jnp.where(row_g == i1, neg_inf, sel[sl]), axis=0, keepdims=True)
        gs.append(m1 + m2)
    kept = []
    for g in range(N_GROUPS):
        beaten = jnp.zeros((1, tm), I32)
        for o in range(N_GROUPS):
            if o != g:
                wins = (gs[o] >= gs[g]) if o < g else (gs[o] > gs[g])
                beaten = beaten + wins.astype(I32)
        kept.append(jnp.where(beaten < TOPK_GROUPS, sel[gslices[g]], neg_inf))
    cur = jnp.concatenate(kept, axis=0)

    idxs, ws = [], []
    picked = jnp.zeros((N_EXPERTS, tm), F32)
    for _ in range(TOP_K):
        _, ik = first_argmax(cur, row)
        hit = row == ik
        idxs.append(ik)
        ws.append(jnp.sum(jnp.where(hit, scores, 0.0), axis=0, keepdims=True))
        cur = jnp.where(hit, neg_inf, cur)
        picked = jnp.where(hit, 1.0, picked)
    wsum = ws[0]
    for k in range(1, TOP_K):
        wsum = wsum + ws[k]

    r = lax.broadcasted_iota(I32, (tm, tm), 0)
    c = lax.broadcasted_iota(I32, (tm, tm), 1)
    before = jnp.dot(picked.astype(BF16), (r < c).astype(BF16), preferred_element_type=F32) + carry[...]
    carry[...] = carry[...] + jnp.sum(picked, axis=1, keepdims=True)
    cnt_ref[...] = carry[...]

    kk = lax.broadcasted_iota(I32, (TOP_K, tm), 0)
    idx_o = jnp.zeros((TOP_K, tm), I32)
    wt_o = jnp.zeros((TOP_K, tm), F32)
    rank_o = jnp.zeros((TOP_K, tm), F32)
    for k in range(TOP_K):
        rk = jnp.sum(jnp.where(row == idxs[k], before, 0.0), axis=0, keepdims=True)
        idx_o = jnp.where(kk == k, idxs[k], idx_o)
        wt_o = jnp.where(kk == k, ws[k] / wsum * ROUTED_SCALE, wt_o)
        rank_o = jnp.where(kk == k, rk, rank_o)
    idx_ref[...] = idx_o
    wt_ref[...] = wt_o
    rank_ref[...] = rank_o.astype(I32)


def _route(logits_t, bias_col, tm):
    n = logits_t.shape[1]
    tokk = pl.BlockSpec((TOP_K, tm), lambda i: (0, i))
    return pl.pallas_call(
        _route_kernel,
        grid=(n // tm,),
        in_specs=[pl.BlockSpec((N_EXPERTS, tm), lambda i: (0, i)), _const_spec((N_EXPERTS, 1))],
        out_specs=[tokk, tokk, tokk, _const_spec((N_EXPERTS, 1))],
        out_shape=[jax.ShapeDtypeStruct((TOP_K, n), I32), jax.ShapeDtypeStruct((TOP_K, n), F32),
                   jax.ShapeDtypeStruct((TOP_K, n), I32), jax.ShapeDtypeStruct((N_EXPERTS, 1), F32)],
        scratch_shapes=[pltpu.VMEM((N_EXPERTS, 1), F32)],
        compiler_params=_cparams(("arbitrary",)),
        name="route_topk",
    )(logits_t, bias_col)


def _plan_kernel(cnt_ref, start_ref, be_ref, valid_ref, nu_ref, *, blk):
    cnt = cnt_ref[...]
    padded = jnp.ceil(cnt * (1.0 / blk)) * blk
    e_r = lax.broadcasted_iota(I32, (N_EXPERTS, N_EXPERTS), 0)
    e_c = lax.broadcasted_iota(I32, (N_EXPERTS, N_EXPERTS), 1)
    incl = (e_r <= e_c).astype(BF16)
    ph, pm, plo = _split3(jnp.broadcast_to(padded, (8, N_EXPERTS)))
    d2 = functools.partial(jnp.dot, preferred_element_type=F32)
    pad_end = (d2(ph, incl) + (d2(pm, incl) + d2(plo, incl)))[0:1, :]
    pad_start = pad_end - padded
    start_ref[...] = pad_start.astype(I32)
    total = jnp.max(pad_end, axis=-1, keepdims=True)
    nu_ref[...] = jnp.broadcast_to(total * (1.0 / blk), (1, N_EXPERTS)).astype(I32)
    nb = be_ref.shape[0]
    first = (lax.broadcasted_iota(I32, (nb, N_EXPERTS), 0) * blk).astype(F32)
    lane = lax.broadcasted_iota(I32, (nb, N_EXPERTS), 1)
    inside = jnp.logical_and(pad_start <= first, first < pad_end)
    be_ref[...] = jnp.sum(jnp.where(inside, lane, 0), axis=-1, keepdims=True)
    rows = jnp.minimum(pad_start + cnt - first, float(blk))
    valid_ref[...] = jnp.sum(jnp.where(inside, rows, 0.0), axis=-1, keepdims=True).astype(I32)


def _plan(counts, n_blocks, blk):
    return pl.pallas_call(
        functools.partial(_plan_kernel, blk=blk),
        out_shape=[jax.ShapeDtypeStruct((1, N_EXPERTS), I32), jax.ShapeDtypeStruct((n_blocks, 1), I32),
                   jax.ShapeDtypeStruct((n_blocks, 1), I32), jax.ShapeDtypeStruct((1, N_EXPERTS), I32)],
        compiler_params=pltpu.CompilerParams(vmem_limit_bytes=VMEM_LIMIT),
        name="dispatch_plan",
    )(counts)


def _dest_kernel(idx_ref, rank_ref, start_ref, dest_ref):
    tm = idx_ref.shape[1]
    row = lax.broadcasted_iota(I32, (N_EXPERTS, tm), 0)
    kk = lax.broadcasted_iota(I32, (TOP_K, tm), 0)
    idx = idx_ref[...]
    base = jnp.zeros((TOP_K, tm), I32)
    for k in range(TOP_K):
        bk = jnp.sum(jnp.where(row == idx[k:k + 1, :], start_ref[...], 0), axis=0, keepdims=True)
        base = jnp.where(kk == k, bk, base)
    dest_ref[...] = base + rank_ref[...]


def _dest(idx, rank, pad_start_col, tm):
    n = idx.shape[1]
    tokk = pl.BlockSpec((TOP_K, tm), lambda i: (0, i))
    return pl.pallas_call(
        _dest_kernel,
        grid=(n // tm,),
        in_specs=[tokk, tokk, _const_spec((N_EXPERTS, 1))],
        out_specs=tokk,
        out_shape=jax.ShapeDtypeStruct((TOP_K, n), I32),
        compiler_params=_cparams(("parallel",)),
        name="dispatch_dest",
    )(idx, rank, pad_start_col)


def _row_copy(src_ref, src_row, dst_ref, dst_row, sem):
    return pltpu.make_async_copy(src_ref.at[pl.ds(src_row, 1)], dst_ref.at[pl.ds(dst_row, 1)], sem)


def _dispatch_kernel(dest_ref, h2_ref, xs_ref, sem):
    tm = h2_ref.shape[0]

    def issue(r, carry):
        for k in range(TOP_K):
            _row_copy(h2_ref, r, xs_ref, dest_ref[k, r], sem).start()
        return carry

    lax.fori_loop(0, tm, issue, 0)

    def drain(r, carry):
        for k in range(TOP_K):
            _row_copy(h2_ref, 0, xs_ref, 0, sem).wait()
        return carry

    lax.fori_loop(0, tm, drain, 0)


def _dispatch(h2, dest_t, n_slots, tm):
    n, d = h2.shape
    return pl.pallas_call(
        _dispatch_kernel,
        grid=(n // tm,),
        in_specs=[pl.BlockSpec((TOP_K, tm), lambda i: (0, i), memory_space=pltpu.SMEM),
                  pl.BlockSpec((tm, d), lambda i: (i, 0))],
        out_specs=pl.BlockSpec(memory_space=pl.ANY),
        out_shape=jax.ShapeDtypeStruct((n_slots, d), F32),
        scratch_shapes=[pltpu.SemaphoreType.DMA(())],
        compiler_params=_cparams(("arbitrary",)),
        name="moe_dispatch",
    )(dest_t, h2)


def _expert_kernel(be_ref, valid_ref, nu_ref, x_ref, wg_ref, wu_ref, wd_ref, y_ref, wg_b, wu_b, wd_b):
    i = pl.program_id(0)
    nv = valid_ref[i]
    new_expert = jnp.logical_or(i == 0, be_ref[i] != be_ref[jnp.maximum(i - 1, 0)])

    @pl.when(jnp.logical_and(nv > 0, new_expert))
    def _():
        wg_b[...] = wg_ref[0].astype(BF16)
        wu_b[...] = wu_ref[0].astype(BF16)
        wd_b[...] = wd_ref[0].astype(BF16)

    @pl.when(nv > 0)
    def _():
        blk = x_ref.shape[0]
        rows = lax.broadcasted_iota(I32, (blk, 1), 0)
        x = jnp.where(rows < nv, x_ref[...], 0.0).astype(BF16)
        hg = jnp.dot(x, wg_b[...], preferred_element_type=F32)
        hu = jnp.dot(x, wu_b[...], preferred_element_type=F32)
        y_ref[...] = jnp.dot((_silu(hg) * hu).astype(BF16), wd_b[...], preferred_element_type=F32)


def _experts(xs, block_e, valid, n_used, w_eg, w_eu, w_ed, blk):
    n_slots, d = xs.shape
    n_blocks = n_slots // blk

    def row_blk(i, be, valid, nu):
        return (jnp.minimum(i, nu[0] - 1), 0)

    def w_blk(i, be, valid, nu):
        return (be[i], 0, 0)

    return pl.pallas_call(
        _expert_kernel,
        grid_spec=pltpu.PrefetchScalarGridSpec(
            num_scalar_prefetch=3,
            grid=(n_blocks,),
            in_specs=[pl.BlockSpec((blk, d), row_blk),
                      pl.BlockSpec((1, d, D_EXPERT), w_blk), pl.BlockSpec((1, d, D_EXPERT), w_blk),
                      pl.BlockSpec((1, D_EXPERT, d), w_blk)],
            out_specs=pl.BlockSpec((blk, d), row_blk),
            scratch_shapes=[pltpu.VMEM((d, D_EXPERT), BF16), pltpu.VMEM((d, D_EXPERT), BF16),
                            pltpu.VMEM((D_EXPERT, d), BF16)]),
        out_shape=jax.ShapeDtypeStruct((n_slots, d), F32),
        compiler_params=_cparams(("arbitrary",)),
        name="moe_experts",
    )(block_e, valid, n_used, xs, w_eg, w_eu, w_ed)


def _final_kernel(dest_ref, x1_ref, h2_ref, wt_ref, mod_ref, wsg_ref, wsu_ref, wsd_ref, ys_ref, o_ref, ybuf, sem):
    bb, tt, d = x1_ref.shape
    m = bb * tt

    def issue(r, carry):
        for k in range(TOP_K):
            _row_copy(ys_ref, dest_ref[k, r], ybuf.at[k], r, sem).start()
        return carry

    lax.fori_loop(0, m, issue, 0)

    hb = h2_ref[...].astype(BF16)
    hg = jnp.dot(hb, wsg_ref[...], preferred_element_type=F32)
    hu = jnp.dot(hb, wsu_ref[...], preferred_element_type=F32)
    ffn = _dot(_silu(hg) * hu, wsd_ref[...])

    def drain(r, carry):
        for k in range(TOP_K):
            _row_copy(ys_ref, 0, ybuf.at[k], 0, sem).wait()
        return carry

    lax.fori_loop(0, m, drain, 0)

    wt = wt_ref[...]
    for k in range(TOP_K):
        ffn = ffn + wt[:, k:k + 1] * ybuf[k]
    o_ref[...] = x1_ref[...] + mod_ref[:, 5:6, :] * ffn.reshape(bb, tt, d)


def _final(x1, h2_all, wts_all, dest_t, ys, mod, w, row_offset):
    b, t, d = x1.shape
    bb, tt = _token_blocks(b, t)
    nt = t // tt
    m = bb * tt
    off = row_offset // m

    def flat_idx(i, j):
        return off + i * nt + j

    return pl.pallas_call(
        _final_kernel,
        grid=(b // bb, nt),
        in_specs=[pl.BlockSpec((TOP_K, m), lambda i, j: (0, flat_idx(i, j)), memory_space=pltpu.SMEM),
                  pl.BlockSpec((bb, tt, d), lambda i, j: (i, j, 0)),
                  pl.BlockSpec((m, d), lambda i, j: (flat_idx(i, j), 0)),
                  pl.BlockSpec((m, TOP_K), lambda i, j: (flat_idx(i, j), 0)),
                  pl.BlockSpec((bb, 6, d), lambda i, j: (i, 0, 0)),
                  _const_spec((d, D_EXPERT)), _const_spec((d, D_EXPERT)), _const_spec((D_EXPERT, d)),
                  pl.BlockSpec(memory_space=pl.ANY)],
        out_specs=pl.BlockSpec((bb, tt, d), lambda i, j: (i, j, 0)),
        out_shape=jax.ShapeDtypeStruct((b, t, d), F32),
        scratch_shapes=[pltpu.VMEM((TOP_K, m, d), F32), pltpu.SemaphoreType.DMA(())],
        compiler_params=_cparams(("arbitrary", "arbitrary")),
        name="moe_combine_final",
    )(dest_t, x1, h2_all, wts_all, mod.reshape(b, 6, d), w["w_sg"], w["w_su"], w["w_sd"], ys)


def _moe_routed(h2_all, logits_all, w, blk=256, tm=256):
    n = h2_all.shape[0]
    n_blocks = (n * TOP_K + N_EXPERTS * (blk - 1)) // blk + 1
    n_blocks = (n_blocks + 7) // 8 * 8
    idx, wts_t, rank, counts = _route(logits_all, w["router_bias"], tm)
    pad_start, block_e, valid, n_used = _plan(counts.reshape(1, N_EXPERTS), n_blocks, blk)
    block_e = block_e.reshape(n_blocks)
    valid = valid.reshape(n_blocks)
    n_used = n_used[0, 0:1]
    dest_t = _dest(idx, rank, pad_start.reshape(N_EXPERTS, 1), tm)
    xs = _dispatch(h2_all, dest_t, n_blocks * blk, tm)
    ys = _experts(xs, block_e, valid, n_used, w["w_eg"], w["w_eu"], w["w_ed"], blk)
    return ys, dest_t, jnp.transpose(wts_t)


def _prep(raw):
    p = {k: v[0] for k, v in raw.items()}
    w_in = p["w_in"]
    o_fox = RWKV_COLS
    o_fl = o_fox + FOX_MAIN_COLS
    o_gate = o_fl + N_HEADS
    row = lambda a: a.reshape(1, -1)
    return dict(
        w_ada=p["w_ada"], b_ada=p["b_ada"],
        g1=row(p["norm1_g"]), g2=row(p["norm2_g"]),
        wr=w_in[:, :o_fox].astype(BF16),
        wf=w_in[:, o_fox:o_fl].astype(BF16),
        wfl=jnp.pad(w_in[:, o_fl:o_gate], ((0, 0), (0, LANES - N_HEADS))).astype(BF16),
        wg=w_in[:, o_gate:].astype(BF16),
        qn=row(jnp.tile(p["fox_q_norm"], N_HEADS)), kn=row(jnp.tile(p["fox_k_norm"], N_HEADS)),
        fb=row(p["fox_f_bias"]),
        gmat=_group_ones(),
        rwkv=dict(mu=row(p["rwkv_mu"]), w0=row(p["rwkv_w0"]), wb=p["rwkv_w_lora_b"], a0=row(p["rwkv_a0"]),
                  ab=p["rwkv_a_lora_b"], gb=p["rwkv_g_lora_b"], kk=row(p["rwkv_k_k"]), ka=row(p["rwkv_k_a"]),
                  rk=row(p["rwkv_r_k"]), lnw=row(p["rwkv_ln_w"]), lnb=row(p["rwkv_ln_b"])),
        w_oa=p["w_out_rwkv"].astype(BF16), w_ob=p["w_out_fox"].astype(BF16), w_o=p["w_out"].astype(BF16),
        wr_hi=p["w_router"].T.astype(BF16),
        wr_lo=(p["w_router"] - p["w_router"].astype(BF16).astype(F32)).T.astype(BF16),
        router_bias=p["router_bias"].reshape(N_EXPERTS, 1),
        w_eg=p["w_exp_gate"], w_eu=p["w_exp_up"], w_ed=p["w_exp_down"],
        w_sg=p["w_sh_gate"].astype(BF16), w_su=p["w_sh_up"].astype(BF16), w_sd=p["w_sh_down"].astype(BF16),
    )


def _token_blocks(b, t):
    if t >= 256:
        return 1, 256
    bb = max(1, min(b, 256 // t))
    while b % bb:
        bb -= 1
    return bb, t


def _mix_path(x, mod, shift0, wkv0, past_k, past_v, past_logf, w):
    b, t, d = x.shape
    bb, tt = _token_blocks(b, t)
    n_past = past_k.shape[1]
    if n_past:
        f_past = _past_cumsum(past_logf)
        init = f_past[:, n_past - 1:n_past, :]
        past = (past_k, past_v, jnp.swapaxes(f_past, 1, 2))
    else:
        init = jnp.zeros((b, 1, N_HEADS), F32)
        past = None
    pr, q, k, v, sg, logf, f_new, gate = _inproj(x, mod.reshape(b, 6, d), w["g1"], w["wr"], w["wf"], w["wfl"],
                                                 w["wg"], w["qn"], w["kn"], w["fb"], w["gmat"], init, bb, tt)
    y_fox = _fox_attention(q, f_new, sg, k, v, jnp.swapaxes(f_new, 1, 2), past=past, tq=min(t, 512),
                           tk_past=min(max(n_past, 1), 512))
    chunk = min(t, RWKV_CHUNK)
    y_rwkv, wkv_new, shift_new = _rwkv(pr, shift0.reshape(b, 1, RWKV_COLS), wkv0, w["rwkv"], w["gmat"],
                                       chunk, max(1, min(RWKV_CHUNKS_PER_STEP, t // chunk)))
    return y_rwkv, y_fox, gate, wkv_new, shift_new, k, v, logf


def _layer(paths, w):
    n_b = [p[0].shape[0] for p in paths]
    mod_all = _ada(jnp.concatenate([p[1] for p in paths], axis=0), w["w_ada"], w["b_ada"])
    mods, o = [], 0
    for nb in n_b:
        mods.append(mod_all[o:o + nb])
        o += nb
    n_total = sum(p[0].shape[0] * p[0].shape[1] for p in paths)
    mixed, x1s = [], []
    shared, row = None, 0
    for (x, _, shift0, wkv0, pk, pv, plf), mod in zip(paths, mods):
        ya, yb, gate, wkv_new, shift_new, k, v, logf = _mix_path(x, mod, shift0, wkv0, pk, pv, plf, w)
        x1, h2_all, lg_all = _merge(x, ya, yb, gate, mod, w, n_total, row, shared)
        shared = (h2_all, lg_all)
        row += x.shape[0] * x.shape[1]
        mixed.append((wkv_new, shift_new, k, v, logf))
        x1s.append(x1)
    ys, dest_t, wts = _moe_routed(h2_all, lg_all, w)
    outs, row = [], 0
    for x1, mod, st in zip(x1s, mods, mixed):
        y = _final(x1, h2_all, wts, dest_t, ys, mod, w, row)
        row += x1.shape[0] * x1.shape[1]
        outs.append((y,) + st)
    return outs


def kernel(x_prompt, x_sample, c_prompt, c_sample, state_rwkv_wkv, state_rwkv_shift, cache_fox_k, cache_fox_v,
           cache_fox_logf, w_ada, b_ada, norm1_g, norm2_g, w_in, rwkv_mu, rwkv_w0, rwkv_w_lora_b, rwkv_a0,
           rwkv_a_lora_b, rwkv_g_lora_b, rwkv_k_k, rwkv_k_a, rwkv_r_k, rwkv_ln_w, rwkv_ln_b, fox_q_norm,
           fox_k_norm, fox_f_bias, w_out_rwkv, w_out_fox, w_out, w_router, router_bias, w_exp_gate, w_exp_up,
           w_exp_down, w_sh_gate, w_sh_up, w_sh_down):
    raw = dict(w_ada=w_ada, b_ada=b_ada, norm1_g=norm1_g, norm2_g=norm2_g, w_in=w_in, rwkv_mu=rwkv_mu,
               rwkv_w0=rwkv_w0, rwkv_w_lora_b=rwkv_w_lora_b, rwkv_a0=rwkv_a0, rwkv_a_lora_b=rwkv_a_lora_b,
               rwkv_g_lora_b=rwkv_g_lora_b, rwkv_k_k=rwkv_k_k, rwkv_k_a=rwkv_k_a, rwkv_r_k=rwkv_r_k,
               rwkv_ln_w=rwkv_ln_w, rwkv_ln_b=rwkv_ln_b, fox_q_norm=fox_q_norm, fox_k_norm=fox_k_norm,
               fox_f_bias=fox_f_bias, w_out_rwkv=w_out_rwkv, w_out_fox=w_out_fox, w_out=w_out,
               w_router=w_router, router_bias=router_bias, w_exp_gate=w_exp_gate, w_exp_up=w_exp_up,
               w_exp_down=w_exp_down, w_sh_gate=w_sh_gate, w_sh_up=w_sh_up, w_sh_down=w_sh_down)
    assert w_in.shape[0] == 1, "single-layer stack"
    w = _prep(raw)
    bp, tp, _ = x_prompt.shape
    bs, ts, _ = x_sample.shape
    n_past = cache_fox_k.shape[2]
    prompt = (x_prompt, c_prompt, jnp.zeros((bp, RWKV_COLS), F32),
              jnp.zeros((bp, N_HEADS, HEAD_DIM, HEAD_DIM), F32),
              jnp.zeros((bp, 0, WIDTH), F32), jnp.zeros((bp, 0, WIDTH), F32), jnp.zeros((bp, 0, N_HEADS), F32))
    sample = (x_sample, c_sample, state_rwkv_shift[0], state_rwkv_wkv[0],
              cache_fox_k[0].reshape(bs, n_past, WIDTH), cache_fox_v[0].reshape(bs, n_past, WIDTH),
              cache_fox_logf[0])
    (yp, wkv_p, sh_p, k_p, v_p, lf_p), (ysm, wkv_s, sh_s, k_s, v_s, lf_s) = _layer([prompt, sample], w)

    def heads(a):
        return a.reshape((1,) + a.shape[:2] + (N_HEADS, HEAD_DIM))

    return (yp, ysm,
            wkv_p[None], sh_p.reshape(1, bp, RWKV_COLS), heads(k_p), heads(v_p), lf_p[None],
            wkv_s[None], sh_s.reshape(1, bs, RWKV_COLS), heads(k_s), heads(v_s), lf_s[None])
```

```python
import functools
import math

import jax
import jax.numpy as jnp
from jax import lax
from jax.experimental import pallas as pl
from jax.experimental.pallas import tpu as pltpu

F32 = jnp.float32
BF16 = jnp.bfloat16
I32 = jnp.int32

D_MODEL = 1024
N_HEADS = 8
HEAD_DIM = 64
WIDTH = N_HEADS * HEAD_DIM
HEADS_PER_GROUP = 4
RWKV_CHUNK = 64
RWKV_CHUNKS_PER_STEP = 4
COMBINE_UNROLL = 4
DECAY_LORA = 64
ICLR_LORA = 64
GATE_LORA = 128
RWKV_COLS = 3 * WIDTH + DECAY_LORA + ICLR_LORA + GATE_LORA
FOX_MAIN_COLS = 4 * WIDTH
GATE_COLS = 2 * D_MODEL
RWKV_GN_EPS = HEAD_DIM * 1e-5
L2_EPS = 1e-12
RMS_EPS = 1e-6
N_EXPERTS = 256
TOP_K = 8
N_GROUPS = 8
TOPK_GROUPS = 4
EXPERTS_PER_GROUP = N_EXPERTS // N_GROUPS
D_EXPERT = 256
ROUTED_SCALE = 2.5

LANES = 128
ROW_TILE_SUBLANES = D_MODEL // LANES
VMEM_LIMIT = 56 * 1024 * 1024
NEG_BIG = -1e30

NN = (((1,), (0,)), ((), ()))
NT = (((1,), (1,)), ((), ()))
TN = (((0,), (0,)), ((), ()))


def _cparams(sem):
    return pltpu.CompilerParams(dimension_semantics=sem, vmem_limit_bytes=VMEM_LIMIT)


def _dot(a, b, dims=NN):
    return lax.dot_general(a.astype(BF16), b.astype(BF16), dims, preferred_element_type=F32)


def _split2(a):
    hi = a.astype(BF16)
    lo = (a - hi.astype(F32)).astype(BF16)
    return hi, lo


def _split3(a):
    hi = a.astype(BF16)
    r1 = a - hi.astype(F32)
    mid = r1.astype(BF16)
    lo = (r1 - mid.astype(F32)).astype(BF16)
    return hi, mid, lo


def _dot3(a, b, dims=NN):
    ah, al = _split2(a)
    bh, bl = _split2(b)
    d = functools.partial(lax.dot_general, dimension_numbers=dims, preferred_element_type=F32)
    return d(ah, bh) + (d(ah, bl) + d(al, bh))


def _mm3(a, b, dims):
    d = functools.partial(lax.dot_general, dimension_numbers=dims, preferred_element_type=F32)
    return d(a[0], b[0]) + (d(a[0], b[1]) + d(a[1], b[0]))


def _bd_parts(x, mask):
    out = []
    for part in _split2(x):
        tiled = jnp.concatenate([part] * HEADS_PER_GROUP, axis=0)
        out.append(jnp.where(mask, tiled, jnp.zeros_like(tiled)))
    return tuple(out)


def _dot_exact_rhs(a_exact, b, dims=NN):
    ab = a_exact.astype(BF16)
    bh, bm, bl = _split3(b)
    d = functools.partial(lax.dot_general, dimension_numbers=dims, preferred_element_type=F32)
    return d(ab, bh) + (d(ab, bm) + d(ab, bl))


def _gsum(x, g_ref):
    hi, mid, lo = _split3(x)
    g = g_ref[...]
    d = functools.partial(jnp.dot, preferred_element_type=F32)
    return d(hi, g) + (d(mid, g) + d(lo, g))


def _sigmoid(x):
    return 1.0 / (1.0 + jnp.exp(-x))


def _softplus(x):
    return jnp.maximum(x, 0.0) + jnp.log1p(jnp.exp(-jnp.abs(x)))


def _silu(x):
    return x * _sigmoid(x)


def _to_row_tiles(ref, x):
    for j in range(ROW_TILE_SUBLANES):
        ref[:, j, :] = x[:, j * LANES:(j + 1) * LANES]


def _from_row_tiles(ref):
    return jnp.concatenate([ref[:, j, :] for j in range(ROW_TILE_SUBLANES)], axis=1)


def _group_ones():
    h = jnp.arange(WIDTH, dtype=I32) // HEAD_DIM
    return (h[:, None] == h[None, :]).astype(BF16)


def _ada_kernel(c_ref, w_ref, b_ref, o_ref):
    o_ref[...] = _dot(_silu(c_ref[...]), w_ref[...]) + b_ref[...]


def _ada(c, w_ada, b_ada):
    nb = c.shape[0]
    n_out = w_ada.shape[1]
    blk = D_MODEL
    return pl.pallas_call(
        _ada_kernel,
        grid=(n_out // blk,),
        in_specs=[pl.BlockSpec((nb, D_MODEL), lambda j: (0, 0)),
                  pl.BlockSpec((D_MODEL, blk), lambda j: (0, j)),
                  pl.BlockSpec((1, blk), lambda j: (0, j))],
        out_specs=pl.BlockSpec((nb, blk), lambda j: (0, j)),
        out_shape=jax.ShapeDtypeStruct((nb, n_out), F32),
        compiler_params=_cparams(("parallel",)),
        name="ada_mod",
    )(c, w_ada, b_ada.reshape(1, n_out))


def _inproj_kernel(x_ref, mod_ref, g1_ref, wr_ref, wf_ref, wfl_ref, wg_ref, qn_ref, kn_ref, fb_ref, gm_ref, f0_ref,
                   pr_ref, q_ref, k_ref, v_ref, sg_ref, lf_ref, cf_ref, gate_ref, carry):
    bb, tt, d = x_ref.shape
    m = bb * tt
    x = x_ref[...]
    ms = jnp.mean(x * x, axis=-1, keepdims=True)
    h = x * lax.rsqrt(ms + RMS_EPS) * g1_ref[...]
    h = h * (1.0 + mod_ref[:, 1:2, :]) + mod_ref[:, 0:1, :]
    hb = h.reshape(m, d).astype(BF16)

    pr_ref[...] = jnp.dot(hb, wr_ref[...], preferred_element_type=F32).reshape(bb, tt, RWKV_COLS)

    f = jnp.dot(hb, wf_ref[...], preferred_element_type=F32)
    q = f[:, 0:WIDTH]
    k = f[:, WIDTH:2 * WIDTH]
    v = f[:, 2 * WIDTH:3 * WIDTH]
    og = f[:, 3 * WIDTH:4 * WIDTH]
    inv_hd = 1.0 / HEAD_DIM
    q = q * lax.rsqrt(_gsum(q * q, gm_ref) * inv_hd + RMS_EPS) * qn_ref[...]
    k = k * lax.rsqrt(_gsum(k * k, gm_ref) * inv_hd + RMS_EPS) * kn_ref[...]
    q_ref[...] = (q * (HEAD_DIM ** -0.5)).astype(BF16).reshape(bb, tt, WIDTH)
    k_ref[...] = k.reshape(bb, tt, WIDTH)
    v_ref[...] = v.reshape(bb, tt, WIDTH)
    sg_ref[...] = _sigmoid(og).reshape(bb, tt, WIDTH)

    fl = jnp.dot(hb, wfl_ref[...], preferred_element_type=F32)[:, 0:N_HEADS] + fb_ref[...]
    lf = -_softplus(-fl)
    lf_ref[...] = lf.reshape(bb, tt, N_HEADS)

    @pl.when(pl.program_id(1) == 0)
    def _():
        carry[...] = f0_ref[...]

    r = lax.broadcasted_iota(I32, (m, m), 0)
    c = lax.broadcasted_iota(I32, (m, m), 1)
    tri = jnp.logical_and(r // tt == c // tt, r >= c).astype(F32)
    cf = _dot_exact_rhs(tri, lf).reshape(bb, tt, N_HEADS) + carry[...]
    cf_ref[...] = cf
    carry[...] = cf[:, tt - 1:tt, :]

    gate_ref[...] =_sigmoid(jnp.dot(hb, wg_ref[...], preferred_element_type=F32)).reshape(bb, tt, GATE_COLS)


def _const_spec(shape):
    nd = len(shape)
    return pl.BlockSpec(shape, lambda *_: (0,) * nd)


def _inproj(x, mod, g1, wr, wf, wfl, wg, qn, kn, fb, gmat, f0, bb, tt):
    b, t, d = x.shape
    grid = (b // bb, t // tt)

    def tok(cols):
        return pl.BlockSpec((bb, tt, cols), lambda i, j: (i, j, 0))

    out_cols = [(RWKV_COLS, F32), (WIDTH, BF16), (WIDTH, F32), (WIDTH, F32), (WIDTH, F32), (N_HEADS, F32),
                (N_HEADS, F32), (GATE_COLS, F32)]
    return pl.pallas_call(
        _inproj_kernel,
        grid=grid,
        in_specs=[tok(d),
                  pl.BlockSpec((bb, 6, d), lambda i, j: (i, 0, 0)),
                  _const_spec((1, d)),
                  _const_spec(wr.shape), _const_spec(wf.shape), _const_spec(wfl.shape), _const_spec(wg.shape),
                  _const_spec((1, WIDTH)), _const_spec((1, WIDTH)), _const_spec((1, N_HEADS)),
                  _const_spec((WIDTH, WIDTH)),
                  pl.BlockSpec((bb, 1, N_HEADS), lambda i, j: (i, 0, 0))],
        out_specs=[tok(c) for c, _ in out_cols],
        out_shape=[jax.ShapeDtypeStruct((b, t, c), dt) for c, dt in out_cols],
        scratch_shapes=[pltpu.VMEM((bb, 1, N_HEADS), F32)],
        compiler_params=_cparams(("parallel", "arbitrary")),
        name="norm1_inproj",
    )(x, mod, g1, wr, wf, wfl, wg, qn, kn, fb, gmat, f0)


def _past_cumsum_kernel(x_ref, o_ref):
    x = x_ref[0]
    rows = x.shape[0]
    li = lax.broadcasted_iota(I32, (LANES, LANES), 0)
    lj = lax.broadcasted_iota(I32, (LANES, LANES), 1)
    same_head = (li % N_HEADS) == (lj % N_HEADS)
    within = jnp.logical_and(same_head, li // N_HEADS <= lj // N_HEADS).astype(BF16)
    xh, xm, xl = _split3(x)
    d2 = functools.partial(jnp.dot, preferred_element_type=F32)
    in_row = d2(xh, within) + (d2(xm, within) + d2(xl, within))
    sh = same_head.astype(BF16)
    row_tot = d2(xh, sh) + (d2(xm, sh) + d2(xl, sh))
    ri = lax.broadcasted_iota(I32, (rows, rows), 0)
    ci = lax.broadcasted_iota(I32, (rows, rows), 1)
    o_ref[0] = in_row + _dot_exact_rhs((ri > ci).astype(F32), row_tot)


def _past_cumsum(past_logf):
    b, p, h = past_logf.shape
    rows = p * h // LANES
    flat = past_logf.reshape(b, rows, LANES)
    out = pl.pallas_call(
        _past_cumsum_kernel,
        grid=(b,),
        in_specs=[pl.BlockSpec((1, rows, LANES), lambda i: (i, 0, 0))],
        out_specs=pl.BlockSpec((1, rows, LANES), lambda i: (i, 0, 0)),
        out_shape=jax.ShapeDtypeStruct((b, rows, LANES), F32),
        compiler_params=_cparams(("parallel",)),
        name="cache_logf_cumsum",
    )(flat)
    return out.reshape(b, p, h)


def _fox_kernel(*refs, n_past_blocks, tq):
    if n_past_blocks:
        (q_ref, fq_ref, sg_ref, kp_ref, vp_ref, fkp_ref, kn_ref, vn_ref, fkn_ref,
         o_ref, m_scr, l_scr, acc_scr) = refs
    else:
        q_ref, fq_ref, sg_ref, kn_ref, vn_ref, fkn_ref, o_ref, m_scr, l_scr, acc_scr = refs
    qi = pl.program_id(1)
    ki = pl.program_id(2)
    nk = pl.num_programs(2)

    @pl.when(ki == 0)
    def _():
        m_scr[...] = jnp.full(m_scr.shape, NEG_BIG, F32)
        l_scr[...] = jnp.zeros(l_scr.shape, F32)
        acc_scr[...] = jnp.zeros(acc_scr.shape, F32)

    lane_a = lax.broadcasted_iota(I32, (tq, LANES), 1) < HEAD_DIM

    def step(k_ref, v_ref, fk_ref, diag):
        tk = k_ref.shape[1]
        if diag:
            rq = lax.broadcasted_iota(I32, (tq, tk), 0)
            ck = lax.broadcasted_iota(I32, (tq, tk), 1)
            visible = ck <= rq
        fq_all = fq_ref[0]
        pairs = range(N_HEADS // 2)
        cols = [slice(j * LANES, (j + 1) * LANES) for j in pairs]
        scores = []
        for j in pairs:
            qj = q_ref[0, :, cols[j]]
            kb = k_ref[0, :, cols[j]].astype(BF16)
            for hh in range(2):
                h = 2 * j + hh
                qm = jnp.where(lane_a if hh == 0 else jnp.logical_not(lane_a), qj, jnp.zeros_like(qj))
                s = lax.dot_general(qm, kb, NT, preferred_element_type=F32)
                s = s + fq_all[:, h:h + 1] - fk_ref[0, h:h + 1, :]
                if diag:
                    s = jnp.where(visible, s, NEG_BIG)
                scores.append(s)
        alphas, probs = [], []
        for h in range(N_HEADS):
            m_old = m_scr[h]
            m_new = jnp.maximum(m_old, jnp.max(scores[h], axis=-1, keepdims=True))
            alpha = jnp.exp(m_old - m_new)
            p = jnp.exp(scores[h] - m_new)
            l_scr[h] = alpha * l_scr[h] + jnp.sum(p, axis=-1, keepdims=True)
            m_scr[h] = m_new
            alphas.append(alpha)
            probs.append(p.astype(BF16))
        for j in pairs:
            vb = v_ref[0, :, cols[j]].astype(BF16)
            pv0 = jnp.dot(probs[2 * j], vb, preferred_element_type=F32)
            pv1 = jnp.dot(probs[2 * j + 1], vb, preferred_element_type=F32)
            acc_scr[:, cols[j]] = (acc_scr[:, cols[j]] * jnp.where(lane_a, alphas[2 * j], alphas[2 * j + 1])
                                   + jnp.where(lane_a, pv0, pv1))

    if n_past_blocks:
        @pl.when(ki < n_past_blocks)
        def _():
            step(kp_ref, vp_ref, fkp_ref, False)

    kn = ki - n_past_blocks

    @pl.when(jnp.logical_and(kn >= 0, kn < qi))
    def _():
        step(kn_ref, vn_ref, fkn_ref, False)

    @pl.when(kn == qi)
    def _():
        step(kn_ref, vn_ref, fkn_ref, True)

    @pl.when(ki == nk - 1)
    def _():
        for j in range(N_HEADS // 2):
            cols = slice(j * LANES, (j + 1) * LANES)
            l = jnp.where(lane_a, l_scr[2 * j], l_scr[2 * j + 1])
            o_ref[0, :, cols] = acc_scr[:, cols] / l * sg_ref[0, :, cols]


def _fox_attention(q, fq, sg, k_new, v_new, fk_new_t, past=None, tq=512, tk_past=512):
    b, t, _ = q.shape
    nq = t // tq
    n_past_blocks = 0 if past is None else past[0].shape[1] // tk_past
    nk = n_past_blocks + nq

    def new_idx(i, qi, ki):
        return jnp.clip(ki - n_past_blocks, 0, qi)

    in_specs = [pl.BlockSpec((1, tq, WIDTH), lambda i, qi, ki: (i, qi, 0)),
                pl.BlockSpec((1, tq, N_HEADS), lambda i, qi, ki: (i, qi, 0)),
                pl.BlockSpec((1, tq, WIDTH), lambda i, qi, ki: (i, qi, 0))]
    args = [q, fq, sg]
    if n_past_blocks:
        def past_idx(i, qi, ki):
            return jnp.minimum(ki, n_past_blocks - 1)
        in_specs += [pl.BlockSpec((1, tk_past, WIDTH), lambda i, qi, ki: (i, past_idx(i, qi, ki), 0)),
                     pl.BlockSpec((1, tk_past, WIDTH), lambda i, qi, ki: (i, past_idx(i, qi, ki), 0)),
                     pl.BlockSpec((1, N_HEADS, tk_past), lambda i, qi, ki: (i, 0, past_idx(i, qi, ki)))]
        args += list(past)
    in_specs += [pl.BlockSpec((1, tq, WIDTH), lambda i, qi, ki: (i, new_idx(i, qi, ki), 0)),
                 pl.BlockSpec((1, tq, WIDTH), lambda i, qi, ki: (i, new_idx(i, qi, ki), 0)),
                 pl.BlockSpec((1, N_HEADS, tq), lambda i, qi, ki: (i, 0, new_idx(i, qi, ki)))]
    args += [k_new, v_new, fk_new_t]
    return pl.pallas_call(
        functools.partial(_fox_kernel, n_past_blocks=n_past_blocks, tq=tq),
        grid=(b, nq, nk),
        in_specs=in_specs,
        out_specs=pl.BlockSpec((1, tq, WIDTH), lambda i, qi, ki: (i, qi, 0)),
        out_shape=jax.ShapeDtypeStruct((b, t, WIDTH), F32),
        scratch_shapes=[pltpu.VMEM((N_HEADS, tq, 1), F32), pltpu.VMEM((N_HEADS, tq, 1), F32),
                        pltpu.VMEM((tq, WIDTH), F32)],
        compiler_params=_cparams(("parallel", "parallel", "arbitrary")),
        name="fox_attention",
    )(*args)


def _rwkv_kernel(p_ref, sh0_ref, s0_ref, mu_ref, w0_ref, wb_ref, a0_ref, ab_ref, gb_ref, kk_ref, ka_ref, rk_ref,
                 lnw_ref, lnb_ref, gm_ref, y_ref, st_ref, sht_ref, z_scr, prev_scr, *, c):
    t = pl.program_id(1)
    nt = pl.num_programs(1)
    n_rows = p_ref.shape[1]
    n_chunks = n_rows // c

    def head_block(h):
        lo = (h % HEADS_PER_GROUP) * HEAD_DIM
        return h // HEADS_PER_GROUP, slice(lo, lo + HEAD_DIM)

    @pl.when(t == 0)
    def _():
        z_scr[...] = jnp.zeros(z_scr.shape, F32)
        for h in range(N_HEADS):
            i, blk = head_block(h)
            z_scr[i, blk, blk] = s0_ref[0, h]
        prev_scr[...] = sh0_ref[0]

    p = p_ref[0]
    row = lax.broadcasted_iota(I32, p.shape, 0)
    prev = jnp.where(row == 0, prev_scr[...], pltpu.roll(p, 1, 0))
    last = p[n_rows - 1:n_rows, :]
    prev_scr[...] = last
    sht_ref[0] = last

    pm = p + (prev - p) * mu_ref[...]
    r = pm[:, 0:WIDTH]
    k = pm[:, WIDTH:2 * WIDTH]
    v = pm[:, 2 * WIDTH:3 * WIDTH]
    o1 = 3 * WIDTH
    wd = pm[:, o1:o1 + DECAY_LORA]
    ad = pm[:, o1 + DECAY_LORA:o1 + DECAY_LORA + ICLR_LORA]
    gd = pm[:, o1 + DECAY_LORA + ICLR_LORA:RWKV_COLS]

    w = -_softplus(-(w0_ref[...] + _dot(jnp.tanh(wd), wb_ref[...]))) - 0.5
    lw = -jnp.exp(w)
    a = _sigmoid(a0_ref[...] + _dot(ad, ab_ref[...]))
    g = _dot(_sigmoid(gd), gb_ref[...])
    kk = k * kk_ref[...]
    kk = kk / jnp.maximum(jnp.sqrt(_gsum(kk * kk, gm_ref)), L2_EPS)
    kf = k * (1.0 + (a - 1.0) * ka_ref[...])

    ri = lax.broadcasted_iota(I32, (n_rows, n_rows), 0)
    ci = lax.broadcasted_iota(I32, (n_rows, n_rows), 1)
    same_chunk = (ri // c) == (ci // c)
    cum = _dot_exact_rhs(jnp.logical_and(same_chunk, ri >= ci).astype(F32), lw)
    cum_last = _dot_exact_rhs(same_chunk.astype(F32), lw)
    r_t = r * jnp.exp(cum)
    a_t = -kk * jnp.exp(cum - lw)
    inv = jnp.exp(-cum)
    b_t = kk * a * inv
    k_t = kf * inv
    to_end = jnp.exp(cum_last - cum)
    b_e = kk * a * to_end
    k_e = kf * to_end
    g_end = jnp.exp(cum_last)

    hg = HEADS_PER_GROUP
    gw = hg * HEAD_DIM
    log_c = int(math.log2(c))
    t_idx = lax.broadcasted_iota(I32, (c, hg * c), 0)
    s_idx = lax.broadcasted_iota(I32, (c, hg * c), 1) & (c - 1)
    strict = s_idx < t_idx
    lower = s_idx <= t_idx
    eye = (s_idx == t_idx).astype(F32)
    rb = lax.broadcasted_iota(I32, (hg * c, gw), 0) >> log_c
    mask_kv = rb == (lax.broadcasted_iota(I32, (hg * c, gw), 1) >> int(math.log2(HEAD_DIM)))
    rs = lax.broadcasted_iota(I32, (hg * c, hg * c), 0) >> log_c
    mask_ss = rs == (lax.broadcasted_iota(I32, (hg * c, hg * c), 1) >> log_c)
    ng = N_HEADS // hg
    cat = functools.partial(jnp.concatenate, axis=0)
    units = [(slice(j * c, (j + 1) * c), slice(i * gw, (i + 1) * gw)) for j in range(n_chunks) for i in range(ng)]
    nu = len(units)

    ar = [_split2(cat([a_t[rs_, s], r_t[rs_, s]])) for rs_, s in units]
    ab = [_mm3(ar[n], _bd_parts(b_t[units[n]], mask_kv), NT) for n in range(nu)]
    ak = [_mm3(ar[n], _bd_parts(k_t[units[n]], mask_kv), NT) for n in range(nu)]
    l_ab = [jnp.where(strict, m[:c], 0.0) for m in ab]
    l_rb = [jnp.where(lower, m[c:], 0.0) for m in ab]
    l_ak = [jnp.where(strict, m[:c], 0.0) for m in ak]
    l_rk = [jnp.where(lower, m[c:], 0.0) for m in ak]
    def mm1(a, b_bd):
        return jnp.dot(a.astype(BF16), b_bd, preferred_element_type=F32)

    def bd1(x, mask):
        tiled = jnp.concatenate([x.astype(BF16)] * hg, axis=0)
        return jnp.where(mask, tiled, jnp.zeros_like(tiled))

    tinv = [eye + m for m in l_ab]
    pw = [mm1(m, bd1(m, mask_ss)) for m in l_ab]
    for _ in range(1, log_c - 1):
        res = [mm1(cat([tinv[n], pw[n]]), bd1(pw[n], mask_ss)) for n in range(nu)]
        tinv = [tinv[n] + res[n][:c] for n in range(nu)]
        pw = [m[c:] for m in res]
    tinv = [tinv[n] + mm1(tinv[n], bd1(pw[n], mask_ss)) for n in range(nu)]
    av = [_mm3(_split2(cat([l_ak[n], l_rk[n]])), _bd_parts(v[units[n]], mask_kv), NN) for n in range(nu)]
    ue = [_split2(cat([b_e[units[n]], k_e[units[n]]])) for n in range(nu)]

    def wide(fn, x):
        return [fn(x[:, :gw]), fn(x[:, gw:])]

    def bd1w(x):
        return jnp.concatenate(wide(lambda h_: bd1(h_, mask_kv), x), axis=1)

    def bd3w(x):
        parts = wide(lambda h_: _bd_parts(h_, mask_kv), x)
        return tuple(jnp.concatenate([parts[0][q], parts[1][q]], axis=1) for q in range(2))

    rhs = [jnp.concatenate([a_t[units[n]], av[n][:c]], axis=1) for n in range(nu)]
    x0 = [mm1(tinv[n], bd1w(rhs[n])) for n in range(nu)]
    resid = [rhs[n] - (x0[n] - _mm3(_split2(l_ab[n]), bd3w(x0[n]), NN)) for n in range(nu)]
    sol = [x0[n] + mm1(tinv[n], bd1w(resid[n])) for n in range(nu)]
    lift = [_mm3(_split2(l_rb[n]), bd3w(sol[n]), NN) for n in range(nu)]
    lhs_s = [_split2(cat([sol[n][:, :gw], r_t[units[n]] + lift[n][:, :gw]])) for n in range(nu)]
    u_loc = [sol[n][:, gw:] for n in range(nu)]
    o_loc = [av[n][c:] + lift[n][:, gw:] for n in range(nu)]

    zr = lax.broadcasted_iota(I32, (gw, gw), 0) >> int(math.log2(HEAD_DIM))
    zmask = zr == (lax.broadcasted_iota(I32, (gw, gw), 1) >> int(math.log2(HEAD_DIM)))
    z = [z_scr[i] for i in range(ng)]
    o_rows = []
    for j in range(n_chunks):
        o_grp = []
        for i in range(ng):
            n = j * ng + i
            rs_, s = units[n]
            sz = _mm3(lhs_s[n], _split2(z[i]), NT)
            u = sz[:c] + u_loc[n]
            o_grp.append(sz[c:] + o_loc[n])
            upd = _mm3(_split2(cat([u, v[rs_, s]])), ue[n], TN)
            z[i] = z[i] * g_end[j * c:j * c + 1, s] + jnp.where(zmask, upd, 0.0)
        o_rows.append(jnp.concatenate(o_grp, axis=1))
    for i in range(ng):
        z_scr[i] = z[i]

    o = cat(o_rows)
    inv_hd = 1.0 / HEAD_DIM
    dlt = o - _gsum(o, gm_ref) * inv_hd
    var = _gsum(dlt * dlt, gm_ref) * inv_hd
    on = dlt * lax.rsqrt(var + RWKV_GN_EPS) * lnw_ref[...] + lnb_ref[...]
    bonus = _gsum(r * kf * rk_ref[...], gm_ref) * v
    y_ref[0] = (on + bonus) * g

    @pl.when(t == nt - 1)
    def _():
        for h in range(N_HEADS):
            i, blk = head_block(h)
            st_ref[0, h] = z_scr[i, blk, blk]


def _rwkv(p, shift0, s0, prm, gmat, chunk, chunks_per_step):
    b, t, _ = p.shape
    row = lambda n: _const_spec((1, n))
    rows = chunk * chunks_per_step
    return pl.pallas_call(
        functools.partial(_rwkv_kernel, c=chunk),
        grid=(b, t // rows),
        in_specs=[pl.BlockSpec((1, rows, RWKV_COLS), lambda i, j: (i, j, 0)),
                  pl.BlockSpec((1, 1, RWKV_COLS), lambda i, j: (i, 0, 0)),
                  pl.BlockSpec((1, N_HEADS, HEAD_DIM, HEAD_DIM), lambda i, j: (i, 0, 0, 0)),
                  row(RWKV_COLS), row(WIDTH), _const_spec((DECAY_LORA, WIDTH)), row(WIDTH),
                  _const_spec((ICLR_LORA, WIDTH)), _const_spec((GATE_LORA, WIDTH)),
                  row(WIDTH), row(WIDTH), row(WIDTH), row(WIDTH), row(WIDTH), _const_spec((WIDTH, WIDTH))],
        out_specs=[pl.BlockSpec((1, rows, WIDTH), lambda i, j: (i, j, 0)),
                   pl.BlockSpec((1, N_HEADS, HEAD_DIM, HEAD_DIM), lambda i, j: (i, 0, 0, 0)),
                   pl.BlockSpec((1, 1, RWKV_COLS), lambda i, j: (i, 0, 0))],
        out_shape=[jax.ShapeDtypeStruct((b, t, WIDTH), F32),
                   jax.ShapeDtypeStruct((b, N_HEADS, HEAD_DIM, HEAD_DIM), F32),
                   jax.ShapeDtypeStruct((b, 1, RWKV_COLS), F32)],
        scratch_shapes=[pltpu.VMEM((N_HEADS // HEADS_PER_GROUP, HEADS_PER_GROUP * HEAD_DIM,
                                    HEADS_PER_GROUP * HEAD_DIM), F32),
                        pltpu.VMEM((1, RWKV_COLS), F32)],
        compiler_params=_cparams(("parallel", "arbitrary")),
        name="rwkv7_mix",
    )(p, shift0, s0, prm["mu"], prm["w0"], prm["wb"], prm["a0"], prm["ab"], prm["gb"], prm["kk"], prm["ka"],
      prm["rk"], prm["lnw"], prm["lnb"], gmat)


def _merge_kernel(x_ref, ya_ref, yb_ref, gate_ref, mod_ref, g2_ref, woa_ref, wob_ref, wo_ref, wrh_ref, wrl_ref,
                  *rest):
    x1_ref, h2_ref, lg_ref = rest[-3:]
    bb, tt, d = x_ref.shape
    m = bb * tt
    gate = gate_ref[...].reshape(m, GATE_COLS)
    merged = (gate[:, 0:d] * _dot(ya_ref[...].reshape(m, WIDTH), woa_ref[...])
              + gate[:, d:2 * d] * _dot(yb_ref[...].reshape(m, WIDTH), wob_ref[...]))
    x1 = x_ref[...] + mod_ref[:, 2:3, :] * _dot(merged, wo_ref[...]).reshape(bb, tt, d)
    x1_ref[...] = x1
    ms = jnp.mean(x1 * x1, axis=-1, keepdims=True)
    h2 = x1 * lax.rsqrt(ms + RMS_EPS) * g2_ref[...]
    h2 = (h2 * (1.0 + mod_ref[:, 4:5, :]) + mod_ref[:, 3:4, :]).reshape(m, d)
    _to_row_tiles(h2_ref, h2)
    lg_ref[...] = _mm3((wrh_ref[...], wrl_ref[...]), _split2(h2), NT)


def _merge(x, ya, yb, gate, mod, w, n_total, row_offset, shared=None):
    b, t, d = x.shape
    bb, tt = _token_blocks(b, t)
    nt = t // tt
    m = bb * tt
    off = row_offset // m

    def tok(cols):
        return pl.BlockSpec((bb, tt, cols), lambda i, j: (i, j, 0))

    in_specs = [tok(d), tok(WIDTH), tok(WIDTH), tok(GATE_COLS),
                pl.BlockSpec((bb, 6, d), lambda i, j: (i, 0, 0)),
                _const_spec((1, d)), _const_spec((WIDTH, d)), _const_spec((WIDTH, d)), _const_spec((d, d)),
                _const_spec((N_EXPERTS, d)), _const_spec((N_EXPERTS, d))]
    args = [x, ya, yb, gate, mod.reshape(b, 6, d), w["g2"], w["w_oa"], w["w_ob"], w["w_o"], w["wr_hi"], w["wr_lo"]]
    aliases = {}
    if shared is not None:
        aliases = {len(args): 1, len(args) + 1: 2}
        in_specs += [pl.BlockSpec(memory_space=pl.ANY), pl.BlockSpec(memory_space=pl.ANY)]
        args += list(shared)
    return pl.pallas_call(
        _merge_kernel,
        grid=(b // bb, nt),
        in_specs=in_specs,
        out_specs=[tok(d), pl.BlockSpec((m, ROW_TILE_SUBLANES, LANES), lambda i, j: (off + i * nt + j, 0, 0)),
                   pl.BlockSpec((N_EXPERTS, m), lambda i, j: (0, off + i * nt + j))],
        out_shape=[jax.ShapeDtypeStruct((b, t, d), F32),
                   jax.ShapeDtypeStruct((n_total, ROW_TILE_SUBLANES, LANES), F32),
                   jax.ShapeDtypeStruct((N_EXPERTS, n_total), F32)],
        input_output_aliases=aliases,
        compiler_params=_cparams(("parallel", "parallel")),
        name="merge_norm2_router",
    )(*args)


def _route_kernel(lg_ref, bias_ref, idx_ref, wt_ref, rank_ref, cnt_ref, carry):
    @pl.when(pl.program_id(0) == 0)
    def _():
        carry[...] = jnp.zeros(carry.shape, F32)

    tm = lg_ref.shape[1]
    scores = _sigmoid(lg_ref[...])
    sel = scores + bias_ref[...]
    row = lax.broadcasted_iota(I32, (N_EXPERTS, tm), 0)
    neg_inf = -jnp.inf

    def first_argmax(vals, rows):
        mx = jnp.max(vals, axis=0, keepdims=True)
        return mx, jnp.min(jnp.where(vals == mx, rows, N_EXPERTS), axis=0, keepdims=True)

    gslices = [slice(g * EXPERTS_PER_GROUP, (g + 1) * EXPERTS_PER_GROUP) for g in range(N_GROUPS)]
    gs = []
    row_g = lax.broadcasted_iota(I32, (EXPERTS_PER_GROUP, tm), 0)
    for sl in gslices:
        m1, i1 = first_argmax(sel[sl], row_g)
        m2 = jnp.max(jnp.where(row_g == i1, neg_inf, sel[sl]), axis=0, keepdims=True)
        gs.append(m1 + m2)
    kept = []
    for g in range(N_GROUPS):
        beaten = jnp.zeros((1, tm), I32)
        for o in range(N_GROUPS):
            if o != g:
                wins = (gs[o] >= gs[g]) if o < g else (gs[o] > gs[g])
                beaten = beaten + wins.astype(I32)
        kept.append(jnp.where(beaten < TOPK_GROUPS, sel[gslices[g]], neg_inf))
    cur = jnp.concatenate(kept, axis=0)

    idxs, ws = [], []
    picked = jnp.zeros((N_EXPERTS, tm), F32)
    for _ in range(TOP_K):
        _, ik = first_argmax(cur, row)
        hit = row == ik
        idxs.append(ik)
        ws.append(jnp.sum(jnp.where(hit, scores, 0.0), axis=0, keepdims=True))
        cur = jnp.where(hit, neg_inf, cur)
        picked = jnp.where(hit, 1.0, picked)
    wsum = ws[0]
    for k in range(1, TOP_K):
        wsum = wsum + ws[k]

    r = lax.broadcasted_iota(I32, (tm, tm), 0)
    c = lax.broadcasted_iota(I32, (tm, tm), 1)
    before = jnp.dot(picked.astype(BF16), (r < c).astype(BF16), preferred_element_type=F32) + carry[...]
    carry[...] = carry[...] + jnp.sum(picked, axis=1, keepdims=True)
    cnt_ref[...] = carry[...]

    kk = lax.broadcasted_iota(I32, (TOP_K, tm), 0)
    idx_o = jnp.zeros((TOP_K, tm), I32)
    wt_o = jnp.zeros((TOP_K, tm), F32)
    rank_o = jnp.zeros((TOP_K, tm), F32)
    for k in range(TOP_K):
        rk = jnp.sum(jnp.where(row == idxs[k], before, 0.0), axis=0, keepdims=True)
        idx_o = jnp.where(kk == k, idxs[k], idx_o)
        wt_o = jnp.where(kk == k, ws[k] / wsum * ROUTED_SCALE, wt_o)
        rank_o = jnp.where(kk == k, rk, rank_o)
    idx_ref[...] = idx_o
    wt_ref[...] = wt_o
    rank_ref[...] = rank_o.astype(I32)


def _route(logits_t, bias_col, tm):
    n = logits_t.shape[1]
    tokk = pl.BlockSpec((TOP_K, tm), lambda i: (0, i))
    return pl.pallas_call(
        _route_kernel,
        grid=(n // tm,),
        in_specs=[pl.BlockSpec((N_EXPERTS, tm), lambda i: (0, i)), _const_spec((N_EXPERTS, 1))],
        out_specs=[tokk, tokk, tokk, _const_spec((N_EXPERTS, 1))],
        out_shape=[jax.ShapeDtypeStruct((TOP_K, n), I32), jax.ShapeDtypeStruct((TOP_K, n), F32),
                   jax.ShapeDtypeStruct((TOP_K, n), I32), jax.ShapeDtypeStruct((N_EXPERTS, 1), F32)],
        scratch_shapes=[pltpu.VMEM((N_EXPERTS, 1), F32)],
        compiler_params=_cparams(("arbitrary",)),
        name="route_topk",
    )(logits_t, bias_col)


def _plan_kernel(cnt_ref, start_ref, be_ref, valid_ref, nu_ref, *, blk):
    cnt = cnt_ref[...]
    padded = jnp.ceil(cnt * (1.0 / blk)) * blk
    e_r = lax.broadcasted_iota(I32, (N_EXPERTS, N_EXPERTS), 0)
    e_c = lax.broadcasted_iota(I32, (N_EXPERTS, N_EXPERTS), 1)
    incl = (e_r <= e_c).astype(BF16)
    ph, pm, plo = _split3(jnp.broadcast_to(padded, (8, N_EXPERTS)))
    d2 = functools.partial(jnp.dot, preferred_element_type=F32)
    pad_end = (d2(ph, incl) + (d2(pm, incl) + d2(plo, incl)))[0:1, :]
    pad_start = pad_end - padded
    start_ref[...] = pad_start.astype(I32)
    total = jnp.max(pad_end, axis=-1, keepdims=True)
    nu_ref[...] = jnp.broadcast_to(total * (1.0 / blk), (1, N_EXPERTS)).astype(I32)
    nb = be_ref.shape[0]
    first = (lax.broadcasted_iota(I32, (nb, N_EXPERTS), 0) * blk).astype(F32)
    lane = lax.broadcasted_iota(I32, (nb, N_EXPERTS), 1)
    inside = jnp.logical_and(pad_start <= first, first < pad_end)
    be_ref[...] = jnp.sum(jnp.where(inside, lane, 0), axis=-1, keepdims=True)
    rows = jnp.minimum(pad_start + cnt - first, float(blk))
    valid_ref[...] = jnp.sum(jnp.where(inside, rows, 0.0), axis=-1, keepdims=True).astype(I32)


def _plan(counts, n_blocks, blk):
    return pl.pallas_call(
        functools.partial(_plan_kernel, blk=blk),
        out_shape=[jax.ShapeDtypeStruct((1, N_EXPERTS), I32), jax.ShapeDtypeStruct((n_blocks, 1), I32),
                   jax.ShapeDtypeStruct((n_blocks, 1), I32), jax.ShapeDtypeStruct((1, N_EXPERTS), I32)],
        compiler_params=pltpu.CompilerParams(vmem_limit_bytes=VMEM_LIMIT),
        name="dispatch_plan",
    )(counts)


def _dest_kernel(idx_ref, rank_ref, start_ref, dest_ref):
    tm = idx_ref.shape[1]
    row = lax.broadcasted_iota(I32, (N_EXPERTS, tm), 0)
    kk = lax.broadcasted_iota(I32, (TOP_K, tm), 0)
    idx = idx_ref[...]
    base = jnp.zeros((TOP_K, tm), I32)
    for k in range(TOP_K):
        bk = jnp.sum(jnp.where(row == idx[k:k + 1, :], start_ref[...], 0), axis=0, keepdims=True)
        base = jnp.where(kk == k, bk, base)
    dest_ref[...] = base + rank_ref[...]


def _dest(idx, rank, pad_start_col, tm):
    n = idx.shape[1]
    tokk = pl.BlockSpec((TOP_K, tm), lambda i: (0, i))
    return pl.pallas_call(
        _dest_kernel,
        grid=(n // tm,),
        in_specs=[tokk, tokk, _const_spec((N_EXPERTS, 1))],
        out_specs=tokk,
        out_shape=jax.ShapeDtypeStruct((TOP_K, n), I32),
        compiler_params=_cparams(("parallel",)),
        name="dispatch_dest",
    )(idx, rank, pad_start_col)


def _row_copy(src_ref, src_row, dst_ref, dst_row, sem):
    return pltpu.make_async_copy(src_ref.at[src_row], dst_ref.at[dst_row], sem)


def _dispatch_kernel(dest_ref, h2_ref, xs_ref, sem):
    tm = h2_ref.shape[0]

    def issue(r, carry):
        for k in range(TOP_K):
            _row_copy(h2_ref, r, xs_ref, dest_ref[k, r], sem).start()
        return carry

    lax.fori_loop(0, tm, issue, 0)
    for k in range(TOP_K):
        pltpu.make_async_copy(h2_ref, xs_ref.at[pl.ds(0, tm)], sem).wait()


def _dispatch(h2, dest_t, n_slots, tm):
    n = h2.shape[0]
    return pl.pallas_call(
        _dispatch_kernel,
        grid=(n // tm,),
        in_specs=[pl.BlockSpec((TOP_K, tm), lambda i: (0, i), memory_space=pltpu.SMEM),
                  pl.BlockSpec((tm, ROW_TILE_SUBLANES, LANES), lambda i: (i, 0, 0))],
        out_specs=pl.BlockSpec(memory_space=pl.ANY),
        out_shape=jax.ShapeDtypeStruct((n_slots, ROW_TILE_SUBLANES, LANES), F32),
        scratch_shapes=[pltpu.SemaphoreType.DMA(())],
        compiler_params=_cparams(("arbitrary",)),
        name="moe_dispatch",
    )(dest_t, h2)


def _expert_kernel(be_ref, valid_ref, nu_ref, x_ref, wg_ref, wu_ref, wd_ref, y_ref, wg_b, wu_b, wd_b):
    i = pl.program_id(0)
    nv = valid_ref[i]
    new_expert = jnp.logical_or(i == 0, be_ref[i] != be_ref[jnp.maximum(i - 1, 0)])

    @pl.when(jnp.logical_and(nv > 0, new_expert))
    def _():
        wg_b[...] = wg_ref[0].astype(BF16)
        wu_b[...] = wu_ref[0].astype(BF16)
        wd_b[...] = wd_ref[0].astype(BF16)

    @pl.when(nv > 0)
    def _():
        blk = x_ref.shape[0]
        rows = lax.broadcasted_iota(I32, (blk, 1), 0)
        x = jnp.where(rows < nv, _from_row_tiles(x_ref), 0.0).astype(BF16)
        hg = jnp.dot(x, wg_b[...], preferred_element_type=F32)
        hu = jnp.dot(x, wu_b[...], preferred_element_type=F32)
        _to_row_tiles(y_ref, jnp.dot((_silu(hg) * hu).astype(BF16), wd_b[...], preferred_element_type=F32))


def _experts(xs, block_e, valid, n_used, w_eg, w_eu, w_ed, blk):
    n_slots = xs.shape[0]
    d = D_MODEL
    n_blocks = n_slots // blk

    def row_blk(i, be, valid, nu):
        return (jnp.minimum(i, nu[0] - 1), 0, 0)

    def w_blk(i, be, valid, nu):
        return (be[i], 0, 0)

    return pl.pallas_call(
        _expert_kernel,
        grid_spec=pltpu.PrefetchScalarGridSpec(
            num_scalar_prefetch=3,
            grid=(n_blocks,),
            in_specs=[pl.BlockSpec((blk, ROW_TILE_SUBLANES, LANES), row_blk),
                      pl.BlockSpec((1, d, D_EXPERT), w_blk), pl.BlockSpec((1, d, D_EXPERT), w_blk),
                      pl.BlockSpec((1, D_EXPERT, d), w_blk)],
            out_specs=pl.BlockSpec((blk, ROW_TILE_SUBLANES, LANES), row_blk),
            scratch_shapes=[pltpu.VMEM((d, D_EXPERT), BF16), pltpu.VMEM((d, D_EXPERT), BF16),
                            pltpu.VMEM((D_EXPERT, d), BF16)]),
        out_shape=jax.ShapeDtypeStruct((n_slots, ROW_TILE_SUBLANES, LANES), F32),
        compiler_params=_cparams(("arbitrary",)),
        name="moe_experts",
    )(block_e, valid, n_used, xs, w_eg, w_eu, w_ed)


def _final_kernel(dest_ref, wt_ref, x1_ref, h2_ref, mod_ref, wsg_ref, wsu_ref, wsd_ref, ys_ref, o_ref,
                  ybuf, acc, sem):
    bb, tt, d = x1_ref.shape
    m = bb * tt

    def issue(r, carry):
        for k in range(TOP_K):
            _row_copy(ys_ref, dest_ref[k, r], ybuf.at[k], r, sem).start()
        return carry

    lax.fori_loop(0, m, issue, 0)

    hb = _from_row_tiles(h2_ref).astype(BF16)
    hg = jnp.dot(hb, wsg_ref[...], preferred_element_type=F32)
    hu = jnp.dot(hb, wsu_ref[...], preferred_element_type=F32)
    shared = _dot(_silu(hg) * hu, wsd_ref[...])

    for k in range(TOP_K):
        pltpu.make_async_copy(ys_ref.at[pl.ds(0, m)], ybuf.at[k], sem).wait()

    def combine(r, carry):
        tile = ybuf[0, r] * wt_ref[0, r]
        for k in range(1, TOP_K):
            tile = tile + ybuf[k, r] * wt_ref[k, r]
        acc[r] = tile
        return carry

    lax.fori_loop(0, m, combine, 0, unroll=COMBINE_UNROLL)
    ffn = _from_row_tiles(acc) + shared
    o_ref[...] = x1_ref[...] + mod_ref[:, 5:6, :] * ffn.reshape(bb, tt, d)


def _final(x1, h2_all, wts_t, dest_t, ys, mod, w, row_offset):
    b, t, d = x1.shape
    bb, tt = _token_blocks(b, t)
    nt = t // tt
    m = bb * tt
    off = row_offset // m

    def flat_idx(i, j):
        return off + i * nt + j

    tile = (ROW_TILE_SUBLANES, LANES)
    return pl.pallas_call(
        _final_kernel,
        grid=(b // bb, nt),
        in_specs=[pl.BlockSpec((TOP_K, m), lambda i, j: (0, flat_idx(i, j)), memory_space=pltpu.SMEM),
                  pl.BlockSpec((TOP_K, m), lambda i, j: (0, flat_idx(i, j)), memory_space=pltpu.SMEM),
                  pl.BlockSpec((bb, tt, d), lambda i, j: (i, j, 0)),
                  pl.BlockSpec((m,) + tile, lambda i, j: (flat_idx(i, j), 0, 0)),
                  pl.BlockSpec((bb, 6, d), lambda i, j: (i, 0, 0)),
                  _const_spec((d, D_EXPERT)), _const_spec((d, D_EXPERT)), _const_spec((D_EXPERT, d)),
                  pl.BlockSpec(memory_space=pl.ANY)],
        out_specs=pl.BlockSpec((bb, tt, d), lambda i, j: (i, j, 0)),
        out_shape=jax.ShapeDtypeStruct((b, t, d), F32),
        scratch_shapes=[pltpu.VMEM((TOP_K, m) + tile, F32), pltpu.VMEM((m,) + tile, F32),
                        pltpu.SemaphoreType.DMA(())],
        compiler_params=_cparams(("arbitrary", "arbitrary")),
        name="moe_combine_final",
    )(dest_t, wts_t, x1, h2_all, mod.reshape(b, 6, d), w["w_sg"], w["w_su"], w["w_sd"], ys)


def _moe_routed(h2_all, logits_all, w, blk=256, tm=256):
    n = h2_all.shape[0]
    n_blocks = (n * TOP_K + N_EXPERTS * (blk - 1)) // blk + 1
    n_blocks = (n_blocks + 7) // 8 * 8
    idx, wts_t, rank, counts = _route(logits_all, w["router_bias"], tm)
    pad_start, block_e, valid, n_used = _plan(counts.reshape(1, N_EXPERTS), n_blocks, blk)
    block_e = block_e.reshape(n_blocks)
    valid = valid.reshape(n_blocks)
    n_used = n_used[0, 0:1]
    dest_t = _dest(idx, rank, pad_start.reshape(N_EXPERTS, 1), tm)
    xs = _dispatch(h2_all, dest_t, n_blocks * blk, tm)
    ys = _experts(xs, block_e, valid, n_used, w["w_eg"], w["w_eu"], w["w_ed"], blk)
    return ys, dest_t, wts_t


def _prep(raw):
    p = {k: v[0] for k, v in raw.items()}
    w_in = p["w_in"]
    o_fox = RWKV_COLS
    o_fl = o_fox + FOX_MAIN_COLS
    o_gate = o_fl + N_HEADS
    row = lambda a: a.reshape(1, -1)
    return dict(
        w_ada=p["w_ada"], b_ada=p["b_ada"],
        g1=row(p["norm1_g"]), g2=row(p["norm2_g"]),
        wr=w_in[:, :o_fox].astype(BF16),
        wf=w_in[:, o_fox:o_fl].astype(BF16),
        wfl=jnp.pad(w_in[:, o_fl:o_gate], ((0, 0), (0, LANES - N_HEADS))).astype(BF16),
        wg=w_in[:, o_gate:].astype(BF16),
        qn=row(jnp.tile(p["fox_q_norm"], N_HEADS)), kn=row(jnp.tile(p["fox_k_norm"], N_HEADS)),
        fb=row(p["fox_f_bias"]),
        gmat=_group_ones(),
        rwkv=dict(mu=row(p["rwkv_mu"]), w0=row(p["rwkv_w0"]), wb=p["rwkv_w_lora_b"], a0=row(p["rwkv_a0"]),
                  ab=p["rwkv_a_lora_b"], gb=p["rwkv_g_lora_b"], kk=row(p["rwkv_k_k"]), ka=row(p["rwkv_k_a"]),
                  rk=row(p["rwkv_r_k"]), lnw=row(p["rwkv_ln_w"]), lnb=row(p["rwkv_ln_b"])),
        w_oa=p["w_out_rwkv"].astype(BF16), w_ob=p["w_out_fox"].astype(BF16), w_o=p["w_out"].astype(BF16),
        wr_hi=p["w_router"].T.astype(BF16),
        wr_lo=(p["w_router"] - p["w_router"].astype(BF16).astype(F32)).T.astype(BF16),
        router_bias=p["router_bias"].reshape(N_EXPERTS, 1),
        w_eg=p["w_exp_gate"], w_eu=p["w_exp_up"], w_ed=p["w_exp_down"],
        w_sg=p["w_sh_gate"].astype(BF16), w_su=p["w_sh_up"].astype(BF16), w_sd=p["w_sh_down"].astype(BF16),
    )


def _token_blocks(b, t):
    if t >= 256:
        return 1, 256
    bb = max(1, min(b, 256 // t))
    while b % bb:
        bb -= 1
    return bb, t


def _mix_path(x, mod, shift0, wkv0, past_k, past_v, past_logf, w):
    b, t, d = x.shape
    bb, tt = _token_blocks(b, t)
    n_past = past_k.shape[1]
    if n_past:
        f_past = _past_cumsum(past_logf)
        init = f_past[:, n_past - 1:n_past, :]
        past = (past_k, past_v, jnp.swapaxes(f_past, 1, 2))
    else:
        init = jnp.zeros((b, 1, N_HEADS), F32)
        past = None
    pr, q, k, v, sg, logf, f_new, gate = _inproj(x, mod.reshape(b, 6, d), w["g1"], w["wr"], w["wf"], w["wfl"],
                                                 w["wg"], w["qn"], w["kn"], w["fb"], w["gmat"], init, bb, tt)
    y_fox = _fox_attention(q, f_new, sg, k, v, jnp.swapaxes(f_new, 1, 2), past=past, tq=min(t, 512),
                           tk_past=min(max(n_past, 1), 512))
    chunk = min(t, RWKV_CHUNK)
    y_rwkv, wkv_new, shift_new = _rwkv(pr, shift0.reshape(b, 1, RWKV_COLS), wkv0, w["rwkv"], w["gmat"],
                                       chunk, max(1, min(RWKV_CHUNKS_PER_STEP, t // chunk)))
    return y_rwkv, y_fox, gate, wkv_new, shift_new, k, v, logf


def _layer(paths, w):
    n_b = [p[0].shape[0] for p in paths]
    mod_all = _ada(jnp.concatenate([p[1] for p in paths], axis=0), w["w_ada"], w["b_ada"])
    mods, o = [], 0
    for nb in n_b:
        mods.append(mod_all[o:o + nb])
        o += nb
    n_total = sum(p[0].shape[0] * p[0].shape[1] for p in paths)
    mixed, x1s = [], []
    shared, row = None, 0
    for (x, _, shift0, wkv0, pk, pv, plf), mod in zip(paths, mods):
        ya, yb, gate, wkv_new, shift_new, k, v, logf = _mix_path(x, mod, shift0, wkv0, pk, pv, plf, w)
        x1, h2_all, lg_all = _merge(x, ya, yb, gate, mod, w, n_total, row, shared)
        shared = (h2_all, lg_all)
        row += x.shape[0] * x.shape[1]
        mixed.append((wkv_new, shift_new, k, v, logf))
        x1s.append(x1)
    ys, dest_t, wts = _moe_routed(h2_all, lg_all, w)
    outs, row = [], 0
    for x1, mod, st in zip(x1s, mods, mixed):
        y = _final(x1, h2_all, wts, dest_t, ys, mod, w, row)
        row += x1.shape[0] * x1.shape[1]
        outs.append((y,) + st)
    return outs


def kernel(x_prompt, x_sample, c_prompt, c_sample, state_rwkv_wkv, state_rwkv_shift, cache_fox_k, cache_fox_v,
           cache_fox_logf, w_ada, b_ada, norm1_g, norm2_g, w_in, rwkv_mu, rwkv_w0, rwkv_w_lora_b, rwkv_a0,
           rwkv_a_lora_b, rwkv_g_lora_b, rwkv_k_k, rwkv_k_a, rwkv_r_k, rwkv_ln_w, rwkv_ln_b, fox_q_norm,
           fox_k_norm, fox_f_bias, w_out_rwkv, w_out_fox, w_out, w_router, router_bias, w_exp_gate, w_exp_up,
           w_exp_down, w_sh_gate, w_sh_up, w_sh_down):
    raw = dict(w_ada=w_ada, b_ada=b_ada, norm1_g=norm1_g, norm2_g=norm2_g, w_in=w_in, rwkv_mu=rwkv_mu,
               rwkv_w0=rwkv_w0, rwkv_w_lora_b=rwkv_w_lora_b, rwkv_a0=rwkv_a0, rwkv_a_lora_b=rwkv_a_lora_b,
               rwkv_g_lora_b=rwkv_g_lora_b, rwkv_k_k=rwkv_k_k, rwkv_k_a=rwkv_k_a, rwkv_r_k=rwkv_r_k,
               rwkv_ln_w=rwkv_ln_w, rwkv_ln_b=rwkv_ln_b, fox_q_norm=fox_q_norm, fox_k_norm=fox_k_norm,
               fox_f_bias=fox_f_bias, w_out_rwkv=w_out_rwkv, w_out_fox=w_out_fox, w_out=w_out,
               w_router=w_router, router_bias=router_bias, w_exp_gate=w_exp_gate, w_exp_up=w_exp_up,
               w_exp_down=w_exp_down, w_sh_gate=w_sh_gate, w_sh_up=w_sh_up, w_sh_down=w_sh_down)
    assert w_in.shape[0] == 1, "single-layer stack"
    w = _prep(raw)
    bp, tp, _ = x_prompt.shape
    bs, ts, _ = x_sample.shape
    n_past = cache_fox_k.shape[2]
    prompt = (x_prompt, c_prompt, jnp.zeros((bp, RWKV_COLS), F32),
              jnp.zeros((bp, N_HEADS, HEAD_DIM, HEAD_DIM), F32),
              jnp.zeros((bp, 0, WIDTH), F32), jnp.zeros((bp, 0, WIDTH), F32), jnp.zeros((bp, 0, N_HEADS), F32))
    sample = (x_sample, c_sample, state_rwkv_shift[0], state_rwkv_wkv[0],
              cache_fox_k[0].reshape(bs, n_past, WIDTH), cache_fox_v[0].reshape(bs, n_past, WIDTH),
              cache_fox_logf[0])
    (yp, wkv_p, sh_p, k_p, v_p, lf_p), (ysm, wkv_s, sh_s, k_s, v_s, lf_s) = _layer([prompt, sample], w)

    def heads(a):
        return a.reshape((1,) + a.shape[:2] + (N_HEADS, HEAD_DIM))

    return (yp, ysm,
            wkv_p[None], sh_p.reshape(1, bp, RWKV_COLS), heads(k_p), heads(v_p), lf_p[None],
            wkv_s[None], sh_s.reshape(1, bs, RWKV_COLS), heads(k_s), heads(v_s), lf_s[None])
```

```python
import functools
import math

import jax
import jax.numpy as jnp
from jax import lax
from jax.experimental import pallas as pl
from jax.experimental.pallas import tpu as pltpu
from jax.experimental.pallas import tpu_sc as plsc

F32 = jnp.float32
BF16 = jnp.bfloat16
I32 = jnp.int32

D_MODEL = 1024
N_HEADS = 8
HEAD_DIM = 64
WIDTH = N_HEADS * HEAD_DIM
HEADS_PER_GROUP = 4
RWKV_CHUNK = 64
RWKV_CHUNKS_PER_STEP = 4
SC_SCATTER_WINDOW = 128
DECAY_LORA = 64
ICLR_LORA = 64
GATE_LORA = 128
RWKV_COLS = 3 * WIDTH + DECAY_LORA + ICLR_LORA + GATE_LORA
FOX_MAIN_COLS = 4 * WIDTH
GATE_COLS = 2 * D_MODEL
RWKV_GN_EPS = HEAD_DIM * 1e-5
L2_EPS = 1e-12
RMS_EPS = 1e-6
N_EXPERTS = 256
TOP_K = 8
N_GROUPS = 8
TOPK_GROUPS = 4
EXPERTS_PER_GROUP = N_EXPERTS // N_GROUPS
D_EXPERT = 256
ROUTED_SCALE = 2.5

LANES = 128
VMEM_LIMIT = 56 * 1024 * 1024
NEG_BIG = -1e30

NN = (((1,), (0,)), ((), ()))
NT = (((1,), (1,)), ((), ()))
TN = (((0,), (0,)), ((), ()))


def _cparams(sem):
    return pltpu.CompilerParams(dimension_semantics=sem, vmem_limit_bytes=VMEM_LIMIT)


def _dot(a, b, dims=NN):
    return lax.dot_general(a.astype(BF16), b.astype(BF16), dims, preferred_element_type=F32)


def _split2(a):
    hi = a.astype(BF16)
    lo = (a - hi.astype(F32)).astype(BF16)
    return hi, lo


def _split3(a):
    hi = a.astype(BF16)
    r1 = a - hi.astype(F32)
    mid = r1.astype(BF16)
    lo = (r1 - mid.astype(F32)).astype(BF16)
    return hi, mid, lo


def _dot3(a, b, dims=NN):
    ah, al = _split2(a)
    bh, bl = _split2(b)
    d = functools.partial(lax.dot_general, dimension_numbers=dims, preferred_element_type=F32)
    return d(ah, bh) + (d(ah, bl) + d(al, bh))


def _mm3(a, b, dims):
    d = functools.partial(lax.dot_general, dimension_numbers=dims, preferred_element_type=F32)
    return d(a[0], b[0]) + (d(a[0], b[1]) + d(a[1], b[0]))


def _bd_parts(x, mask):
    out = []
    for part in _split2(x):
        tiled = jnp.concatenate([part] * HEADS_PER_GROUP, axis=0)
        out.append(jnp.where(mask, tiled, jnp.zeros_like(tiled)))
    return tuple(out)


def _dot_exact_rhs(a_exact, b, dims=NN):
    ab = a_exact.astype(BF16)
    bh, bm, bl = _split3(b)
    d = functools.partial(lax.dot_general, dimension_numbers=dims, preferred_element_type=F32)
    return d(ab, bh) + (d(ab, bm) + d(ab, bl))


def _gsum(x, g_ref):
    hi, mid, lo = _split3(x)
    g = g_ref[...]
    d = functools.partial(jnp.dot, preferred_element_type=F32)
    return d(hi, g) + (d(mid, g) + d(lo, g))


def _sigmoid(x):
    return 1.0 / (1.0 + jnp.exp(-x))


def _softplus(x):
    return jnp.maximum(x, 0.0) + jnp.log1p(jnp.exp(-jnp.abs(x)))


def _silu(x):
    return x * _sigmoid(x)


def _group_ones():
    h = jnp.arange(WIDTH, dtype=I32) // HEAD_DIM
    return (h[:, None] == h[None, :]).astype(BF16)


def _ada_kernel(c_ref, w_ref, b_ref, o_ref):
    o_ref[...] = _dot(_silu(c_ref[...]), w_ref[...]) + b_ref[...]


def _ada(c, w_ada, b_ada):
    nb = c.shape[0]
    n_out = w_ada.shape[1]
    blk = D_MODEL
    return pl.pallas_call(
        _ada_kernel,
        grid=(n_out // blk,),
        in_specs=[pl.BlockSpec((nb, D_MODEL), lambda j: (0, 0)),
                  pl.BlockSpec((D_MODEL, blk), lambda j: (0, j)),
                  pl.BlockSpec((1, blk), lambda j: (0, j))],
        out_specs=pl.BlockSpec((nb, blk), lambda j: (0, j)),
        out_shape=jax.ShapeDtypeStruct((nb, n_out), F32),
        compiler_params=_cparams(("parallel",)),
        name="ada_mod",
    )(c, w_ada, b_ada.reshape(1, n_out))


def _inproj_kernel(x_ref, mod_ref, g1_ref, wr_ref, wf_ref, wfl_ref, wg_ref, qn_ref, kn_ref, fb_ref, gm_ref, f0_ref,
                   pr_ref, q_ref, k_ref, v_ref, sg_ref, lf_ref, cf_ref, gate_ref, carry):
    bb, tt, d = x_ref.shape
    m = bb * tt
    x = x_ref[...]
    ms = jnp.mean(x * x, axis=-1, keepdims=True)
    h = x * lax.rsqrt(ms + RMS_EPS) * g1_ref[...]
    h = h * (1.0 + mod_ref[:, 1:2, :]) + mod_ref[:, 0:1, :]
    hb = h.reshape(m, d).astype(BF16)

    pr_ref[...] = jnp.dot(hb, wr_ref[...], preferred_element_type=F32).reshape(bb, tt, RWKV_COLS)

    f = jnp.dot(hb, wf_ref[...], preferred_element_type=F32)
    q = f[:, 0:WIDTH]
    k = f[:, WIDTH:2 * WIDTH]
    v = f[:, 2 * WIDTH:3 * WIDTH]
    og = f[:, 3 * WIDTH:4 * WIDTH]
    inv_hd = 1.0 / HEAD_DIM
    q = q * lax.rsqrt(_gsum(q * q, gm_ref) * inv_hd + RMS_EPS) * qn_ref[...]
    k = k * lax.rsqrt(_gsum(k * k, gm_ref) * inv_hd + RMS_EPS) * kn_ref[...]
    q_ref[...] = (q * (HEAD_DIM ** -0.5)).astype(BF16).reshape(bb, tt, WIDTH)
    k_ref[...] = k.reshape(bb, tt, WIDTH)
    v_ref[...] = v.reshape(bb, tt, WIDTH)
    sg_ref[...] = _sigmoid(og).reshape(bb, tt, WIDTH)

    fl = jnp.dot(hb, wfl_ref[...], preferred_element_type=F32)[:, 0:N_HEADS] + fb_ref[...]
    lf = -_softplus(-fl)
    lf_ref[...] = lf.reshape(bb, tt, N_HEADS)

    @pl.when(pl.program_id(1) == 0)
    def _():
        carry[...] = f0_ref[...]

    r = lax.broadcasted_iota(I32, (m, m), 0)
    c = lax.broadcasted_iota(I32, (m, m), 1)
    tri = jnp.logical_and(r // tt == c // tt, r >= c).astype(F32)
    cf = _dot_exact_rhs(tri, lf).reshape(bb, tt, N_HEADS) + carry[...]
    cf_ref[...] = cf
    carry[...] = cf[:, tt - 1:tt, :]

    gate_ref[...] =_sigmoid(jnp.dot(hb, wg_ref[...], preferred_element_type=F32)).reshape(bb, tt, GATE_COLS)


def _const_spec(shape):
    nd = len(shape)
    return pl.BlockSpec(shape, lambda *_: (0,) * nd)


def _inproj(x, mod, g1, wr, wf, wfl, wg, qn, kn, fb, gmat, f0, bb, tt):
    b, t, d = x.shape
    grid = (b // bb, t // tt)

    def tok(cols):
        return pl.BlockSpec((bb, tt, cols), lambda i, j: (i, j, 0))

    out_cols = [(RWKV_COLS, F32), (WIDTH, BF16), (WIDTH, F32), (WIDTH, F32), (WIDTH, F32), (N_HEADS, F32),
                (N_HEADS, F32), (GATE_COLS, F32)]
    return pl.pallas_call(
        _inproj_kernel,
        grid=grid,
        in_specs=[tok(d),
                  pl.BlockSpec((bb, 6, d), lambda i, j: (i, 0, 0)),
                  _const_spec((1, d)),
                  _const_spec(wr.shape), _const_spec(wf.shape), _const_spec(wfl.shape), _const_spec(wg.shape),
                  _const_spec((1, WIDTH)), _const_spec((1, WIDTH)), _const_spec((1, N_HEADS)),
                  _const_spec((WIDTH, WIDTH)),
                  pl.BlockSpec((bb, 1, N_HEADS), lambda i, j: (i, 0, 0))],
        out_specs=[tok(c) for c, _ in out_cols],
        out_shape=[jax.ShapeDtypeStruct((b, t, c), dt) for c, dt in out_cols],
        scratch_shapes=[pltpu.VMEM((bb, 1, N_HEADS), F32)],
        compiler_params=_cparams(("parallel", "arbitrary")),
        name="norm1_inproj",
    )(x, mod, g1, wr, wf, wfl, wg, qn, kn, fb, gmat, f0)


def _past_cumsum_kernel(x_ref, o_ref):
    x = x_ref[0]
    rows = x.shape[0]
    li = lax.broadcasted_iota(I32, (LANES, LANES), 0)
    lj = lax.broadcasted_iota(I32, (LANES, LANES), 1)
    same_head = (li % N_HEADS) == (lj % N_HEADS)
    within = jnp.logical_and(same_head, li // N_HEADS <= lj // N_HEADS).astype(BF16)
    xh, xm, xl = _split3(x)
    d2 = functools.partial(jnp.dot, preferred_element_type=F32)
    in_row = d2(xh, within) + (d2(xm, within) + d2(xl, within))
    sh = same_head.astype(BF16)
    row_tot = d2(xh, sh) + (d2(xm, sh) + d2(xl, sh))
    ri = lax.broadcasted_iota(I32, (rows, rows), 0)
    ci = lax.broadcasted_iota(I32, (rows, rows), 1)
    o_ref[0] = in_row + _dot_exact_rhs((ri > ci).astype(F32), row_tot)


def _past_cumsum(past_logf):
    b, p, h = past_logf.shape
    rows = p * h // LANES
    flat = past_logf.reshape(b, rows, LANES)
    out = pl.pallas_call(
        _past_cumsum_kernel,
        grid=(b,),
        in_specs=[pl.BlockSpec((1, rows, LANES), lambda i: (i, 0, 0))],
        out_specs=pl.BlockSpec((1, rows, LANES), lambda i: (i, 0, 0)),
        out_shape=jax.ShapeDtypeStruct((b, rows, LANES), F32),
        compiler_params=_cparams(("parallel",)),
        name="cache_logf_cumsum",
    )(flat)
    return out.reshape(b, p, h)


def _fox_kernel(*refs, n_past_blocks, tq):
    if n_past_blocks:
        (q_ref, fq_ref, sg_ref, kp_ref, vp_ref, fkp_ref, kn_ref, vn_ref, fkn_ref,
         o_ref, m_scr, l_scr, acc_scr) = refs
    else:
        q_ref, fq_ref, sg_ref, kn_ref, vn_ref, fkn_ref, o_ref, m_scr, l_scr, acc_scr = refs
    qi = pl.program_id(1)
    ki = pl.program_id(2)
    nk = pl.num_programs(2)

    @pl.when(ki == 0)
    def _():
        m_scr[...] = jnp.full(m_scr.shape, NEG_BIG, F32)
        l_scr[...] = jnp.zeros(l_scr.shape, F32)
        acc_scr[...] = jnp.zeros(acc_scr.shape, F32)

    lane_a = lax.broadcasted_iota(I32, (tq, LANES), 1) < HEAD_DIM

    def step(k_ref, v_ref, fk_ref, diag):
        tk = k_ref.shape[1]
        if diag:
            rq = lax.broadcasted_iota(I32, (tq, tk), 0)
            ck = lax.broadcasted_iota(I32, (tq, tk), 1)
            visible = ck <= rq
        fq_all = fq_ref[0]
        pairs = range(N_HEADS // 2)
        cols = [slice(j * LANES, (j + 1) * LANES) for j in pairs]
        scores = []
        for j in pairs:
            qj = q_ref[0, :, cols[j]]
            kb = k_ref[0, :, cols[j]].astype(BF16)
            for hh in range(2):
                h = 2 * j + hh
                qm = jnp.where(lane_a if hh == 0 else jnp.logical_not(lane_a), qj, jnp.zeros_like(qj))
                s = lax.dot_general(qm, kb, NT, preferred_element_type=F32)
                s = s + fq_all[:, h:h + 1] - fk_ref[0, h:h + 1, :]
                if diag:
                    s = jnp.where(visible, s, NEG_BIG)
                scores.append(s)
        alphas, probs = [], []
        for h in range(N_HEADS):
            m_old = m_scr[h]
            m_new = jnp.maximum(m_old, jnp.max(scores[h], axis=-1, keepdims=True))
            alpha = jnp.exp(m_old - m_new)
            p = jnp.exp(scores[h] - m_new)
            l_scr[h] = alpha * l_scr[h] + jnp.sum(p, axis=-1, keepdims=True)
            m_scr[h] = m_new
            alphas.append(alpha)
            probs.append(p.astype(BF16))
        for j in pairs:
            vb = v_ref[0, :, cols[j]].astype(BF16)
            pv0 = jnp.dot(probs[2 * j], vb, preferred_element_type=F32)
            pv1 = jnp.dot(probs[2 * j + 1], vb, preferred_element_type=F32)
            acc_scr[:, cols[j]] = (acc_scr[:, cols[j]] * jnp.where(lane_a, alphas[2 * j], alphas[2 * j + 1])
                                   + jnp.where(lane_a, pv0, pv1))

    if n_past_blocks:
        @pl.when(ki < n_past_blocks)
        def _():
            step(kp_ref, vp_ref, fkp_ref, False)

    kn = ki - n_past_blocks

    @pl.when(jnp.logical_and(kn >= 0, kn < qi))
    def _():
        step(kn_ref, vn_ref, fkn_ref, False)

    @pl.when(kn == qi)
    def _():
        step(kn_ref, vn_ref, fkn_ref, True)

    @pl.when(ki == nk - 1)
    def _():
        for j in range(N_HEADS // 2):
            cols = slice(j * LANES, (j + 1) * LANES)
            l = jnp.where(lane_a, l_scr[2 * j], l_scr[2 * j + 1])
            o_ref[0, :, cols] = acc_scr[:, cols] / l * sg_ref[0, :, cols]


def _fox_attention(q, fq, sg, k_new, v_new, fk_new_t, past=None, tq=512, tk_past=512):
    b, t, _ = q.shape
    nq = t // tq
    n_past_blocks = 0 if past is None else past[0].shape[1] // tk_past
    nk = n_past_blocks + nq

    def new_idx(i, qi, ki):
        return jnp.clip(ki - n_past_blocks, 0, qi)

    in_specs = [pl.BlockSpec((1, tq, WIDTH), lambda i, qi, ki: (i, qi, 0)),
                pl.BlockSpec((1, tq, N_HEADS), lambda i, qi, ki: (i, qi, 0)),
                pl.BlockSpec((1, tq, WIDTH), lambda i, qi, ki: (i, qi, 0))]
    args = [q, fq, sg]
    if n_past_blocks:
        def past_idx(i, qi, ki):
            return jnp.minimum(ki, n_past_blocks - 1)
        in_specs += [pl.BlockSpec((1, tk_past, WIDTH), lambda i, qi, ki: (i, past_idx(i, qi, ki), 0)),
                     pl.BlockSpec((1, tk_past, WIDTH), lambda i, qi, ki: (i, past_idx(i, qi, ki), 0)),
                     pl.BlockSpec((1, N_HEADS, tk_past), lambda i, qi, ki: (i, 0, past_idx(i, qi, ki)))]
        args += list(past)
    in_specs += [pl.BlockSpec((1, tq, WIDTH), lambda i, qi, ki: (i, new_idx(i, qi, ki), 0)),
                 pl.BlockSpec((1, tq, WIDTH), lambda i, qi, ki: (i, new_idx(i, qi, ki), 0)),
                 pl.BlockSpec((1, N_HEADS, tq), lambda i, qi, ki: (i, 0, new_idx(i, qi, ki)))]
    args += [k_new, v_new, fk_new_t]
    return pl.pallas_call(
        functools.partial(_fox_kernel, n_past_blocks=n_past_blocks, tq=tq),
        grid=(b, nq, nk),
        in_specs=in_specs,
        out_specs=pl.BlockSpec((1, tq, WIDTH), lambda i, qi, ki: (i, qi, 0)),
        out_shape=jax.ShapeDtypeStruct((b, t, WIDTH), F32),
        scratch_shapes=[pltpu.VMEM((N_HEADS, tq, 1), F32), pltpu.VMEM((N_HEADS, tq, 1), F32),
                        pltpu.VMEM((tq, WIDTH), F32)],
        compiler_params=_cparams(("parallel", "parallel", "arbitrary")),
        name="fox_attention",
    )(*args)


def _rwkv_kernel(p_ref, sh0_ref, s0_ref, mu_ref, w0_ref, wb_ref, a0_ref, ab_ref, gb_ref, kk_ref, ka_ref, rk_ref,
                 lnw_ref, lnb_ref, gm_ref, y_ref, st_ref, sht_ref, z_scr, prev_scr, *, c):
    t = pl.program_id(1)
    nt = pl.num_programs(1)
    n_rows = p_ref.shape[1]
    n_chunks = n_rows // c

    def head_block(h):
        lo = (h % HEADS_PER_GROUP) * HEAD_DIM
        return h // HEADS_PER_GROUP, slice(lo, lo + HEAD_DIM)

    @pl.when(t == 0)
    def _():
        z_scr[...] = jnp.zeros(z_scr.shape, F32)
        for h in range(N_HEADS):
            i, blk = head_block(h)
            z_scr[i, blk, blk] = s0_ref[0, h]
        prev_scr[...] = sh0_ref[0]

    p = p_ref[0]
    row = lax.broadcasted_iota(I32, p.shape, 0)
    prev = jnp.where(row == 0, prev_scr[...], pltpu.roll(p, 1, 0))
    last = p[n_rows - 1:n_rows, :]
    prev_scr[...] = last
    sht_ref[0] = last

    pm = p + (prev - p) * mu_ref[...]
    r = pm[:, 0:WIDTH]
    k = pm[:, WIDTH:2 * WIDTH]
    v = pm[:, 2 * WIDTH:3 * WIDTH]
    o1 = 3 * WIDTH
    wd = pm[:, o1:o1 + DECAY_LORA]
    ad = pm[:, o1 + DECAY_LORA:o1 + DECAY_LORA + ICLR_LORA]
    gd = pm[:, o1 + DECAY_LORA + ICLR_LORA:RWKV_COLS]

    w = -_softplus(-(w0_ref[...] + _dot(jnp.tanh(wd), wb_ref[...]))) - 0.5
    lw = -jnp.exp(w)
    a = _sigmoid(a0_ref[...] + _dot(ad, ab_ref[...]))
    g = _dot(_sigmoid(gd), gb_ref[...])
    kk = k * kk_ref[...]
    kk = kk / jnp.maximum(jnp.sqrt(_gsum(kk * kk, gm_ref)), L2_EPS)
    kf = k * (1.0 + (a - 1.0) * ka_ref[...])

    ri = lax.broadcasted_iota(I32, (n_rows, n_rows), 0)
    ci = lax.broadcasted_iota(I32, (n_rows, n_rows), 1)
    same_chunk = (ri // c) == (ci // c)
    cum = _dot_exact_rhs(jnp.logical_and(same_chunk, ri >= ci).astype(F32), lw)
    cum_last = _dot_exact_rhs(same_chunk.astype(F32), lw)
    r_t = r * jnp.exp(cum)
    a_t = -kk * jnp.exp(cum - lw)
    inv = jnp.exp(-cum)
    b_t = kk * a * inv
    k_t = kf * inv
    to_end = jnp.exp(cum_last - cum)
    b_e = kk * a * to_end
    k_e = kf * to_end
    g_end = jnp.exp(cum_last)

    hg = HEADS_PER_GROUP
    gw = hg * HEAD_DIM
    log_c = int(math.log2(c))
    t_idx = lax.broadcasted_iota(I32, (c, hg * c), 0)
    s_idx = lax.broadcasted_iota(I32, (c, hg * c), 1) & (c - 1)
    strict = s_idx < t_idx
    lower = s_idx <= t_idx
    eye = (s_idx == t_idx).astype(F32)
    rb = lax.broadcasted_iota(I32, (hg * c, gw), 0) >> log_c
    mask_kv = rb == (lax.broadcasted_iota(I32, (hg * c, gw), 1) >> int(math.log2(HEAD_DIM)))
    rs = lax.broadcasted_iota(I32, (hg * c, hg * c), 0) >> log_c
    mask_ss = rs == (lax.broadcasted_iota(I32, (hg * c, hg * c), 1) >> log_c)
    ng = N_HEADS // hg
    cat = functools.partial(jnp.concatenate, axis=0)
    units = [(slice(j * c, (j + 1) * c), slice(i * gw, (i + 1) * gw)) for j in range(n_chunks) for i in range(ng)]
    nu = len(units)

    ar = [_split2(cat([a_t[rs_, s], r_t[rs_, s]])) for rs_, s in units]
    ab = [_mm3(ar[n], _bd_parts(b_t[units[n]], mask_kv), NT) for n in range(nu)]
    ak = [_mm3(ar[n], _bd_parts(k_t[units[n]], mask_kv), NT) for n in range(nu)]
    l_ab = [jnp.where(strict, m[:c], 0.0) for m in ab]
    l_rb = [jnp.where(lower, m[c:], 0.0) for m in ab]
    l_ak = [jnp.where(strict, m[:c], 0.0) for m in ak]
    l_rk = [jnp.where(lower, m[c:], 0.0) for m in ak]
    def mm1(a, b_bd):
        return jnp.dot(a.astype(BF16), b_bd, preferred_element_type=F32)

    def bd1(x, mask):
        tiled = jnp.concatenate([x.astype(BF16)] * hg, axis=0)
        return jnp.where(mask, tiled, jnp.zeros_like(tiled))

    tinv = [eye + m for m in l_ab]
    pw = [mm1(m, bd1(m, mask_ss)) for m in l_ab]
    for _ in range(1, log_c - 1):
        res = [mm1(cat([tinv[n], pw[n]]), bd1(pw[n], mask_ss)) for n in range(nu)]
        tinv = [tinv[n] + res[n][:c] for n in range(nu)]
        pw = [m[c:] for m in res]
    tinv = [tinv[n] + mm1(tinv[n], bd1(pw[n], mask_ss)) for n in range(nu)]
    av = [_mm3(_split2(cat([l_ak[n], l_rk[n]])), _bd_parts(v[units[n]], mask_kv), NN) for n in range(nu)]
    ue = [_split2(cat([b_e[units[n]], k_e[units[n]]])) for n in range(nu)]

    def wide(fn, x):
        return [fn(x[:, :gw]), fn(x[:, gw:])]

    def bd1w(x):
        return jnp.concatenate(wide(lambda h_: bd1(h_, mask_kv), x), axis=1)

    def bd3w(x):
        parts = wide(lambda h_: _bd_parts(h_, mask_kv), x)
        return tuple(jnp.concatenate([parts[0][q], parts[1][q]], axis=1) for q in range(2))

    rhs = [jnp.concatenate([a_t[units[n]], av[n][:c]], axis=1) for n in range(nu)]
    x0 = [mm1(tinv[n], bd1w(rhs[n])) for n in range(nu)]
    resid = [rhs[n] - (x0[n] - _mm3(_split2(l_ab[n]), bd3w(x0[n]), NN)) for n in range(nu)]
    sol = [x0[n] + mm1(tinv[n], bd1w(resid[n])) for n in range(nu)]
    lift = [_mm3(_split2(l_rb[n]), bd3w(sol[n]), NN) for n in range(nu)]
    lhs_s = [_split2(cat([sol[n][:, :gw], r_t[units[n]] + lift[n][:, :gw]])) for n in range(nu)]
    u_loc = [sol[n][:, gw:] for n in range(nu)]
    o_loc = [av[n][c:] + lift[n][:, gw:] for n in range(nu)]

    zr = lax.broadcasted_iota(I32, (gw, gw), 0) >> int(math.log2(HEAD_DIM))
    zmask = zr == (lax.broadcasted_iota(I32, (gw, gw), 1) >> int(math.log2(HEAD_DIM)))
    z = [z_scr[i] for i in range(ng)]
    o_rows = []
    for j in range(n_chunks):
        o_grp = []
        for i in range(ng):
            n = j * ng + i
            rs_, s = units[n]
            sz = _mm3(lhs_s[n], _split2(z[i]), NT)
            u = sz[:c] + u_loc[n]
            o_grp.append(sz[c:] + o_loc[n])
            upd = _mm3(_split2(cat([u, v[rs_, s]])), ue[n], TN)
            z[i] = z[i] * g_end[j * c:j * c + 1, s] + jnp.where(zmask, upd, 0.0)
        o_rows.append(jnp.concatenate(o_grp, axis=1))
    for i in range(ng):
        z_scr[i] = z[i]

    o = cat(o_rows)
    inv_hd = 1.0 / HEAD_DIM
    dlt = o - _gsum(o, gm_ref) * inv_hd
    var = _gsum(dlt * dlt, gm_ref) * inv_hd
    on = dlt * lax.rsqrt(var + RWKV_GN_EPS) * lnw_ref[...] + lnb_ref[...]
    bonus = _gsum(r * kf * rk_ref[...], gm_ref) * v
    y_ref[0] = (on + bonus) * g

    @pl.when(t == nt - 1)
    def _():
        for h in range(N_HEADS):
            i, blk = head_block(h)
            st_ref[0, h] = z_scr[i, blk, blk]


def _rwkv(p, shift0, s0, prm, gmat, chunk, chunks_per_step):
    b, t, _ = p.shape
    row = lambda n: _const_spec((1, n))
    rows = chunk * chunks_per_step
    return pl.pallas_call(
        functools.partial(_rwkv_kernel, c=chunk),
        grid=(b, t // rows),
        in_specs=[pl.BlockSpec((1, rows, RWKV_COLS), lambda i, j: (i, j, 0)),
                  pl.BlockSpec((1, 1, RWKV_COLS), lambda i, j: (i, 0, 0)),
                  pl.BlockSpec((1, N_HEADS, HEAD_DIM, HEAD_DIM), lambda i, j: (i, 0, 0, 0)),
                  row(RWKV_COLS), row(WIDTH), _const_spec((DECAY_LORA, WIDTH)), row(WIDTH),
                  _const_spec((ICLR_LORA, WIDTH)), _const_spec((GATE_LORA, WIDTH)),
                  row(WIDTH), row(WIDTH), row(WIDTH), row(WIDTH), row(WIDTH), _const_spec((WIDTH, WIDTH))],
        out_specs=[pl.BlockSpec((1, rows, WIDTH), lambda i, j: (i, j, 0)),
                   pl.BlockSpec((1, N_HEADS, HEAD_DIM, HEAD_DIM), lambda i, j: (i, 0, 0, 0)),
                   pl.BlockSpec((1, 1, RWKV_COLS), lambda i, j: (i, 0, 0))],
        out_shape=[jax.ShapeDtypeStruct((b, t, WIDTH), F32),
                   jax.ShapeDtypeStruct((b, N_HEADS, HEAD_DIM, HEAD_DIM), F32),
                   jax.ShapeDtypeStruct((b, 1, RWKV_COLS), F32)],
        scratch_shapes=[pltpu.VMEM((N_HEADS // HEADS_PER_GROUP, HEADS_PER_GROUP * HEAD_DIM,
                                    HEADS_PER_GROUP * HEAD_DIM), F32),
                        pltpu.VMEM((1, RWKV_COLS), F32)],
        compiler_params=_cparams(("parallel", "arbitrary")),
        name="rwkv7_mix",
    )(p, shift0, s0, prm["mu"], prm["w0"], prm["wb"], prm["a0"], prm["ab"], prm["gb"], prm["kk"], prm["ka"],
      prm["rk"], prm["lnw"], prm["lnb"], gmat)


def _merge_kernel(x_ref, ya_ref, yb_ref, gate_ref, mod_ref, g2_ref, woa_ref, wob_ref, wo_ref, wrh_ref, wrl_ref,
                  *rest):
    x1_ref, h2_ref, lg_ref = rest[-3:]
    bb, tt, d = x_ref.shape
    m = bb * tt
    gate = gate_ref[...].reshape(m, GATE_COLS)
    merged = (gate[:, 0:d] * _dot(ya_ref[...].reshape(m, WIDTH), woa_ref[...])
              + gate[:, d:2 * d] * _dot(yb_ref[...].reshape(m, WIDTH), wob_ref[...]))
    x1 = x_ref[...] + mod_ref[:, 2:3, :] * _dot(merged, wo_ref[...]).reshape(bb, tt, d)
    x1_ref[...] = x1
    ms = jnp.mean(x1 * x1, axis=-1, keepdims=True)
    h2 = x1 * lax.rsqrt(ms + RMS_EPS) * g2_ref[...]
    h2 = (h2 * (1.0 + mod_ref[:, 4:5, :]) + mod_ref[:, 3:4, :]).reshape(m, d)
    h2_ref[...] = h2
    lg_ref[...] = _mm3((wrh_ref[...], wrl_ref[...]), _split2(h2), NT)


def _merge(x, ya, yb, gate, mod, w, n_total, row_offset, shared=None):
    b, t, d = x.shape
    bb, tt = _token_blocks(b, t)
    nt = t // tt
    m = bb * tt
    off = row_offset // m

    def tok(cols):
        return pl.BlockSpec((bb, tt, cols), lambda i, j: (i, j, 0))

    in_specs = [tok(d), tok(WIDTH), tok(WIDTH), tok(GATE_COLS),
                pl.BlockSpec((bb, 6, d), lambda i, j: (i, 0, 0)),
                _const_spec((1, d)), _const_spec((WIDTH, d)), _const_spec((WIDTH, d)), _const_spec((d, d)),
                _const_spec((N_EXPERTS, d)), _const_spec((N_EXPERTS, d))]
    args = [x, ya, yb, gate, mod.reshape(b, 6, d), w["g2"], w["w_oa"], w["w_ob"], w["w_o"], w["wr_hi"], w["wr_lo"]]
    aliases = {}
    if shared is not None:
        aliases = {len(args): 1, len(args) + 1: 2}
        in_specs += [pl.BlockSpec(memory_space=pl.ANY), pl.BlockSpec(memory_space=pl.ANY)]
        args += list(shared)
    return pl.pallas_call(
        _merge_kernel,
        grid=(b // bb, nt),
        in_specs=in_specs,
        out_specs=[tok(d), pl.BlockSpec((m, d), lambda i, j: (off + i * nt + j, 0)),
                   pl.BlockSpec((N_EXPERTS, m), lambda i, j: (0, off + i * nt + j))],
        out_shape=[jax.ShapeDtypeStruct((b, t, d), F32), jax.ShapeDtypeStruct((n_total, d), F32),
                   jax.ShapeDtypeStruct((N_EXPERTS, n_total), F32)],
        input_output_aliases=aliases,
        compiler_params=_cparams(("parallel", "parallel")),
        name="merge_norm2_router",
    )(*args)


def _route_kernel(lg_ref, bias_ref, idx_ref, wt_ref, rank_ref, cnt_ref, carry):
    @pl.when(pl.program_id(0) == 0)
    def _():
        carry[...] = jnp.zeros(carry.shape, F32)

    tm = lg_ref.shape[1]
    scores = _sigmoid(lg_ref[...])
    sel = scores + bias_ref[...]
    row = lax.broadcasted_iota(I32, (N_EXPERTS, tm), 0)
    neg_inf = -jnp.inf

    def first_argmax(vals, rows):
        mx = jnp.max(vals, axis=0, keepdims=True)
        return mx, jnp.min(jnp.where(vals == mx, rows, N_EXPERTS), axis=0, keepdims=True)

    gslices = [slice(g * EXPERTS_PER_GROUP, (g + 1) * EXPERTS_PER_GROUP) for g in range(N_GROUPS)]
    gs = []
    row_g = lax.broadcasted_iota(I32, (EXPERTS_PER_GROUP, tm), 0)
    for sl in gslices:
        m1, i1 = first_argmax(sel[sl], row_g)
        m2 = jnp.max(jnp.where(row_g == i1, neg_inf, sel[sl]), axis=0, keepdims=True)
        gs.append(m1 + m2)
    kept = []
    for g in range(N_GROUPS):
        beaten = jnp.zeros((1, tm), I32)
        for o in range(N_GROUPS):
            if o != g:
                wins = (gs[o] >= gs[g]) if o < g else (gs[o] > gs[g])
                beaten = beaten + wins.astype(I32)
        kept.append(jnp.where(beaten < TOPK_GROUPS, sel[gslices[g]], neg_inf))
    cur = jnp.concatenate(kept, axis=0)

    idxs, ws = [], []
    picked = jnp.zeros((N_EXPERTS, tm), F32)
    for _ in range(TOP_K):
        _, ik = first_argmax(cur, row)
        hit = row == ik
        idxs.append(ik)
        ws.append(jnp.sum(jnp.where(hit, scores, 0.0), axis=0, keepdims=True))
        cur = jnp.where(hit, neg_inf, cur)
        picked = jnp.where(hit, 1.0, picked)
    wsum = ws[0]
    for k in range(1, TOP_K):
        wsum = wsum + ws[k]

    r = lax.broadcasted_iota(I32, (tm, tm), 0)
    c = lax.broadcasted_iota(I32, (tm, tm), 1)
    before = jnp.dot(picked.astype(BF16), (r < c).astype(BF16), preferred_element_type=F32) + carry[...]
    carry[...] = carry[...] + jnp.sum(picked, axis=1, keepdims=True)
    cnt_ref[...] = carry[...]

    kk = lax.broadcasted_iota(I32, (TOP_K, tm), 0)
    idx_o = jnp.zeros((TOP_K, tm), I32)
    wt_o = jnp.zeros((TOP_K, tm), F32)
    rank_o = jnp.zeros((TOP_K, tm), F32)
    for k in range(TOP_K):
        rk = jnp.sum(jnp.where(row == idxs[k], before, 0.0), axis=0, keepdims=True)
        idx_o = jnp.where(kk == k, idxs[k], idx_o)
        wt_o = jnp.where(kk == k, ws[k] / wsum * ROUTED_SCALE, wt_o)
        rank_o = jnp.where(kk == k, rk, rank_o)
    idx_ref[...] = idx_o
    wt_ref[...] = wt_o
    rank_ref[...] = rank_o.astype(I32)


def _route(logits_t, bias_col, tm):
    n = logits_t.shape[1]
    tokk = pl.BlockSpec((TOP_K, tm), lambda i: (0, i))
    return pl.pallas_call(
        _route_kernel,
        grid=(n // tm,),
        in_specs=[pl.BlockSpec((N_EXPERTS, tm), lambda i: (0, i)), _const_spec((N_EXPERTS, 1))],
        out_specs=[tokk, tokk, tokk, _const_spec((N_EXPERTS, 1))],
        out_shape=[jax.ShapeDtypeStruct((TOP_K, n), I32), jax.ShapeDtypeStruct((TOP_K, n), F32),
                   jax.ShapeDtypeStruct((TOP_K, n), I32), jax.ShapeDtypeStruct((N_EXPERTS, 1), F32)],
        scratch_shapes=[pltpu.VMEM((N_EXPERTS, 1), F32)],
        compiler_params=_cparams(("arbitrary",)),
        name="route_topk",
    )(logits_t, bias_col)


def _plan_kernel(cnt_ref, start_ref, be_ref, valid_ref, nu_ref, *, blk):
    cnt = cnt_ref[...]
    padded = jnp.ceil(cnt * (1.0 / blk)) * blk
    e_r = lax.broadcasted_iota(I32, (N_EXPERTS, N_EXPERTS), 0)
    e_c = lax.broadcasted_iota(I32, (N_EXPERTS, N_EXPERTS), 1)
    incl = (e_r <= e_c).astype(BF16)
    ph, pm, plo = _split3(jnp.broadcast_to(padded, (8, N_EXPERTS)))
    d2 = functools.partial(jnp.dot, preferred_element_type=F32)
    pad_end = (d2(ph, incl) + (d2(pm, incl) + d2(plo, incl)))[0:1, :]
    pad_start = pad_end - padded
    start_ref[...] = pad_start.astype(I32)
    total = jnp.max(pad_end, axis=-1, keepdims=True)
    nu_ref[...] = jnp.broadcast_to(total * (1.0 / blk), (1, N_EXPERTS)).astype(I32)
    nb = be_ref.shape[0]
    first = (lax.broadcasted_iota(I32, (nb, N_EXPERTS), 0) * blk).astype(F32)
    lane = lax.broadcasted_iota(I32, (nb, N_EXPERTS), 1)
    inside = jnp.logical_and(pad_start <= first, first < pad_end)
    be_ref[...] = jnp.sum(jnp.where(inside, lane, 0), axis=-1, keepdims=True)
    rows = jnp.minimum(pad_start + cnt - first, float(blk))
    valid_ref[...] = jnp.sum(jnp.where(inside, rows, 0.0), axis=-1, keepdims=True).astype(I32)


def _plan(counts, n_blocks, blk):
    return pl.pallas_call(
        functools.partial(_plan_kernel, blk=blk),
        out_shape=[jax.ShapeDtypeStruct((1, N_EXPERTS), I32), jax.ShapeDtypeStruct((n_blocks, 1), I32),
                   jax.ShapeDtypeStruct((n_blocks, 1), I32), jax.ShapeDtypeStruct((1, N_EXPERTS), I32)],
        compiler_params=pltpu.CompilerParams(vmem_limit_bytes=VMEM_LIMIT),
        name="dispatch_plan",
    )(counts)


def _dest_kernel(idx_ref, rank_ref, start_ref, dest_ref):
    tm = idx_ref.shape[1]
    row = lax.broadcasted_iota(I32, (N_EXPERTS, tm), 0)
    kk = lax.broadcasted_iota(I32, (TOP_K, tm), 0)
    idx = idx_ref[...]
    base = jnp.zeros((TOP_K, tm), I32)
    for k in range(TOP_K):
        bk = jnp.sum(jnp.where(row == idx[k:k + 1, :], start_ref[...], 0), axis=0, keepdims=True)
        base = jnp.where(kk == k, bk, base)
    dest_ref[...] = base + rank_ref[...]


def _dest(idx, rank, pad_start_col, tm):
    n = idx.shape[1]
    tokk = pl.BlockSpec((TOP_K, tm), lambda i: (0, i))
    return pl.pallas_call(
        _dest_kernel,
        grid=(n // tm,),
        in_specs=[tokk, tokk, _const_spec((N_EXPERTS, 1))],
        out_specs=tokk,
        out_shape=jax.ShapeDtypeStruct((TOP_K, n), I32),
        compiler_params=_cparams(("parallel",)),
        name="dispatch_dest",
    )(idx, rank, pad_start_col)


def _slot_tokens(dest_t, n_slots):
    k, n = dest_t.shape
    m = k * n
    window = SC_SCATTER_WINDOW
    tok_rows = jnp.broadcast_to((jnp.arange(m, dtype=I32) % n)[:, None], (m, LANES))
    mesh = plsc.VectorSubcoreMesh(core_axis_name="core", subcore_axis_name="subcore")

    @functools.partial(pl.kernel, out_type=jax.ShapeDtypeStruct((n_slots, LANES), I32), mesh=mesh,
                       scratch_types=[])
    def scatter(rows_hbm, idx_hbm, out_hbm):
        def body(rows_vmem, idx_vmem):
            pltpu.sync_copy(rows_vmem, out_hbm.at[idx_vmem.at[0]])

        pltpu.emit_pipeline(
            body, grid=(m // window,),
            in_specs=[pl.BlockSpec((window, LANES), index_map=lambda i: (i, 0)),
                      pl.BlockSpec((1, window), index_map=lambda i: (0, i))],
            out_specs=[], core_axis_name="subcore", dimension_semantics=(pltpu.PARALLEL,),
        )(rows_hbm, idx_hbm)

    return jnp.clip(scatter(tok_rows, dest_t.reshape(1, m))[:, 0], 0, n - 1)


def _expert_kernel(be_ref, valid_ref, nu_ref, tok_ref, tok_next_ref, h2_ref, wg_ref, wu_ref, wd_ref, y_ref,
                   xbuf, sems, wg_b, wu_b, wd_b):
    i = pl.program_id(0)
    n_used = nu_ref[0]
    blk = xbuf.shape[1]
    slot = i % 2

    def gather(toks, buf):
        for r in range(blk):
            pltpu.make_async_copy(h2_ref.at[pl.ds(toks[0, 0, r], 1)], xbuf.at[buf, pl.ds(r, 1)],
                                  sems.at[buf]).start()

    def wait(buf):
        pltpu.make_async_copy(h2_ref.at[pl.ds(0, blk)], xbuf.at[buf], sems.at[buf]).wait()

    @pl.when(i == 0)
    def _():
        gather(tok_ref, 0)

    new_expert = jnp.logical_or(i == 0, be_ref[i] != be_ref[jnp.maximum(i - 1, 0)])

    @pl.when(jnp.logical_and(i < n_used, new_expert))
    def _():
        wg_b[...] = wg_ref[0].astype(BF16)
        wu_b[...] = wu_ref[0].astype(BF16)
        wd_b[...] = wd_ref[0].astype(BF16)

    def process(cur, nxt):
        wait(cur)
        rows = lax.broadcasted_iota(I32, (blk, 1), 0)
        x = jnp.where(rows < valid_ref[i], xbuf[cur], 0.0).astype(BF16)
        gather(tok_next_ref, nxt)
        hg = jnp.dot(x, wg_b[...], preferred_element_type=F32)
        hu = jnp.dot(x, wu_b[...], preferred_element_type=F32)
        y_ref[...] = jnp.dot((_silu(hg) * hu).astype(BF16), wd_b[...], preferred_element_type=F32)

        @pl.when(i == n_used - 1)
        def _():
            wait(nxt)

    for cur in range(2):
        pl.when(jnp.logical_and(i < n_used, slot == cur))(functools.partial(process, cur, 1 - cur))


def _experts(h2, slot_tok, block_e, valid, n_used, w_eg, w_eu, w_ed, blk):
    n_tok, d = h2.shape
    n_blocks = slot_tok.shape[0] // blk
    toks = slot_tok.reshape(n_blocks, 1, blk)

    def row_blk(i, be, valid, nu):
        return (jnp.minimum(i, nu[0] - 1), 0)

    def w_blk(i, be, valid, nu):
        return (be[i], 0, 0)

    def tok_blk(ahead):
        return pl.BlockSpec((1, 1, blk), lambda i, be, valid, nu: (jnp.minimum(i + ahead, nu[0] - 1), 0, 0),
                            memory_space=pltpu.SMEM)

    return pl.pallas_call(
        _expert_kernel,
        grid_spec=pltpu.PrefetchScalarGridSpec(
            num_scalar_prefetch=3,
            grid=(n_blocks,),
            in_specs=[tok_blk(0), tok_blk(1), pl.BlockSpec(memory_space=pl.ANY),
                      pl.BlockSpec((1, d, D_EXPERT), w_blk), pl.BlockSpec((1, d, D_EXPERT), w_blk),
                      pl.BlockSpec((1, D_EXPERT, d), w_blk)],
            out_specs=pl.BlockSpec((blk, d), row_blk),
            scratch_shapes=[pltpu.VMEM((2, blk, d), F32), pltpu.SemaphoreType.DMA((2,)),
                            pltpu.VMEM((d, D_EXPERT), BF16), pltpu.VMEM((d, D_EXPERT), BF16),
                            pltpu.VMEM((D_EXPERT, d), BF16)]),
        out_shape=jax.ShapeDtypeStruct((n_blocks * blk, d), F32),
        compiler_params=_cparams(("arbitrary",)),
        name="moe_experts",
    )(block_e, valid, n_used, toks, toks, h2, w_eg, w_eu, w_ed)


def _final_kernel(dest_ref, x1_ref, h2_ref, wt_ref, mod_ref, wsg_ref, wsu_ref, wsd_ref, ys_ref, o_ref, ybuf, sem):
    bb, tt, d = x1_ref.shape
    m = bb * tt

    def issue(r, carry):
        for k in range(TOP_K):
            pltpu.make_async_copy(ys_ref.at[pl.ds(dest_ref[k, r], 1)], ybuf.at[k, pl.ds(r, 1)], sem).start()
        return carry

    lax.fori_loop(0, m, issue, 0)

    hb = h2_ref[...].astype(BF16)
    hg = jnp.dot(hb, wsg_ref[...], preferred_element_type=F32)
    hu = jnp.dot(hb, wsu_ref[...], preferred_element_type=F32)
    ffn = _dot(_silu(hg) * hu, wsd_ref[...])

    for k in range(TOP_K):
        pltpu.make_async_copy(ys_ref.at[pl.ds(0, m)], ybuf.at[k], sem).wait()

    wt = wt_ref[...]
    for k in range(TOP_K):
        ffn = ffn + wt[:, k:k + 1] * ybuf[k]
    o_ref[...] = x1_ref[...] + mod_ref[:, 5:6, :] * ffn.reshape(bb, tt, d)


def _final(x1, h2_all, wts_all, dest_t, ys, mod, w, row_offset):
    b, t, d = x1.shape
    bb, tt = _token_blocks(b, t)
    nt = t // tt
    m = bb * tt
    off = row_offset // m

    def flat_idx(i, j):
        return off + i * nt + j

    return pl.pallas_call(
        _final_kernel,
        grid=(b // bb, nt),
        in_specs=[pl.BlockSpec((TOP_K, m), lambda i, j: (0, flat_idx(i, j)), memory_space=pltpu.SMEM),
                  pl.BlockSpec((bb, tt, d), lambda i, j: (i, j, 0)),
                  pl.BlockSpec((m, d), lambda i, j: (flat_idx(i, j), 0)),
                  pl.BlockSpec((m, TOP_K), lambda i, j: (flat_idx(i, j), 0)),
                  pl.BlockSpec((bb, 6, d), lambda i, j: (i, 0, 0)),
                  _const_spec((d, D_EXPERT)), _const_spec((d, D_EXPERT)), _const_spec((D_EXPERT, d)),
                  pl.BlockSpec(memory_space=pl.ANY)],
        out_specs=pl.BlockSpec((bb, tt, d), lambda i, j: (i, j, 0)),
        out_shape=jax.ShapeDtypeStruct((b, t, d), F32),
        scratch_shapes=[pltpu.VMEM((TOP_K, m, d), F32), pltpu.SemaphoreType.DMA(())],
        compiler_params=_cparams(("arbitrary", "arbitrary")),
        name="moe_combine_final",
    )(dest_t, x1, h2_all, wts_all, mod.reshape(b, 6, d), w["w_sg"], w["w_su"], w["w_sd"], ys)


def _moe_routed(h2_all, logits_all, w, blk=256, tm=256):
    n = h2_all.shape[0]
    n_blocks = (n * TOP_K + N_EXPERTS * (blk - 1)) // blk + 1
    n_blocks = (n_blocks + 7) // 8 * 8
    idx, wts_t, rank, counts = _route(logits_all, w["router_bias"], tm)
    pad_start, block_e, valid, n_used = _plan(counts.reshape(1, N_EXPERTS), n_blocks, blk)
    block_e = block_e.reshape(n_blocks)
    valid = valid.reshape(n_blocks)
    n_used = n_used[0, 0:1]
    dest_t = _dest(idx, rank, pad_start.reshape(N_EXPERTS, 1), tm)
    slot_tok = _slot_tokens(dest_t, n_blocks * blk)
    ys = _experts(h2_all, slot_tok, block_e, valid, n_used, w["w_eg"], w["w_eu"], w["w_ed"], blk)
    return ys, dest_t, jnp.transpose(wts_t)


def _prep(raw):
    p = {k: v[0] for k, v in raw.items()}
    w_in = p["w_in"]
    o_fox = RWKV_COLS
    o_fl = o_fox + FOX_MAIN_COLS
    o_gate = o_fl + N_HEADS
    row = lambda a: a.reshape(1, -1)
    return dict(
        w_ada=p["w_ada"], b_ada=p["b_ada"],
        g1=row(p["norm1_g"]), g2=row(p["norm2_g"]),
        wr=w_in[:, :o_fox].astype(BF16),
        wf=w_in[:, o_fox:o_fl].astype(BF16),
        wfl=jnp.pad(w_in[:, o_fl:o_gate], ((0, 0), (0, LANES - N_HEADS))).astype(BF16),
        wg=w_in[:, o_gate:].astype(BF16),
        qn=row(jnp.tile(p["fox_q_norm"], N_HEADS)), kn=row(jnp.tile(p["fox_k_norm"], N_HEADS)),
        fb=row(p["fox_f_bias"]),
        gmat=_group_ones(),
        rwkv=dict(mu=row(p["rwkv_mu"]), w0=row(p["rwkv_w0"]), wb=p["rwkv_w_lora_b"], a0=row(p["rwkv_a0"]),
                  ab=p["rwkv_a_lora_b"], gb=p["rwkv_g_lora_b"], kk=row(p["rwkv_k_k"]), ka=row(p["rwkv_k_a"]),
                  rk=row(p["rwkv_r_k"]), lnw=row(p["rwkv_ln_w"]), lnb=row(p["rwkv_ln_b"])),
        w_oa=p["w_out_rwkv"].astype(BF16), w_ob=p["w_out_fox"].astype(BF16), w_o=p["w_out"].astype(BF16),
        wr_hi=p["w_router"].T.astype(BF16),
        wr_lo=(p["w_router"] - p["w_router"].astype(BF16).astype(F32)).T.astype(BF16),
        router_bias=p["router_bias"].reshape(N_EXPERTS, 1),
        w_eg=p["w_exp_gate"], w_eu=p["w_exp_up"], w_ed=p["w_exp_down"],
        w_sg=p["w_sh_gate"].astype(BF16), w_su=p["w_sh_up"].astype(BF16), w_sd=p["w_sh_down"].astype(BF16),
    )


def _token_blocks(b, t):
    if t >= 256:
        return 1, 256
    bb = max(1, min(b, 256 // t))
    while b % bb:
        bb -= 1
    return bb, t


def _mix_path(x, mod, shift0, wkv0, past_k, past_v, past_logf, w):
    b, t, d = x.shape
    bb, tt = _token_blocks(b, t)
    n_past = past_k.shape[1]
    if n_past:
        f_past = _past_cumsum(past_logf)
        init = f_past[:, n_past - 1:n_past, :]
        past = (past_k, past_v, jnp.swapaxes(f_past, 1, 2))
    else:
        init = jnp.zeros((b, 1, N_HEADS), F32)
        past = None
    pr, q, k, v, sg, logf, f_new, gate = _inproj(x, mod.reshape(b, 6, d), w["g1"], w["wr"], w["wf"], w["wfl"],
                                                 w["wg"], w["qn"], w["kn"], w["fb"], w["gmat"], init, bb, tt)
    y_fox = _fox_attention(q, f_new, sg, k, v, jnp.swapaxes(f_new, 1, 2), past=past, tq=min(t, 512),
                           tk_past=min(max(n_past, 1), 512))
    chunk = min(t, RWKV_CHUNK)
    y_rwkv, wkv_new, shift_new = _rwkv(pr, shift0.reshape(b, 1, RWKV_COLS), wkv0, w["rwkv"], w["gmat"],
                                       chunk, max(1, min(RWKV_CHUNKS_PER_STEP, t // chunk)))
    return y_rwkv, y_fox, gate, wkv_new, shift_new, k, v, logf


def _layer(paths, w):
    n_b = [p[0].shape[0] for p in paths]
    mod_all = _ada(jnp.concatenate([p[1] for p in paths], axis=0), w["w_ada"], w["b_ada"])
    mods, o = [], 0
    for nb in n_b:
        mods.append(mod_all[o:o + nb])
        o += nb
    n_total = sum(p[0].shape[0] * p[0].shape[1] for p in paths)
    mixed, x1s = [], []
    shared, row = None, 0
    for (x, _, shift0, wkv0, pk, pv, plf), mod in zip(paths, mods):
        ya, yb, gate, wkv_new, shift_new, k, v, logf = _mix_path(x, mod, shift0, wkv0, pk, pv, plf, w)
        x1, h2_all, lg_all = _merge(x, ya, yb, gate, mod, w, n_total, row, shared)
        shared = (h2_all, lg_all)
        row += x.shape[0] * x.shape[1]
        mixed.append((wkv_new, shift_new, k, v, logf))
        x1s.append(x1)
    ys, dest_t, wts = _moe_routed(h2_all, lg_all, w)
    outs, row = [], 0
    for x1, mod, st in zip(x1s, mods, mixed):
        y = _final(x1, h2_all, wts, dest_t, ys, mod, w, row)
        row += x1.shape[0] * x1.shape[1]
        outs.append((y,) + st)
    return outs


def kernel(x_prompt, x_sample, c_prompt, c_sample, state_rwkv_wkv, state_rwkv_shift, cache_fox_k, cache_fox_v,
           cache_fox_logf, w_ada, b_ada, norm1_g, norm2_g, w_in, rwkv_mu, rwkv_w0, rwkv_w_lora_b, rwkv_a0,
           rwkv_a_lora_b, rwkv_g_lora_b, rwkv_k_k, rwkv_k_a, rwkv_r_k, rwkv_ln_w, rwkv_ln_b, fox_q_norm,
           fox_k_norm, fox_f_bias, w_out_rwkv, w_out_fox, w_out, w_router, router_bias, w_exp_gate, w_exp_up,
           w_exp_down, w_sh_gate, w_sh_up, w_sh_down):
    raw = dict(w_ada=w_ada, b_ada=b_ada, norm1_g=norm1_g, norm2_g=norm2_g, w_in=w_in, rwkv_mu=rwkv_mu,
               rwkv_w0=rwkv_w0, rwkv_w_lora_b=rwkv_w_lora_b, rwkv_a0=rwkv_a0, rwkv_a_lora_b=rwkv_a_lora_b,
               rwkv_g_lora_b=rwkv_g_lora_b, rwkv_k_k=rwkv_k_k, rwkv_k_a=rwkv_k_a, rwkv_r_k=rwkv_r_k,
               rwkv_ln_w=rwkv_ln_w, rwkv_ln_b=rwkv_ln_b, fox_q_norm=fox_q_norm, fox_k_norm=fox_k_norm,
               fox_f_bias=fox_f_bias, w_out_rwkv=w_out_rwkv, w_out_fox=w_out_fox, w_out=w_out,
               w_router=w_router, router_bias=router_bias, w_exp_gate=w_exp_gate, w_exp_up=w_exp_up,
               w_exp_down=w_exp_down, w_sh_gate=w_sh_gate, w_sh_up=w_sh_up, w_sh_down=w_sh_down)
    assert w_in.shape[0] == 1, "single-layer stack"
    w = _prep(raw)
    bp, tp, _ = x_prompt.shape
    bs, ts, _ = x_sample.shape
    n_past = cache_fox_k.shape[2]
    prompt = (x_prompt, c_prompt, jnp.zeros((bp, RWKV_COLS), F32),
              jnp.zeros((bp, N_HEADS, HEAD_DIM, HEAD_DIM), F32),
              jnp.zeros((bp, 0, WIDTH), F32), jnp.zeros((bp, 0, WIDTH), F32), jnp.zeros((bp, 0, N_HEADS), F32))
    sample = (x_sample, c_sample, state_rwkv_shift[0], state_rwkv_wkv[0],
              cache_fox_k[0].reshape(bs, n_past, WIDTH), cache_fox_v[0].reshape(bs, n_past, WIDTH),
              cache_fox_logf[0])
    (yp, wkv_p, sh_p, k_p, v_p, lf_p), (ysm, wkv_s, sh_s, k_s, v_s, lf_s) = _layer([prompt, sample], w)

    def heads(a):
        return a.reshape((1,) + a.shape[:2] + (N_HEADS, HEAD_DIM))

    return (yp, ysm,
            wkv_p[None], sh_p.reshape(1, bp, RWKV_COLS), heads(k_p), heads(v_p), lf_p[None],
            wkv_s[None], sh_s.reshape(1, bs, RWKV_COLS), heads(k_s), heads(v_s), lf_s[None])
```

```python
import functools
import math

import jax
import jax.numpy as jnp
from jax import lax
from jax.experimental import pallas as pl
from jax.experimental.pallas import tpu as pltpu
from jax.experimental.pallas import tpu_sc as plsc

F32 = jnp.float32
BF16 = jnp.bfloat16
I32 = jnp.int32

D_MODEL = 1024
N_HEADS = 8
HEAD_DIM = 64
WIDTH = N_HEADS * HEAD_DIM
HEADS_PER_GROUP = 4
RWKV_CHUNK = 64
RWKV_CHUNKS_PER_STEP = 4
GATHER_BUFFERS = 3
SC_SCATTER_WINDOW = 128
DECAY_LORA = 64
ICLR_LORA = 64
GATE_LORA = 128
RWKV_COLS = 3 * WIDTH + DECAY_LORA + ICLR_LORA + GATE_LORA
FOX_MAIN_COLS = 4 * WIDTH
GATE_COLS = 2 * D_MODEL
RWKV_GN_EPS = HEAD_DIM * 1e-5
L2_EPS = 1e-12
RMS_EPS = 1e-6
N_EXPERTS = 256
TOP_K = 8
N_GROUPS = 8
TOPK_GROUPS = 4
EXPERTS_PER_GROUP = N_EXPERTS // N_GROUPS
D_EXPERT = 256
ROUTED_SCALE = 2.5

LANES = 128
VMEM_LIMIT = 56 * 1024 * 1024
NEG_BIG = -1e30

NN = (((1,), (0,)), ((), ()))
NT = (((1,), (1,)), ((), ()))
TN = (((0,), (0,)), ((), ()))


def _cparams(sem):
    return pltpu.CompilerParams(dimension_semantics=sem, vmem_limit_bytes=VMEM_LIMIT)


def _dot(a, b, dims=NN):
    return lax.dot_general(a.astype(BF16), b.astype(BF16), dims, preferred_element_type=F32)


def _split2(a):
    hi = a.astype(BF16)
    lo = (a - hi.astype(F32)).astype(BF16)
    return hi, lo


def _split3(a):
    hi = a.astype(BF16)
    r1 = a - hi.astype(F32)
    mid = r1.astype(BF16)
    lo = (r1 - mid.astype(F32)).astype(BF16)
    return hi, mid, lo


def _dot3(a, b, dims=NN):
    ah, al = _split2(a)
    bh, bl = _split2(b)
    d = functools.partial(lax.dot_general, dimension_numbers=dims, preferred_element_type=F32)
    return d(ah, bh) + (d(ah, bl) + d(al, bh))


def _mm3(a, b, dims):
    d = functools.partial(lax.dot_general, dimension_numbers=dims, preferred_element_type=F32)
    return d(a[0], b[0]) + (d(a[0], b[1]) + d(a[1], b[0]))


def _bd_parts(x, mask):
    out = []
    for part in _split2(x):
        tiled = jnp.concatenate([part] * HEADS_PER_GROUP, axis=0)
        out.append(jnp.where(mask, tiled, jnp.zeros_like(tiled)))
    return tuple(out)


def _dot_exact_rhs(a_exact, b, dims=NN):
    ab = a_exact.astype(BF16)
    bh, bm, bl = _split3(b)
    d = functools.partial(lax.dot_general, dimension_numbers=dims, preferred_element_type=F32)
    return d(ab, bh) + (d(ab, bm) + d(ab, bl))


def _gsum(x, g_ref):
    hi, mid, lo = _split3(x)
    g = g_ref[...]
    d = functools.partial(jnp.dot, preferred_element_type=F32)
    return d(hi, g) + (d(mid, g) + d(lo, g))


def _sigmoid(x):
    return 1.0 / (1.0 + jnp.exp(-x))


def _softplus(x):
    return jnp.maximum(x, 0.0) + jnp.log1p(jnp.exp(-jnp.abs(x)))


def _silu(x):
    return x * _sigmoid(x)


def _group_ones():
    h = jnp.arange(WIDTH, dtype=I32) // HEAD_DIM
    return (h[:, None] == h[None, :]).astype(BF16)


def _ada_kernel(c_ref, w_ref, b_ref, o_ref):
    o_ref[...] = _dot(_silu(c_ref[...]), w_ref[...]) + b_ref[...]


def _ada(c, w_ada, b_ada):
    nb = c.shape[0]
    n_out = w_ada.shape[1]
    blk = D_MODEL
    return pl.pallas_call(
        _ada_kernel,
        grid=(n_out // blk,),
        in_specs=[pl.BlockSpec((nb, D_MODEL), lambda j: (0, 0)),
                  pl.BlockSpec((D_MODEL, blk), lambda j: (0, j)),
                  pl.BlockSpec((1, blk), lambda j: (0, j))],
        out_specs=pl.BlockSpec((nb, blk), lambda j: (0, j)),
        out_shape=jax.ShapeDtypeStruct((nb, n_out), F32),
        compiler_params=_cparams(("parallel",)),
        name="ada_mod",
    )(c, w_ada, b_ada.reshape(1, n_out))


def _inproj_kernel(x_ref, mod_ref, g1_ref, wr_ref, wf_ref, wfl_ref, wg_ref, qn_ref, kn_ref, fb_ref, gm_ref, f0_ref,
                   pr_ref, q_ref, k_ref, v_ref, sg_ref, lf_ref, cf_ref, gate_ref, carry):
    bb, tt, d = x_ref.shape
    m = bb * tt
    x = x_ref[...]
    ms = jnp.mean(x * x, axis=-1, keepdims=True)
    h = x * lax.rsqrt(ms + RMS_EPS) * g1_ref[...]
    h = h * (1.0 + mod_ref[:, 1:2, :]) + mod_ref[:, 0:1, :]
    hb = h.reshape(m, d).astype(BF16)

    pr_ref[...] = jnp.dot(hb, wr_ref[...], preferred_element_type=F32).reshape(bb, tt, RWKV_COLS)

    f = jnp.dot(hb, wf_ref[...], preferred_element_type=F32)
    q = f[:, 0:WIDTH]
    k = f[:, WIDTH:2 * WIDTH]
    v = f[:, 2 * WIDTH:3 * WIDTH]
    og = f[:, 3 * WIDTH:4 * WIDTH]
    inv_hd = 1.0 / HEAD_DIM
    q = q * lax.rsqrt(_gsum(q * q, gm_ref) * inv_hd + RMS_EPS) * qn_ref[...]
    k = k * lax.rsqrt(_gsum(k * k, gm_ref) * inv_hd + RMS_EPS) * kn_ref[...]
    q_ref[...] = (q * (HEAD_DIM ** -0.5)).astype(BF16).reshape(bb, tt, WIDTH)
    k_ref[...] = k.reshape(bb, tt, WIDTH)
    v_ref[...] = v.reshape(bb, tt, WIDTH)
    sg_ref[...] = _sigmoid(og).reshape(bb, tt, WIDTH)

    fl = jnp.dot(hb, wfl_ref[...], preferred_element_type=F32)[:, 0:N_HEADS] + fb_ref[...]
    lf = -_softplus(-fl)
    lf_ref[...] = lf.reshape(bb, tt, N_HEADS)

    @pl.when(pl.program_id(1) == 0)
    def _():
        carry[...] = f0_ref[...]

    r = lax.broadcasted_iota(I32, (m, m), 0)
    c = lax.broadcasted_iota(I32, (m, m), 1)
    tri = jnp.logical_and(r // tt == c // tt, r >= c).astype(F32)
    cf = _dot_exact_rhs(tri, lf).reshape(bb, tt, N_HEADS) + carry[...]
    cf_ref[...] = cf
    carry[...] = cf[:, tt - 1:tt, :]

    gate_ref[...] =_sigmoid(jnp.dot(hb, wg_ref[...], preferred_element_type=F32)).reshape(bb, tt, GATE_COLS)


def _const_spec(shape):
    nd = len(shape)
    return pl.BlockSpec(shape, lambda *_: (0,) * nd)


def _inproj(x, mod, g1, wr, wf, wfl, wg, qn, kn, fb, gmat, f0, bb, tt):
    b, t, d = x.shape
    grid = (b // bb, t // tt)

    def tok(cols):
        return pl.BlockSpec((bb, tt, cols), lambda i, j: (i, j, 0))

    out_cols = [(RWKV_COLS, F32), (WIDTH, BF16), (WIDTH, F32), (WIDTH, F32), (WIDTH, F32), (N_HEADS, F32),
                (N_HEADS, F32), (GATE_COLS, F32)]
    return pl.pallas_call(
        _inproj_kernel,
        grid=grid,
        in_specs=[tok(d),
                  pl.BlockSpec((bb, 6, d), lambda i, j: (i, 0, 0)),
                  _const_spec((1, d)),
                  _const_spec(wr.shape), _const_spec(wf.shape), _const_spec(wfl.shape), _const_spec(wg.shape),
                  _const_spec((1, WIDTH)), _const_spec((1, WIDTH)), _const_spec((1, N_HEADS)),
                  _const_spec((WIDTH, WIDTH)),
                  pl.BlockSpec((bb, 1, N_HEADS), lambda i, j: (i, 0, 0))],
        out_specs=[tok(c) for c, _ in out_cols],
        out_shape=[jax.ShapeDtypeStruct((b, t, c), dt) for c, dt in out_cols],
        scratch_shapes=[pltpu.VMEM((bb, 1, N_HEADS), F32)],
        compiler_params=_cparams(("parallel", "arbitrary")),
        name="norm1_inproj",
    )(x, mod, g1, wr, wf, wfl, wg, qn, kn, fb, gmat, f0)


def _past_cumsum_kernel(x_ref, o_ref):
    x = x_ref[0]
    rows = x.shape[0]
    li = lax.broadcasted_iota(I32, (LANES, LANES), 0)
    lj = lax.broadcasted_iota(I32, (LANES, LANES), 1)
    same_head = (li % N_HEADS) == (lj % N_HEADS)
    within = jnp.logical_and(same_head, li // N_HEADS <= lj // N_HEADS).astype(BF16)
    xh, xm, xl = _split3(x)
    d2 = functools.partial(jnp.dot, preferred_element_type=F32)
    in_row = d2(xh, within) + (d2(xm, within) + d2(xl, within))
    sh = same_head.astype(BF16)
    row_tot = d2(xh, sh) + (d2(xm, sh) + d2(xl, sh))
    ri = lax.broadcasted_iota(I32, (rows, rows), 0)
    ci = lax.broadcasted_iota(I32, (rows, rows), 1)
    o_ref[0] = in_row + _dot_exact_rhs((ri > ci).astype(F32), row_tot)


def _past_cumsum(past_logf):
    b, p, h = past_logf.shape
    rows = p * h // LANES
    flat = past_logf.reshape(b, rows, LANES)
    out = pl.pallas_call(
        _past_cumsum_kernel,
        grid=(b,),
        in_specs=[pl.BlockSpec((1, rows, LANES), lambda i: (i, 0, 0))],
        out_specs=pl.BlockSpec((1, rows, LANES), lambda i: (i, 0, 0)),
        out_shape=jax.ShapeDtypeStruct((b, rows, LANES), F32),
        compiler_params=_cparams(("parallel",)),
        name="cache_logf_cumsum",
    )(flat)
    return out.reshape(b, p, h)


def _fox_kernel(*refs, n_past_blocks, tq):
    if n_past_blocks:
        (q_ref, fq_ref, sg_ref, kp_ref, vp_ref, fkp_ref, kn_ref, vn_ref, fkn_ref,
         o_ref, m_scr, l_scr, acc_scr) = refs
    else:
        q_ref, fq_ref, sg_ref, kn_ref, vn_ref, fkn_ref, o_ref, m_scr, l_scr, acc_scr = refs
    qi = pl.program_id(1)
    ki = pl.program_id(2)
    nk = pl.num_programs(2)

    @pl.when(ki == 0)
    def _():
        m_scr[...] = jnp.full(m_scr.shape, NEG_BIG, F32)
        l_scr[...] = jnp.zeros(l_scr.shape, F32)
        acc_scr[...] = jnp.zeros(acc_scr.shape, F32)

    lane_a = lax.broadcasted_iota(I32, (tq, LANES), 1) < HEAD_DIM

    def step(k_ref, v_ref, fk_ref, diag):
        tk = k_ref.shape[1]
        if diag:
            rq = lax.broadcasted_iota(I32, (tq, tk), 0)
            ck = lax.broadcasted_iota(I32, (tq, tk), 1)
            visible = ck <= rq
        fq_all = fq_ref[0]
        pairs = range(N_HEADS // 2)
        cols = [slice(j * LANES, (j + 1) * LANES) for j in pairs]
        scores = []
        for j in pairs:
            qj = q_ref[0, :, cols[j]]
            kb = k_ref[0, :, cols[j]].astype(BF16)
            for hh in range(2):
                h = 2 * j + hh
                qm = jnp.where(lane_a if hh == 0 else jnp.logical_not(lane_a), qj, jnp.zeros_like(qj))
                s = lax.dot_general(qm, kb, NT, preferred_element_type=F32)
                s = s + fq_all[:, h:h + 1] - fk_ref[0, h:h + 1, :]
                if diag:
                    s = jnp.where(visible, s, NEG_BIG)
                scores.append(s)
        alphas, probs = [], []
        for h in range(N_HEADS):
            m_old = m_scr[h]
            m_new = jnp.maximum(m_old, jnp.max(scores[h], axis=-1, keepdims=True))
            alpha = jnp.exp(m_old - m_new)
            p = jnp.exp(scores[h] - m_new)
            l_scr[h] = alpha * l_scr[h] + jnp.sum(p, axis=-1, keepdims=True)
            m_scr[h] = m_new
            alphas.append(alpha)
            probs.append(p.astype(BF16))
        for j in pairs:
            vb = v_ref[0, :, cols[j]].astype(BF16)
            pv0 = jnp.dot(probs[2 * j], vb, preferred_element_type=F32)
            pv1 = jnp.dot(probs[2 * j + 1], vb, preferred_element_type=F32)
            acc_scr[:, cols[j]] = (acc_scr[:, cols[j]] * jnp.where(lane_a, alphas[2 * j], alphas[2 * j + 1])
                                   + jnp.where(lane_a, pv0, pv1))

    if n_past_blocks:
        @pl.when(ki < n_past_blocks)
        def _():
            step(kp_ref, vp_ref, fkp_ref, False)

    kn = ki - n_past_blocks

    @pl.when(jnp.logical_and(kn >= 0, kn < qi))
    def _():
        step(kn_ref, vn_ref, fkn_ref, False)

    @pl.when(kn == qi)
    def _():
        step(kn_ref, vn_ref, fkn_ref, True)

    @pl.when(ki == nk - 1)
    def _():
        for j in range(N_HEADS // 2):
            cols = slice(j * LANES, (j + 1) * LANES)
            l = jnp.where(lane_a, l_scr[2 * j], l_scr[2 * j + 1])
            o_ref[0, :, cols] = acc_scr[:, cols] / l * sg_ref[0, :, cols]


def _fox_attention(q, fq, sg, k_new, v_new, fk_new_t, past=None, tq=512, tk_past=512):
    b, t, _ = q.shape
    nq = t // tq
    n_past_blocks = 0 if past is None else past[0].shape[1] // tk_past
    nk = n_past_blocks + nq

    def new_idx(i, qi, ki):
        return jnp.clip(ki - n_past_blocks, 0, qi)

    in_specs = [pl.BlockSpec((1, tq, WIDTH), lambda i, qi, ki: (i, qi, 0)),
                pl.BlockSpec((1, tq, N_HEADS), lambda i, qi, ki: (i, qi, 0)),
                pl.BlockSpec((1, tq, WIDTH), lambda i, qi, ki: (i, qi, 0))]
    args = [q, fq, sg]
    if n_past_blocks:
        def past_idx(i, qi, ki):
            return jnp.minimum(ki, n_past_blocks - 1)
        in_specs += [pl.BlockSpec((1, tk_past, WIDTH), lambda i, qi, ki: (i, past_idx(i, qi, ki), 0)),
                     pl.BlockSpec((1, tk_past, WIDTH), lambda i, qi, ki: (i, past_idx(i, qi, ki), 0)),
                     pl.BlockSpec((1, N_HEADS, tk_past), lambda i, qi, ki: (i, 0, past_idx(i, qi, ki)))]
        args += list(past)
    in_specs += [pl.BlockSpec((1, tq, WIDTH), lambda i, qi, ki: (i, new_idx(i, qi, ki), 0)),
                 pl.BlockSpec((1, tq, WIDTH), lambda i, qi, ki: (i, new_idx(i, qi, ki), 0)),
                 pl.BlockSpec((1, N_HEADS, tq), lambda i, qi, ki: (i, 0, new_idx(i, qi, ki)))]
    args += [k_new, v_new, fk_new_t]
    return pl.pallas_call(
        functools.partial(_fox_kernel, n_past_blocks=n_past_blocks, tq=tq),
        grid=(b, nq, nk),
        in_specs=in_specs,
        out_specs=pl.BlockSpec((1, tq, WIDTH), lambda i, qi, ki: (i, qi, 0)),
        out_shape=jax.ShapeDtypeStruct((b, t, WIDTH), F32),
        scratch_shapes=[pltpu.VMEM((N_HEADS, tq, 1), F32), pltpu.VMEM((N_HEADS, tq, 1), F32),
                        pltpu.VMEM((tq, WIDTH), F32)],
        compiler_params=_cparams(("parallel", "parallel", "arbitrary")),
        name="fox_attention",
    )(*args)


def _rwkv_kernel(p_ref, sh0_ref, s0_ref, mu_ref, w0_ref, wb_ref, a0_ref, ab_ref, gb_ref, kk_ref, ka_ref, rk_ref,
                 lnw_ref, lnb_ref, gm_ref, y_ref, st_ref, sht_ref, z_scr, prev_scr, *, c):
    t = pl.program_id(1)
    nt = pl.num_programs(1)
    n_rows = p_ref.shape[1]
    n_chunks = n_rows // c

    def head_block(h):
        lo = (h % HEADS_PER_GROUP) * HEAD_DIM
        return h // HEADS_PER_GROUP, slice(lo, lo + HEAD_DIM)

    @pl.when(t == 0)
    def _():
        z_scr[...] = jnp.zeros(z_scr.shape, F32)
        for h in range(N_HEADS):
            i, blk = head_block(h)
            z_scr[i, blk, blk] = s0_ref[0, h]
        prev_scr[...] = sh0_ref[0]

    p = p_ref[0]
    row = lax.broadcasted_iota(I32, p.shape, 0)
    prev = jnp.where(row == 0, prev_scr[...], pltpu.roll(p, 1, 0))
    last = p[n_rows - 1:n_rows, :]
    prev_scr[...] = last
    sht_ref[0] = last

    pm = p + (prev - p) * mu_ref[...]
    r = pm[:, 0:WIDTH]
    k = pm[:, WIDTH:2 * WIDTH]
    v = pm[:, 2 * WIDTH:3 * WIDTH]
    o1 = 3 * WIDTH
    wd = pm[:, o1:o1 + DECAY_LORA]
    ad = pm[:, o1 + DECAY_LORA:o1 + DECAY_LORA + ICLR_LORA]
    gd = pm[:, o1 + DECAY_LORA + ICLR_LORA:RWKV_COLS]

    w = -_softplus(-(w0_ref[...] + _dot(jnp.tanh(wd), wb_ref[...]))) - 0.5
    lw = -jnp.exp(w)
    a = _sigmoid(a0_ref[...] + _dot(ad, ab_ref[...]))
    g = _dot(_sigmoid(gd), gb_ref[...])
    kk = k * kk_ref[...]
    kk = kk / jnp.maximum(jnp.sqrt(_gsum(kk * kk, gm_ref)), L2_EPS)
    kf = k * (1.0 + (a - 1.0) * ka_ref[...])

    ri = lax.broadcasted_iota(I32, (n_rows, n_rows), 0)
    ci = lax.broadcasted_iota(I32, (n_rows, n_rows), 1)
    same_chunk = (ri // c) == (ci // c)
    cum = _dot_exact_rhs(jnp.logical_and(same_chunk, ri >= ci).astype(F32), lw)
    cum_last = _dot_exact_rhs(same_chunk.astype(F32), lw)
    r_t = r * jnp.exp(cum)
    a_t = -kk * jnp.exp(cum - lw)
    inv = jnp.exp(-cum)
    b_t = kk * a * inv
    k_t = kf * inv
    to_end = jnp.exp(cum_last - cum)
    b_e = kk * a * to_end
    k_e = kf * to_end
    g_end = jnp.exp(cum_last)

    hg = HEADS_PER_GROUP
    gw = hg * HEAD_DIM
    log_c = int(math.log2(c))
    t_idx = lax.broadcasted_iota(I32, (c, hg * c), 0)
    s_idx = lax.broadcasted_iota(I32, (c, hg * c), 1) & (c - 1)
    strict = s_idx < t_idx
    lower = s_idx <= t_idx
    eye = (s_idx == t_idx).astype(F32)
    rb = lax.broadcasted_iota(I32, (hg * c, gw), 0) >> log_c
    mask_kv = rb == (lax.broadcasted_iota(I32, (hg * c, gw), 1) >> int(math.log2(HEAD_DIM)))
    rs = lax.broadcasted_iota(I32, (hg * c, hg * c), 0) >> log_c
    mask_ss = rs == (lax.broadcasted_iota(I32, (hg * c, hg * c), 1) >> log_c)
    ng = N_HEADS // hg
    cat = functools.partial(jnp.concatenate, axis=0)
    units = [(slice(j * c, (j + 1) * c), slice(i * gw, (i + 1) * gw)) for j in range(n_chunks) for i in range(ng)]
    nu = len(units)

    ar = [_split2(cat([a_t[rs_, s], r_t[rs_, s]])) for rs_, s in units]
    ab = [_mm3(ar[n], _bd_parts(b_t[units[n]], mask_kv), NT) for n in range(nu)]
    ak = [_mm3(ar[n], _bd_parts(k_t[units[n]], mask_kv), NT) for n in range(nu)]
    l_ab = [jnp.where(strict, m[:c], 0.0) for m in ab]
    l_rb = [jnp.where(lower, m[c:], 0.0) for m in ab]
    l_ak = [jnp.where(strict, m[:c], 0.0) for m in ak]
    l_rk = [jnp.where(lower, m[c:], 0.0) for m in ak]
    def mm1(a, b_bd):
        return jnp.dot(a.astype(BF16), b_bd, preferred_element_type=F32)

    def bd1(x, mask):
        tiled = jnp.concatenate([x.astype(BF16)] * hg, axis=0)
        return jnp.where(mask, tiled, jnp.zeros_like(tiled))

    tinv = [eye + m for m in l_ab]
    pw = [mm1(m, bd1(m, mask_ss)) for m in l_ab]
    for _ in range(1, log_c - 1):
        res = [mm1(cat([tinv[n], pw[n]]), bd1(pw[n], mask_ss)) for n in range(nu)]
        tinv = [tinv[n] + res[n][:c] for n in range(nu)]
        pw = [m[c:] for m in res]
    tinv = [tinv[n] + mm1(tinv[n], bd1(pw[n], mask_ss)) for n in range(nu)]
    av = [_mm3(_split2(cat([l_ak[n], l_rk[n]])), _bd_parts(v[units[n]], mask_kv), NN) for n in range(nu)]
    ue = [_split2(cat([b_e[units[n]], k_e[units[n]]])) for n in range(nu)]

    def wide(fn, x):
        return [fn(x[:, :gw]), fn(x[:, gw:])]

    def bd1w(x):
        return jnp.concatenate(wide(lambda h_: bd1(h_, mask_kv), x), axis=1)

    def bd3w(x):
        parts = wide(lambda h_: _bd_parts(h_, mask_kv), x)
        return tuple(jnp.concatenate([parts[0][q], parts[1][q]], axis=1) for q in range(2))

    rhs = [jnp.concatenate([a_t[units[n]], av[n][:c]], axis=1) for n in range(nu)]
    x0 = [mm1(tinv[n], bd1w(rhs[n])) for n in range(nu)]
    resid = [rhs[n] - (x0[n] - _mm3(_split2(l_ab[n]), bd3w(x0[n]), NN)) for n in range(nu)]
    sol = [x0[n] + mm1(tinv[n], bd1w(resid[n])) for n in range(nu)]
    lift = [_mm3(_split2(l_rb[n]), bd3w(sol[n]), NN) for n in range(nu)]
    lhs_s = [_split2(cat([sol[n][:, :gw], r_t[units[n]] + lift[n][:, :gw]])) for n in range(nu)]
    u_loc = [sol[n][:, gw:] for n in range(nu)]
    o_loc = [av[n][c:] + lift[n][:, gw:] for n in range(nu)]

    zr = lax.broadcasted_iota(I32, (gw, gw), 0) >> int(math.log2(HEAD_DIM))
    zmask = zr == (lax.broadcasted_iota(I32, (gw, gw), 1) >> int(math.log2(HEAD_DIM)))
    z = [z_scr[i] for i in range(ng)]
    o_rows = []
    for j in range(n_chunks):
        o_grp = []
        for i in range(ng):
            n = j * ng + i
            rs_, s = units[n]
            sz = _mm3(lhs_s[n], _split2(z[i]), NT)
            u = sz[:c] + u_loc[n]
            o_grp.append(sz[c:] + o_loc[n])
            upd = _mm3(_split2(cat([u, v[rs_, s]])), ue[n], TN)
            z[i] = z[i] * g_end[j * c:j * c + 1, s] + jnp.where(zmask, upd, 0.0)
        o_rows.append(jnp.concatenate(o_grp, axis=1))
    for i in range(ng):
        z_scr[i] = z[i]

    o = cat(o_rows)
    inv_hd = 1.0 / HEAD_DIM
    dlt = o - _gsum(o, gm_ref) * inv_hd
    var = _gsum(dlt * dlt, gm_ref) * inv_hd
    on = dlt * lax.rsqrt(var + RWKV_GN_EPS) * lnw_ref[...] + lnb_ref[...]
    bonus = _gsum(r * kf * rk_ref[...], gm_ref) * v
    y_ref[0] = (on + bonus) * g

    @pl.when(t == nt - 1)
    def _():
        for h in range(N_HEADS):
            i, blk = head_block(h)
            st_ref[0, h] = z_scr[i, blk, blk]


def _rwkv(p, shift0, s0, prm, gmat, chunk, chunks_per_step):
    b, t, _ = p.shape
    row = lambda n: _const_spec((1, n))
    rows = chunk * chunks_per_step
    return pl.pallas_call(
        functools.partial(_rwkv_kernel, c=chunk),
        grid=(b, t // rows),
        in_specs=[pl.BlockSpec((1, rows, RWKV_COLS), lambda i, j: (i, j, 0)),
                  pl.BlockSpec((1, 1, RWKV_COLS), lambda i, j: (i, 0, 0)),
                  pl.BlockSpec((1, N_HEADS, HEAD_DIM, HEAD_DIM), lambda i, j: (i, 0, 0, 0)),
                  row(RWKV_COLS), row(WIDTH), _const_spec((DECAY_LORA, WIDTH)), row(WIDTH),
                  _const_spec((ICLR_LORA, WIDTH)), _const_spec((GATE_LORA, WIDTH)),
                  row(WIDTH), row(WIDTH), row(WIDTH), row(WIDTH), row(WIDTH), _const_spec((WIDTH, WIDTH))],
        out_specs=[pl.BlockSpec((1, rows, WIDTH), lambda i, j: (i, j, 0)),
                   pl.BlockSpec((1, N_HEADS, HEAD_DIM, HEAD_DIM), lambda i, j: (i, 0, 0, 0)),
                   pl.BlockSpec((1, 1, RWKV_COLS), lambda i, j: (i, 0, 0))],
        out_shape=[jax.ShapeDtypeStruct((b, t, WIDTH), F32),
                   jax.ShapeDtypeStruct((b, N_HEADS, HEAD_DIM, HEAD_DIM), F32),
                   jax.ShapeDtypeStruct((b, 1, RWKV_COLS), F32)],
        scratch_shapes=[pltpu.VMEM((N_HEADS // HEADS_PER_GROUP, HEADS_PER_GROUP * HEAD_DIM,
                                    HEADS_PER_GROUP * HEAD_DIM), F32),
                        pltpu.VMEM((1, RWKV_COLS), F32)],
        compiler_params=_cparams(("parallel", "arbitrary")),
        name="rwkv7_mix",
    )(p, shift0, s0, prm["mu"], prm["w0"], prm["wb"], prm["a0"], prm["ab"], prm["gb"], prm["kk"], prm["ka"],
      prm["rk"], prm["lnw"], prm["lnb"], gmat)


def _merge_kernel(x_ref, ya_ref, yb_ref, gate_ref, mod_ref, g2_ref, woa_ref, wob_ref, wo_ref, wrh_ref, wrl_ref,
                  *rest):
    x1_ref, h2_ref, lg_ref = rest[-3:]
    bb, tt, d = x_ref.shape
    m = bb * tt
    gate = gate_ref[...].reshape(m, GATE_COLS)
    merged = (gate[:, 0:d] * _dot(ya_ref[...].reshape(m, WIDTH), woa_ref[...])
              + gate[:, d:2 * d] * _dot(yb_ref[...].reshape(m, WIDTH), wob_ref[...]))
    x1 = x_ref[...] + mod_ref[:, 2:3, :] * _dot(merged, wo_ref[...]).reshape(bb, tt, d)
    x1_ref[...] = x1
    ms = jnp.mean(x1 * x1, axis=-1, keepdims=True)
    h2 = x1 * lax.rsqrt(ms + RMS_EPS) * g2_ref[...]
    h2 = (h2 * (1.0 + mod_ref[:, 4:5, :]) + mod_ref[:, 3:4, :]).reshape(m, d)
    h2_ref[...] = h2
    lg_ref[...] = _mm3((wrh_ref[...], wrl_ref[...]), _split2(h2), NT)


def _merge(x, ya, yb, gate, mod, w, n_total, row_offset, shared=None):
    b, t, d = x.shape
    bb, tt = _token_blocks(b, t)
    nt = t // tt
    m = bb * tt
    off = row_offset // m

    def tok(cols):
        return pl.BlockSpec((bb, tt, cols), lambda i, j: (i, j, 0))

    in_specs = [tok(d), tok(WIDTH), tok(WIDTH), tok(GATE_COLS),
                pl.BlockSpec((bb, 6, d), lambda i, j: (i, 0, 0)),
                _const_spec((1, d)), _const_spec((WIDTH, d)), _const_spec((WIDTH, d)), _const_spec((d, d)),
                _const_spec((N_EXPERTS, d)), _const_spec((N_EXPERTS, d))]
    args = [x, ya, yb, gate, mod.reshape(b, 6, d), w["g2"], w["w_oa"], w["w_ob"], w["w_o"], w["wr_hi"], w["wr_lo"]]
    aliases = {}
    if shared is not None:
        aliases = {len(args): 1, len(args) + 1: 2}
        in_specs += [pl.BlockSpec(memory_space=pl.ANY), pl.BlockSpec(memory_space=pl.ANY)]
        args += list(shared)
    return pl.pallas_call(
        _merge_kernel,
        grid=(b // bb, nt),
        in_specs=in_specs,
        out_specs=[tok(d), pl.BlockSpec((m, d), lambda i, j: (off + i * nt + j, 0)),
                   pl.BlockSpec((N_EXPERTS, m), lambda i, j: (0, off + i * nt + j))],
        out_shape=[jax.ShapeDtypeStruct((b, t, d), F32), jax.ShapeDtypeStruct((n_total, d), F32),
                   jax.ShapeDtypeStruct((N_EXPERTS, n_total), F32)],
        input_output_aliases=aliases,
        compiler_params=_cparams(("parallel", "parallel")),
        name="merge_norm2_router",
    )(*args)


def _route_kernel(lg_ref, bias_ref, idx_ref, wt_ref, rank_ref, cnt_ref, carry):
    @pl.when(pl.program_id(0) == 0)
    def _():
        carry[...] = jnp.zeros(carry.shape, F32)

    tm = lg_ref.shape[1]
    scores = _sigmoid(lg_ref[...])
    sel = scores + bias_ref[...]
    row = lax.broadcasted_iota(I32, (N_EXPERTS, tm), 0)
    neg_inf = -jnp.inf

    def first_argmax(vals, rows):
        mx = jnp.max(vals, axis=0, keepdims=True)
        return mx, jnp.min(jnp.where(vals == mx, rows, N_EXPERTS), axis=0, keepdims=True)

    gslices = [slice(g * EXPERTS_PER_GROUP, (g + 1) * EXPERTS_PER_GROUP) for g in range(N_GROUPS)]
    gs = []
    row_g = lax.broadcasted_iota(I32, (EXPERTS_PER_GROUP, tm), 0)
    for sl in gslices:
        m1, i1 = first_argmax(sel[sl], row_g)
        m2 = jnp.max(jnp.where(row_g == i1, neg_inf, sel[sl]), axis=0, keepdims=True)
        gs.append(m1 + m2)
    kept = []
    for g in range(N_GROUPS):
        beaten = jnp.zeros((1, tm), I32)
        for o in range(N_GROUPS):
            if o != g:
                wins = (gs[o] >= gs[g]) if o < g else (gs[o] > gs[g])
                beaten = beaten + wins.astype(I32)
        kept.append(jnp.where(beaten < TOPK_GROUPS, sel[gslices[g]], neg_inf))
    cur = jnp.concatenate(kept, axis=0)

    idxs, ws = [], []
    picked = jnp.zeros((N_EXPERTS, tm), F32)
    for _ in range(TOP_K):
        _, ik = first_argmax(cur, row)
        hit = row == ik
        idxs.append(ik)
        ws.append(jnp.sum(jnp.where(hit, scores, 0.0), axis=0, keepdims=True))
        cur = jnp.where(hit, neg_inf, cur)
        picked = jnp.where(hit, 1.0, picked)
    wsum = ws[0]
    for k in range(1, TOP_K):
        wsum = wsum + ws[k]

    r = lax.broadcasted_iota(I32, (tm, tm), 0)
    c = lax.broadcasted_iota(I32, (tm, tm), 1)
    before = jnp.dot(picked.astype(BF16), (r < c).astype(BF16), preferred_element_type=F32) + carry[...]
    carry[...] = carry[...] + jnp.sum(picked, axis=1, keepdims=True)
    cnt_ref[...] = carry[...]

    kk = lax.broadcasted_iota(I32, (TOP_K, tm), 0)
    idx_o = jnp.zeros((TOP_K, tm), I32)
    wt_o = jnp.zeros((TOP_K, tm), F32)
    rank_o = jnp.zeros((TOP_K, tm), F32)
    for k in range(TOP_K):
        rk = jnp.sum(jnp.where(row == idxs[k], before, 0.0), axis=0, keepdims=True)
        idx_o = jnp.where(kk == k, idxs[k], idx_o)
        wt_o = jnp.where(kk == k, ws[k] / wsum * ROUTED_SCALE, wt_o)
        rank_o = jnp.where(kk == k, rk, rank_o)
    idx_ref[...] = idx_o
    wt_ref[...] = wt_o
    rank_ref[...] = rank_o.astype(I32)


def _route(logits_t, bias_col, tm):
    n = logits_t.shape[1]
    tokk = pl.BlockSpec((TOP_K, tm), lambda i: (0, i))
    return pl.pallas_call(
        _route_kernel,
        grid=(n // tm,),
        in_specs=[pl.BlockSpec((N_EXPERTS, tm), lambda i: (0, i)), _const_spec((N_EXPERTS, 1))],
        out_specs=[tokk, tokk, tokk, _const_spec((N_EXPERTS, 1))],
        out_shape=[jax.ShapeDtypeStruct((TOP_K, n), I32), jax.ShapeDtypeStruct((TOP_K, n), F32),
                   jax.ShapeDtypeStruct((TOP_K, n), I32), jax.ShapeDtypeStruct((N_EXPERTS, 1), F32)],
        scratch_shapes=[pltpu.VMEM((N_EXPERTS, 1), F32)],
        compiler_params=_cparams(("arbitrary",)),
        name="route_topk",
    )(logits_t, bias_col)


def _plan_kernel(cnt_ref, start_ref, be_ref, valid_ref, nu_ref, *, blk):
    cnt = cnt_ref[...]
    padded = jnp.ceil(cnt * (1.0 / blk)) * blk
    e_r = lax.broadcasted_iota(I32, (N_EXPERTS, N_EXPERTS), 0)
    e_c = lax.broadcasted_iota(I32, (N_EXPERTS, N_EXPERTS), 1)
    incl = (e_r <= e_c).astype(BF16)
    ph, pm, plo = _split3(jnp.broadcast_to(padded, (8, N_EXPERTS)))
    d2 = functools.partial(jnp.dot, preferred_element_type=F32)
    pad_end = (d2(ph, incl) + (d2(pm, incl) + d2(plo, incl)))[0:1, :]
    pad_start = pad_end - padded
    start_ref[...] = pad_start.astype(I32)
    total = jnp.max(pad_end, axis=-1, keepdims=True)
    nu_ref[...] = jnp.broadcast_to(total * (1.0 / blk), (1, N_EXPERTS)).astype(I32)
    nb = be_ref.shape[0]
    first = (lax.broadcasted_iota(I32, (nb, N_EXPERTS), 0) * blk).astype(F32)
    lane = lax.broadcasted_iota(I32, (nb, N_EXPERTS), 1)
    inside = jnp.logical_and(pad_start <= first, first < pad_end)
    be_ref[...] = jnp.sum(jnp.where(inside, lane, 0), axis=-1, keepdims=True)
    rows = jnp.minimum(pad_start + cnt - first, float(blk))
    valid_ref[...] = jnp.sum(jnp.where(inside, rows, 0.0), axis=-1, keepdims=True).astype(I32)


def _plan(counts, n_blocks, blk):
    return pl.pallas_call(
        functools.partial(_plan_kernel, blk=blk),
        out_shape=[jax.ShapeDtypeStruct((1, N_EXPERTS), I32), jax.ShapeDtypeStruct((n_blocks, 1), I32),
                   jax.ShapeDtypeStruct((n_blocks, 1), I32), jax.ShapeDtypeStruct((1, N_EXPERTS), I32)],
        compiler_params=pltpu.CompilerParams(vmem_limit_bytes=VMEM_LIMIT),
        name="dispatch_plan",
    )(counts)


def _dest_kernel(idx_ref, rank_ref, start_ref, dest_ref):
    tm = idx_ref.shape[1]
    row = lax.broadcasted_iota(I32, (N_EXPERTS, tm), 0)
    kk = lax.broadcasted_iota(I32, (TOP_K, tm), 0)
    idx = idx_ref[...]
    base = jnp.zeros((TOP_K, tm), I32)
    for k in range(TOP_K):
        bk = jnp.sum(jnp.where(row == idx[k:k + 1, :], start_ref[...], 0), axis=0, keepdims=True)
        base = jnp.where(kk == k, bk, base)
    dest_ref[...] = base + rank_ref[...]


def _dest(idx, rank, pad_start_col, tm):
    n = idx.shape[1]
    tokk = pl.BlockSpec((TOP_K, tm), lambda i: (0, i))
    return pl.pallas_call(
        _dest_kernel,
        grid=(n // tm,),
        in_specs=[tokk, tokk, _const_spec((N_EXPERTS, 1))],
        out_specs=tokk,
        out_shape=jax.ShapeDtypeStruct((TOP_K, n), I32),
        compiler_params=_cparams(("parallel",)),
        name="dispatch_dest",
    )(idx, rank, pad_start_col)


def _slot_tokens(dest_t, n_slots):
    k, n = dest_t.shape
    m = k * n
    window = SC_SCATTER_WINDOW
    tok_rows = jnp.broadcast_to((jnp.arange(m, dtype=I32) % n)[:, None], (m, LANES))
    mesh = plsc.VectorSubcoreMesh(core_axis_name="core", subcore_axis_name="subcore")

    @functools.partial(pl.kernel, out_type=jax.ShapeDtypeStruct((n_slots, LANES), I32), mesh=mesh,
                       scratch_types=[])
    def scatter(rows_hbm, idx_hbm, out_hbm):
        def body(rows_vmem, idx_vmem):
            pltpu.sync_copy(rows_vmem, out_hbm.at[idx_vmem.at[0]])

        pltpu.emit_pipeline(
            body, grid=(m // window,),
            in_specs=[pl.BlockSpec((window, LANES), index_map=lambda i: (i, 0)),
                      pl.BlockSpec((1, window), index_map=lambda i: (0, i))],
            out_specs=[], core_axis_name="subcore", dimension_semantics=(pltpu.PARALLEL,),
        )(rows_hbm, idx_hbm)

    return jnp.clip(scatter(tok_rows, dest_t.reshape(1, m))[:, 0], 0, n - 1)


def _expert_kernel(be_ref, valid_ref, nu_ref, tok0_ref, tok1_ref, tok2_ref, h2_ref, wg_ref, wu_ref, wd_ref, y_ref,
                   xbuf, sems, wg_b, wu_b, wd_b):
    i = pl.program_id(0)
    n_used = nu_ref[0]
    blk = xbuf.shape[1]
    slot = i % GATHER_BUFFERS

    def gather(toks, buf):
        for r in range(blk):
            pltpu.make_async_copy(h2_ref.at[pl.ds(toks[0, 0, r], 1)], xbuf.at[buf, pl.ds(r, 1)],
                                  sems.at[buf]).start()

    def wait(buf):
        pltpu.make_async_copy(h2_ref.at[pl.ds(0, blk)], xbuf.at[buf], sems.at[buf]).wait()

    @pl.when(i == 0)
    def _():
        gather(tok0_ref, 0)
        gather(tok1_ref, 1)

    new_expert = jnp.logical_or(i == 0, be_ref[i] != be_ref[jnp.maximum(i - 1, 0)])

    @pl.when(jnp.logical_and(i < n_used, new_expert))
    def _():
        wg_b[...] = wg_ref[0].astype(BF16)
        wu_b[...] = wu_ref[0].astype(BF16)
        wd_b[...] = wd_ref[0].astype(BF16)

    def process(cur):
        ahead1, ahead2 = (cur + 1) % GATHER_BUFFERS, (cur + 2) % GATHER_BUFFERS
        wait(cur)
        rows = lax.broadcasted_iota(I32, (blk, 1), 0)
        x = jnp.where(rows < valid_ref[i], xbuf[cur], 0.0).astype(BF16)
        gather(tok2_ref, ahead2)
        hg = jnp.dot(x, wg_b[...], preferred_element_type=F32)
        hu = jnp.dot(x, wu_b[...], preferred_element_type=F32)
        y_ref[...] = jnp.dot((_silu(hg) * hu).astype(BF16), wd_b[...], preferred_element_type=F32)

        @pl.when(i == n_used - 1)
        def _():
            wait(ahead1)
            wait(ahead2)

    for cur in range(GATHER_BUFFERS):
        pl.when(jnp.logical_and(i < n_used, slot == cur))(functools.partial(process, cur))


def _experts(h2, slot_tok, block_e, valid, n_used, w_eg, w_eu, w_ed, blk):
    n_tok, d = h2.shape
    n_blocks = slot_tok.shape[0] // blk
    toks = slot_tok.reshape(n_blocks, 1, blk)

    def row_blk(i, be, valid, nu):
        return (jnp.minimum(i, nu[0] - 1), 0)

    def w_blk(i, be, valid, nu):
        return (be[i], 0, 0)

    def tok_blk(ahead):
        return pl.BlockSpec((1, 1, blk), lambda i, be, valid, nu: (jnp.minimum(i + ahead, nu[0] - 1), 0, 0),
                            memory_space=pltpu.SMEM)

    return pl.pallas_call(
        _expert_kernel,
        grid_spec=pltpu.PrefetchScalarGridSpec(
            num_scalar_prefetch=3,
            grid=(n_blocks,),
            in_specs=[tok_blk(0), tok_blk(1), tok_blk(2), pl.BlockSpec(memory_space=pl.ANY),
                      pl.BlockSpec((1, d, D_EXPERT), w_blk), pl.BlockSpec((1, d, D_EXPERT), w_blk),
                      pl.BlockSpec((1, D_EXPERT, d), w_blk)],
            out_specs=pl.BlockSpec((blk, d), row_blk),
            scratch_shapes=[pltpu.VMEM((GATHER_BUFFERS, blk, d), F32), pltpu.SemaphoreType.DMA((GATHER_BUFFERS,)),
                            pltpu.VMEM((d, D_EXPERT), BF16), pltpu.VMEM((d, D_EXPERT), BF16),
                            pltpu.VMEM((D_EXPERT, d), BF16)]),
        out_shape=jax.ShapeDtypeStruct((n_blocks * blk, d), F32),
        compiler_params=_cparams(("arbitrary",)),
        name="moe_experts",
    )(block_e, valid, n_used, toks, toks, toks, h2, w_eg, w_eu, w_ed)


def _final_kernel(dest_ref, x1_ref, h2_ref, wt_ref, mod_ref, wsg_ref, wsu_ref, wsd_ref, ys_ref, o_ref, ybuf, sem):
    bb, tt, d = x1_ref.shape
    m = bb * tt

    def issue(r, carry):
        for k in range(TOP_K):
            pltpu.make_async_copy(ys_ref.at[pl.ds(dest_ref[k, r], 1)], ybuf.at[k, pl.ds(r, 1)], sem).start()
        return carry

    lax.fori_loop(0, m, issue, 0)

    hb = h2_ref[...].astype(BF16)
    hg = jnp.dot(hb, wsg_ref[...], preferred_element_type=F32)
    hu = jnp.dot(hb, wsu_ref[...], preferred_element_type=F32)
    ffn = _dot(_silu(hg) * hu, wsd_ref[...])

    for k in range(TOP_K):
        pltpu.make_async_copy(ys_ref.at[pl.ds(0, m)], ybuf.at[k], sem).wait()

    wt = wt_ref[...]
    for k in range(TOP_K):
        ffn = ffn + wt[:, k:k + 1] * ybuf[k]
    o_ref[...] = x1_ref[...] + mod_ref[:, 5:6, :] * ffn.reshape(bb, tt, d)


def _final(x1, h2_all, wts_all, dest_t, ys, mod, w, row_offset):
    b, t, d = x1.shape
    bb, tt = _token_blocks(b, t)
    nt = t // tt
    m = bb * tt
    off = row_offset // m

    def flat_idx(i, j):
        return off + i * nt + j

    return pl.pallas_call(
        _final_kernel,
        grid=(b // bb, nt),
        in_specs=[pl.BlockSpec((TOP_K, m), lambda i, j: (0, flat_idx(i, j)), memory_space=pltpu.SMEM),
                  pl.BlockSpec((bb, tt, d), lambda i, j: (i, j, 0)),
                  pl.BlockSpec((m, d), lambda i, j: (flat_idx(i, j), 0)),
                  pl.BlockSpec((m, TOP_K), lambda i, j: (flat_idx(i, j), 0)),
                  pl.BlockSpec((bb, 6, d), lambda i, j: (i, 0, 0)),
                  _const_spec((d, D_EXPERT)), _const_spec((d, D_EXPERT)), _const_spec((D_EXPERT, d)),
                  pl.BlockSpec(memory_space=pl.ANY)],
        out_specs=pl.BlockSpec((bb, tt, d), lambda i, j: (i, j, 0)),
        out_shape=jax.ShapeDtypeStruct((b, t, d), F32),
        scratch_shapes=[pltpu.VMEM((TOP_K, m, d), F32), pltpu.SemaphoreType.DMA(())],
        compiler_params=_cparams(("arbitrary", "arbitrary")),
        name="moe_combine_final",
    )(dest_t, x1, h2_all, wts_all, mod.reshape(b, 6, d), w["w_sg"], w["w_su"], w["w_sd"], ys)


def _moe_routed(h2_all, logits_all, w, blk=256, tm=256):
    n = h2_all.shape[0]
    n_blocks = (n * TOP_K + N_EXPERTS * (blk - 1)) // blk + 1
    n_blocks = (n_blocks + 7) // 8 * 8
    idx, wts_t, rank, counts = _route(logits_all, w["router_bias"], tm)
    pad_start, block_e, valid, n_used = _plan(counts.reshape(1, N_EXPERTS), n_blocks, blk)
    block_e = block_e.reshape(n_blocks)
    valid = valid.reshape(n_blocks)
    n_used = n_used[0, 0:1]
    dest_t = _dest(idx, rank, pad_start.reshape(N_EXPERTS, 1), tm)
    slot_tok = _slot_tokens(dest_t, n_blocks * blk)
    ys = _experts(h2_all, slot_tok, block_e, valid, n_used, w["w_eg"], w["w_eu"], w["w_ed"], blk)
    return ys, dest_t, jnp.transpose(wts_t)


def _prep(raw):
    p = {k: v[0] for k, v in raw.items()}
    w_in = p["w_in"]
    o_fox = RWKV_COLS
    o_fl = o_fox + FOX_MAIN_COLS
    o_gate = o_fl + N_HEADS
    row = lambda a: a.reshape(1, -1)
    return dict(
        w_ada=p["w_ada"], b_ada=p["b_ada"],
        g1=row(p["norm1_g"]), g2=row(p["norm2_g"]),
        wr=w_in[:, :o_fox].astype(BF16),
        wf=w_in[:, o_fox:o_fl].astype(BF16),
        wfl=jnp.pad(w_in[:, o_fl:o_gate], ((0, 0), (0, LANES - N_HEADS))).astype(BF16),
        wg=w_in[:, o_gate:].astype(BF16),
        qn=row(jnp.tile(p["fox_q_norm"], N_HEADS)), kn=row(jnp.tile(p["fox_k_norm"], N_HEADS)),
        fb=row(p["fox_f_bias"]),
        gmat=_group_ones(),
        rwkv=dict(mu=row(p["rwkv_mu"]), w0=row(p["rwkv_w0"]), wb=p["rwkv_w_lora_b"], a0=row(p["rwkv_a0"]),
                  ab=p["rwkv_a_lora_b"], gb=p["rwkv_g_lora_b"], kk=row(p["rwkv_k_k"]), ka=row(p["rwkv_k_a"]),
                  rk=row(p["rwkv_r_k"]), lnw=row(p["rwkv_ln_w"]), lnb=row(p["rwkv_ln_b"])),
        w_oa=p["w_out_rwkv"].astype(BF16), w_ob=p["w_out_fox"].astype(BF16), w_o=p["w_out"].astype(BF16),
        wr_hi=p["w_router"].T.astype(BF16),
        wr_lo=(p["w_router"] - p["w_router"].astype(BF16).astype(F32)).T.astype(BF16),
        router_bias=p["router_bias"].reshape(N_EXPERTS, 1),
        w_eg=p["w_exp_gate"], w_eu=p["w_exp_up"], w_ed=p["w_exp_down"],
        w_sg=p["w_sh_gate"].astype(BF16), w_su=p["w_sh_up"].astype(BF16), w_sd=p["w_sh_down"].astype(BF16),
    )


def _token_blocks(b, t):
    if t >= 256:
        return 1, 256
    bb = max(1, min(b, 256 // t))
    while b % bb:
        bb -= 1
    return bb, t


def _mix_path(x, mod, shift0, wkv0, past_k, past_v, past_logf, w):
    b, t, d = x.shape
    bb, tt = _token_blocks(b, t)
    n_past = past_k.shape[1]
    if n_past:
        f_past = _past_cumsum(past_logf)
        init = f_past[:, n_past - 1:n_past, :]
        past = (past_k, past_v, jnp.swapaxes(f_past, 1, 2))
    else:
        init = jnp.zeros((b, 1, N_HEADS), F32)
        past = None
    pr, q, k, v, sg, logf, f_new, gate = _inproj(x, mod.reshape(b, 6, d), w["g1"], w["wr"], w["wf"], w["wfl"],
                                                 w["wg"], w["qn"], w["kn"], w["fb"], w["gmat"], init, bb, tt)
    y_fox = _fox_attention(q, f_new, sg, k, v, jnp.swapaxes(f_new, 1, 2), past=past, tq=min(t, 512),
                           tk_past=min(max(n_past, 1), 512))
    chunk = min(t, RWKV_CHUNK)
    y_rwkv, wkv_new, shift_new = _rwkv(pr, shift0.reshape(b, 1, RWKV_COLS), wkv0, w["rwkv"], w["gmat"],
                                       chunk, max(1, min(RWKV_CHUNKS_PER_STEP, t // chunk)))
    return y_rwkv, y_fox, gate, wkv_new, shift_new, k, v, logf


def _layer(paths, w):
    n_b = [p[0].shape[0] for p in paths]
    mod_all = _ada(jnp.concatenate([p[1] for p in paths], axis=0), w["w_ada"], w["b_ada"])
    mods, o = [], 0
    for nb in n_b:
        mods.append(mod_all[o:o + nb])
        o += nb
    n_total = sum(p[0].shape[0] * p[0].shape[1] for p in paths)
    mixed, x1s = [], []
    shared, row = None, 0
    for (x, _, shift0, wkv0, pk, pv, plf), mod in zip(paths, mods):
        ya, yb, gate, wkv_new, shift_new, k, v, logf = _mix_path(x, mod, shift0, wkv0, pk, pv, plf, w)
        x1, h2_all, lg_all = _merge(x, ya, yb, gate, mod, w, n_total, row, shared)
        shared = (h2_all, lg_all)
        row += x.shape[0] * x.shape[1]
        mixed.append((wkv_new, shift_new, k, v, logf))
        x1s.append(x1)
    ys, dest_t, wts = _moe_routed(h2_all, lg_all, w)
    outs, row = [], 0
    for x1, mod, st in zip(x1s, mods, mixed):
        y = _final(x1, h2_all, wts, dest_t, ys, mod, w, row)
        row += x1.shape[0] * x1.shape[1]
        outs.append((y,) + st)
    return outs


def kernel(x_prompt, x_sample, c_prompt, c_sample, state_rwkv_wkv, state_rwkv_shift, cache_fox_k, cache_fox_v,
           cache_fox_logf, w_ada, b_ada, norm1_g, norm2_g, w_in, rwkv_mu, rwkv_w0, rwkv_w_lora_b, rwkv_a0,
           rwkv_a_lora_b, rwkv_g_lora_b, rwkv_k_k, rwkv_k_a, rwkv_r_k, rwkv_ln_w, rwkv_ln_b, fox_q_norm,
           fox_k_norm, fox_f_bias, w_out_rwkv, w_out_fox, w_out, w_router, router_bias, w_exp_gate, w_exp_up,
           w_exp_down, w_sh_gate, w_sh_up, w_sh_down):
    raw = dict(w_ada=w_ada, b_ada=b_ada, norm1_g=norm1_g, norm2_g=norm2_g, w_in=w_in, rwkv_mu=rwkv_mu,
               rwkv_w0=rwkv_w0, rwkv_w_lora_b=rwkv_w_lora_b, rwkv_a0=rwkv_a0, rwkv_a_lora_b=rwkv_a_lora_b,
               rwkv_g_lora_b=rwkv_g_lora_b, rwkv_k_k=rwkv_k_k, rwkv_k_a=rwkv_k_a, rwkv_r_k=rwkv_r_k,
               rwkv_ln_w=rwkv_ln_w, rwkv_ln_b=rwkv_ln_b, fox_q_norm=fox_q_norm, fox_k_norm=fox_k_norm,
               fox_f_bias=fox_f_bias, w_out_rwkv=w_out_rwkv, w_out_fox=w_out_fox, w_out=w_out,
               w_router=w_router, router_bias=router_bias, w_exp_gate=w_exp_gate, w_exp_up=w_exp_up,
               w_exp_down=w_exp_down, w_sh_gate=w_sh_gate, w_sh_up=w_sh_up, w_sh_down=w_sh_down)
    assert w_in.shape[0] == 1, "single-layer stack"
    w = _prep(raw)
    bp, tp, _ = x_prompt.shape
    bs, ts, _ = x_sample.shape
    n_past = cache_fox_k.shape[2]
    prompt = (x_prompt, c_prompt, jnp.zeros((bp, RWKV_COLS), F32),
              jnp.zeros((bp, N_HEADS, HEAD_DIM, HEAD_DIM), F32),
              jnp.zeros((bp, 0, WIDTH), F32), jnp.zeros((bp, 0, WIDTH), F32), jnp.zeros((bp, 0, N_HEADS), F32))
    sample = (x_sample, c_sample, state_rwkv_shift[0], state_rwkv_wkv[0],
              cache_fox_k[0].reshape(bs, n_past, WIDTH), cache_fox_v[0].reshape(bs, n_past, WIDTH),
              cache_fox_logf[0])
    (yp, wkv_p, sh_p, k_p, v_p, lf_p), (ysm, wkv_s, sh_s, k_s, v_s, lf_s) = _layer([prompt, sample], w)

    def heads(a):
        return a.reshape((1,) + a.shape[:2] + (N_HEADS, HEAD_DIM))

    return (yp, ysm,
            wkv_p[None], sh_p.reshape(1, bp, RWKV_COLS), heads(k_p), heads(v_p), lf_p[None],
            wkv_s[None], sh_s.reshape(1, bs, RWKV_COLS), heads(k_s), heads(v_s), lf_s[None])
```

```python
import functools
import math

import jax
import jax.numpy as jnp
from jax import lax
from jax.experimental import pallas as pl
from jax.experimental.pallas import tpu as pltpu

F32 = jnp.float32
BF16 = jnp.bfloat16
I32 = jnp.int32

D_MODEL = 1024
N_HEADS = 8
HEAD_DIM = 64
WIDTH = N_HEADS * HEAD_DIM
HEADS_PER_GROUP = 4
RWKV_CHUNK = 64
RWKV_CHUNKS_PER_STEP = 4
DECAY_LORA = 64
ICLR_LORA = 64
GATE_LORA = 128
RWKV_COLS = 3 * WIDTH + DECAY_LORA + ICLR_LORA + GATE_LORA
FOX_MAIN_COLS = 4 * WIDTH
GATE_COLS = 2 * D_MODEL
RWKV_GN_EPS = HEAD_DIM * 1e-5
L2_EPS = 1e-12
RMS_EPS = 1e-6
N_EXPERTS = 256
TOP_K = 8
N_GROUPS = 8
TOPK_GROUPS = 4
EXPERTS_PER_GROUP = N_EXPERTS // N_GROUPS
D_EXPERT = 256
ROUTED_SCALE = 2.5

LANES = 128
VMEM_LIMIT = 56 * 1024 * 1024
NEG_BIG = -1e30

NN = (((1,), (0,)), ((), ()))
NT = (((1,), (1,)), ((), ()))
TN = (((0,), (0,)), ((), ()))


def _cparams(sem):
    return pltpu.CompilerParams(dimension_semantics=sem, vmem_limit_bytes=VMEM_LIMIT)


def _dot(a, b, dims=NN):
    return lax.dot_general(a.astype(BF16), b.astype(BF16), dims, preferred_element_type=F32)


def _split2(a):
    hi = a.astype(BF16)
    lo = (a - hi.astype(F32)).astype(BF16)
    return hi, lo


def _split3(a):
    hi = a.astype(BF16)
    r1 = a - hi.astype(F32)
    mid = r1.astype(BF16)
    lo = (r1 - mid.astype(F32)).astype(BF16)
    return hi, mid, lo


def _dot3(a, b, dims=NN):
    ah, al = _split2(a)
    bh, bl = _split2(b)
    d = functools.partial(lax.dot_general, dimension_numbers=dims, preferred_element_type=F32)
    return d(ah, bh) + (d(ah, bl) + d(al, bh))


def _mm3(a, b, dims):
    d = functools.partial(lax.dot_general, dimension_numbers=dims, preferred_element_type=F32)
    return d(a[0], b[0]) + (d(a[0], b[1]) + d(a[1], b[0]))


def _bd_parts(x, mask):
    out = []
    for part in _split2(x):
        tiled = jnp.concatenate([part] * HEADS_PER_GROUP, axis=0)
        out.append(jnp.where(mask, tiled, jnp.zeros_like(tiled)))
    return tuple(out)


def _dot_exact_rhs(a_exact, b, dims=NN):
    ab = a_exact.astype(BF16)
    bh, bm, bl = _split3(b)
    d = functools.partial(lax.dot_general, dimension_numbers=dims, preferred_element_type=F32)
    return d(ab, bh) + (d(ab, bm) + d(ab, bl))


def _gsum(x, g_ref):
    hi, mid, lo = _split3(x)
    g = g_ref[...]
    d = functools.partial(jnp.dot, preferred_element_type=F32)
    return d(hi, g) + (d(mid, g) + d(lo, g))


def _sigmoid(x):
    return 1.0 / (1.0 + jnp.exp(-x))


def _softplus(x):
    return jnp.maximum(x, 0.0) + jnp.log1p(jnp.exp(-jnp.abs(x)))


def _silu(x):
    return x * _sigmoid(x)


def _group_ones():
    h = jnp.arange(WIDTH, dtype=I32) // HEAD_DIM
    return (h[:, None] == h[None, :]).astype(BF16)


def _ada_kernel(c_ref, w_ref, b_ref, o_ref):
    o_ref[...] = _dot(_silu(c_ref[...]), w_ref[...]) + b_ref[...]


def _ada(c, w_ada, b_ada):
    nb = c.shape[0]
    n_out = w_ada.shape[1]
    blk = D_MODEL
    return pl.pallas_call(
        _ada_kernel,
        grid=(n_out // blk,),
        in_specs=[pl.BlockSpec((nb, D_MODEL), lambda j: (0, 0)),
                  pl.BlockSpec((D_MODEL, blk), lambda j: (0, j)),
                  pl.BlockSpec((1, blk), lambda j: (0, j))],
        out_specs=pl.BlockSpec((nb, blk), lambda j: (0, j)),
        out_shape=jax.ShapeDtypeStruct((nb, n_out), F32),
        compiler_params=_cparams(("parallel",)),
        name="ada_mod",
    )(c, w_ada, b_ada.reshape(1, n_out))


def _inproj_kernel(x_ref, mod_ref, g1_ref, wr_ref, wf_ref, wfl_ref, wg_ref, qn_ref, kn_ref, fb_ref, gm_ref, f0_ref,
                   pr_ref, q_ref, k_ref, v_ref, sg_ref, lf_ref, cf_ref, gate_ref, carry):
    bb, tt, d = x_ref.shape
    m = bb * tt
    x = x_ref[...]
    ms = jnp.mean(x * x, axis=-1, keepdims=True)
    h = x * lax.rsqrt(ms + RMS_EPS) * g1_ref[...]
    h = h * (1.0 + mod_ref[:, 1:2, :]) + mod_ref[:, 0:1, :]
    hb = h.reshape(m, d).astype(BF16)

    pr_ref[...] = jnp.dot(hb, wr_ref[...], preferred_element_type=F32).reshape(bb, tt, RWKV_COLS)

    f = jnp.dot(hb, wf_ref[...], preferred_element_type=F32)
    q = f[:, 0:WIDTH]
    k = f[:, WIDTH:2 * WIDTH]
    v = f[:, 2 * WIDTH:3 * WIDTH]
    og = f[:, 3 * WIDTH:4 * WIDTH]
    inv_hd = 1.0 / HEAD_DIM
    q = q * lax.rsqrt(_gsum(q * q, gm_ref) * inv_hd + RMS_EPS) * qn_ref[...]
    k = k * lax.rsqrt(_gsum(k * k, gm_ref) * inv_hd + RMS_EPS) * kn_ref[...]
    q_ref[...] = (q * (HEAD_DIM ** -0.5)).astype(BF16).reshape(bb, tt, WIDTH)
    k_ref[...] = k.reshape(bb, tt, WIDTH)
    v_ref[...] = v.reshape(bb, tt, WIDTH)
    sg_ref[...] = _sigmoid(og).reshape(bb, tt, WIDTH)

    fl = jnp.dot(hb, wfl_ref[...], preferred_element_type=F32)[:, 0:N_HEADS] + fb_ref[...]
    lf = -_softplus(-fl)
    lf_ref[...] = lf.reshape(bb, tt, N_HEADS)

    @pl.when(pl.program_id(1) == 0)
    def _():
        carry[...] = f0_ref[...]

    r = lax.broadcasted_iota(I32, (m, m), 0)
    c = lax.broadcasted_iota(I32, (m, m), 1)
    tri = jnp.logical_and(r // tt == c // tt, r >= c).astype(F32)
    cf = _dot_exact_rhs(tri, lf).reshape(bb, tt, N_HEADS) + carry[...]
    cf_ref[...] = cf
    carry[...] = cf[:, tt - 1:tt, :]

    gate_ref[...] =_sigmoid(jnp.dot(hb, wg_ref[...], preferred_element_type=F32)).reshape(bb, tt, GATE_COLS)


def _const_spec(shape):
    nd = len(shape)
    return pl.BlockSpec(shape, lambda *_: (0,) * nd)


def _inproj(x, mod, g1, wr, wf, wfl, wg, qn, kn, fb, gmat, f0, bb, tt):
    b, t, d = x.shape
    grid = (b // bb, t // tt)

    def tok(cols):
        return pl.BlockSpec((bb, tt, cols), lambda i, j: (i, j, 0))

    out_cols = [(RWKV_COLS, F32), (WIDTH, BF16), (WIDTH, F32), (WIDTH, F32), (WIDTH, F32), (N_HEADS, F32),
                (N_HEADS, F32), (GATE_COLS, F32)]
    return pl.pallas_call(
        _inproj_kernel,
        grid=grid,
        in_specs=[tok(d),
                  pl.BlockSpec((bb, 6, d), lambda i, j: (i, 0, 0)),
                  _const_spec((1, d)),
                  _const_spec(wr.shape), _const_spec(wf.shape), _const_spec(wfl.shape), _const_spec(wg.shape),
                  _const_spec((1, WIDTH)), _const_spec((1, WIDTH)), _const_spec((1, N_HEADS)),
                  _const_spec((WIDTH, WIDTH)),
                  pl.BlockSpec((bb, 1, N_HEADS), lambda i, j: (i, 0, 0))],
        out_specs=[tok(c) for c, _ in out_cols],
        out_shape=[jax.ShapeDtypeStruct((b, t, c), dt) for c, dt in out_cols],
        scratch_shapes=[pltpu.VMEM((bb, 1, N_HEADS), F32)],
        compiler_params=_cparams(("parallel", "arbitrary")),
        name="norm1_inproj",
    )(x, mod, g1, wr, wf, wfl, wg, qn, kn, fb, gmat, f0)


def _past_cumsum_kernel(x_ref, o_ref):
    x = x_ref[0]
    rows = x.shape[0]
    li = lax.broadcasted_iota(I32, (LANES, LANES), 0)
    lj = lax.broadcasted_iota(I32, (LANES, LANES), 1)
    same_head = (li % N_HEADS) == (lj % N_HEADS)
    within = jnp.logical_and(same_head, li // N_HEADS <= lj // N_HEADS).astype(BF16)
    xh, xm, xl = _split3(x)
    d2 = functools.partial(jnp.dot, preferred_element_type=F32)
    in_row = d2(xh, within) + (d2(xm, within) + d2(xl, within))
    sh = same_head.astype(BF16)
    row_tot = d2(xh, sh) + (d2(xm, sh) + d2(xl, sh))
    ri = lax.broadcasted_iota(I32, (rows, rows), 0)
    ci = lax.broadcasted_iota(I32, (rows, rows), 1)
    o_ref[0] = in_row + _dot_exact_rhs((ri > ci).astype(F32), row_tot)


def _past_cumsum(past_logf):
    b, p, h = past_logf.shape
    rows = p * h // LANES
    flat = past_logf.reshape(b, rows, LANES)
    out = pl.pallas_call(
        _past_cumsum_kernel,
        grid=(b,),
        in_specs=[pl.BlockSpec((1, rows, LANES), lambda i: (i, 0, 0))],
        out_specs=pl.BlockSpec((1, rows, LANES), lambda i: (i, 0, 0)),
        out_shape=jax.ShapeDtypeStruct((b, rows, LANES), F32),
        compiler_params=_cparams(("parallel",)),
        name="cache_logf_cumsum",
    )(flat)
    return out.reshape(b, p, h)


def _fox_kernel(*refs, n_past_blocks, tq):
    if n_past_blocks:
        (q_ref, fq_ref, sg_ref, kp_ref, vp_ref, fkp_ref, kn_ref, vn_ref, fkn_ref,
         o_ref, m_scr, l_scr, acc_scr) = refs
    else:
        q_ref, fq_ref, sg_ref, kn_ref, vn_ref, fkn_ref, o_ref, m_scr, l_scr, acc_scr = refs
    qi = pl.program_id(1)
    ki = pl.program_id(2)
    nk = pl.num_programs(2)

    @pl.when(ki == 0)
    def _():
        m_scr[...] = jnp.full(m_scr.shape, NEG_BIG, F32)
        l_scr[...] = jnp.zeros(l_scr.shape, F32)
        acc_scr[...] = jnp.zeros(acc_scr.shape, F32)

    lane_a = lax.broadcasted_iota(I32, (tq, LANES), 1) < HEAD_DIM

    def step(k_ref, v_ref, fk_ref, diag):
        tk = k_ref.shape[1]
        if diag:
            rq = lax.broadcasted_iota(I32, (tq, tk), 0)
            ck = lax.broadcasted_iota(I32, (tq, tk), 1)
            visible = ck <= rq
        fq_all = fq_ref[0]
        pairs = range(N_HEADS // 2)
        cols = [slice(j * LANES, (j + 1) * LANES) for j in pairs]
        scores = []
        for j in pairs:
            qj = q_ref[0, :, cols[j]]
            kb = k_ref[0, :, cols[j]].astype(BF16)
            for hh in range(2):
                h = 2 * j + hh
                qm = jnp.where(lane_a if hh == 0 else jnp.logical_not(lane_a), qj, jnp.zeros_like(qj))
                s = lax.dot_general(qm, kb, NT, preferred_element_type=F32)
                s = s + fq_all[:, h:h + 1] - fk_ref[0, h:h + 1, :]
                if diag:
                    s = jnp.where(visible, s, NEG_BIG)
                scores.append(s)
        alphas, probs = [], []
        for h in range(N_HEADS):
            m_old = m_scr[h]
            m_new = jnp.maximum(m_old, jnp.max(scores[h], axis=-1, keepdims=True))
            alpha = jnp.exp(m_old - m_new)
            p = jnp.exp(scores[h] - m_new)
            l_scr[h] = alpha * l_scr[h] + jnp.sum(p, axis=-1, keepdims=True)
            m_scr[h] = m_new
            alphas.append(alpha)
            probs.append(p.astype(BF16))
        for j in pairs:
            vb = v_ref[0, :, cols[j]].astype(BF16)
            pv0 = jnp.dot(probs[2 * j], vb, preferred_element_type=F32)
            pv1 = jnp.dot(probs[2 * j + 1], vb, preferred_element_type=F32)
            acc_scr[:, cols[j]] = (acc_scr[:, cols[j]] * jnp.where(lane_a, alphas[2 * j], alphas[2 * j + 1])
                                   + jnp.where(lane_a, pv0, pv1))

    if n_past_blocks:
        @pl.when(ki < n_past_blocks)
        def _():
            step(kp_ref, vp_ref, fkp_ref, False)

    kn = ki - n_past_blocks

    @pl.when(jnp.logical_and(kn >= 0, kn < qi))
    def _():
        step(kn_ref, vn_ref, fkn_ref, False)

    @pl.when(kn == qi)
    def _():
        step(kn_ref, vn_ref, fkn_ref, True)

    @pl.when(ki == nk - 1)
    def _():
        for j in range(N_HEADS // 2):
            cols = slice(j * LANES, (j + 1) * LANES)
            l = jnp.where(lane_a, l_scr[2 * j], l_scr[2 * j + 1])
            o_ref[0, :, cols] = acc_scr[:, cols] / l * sg_ref[0, :, cols]


def _fox_attention(q, fq, sg, k_new, v_new, fk_new_t, past=None, tq=512, tk_past=512):
    b, t, _ = q.shape
    nq = t // tq
    n_past_blocks = 0 if past is None else past[0].shape[1] // tk_past
    nk = n_past_blocks + nq

    def new_idx(i, qi, ki):
        return jnp.clip(ki - n_past_blocks, 0, qi)

    in_specs = [pl.BlockSpec((1, tq, WIDTH), lambda i, qi, ki: (i, qi, 0)),
                pl.BlockSpec((1, tq, N_HEADS), lambda i, qi, ki: (i, qi, 0)),
                pl.BlockSpec((1, tq, WIDTH), lambda i, qi, ki: (i, qi, 0))]
    args = [q, fq, sg]
    if n_past_blocks:
        def past_idx(i, qi, ki):
            return jnp.minimum(ki, n_past_blocks - 1)
        in_specs += [pl.BlockSpec((1, tk_past, WIDTH), lambda i, qi, ki: (i, past_idx(i, qi, ki), 0)),
                     pl.BlockSpec((1, tk_past, WIDTH), lambda i, qi, ki: (i, past_idx(i, qi, ki), 0)),
                     pl.BlockSpec((1, N_HEADS, tk_past), lambda i, qi, ki: (i, 0, past_idx(i, qi, ki)))]
        args += list(past)
    in_specs += [pl.BlockSpec((1, tq, WIDTH), lambda i, qi, ki: (i, new_idx(i, qi, ki), 0)),
                 pl.BlockSpec((1, tq, WIDTH), lambda i, qi, ki: (i, new_idx(i, qi, ki), 0)),
                 pl.BlockSpec((1, N_HEADS, tq), lambda i, qi, ki: (i, 0, new_idx(i, qi, ki)))]
    args += [k_new, v_new, fk_new_t]
    return pl.pallas_call(
        functools.partial(_fox_kernel, n_past_blocks=n_past_blocks, tq=tq),
        grid=(b, nq, nk),
        in_specs=in_specs,
        out_specs=pl.BlockSpec((1, tq, WIDTH), lambda i, qi, ki: (i, qi, 0)),
        out_shape=jax.ShapeDtypeStruct((b, t, WIDTH), F32),
        scratch_shapes=[pltpu.VMEM((N_HEADS, tq, 1), F32), pltpu.VMEM((N_HEADS, tq, 1), F32),
                        pltpu.VMEM((tq, WIDTH), F32)],
        compiler_params=_cparams(("parallel", "parallel", "arbitrary")),
        name="fox_attention",
    )(*args)


def _rwkv_kernel(p_ref, sh0_ref, s0_ref, mu_ref, w0_ref, wb_ref, a0_ref, ab_ref, gb_ref, kk_ref, ka_ref, rk_ref,
                 lnw_ref, lnb_ref, gm_ref, y_ref, st_ref, sht_ref, z_scr, prev_scr, *, c):
    t = pl.program_id(1)
    nt = pl.num_programs(1)
    n_rows = p_ref.shape[1]
    n_chunks = n_rows // c

    def head_block(h):
        lo = (h % HEADS_PER_GROUP) * HEAD_DIM
        return h // HEADS_PER_GROUP, slice(lo, lo + HEAD_DIM)

    @pl.when(t == 0)
    def _():
        z_scr[...] = jnp.zeros(z_scr.shape, F32)
        for h in range(N_HEADS):
            i, blk = head_block(h)
            z_scr[i, blk, blk] = s0_ref[0, h]
        prev_scr[...] = sh0_ref[0]

    p = p_ref[0]
    row = lax.broadcasted_iota(I32, p.shape, 0)
    prev = jnp.where(row == 0, prev_scr[...], pltpu.roll(p, 1, 0))
    last = p[n_rows - 1:n_rows, :]
    prev_scr[...] = last
    sht_ref[0] = last

    pm = p + (prev - p) * mu_ref[...]
    r = pm[:, 0:WIDTH]
    k = pm[:, WIDTH:2 * WIDTH]
    v = pm[:, 2 * WIDTH:3 * WIDTH]
    o1 = 3 * WIDTH
    wd = pm[:, o1:o1 + DECAY_LORA]
    ad = pm[:, o1 + DECAY_LORA:o1 + DECAY_LORA + ICLR_LORA]
    gd = pm[:, o1 + DECAY_LORA + ICLR_LORA:RWKV_COLS]

    w = -_softplus(-(w0_ref[...] + _dot(jnp.tanh(wd), wb_ref[...]))) - 0.5
    lw = -jnp.exp(w)
    a = _sigmoid(a0_ref[...] + _dot(ad, ab_ref[...]))
    g = _dot(_sigmoid(gd), gb_ref[...])
    kk = k * kk_ref[...]
    kk = kk / jnp.maximum(jnp.sqrt(_gsum(kk * kk, gm_ref)), L2_EPS)
    kf = k * (1.0 + (a - 1.0) * ka_ref[...])

    ri = lax.broadcasted_iota(I32, (n_rows, n_rows), 0)
    ci = lax.broadcasted_iota(I32, (n_rows, n_rows), 1)
    same_chunk = (ri // c) == (ci // c)
    cum = _dot_exact_rhs(jnp.logical_and(same_chunk, ri >= ci).astype(F32), lw)
    cum_last = _dot_exact_rhs(same_chunk.astype(F32), lw)
    r_t = r * jnp.exp(cum)
    a_t = -kk * jnp.exp(cum - lw)
    inv = jnp.exp(-cum)
    b_t = kk * a * inv
    k_t = kf * inv
    to_end = jnp.exp(cum_last - cum)
    b_e = kk * a * to_end
    k_e = kf * to_end
    g_end = jnp.exp(cum_last)

    hg = HEADS_PER_GROUP
    gw = hg * HEAD_DIM
    log_c = int(math.log2(c))
    t_idx = lax.broadcasted_iota(I32, (c, hg * c), 0)
    s_idx = lax.broadcasted_iota(I32, (c, hg * c), 1) & (c - 1)
    strict = s_idx < t_idx
    lower = s_idx <= t_idx
    eye = (s_idx == t_idx).astype(F32)
    rb = lax.broadcasted_iota(I32, (hg * c, gw), 0) >> log_c
    mask_kv = rb == (lax.broadcasted_iota(I32, (hg * c, gw), 1) >> int(math.log2(HEAD_DIM)))
    rs = lax.broadcasted_iota(I32, (hg * c, hg * c), 0) >> log_c
    mask_ss = rs == (lax.broadcasted_iota(I32, (hg * c, hg * c), 1) >> log_c)
    ng = N_HEADS // hg
    cat = functools.partial(jnp.concatenate, axis=0)
    units = [(slice(j * c, (j + 1) * c), slice(i * gw, (i + 1) * gw)) for j in range(n_chunks) for i in range(ng)]
    nu = len(units)

    ar = [_split2(cat([a_t[rs_, s], r_t[rs_, s]])) for rs_, s in units]
    ab = [_mm3(ar[n], _bd_parts(b_t[units[n]], mask_kv), NT) for n in range(nu)]
    ak = [_mm3(ar[n], _bd_parts(k_t[units[n]], mask_kv), NT) for n in range(nu)]
    l_ab = [jnp.where(strict, m[:c], 0.0) for m in ab]
    l_rb = [jnp.where(lower, m[c:], 0.0) for m in ab]
    l_ak = [jnp.where(strict, m[:c], 0.0) for m in ak]
    l_rk = [jnp.where(lower, m[c:], 0.0) for m in ak]
    def mm1(a, b_bd):
        return jnp.dot(a.astype(BF16), b_bd, preferred_element_type=F32)

    def bd1(x, mask):
        tiled = jnp.concatenate([x.astype(BF16)] * hg, axis=0)
        return jnp.where(mask, tiled, jnp.zeros_like(tiled))

    tinv = [eye + m for m in l_ab]
    pw = [mm1(m, bd1(m, mask_ss)) for m in l_ab]
    for _ in range(1, log_c - 1):
        res = [mm1(cat([tinv[n], pw[n]]), bd1(pw[n], mask_ss)) for n in range(nu)]
        tinv = [tinv[n] + res[n][:c] for n in range(nu)]
        pw = [m[c:] for m in res]
    tinv = [tinv[n] + mm1(tinv[n], bd1(pw[n], mask_ss)) for n in range(nu)]
    av = [_mm3(_split2(cat([l_ak[n], l_rk[n]])), _bd_parts(v[units[n]], mask_kv), NN) for n in range(nu)]
    ue = [_split2(cat([b_e[units[n]], k_e[units[n]]])) for n in range(nu)]

    def wide(fn, x):
        return [fn(x[:, :gw]), fn(x[:, gw:])]

    def bd1w(x):
        return jnp.concatenate(wide(lambda h_: bd1(h_, mask_kv), x), axis=1)

    def bd3w(x):
        parts = wide(lambda h_: _bd_parts(h_, mask_kv), x)
        return tuple(jnp.concatenate([parts[0][q], parts[1][q]], axis=1) for q in range(2))

    rhs = [jnp.concatenate([a_t[units[n]], av[n][:c]], axis=1) for n in range(nu)]
    x0 = [mm1(tinv[n], bd1w(rhs[n])) for n in range(nu)]
    resid = [rhs[n] - (x0[n] - _mm3(_split2(l_ab[n]), bd3w(x0[n]), NN)) for n in range(nu)]
    sol = [x0[n] + mm1(tinv[n], bd1w(resid[n])) for n in range(nu)]
    lift = [_mm3(_split2(l_rb[n]), bd3w(sol[n]), NN) for n in range(nu)]
    lhs_s = [_split2(cat([sol[n][:, :gw], r_t[units[n]] + lift[n][:, :gw]])) for n in range(nu)]
    u_loc = [sol[n][:, gw:] for n in range(nu)]
    o_loc = [av[n][c:] + lift[n][:, gw:] for n in range(nu)]

    zr = lax.broadcasted_iota(I32, (gw, gw), 0) >> int(math.log2(HEAD_DIM))
    zmask = zr == (lax.broadcasted_iota(I32, (gw, gw), 1) >> int(math.log2(HEAD_DIM)))
    z = [z_scr[i] for i in range(ng)]
    o_rows = []
    for j in range(n_chunks):
        o_grp = []
        for i in range(ng):
            n = j * ng + i
            rs_, s = units[n]
            sz = _mm3(lhs_s[n], _split2(z[i]), NT)
            u = sz[:c] + u_loc[n]
            o_grp.append(sz[c:] + o_loc[n])
            upd = _mm3(_split2(cat([u, v[rs_, s]])), ue[n], TN)
            z[i] = z[i] * g_end[j * c:j * c + 1, s] + jnp.where(zmask, upd, 0.0)
        o_rows.append(jnp.concatenate(o_grp, axis=1))
    for i in range(ng):
        z_scr[i] = z[i]

    o = cat(o_rows)
    inv_hd = 1.0 / HEAD_DIM
    dlt = o - _gsum(o, gm_ref) * inv_hd
    var = _gsum(dlt * dlt, gm_ref) * inv_hd
    on = dlt * lax.rsqrt(var + RWKV_GN_EPS) * lnw_ref[...] + lnb_ref[...]
    bonus = _gsum(r * kf * rk_ref[...], gm_ref) * v
    y_ref[0] = (on + bonus) * g

    @pl.when(t == nt - 1)
    def _():
        for h in range(N_HEADS):
            i, blk = head_block(h)
            st_ref[0, h] = z_scr[i, blk, blk]


def _rwkv(p, shift0, s0, prm, gmat, chunk, chunks_per_step):
    b, t, _ = p.shape
    row = lambda n: _const_spec((1, n))
    rows = chunk * chunks_per_step
    return pl.pallas_call(
        functools.partial(_rwkv_kernel, c=chunk),
        grid=(b, t // rows),
        in_specs=[pl.BlockSpec((1, rows, RWKV_COLS), lambda i, j: (i, j, 0)),
                  pl.BlockSpec((1, 1, RWKV_COLS), lambda i, j: (i, 0, 0)),
                  pl.BlockSpec((1, N_HEADS, HEAD_DIM, HEAD_DIM), lambda i, j: (i, 0, 0, 0)),
                  row(RWKV_COLS), row(WIDTH), _const_spec((DECAY_LORA, WIDTH)), row(WIDTH),
                  _const_spec((ICLR_LORA, WIDTH)), _const_spec((GATE_LORA, WIDTH)),
                  row(WIDTH), row(WIDTH), row(WIDTH), row(WIDTH), row(WIDTH), _const_spec((WIDTH, WIDTH))],
        out_specs=[pl.BlockSpec((1, rows, WIDTH), lambda i, j: (i, j, 0)),
                   pl.BlockSpec((1, N_HEADS, HEAD_DIM, HEAD_DIM), lambda i, j: (i, 0, 0, 0)),
                   pl.BlockSpec((1, 1, RWKV_COLS), lambda i, j: (i, 0, 0))],
        out_shape=[jax.ShapeDtypeStruct((b, t, WIDTH), F32),
                   jax.ShapeDtypeStruct((b, N_HEADS, HEAD_DIM, HEAD_DIM), F32),
                   jax.ShapeDtypeStruct((b, 1, RWKV_COLS), F32)],
        scratch_shapes=[pltpu.VMEM((N_HEADS // HEADS_PER_GROUP, HEADS_PER_GROUP * HEAD_DIM,
                                    HEADS_PER_GROUP * HEAD_DIM), F32),
                        pltpu.VMEM((1, RWKV_COLS), F32)],
        compiler_params=_cparams(("parallel", "arbitrary")),
        name="rwkv7_mix",
    )(p, shift0, s0, prm["mu"], prm["w0"], prm["wb"], prm["a0"], prm["ab"], prm["gb"], prm["kk"], prm["ka"],
      prm["rk"], prm["lnw"], prm["lnb"], gmat)


def _merge_kernel(x_ref, ya_ref, yb_ref, gate_ref, mod_ref, g2_ref, woa_ref, wob_ref, wo_ref, wrh_ref, wrl_ref,
                  *rest):
    x1_ref, h2_ref, lg_ref = rest[-3:]
    bb, tt, d = x_ref.shape
    m = bb * tt
    gate = gate_ref[...].reshape(m, GATE_COLS)
    merged = (gate[:, 0:d] * _dot(ya_ref[...].reshape(m, WIDTH), woa_ref[...])
              + gate[:, d:2 * d] * _dot(yb_ref[...].reshape(m, WIDTH), wob_ref[...]))
    x1 = x_ref[...] + mod_ref[:, 2:3, :] * _dot(merged, wo_ref[...]).reshape(bb, tt, d)
    x1_ref[...] = x1
    ms = jnp.mean(x1 * x1, axis=-1, keepdims=True)
    h2 = x1 * lax.rsqrt(ms + RMS_EPS) * g2_ref[...]
    h2 = (h2 * (1.0 + mod_ref[:, 4:5, :]) + mod_ref[:, 3:4, :]).reshape(m, d)
    h2_ref[...] = h2
    lg_ref[...] = _mm3((wrh_ref[...], wrl_ref[...]), _split2(h2), NT)


def _merge(x, ya, yb, gate, mod, w, n_total, row_offset, shared=None):
    b, t, d = x.shape
    bb, tt = _token_blocks(b, t)
    nt = t // tt
    m = bb * tt
    off = row_offset // m

    def tok(cols):
        return pl.BlockSpec((bb, tt, cols), lambda i, j: (i, j, 0))

    in_specs = [tok(d), tok(WIDTH), tok(WIDTH), tok(GATE_COLS),
                pl.BlockSpec((bb, 6, d), lambda i, j: (i, 0, 0)),
                _const_spec((1, d)), _const_spec((WIDTH, d)), _const_spec((WIDTH, d)), _const_spec((d, d)),
                _const_spec((N_EXPERTS, d)), _const_spec((N_EXPERTS, d))]
    args = [x, ya, yb, gate, mod.reshape(b, 6, d), w["g2"], w["w_oa"], w["w_ob"], w["w_o"], w["wr_hi"], w["wr_lo"]]
    aliases = {}
    if shared is not None:
        aliases = {len(args): 1, len(args) + 1: 2}
        in_specs += [pl.BlockSpec(memory_space=pl.ANY), pl.BlockSpec(memory_space=pl.ANY)]
        args += list(shared)
    return pl.pallas_call(
        _merge_kernel,
        grid=(b // bb, nt),
        in_specs=in_specs,
        out_specs=[tok(d), pl.BlockSpec((m, d), lambda i, j: (off + i * nt + j, 0)),
                   pl.BlockSpec((N_EXPERTS, m), lambda i, j: (0, off + i * nt + j))],
        out_shape=[jax.ShapeDtypeStruct((b, t, d), F32), jax.ShapeDtypeStruct((n_total, d), F32),
                   jax.ShapeDtypeStruct((N_EXPERTS, n_total), F32)],
        input_output_aliases=aliases,
        compiler_params=_cparams(("parallel", "parallel")),
        name="merge_norm2_router",
    )(*args)


def _route_kernel(lg_ref, bias_ref, idx_ref, wt_ref, rank_ref, cnt_ref, carry):
    @pl.when(pl.program_id(0) == 0)
    def _():
        carry[...] = jnp.zeros(carry.shape, F32)

    tm = lg_ref.shape[1]
    scores = _sigmoid(lg_ref[...])
    sel = scores + bias_ref[...]
    row = lax.broadcasted_iota(I32, (N_EXPERTS, tm), 0)
    neg_inf = -jnp.inf

    def first_argmax(vals, rows):
        mx = jnp.max(vals, axis=0, keepdims=True)
        return mx, jnp.min(jnp.where(vals == mx, rows, N_EXPERTS), axis=0, keepdims=True)

    gslices = [slice(g * EXPERTS_PER_GROUP, (g + 1) * EXPERTS_PER_GROUP) for g in range(N_GROUPS)]
    gs = []
    row_g = lax.broadcasted_iota(I32, (EXPERTS_PER_GROUP, tm), 0)
    for sl in gslices:
        m1, i1 = first_argmax(sel[sl], row_g)
        m2 = jnp.max(jnp.where(row_g == i1, neg_inf, sel[sl]), axis=0, keepdims=True)
        gs.append(m1 + m2)
    kept = []
    for g in range(N_GROUPS):
        beaten = jnp.zeros((1, tm), I32)
        for o in range(N_GROUPS):
            if o != g:
                wins = (gs[o] >= gs[g]) if o < g else (gs[o] > gs[g])
                beaten = beaten + wins.astype(I32)
        kept.append(jnp.where(beaten < TOPK_GROUPS, sel[gslices[g]], neg_inf))
    cur = jnp.concatenate(kept, axis=0)

    idxs, ws = [], []
    picked = jnp.zeros((N_EXPERTS, tm), F32)
    for _ in range(TOP_K):
        _, ik = first_argmax(cur, row)
        hit = row == ik
        idxs.append(ik)
        ws.append(jnp.sum(jnp.where(hit, scores, 0.0), axis=0, keepdims=True))
        cur = jnp.where(hit, neg_inf, cur)
        picked = jnp.where(hit, 1.0, picked)
    wsum = ws[0]
    for k in range(1, TOP_K):
        wsum = wsum + ws[k]

    r = lax.broadcasted_iota(I32, (tm, tm), 0)
    c = lax.broadcasted_iota(I32, (tm, tm), 1)
    before = jnp.dot(picked.astype(BF16), (r < c).astype(BF16), preferred_element_type=F32) + carry[...]
    carry[...] = carry[...] + jnp.sum(picked, axis=1, keepdims=True)
    cnt_ref[...] = carry[...]

    kk = lax.broadcasted_iota(I32, (TOP_K, tm), 0)
    idx_o = jnp.zeros((TOP_K, tm), I32)
    wt_o = jnp.zeros((TOP_K, tm), F32)
    rank_o = jnp.zeros((TOP_K, tm), F32)
    for k in range(TOP_K):
        rk = jnp.sum(jnp.where(row == idxs[k], before, 0.0), axis=0, keepdims=True)
        idx_o = jnp.where(kk == k, idxs[k], idx_o)
        wt_o = jnp.where(kk == k, ws[k] / wsum * ROUTED_SCALE, wt_o)
        rank_o = jnp.where(kk == k, rk, rank_o)
    idx_ref[...] = idx_o
    wt_ref[...] = wt_o
    rank_ref[...] = rank_o.astype(I32)


def _route(logits_t, bias_col, tm):
    n = logits_t.shape[1]
    tokk = pl.BlockSpec((TOP_K, tm), lambda i: (0, i))
    return pl.pallas_call(
        _route_kernel,
        grid=(n // tm,),
        in_specs=[pl.BlockSpec((N_EXPERTS, tm), lambda i: (0, i)), _const_spec((N_EXPERTS, 1))],
        out_specs=[tokk, tokk, tokk, _const_spec((N_EXPERTS, 1))],
        out_shape=[jax.ShapeDtypeStruct((TOP_K, n), I32), jax.ShapeDtypeStruct((TOP_K, n), F32),
                   jax.ShapeDtypeStruct((TOP_K, n), I32), jax.ShapeDtypeStruct((N_EXPERTS, 1), F32)],
        scratch_shapes=[pltpu.VMEM((N_EXPERTS, 1), F32)],
        compiler_params=_cparams(("arbitrary",)),
        name="route_topk",
    )(logits_t, bias_col)


def _plan_kernel(cnt_ref, start_ref, be_ref, valid_ref, nu_ref, *, blk):
    cnt = cnt_ref[...]
    padded = jnp.ceil(cnt * (1.0 / blk)) * blk
    e_r = lax.broadcasted_iota(I32, (N_EXPERTS, N_EXPERTS), 0)
    e_c = lax.broadcasted_iota(I32, (N_EXPERTS, N_EXPERTS), 1)
    incl = (e_r <= e_c).astype(BF16)
    ph, pm, plo = _split3(jnp.broadcast_to(padded, (8, N_EXPERTS)))
    d2 = functools.partial(jnp.dot, preferred_element_type=F32)
    pad_end = (d2(ph, incl) + (d2(pm, incl) + d2(plo, incl)))[0:1, :]
    pad_start = pad_end - padded
    start_ref[...] = pad_start.astype(I32)
    total = jnp.max(pad_end, axis=-1, keepdims=True)
    nu_ref[...] = jnp.broadcast_to(total * (1.0 / blk), (1, N_EXPERTS)).astype(I32)
    nb = be_ref.shape[0]
    first = (lax.broadcasted_iota(I32, (nb, N_EXPERTS), 0) * blk).astype(F32)
    lane = lax.broadcasted_iota(I32, (nb, N_EXPERTS), 1)
    inside = jnp.logical_and(pad_start <= first, first < pad_end)
    be_ref[...] = jnp.sum(jnp.where(inside, lane, 0), axis=-1, keepdims=True)
    rows = jnp.minimum(pad_start + cnt - first, float(blk))
    valid_ref[...] = jnp.sum(jnp.where(inside, rows, 0.0), axis=-1, keepdims=True).astype(I32)


def _plan(counts, n_blocks, blk):
    return pl.pallas_call(
        functools.partial(_plan_kernel, blk=blk),
        out_shape=[jax.ShapeDtypeStruct((1, N_EXPERTS), I32), jax.ShapeDtypeStruct((n_blocks, 1), I32),
                   jax.ShapeDtypeStruct((n_blocks, 1), I32), jax.ShapeDtypeStruct((1, N_EXPERTS), I32)],
        compiler_params=pltpu.CompilerParams(vmem_limit_bytes=VMEM_LIMIT),
        name="dispatch_plan",
    )(counts)


def _dest_kernel(idx_ref, rank_ref, start_ref, dest_ref):
    tm = idx_ref.shape[1]
    row = lax.broadcasted_iota(I32, (N_EXPERTS, tm), 0)
    kk = lax.broadcasted_iota(I32, (TOP_K, tm), 0)
    idx = idx_ref[...]
    base = jnp.zeros((TOP_K, tm), I32)
    for k in range(TOP_K):
        bk = jnp.sum(jnp.where(row == idx[k:k + 1, :], start_ref[...], 0), axis=0, keepdims=True)
        base = jnp.where(kk == k, bk, base)
    dest_ref[...] = base + rank_ref[...]


def _dest(idx, rank, pad_start_col, tm):
    n = idx.shape[1]
    tokk = pl.BlockSpec((TOP_K, tm), lambda i: (0, i))
    return pl.pallas_call(
        _dest_kernel,
        grid=(n // tm,),
        in_specs=[tokk, tokk, _const_spec((N_EXPERTS, 1))],
        out_specs=tokk,
        out_shape=jax.ShapeDtypeStruct((TOP_K, n), I32),
        compiler_params=_cparams(("parallel",)),
        name="dispatch_dest",
    )(idx, rank, pad_start_col)


def _dispatch_kernel(dest_ref, h2_ref, xs_ref, sem):
    tm = h2_ref.shape[0]

    def issue(r, carry):
        for k in range(TOP_K):
            pltpu.make_async_copy(h2_ref.at[pl.ds(r, 1)], xs_ref.at[pl.ds(dest_ref[k, r], 1)], sem).start()
        return carry

    lax.fori_loop(0, tm, issue, 0)
    for k in range(TOP_K):
        pltpu.make_async_copy(h2_ref, xs_ref.at[pl.ds(0, tm)], sem).wait()


def _dispatch(h2, dest_t, n_slots, tm):
    n, d = h2.shape
    return pl.pallas_call(
        _dispatch_kernel,
        grid=(n // tm,),
        in_specs=[pl.BlockSpec((TOP_K, tm), lambda i: (0, i), memory_space=pltpu.SMEM),
                  pl.BlockSpec((tm, d), lambda i: (i, 0))],
        out_specs=pl.BlockSpec(memory_space=pl.ANY),
        out_shape=jax.ShapeDtypeStruct((n_slots, d), F32),
        scratch_shapes=[pltpu.SemaphoreType.DMA(())],
        compiler_params=_cparams(("arbitrary",)),
        name="moe_dispatch",
    )(dest_t, h2)


def _expert_kernel(be_ref, valid_ref, nu_ref, x_ref, wg_ref, wu_ref, wd_ref, y_ref, wg_b, wu_b, wd_b):
    i = pl.program_id(0)
    nv = valid_ref[i]
    new_expert = jnp.logical_or(i == 0, be_ref[i] != be_ref[jnp.maximum(i - 1, 0)])

    @pl.when(jnp.logical_and(nv > 0, new_expert))
    def _():
        wg_b[...] = wg_ref[0].astype(BF16)
        wu_b[...] = wu_ref[0].astype(BF16)
        wd_b[...] = wd_ref[0].astype(BF16)

    @pl.when(nv > 0)
    def _():
        blk = x_ref.shape[0]
        rows = lax.broadcasted_iota(I32, (blk, 1), 0)
        x = jnp.where(rows < nv, x_ref[...], 0.0).astype(BF16)
        hg = jnp.dot(x, wg_b[...], preferred_element_type=F32)
        hu = jnp.dot(x, wu_b[...], preferred_element_type=F32)
        y_ref[...] = jnp.dot((_silu(hg) * hu).astype(BF16), wd_b[...], preferred_element_type=F32)


def _experts(xs, block_e, valid, n_used, w_eg, w_eu, w_ed, blk):
    n_slots, d = xs.shape
    n_blocks = n_slots // blk

    def row_blk(i, be, valid, nu):
        return (jnp.minimum(i, nu[0] - 1), 0)

    def w_blk(i, be, valid, nu):
        return (be[i], 0, 0)

    return pl.pallas_call(
        _expert_kernel,
        grid_spec=pltpu.PrefetchScalarGridSpec(
            num_scalar_prefetch=3,
            grid=(n_blocks,),
            in_specs=[pl.BlockSpec((blk, d), row_blk),
                      pl.BlockSpec((1, d, D_EXPERT), w_blk), pl.BlockSpec((1, d, D_EXPERT), w_blk),
                      pl.BlockSpec((1, D_EXPERT, d), w_blk)],
            out_specs=pl.BlockSpec((blk, d), row_blk),
            scratch_shapes=[pltpu.VMEM((d, D_EXPERT), BF16), pltpu.VMEM((d, D_EXPERT), BF16),
                            pltpu.VMEM((D_EXPERT, d), BF16)]),
        out_shape=jax.ShapeDtypeStruct((n_slots, d), F32),
        compiler_params=_cparams(("arbitrary",)),
        name="moe_experts",
    )(block_e, valid, n_used, xs, w_eg, w_eu, w_ed)


def _final_kernel(dest_ref, x1_ref, h2_ref, wt_ref, mod_ref, wsg_ref, wsu_ref, wsd_ref, ys_ref, o_ref, ybuf, sem):
    bb, tt, d = x1_ref.shape
    m = bb * tt

    def issue(r, carry):
        for k in range(TOP_K):
            pltpu.make_async_copy(ys_ref.at[pl.ds(dest_ref[k, r], 1)], ybuf.at[k, pl.ds(r, 1)], sem).start()
        return carry

    lax.fori_loop(0, m, issue, 0)

    hb = h2_ref[...].astype(BF16)
    hg = jnp.dot(hb, wsg_ref[...], preferred_element_type=F32)
    hu = jnp.dot(hb, wsu_ref[...], preferred_element_type=F32)
    ffn = _dot(_silu(hg) * hu, wsd_ref[...])

    for k in range(TOP_K):
        pltpu.make_async_copy(ys_ref.at[pl.ds(0, m)], ybuf.at[k], sem).wait()

    wt = wt_ref[...]
    for k in range(TOP_K):
        ffn = ffn + wt[:, k:k + 1] * ybuf[k]
    o_ref[...] = x1_ref[...] + mod_ref[:, 5:6, :] * ffn.reshape(bb, tt, d)


def _final(x1, h2_all, wts_all, dest_t, ys, mod, w, row_offset):
    b, t, d = x1.shape
    bb, tt = _token_blocks(b, t)
    nt = t // tt
    m = bb * tt
    off = row_offset // m

    def flat_idx(i, j):
        return off + i * nt + j

    return pl.pallas_call(
        _final_kernel,
        grid=(b // bb, nt),
        in_specs=[pl.BlockSpec((TOP_K, m), lambda i, j: (0, flat_idx(i, j)), memory_space=pltpu.SMEM),
                  pl.BlockSpec((bb, tt, d), lambda i, j: (i, j, 0)),
                  pl.BlockSpec((m, d), lambda i, j: (flat_idx(i, j), 0)),
                  pl.BlockSpec((m, TOP_K), lambda i, j: (flat_idx(i, j), 0)),
                  pl.BlockSpec((bb, 6, d), lambda i, j: (i, 0, 0)),
                  _const_spec((d, D_EXPERT)), _const_spec((d, D_EXPERT)), _const_spec((D_EXPERT, d)),
                  pl.BlockSpec(memory_space=pl.ANY)],
        out_specs=pl.BlockSpec((bb, tt, d), lambda i, j: (i, j, 0)),
        out_shape=jax.ShapeDtypeStruct((b, t, d), F32),
        scratch_shapes=[pltpu.VMEM((TOP_K, m, d), F32), pltpu.SemaphoreType.DMA(())],
        compiler_params=_cparams(("arbitrary", "arbitrary")),
        name="moe_combine_final",
    )(dest_t, x1, h2_all, wts_all, mod.reshape(b, 6, d), w["w_sg"], w["w_su"], w["w_sd"], ys)


def _moe_routed(h2_all, logits_all, w, blk=256, tm=256):
    n = h2_all.shape[0]
    n_blocks = (n * TOP_K + N_EXPERTS * (blk - 1)) // blk + 1
    n_blocks = (n_blocks + 7) // 8 * 8
    idx, wts_t, rank, counts = _route(logits_all, w["router_bias"], tm)
    pad_start, block_e, valid, n_used = _plan(counts.reshape(1, N_EXPERTS), n_blocks, blk)
    block_e = block_e.reshape(n_blocks)
    valid = valid.reshape(n_blocks)
    n_used = n_used[0, 0:1]
    dest_t = _dest(idx, rank, pad_start.reshape(N_EXPERTS, 1), tm)
    xs = _dispatch(h2_all, dest_t, n_blocks * blk, tm)
    ys = _experts(xs, block_e, valid, n_used, w["w_eg"], w["w_eu"], w["w_ed"], blk)
    return ys, dest_t, jnp.transpose(wts_t)


def _prep(raw):
    p = {k: v[0] for k, v in raw.items()}
    w_in = p["w_in"]
    o_fox = RWKV_COLS
    o_fl = o_fox + FOX_MAIN_COLS
    o_gate = o_fl + N_HEADS
    row = lambda a: a.reshape(1, -1)
    return dict(
        w_ada=p["w_ada"], b_ada=p["b_ada"],
        g1=row(p["norm1_g"]), g2=row(p["norm2_g"]),
        wr=w_in[:, :o_fox].astype(BF16),
        wf=w_in[:, o_fox:o_fl].astype(BF16),
        wfl=jnp.pad(w_in[:, o_fl:o_gate], ((0, 0), (0, LANES - N_HEADS))).astype(BF16),
        wg=w_in[:, o_gate:].astype(BF16),
        qn=row(jnp.tile(p["fox_q_norm"], N_HEADS)), kn=row(jnp.tile(p["fox_k_norm"], N_HEADS)),
        fb=row(p["fox_f_bias"]),
        gmat=_group_ones(),
        rwkv=dict(mu=row(p["rwkv_mu"]), w0=row(p["rwkv_w0"]), wb=p["rwkv_w_lora_b"], a0=row(p["rwkv_a0"]),
                  ab=p["rwkv_a_lora_b"], gb=p["rwkv_g_lora_b"], kk=row(p["rwkv_k_k"]), ka=row(p["rwkv_k_a"]),
                  rk=row(p["rwkv_r_k"]), lnw=row(p["rwkv_ln_w"]), lnb=row(p["rwkv_ln_b"])),
        w_oa=p["w_out_rwkv"].astype(BF16), w_ob=p["w_out_fox"].astype(BF16), w_o=p["w_out"].astype(BF16),
        wr_hi=p["w_router"].T.astype(BF16),
        wr_lo=(p["w_router"] - p["w_router"].astype(BF16).astype(F32)).T.astype(BF16),
        router_bias=p["router_bias"].reshape(N_EXPERTS, 1),
        w_eg=p["w_exp_gate"], w_eu=p["w_exp_up"], w_ed=p["w_exp_down"],
        w_sg=p["w_sh_gate"].astype(BF16), w_su=p["w_sh_up"].astype(BF16), w_sd=p["w_sh_down"].astype(BF16),
    )


def _token_blocks(b, t):
    if t >= 256:
        return 1, 256
    bb = max(1, min(b, 256 // t))
    while b % bb:
        bb -= 1
    return bb, t


def _mix_path(x, mod, shift0, wkv0, past_k, past_v, past_logf, w):
    b, t, d = x.shape
    bb, tt = _token_blocks(b, t)
    n_past = past_k.shape[1]
    if n_past:
        f_past = _past_cumsum(past_logf)
        init = f_past[:, n_past - 1:n_past, :]
        past = (past_k, past_v, jnp.swapaxes(f_past, 1, 2))
    else:
        init = jnp.zeros((b, 1, N_HEADS), F32)
        past = None
    pr, q, k, v, sg, logf, f_new, gate = _inproj(x, mod.reshape(b, 6, d), w["g1"], w["wr"], w["wf"], w["wfl"],
                                                 w["wg"], w["qn"], w["kn"], w["fb"], w["gmat"], init, bb, tt)
    y_fox = _fox_attention(q, f_new, sg, k, v, jnp.swapaxes(f_new, 1, 2), past=past, tq=min(t, 512),
                           tk_past=min(max(n_past, 1), 512))
    chunk = min(t, RWKV_CHUNK)
    y_rwkv, wkv_new, shift_new = _rwkv(pr, shift0.reshape(b, 1, RWKV_COLS), wkv0, w["rwkv"], w["gmat"],
                                       chunk, max(1, min(RWKV_CHUNKS_PER_STEP, t // chunk)))
    return y_rwkv, y_fox, gate, wkv_new, shift_new, k, v, logf


def _layer(paths, w):
    n_b = [p[0].shape[0] for p in paths]
    mod_all = _ada(jnp.concatenate([p[1] for p in paths], axis=0), w["w_ada"], w["b_ada"])
    mods, o = [], 0
    for nb in n_b:
        mods.append(mod_all[o:o + nb])
        o += nb
    n_total = sum(p[0].shape[0] * p[0].shape[1] for p in paths)
    mixed, x1s = [], []
    shared, row = None, 0
    for (x, _, shift0, wkv0, pk, pv, plf), mod in zip(paths, mods):
        ya, yb, gate, wkv_new, shift_new, k, v, logf = _mix_path(x, mod, shift0, wkv0, pk, pv, plf, w)
        x1, h2_all, lg_all = _merge(x, ya, yb, gate, mod, w, n_total, row, shared)
        shared = (h2_all, lg_all)
        row += x.shape[0] * x.shape[1]
        mixed.append((wkv_new, shift_new, k, v, logf))
        x1s.append(x1)
    ys, dest_t, wts = _moe_routed(h2_all, lg_all, w)
    outs, row = [], 0
    for x1, mod, st in zip(x1s, mods, mixed):
        y = _final(x1, h2_all, wts, dest_t, ys, mod, w, row)
        row += x1.shape[0] * x1.shape[1]
        outs.append((y,) + st)
    return outs


def kernel(x_prompt, x_sample, c_prompt, c_sample, state_rwkv_wkv, state_rwkv_shift, cache_fox_k, cache_fox_v,
           cache_fox_logf, w_ada, b_ada, norm1_g, norm2_g, w_in, rwkv_mu, rwkv_w0, rwkv_w_lora_b, rwkv_a0,
           rwkv_a_lora_b, rwkv_g_lora_b, rwkv_k_k, rwkv_k_a, rwkv_r_k, rwkv_ln_w, rwkv_ln_b, fox_q_norm,
           fox_k_norm, fox_f_bias, w_out_rwkv, w_out_fox, w_out, w_router, router_bias, w_exp_gate, w_exp_up,
           w_exp_down, w_sh_gate, w_sh_up, w_sh_down):
    raw = dict(w_ada=w_ada, b_ada=b_ada, norm1_g=norm1_g, norm2_g=norm2_g, w_in=w_in, rwkv_mu=rwkv_mu,
               rwkv_w0=rwkv_w0, rwkv_w_lora_b=rwkv_w_lora_b, rwkv_a0=rwkv_a0, rwkv_a_lora_b=rwkv_a_lora_b,
               rwkv_g_lora_b=rwkv_g_lora_b, rwkv_k_k=rwkv_k_k, rwkv_k_a=rwkv_k_a, rwkv_r_k=rwkv_r_k,
               rwkv_ln_w=rwkv_ln_w, rwkv_ln_b=rwkv_ln_b, fox_q_norm=fox_q_norm, fox_k_norm=fox_k_norm,
               fox_f_bias=fox_f_bias, w_out_rwkv=w_out_rwkv, w_out_fox=w_out_fox, w_out=w_out,
               w_router=w_router, router_bias=router_bias, w_exp_gate=w_exp_gate, w_exp_up=w_exp_up,
               w_exp_down=w_exp_down, w_sh_gate=w_sh_gate, w_sh_up=w_sh_up, w_sh_down=w_sh_down)
    assert w_in.shape[0] == 1, "single-layer stack"
    w = _prep(raw)
    bp, tp, _ = x_prompt.shape
    bs, ts, _ = x_sample.shape
    n_past = cache_fox_k.shape[2]
    prompt = (x_prompt, c_prompt, jnp.zeros((bp, RWKV_COLS), F32),
              jnp.zeros((bp, N_HEADS, HEAD_DIM, HEAD_DIM), F32),
              jnp.zeros((bp, 0, WIDTH), F32), jnp.zeros((bp, 0, WIDTH), F32), jnp.zeros((bp, 0, N_HEADS), F32))
    sample = (x_sample, c_sample, state_rwkv_shift[0], state_rwkv_wkv[0],
              cache_fox_k[0].reshape(bs, n_past, WIDTH), cache_fox_v[0].reshape(bs, n_past, WIDTH),
              cache_fox_logf[0])
    (yp, wkv_p, sh_p, k_p, v_p, lf_p), (ysm, wkv_s, sh_s, k_s, v_s, lf_s) = _layer([prompt, sample], w)

    def heads(a):
        return a.reshape((1,) + a.shape[:2] + (N_HEADS, HEAD_DIM))

    return (yp, ysm,
            wkv_p[None], sh_p.reshape(1, bp, RWKV_COLS), heads(k_p), heads(v_p), lf_p[None],
            wkv_s[None], sh_s.reshape(1, bs, RWKV_COLS), heads(k_s), heads(v_s), lf_s[None])
```

```python
import functools
import math

import jax
import jax.numpy as jnp
from jax import lax
from jax.experimental import pallas as pl
from jax.experimental.pallas import tpu as pltpu

F32 = jnp.float32
BF16 = jnp.bfloat16
I32 = jnp.int32

D_MODEL = 1024
N_HEADS = 8
HEAD_DIM = 64
WIDTH = N_HEADS * HEAD_DIM
HEADS_PER_GROUP = 4
RWKV_CHUNK = 64
RWKV_CHUNKS_PER_STEP = 4
DECAY_LORA = 64
ICLR_LORA = 64
GATE_LORA = 128
RWKV_COLS = 3 * WIDTH + DECAY_LORA + ICLR_LORA + GATE_LORA
FOX_MAIN_COLS = 4 * WIDTH
GATE_COLS = 2 * D_MODEL
RWKV_GN_EPS = HEAD_DIM * 1e-5
L2_EPS = 1e-12
RMS_EPS = 1e-6
N_EXPERTS = 256
TOP_K = 8
N_GROUPS = 8
TOPK_GROUPS = 4
EXPERTS_PER_GROUP = N_EXPERTS // N_GROUPS
D_EXPERT = 256
ROUTED_SCALE = 2.5

LANES = 128
VMEM_LIMIT = 56 * 1024 * 1024
NEG_BIG = -1e30

NN = (((1,), (0,)), ((), ()))
NT = (((1,), (1,)), ((), ()))
TN = (((0,), (0,)), ((), ()))


def _cparams(sem):
    return pltpu.CompilerParams(dimension_semantics=sem, vmem_limit_bytes=VMEM_LIMIT)


def _dot(a, b, dims=NN):
    return lax.dot_general(a.astype(BF16), b.astype(BF16), dims, preferred_element_type=F32)


def _split2(a):
    hi = a.astype(BF16)
    lo = (a - hi.astype(F32)).astype(BF16)
    return hi, lo


def _split3(a):
    hi = a.astype(BF16)
    r1 = a - hi.astype(F32)
    mid = r1.astype(BF16)
    lo = (r1 - mid.astype(F32)).astype(BF16)
    return hi, mid, lo


def _dot3(a, b, dims=NN):
    ah, al = _split2(a)
    bh, bl = _split2(b)
    d = functools.partial(lax.dot_general, dimension_numbers=dims, preferred_element_type=F32)
    return d(ah, bh) + (d(ah, bl) + d(al, bh))


def _mm3(a, b, dims):
    d = functools.partial(lax.dot_general, dimension_numbers=dims, preferred_element_type=F32)
    return d(a[0], b[0]) + (d(a[0], b[1]) + d(a[1], b[0]))


def _bd_parts(x, mask):
    out = []
    for part in _split2(x):
        tiled = jnp.concatenate([part] * HEADS_PER_GROUP, axis=0)
        out.append(jnp.where(mask, tiled, jnp.zeros_like(tiled)))
    return tuple(out)


def _dot_exact_rhs(a_exact, b, dims=NN):
    ab = a_exact.astype(BF16)
    bh, bm, bl = _split3(b)
    d = functools.partial(lax.dot_general, dimension_numbers=dims, preferred_element_type=F32)
    return d(ab, bh) + (d(ab, bm) + d(ab, bl))


def _gsum(x, g_ref):
    hi, mid, lo = _split3(x)
    g = g_ref[...]
    d = functools.partial(jnp.dot, preferred_element_type=F32)
    return d(hi, g) + (d(mid, g) + d(lo, g))


def _sigmoid(x):
    return 1.0 / (1.0 + jnp.exp(-x))


def _softplus(x):
    return jnp.maximum(x, 0.0) + jnp.log1p(jnp.exp(-jnp.abs(x)))


def _silu(x):
    return x * _sigmoid(x)


def _pack_bf16_pairs(x):
    w = x.shape[1] // 2
    bits = lax.bitcast_convert_type(x.astype(BF16).astype(F32), I32)
    return lax.shift_right_logical(bits[:, :w], 16) | (bits[:, w:] & -65536)


def _unpack_bf16_pairs(p):
    lo = lax.bitcast_convert_type(lax.shift_left(p, 16), F32)
    hi = lax.bitcast_convert_type(p & -65536, F32)
    return jnp.concatenate([lo, hi], axis=1).astype(BF16)


def _group_ones():
    h = jnp.arange(WIDTH, dtype=I32) // HEAD_DIM
    return (h[:, None] == h[None, :]).astype(BF16)


def _ada_kernel(c_ref, w_ref, b_ref, o_ref):
    o_ref[...] = _dot(_silu(c_ref[...]), w_ref[...]) + b_ref[...]


def _ada(c, w_ada, b_ada):
    nb = c.shape[0]
    n_out = w_ada.shape[1]
    blk = D_MODEL
    return pl.pallas_call(
        _ada_kernel,
        grid=(n_out // blk,),
        in_specs=[pl.BlockSpec((nb, D_MODEL), lambda j: (0, 0)),
                  pl.BlockSpec((D_MODEL, blk), lambda j: (0, j)),
                  pl.BlockSpec((1, blk), lambda j: (0, j))],
        out_specs=pl.BlockSpec((nb, blk), lambda j: (0, j)),
        out_shape=jax.ShapeDtypeStruct((nb, n_out), F32),
        compiler_params=_cparams(("parallel",)),
        name="ada_mod",
    )(c, w_ada, b_ada.reshape(1, n_out))


def _inproj_kernel(x_ref, mod_ref, g1_ref, wr_ref, wf_ref, wfl_ref, wg_ref, qn_ref, kn_ref, fb_ref, gm_ref, f0_ref,
                   pr_ref, q_ref, k_ref, v_ref, sg_ref, lf_ref, cf_ref, gate_ref, carry):
    bb, tt, d = x_ref.shape
    m = bb * tt
    x = x_ref[...]
    ms = jnp.mean(x * x, axis=-1, keepdims=True)
    h = x * lax.rsqrt(ms + RMS_EPS) * g1_ref[...]
    h = h * (1.0 + mod_ref[:, 1:2, :]) + mod_ref[:, 0:1, :]
    hb = h.reshape(m, d).astype(BF16)

    pr_ref[...] = jnp.dot(hb, wr_ref[...], preferred_element_type=F32).reshape(bb, tt, RWKV_COLS)

    f = jnp.dot(hb, wf_ref[...], preferred_element_type=F32)
    q = f[:, 0:WIDTH]
    k = f[:, WIDTH:2 * WIDTH]
    v = f[:, 2 * WIDTH:3 * WIDTH]
    og = f[:, 3 * WIDTH:4 * WIDTH]
    inv_hd = 1.0 / HEAD_DIM
    q = q * lax.rsqrt(_gsum(q * q, gm_ref) * inv_hd + RMS_EPS) * qn_ref[...]
    k = k * lax.rsqrt(_gsum(k * k, gm_ref) * inv_hd + RMS_EPS) * kn_ref[...]
    q_ref[...] = (q * (HEAD_DIM ** -0.5)).astype(BF16).reshape(bb, tt, WIDTH)
    k_ref[...] = k.reshape(bb, tt, WIDTH)
    v_ref[...] = v.reshape(bb, tt, WIDTH)
    sg_ref[...] = _sigmoid(og).reshape(bb, tt, WIDTH)

    fl = jnp.dot(hb, wfl_ref[...], preferred_element_type=F32)[:, 0:N_HEADS] + fb_ref[...]
    lf = -_softplus(-fl)
    lf_ref[...] = lf.reshape(bb, tt, N_HEADS)

    @pl.when(pl.program_id(1) == 0)
    def _():
        carry[...] = f0_ref[...]

    r = lax.broadcasted_iota(I32, (m, m), 0)
    c = lax.broadcasted_iota(I32, (m, m), 1)
    tri = jnp.logical_and(r // tt == c // tt, r >= c).astype(F32)
    cf = _dot_exact_rhs(tri, lf).reshape(bb, tt, N_HEADS) + carry[...]
    cf_ref[...] = cf
    carry[...] = cf[:, tt - 1:tt, :]

    gate_ref[...] =_sigmoid(jnp.dot(hb, wg_ref[...], preferred_element_type=F32)).reshape(bb, tt, GATE_COLS)


def _const_spec(shape):
    nd = len(shape)
    return pl.BlockSpec(shape, lambda *_: (0,) * nd)


def _inproj(x, mod, g1, wr, wf, wfl, wg, qn, kn, fb, gmat, f0, bb, tt):
    b, t, d = x.shape
    grid = (b // bb, t // tt)

    def tok(cols):
        return pl.BlockSpec((bb, tt, cols), lambda i, j: (i, j, 0))

    out_cols = [(RWKV_COLS, F32), (WIDTH, BF16), (WIDTH, F32), (WIDTH, F32), (WIDTH, F32), (N_HEADS, F32),
                (N_HEADS, F32), (GATE_COLS, F32)]
    return pl.pallas_call(
        _inproj_kernel,
        grid=grid,
        in_specs=[tok(d),
                  pl.BlockSpec((bb, 6, d), lambda i, j: (i, 0, 0)),
                  _const_spec((1, d)),
                  _const_spec(wr.shape), _const_spec(wf.shape), _const_spec(wfl.shape), _const_spec(wg.shape),
                  _const_spec((1, WIDTH)), _const_spec((1, WIDTH)), _const_spec((1, N_HEADS)),
                  _const_spec((WIDTH, WIDTH)),
                  pl.BlockSpec((bb, 1, N_HEADS), lambda i, j: (i, 0, 0))],
        out_specs=[tok(c) for c, _ in out_cols],
        out_shape=[jax.ShapeDtypeStruct((b, t, c), dt) for c, dt in out_cols],
        scratch_shapes=[pltpu.VMEM((bb, 1, N_HEADS), F32)],
        compiler_params=_cparams(("parallel", "arbitrary")),
        name="norm1_inproj",
    )(x, mod, g1, wr, wf, wfl, wg, qn, kn, fb, gmat, f0)


def _past_cumsum_kernel(x_ref, o_ref):
    x = x_ref[0]
    rows = x.shape[0]
    li = lax.broadcasted_iota(I32, (LANES, LANES), 0)
    lj = lax.broadcasted_iota(I32, (LANES, LANES), 1)
    same_head = (li % N_HEADS) == (lj % N_HEADS)
    within = jnp.logical_and(same_head, li // N_HEADS <= lj // N_HEADS).astype(BF16)
    xh, xm, xl = _split3(x)
    d2 = functools.partial(jnp.dot, preferred_element_type=F32)
    in_row = d2(xh, within) + (d2(xm, within) + d2(xl, within))
    sh = same_head.astype(BF16)
    row_tot = d2(xh, sh) + (d2(xm, sh) + d2(xl, sh))
    ri = lax.broadcasted_iota(I32, (rows, rows), 0)
    ci = lax.broadcasted_iota(I32, (rows, rows), 1)
    o_ref[0] = in_row + _dot_exact_rhs((ri > ci).astype(F32), row_tot)


def _past_cumsum(past_logf):
    b, p, h = past_logf.shape
    rows = p * h // LANES
    flat = past_logf.reshape(b, rows, LANES)
    out = pl.pallas_call(
        _past_cumsum_kernel,
        grid=(b,),
        in_specs=[pl.BlockSpec((1, rows, LANES), lambda i: (i, 0, 0))],
        out_specs=pl.BlockSpec((1, rows, LANES), lambda i: (i, 0, 0)),
        out_shape=jax.ShapeDtypeStruct((b, rows, LANES), F32),
        compiler_params=_cparams(("parallel",)),
        name="cache_logf_cumsum",
    )(flat)
    return out.reshape(b, p, h)


def _fox_kernel(*refs, n_past_blocks, tq):
    if n_past_blocks:
        (q_ref, fq_ref, sg_ref, kp_ref, vp_ref, fkp_ref, kn_ref, vn_ref, fkn_ref,
         o_ref, m_scr, l_scr, acc_scr) = refs
    else:
        q_ref, fq_ref, sg_ref, kn_ref, vn_ref, fkn_ref, o_ref, m_scr, l_scr, acc_scr = refs
    qi = pl.program_id(1)
    ki = pl.program_id(2)
    nk = pl.num_programs(2)

    @pl.when(ki == 0)
    def _():
        m_scr[...] = jnp.full(m_scr.shape, NEG_BIG, F32)
        l_scr[...] = jnp.zeros(l_scr.shape, F32)
        acc_scr[...] = jnp.zeros(acc_scr.shape, F32)

    lane_a = lax.broadcasted_iota(I32, (tq, LANES), 1) < HEAD_DIM

    def step(k_ref, v_ref, fk_ref, diag):
        tk = k_ref.shape[1]
        if diag:
            rq = lax.broadcasted_iota(I32, (tq, tk), 0)
            ck = lax.broadcasted_iota(I32, (tq, tk), 1)
            visible = ck <= rq
        fq_all = fq_ref[0]
        pairs = range(N_HEADS // 2)
        cols = [slice(j * LANES, (j + 1) * LANES) for j in pairs]
        scores = []
        for j in pairs:
            qj = q_ref[0, :, cols[j]]
            kb = k_ref[0, :, cols[j]].astype(BF16)
            for hh in range(2):
                h = 2 * j + hh
                qm = jnp.where(lane_a if hh == 0 else jnp.logical_not(lane_a), qj, jnp.zeros_like(qj))
                s = lax.dot_general(qm, kb, NT, preferred_element_type=F32)
                s = s + fq_all[:, h:h + 1] - fk_ref[0, h:h + 1, :]
                if diag:
                    s = jnp.where(visible, s, NEG_BIG)
                scores.append(s)
        alphas, probs = [], []
        for h in range(N_HEADS):
            m_old = m_scr[h]
            m_new = jnp.maximum(m_old, jnp.max(scores[h], axis=-1, keepdims=True))
            alpha = jnp.exp(m_old - m_new)
            p = jnp.exp(scores[h] - m_new)
            l_scr[h] = alpha * l_scr[h] + jnp.sum(p, axis=-1, keepdims=True)
            m_scr[h] = m_new
            alphas.append(alpha)
            probs.append(p.astype(BF16))
        for j in pairs:
            vb = v_ref[0, :, cols[j]].astype(BF16)
            pv0 = jnp.dot(probs[2 * j], vb, preferred_element_type=F32)
            pv1 = jnp.dot(probs[2 * j + 1], vb, preferred_element_type=F32)
            acc_scr[:, cols[j]] = (acc_scr[:, cols[j]] * jnp.where(lane_a, alphas[2 * j], alphas[2 * j + 1])
                                   + jnp.where(lane_a, pv0, pv1))

    if n_past_blocks:
        @pl.when(ki < n_past_blocks)
        def _():
            step(kp_ref, vp_ref, fkp_ref, False)

    kn = ki - n_past_blocks

    @pl.when(jnp.logical_and(kn >= 0, kn < qi))
    def _():
        step(kn_ref, vn_ref, fkn_ref, False)

    @pl.when(kn == qi)
    def _():
        step(kn_ref, vn_ref, fkn_ref, True)

    @pl.when(ki == nk - 1)
    def _():
        for j in range(N_HEADS // 2):
            cols = slice(j * LANES, (j + 1) * LANES)
            l = jnp.where(lane_a, l_scr[2 * j], l_scr[2 * j + 1])
            o_ref[0, :, cols] = acc_scr[:, cols] / l * sg_ref[0, :, cols]


def _fox_attention(q, fq, sg, k_new, v_new, fk_new_t, past=None, tq=512, tk_past=512):
    b, t, _ = q.shape
    nq = t // tq
    n_past_blocks = 0 if past is None else past[0].shape[1] // tk_past
    nk = n_past_blocks + nq

    def new_idx(i, qi, ki):
        return jnp.clip(ki - n_past_blocks, 0, qi)

    in_specs = [pl.BlockSpec((1, tq, WIDTH), lambda i, qi, ki: (i, qi, 0)),
                pl.BlockSpec((1, tq, N_HEADS), lambda i, qi, ki: (i, qi, 0)),
                pl.BlockSpec((1, tq, WIDTH), lambda i, qi, ki: (i, qi, 0))]
    args = [q, fq, sg]
    if n_past_blocks:
        def past_idx(i, qi, ki):
            return jnp.minimum(ki, n_past_blocks - 1)
        in_specs += [pl.BlockSpec((1, tk_past, WIDTH), lambda i, qi, ki: (i, past_idx(i, qi, ki), 0)),
                     pl.BlockSpec((1, tk_past, WIDTH), lambda i, qi, ki: (i, past_idx(i, qi, ki), 0)),
                     pl.BlockSpec((1, N_HEADS, tk_past), lambda i, qi, ki: (i, 0, past_idx(i, qi, ki)))]
        args += list(past)
    in_specs += [pl.BlockSpec((1, tq, WIDTH), lambda i, qi, ki: (i, new_idx(i, qi, ki), 0)),
                 pl.BlockSpec((1, tq, WIDTH), lambda i, qi, ki: (i, new_idx(i, qi, ki), 0)),
                 pl.BlockSpec((1, N_HEADS, tq), lambda i, qi, ki: (i, 0, new_idx(i, qi, ki)))]
    args += [k_new, v_new, fk_new_t]
    return pl.pallas_call(
        functools.partial(_fox_kernel, n_past_blocks=n_past_blocks, tq=tq),
        grid=(b, nq, nk),
        in_specs=in_specs,
        out_specs=pl.BlockSpec((1, tq, WIDTH), lambda i, qi, ki: (i, qi, 0)),
        out_shape=jax.ShapeDtypeStruct((b, t, WIDTH), F32),
        scratch_shapes=[pltpu.VMEM((N_HEADS, tq, 1), F32), pltpu.VMEM((N_HEADS, tq, 1), F32),
                        pltpu.VMEM((tq, WIDTH), F32)],
        compiler_params=_cparams(("parallel", "parallel", "arbitrary")),
        name="fox_attention",
    )(*args)


def _rwkv_kernel(p_ref, sh0_ref, s0_ref, mu_ref, w0_ref, wb_ref, a0_ref, ab_ref, gb_ref, kk_ref, ka_ref, rk_ref,
                 lnw_ref, lnb_ref, gm_ref, y_ref, st_ref, sht_ref, z_scr, prev_scr, *, c):
    t = pl.program_id(1)
    nt = pl.num_programs(1)
    n_rows = p_ref.shape[1]
    n_chunks = n_rows // c

    def head_block(h):
        lo = (h % HEADS_PER_GROUP) * HEAD_DIM
        return h // HEADS_PER_GROUP, slice(lo, lo + HEAD_DIM)

    @pl.when(t == 0)
    def _():
        z_scr[...] = jnp.zeros(z_scr.shape, F32)
        for h in range(N_HEADS):
            i, blk = head_block(h)
            z_scr[i, blk, blk] = s0_ref[0, h]
        prev_scr[...] = sh0_ref[0]

    p = p_ref[0]
    row = lax.broadcasted_iota(I32, p.shape, 0)
    prev = jnp.where(row == 0, prev_scr[...], pltpu.roll(p, 1, 0))
    last = p[n_rows - 1:n_rows, :]
    prev_scr[...] = last
    sht_ref[0] = last

    pm = p + (prev - p) * mu_ref[...]
    r = pm[:, 0:WIDTH]
    k = pm[:, WIDTH:2 * WIDTH]
    v = pm[:, 2 * WIDTH:3 * WIDTH]
    o1 = 3 * WIDTH
    wd = pm[:, o1:o1 + DECAY_LORA]
    ad = pm[:, o1 + DECAY_LORA:o1 + DECAY_LORA + ICLR_LORA]
    gd = pm[:, o1 + DECAY_LORA + ICLR_LORA:RWKV_COLS]

    w = -_softplus(-(w0_ref[...] + _dot(jnp.tanh(wd), wb_ref[...]))) - 0.5
    lw = -jnp.exp(w)
    a = _sigmoid(a0_ref[...] + _dot(ad, ab_ref[...]))
    g = _dot(_sigmoid(gd), gb_ref[...])
    kk = k * kk_ref[...]
    kk = kk / jnp.maximum(jnp.sqrt(_gsum(kk * kk, gm_ref)), L2_EPS)
    kf = k * (1.0 + (a - 1.0) * ka_ref[...])

    ri = lax.broadcasted_iota(I32, (n_rows, n_rows), 0)
    ci = lax.broadcasted_iota(I32, (n_rows, n_rows), 1)
    same_chunk = (ri // c) == (ci // c)
    cum = _dot_exact_rhs(jnp.logical_and(same_chunk, ri >= ci).astype(F32), lw)
    cum_last = _dot_exact_rhs(same_chunk.astype(F32), lw)
    r_t = r * jnp.exp(cum)
    a_t = -kk * jnp.exp(cum - lw)
    inv = jnp.exp(-cum)
    b_t = kk * a * inv
    k_t = kf * inv
    to_end = jnp.exp(cum_last - cum)
    b_e = kk * a * to_end
    k_e = kf * to_end
    g_end = jnp.exp(cum_last)

    hg = HEADS_PER_GROUP
    gw = hg * HEAD_DIM
    log_c = int(math.log2(c))
    t_idx = lax.broadcasted_iota(I32, (c, hg * c), 0)
    s_idx = lax.broadcasted_iota(I32, (c, hg * c), 1) & (c - 1)
    strict = s_idx < t_idx
    lower = s_idx <= t_idx
    eye = (s_idx == t_idx).astype(F32)
    rb = lax.broadcasted_iota(I32, (hg * c, gw), 0) >> log_c
    mask_kv = rb == (lax.broadcasted_iota(I32, (hg * c, gw), 1) >> int(math.log2(HEAD_DIM)))
    rs = lax.broadcasted_iota(I32, (hg * c, hg * c), 0) >> log_c
    mask_ss = rs == (lax.broadcasted_iota(I32, (hg * c, hg * c), 1) >> log_c)
    ng = N_HEADS // hg
    cat = functools.partial(jnp.concatenate, axis=0)
    units = [(slice(j * c, (j + 1) * c), slice(i * gw, (i + 1) * gw)) for j in range(n_chunks) for i in range(ng)]
    nu = len(units)

    ar = [_split2(cat([a_t[rs_, s], r_t[rs_, s]])) for rs_, s in units]
    ab = [_mm3(ar[n], _bd_parts(b_t[units[n]], mask_kv), NT) for n in range(nu)]
    ak = [_mm3(ar[n], _bd_parts(k_t[units[n]], mask_kv), NT) for n in range(nu)]
    l_ab = [jnp.where(strict, m[:c], 0.0) for m in ab]
    l_rb = [jnp.where(lower, m[c:], 0.0) for m in ab]
    l_ak = [jnp.where(strict, m[:c], 0.0) for m in ak]
    l_rk = [jnp.where(lower, m[c:], 0.0) for m in ak]
    def mm1(a, b_bd):
        return jnp.dot(a.astype(BF16), b_bd, preferred_element_type=F32)

    def bd1(x, mask):
        tiled = jnp.concatenate([x.astype(BF16)] * hg, axis=0)
        return jnp.where(mask, tiled, jnp.zeros_like(tiled))

    tinv = [eye + m for m in l_ab]
    pw = [mm1(m, bd1(m, mask_ss)) for m in l_ab]
    for _ in range(1, log_c - 1):
        res = [mm1(cat([tinv[n], pw[n]]), bd1(pw[n], mask_ss)) for n in range(nu)]
        tinv = [tinv[n] + res[n][:c] for n in range(nu)]
        pw = [m[c:] for m in res]
    tinv = [tinv[n] + mm1(tinv[n], bd1(pw[n], mask_ss)) for n in range(nu)]
    av = [_mm3(_split2(cat([l_ak[n], l_rk[n]])), _bd_parts(v[units[n]], mask_kv), NN) for n in range(nu)]
    ue = [_split2(cat([b_e[units[n]], k_e[units[n]]])) for n in range(nu)]

    def wide(fn, x):
        return [fn(x[:, :gw]), fn(x[:, gw:])]

    def bd1w(x):
        return jnp.concatenate(wide(lambda h_: bd1(h_, mask_kv), x), axis=1)

    def bd3w(x):
        parts = wide(lambda h_: _bd_parts(h_, mask_kv), x)
        return tuple(jnp.concatenate([parts[0][q], parts[1][q]], axis=1) for q in range(2))

    rhs = [jnp.concatenate([a_t[units[n]], av[n][:c]], axis=1) for n in range(nu)]
    x0 = [mm1(tinv[n], bd1w(rhs[n])) for n in range(nu)]
    resid = [rhs[n] - (x0[n] - _mm3(_split2(l_ab[n]), bd3w(x0[n]), NN)) for n in range(nu)]
    sol = [x0[n] + mm1(tinv[n], bd1w(resid[n])) for n in range(nu)]
    lift = [_mm3(_split2(l_rb[n]), bd3w(sol[n]), NN) for n in range(nu)]
    lhs_s = [_split2(cat([sol[n][:, :gw], r_t[units[n]] + lift[n][:, :gw]])) for n in range(nu)]
    u_loc = [sol[n][:, gw:] for n in range(nu)]
    o_loc = [av[n][c:] + lift[n][:, gw:] for n in range(nu)]

    zr = lax.broadcasted_iota(I32, (gw, gw), 0) >> int(math.log2(HEAD_DIM))
    zmask = zr == (lax.broadcasted_iota(I32, (gw, gw), 1) >> int(math.log2(HEAD_DIM)))
    z = [z_scr[i] for i in range(ng)]
    o_rows = []
    for j in range(n_chunks):
        o_grp = []
        for i in range(ng):
            n = j * ng + i
            rs_, s = units[n]
            sz = _mm3(lhs_s[n], _split2(z[i]), NT)
            u = sz[:c] + u_loc[n]
            o_grp.append(sz[c:] + o_loc[n])
            upd = _mm3(_split2(cat([u, v[rs_, s]])), ue[n], TN)
            z[i] = z[i] * g_end[j * c:j * c + 1, s] + jnp.where(zmask, upd, 0.0)
        o_rows.append(jnp.concatenate(o_grp, axis=1))
    for i in range(ng):
        z_scr[i] = z[i]

    o = cat(o_rows)
    inv_hd = 1.0 / HEAD_DIM
    dlt = o - _gsum(o, gm_ref) * inv_hd
    var = _gsum(dlt * dlt, gm_ref) * inv_hd
    on = dlt * lax.rsqrt(var + RWKV_GN_EPS) * lnw_ref[...] + lnb_ref[...]
    bonus = _gsum(r * kf * rk_ref[...], gm_ref) * v
    y_ref[0] = (on + bonus) * g

    @pl.when(t == nt - 1)
    def _():
        for h in range(N_HEADS):
            i, blk = head_block(h)
            st_ref[0, h] = z_scr[i, blk, blk]


def _rwkv(p, shift0, s0, prm, gmat, chunk, chunks_per_step):
    b, t, _ = p.shape
    row = lambda n: _const_spec((1, n))
    rows = chunk * chunks_per_step
    return pl.pallas_call(
        functools.partial(_rwkv_kernel, c=chunk),
        grid=(b, t // rows),
        in_specs=[pl.BlockSpec((1, rows, RWKV_COLS), lambda i, j: (i, j, 0)),
                  pl.BlockSpec((1, 1, RWKV_COLS), lambda i, j: (i, 0, 0)),
                  pl.BlockSpec((1, N_HEADS, HEAD_DIM, HEAD_DIM), lambda i, j: (i, 0, 0, 0)),
                  row(RWKV_COLS), row(WIDTH), _const_spec((DECAY_LORA, WIDTH)), row(WIDTH),
                  _const_spec((ICLR_LORA, WIDTH)), _const_spec((GATE_LORA, WIDTH)),
                  row(WIDTH), row(WIDTH), row(WIDTH), row(WIDTH), row(WIDTH), _const_spec((WIDTH, WIDTH))],
        out_specs=[pl.BlockSpec((1, rows, WIDTH), lambda i, j: (i, j, 0)),
                   pl.BlockSpec((1, N_HEADS, HEAD_DIM, HEAD_DIM), lambda i, j: (i, 0, 0, 0)),
                   pl.BlockSpec((1, 1, RWKV_COLS), lambda i, j: (i, 0, 0))],
        out_shape=[jax.ShapeDtypeStruct((b, t, WIDTH), F32),
                   jax.ShapeDtypeStruct((b, N_HEADS, HEAD_DIM, HEAD_DIM), F32),
                   jax.ShapeDtypeStruct((b, 1, RWKV_COLS), F32)],
        scratch_shapes=[pltpu.VMEM((N_HEADS // HEADS_PER_GROUP, HEADS_PER_GROUP * HEAD_DIM,
                                    HEADS_PER_GROUP * HEAD_DIM), F32),
                        pltpu.VMEM((1, RWKV_COLS), F32)],
        compiler_params=_cparams(("parallel", "arbitrary")),
        name="rwkv7_mix",
    )(p, shift0, s0, prm["mu"], prm["w0"], prm["wb"], prm["a0"], prm["ab"], prm["gb"], prm["kk"], prm["ka"],
      prm["rk"], prm["lnw"], prm["lnb"], gmat)


def _merge_kernel(x_ref, ya_ref, yb_ref, gate_ref, mod_ref, g2_ref, woa_ref, wob_ref, wo_ref, wrh_ref, wrl_ref,
                  *rest):
    x1_ref, h2_ref, lg_ref = rest[-3:]
    bb, tt, d = x_ref.shape
    m = bb * tt
    gate = gate_ref[...].reshape(m, GATE_COLS)
    merged = (gate[:, 0:d] * _dot(ya_ref[...].reshape(m, WIDTH), woa_ref[...])
              + gate[:, d:2 * d] * _dot(yb_ref[...].reshape(m, WIDTH), wob_ref[...]))
    x1 = x_ref[...] + mod_ref[:, 2:3, :] * _dot(merged, wo_ref[...]).reshape(bb, tt, d)
    x1_ref[...] = x1
    ms = jnp.mean(x1 * x1, axis=-1, keepdims=True)
    h2 = x1 * lax.rsqrt(ms + RMS_EPS) * g2_ref[...]
    h2 = (h2 * (1.0 + mod_ref[:, 4:5, :]) + mod_ref[:, 3:4, :]).reshape(m, d)
    h2_ref[...] = _pack_bf16_pairs(h2)
    lg_ref[...] = _mm3((wrh_ref[...], wrl_ref[...]), _split2(h2), NT)


def _merge(x, ya, yb, gate, mod, w, n_total, row_offset, shared=None):
    b, t, d = x.shape
    bb, tt = _token_blocks(b, t)
    nt = t // tt
    m = bb * tt
    off = row_offset // m

    def tok(cols):
        return pl.BlockSpec((bb, tt, cols), lambda i, j: (i, j, 0))

    in_specs = [tok(d), tok(WIDTH), tok(WIDTH), tok(GATE_COLS),
                pl.BlockSpec((bb, 6, d), lambda i, j: (i, 0, 0)),
                _const_spec((1, d)), _const_spec((WIDTH, d)), _const_spec((WIDTH, d)), _const_spec((d, d)),
                _const_spec((N_EXPERTS, d)), _const_spec((N_EXPERTS, d))]
    args = [x, ya, yb, gate, mod.reshape(b, 6, d), w["g2"], w["w_oa"], w["w_ob"], w["w_o"], w["wr_hi"], w["wr_lo"]]
    aliases = {}
    if shared is not None:
        aliases = {len(args): 1, len(args) + 1: 2}
        in_specs += [pl.BlockSpec(memory_space=pl.ANY), pl.BlockSpec(memory_space=pl.ANY)]
        args += list(shared)
    return pl.pallas_call(
        _merge_kernel,
        grid=(b // bb, nt),
        in_specs=in_specs,
        out_specs=[tok(d), pl.BlockSpec((m, d // 2), lambda i, j: (off + i * nt + j, 0)),
                   pl.BlockSpec((N_EXPERTS, m), lambda i, j: (0, off + i * nt + j))],
        out_shape=[jax.ShapeDtypeStruct((b, t, d), F32), jax.ShapeDtypeStruct((n_total, d // 2), I32),
                   jax.ShapeDtypeStruct((N_EXPERTS, n_total), F32)],
        input_output_aliases=aliases,
        compiler_params=_cparams(("parallel", "parallel")),
        name="merge_norm2_router",
    )(*args)


def _route_kernel(lg_ref, bias_ref, idx_ref, wt_ref, rank_ref, cnt_ref, carry):
    @pl.when(pl.program_id(0) == 0)
    def _():
        carry[...] = jnp.zeros(carry.shape, F32)

    tm = lg_ref.shape[1]
    scores = _sigmoid(lg_ref[...])
    sel = scores + bias_ref[...]
    row = lax.broadcasted_iota(I32, (N_EXPERTS, tm), 0)
    neg_inf = -jnp.inf

    def first_argmax(vals, rows):
        mx = jnp.max(vals, axis=0, keepdims=True)
        return mx, jnp.min(jnp.where(vals == mx, rows, N_EXPERTS), axis=0, keepdims=True)

    gslices = [slice(g * EXPERTS_PER_GROUP, (g + 1) * EXPERTS_PER_GROUP) for g in range(N_GROUPS)]
    gs = []
    row_g = lax.broadcasted_iota(I32, (EXPERTS_PER_GROUP, tm), 0)
    for sl in gslices:
        m1, i1 = first_argmax(sel[sl], row_g)
        m2 = jnp.max(jnp.where(row_g == i1, neg_inf, sel[sl]), axis=0, keepdims=True)
        gs.append(m1 + m2)
    kept = []
    for g in range(N_GROUPS):
        beaten = jnp.zeros((1, tm), I32)
        for o in range(N_GROUPS):
            if o != g:
                wins = (gs[o] >= gs[g]) if o < g else (gs[o] > gs[g])
                beaten = beaten + wins.astype(I32)
        kept.append(jnp.where(beaten < TOPK_GROUPS, sel[gslices[g]], neg_inf))
    cur = jnp.concatenate(kept, axis=0)

    idxs, ws = [], []
    picked = jnp.zeros((N_EXPERTS, tm), F32)
    for _ in range(TOP_K):
        _, ik = first_argmax(cur, row)
        hit = row == ik
        idxs.append(ik)
        ws.append(jnp.sum(jnp.where(hit, scores, 0.0), axis=0, keepdims=True))
        cur = jnp.where(hit, neg_inf, cur)
        picked = jnp.where(hit, 1.0, picked)
    wsum = ws[0]
    for k in range(1, TOP_K):
        wsum = wsum + ws[k]

    r = lax.broadcasted_iota(I32, (tm, tm), 0)
    c = lax.broadcasted_iota(I32, (tm, tm), 1)
    before = jnp.dot(picked.astype(BF16), (r < c).astype(BF16), preferred_element_type=F32) + carry[...]
    carry[...] = carry[...] + jnp.sum(picked, axis=1, keepdims=True)
    cnt_ref[...] = carry[...]

    kk = lax.broadcasted_iota(I32, (TOP_K, tm), 0)
    idx_o = jnp.zeros((TOP_K, tm), I32)
    wt_o = jnp.zeros((TOP_K, tm), F32)
    rank_o = jnp.zeros((TOP_K, tm), F32)
    for k in range(TOP_K):
        rk = jnp.sum(jnp.where(row == idxs[k], before, 0.0), axis=0, keepdims=True)
        idx_o = jnp.where(kk == k, idxs[k], idx_o)
        wt_o = jnp.where(kk == k, ws[k] / wsum * ROUTED_SCALE, wt_o)
        rank_o = jnp.where(kk == k, rk, rank_o)
    idx_ref[...] = idx_o
    wt_ref[...] = wt_o
    rank_ref[...] = rank_o.astype(I32)


def _route(logits_t, bias_col, tm):
    n = logits_t.shape[1]
    tokk = pl.BlockSpec((TOP_K, tm), lambda i: (0, i))
    return pl.pallas_call(
        _route_kernel,
        grid=(n // tm,),
        in_specs=[pl.BlockSpec((N_EXPERTS, tm), lambda i: (0, i)), _const_spec((N_EXPERTS, 1))],
        out_specs=[tokk, tokk, tokk, _const_spec((N_EXPERTS, 1))],
        out_shape=[jax.ShapeDtypeStruct((TOP_K, n), I32), jax.ShapeDtypeStruct((TOP_K, n), F32),
                   jax.ShapeDtypeStruct((TOP_K, n), I32), jax.ShapeDtypeStruct((N_EXPERTS, 1), F32)],
        scratch_shapes=[pltpu.VMEM((N_EXPERTS, 1), F32)],
        compiler_params=_cparams(("arbitrary",)),
        name="route_topk",
    )(logits_t, bias_col)


def _plan_kernel(cnt_ref, start_ref, be_ref, valid_ref, nu_ref, *, blk):
    cnt = cnt_ref[...]
    padded = jnp.ceil(cnt * (1.0 / blk)) * blk
    e_r = lax.broadcasted_iota(I32, (N_EXPERTS, N_EXPERTS), 0)
    e_c = lax.broadcasted_iota(I32, (N_EXPERTS, N_EXPERTS), 1)
    incl = (e_r <= e_c).astype(BF16)
    ph, pm, plo = _split3(jnp.broadcast_to(padded, (8, N_EXPERTS)))
    d2 = functools.partial(jnp.dot, preferred_element_type=F32)
    pad_end = (d2(ph, incl) + (d2(pm, incl) + d2(plo, incl)))[0:1, :]
    pad_start = pad_end - padded
    start_ref[...] = pad_start.astype(I32)
    total = jnp.max(pad_end, axis=-1, keepdims=True)
    nu_ref[...] = jnp.broadcast_to(total * (1.0 / blk), (1, N_EXPERTS)).astype(I32)
    nb = be_ref.shape[0]
    first = (lax.broadcasted_iota(I32, (nb, N_EXPERTS), 0) * blk).astype(F32)
    lane = lax.broadcasted_iota(I32, (nb, N_EXPERTS), 1)
    inside = jnp.logical_and(pad_start <= first, first < pad_end)
    be_ref[...] = jnp.sum(jnp.where(inside, lane, 0), axis=-1, keepdims=True)
    rows = jnp.minimum(pad_start + cnt - first, float(blk))
    valid_ref[...] = jnp.sum(jnp.where(inside, rows, 0.0), axis=-1, keepdims=True).astype(I32)


def _plan(counts, n_blocks, blk):
    return pl.pallas_call(
        functools.partial(_plan_kernel, blk=blk),
        out_shape=[jax.ShapeDtypeStruct((1, N_EXPERTS), I32), jax.ShapeDtypeStruct((n_blocks, 1), I32),
                   jax.ShapeDtypeStruct((n_blocks, 1), I32), jax.ShapeDtypeStruct((1, N_EXPERTS), I32)],
        compiler_params=pltpu.CompilerParams(vmem_limit_bytes=VMEM_LIMIT),
        name="dispatch_plan",
    )(counts)


def _dest_kernel(idx_ref, rank_ref, start_ref, dest_ref):
    tm = idx_ref.shape[1]
    row = lax.broadcasted_iota(I32, (N_EXPERTS, tm), 0)
    kk = lax.broadcasted_iota(I32, (TOP_K, tm), 0)
    idx = idx_ref[...]
    base = jnp.zeros((TOP_K, tm), I32)
    for k in range(TOP_K):
        bk = jnp.sum(jnp.where(row == idx[k:k + 1, :], start_ref[...], 0), axis=0, keepdims=True)
        base = jnp.where(kk == k, bk, base)
    dest_ref[...] = base + rank_ref[...]


def _dest(idx, rank, pad_start_col, tm):
    n = idx.shape[1]
    tokk = pl.BlockSpec((TOP_K, tm), lambda i: (0, i))
    return pl.pallas_call(
        _dest_kernel,
        grid=(n // tm,),
        in_specs=[tokk, tokk, _const_spec((N_EXPERTS, 1))],
        out_specs=tokk,
        out_shape=jax.ShapeDtypeStruct((TOP_K, n), I32),
        compiler_params=_cparams(("parallel",)),
        name="dispatch_dest",
    )(idx, rank, pad_start_col)


def _dispatch_kernel(dest_ref, h2_ref, xs_ref, sem):
    tm = h2_ref.shape[0]

    def issue(r, carry):
        for k in range(TOP_K):
            pltpu.make_async_copy(h2_ref.at[pl.ds(r, 1)], xs_ref.at[pl.ds(dest_ref[k, r], 1)], sem).start()
        return carry

    lax.fori_loop(0, tm, issue, 0)
    for k in range(TOP_K):
        pltpu.make_async_copy(h2_ref, xs_ref.at[pl.ds(0, tm)], sem).wait()


def _dispatch(h2, dest_t, n_slots, tm):
    n, d = h2.shape
    return pl.pallas_call(
        _dispatch_kernel,
        grid=(n // tm,),
        in_specs=[pl.BlockSpec((TOP_K, tm), lambda i: (0, i), memory_space=pltpu.SMEM),
                  pl.BlockSpec((tm, d), lambda i: (i, 0))],
        out_specs=pl.BlockSpec(memory_space=pl.ANY),
        out_shape=jax.ShapeDtypeStruct((n_slots, d), h2.dtype),
        scratch_shapes=[pltpu.SemaphoreType.DMA(())],
        compiler_params=_cparams(("arbitrary",)),
        name="moe_dispatch",
    )(dest_t, h2)


def _expert_kernel(be_ref, valid_ref, nu_ref, x_ref, wg_ref, wu_ref, wd_ref, y_ref, wg_b, wu_b, wd_b):
    i = pl.program_id(0)
    nv = valid_ref[i]
    new_expert = jnp.logical_or(i == 0, be_ref[i] != be_ref[jnp.maximum(i - 1, 0)])

    @pl.when(jnp.logical_and(nv > 0, new_expert))
    def _():
        wg_b[...] = wg_ref[0].astype(BF16)
        wu_b[...] = wu_ref[0].astype(BF16)
        wd_b[...] = wd_ref[0].astype(BF16)

    @pl.when(nv > 0)
    def _():
        blk = x_ref.shape[0]
        rows = lax.broadcasted_iota(I32, (blk, 1), 0)
        x = _unpack_bf16_pairs(jnp.where(rows < nv, x_ref[...], 0))
        hg = jnp.dot(x, wg_b[...], preferred_element_type=F32)
        hu = jnp.dot(x, wu_b[...], preferred_element_type=F32)
        y_ref[...] = jnp.dot((_silu(hg) * hu).astype(BF16), wd_b[...], preferred_element_type=F32)


def _experts(xs, block_e, valid, n_used, w_eg, w_eu, w_ed, blk):
    n_slots, packed = xs.shape
    d = w_eg.shape[1]
    n_blocks = n_slots // blk

    def row_blk(i, be, valid, nu):
        return (jnp.minimum(i, nu[0] - 1), 0)

    def w_blk(i, be, valid, nu):
        return (be[i], 0, 0)

    return pl.pallas_call(
        _expert_kernel,
        grid_spec=pltpu.PrefetchScalarGridSpec(
            num_scalar_prefetch=3,
            grid=(n_blocks,),
            in_specs=[pl.BlockSpec((blk, packed), row_blk),
                      pl.BlockSpec((1, d, D_EXPERT), w_blk), pl.BlockSpec((1, d, D_EXPERT), w_blk),
                      pl.BlockSpec((1, D_EXPERT, d), w_blk)],
            out_specs=pl.BlockSpec((blk, d), row_blk),
            scratch_shapes=[pltpu.VMEM((d, D_EXPERT), BF16), pltpu.VMEM((d, D_EXPERT), BF16),
                            pltpu.VMEM((D_EXPERT, d), BF16)]),
        out_shape=jax.ShapeDtypeStruct((n_slots, d), F32),
        compiler_params=_cparams(("arbitrary",)),
        name="moe_experts",
    )(block_e, valid, n_used, xs, w_eg, w_eu, w_ed)


def _final_kernel(dest_ref, dest_next_ref, x1_ref, h2_ref, wt_ref, mod_ref, wsg_ref, wsu_ref, wsd_ref, ys_ref,
                  o_ref, ybuf, sems):
    bb, tt, d = x1_ref.shape
    m = bb * tt
    step = pl.program_id(0) * pl.num_programs(1) + pl.program_id(1)
    n_steps = pl.num_programs(0) * pl.num_programs(1)
    slot = step % 2

    def gather(dests, buf):
        def issue(r, carry):
            for k in range(TOP_K):
                pltpu.make_async_copy(ys_ref.at[pl.ds(dests[k, r], 1)], ybuf.at[buf, k, pl.ds(r, 1)],
                                      sems.at[buf]).start()
            return carry

        lax.fori_loop(0, m, issue, 0)

    @pl.when(step == 0)
    def _():
        gather(dest_ref, 0)

    @pl.when(step + 1 < n_steps)
    def _():
        gather(dest_next_ref, 1 - slot)

    hb = _unpack_bf16_pairs(h2_ref[...])
    hg = jnp.dot(hb, wsg_ref[...], preferred_element_type=F32)
    hu = jnp.dot(hb, wsu_ref[...], preferred_element_type=F32)
    ffn = _dot(_silu(hg) * hu, wsd_ref[...])

    for k in range(TOP_K):
        pltpu.make_async_copy(ys_ref.at[pl.ds(0, m)], ybuf.at[slot, k], sems.at[slot]).wait()

    wt = wt_ref[...]
    for k in range(TOP_K):
        ffn = ffn + wt[:, k:k + 1] * ybuf[slot, k]
    o_ref[...] = x1_ref[...] + mod_ref[:, 5:6, :] * ffn.reshape(bb, tt, d)


def _final(x1, h2_all, wts_all, dest_t, ys, mod, w, row_offset):
    b, t, d = x1.shape
    bb, tt = _token_blocks(b, t)
    nt = t // tt
    m = bb * tt
    off = row_offset // m

    def flat_idx(i, j):
        return off + i * nt + j

    last = off + (b // bb) * nt - 1

    return pl.pallas_call(
        _final_kernel,
        grid=(b // bb, nt),
        in_specs=[pl.BlockSpec((TOP_K, m), lambda i, j: (0, flat_idx(i, j)), memory_space=pltpu.SMEM),
                  pl.BlockSpec((TOP_K, m), lambda i, j: (0, jnp.minimum(flat_idx(i, j) + 1, last)),
                               memory_space=pltpu.SMEM),
                  pl.BlockSpec((bb, tt, d), lambda i, j: (i, j, 0)),
                  pl.BlockSpec((m, h2_all.shape[1]), lambda i, j: (flat_idx(i, j), 0)),
                  pl.BlockSpec((m, TOP_K), lambda i, j: (flat_idx(i, j), 0)),
                  pl.BlockSpec((bb, 6, d), lambda i, j: (i, 0, 0)),
                  _const_spec((d, D_EXPERT)), _const_spec((d, D_EXPERT)), _const_spec((D_EXPERT, d)),
                  pl.BlockSpec(memory_space=pl.ANY)],
        out_specs=pl.BlockSpec((bb, tt, d), lambda i, j: (i, j, 0)),
        out_shape=jax.ShapeDtypeStruct((b, t, d), F32),
        scratch_shapes=[pltpu.VMEM((2, TOP_K, m, d), F32), pltpu.SemaphoreType.DMA((2,))],
        compiler_params=_cparams(("arbitrary", "arbitrary")),
        name="moe_combine_final",
    )(dest_t, dest_t, x1, h2_all, wts_all, mod.reshape(b, 6, d), w["w_sg"], w["w_su"], w["w_sd"], ys)


def _moe_routed(h2_all, logits_all, w, blk=256, tm=256):
    n = h2_all.shape[0]
    n_blocks = (n * TOP_K + N_EXPERTS * (blk - 1)) // blk + 1
    n_blocks = (n_blocks + 7) // 8 * 8
    idx, wts_t, rank, counts = _route(logits_all, w["router_bias"], tm)
    pad_start, block_e, valid, n_used = _plan(counts.reshape(1, N_EXPERTS), n_blocks, blk)
    block_e = block_e.reshape(n_blocks)
    valid = valid.reshape(n_blocks)
    n_used = n_used[0, 0:1]
    dest_t = _dest(idx, rank, pad_start.reshape(N_EXPERTS, 1), tm)
    xs = _dispatch(h2_all, dest_t, n_blocks * blk, tm)
    ys = _experts(xs, block_e, valid, n_used, w["w_eg"], w["w_eu"], w["w_ed"], blk)
    return ys, dest_t, jnp.transpose(wts_t)


def _prep(raw):
    p = {k: v[0] for k, v in raw.items()}
    w_in = p["w_in"]
    o_fox = RWKV_COLS
    o_fl = o_fox + FOX_MAIN_COLS
    o_gate = o_fl + N_HEADS
    row = lambda a: a.reshape(1, -1)
    return dict(
        w_ada=p["w_ada"], b_ada=p["b_ada"],
        g1=row(p["norm1_g"]), g2=row(p["norm2_g"]),
        wr=w_in[:, :o_fox].astype(BF16),
        wf=w_in[:, o_fox:o_fl].astype(BF16),
        wfl=jnp.pad(w_in[:, o_fl:o_gate], ((0, 0), (0, LANES - N_HEADS))).astype(BF16),
        wg=w_in[:, o_gate:].astype(BF16),
        qn=row(jnp.tile(p["fox_q_norm"], N_HEADS)), kn=row(jnp.tile(p["fox_k_norm"], N_HEADS)),
        fb=row(p["fox_f_bias"]),
        gmat=_group_ones(),
        rwkv=dict(mu=row(p["rwkv_mu"]), w0=row(p["rwkv_w0"]), wb=p["rwkv_w_lora_b"], a0=row(p["rwkv_a0"]),
                  ab=p["rwkv_a_lora_b"], gb=p["rwkv_g_lora_b"], kk=row(p["rwkv_k_k"]), ka=row(p["rwkv_k_a"]),
                  rk=row(p["rwkv_r_k"]), lnw=row(p["rwkv_ln_w"]), lnb=row(p["rwkv_ln_b"])),
        w_oa=p["w_out_rwkv"].astype(BF16), w_ob=p["w_out_fox"].astype(BF16), w_o=p["w_out"].astype(BF16),
        wr_hi=p["w_router"].T.astype(BF16),
        wr_lo=(p["w_router"] - p["w_router"].astype(BF16).astype(F32)).T.astype(BF16),
        router_bias=p["router_bias"].reshape(N_EXPERTS, 1),
        w_eg=p["w_exp_gate"], w_eu=p["w_exp_up"], w_ed=p["w_exp_down"],
        w_sg=p["w_sh_gate"].astype(BF16), w_su=p["w_sh_up"].astype(BF16), w_sd=p["w_sh_down"].astype(BF16),
    )


def _token_blocks(b, t):
    if t >= 256:
        return 1, 256
    bb = max(1, min(b, 256 // t))
    while b % bb:
        bb -= 1
    return bb, t


def _mix_path(x, mod, shift0, wkv0, past_k, past_v, past_logf, w):
    b, t, d = x.shape
    bb, tt = _token_blocks(b, t)
    n_past = past_k.shape[1]
    if n_past:
        f_past = _past_cumsum(past_logf)
        init = f_past[:, n_past - 1:n_past, :]
        past = (past_k, past_v, jnp.swapaxes(f_past, 1, 2))
    else:
        init = jnp.zeros((b, 1, N_HEADS), F32)
        past = None
    pr, q, k, v, sg, logf, f_new, gate = _inproj(x, mod.reshape(b, 6, d), w["g1"], w["wr"], w["wf"], w["wfl"],
                                                 w["wg"], w["qn"], w["kn"], w["fb"], w["gmat"], init, bb, tt)
    y_fox = _fox_attention(q, f_new, sg, k, v, jnp.swapaxes(f_new, 1, 2), past=past, tq=min(t, 512),
                           tk_past=min(max(n_past, 1), 512))
    chunk = min(t, RWKV_CHUNK)
    y_rwkv, wkv_new, shift_new = _rwkv(pr, shift0.reshape(b, 1, RWKV_COLS), wkv0, w["rwkv"], w["gmat"],
                                       chunk, max(1, min(RWKV_CHUNKS_PER_STEP, t // chunk)))
    return y_rwkv, y_fox, gate, wkv_new, shift_new, k, v, logf


def _layer(paths, w):
    n_b = [p[0].shape[0] for p in paths]
    mod_all = _ada(jnp.concatenate([p[1] for p in paths], axis=0), w["w_ada"], w["b_ada"])
    mods, o = [], 0
    for nb in n_b:
        mods.append(mod_all[o:o + nb])
        o += nb
    n_total = sum(p[0].shape[0] * p[0].shape[1] for p in paths)
    mixed, x1s = [], []
    shared, row = None, 0
    for (x, _, shift0, wkv0, pk, pv, plf), mod in zip(paths, mods):
        ya, yb, gate, wkv_new, shift_new, k, v, logf = _mix_path(x, mod, shift0, wkv0, pk, pv, plf, w)
        x1, h2_all, lg_all = _merge(x, ya, yb, gate, mod, w, n_total, row, shared)
        shared = (h2_all, lg_all)
        row += x.shape[0] * x.shape[1]
        mixed.append((wkv_new, shift_new, k, v, logf))
        x1s.append(x1)
    ys, dest_t, wts = _moe_routed(h2_all, lg_all, w)
    outs, row = [], 0
    for x1, mod, st in zip(x1s, mods, mixed):
        y = _final(x1, h2_all, wts, dest_t, ys, mod, w, row)
        row += x1.shape[0] * x1.shape[1]
        outs.append((y,) + st)
    return outs


def kernel(x_prompt, x_sample, c_prompt, c_sample, state_rwkv_wkv, state_rwkv_shift, cache_fox_k, cache_fox_v,
           cache_fox_logf, w_ada, b_ada, norm1_g, norm2_g, w_in, rwkv_mu, rwkv_w0, rwkv_w_lora_b, rwkv_a0,
           rwkv_a_lora_b, rwkv_g_lora_b, rwkv_k_k, rwkv_k_a, rwkv_r_k, rwkv_ln_w, rwkv_ln_b, fox_q_norm,
           fox_k_norm, fox_f_bias, w_out_rwkv, w_out_fox, w_out, w_router, router_bias, w_exp_gate, w_exp_up,
           w_exp_down, w_sh_gate, w_sh_up, w_sh_down):
    raw = dict(w_ada=w_ada, b_ada=b_ada, norm1_g=norm1_g, norm2_g=norm2_g, w_in=w_in, rwkv_mu=rwkv_mu,
               rwkv_w0=rwkv_w0, rwkv_w_lora_b=rwkv_w_lora_b, rwkv_a0=rwkv_a0, rwkv_a_lora_b=rwkv_a_lora_b,
               rwkv_g_lora_b=rwkv_g_lora_b, rwkv_k_k=rwkv_k_k, rwkv_k_a=rwkv_k_a, rwkv_r_k=rwkv_r_k,
               rwkv_ln_w=rwkv_ln_w, rwkv_ln_b=rwkv_ln_b, fox_q_norm=fox_q_norm, fox_k_norm=fox_k_norm,
               fox_f_bias=fox_f_bias, w_out_rwkv=w_out_rwkv, w_out_fox=w_out_fox, w_out=w_out,
               w_router=w_router, router_bias=router_bias, w_exp_gate=w_exp_gate, w_exp_up=w_exp_up,
               w_exp_down=w_exp_down, w_sh_gate=w_sh_gate, w_sh_up=w_sh_up, w_sh_down=w_sh_down)
    assert w_in.shape[0] == 1, "single-layer stack"
    w = _prep(raw)
    bp, tp, _ = x_prompt.shape
    bs, ts, _ = x_sample.shape
    n_past = cache_fox_k.shape[2]
    prompt = (x_prompt, c_prompt, jnp.zeros((bp, RWKV_COLS), F32),
              jnp.zeros((bp, N_HEADS, HEAD_DIM, HEAD_DIM), F32),
              jnp.zeros((bp, 0, WIDTH), F32), jnp.zeros((bp, 0, WIDTH), F32), jnp.zeros((bp, 0, N_HEADS), F32))
    sample = (x_sample, c_sample, state_rwkv_shift[0], state_rwkv_wkv[0],
              cache_fox_k[0].reshape(bs, n_past, WIDTH), cache_fox_v[0].reshape(bs, n_past, WIDTH),
              cache_fox_logf[0])
    (yp, wkv_p, sh_p, k_p, v_p, lf_p), (ysm, wkv_s, sh_s, k_s, v_s, lf_s) = _layer([prompt, sample], w)

    def heads(a):
        return a.reshape((1,) + a.shape[:2] + (N_HEADS, HEAD_DIM))

    return (yp, ysm,
            wkv_p[None], sh_p.reshape(1, bp, RWKV_COLS), heads(k_p), heads(v_p), lf_p[None],
            wkv_s[None], sh_s.reshape(1, bs, RWKV_COLS), heads(k_s), heads(v_s), lf_s[None])
```

```python
import functools
import math

import jax
import jax.numpy as jnp
from jax import lax
from jax.experimental import pallas as pl
from jax.experimental.pallas import tpu as pltpu
from jax.experimental.pallas import tpu_sc as plsc

F32 = jnp.float32
BF16 = jnp.bfloat16
I32 = jnp.int32

D_MODEL = 1024
N_HEADS = 8
HEAD_DIM = 64
WIDTH = N_HEADS * HEAD_DIM
HEADS_PER_GROUP = 4
RWKV_CHUNK = 64
RWKV_CHUNKS_PER_STEP = 4
SC_SCATTER_WINDOW = 128
SC_ROW_SPLIT = 2
DECAY_LORA = 64
ICLR_LORA = 64
GATE_LORA = 128
RWKV_COLS = 3 * WIDTH + DECAY_LORA + ICLR_LORA + GATE_LORA
FOX_MAIN_COLS = 4 * WIDTH
GATE_COLS = 2 * D_MODEL
RWKV_GN_EPS = HEAD_DIM * 1e-5
L2_EPS = 1e-12
RMS_EPS = 1e-6
N_EXPERTS = 256
TOP_K = 8
N_GROUPS = 8
TOPK_GROUPS = 4
EXPERTS_PER_GROUP = N_EXPERTS // N_GROUPS
D_EXPERT = 256
ROUTED_SCALE = 2.5

LANES = 128
VMEM_LIMIT = 56 * 1024 * 1024
NEG_BIG = -1e30

NN = (((1,), (0,)), ((), ()))
NT = (((1,), (1,)), ((), ()))
TN = (((0,), (0,)), ((), ()))


def _cparams(sem):
    return pltpu.CompilerParams(dimension_semantics=sem, vmem_limit_bytes=VMEM_LIMIT)


def _dot(a, b, dims=NN):
    return lax.dot_general(a.astype(BF16), b.astype(BF16), dims, preferred_element_type=F32)


def _split2(a):
    hi = a.astype(BF16)
    lo = (a - hi.astype(F32)).astype(BF16)
    return hi, lo


def _split3(a):
    hi = a.astype(BF16)
    r1 = a - hi.astype(F32)
    mid = r1.astype(BF16)
    lo = (r1 - mid.astype(F32)).astype(BF16)
    return hi, mid, lo


def _dot3(a, b, dims=NN):
    ah, al = _split2(a)
    bh, bl = _split2(b)
    d = functools.partial(lax.dot_general, dimension_numbers=dims, preferred_element_type=F32)
    return d(ah, bh) + (d(ah, bl) + d(al, bh))


def _mm3(a, b, dims):
    d = functools.partial(lax.dot_general, dimension_numbers=dims, preferred_element_type=F32)
    return d(a[0], b[0]) + (d(a[0], b[1]) + d(a[1], b[0]))


def _bd_parts(x, mask):
    out = []
    for part in _split2(x):
        tiled = jnp.concatenate([part] * HEADS_PER_GROUP, axis=0)
        out.append(jnp.where(mask, tiled, jnp.zeros_like(tiled)))
    return tuple(out)


def _dot_exact_rhs(a_exact, b, dims=NN):
    ab = a_exact.astype(BF16)
    bh, bm, bl = _split3(b)
    d = functools.partial(lax.dot_general, dimension_numbers=dims, preferred_element_type=F32)
    return d(ab, bh) + (d(ab, bm) + d(ab, bl))


def _gsum(x, g_ref):
    hi, mid, lo = _split3(x)
    g = g_ref[...]
    d = functools.partial(jnp.dot, preferred_element_type=F32)
    return d(hi, g) + (d(mid, g) + d(lo, g))


def _sigmoid(x):
    return 1.0 / (1.0 + jnp.exp(-x))


def _softplus(x):
    return jnp.maximum(x, 0.0) + jnp.log1p(jnp.exp(-jnp.abs(x)))


def _silu(x):
    return x * _sigmoid(x)


def _pack_bf16_pairs(x):
    w = x.shape[1] // 2
    bits = lax.bitcast_convert_type(x.astype(BF16).astype(F32), I32)
    return lax.shift_right_logical(bits[:, :w], 16) | (bits[:, w:] & -65536)


def _unpack_bf16_pairs(p):
    lo = lax.bitcast_convert_type(lax.shift_left(p, 16), F32)
    hi = lax.bitcast_convert_type(p & -65536, F32)
    return jnp.concatenate([lo, hi], axis=1).astype(BF16)


def _group_ones():
    h = jnp.arange(WIDTH, dtype=I32) // HEAD_DIM
    return (h[:, None] == h[None, :]).astype(BF16)


def _ada_kernel(c_ref, w_ref, b_ref, o_ref):
    o_ref[...] = _dot(_silu(c_ref[...]), w_ref[...]) + b_ref[...]


def _ada(c, w_ada, b_ada):
    nb = c.shape[0]
    n_out = w_ada.shape[1]
    blk = D_MODEL
    return pl.pallas_call(
        _ada_kernel,
        grid=(n_out // blk,),
        in_specs=[pl.BlockSpec((nb, D_MODEL), lambda j: (0, 0)),
                  pl.BlockSpec((D_MODEL, blk), lambda j: (0, j)),
                  pl.BlockSpec((1, blk), lambda j: (0, j))],
        out_specs=pl.BlockSpec((nb, blk), lambda j: (0, j)),
        out_shape=jax.ShapeDtypeStruct((nb, n_out), F32),
        compiler_params=_cparams(("parallel",)),
        name="ada_mod",
    )(c, w_ada, b_ada.reshape(1, n_out))


def _inproj_kernel(x_ref, mod_ref, g1_ref, wr_ref, wf_ref, wfl_ref, wg_ref, qn_ref, kn_ref, fb_ref, gm_ref, f0_ref,
                   pr_ref, q_ref, k_ref, v_ref, sg_ref, lf_ref, cf_ref, gate_ref, carry):
    bb, tt, d = x_ref.shape
    m = bb * tt
    x = x_ref[...]
    ms = jnp.mean(x * x, axis=-1, keepdims=True)
    h = x * lax.rsqrt(ms + RMS_EPS) * g1_ref[...]
    h = h * (1.0 + mod_ref[:, 1:2, :]) + mod_ref[:, 0:1, :]
    hb = h.reshape(m, d).astype(BF16)

    pr_ref[...] = jnp.dot(hb, wr_ref[...], preferred_element_type=F32).reshape(bb, tt, RWKV_COLS)

    f = jnp.dot(hb, wf_ref[...], preferred_element_type=F32)
    q = f[:, 0:WIDTH]
    k = f[:, WIDTH:2 * WIDTH]
    v = f[:, 2 * WIDTH:3 * WIDTH]
    og = f[:, 3 * WIDTH:4 * WIDTH]
    inv_hd = 1.0 / HEAD_DIM
    q = q * lax.rsqrt(_gsum(q * q, gm_ref) * inv_hd + RMS_EPS) * qn_ref[...]
    k = k * lax.rsqrt(_gsum(k * k, gm_ref) * inv_hd + RMS_EPS) * kn_ref[...]
    q_ref[...] = (q * (HEAD_DIM ** -0.5)).astype(BF16).reshape(bb, tt, WIDTH)
    k_ref[...] = k.reshape(bb, tt, WIDTH)
    v_ref[...] = v.reshape(bb, tt, WIDTH)
    sg_ref[...] = _sigmoid(og).reshape(bb, tt, WIDTH)

    fl = jnp.dot(hb, wfl_ref[...], preferred_element_type=F32)[:, 0:N_HEADS] + fb_ref[...]
    lf = -_softplus(-fl)
    lf_ref[...] = lf.reshape(bb, tt, N_HEADS)

    @pl.when(pl.program_id(1) == 0)
    def _():
        carry[...] = f0_ref[...]

    r = lax.broadcasted_iota(I32, (m, m), 0)
    c = lax.broadcasted_iota(I32, (m, m), 1)
    tri = jnp.logical_and(r // tt == c // tt, r >= c).astype(F32)
    cf = _dot_exact_rhs(tri, lf).reshape(bb, tt, N_HEADS) + carry[...]
    cf_ref[...] = cf
    carry[...] = cf[:, tt - 1:tt, :]

    gate_ref[...] =_sigmoid(jnp.dot(hb, wg_ref[...], preferred_element_type=F32)).reshape(bb, tt, GATE_COLS)


def _const_spec(shape):
    nd = len(shape)
    return pl.BlockSpec(shape, lambda *_: (0,) * nd)


def _inproj(x, mod, g1, wr, wf, wfl, wg, qn, kn, fb, gmat, f0, bb, tt):
    b, t, d = x.shape
    grid = (b // bb, t // tt)

    def tok(cols):
        return pl.BlockSpec((bb, tt, cols), lambda i, j: (i, j, 0))

    out_cols = [(RWKV_COLS, F32), (WIDTH, BF16), (WIDTH, F32), (WIDTH, F32), (WIDTH, F32), (N_HEADS, F32),
                (N_HEADS, F32), (GATE_COLS, F32)]
    return pl.pallas_call(
        _inproj_kernel,
        grid=grid,
        in_specs=[tok(d),
                  pl.BlockSpec((bb, 6, d), lambda i, j: (i, 0, 0)),
                  _const_spec((1, d)),
                  _const_spec(wr.shape), _const_spec(wf.shape), _const_spec(wfl.shape), _const_spec(wg.shape),
                  _const_spec((1, WIDTH)), _const_spec((1, WIDTH)), _const_spec((1, N_HEADS)),
                  _const_spec((WIDTH, WIDTH)),
                  pl.BlockSpec((bb, 1, N_HEADS), lambda i, j: (i, 0, 0))],
        out_specs=[tok(c) for c, _ in out_cols],
        out_shape=[jax.ShapeDtypeStruct((b, t, c), dt) for c, dt in out_cols],
        scratch_shapes=[pltpu.VMEM((bb, 1, N_HEADS), F32)],
        compiler_params=_cparams(("parallel", "arbitrary")),
        name="norm1_inproj",
    )(x, mod, g1, wr, wf, wfl, wg, qn, kn, fb, gmat, f0)


def _past_cumsum_kernel(x_ref, o_ref):
    x = x_ref[0]
    rows = x.shape[0]
    li = lax.broadcasted_iota(I32, (LANES, LANES), 0)
    lj = lax.broadcasted_iota(I32, (LANES, LANES), 1)
    same_head = (li % N_HEADS) == (lj % N_HEADS)
    within = jnp.logical_and(same_head, li // N_HEADS <= lj // N_HEADS).astype(BF16)
    xh, xm, xl = _split3(x)
    d2 = functools.partial(jnp.dot, preferred_element_type=F32)
    in_row = d2(xh, within) + (d2(xm, within) + d2(xl, within))
    sh = same_head.astype(BF16)
    row_tot = d2(xh, sh) + (d2(xm, sh) + d2(xl, sh))
    ri = lax.broadcasted_iota(I32, (rows, rows), 0)
    ci = lax.broadcasted_iota(I32, (rows, rows), 1)
    o_ref[0] = in_row + _dot_exact_rhs((ri > ci).astype(F32), row_tot)


def _past_cumsum(past_logf):
    b, p, h = past_logf.shape
    rows = p * h // LANES
    flat = past_logf.reshape(b, rows, LANES)
    out = pl.pallas_call(
        _past_cumsum_kernel,
        grid=(b,),
        in_specs=[pl.BlockSpec((1, rows, LANES), lambda i: (i, 0, 0))],
        out_specs=pl.BlockSpec((1, rows, LANES), lambda i: (i, 0, 0)),
        out_shape=jax.ShapeDtypeStruct((b, rows, LANES), F32),
        compiler_params=_cparams(("parallel",)),
        name="cache_logf_cumsum",
    )(flat)
    return out.reshape(b, p, h)


def _fox_kernel(*refs, n_past_blocks, tq):
    if n_past_blocks:
        (q_ref, fq_ref, sg_ref, kp_ref, vp_ref, fkp_ref, kn_ref, vn_ref, fkn_ref,
         o_ref, m_scr, l_scr, acc_scr) = refs
    else:
        q_ref, fq_ref, sg_ref, kn_ref, vn_ref, fkn_ref, o_ref, m_scr, l_scr, acc_scr = refs
    qi = pl.program_id(1)
    ki = pl.program_id(2)
    nk = pl.num_programs(2)

    @pl.when(ki == 0)
    def _():
        m_scr[...] = jnp.full(m_scr.shape, NEG_BIG, F32)
        l_scr[...] = jnp.zeros(l_scr.shape, F32)
        acc_scr[...] = jnp.zeros(acc_scr.shape, F32)

    lane_a = lax.broadcasted_iota(I32, (tq, LANES), 1) < HEAD_DIM

    def step(k_ref, v_ref, fk_ref, diag):
        tk = k_ref.shape[1]
        if diag:
            rq = lax.broadcasted_iota(I32, (tq, tk), 0)
            ck = lax.broadcasted_iota(I32, (tq, tk), 1)
            visible = ck <= rq
        fq_all = fq_ref[0]
        pairs = range(N_HEADS // 2)
        cols = [slice(j * LANES, (j + 1) * LANES) for j in pairs]
        scores = []
        for j in pairs:
            qj = q_ref[0, :, cols[j]]
            kb = k_ref[0, :, cols[j]].astype(BF16)
            for hh in range(2):
                h = 2 * j + hh
                qm = jnp.where(lane_a if hh == 0 else jnp.logical_not(lane_a), qj, jnp.zeros_like(qj))
                s = lax.dot_general(qm, kb, NT, preferred_element_type=F32)
                s = s + fq_all[:, h:h + 1] - fk_ref[0, h:h + 1, :]
                if diag:
                    s = jnp.where(visible, s, NEG_BIG)
                scores.append(s)
        alphas, probs = [], []
        for h in range(N_HEADS):
            m_old = m_scr[h]
            m_new = jnp.maximum(m_old, jnp.max(scores[h], axis=-1, keepdims=True))
            alpha = jnp.exp(m_old - m_new)
            p = jnp.exp(scores[h] - m_new)
            l_scr[h] = alpha * l_scr[h] + jnp.sum(p, axis=-1, keepdims=True)
            m_scr[h] = m_new
            alphas.append(alpha)
            probs.append(p.astype(BF16))
        for j in pairs:
            vb = v_ref[0, :, cols[j]].astype(BF16)
            pv0 = jnp.dot(probs[2 * j], vb, preferred_element_type=F32)
            pv1 = jnp.dot(probs[2 * j + 1], vb, preferred_element_type=F32)
            acc_scr[:, cols[j]] = (acc_scr[:, cols[j]] * jnp.where(lane_a, alphas[2 * j], alphas[2 * j + 1])
                                   + jnp.where(lane_a, pv0, pv1))

    if n_past_blocks:
        @pl.when(ki < n_past_blocks)
        def _():
            step(kp_ref, vp_ref, fkp_ref, False)

    kn = ki - n_past_blocks

    @pl.when(jnp.logical_and(kn >= 0, kn < qi))
    def _():
        step(kn_ref, vn_ref, fkn_ref, False)

    @pl.when(kn == qi)
    def _():
        step(kn_ref, vn_ref, fkn_ref, True)

    @pl.when(ki == nk - 1)
    def _():
        for j in range(N_HEADS // 2):
            cols = slice(j * LANES, (j + 1) * LANES)
            l = jnp.where(lane_a, l_scr[2 * j], l_scr[2 * j + 1])
            o_ref[0, :, cols] = acc_scr[:, cols] / l * sg_ref[0, :, cols]


def _fox_attention(q, fq, sg, k_new, v_new, fk_new_t, past=None, tq=512, tk_past=512):
    b, t, _ = q.shape
    nq = t // tq
    n_past_blocks = 0 if past is None else past[0].shape[1] // tk_past
    nk = n_past_blocks + nq

    def new_idx(i, qi, ki):
        return jnp.clip(ki - n_past_blocks, 0, qi)

    in_specs = [pl.BlockSpec((1, tq, WIDTH), lambda i, qi, ki: (i, qi, 0)),
                pl.BlockSpec((1, tq, N_HEADS), lambda i, qi, ki: (i, qi, 0)),
                pl.BlockSpec((1, tq, WIDTH), lambda i, qi, ki: (i, qi, 0))]
    args = [q, fq, sg]
    if n_past_blocks:
        def past_idx(i, qi, ki):
            return jnp.minimum(ki, n_past_blocks - 1)
        in_specs += [pl.BlockSpec((1, tk_past, WIDTH), lambda i, qi, ki: (i, past_idx(i, qi, ki), 0)),
                     pl.BlockSpec((1, tk_past, WIDTH), lambda i, qi, ki: (i, past_idx(i, qi, ki), 0)),
                     pl.BlockSpec((1, N_HEADS, tk_past), lambda i, qi, ki: (i, 0, past_idx(i, qi, ki)))]
        args += list(past)
    in_specs += [pl.BlockSpec((1, tq, WIDTH), lambda i, qi, ki: (i, new_idx(i, qi, ki), 0)),
                 pl.BlockSpec((1, tq, WIDTH), lambda i, qi, ki: (i, new_idx(i, qi, ki), 0)),
                 pl.BlockSpec((1, N_HEADS, tq), lambda i, qi, ki: (i, 0, new_idx(i, qi, ki)))]
    args += [k_new, v_new, fk_new_t]
    return pl.pallas_call(
        functools.partial(_fox_kernel, n_past_blocks=n_past_blocks, tq=tq),
        grid=(b, nq, nk),
        in_specs=in_specs,
        out_specs=pl.BlockSpec((1, tq, WIDTH), lambda i, qi, ki: (i, qi, 0)),
        out_shape=jax.ShapeDtypeStruct((b, t, WIDTH), F32),
        scratch_shapes=[pltpu.VMEM((N_HEADS, tq, 1), F32), pltpu.VMEM((N_HEADS, tq, 1), F32),
                        pltpu.VMEM((tq, WIDTH), F32)],
        compiler_params=_cparams(("parallel", "parallel", "arbitrary")),
        name="fox_attention",
    )(*args)


def _rwkv_kernel(p_ref, sh0_ref, s0_ref, mu_ref, w0_ref, wb_ref, a0_ref, ab_ref, gb_ref, kk_ref, ka_ref, rk_ref,
                 lnw_ref, lnb_ref, gm_ref, y_ref, st_ref, sht_ref, z_scr, prev_scr, *, c):
    t = pl.program_id(1)
    nt = pl.num_programs(1)
    n_rows = p_ref.shape[1]
    n_chunks = n_rows // c

    def head_block(h):
        lo = (h % HEADS_PER_GROUP) * HEAD_DIM
        return h // HEADS_PER_GROUP, slice(lo, lo + HEAD_DIM)

    @pl.when(t == 0)
    def _():
        z_scr[...] = jnp.zeros(z_scr.shape, F32)
        for h in range(N_HEADS):
            i, blk = head_block(h)
            z_scr[i, blk, blk] = s0_ref[0, h]
        prev_scr[...] = sh0_ref[0]

    p = p_ref[0]
    row = lax.broadcasted_iota(I32, p.shape, 0)
    prev = jnp.where(row == 0, prev_scr[...], pltpu.roll(p, 1, 0))
    last = p[n_rows - 1:n_rows, :]
    prev_scr[...] = last
    sht_ref[0] = last

    pm = p + (prev - p) * mu_ref[...]
    r = pm[:, 0:WIDTH]
    k = pm[:, WIDTH:2 * WIDTH]
    v = pm[:, 2 * WIDTH:3 * WIDTH]
    o1 = 3 * WIDTH
    wd = pm[:, o1:o1 + DECAY_LORA]
    ad = pm[:, o1 + DECAY_LORA:o1 + DECAY_LORA + ICLR_LORA]
    gd = pm[:, o1 + DECAY_LORA + ICLR_LORA:RWKV_COLS]

    w = -_softplus(-(w0_ref[...] + _dot(jnp.tanh(wd), wb_ref[...]))) - 0.5
    lw = -jnp.exp(w)
    a = _sigmoid(a0_ref[...] + _dot(ad, ab_ref[...]))
    g = _dot(_sigmoid(gd), gb_ref[...])
    kk = k * kk_ref[...]
    kk = kk / jnp.maximum(jnp.sqrt(_gsum(kk * kk, gm_ref)), L2_EPS)
    kf = k * (1.0 + (a - 1.0) * ka_ref[...])

    ri = lax.broadcasted_iota(I32, (n_rows, n_rows), 0)
    ci = lax.broadcasted_iota(I32, (n_rows, n_rows), 1)
    same_chunk = (ri // c) == (ci // c)
    cum = _dot_exact_rhs(jnp.logical_and(same_chunk, ri >= ci).astype(F32), lw)
    cum_last = _dot_exact_rhs(same_chunk.astype(F32), lw)
    r_t = r * jnp.exp(cum)
    a_t = -kk * jnp.exp(cum - lw)
    inv = jnp.exp(-cum)
    b_t = kk * a * inv
    k_t = kf * inv
    to_end = jnp.exp(cum_last - cum)
    b_e = kk * a * to_end
    k_e = kf * to_end
    g_end = jnp.exp(cum_last)

    hg = HEADS_PER_GROUP
    gw = hg * HEAD_DIM
    log_c = int(math.log2(c))
    t_idx = lax.broadcasted_iota(I32, (c, hg * c), 0)
    s_idx = lax.broadcasted_iota(I32, (c, hg * c), 1) & (c - 1)
    strict = s_idx < t_idx
    lower = s_idx <= t_idx
    eye = (s_idx == t_idx).astype(F32)
    rb = lax.broadcasted_iota(I32, (hg * c, gw), 0) >> log_c
    mask_kv = rb == (lax.broadcasted_iota(I32, (hg * c, gw), 1) >> int(math.log2(HEAD_DIM)))
    rs = lax.broadcasted_iota(I32, (hg * c, hg * c), 0) >> log_c
    mask_ss = rs == (lax.broadcasted_iota(I32, (hg * c, hg * c), 1) >> log_c)
    ng = N_HEADS // hg
    cat = functools.partial(jnp.concatenate, axis=0)
    units = [(slice(j * c, (j + 1) * c), slice(i * gw, (i + 1) * gw)) for j in range(n_chunks) for i in range(ng)]
    nu = len(units)

    ar = [_split2(cat([a_t[rs_, s], r_t[rs_, s]])) for rs_, s in units]
    ab = [_mm3(ar[n], _bd_parts(b_t[units[n]], mask_kv), NT) for n in range(nu)]
    ak = [_mm3(ar[n], _bd_parts(k_t[units[n]], mask_kv), NT) for n in range(nu)]
    l_ab = [jnp.where(strict, m[:c], 0.0) for m in ab]
    l_rb = [jnp.where(lower, m[c:], 0.0) for m in ab]
    l_ak = [jnp.where(strict, m[:c], 0.0) for m in ak]
    l_rk = [jnp.where(lower, m[c:], 0.0) for m in ak]
    def mm1(a, b_bd):
        return jnp.dot(a.astype(BF16), b_bd, preferred_element_type=F32)

    def bd1(x, mask):
        tiled = jnp.concatenate([x.astype(BF16)] * hg, axis=0)
        return jnp.where(mask, tiled, jnp.zeros_like(tiled))

    tinv = [eye + m for m in l_ab]
    pw = [mm1(m, bd1(m, mask_ss)) for m in l_ab]
    for _ in range(1, log_c - 1):
        res = [mm1(cat([tinv[n], pw[n]]), bd1(pw[n], mask_ss)) for n in range(nu)]
        tinv = [tinv[n] + res[n][:c] for n in range(nu)]
        pw = [m[c:] for m in res]
    tinv = [tinv[n] + mm1(tinv[n], bd1(pw[n], mask_ss)) for n in range(nu)]
    av = [_mm3(_split2(cat([l_ak[n], l_rk[n]])), _bd_parts(v[units[n]], mask_kv), NN) for n in range(nu)]
    ue = [_split2(cat([b_e[units[n]], k_e[units[n]]])) for n in range(nu)]

    def wide(fn, x):
        return [fn(x[:, :gw]), fn(x[:, gw:])]

    def bd1w(x):
        return jnp.concatenate(wide(lambda h_: bd1(h_, mask_kv), x), axis=1)

    def bd3w(x):
        parts = wide(lambda h_: _bd_parts(h_, mask_kv), x)
        return tuple(jnp.concatenate([parts[0][q], parts[1][q]], axis=1) for q in range(2))

    rhs = [jnp.concatenate([a_t[units[n]], av[n][:c]], axis=1) for n in range(nu)]
    x0 = [mm1(tinv[n], bd1w(rhs[n])) for n in range(nu)]
    resid = [rhs[n] - (x0[n] - _mm3(_split2(l_ab[n]), bd3w(x0[n]), NN)) for n in range(nu)]
    sol = [x0[n] + mm1(tinv[n], bd1w(resid[n])) for n in range(nu)]
    lift = [_mm3(_split2(l_rb[n]), bd3w(sol[n]), NN) for n in range(nu)]
    lhs_s = [_split2(cat([sol[n][:, :gw], r_t[units[n]] + lift[n][:, :gw]])) for n in range(nu)]
    u_loc = [sol[n][:, gw:] for n in range(nu)]
    o_loc = [av[n][c:] + lift[n][:, gw:] for n in range(nu)]

    zr = lax.broadcasted_iota(I32, (gw, gw), 0) >> int(math.log2(HEAD_DIM))
    zmask = zr == (lax.broadcasted_iota(I32, (gw, gw), 1) >> int(math.log2(HEAD_DIM)))
    z = [z_scr[i] for i in range(ng)]
    o_rows = []
    for j in range(n_chunks):
        o_grp = []
        for i in range(ng):
            n = j * ng + i
            rs_, s = units[n]
            sz = _mm3(lhs_s[n], _split2(z[i]), NT)
            u = sz[:c] + u_loc[n]
            o_grp.append(sz[c:] + o_loc[n])
            upd = _mm3(_split2(cat([u, v[rs_, s]])), ue[n], TN)
            z[i] = z[i] * g_end[j * c:j * c + 1, s] + jnp.where(zmask, upd, 0.0)
        o_rows.append(jnp.concatenate(o_grp, axis=1))
    for i in range(ng):
        z_scr[i] = z[i]

    o = cat(o_rows)
    inv_hd = 1.0 / HEAD_DIM
    dlt = o - _gsum(o, gm_ref) * inv_hd
    var = _gsum(dlt * dlt, gm_ref) * inv_hd
    on = dlt * lax.rsqrt(var + RWKV_GN_EPS) * lnw_ref[...] + lnb_ref[...]
    bonus = _gsum(r * kf * rk_ref[...], gm_ref) * v
    y_ref[0] = (on + bonus) * g

    @pl.when(t == nt - 1)
    def _():
        for h in range(N_HEADS):
            i, blk = head_block(h)
            st_ref[0, h] = z_scr[i, blk, blk]


def _rwkv(p, shift0, s0, prm, gmat, chunk, chunks_per_step):
    b, t, _ = p.shape
    row = lambda n: _const_spec((1, n))
    rows = chunk * chunks_per_step
    return pl.pallas_call(
        functools.partial(_rwkv_kernel, c=chunk),
        grid=(b, t // rows),
        in_specs=[pl.BlockSpec((1, rows, RWKV_COLS), lambda i, j: (i, j, 0)),
                  pl.BlockSpec((1, 1, RWKV_COLS), lambda i, j: (i, 0, 0)),
                  pl.BlockSpec((1, N_HEADS, HEAD_DIM, HEAD_DIM), lambda i, j: (i, 0, 0, 0)),
                  row(RWKV_COLS), row(WIDTH), _const_spec((DECAY_LORA, WIDTH)), row(WIDTH),
                  _const_spec((ICLR_LORA, WIDTH)), _const_spec((GATE_LORA, WIDTH)),
                  row(WIDTH), row(WIDTH), row(WIDTH), row(WIDTH), row(WIDTH), _const_spec((WIDTH, WIDTH))],
        out_specs=[pl.BlockSpec((1, rows, WIDTH), lambda i, j: (i, j, 0)),
                   pl.BlockSpec((1, N_HEADS, HEAD_DIM, HEAD_DIM), lambda i, j: (i, 0, 0, 0)),
                   pl.BlockSpec((1, 1, RWKV_COLS), lambda i, j: (i, 0, 0))],
        out_shape=[jax.ShapeDtypeStruct((b, t, WIDTH), F32),
                   jax.ShapeDtypeStruct((b, N_HEADS, HEAD_DIM, HEAD_DIM), F32),
                   jax.ShapeDtypeStruct((b, 1, RWKV_COLS), F32)],
        scratch_shapes=[pltpu.VMEM((N_HEADS // HEADS_PER_GROUP, HEADS_PER_GROUP * HEAD_DIM,
                                    HEADS_PER_GROUP * HEAD_DIM), F32),
                        pltpu.VMEM((1, RWKV_COLS), F32)],
        compiler_params=_cparams(("parallel", "arbitrary")),
        name="rwkv7_mix",
    )(p, shift0, s0, prm["mu"], prm["w0"], prm["wb"], prm["a0"], prm["ab"], prm["gb"], prm["kk"], prm["ka"],
      prm["rk"], prm["lnw"], prm["lnb"], gmat)


def _merge_kernel(x_ref, ya_ref, yb_ref, gate_ref, mod_ref, g2_ref, woa_ref, wob_ref, wo_ref, wrh_ref, wrl_ref,
                  *rest):
    x1_ref, h2_ref, lg_ref = rest[-3:]
    bb, tt, d = x_ref.shape
    m = bb * tt
    gate = gate_ref[...].reshape(m, GATE_COLS)
    merged = (gate[:, 0:d] * _dot(ya_ref[...].reshape(m, WIDTH), woa_ref[...])
              + gate[:, d:2 * d] * _dot(yb_ref[...].reshape(m, WIDTH), wob_ref[...]))
    x1 = x_ref[...] + mod_ref[:, 2:3, :] * _dot(merged, wo_ref[...]).reshape(bb, tt, d)
    x1_ref[...] = x1
    ms = jnp.mean(x1 * x1, axis=-1, keepdims=True)
    h2 = x1 * lax.rsqrt(ms + RMS_EPS) * g2_ref[...]
    h2 = (h2 * (1.0 + mod_ref[:, 4:5, :]) + mod_ref[:, 3:4, :]).reshape(m, d)
    h2_ref[...] = _pack_bf16_pairs(h2)
    lg_ref[...] = _mm3((wrh_ref[...], wrl_ref[...]), _split2(h2), NT)


def _merge(x, ya, yb, gate, mod, w, n_total, row_offset, shared=None):
    b, t, d = x.shape
    bb, tt = _token_blocks(b, t)
    nt = t // tt
    m = bb * tt
    off = row_offset // m

    def tok(cols):
        return pl.BlockSpec((bb, tt, cols), lambda i, j: (i, j, 0))

    in_specs = [tok(d), tok(WIDTH), tok(WIDTH), tok(GATE_COLS),
                pl.BlockSpec((bb, 6, d), lambda i, j: (i, 0, 0)),
                _const_spec((1, d)), _const_spec((WIDTH, d)), _const_spec((WIDTH, d)), _const_spec((d, d)),
                _const_spec((N_EXPERTS, d)), _const_spec((N_EXPERTS, d))]
    args = [x, ya, yb, gate, mod.reshape(b, 6, d), w["g2"], w["w_oa"], w["w_ob"], w["w_o"], w["wr_hi"], w["wr_lo"]]
    aliases = {}
    if shared is not None:
        aliases = {len(args): 1, len(args) + 1: 2}
        in_specs += [pl.BlockSpec(memory_space=pl.ANY), pl.BlockSpec(memory_space=pl.ANY)]
        args += list(shared)
    return pl.pallas_call(
        _merge_kernel,
        grid=(b // bb, nt),
        in_specs=in_specs,
        out_specs=[tok(d), pl.BlockSpec((m, d // 2), lambda i, j: (off + i * nt + j, 0)),
                   pl.BlockSpec((N_EXPERTS, m), lambda i, j: (0, off + i * nt + j))],
        out_shape=[jax.ShapeDtypeStruct((b, t, d), F32), jax.ShapeDtypeStruct((n_total, d // 2), I32),
                   jax.ShapeDtypeStruct((N_EXPERTS, n_total), F32)],
        input_output_aliases=aliases,
        compiler_params=_cparams(("parallel", "parallel")),
        name="merge_norm2_router",
    )(*args)


def _route_kernel(lg_ref, bias_ref, idx_ref, wt_ref, rank_ref, cnt_ref, carry):
    @pl.when(pl.program_id(0) == 0)
    def _():
        carry[...] = jnp.zeros(carry.shape, F32)

    tm = lg_ref.shape[1]
    scores = _sigmoid(lg_ref[...])
    sel = scores + bias_ref[...]
    row = lax.broadcasted_iota(I32, (N_EXPERTS, tm), 0)
    neg_inf = -jnp.inf

    def first_argmax(vals, rows):
        mx = jnp.max(vals, axis=0, keepdims=True)
        return mx, jnp.min(jnp.where(vals == mx, rows, N_EXPERTS), axis=0, keepdims=True)

    gslices = [slice(g * EXPERTS_PER_GROUP, (g + 1) * EXPERTS_PER_GROUP) for g in range(N_GROUPS)]
    gs = []
    row_g = lax.broadcasted_iota(I32, (EXPERTS_PER_GROUP, tm), 0)
    for sl in gslices:
        m1, i1 = first_argmax(sel[sl], row_g)
        m2 = jnp.max(jnp.where(row_g == i1, neg_inf, sel[sl]), axis=0, keepdims=True)
        gs.append(m1 + m2)
    kept = []
    for g in range(N_GROUPS):
        beaten = jnp.zeros((1, tm), I32)
        for o in range(N_GROUPS):
            if o != g:
                wins = (gs[o] >= gs[g]) if o < g else (gs[o] > gs[g])
                beaten = beaten + wins.astype(I32)
        kept.append(jnp.where(beaten < TOPK_GROUPS, sel[gslices[g]], neg_inf))
    cur = jnp.concatenate(kept, axis=0)

    idxs, ws = [], []
    picked = jnp.zeros((N_EXPERTS, tm), F32)
    for _ in range(TOP_K):
        _, ik = first_argmax(cur, row)
        hit = row == ik
        idxs.append(ik)
        ws.append(jnp.sum(jnp.where(hit, scores, 0.0), axis=0, keepdims=True))
        cur = jnp.where(hit, neg_inf, cur)
        picked = jnp.where(hit, 1.0, picked)
    wsum = ws[0]
    for k in range(1, TOP_K):
        wsum = wsum + ws[k]

    r = lax.broadcasted_iota(I32, (tm, tm), 0)
    c = lax.broadcasted_iota(I32, (tm, tm), 1)
    before = jnp.dot(picked.astype(BF16), (r < c).astype(BF16), preferred_element_type=F32) + carry[...]
    carry[...] = carry[...] + jnp.sum(picked, axis=1, keepdims=True)
    cnt_ref[...] = carry[...]

    kk = lax.broadcasted_iota(I32, (TOP_K, tm), 0)
    idx_o = jnp.zeros((TOP_K, tm), I32)
    wt_o = jnp.zeros((TOP_K, tm), F32)
    rank_o = jnp.zeros((TOP_K, tm), F32)
    for k in range(TOP_K):
        rk = jnp.sum(jnp.where(row == idxs[k], before, 0.0), axis=0, keepdims=True)
        idx_o = jnp.where(kk == k, idxs[k], idx_o)
        wt_o = jnp.where(kk == k, ws[k] / wsum * ROUTED_SCALE, wt_o)
        rank_o = jnp.where(kk == k, rk, rank_o)
    idx_ref[...] = idx_o
    wt_ref[...] = wt_o
    rank_ref[...] = rank_o.astype(I32)


def _route(logits_t, bias_col, tm):
    n = logits_t.shape[1]
    tokk = pl.BlockSpec((TOP_K, tm), lambda i: (0, i))
    return pl.pallas_call(
        _route_kernel,
        grid=(n // tm,),
        in_specs=[pl.BlockSpec((N_EXPERTS, tm), lambda i: (0, i)), _const_spec((N_EXPERTS, 1))],
        out_specs=[tokk, tokk, tokk, _const_spec((N_EXPERTS, 1))],
        out_shape=[jax.ShapeDtypeStruct((TOP_K, n), I32), jax.ShapeDtypeStruct((TOP_K, n), F32),
                   jax.ShapeDtypeStruct((TOP_K, n), I32), jax.ShapeDtypeStruct((N_EXPERTS, 1), F32)],
        scratch_shapes=[pltpu.VMEM((N_EXPERTS, 1), F32)],
        compiler_params=_cparams(("arbitrary",)),
        name="route_topk",
    )(logits_t, bias_col)


def _plan_kernel(cnt_ref, start_ref, be_ref, valid_ref, nu_ref, *, blk):
    cnt = cnt_ref[...]
    padded = jnp.ceil(cnt * (1.0 / blk)) * blk
    e_r = lax.broadcasted_iota(I32, (N_EXPERTS, N_EXPERTS), 0)
    e_c = lax.broadcasted_iota(I32, (N_EXPERTS, N_EXPERTS), 1)
    incl = (e_r <= e_c).astype(BF16)
    ph, pm, plo = _split3(jnp.broadcast_to(padded, (8, N_EXPERTS)))
    d2 = functools.partial(jnp.dot, preferred_element_type=F32)
    pad_end = (d2(ph, incl) + (d2(pm, incl) + d2(plo, incl)))[0:1, :]
    pad_start = pad_end - padded
    start_ref[...] = pad_start.astype(I32)
    total = jnp.max(pad_end, axis=-1, keepdims=True)
    nu_ref[...] = jnp.broadcast_to(total * (1.0 / blk), (1, N_EXPERTS)).astype(I32)
    nb = be_ref.shape[0]
    first = (lax.broadcasted_iota(I32, (nb, N_EXPERTS), 0) * blk).astype(F32)
    lane = lax.broadcasted_iota(I32, (nb, N_EXPERTS), 1)
    inside = jnp.logical_and(pad_start <= first, first < pad_end)
    be_ref[...] = jnp.sum(jnp.where(inside, lane, 0), axis=-1, keepdims=True)
    rows = jnp.minimum(pad_start + cnt - first, float(blk))
    valid_ref[...] = jnp.sum(jnp.where(inside, rows, 0.0), axis=-1, keepdims=True).astype(I32)


def _plan(counts, n_blocks, blk):
    return pl.pallas_call(
        functools.partial(_plan_kernel, blk=blk),
        out_shape=[jax.ShapeDtypeStruct((1, N_EXPERTS), I32), jax.ShapeDtypeStruct((n_blocks, 1), I32),
                   jax.ShapeDtypeStruct((n_blocks, 1), I32), jax.ShapeDtypeStruct((1, N_EXPERTS), I32)],
        compiler_params=pltpu.CompilerParams(vmem_limit_bytes=VMEM_LIMIT),
        name="dispatch_plan",
    )(counts)


def _dest_kernel(idx_ref, rank_ref, start_ref, dest_ref):
    tm = idx_ref.shape[1]
    row = lax.broadcasted_iota(I32, (N_EXPERTS, tm), 0)
    kk = lax.broadcasted_iota(I32, (TOP_K, tm), 0)
    idx = idx_ref[...]
    base = jnp.zeros((TOP_K, tm), I32)
    for k in range(TOP_K):
        bk = jnp.sum(jnp.where(row == idx[k:k + 1, :], start_ref[...], 0), axis=0, keepdims=True)
        base = jnp.where(kk == k, bk, base)
    dest_ref[...] = base + rank_ref[...]


def _dest(idx, rank, pad_start_col, tm):
    n = idx.shape[1]
    tokk = pl.BlockSpec((TOP_K, tm), lambda i: (0, i))
    return pl.pallas_call(
        _dest_kernel,
        grid=(n // tm,),
        in_specs=[tokk, tokk, _const_spec((N_EXPERTS, 1))],
        out_specs=tokk,
        out_shape=jax.ShapeDtypeStruct((TOP_K, n), I32),
        compiler_params=_cparams(("parallel",)),
        name="dispatch_dest",
    )(idx, rank, pad_start_col)


def _dispatch(h2p, dest_t, n_slots):
    n, wp = h2p.shape
    half = wp // SC_ROW_SPLIT
    window = SC_SCATTER_WINDOW
    mesh = plsc.VectorSubcoreMesh(core_axis_name="core", subcore_axis_name="subcore")
    out = jax.ShapeDtypeStruct((n_slots, half), h2p.dtype)

    @functools.partial(pl.kernel, out_type=[out] * SC_ROW_SPLIT, mesh=mesh, scratch_types=[])
    def scatter(rows_hbm, idx_hbm, *outs):
        for c, out_hbm in enumerate(outs):
            def body(rows_vmem, idx_vmem, out_hbm=out_hbm):
                for k in range(TOP_K):
                    pltpu.sync_copy(rows_vmem, out_hbm.at[idx_vmem.at[k]])

            pltpu.emit_pipeline(
                body, grid=(n // window,),
                in_specs=[pl.BlockSpec((window, half), index_map=lambda i, c=c: (i, c)),
                          pl.BlockSpec((TOP_K, window), index_map=lambda i: (0, i))],
                out_specs=[], core_axis_name=("core", "subcore"), dimension_semantics=(pltpu.PARALLEL,),
            )(rows_hbm, idx_hbm)

    return scatter(h2p, dest_t)


def _expert_kernel(be_ref, valid_ref, nu_ref, xa_ref, xb_ref, wg_ref, wu_ref, wd_ref, y_ref, wg_b, wu_b, wd_b):
    i = pl.program_id(0)
    nv = valid_ref[i]
    new_expert = jnp.logical_or(i == 0, be_ref[i] != be_ref[jnp.maximum(i - 1, 0)])

    @pl.when(jnp.logical_and(nv > 0, new_expert))
    def _():
        wg_b[...] = wg_ref[0].astype(BF16)
        wu_b[...] = wu_ref[0].astype(BF16)
        wd_b[...] = wd_ref[0].astype(BF16)

    @pl.when(nv > 0)
    def _():
        blk = xa_ref.shape[0]
        rows = lax.broadcasted_iota(I32, (blk, 1), 0)
        packed = jnp.concatenate([xa_ref[...], xb_ref[...]], axis=1)
        x = _unpack_bf16_pairs(jnp.where(rows < nv, packed, 0))
        hg = jnp.dot(x, wg_b[...], preferred_element_type=F32)
        hu = jnp.dot(x, wu_b[...], preferred_element_type=F32)
        y_ref[...] = jnp.dot((_silu(hg) * hu).astype(BF16), wd_b[...], preferred_element_type=F32)


def _experts(xs, block_e, valid, n_used, w_eg, w_eu, w_ed, blk):
    xa, xb = xs
    n_slots, packed = xa.shape
    d = w_eg.shape[1]
    n_blocks = n_slots // blk

    def row_blk(i, be, valid, nu):
        return (jnp.minimum(i, nu[0] - 1), 0)

    def w_blk(i, be, valid, nu):
        return (be[i], 0, 0)

    return pl.pallas_call(
        _expert_kernel,
        grid_spec=pltpu.PrefetchScalarGridSpec(
            num_scalar_prefetch=3,
            grid=(n_blocks,),
            in_specs=[pl.BlockSpec((blk, packed), row_blk), pl.BlockSpec((blk, packed), row_blk),
                      pl.BlockSpec((1, d, D_EXPERT), w_blk), pl.BlockSpec((1, d, D_EXPERT), w_blk),
                      pl.BlockSpec((1, D_EXPERT, d), w_blk)],
            out_specs=pl.BlockSpec((blk, d), row_blk),
            scratch_shapes=[pltpu.VMEM((d, D_EXPERT), BF16), pltpu.VMEM((d, D_EXPERT), BF16),
                            pltpu.VMEM((D_EXPERT, d), BF16)]),
        out_shape=jax.ShapeDtypeStruct((n_slots, d), F32),
        compiler_params=_cparams(("arbitrary",)),
        name="moe_experts",
    )(block_e, valid, n_used, xa, xb, w_eg, w_eu, w_ed)


def _final_kernel(dest_ref, dest_next_ref, x1_ref, h2_ref, wt_ref, mod_ref, wsg_ref, wsu_ref, wsd_ref, ys_ref,
                  o_ref, ybuf, sems):
    bb, tt, d = x1_ref.shape
    m = bb * tt
    step = pl.program_id(0) * pl.num_programs(1) + pl.program_id(1)
    n_steps = pl.num_programs(0) * pl.num_programs(1)
    slot = step % 2

    def gather(dests, buf):
        def issue(r, carry):
            for k in range(TOP_K):
                pltpu.make_async_copy(ys_ref.at[pl.ds(dests[k, r], 1)], ybuf.at[buf, k, pl.ds(r, 1)],
                                      sems.at[buf]).start()
            return carry

        lax.fori_loop(0, m, issue, 0)

    @pl.when(step == 0)
    def _():
        gather(dest_ref, 0)

    @pl.when(step + 1 < n_steps)
    def _():
        gather(dest_next_ref, 1 - slot)

    hb = _unpack_bf16_pairs(h2_ref[...])
    hg = jnp.dot(hb, wsg_ref[...], preferred_element_type=F32)
    hu = jnp.dot(hb, wsu_ref[...], preferred_element_type=F32)
    ffn = _dot(_silu(hg) * hu, wsd_ref[...])

    for k in range(TOP_K):
        pltpu.make_async_copy(ys_ref.at[pl.ds(0, m)], ybuf.at[slot, k], sems.at[slot]).wait()

    wt = wt_ref[...]
    for k in range(TOP_K):
        ffn = ffn + wt[:, k:k + 1] * ybuf[slot, k]
    o_ref[...] = x1_ref[...] + mod_ref[:, 5:6, :] * ffn.reshape(bb, tt, d)


def _final(x1, h2_all, wts_all, dest_t, ys, mod, w, row_offset):
    b, t, d = x1.shape
    bb, tt = _token_blocks(b, t)
    nt = t // tt
    m = bb * tt
    off = row_offset // m

    def flat_idx(i, j):
        return off + i * nt + j

    last = off + (b // bb) * nt - 1

    return pl.pallas_call(
        _final_kernel,
        grid=(b // bb, nt),
        in_specs=[pl.BlockSpec((TOP_K, m), lambda i, j: (0, flat_idx(i, j)), memory_space=pltpu.SMEM),
                  pl.BlockSpec((TOP_K, m), lambda i, j: (0, jnp.minimum(flat_idx(i, j) + 1, last)),
                               memory_space=pltpu.SMEM),
                  pl.BlockSpec((bb, tt, d), lambda i, j: (i, j, 0)),
                  pl.BlockSpec((m, h2_all.shape[1]), lambda i, j: (flat_idx(i, j), 0)),
                  pl.BlockSpec((m, TOP_K), lambda i, j: (flat_idx(i, j), 0)),
                  pl.BlockSpec((bb, 6, d), lambda i, j: (i, 0, 0)),
                  _const_spec((d, D_EXPERT)), _const_spec((d, D_EXPERT)), _const_spec((D_EXPERT, d)),
                  pl.BlockSpec(memory_space=pl.ANY)],
        out_specs=pl.BlockSpec((bb, tt, d), lambda i, j: (i, j, 0)),
        out_shape=jax.ShapeDtypeStruct((b, t, d), F32),
        scratch_shapes=[pltpu.VMEM((2, TOP_K, m, d), F32), pltpu.SemaphoreType.DMA((2,))],
        compiler_params=_cparams(("arbitrary", "arbitrary")),
        name="moe_combine_final",
    )(dest_t, dest_t, x1, h2_all, wts_all, mod.reshape(b, 6, d), w["w_sg"], w["w_su"], w["w_sd"], ys)


def _moe_routed(h2_all, logits_all, w, blk=256, tm=256):
    n = h2_all.shape[0]
    n_blocks = (n * TOP_K + N_EXPERTS * (blk - 1)) // blk + 1
    n_blocks = (n_blocks + 7) // 8 * 8
    idx, wts_t, rank, counts = _route(logits_all, w["router_bias"], tm)
    pad_start, block_e, valid, n_used = _plan(counts.reshape(1, N_EXPERTS), n_blocks, blk)
    block_e = block_e.reshape(n_blocks)
    valid = valid.reshape(n_blocks)
    n_used = n_used[0, 0:1]
    dest_t = _dest(idx, rank, pad_start.reshape(N_EXPERTS, 1), tm)
    xs = _dispatch(h2_all, dest_t, n_blocks * blk)
    ys = _experts(xs, block_e, valid, n_used, w["w_eg"], w["w_eu"], w["w_ed"], blk)
    return ys, dest_t, jnp.transpose(wts_t)


def _prep(raw):
    p = {k: v[0] for k, v in raw.items()}
    w_in = p["w_in"]
    o_fox = RWKV_COLS
    o_fl = o_fox + FOX_MAIN_COLS
    o_gate = o_fl + N_HEADS
    row = lambda a: a.reshape(1, -1)
    return dict(
        w_ada=p["w_ada"], b_ada=p["b_ada"],
        g1=row(p["norm1_g"]), g2=row(p["norm2_g"]),
        wr=w_in[:, :o_fox].astype(BF16),
        wf=w_in[:, o_fox:o_fl].astype(BF16),
        wfl=jnp.pad(w_in[:, o_fl:o_gate], ((0, 0), (0, LANES - N_HEADS))).astype(BF16),
        wg=w_in[:, o_gate:].astype(BF16),
        qn=row(jnp.tile(p["fox_q_norm"], N_HEADS)), kn=row(jnp.tile(p["fox_k_norm"], N_HEADS)),
        fb=row(p["fox_f_bias"]),
        gmat=_group_ones(),
        rwkv=dict(mu=row(p["rwkv_mu"]), w0=row(p["rwkv_w0"]), wb=p["rwkv_w_lora_b"], a0=row(p["rwkv_a0"]),
                  ab=p["rwkv_a_lora_b"], gb=p["rwkv_g_lora_b"], kk=row(p["rwkv_k_k"]), ka=row(p["rwkv_k_a"]),
                  rk=row(p["rwkv_r_k"]), lnw=row(p["rwkv_ln_w"]), lnb=row(p["rwkv_ln_b"])),
        w_oa=p["w_out_rwkv"].astype(BF16), w_ob=p["w_out_fox"].astype(BF16), w_o=p["w_out"].astype(BF16),
        wr_hi=p["w_router"].T.astype(BF16),
        wr_lo=(p["w_router"] - p["w_router"].astype(BF16).astype(F32)).T.astype(BF16),
        router_bias=p["router_bias"].reshape(N_EXPERTS, 1),
        w_eg=p["w_exp_gate"], w_eu=p["w_exp_up"], w_ed=p["w_exp_down"],
        w_sg=p["w_sh_gate"].astype(BF16), w_su=p["w_sh_up"].astype(BF16), w_sd=p["w_sh_down"].astype(BF16),
    )


def _token_blocks(b, t):
    if t >= 256:
        return 1, 256
    bb = max(1, min(b, 256 // t))
    while b % bb:
        bb -= 1
    return bb, t


def _mix_path(x, mod, shift0, wkv0, past_k, past_v, past_logf, w):
    b, t, d = x.shape
    bb, tt = _token_blocks(b, t)
    n_past = past_k.shape[1]
    if n_past:
        f_past = _past_cumsum(past_logf)
        init = f_past[:, n_past - 1:n_past, :]
        past = (past_k, past_v, jnp.swapaxes(f_past, 1, 2))
    else:
        init = jnp.zeros((b, 1, N_HEADS), F32)
        past = None
    pr, q, k, v, sg, logf, f_new, gate = _inproj(x, mod.reshape(b, 6, d), w["g1"], w["wr"], w["wf"], w["wfl"],
                                                 w["wg"], w["qn"], w["kn"], w["fb"], w["gmat"], init, bb, tt)
    y_fox = _fox_attention(q, f_new, sg, k, v, jnp.swapaxes(f_new, 1, 2), past=past, tq=min(t, 512),
                           tk_past=min(max(n_past, 1), 512))
    chunk = min(t, RWKV_CHUNK)
    y_rwkv, wkv_new, shift_new = _rwkv(pr, shift0.reshape(b, 1, RWKV_COLS), wkv0, w["rwkv"], w["gmat"],
                                       chunk, max(1, min(RWKV_CHUNKS_PER_STEP, t // chunk)))
    return y_rwkv, y_fox, gate, wkv_new, shift_new, k, v, logf


def _layer(paths, w):
    n_b = [p[0].shape[0] for p in paths]
    mod_all = _ada(jnp.concatenate([p[1] for p in paths], axis=0), w["w_ada"], w["b_ada"])
    mods, o = [], 0
    for nb in n_b:
        mods.append(mod_all[o:o + nb])
        o += nb
    n_total = sum(p[0].shape[0] * p[0].shape[1] for p in paths)
    mixed, x1s = [], []
    shared, row = None, 0
    for (x, _, shift0, wkv0, pk, pv, plf), mod in zip(paths, mods):
        ya, yb, gate, wkv_new, shift_new, k, v, logf = _mix_path(x, mod, shift0, wkv0, pk, pv, plf, w)
        x1, h2_all, lg_all = _merge(x, ya, yb, gate, mod, w, n_total, row, shared)
        shared = (h2_all, lg_all)
        row += x.shape[0] * x.shape[1]
        mixed.append((wkv_new, shift_new, k, v, logf))
        x1s.append(x1)
    ys, dest_t, wts = _moe_routed(h2_all, lg_all, w)
    outs, row = [], 0
    for x1, mod, st in zip(x1s, mods, mixed):
        y = _final(x1, h2_all, wts, dest_t, ys, mod, w, row)
        row += x1.shape[0] * x1.shape[1]
        outs.append((y,) + st)
    return outs


def kernel(x_prompt, x_sample, c_prompt, c_sample, state_rwkv_wkv, state_rwkv_shift, cache_fox_k, cache_fox_v,
           cache_fox_logf, w_ada, b_ada, norm1_g, norm2_g, w_in, rwkv_mu, rwkv_w0, rwkv_w_lora_b, rwkv_a0,
           rwkv_a_lora_b, rwkv_g_lora_b, rwkv_k_k, rwkv_k_a, rwkv_r_k, rwkv_ln_w, rwkv_ln_b, fox_q_norm,
           fox_k_norm, fox_f_bias, w_out_rwkv, w_out_fox, w_out, w_router, router_bias, w_exp_gate, w_exp_up,
           w_exp_down, w_sh_gate, w_sh_up, w_sh_down):
    raw = dict(w_ada=w_ada, b_ada=b_ada, norm1_g=norm1_g, norm2_g=norm2_g, w_in=w_in, rwkv_mu=rwkv_mu,
               rwkv_w0=rwkv_w0, rwkv_w_lora_b=rwkv_w_lora_b, rwkv_a0=rwkv_a0, rwkv_a_lora_b=rwkv_a_lora_b,
               rwkv_g_lora_b=rwkv_g_lora_b, rwkv_k_k=rwkv_k_k, rwkv_k_a=rwkv_k_a, rwkv_r_k=rwkv_r_k,
               rwkv_ln_w=rwkv_ln_w, rwkv_ln_b=rwkv_ln_b, fox_q_norm=fox_q_norm, fox_k_norm=fox_k_norm,
               fox_f_bias=fox_f_bias, w_out_rwkv=w_out_rwkv, w_out_fox=w_out_fox, w_out=w_out,
               w_router=w_router, router_bias=router_bias, w_exp_gate=w_exp_gate, w_exp_up=w_exp_up,
               w_exp_down=w_exp_down, w_sh_gate=w_sh_gate, w_sh_up=w_sh_up, w_sh_down=w_sh_down)
    assert w_in.shape[0] == 1, "single-layer stack"
    w = _prep(raw)
    bp, tp, _ = x_prompt.shape
    bs, ts, _ = x_sample.shape
    n_past = cache_fox_k.shape[2]
    prompt = (x_prompt, c_prompt, jnp.zeros((bp, RWKV_COLS), F32),
              jnp.zeros((bp, N_HEADS, HEAD_DIM, HEAD_DIM), F32),
              jnp.zeros((bp, 0, WIDTH), F32), jnp.zeros((bp, 0, WIDTH), F32), jnp.zeros((bp, 0, N_HEADS), F32))
    sample = (x_sample, c_sample, state_rwkv_shift[0], state_rwkv_wkv[0],
              cache_fox_k[0].reshape(bs, n_past, WIDTH), cache_fox_v[0].reshape(bs, n_past, WIDTH),
              cache_fox_logf[0])
    (yp, wkv_p, sh_p, k_p, v_p, lf_p), (ysm, wkv_s, sh_s, k_s, v_s, lf_s) = _layer([prompt, sample], w)

    def heads(a):
        return a.reshape((1,) + a.shape[:2] + (N_HEADS, HEAD_DIM))

    return (yp, ysm,
            wkv_p[None], sh_p.reshape(1, bp, RWKV_COLS), heads(k_p), heads(v_p), lf_p[None],
            wkv_s[None], sh_s.reshape(1, bs, RWKV_COLS), heads(k_s), heads(v_s), lf_s[None])
```

```python
import functools
import math

import jax
import jax.numpy as jnp
from jax import lax
from jax.experimental import pallas as pl
from jax.experimental.pallas import tpu as pltpu
from jax.experimental.pallas import tpu_sc as plsc

F32 = jnp.float32
BF16 = jnp.bfloat16
I32 = jnp.int32

D_MODEL = 1024
N_HEADS = 8
HEAD_DIM = 64
WIDTH = N_HEADS * HEAD_DIM
HEADS_PER_GROUP = 4
RWKV_CHUNK = 64
RWKV_CHUNKS_PER_STEP = 4
SC_SCATTER_WINDOW = 128
SC_ROW_SPLIT = 2
SC_COL_CHUNKS = 4
DECAY_LORA = 64
ICLR_LORA = 64
GATE_LORA = 128
RWKV_COLS = 3 * WIDTH + DECAY_LORA + ICLR_LORA + GATE_LORA
FOX_MAIN_COLS = 4 * WIDTH
GATE_COLS = 2 * D_MODEL
RWKV_GN_EPS = HEAD_DIM * 1e-5
L2_EPS = 1e-12
RMS_EPS = 1e-6
N_EXPERTS = 256
TOP_K = 8
N_GROUPS = 8
TOPK_GROUPS = 4
EXPERTS_PER_GROUP = N_EXPERTS // N_GROUPS
D_EXPERT = 256
ROUTED_SCALE = 2.5

LANES = 128
VMEM_LIMIT = 56 * 1024 * 1024
NEG_BIG = -1e30

NN = (((1,), (0,)), ((), ()))
NT = (((1,), (1,)), ((), ()))
TN = (((0,), (0,)), ((), ()))


def _cparams(sem):
    return pltpu.CompilerParams(dimension_semantics=sem, vmem_limit_bytes=VMEM_LIMIT)


def _dot(a, b, dims=NN):
    return lax.dot_general(a.astype(BF16), b.astype(BF16), dims, preferred_element_type=F32)


def _split2(a):
    hi = a.astype(BF16)
    lo = (a - hi.astype(F32)).astype(BF16)
    return hi, lo


def _split3(a):
    hi = a.astype(BF16)
    r1 = a - hi.astype(F32)
    mid = r1.astype(BF16)
    lo = (r1 - mid.astype(F32)).astype(BF16)
    return hi, mid, lo


def _dot3(a, b, dims=NN):
    ah, al = _split2(a)
    bh, bl = _split2(b)
    d = functools.partial(lax.dot_general, dimension_numbers=dims, preferred_element_type=F32)
    return d(ah, bh) + (d(ah, bl) + d(al, bh))


def _mm3(a, b, dims):
    d = functools.partial(lax.dot_general, dimension_numbers=dims, preferred_element_type=F32)
    return d(a[0], b[0]) + (d(a[0], b[1]) + d(a[1], b[0]))


def _bd_parts(x, mask):
    out = []
    for part in _split2(x):
        tiled = jnp.concatenate([part] * HEADS_PER_GROUP, axis=0)
        out.append(jnp.where(mask, tiled, jnp.zeros_like(tiled)))
    return tuple(out)


def _dot_exact_rhs(a_exact, b, dims=NN):
    ab = a_exact.astype(BF16)
    bh, bm, bl = _split3(b)
    d = functools.partial(lax.dot_general, dimension_numbers=dims, preferred_element_type=F32)
    return d(ab, bh) + (d(ab, bm) + d(ab, bl))


def _gsum(x, g_ref):
    hi, mid, lo = _split3(x)
    g = g_ref[...]
    d = functools.partial(jnp.dot, preferred_element_type=F32)
    return d(hi, g) + (d(mid, g) + d(lo, g))


def _sigmoid(x):
    return 1.0 / (1.0 + jnp.exp(-x))


def _softplus(x):
    return jnp.maximum(x, 0.0) + jnp.log1p(jnp.exp(-jnp.abs(x)))


def _silu(x):
    return x * _sigmoid(x)


def _pack_bf16_pairs(x):
    w = x.shape[1] // 2
    bits = lax.bitcast_convert_type(x.astype(BF16).astype(F32), I32)
    return lax.shift_right_logical(bits[:, :w], 16) | (bits[:, w:] & -65536)


def _unpack_bf16_pairs(p):
    lo = lax.bitcast_convert_type(lax.shift_left(p, 16), F32)
    hi = lax.bitcast_convert_type(p & -65536, F32)
    return jnp.concatenate([lo, hi], axis=1).astype(BF16)


def _group_ones():
    h = jnp.arange(WIDTH, dtype=I32) // HEAD_DIM
    return (h[:, None] == h[None, :]).astype(BF16)


def _ada_kernel(c_ref, w_ref, b_ref, o_ref):
    o_ref[...] = _dot(_silu(c_ref[...]), w_ref[...]) + b_ref[...]


def _ada(c, w_ada, b_ada):
    nb = c.shape[0]
    n_out = w_ada.shape[1]
    blk = D_MODEL
    return pl.pallas_call(
        _ada_kernel,
        grid=(n_out // blk,),
        in_specs=[pl.BlockSpec((nb, D_MODEL), lambda j: (0, 0)),
                  pl.BlockSpec((D_MODEL, blk), lambda j: (0, j)),
                  pl.BlockSpec((1, blk), lambda j: (0, j))],
        out_specs=pl.BlockSpec((nb, blk), lambda j: (0, j)),
        out_shape=jax.ShapeDtypeStruct((nb, n_out), F32),
        compiler_params=_cparams(("parallel",)),
        name="ada_mod",
    )(c, w_ada, b_ada.reshape(1, n_out))


def _inproj_kernel(x_ref, mod_ref, g1_ref, wr_ref, wf_ref, wfl_ref, wg_ref, qn_ref, kn_ref, fb_ref, gm_ref, f0_ref,
                   pr_ref, q_ref, k_ref, v_ref, sg_ref, lf_ref, cf_ref, gate_ref, carry):
    bb, tt, d = x_ref.shape
    m = bb * tt
    x = x_ref[...]
    ms = jnp.mean(x * x, axis=-1, keepdims=True)
    h = x * lax.rsqrt(ms + RMS_EPS) * g1_ref[...]
    h = h * (1.0 + mod_ref[:, 1:2, :]) + mod_ref[:, 0:1, :]
    hb = h.reshape(m, d).astype(BF16)

    pr_ref[...] = jnp.dot(hb, wr_ref[...], preferred_element_type=F32).reshape(bb, tt, RWKV_COLS)

    f = jnp.dot(hb, wf_ref[...], preferred_element_type=F32)
    q = f[:, 0:WIDTH]
    k = f[:, WIDTH:2 * WIDTH]
    v = f[:, 2 * WIDTH:3 * WIDTH]
    og = f[:, 3 * WIDTH:4 * WIDTH]
    inv_hd = 1.0 / HEAD_DIM
    q = q * lax.rsqrt(_gsum(q * q, gm_ref) * inv_hd + RMS_EPS) * qn_ref[...]
    k = k * lax.rsqrt(_gsum(k * k, gm_ref) * inv_hd + RMS_EPS) * kn_ref[...]
    q_ref[...] = (q * (HEAD_DIM ** -0.5)).astype(BF16).reshape(bb, tt, WIDTH)
    k_ref[...] = k.reshape(bb, tt, WIDTH)
    v_ref[...] = v.reshape(bb, tt, WIDTH)
    sg_ref[...] = _sigmoid(og).reshape(bb, tt, WIDTH)

    fl = jnp.dot(hb, wfl_ref[...], preferred_element_type=F32)[:, 0:N_HEADS] + fb_ref[...]
    lf = -_softplus(-fl)
    lf_ref[...] = lf.reshape(bb, tt, N_HEADS)

    @pl.when(pl.program_id(1) == 0)
    def _():
        carry[...] = f0_ref[...]

    r = lax.broadcasted_iota(I32, (m, m), 0)
    c = lax.broadcasted_iota(I32, (m, m), 1)
    tri = jnp.logical_and(r // tt == c // tt, r >= c).astype(F32)
    cf = _dot_exact_rhs(tri, lf).reshape(bb, tt, N_HEADS) + carry[...]
    cf_ref[...] = cf
    carry[...] = cf[:, tt - 1:tt, :]

    gate_ref[...] =_sigmoid(jnp.dot(hb, wg_ref[...], preferred_element_type=F32)).reshape(bb, tt, GATE_COLS)


def _const_spec(shape):
    nd = len(shape)
    return pl.BlockSpec(shape, lambda *_: (0,) * nd)


def _inproj(x, mod, g1, wr, wf, wfl, wg, qn, kn, fb, gmat, f0, bb, tt):
    b, t, d = x.shape
    grid = (b // bb, t // tt)

    def tok(cols):
        return pl.BlockSpec((bb, tt, cols), lambda i, j: (i, j, 0))

    out_cols = [(RWKV_COLS, F32), (WIDTH, BF16), (WIDTH, F32), (WIDTH, F32), (WIDTH, F32), (N_HEADS, F32),
                (N_HEADS, F32), (GATE_COLS, F32)]
    return pl.pallas_call(
        _inproj_kernel,
        grid=grid,
        in_specs=[tok(d),
                  pl.BlockSpec((bb, 6, d), lambda i, j: (i, 0, 0)),
                  _const_spec((1, d)),
                  _const_spec(wr.shape), _const_spec(wf.shape), _const_spec(wfl.shape), _const_spec(wg.shape),
                  _const_spec((1, WIDTH)), _const_spec((1, WIDTH)), _const_spec((1, N_HEADS)),
                  _const_spec((WIDTH, WIDTH)),
                  pl.BlockSpec((bb, 1, N_HEADS), lambda i, j: (i, 0, 0))],
        out_specs=[tok(c) for c, _ in out_cols],
        out_shape=[jax.ShapeDtypeStruct((b, t, c), dt) for c, dt in out_cols],
        scratch_shapes=[pltpu.VMEM((bb, 1, N_HEADS), F32)],
        compiler_params=_cparams(("parallel", "arbitrary")),
        name="norm1_inproj",
    )(x, mod, g1, wr, wf, wfl, wg, qn, kn, fb, gmat, f0)


def _past_cumsum_kernel(x_ref, o_ref):
    x = x_ref[0]
    rows = x.shape[0]
    li = lax.broadcasted_iota(I32, (LANES, LANES), 0)
    lj = lax.broadcasted_iota(I32, (LANES, LANES), 1)
    same_head = (li % N_HEADS) == (lj % N_HEADS)
    within = jnp.logical_and(same_head, li // N_HEADS <= lj // N_HEADS).astype(BF16)
    xh, xm, xl = _split3(x)
    d2 = functools.partial(jnp.dot, preferred_element_type=F32)
    in_row = d2(xh, within) + (d2(xm, within) + d2(xl, within))
    sh = same_head.astype(BF16)
    row_tot = d2(xh, sh) + (d2(xm, sh) + d2(xl, sh))
    ri = lax.broadcasted_iota(I32, (rows, rows), 0)
    ci = lax.broadcasted_iota(I32, (rows, rows), 1)
    o_ref[0] = in_row + _dot_exact_rhs((ri > ci).astype(F32), row_tot)


def _past_cumsum(past_logf):
    b, p, h = past_logf.shape
    rows = p * h // LANES
    flat = past_logf.reshape(b, rows, LANES)
    out = pl.pallas_call(
        _past_cumsum_kernel,
        grid=(b,),
        in_specs=[pl.BlockSpec((1, rows, LANES), lambda i: (i, 0, 0))],
        out_specs=pl.BlockSpec((1, rows, LANES), lambda i: (i, 0, 0)),
        out_shape=jax.ShapeDtypeStruct((b, rows, LANES), F32),
        compiler_params=_cparams(("parallel",)),
        name="cache_logf_cumsum",
    )(flat)
    return out.reshape(b, p, h)


def _fox_kernel(*refs, n_past_blocks, tq):
    if n_past_blocks:
        (q_ref, fq_ref, sg_ref, kp_ref, vp_ref, fkp_ref, kn_ref, vn_ref, fkn_ref,
         o_ref, m_scr, l_scr, acc_scr) = refs
    else:
        q_ref, fq_ref, sg_ref, kn_ref, vn_ref, fkn_ref, o_ref, m_scr, l_scr, acc_scr = refs
    qi = pl.program_id(1)
    ki = pl.program_id(2)
    nk = pl.num_programs(2)

    @pl.when(ki == 0)
    def _():
        m_scr[...] = jnp.full(m_scr.shape, NEG_BIG, F32)
        l_scr[...] = jnp.zeros(l_scr.shape, F32)
        acc_scr[...] = jnp.zeros(acc_scr.shape, F32)

    lane_a = lax.broadcasted_iota(I32, (tq, LANES), 1) < HEAD_DIM

    def step(k_ref, v_ref, fk_ref, diag):
        tk = k_ref.shape[1]
        if diag:
            rq = lax.broadcasted_iota(I32, (tq, tk), 0)
            ck = lax.broadcasted_iota(I32, (tq, tk), 1)
            visible = ck <= rq
        fq_all = fq_ref[0]
        pairs = range(N_HEADS // 2)
        cols = [slice(j * LANES, (j + 1) * LANES) for j in pairs]
        scores = []
        for j in pairs:
            qj = q_ref[0, :, cols[j]]
            kb = k_ref[0, :, cols[j]].astype(BF16)
            for hh in range(2):
                h = 2 * j + hh
                qm = jnp.where(lane_a if hh == 0 else jnp.logical_not(lane_a), qj, jnp.zeros_like(qj))
                s = lax.dot_general(qm, kb, NT, preferred_element_type=F32)
                s = s + fq_all[:, h:h + 1] - fk_ref[0, h:h + 1, :]
                if diag:
                    s = jnp.where(visible, s, NEG_BIG)
                scores.append(s)
        alphas, probs = [], []
        for h in range(N_HEADS):
            m_old = m_scr[h]
            m_new = jnp.maximum(m_old, jnp.max(scores[h], axis=-1, keepdims=True))
            alpha = jnp.exp(m_old - m_new)
            p = jnp.exp(scores[h] - m_new)
            l_scr[h] = alpha * l_scr[h] + jnp.sum(p, axis=-1, keepdims=True)
            m_scr[h] = m_new
            alphas.append(alpha)
            probs.append(p.astype(BF16))
        for j in pairs:
            vb = v_ref[0, :, cols[j]].astype(BF16)
            pv0 = jnp.dot(probs[2 * j], vb, preferred_element_type=F32)
            pv1 = jnp.dot(probs[2 * j + 1], vb, preferred_element_type=F32)
            acc_scr[:, cols[j]] = (acc_scr[:, cols[j]] * jnp.where(lane_a, alphas[2 * j], alphas[2 * j + 1])
                                   + jnp.where(lane_a, pv0, pv1))

    if n_past_blocks:
        @pl.when(ki < n_past_blocks)
        def _():
            step(kp_ref, vp_ref, fkp_ref, False)

    kn = ki - n_past_blocks

    @pl.when(jnp.logical_and(kn >= 0, kn < qi))
    def _():
        step(kn_ref, vn_ref, fkn_ref, False)

    @pl.when(kn == qi)
    def _():
        step(kn_ref, vn_ref, fkn_ref, True)

    @pl.when(ki == nk - 1)
    def _():
        for j in range(N_HEADS // 2):
            cols = slice(j * LANES, (j + 1) * LANES)
            l = jnp.where(lane_a, l_scr[2 * j], l_scr[2 * j + 1])
            o_ref[0, :, cols] = acc_scr[:, cols] / l * sg_ref[0, :, cols]


def _fox_attention(q, fq, sg, k_new, v_new, fk_new_t, past=None, tq=512, tk_past=512):
    b, t, _ = q.shape
    nq = t // tq
    n_past_blocks = 0 if past is None else past[0].shape[1] // tk_past
    nk = n_past_blocks + nq

    def new_idx(i, qi, ki):
        return jnp.clip(ki - n_past_blocks, 0, qi)

    in_specs = [pl.BlockSpec((1, tq, WIDTH), lambda i, qi, ki: (i, qi, 0)),
                pl.BlockSpec((1, tq, N_HEADS), lambda i, qi, ki: (i, qi, 0)),
                pl.BlockSpec((1, tq, WIDTH), lambda i, qi, ki: (i, qi, 0))]
    args = [q, fq, sg]
    if n_past_blocks:
        def past_idx(i, qi, ki):
            return jnp.minimum(ki, n_past_blocks - 1)
        in_specs += [pl.BlockSpec((1, tk_past, WIDTH), lambda i, qi, ki: (i, past_idx(i, qi, ki), 0)),
                     pl.BlockSpec((1, tk_past, WIDTH), lambda i, qi, ki: (i, past_idx(i, qi, ki), 0)),
                     pl.BlockSpec((1, N_HEADS, tk_past), lambda i, qi, ki: (i, 0, past_idx(i, qi, ki)))]
        args += list(past)
    in_specs += [pl.BlockSpec((1, tq, WIDTH), lambda i, qi, ki: (i, new_idx(i, qi, ki), 0)),
                 pl.BlockSpec((1, tq, WIDTH), lambda i, qi, ki: (i, new_idx(i, qi, ki), 0)),
                 pl.BlockSpec((1, N_HEADS, tq), lambda i, qi, ki: (i, 0, new_idx(i, qi, ki)))]
    args += [k_new, v_new, fk_new_t]
    return pl.pallas_call(
        functools.partial(_fox_kernel, n_past_blocks=n_past_blocks, tq=tq),
        grid=(b, nq, nk),
        in_specs=in_specs,
        out_specs=pl.BlockSpec((1, tq, WIDTH), lambda i, qi, ki: (i, qi, 0)),
        out_shape=jax.ShapeDtypeStruct((b, t, WIDTH), F32),
        scratch_shapes=[pltpu.VMEM((N_HEADS, tq, 1), F32), pltpu.VMEM((N_HEADS, tq, 1), F32),
                        pltpu.VMEM((tq, WIDTH), F32)],
        compiler_params=_cparams(("parallel", "parallel", "arbitrary")),
        name="fox_attention",
    )(*args)


def _rwkv_kernel(p_ref, sh0_ref, s0_ref, mu_ref, w0_ref, wb_ref, a0_ref, ab_ref, gb_ref, kk_ref, ka_ref, rk_ref,
                 lnw_ref, lnb_ref, gm_ref, y_ref, st_ref, sht_ref, z_scr, prev_scr, *, c):
    t = pl.program_id(1)
    nt = pl.num_programs(1)
    n_rows = p_ref.shape[1]
    n_chunks = n_rows // c

    def head_block(h):
        lo = (h % HEADS_PER_GROUP) * HEAD_DIM
        return h // HEADS_PER_GROUP, slice(lo, lo + HEAD_DIM)

    @pl.when(t == 0)
    def _():
        z_scr[...] = jnp.zeros(z_scr.shape, F32)
        for h in range(N_HEADS):
            i, blk = head_block(h)
            z_scr[i, blk, blk] = s0_ref[0, h]
        prev_scr[...] = sh0_ref[0]

    p = p_ref[0]
    row = lax.broadcasted_iota(I32, p.shape, 0)
    prev = jnp.where(row == 0, prev_scr[...], pltpu.roll(p, 1, 0))
    last = p[n_rows - 1:n_rows, :]
    prev_scr[...] = last
    sht_ref[0] = last

    pm = p + (prev - p) * mu_ref[...]
    r = pm[:, 0:WIDTH]
    k = pm[:, WIDTH:2 * WIDTH]
    v = pm[:, 2 * WIDTH:3 * WIDTH]
    o1 = 3 * WIDTH
    wd = pm[:, o1:o1 + DECAY_LORA]
    ad = pm[:, o1 + DECAY_LORA:o1 + DECAY_LORA + ICLR_LORA]
    gd = pm[:, o1 + DECAY_LORA + ICLR_LORA:RWKV_COLS]

    w = -_softplus(-(w0_ref[...] + _dot(jnp.tanh(wd), wb_ref[...]))) - 0.5
    lw = -jnp.exp(w)
    a = _sigmoid(a0_ref[...] + _dot(ad, ab_ref[...]))
    g = _dot(_sigmoid(gd), gb_ref[...])
    kk = k * kk_ref[...]
    kk = kk / jnp.maximum(jnp.sqrt(_gsum(kk * kk, gm_ref)), L2_EPS)
    kf = k * (1.0 + (a - 1.0) * ka_ref[...])

    ri = lax.broadcasted_iota(I32, (n_rows, n_rows), 0)
    ci = lax.broadcasted_iota(I32, (n_rows, n_rows), 1)
    same_chunk = (ri // c) == (ci // c)
    cum = _dot_exact_rhs(jnp.logical_and(same_chunk, ri >= ci).astype(F32), lw)
    cum_last = _dot_exact_rhs(same_chunk.astype(F32), lw)
    r_t = r * jnp.exp(cum)
    a_t = -kk * jnp.exp(cum - lw)
    inv = jnp.exp(-cum)
    b_t = kk * a * inv
    k_t = kf * inv
    to_end = jnp.exp(cum_last - cum)
    b_e = kk * a * to_end
    k_e = kf * to_end
    g_end = jnp.exp(cum_last)

    hg = HEADS_PER_GROUP
    gw = hg * HEAD_DIM
    log_c = int(math.log2(c))
    t_idx = lax.broadcasted_iota(I32, (c, hg * c), 0)
    s_idx = lax.broadcasted_iota(I32, (c, hg * c), 1) & (c - 1)
    strict = s_idx < t_idx
    lower = s_idx <= t_idx
    eye = (s_idx == t_idx).astype(F32)
    rb = lax.broadcasted_iota(I32, (hg * c, gw), 0) >> log_c
    mask_kv = rb == (lax.broadcasted_iota(I32, (hg * c, gw), 1) >> int(math.log2(HEAD_DIM)))
    rs = lax.broadcasted_iota(I32, (hg * c, hg * c), 0) >> log_c
    mask_ss = rs == (lax.broadcasted_iota(I32, (hg * c, hg * c), 1) >> log_c)
    ng = N_HEADS // hg
    cat = functools.partial(jnp.concatenate, axis=0)
    units = [(slice(j * c, (j + 1) * c), slice(i * gw, (i + 1) * gw)) for j in range(n_chunks) for i in range(ng)]
    nu = len(units)

    ar = [_split2(cat([a_t[rs_, s], r_t[rs_, s]])) for rs_, s in units]
    ab = [_mm3(ar[n], _bd_parts(b_t[units[n]], mask_kv), NT) for n in range(nu)]
    ak = [_mm3(ar[n], _bd_parts(k_t[units[n]], mask_kv), NT) for n in range(nu)]
    l_ab = [jnp.where(strict, m[:c], 0.0) for m in ab]
    l_rb = [jnp.where(lower, m[c:], 0.0) for m in ab]
    l_ak = [jnp.where(strict, m[:c], 0.0) for m in ak]
    l_rk = [jnp.where(lower, m[c:], 0.0) for m in ak]
    def mm1(a, b_bd):
        return jnp.dot(a.astype(BF16), b_bd, preferred_element_type=F32)

    def bd1(x, mask):
        tiled = jnp.concatenate([x.astype(BF16)] * hg, axis=0)
        return jnp.where(mask, tiled, jnp.zeros_like(tiled))

    tinv = [eye + m for m in l_ab]
    pw = [mm1(m, bd1(m, mask_ss)) for m in l_ab]
    for _ in range(1, log_c - 1):
        res = [mm1(cat([tinv[n], pw[n]]), bd1(pw[n], mask_ss)) for n in range(nu)]
        tinv = [tinv[n] + res[n][:c] for n in range(nu)]
        pw = [m[c:] for m in res]
    tinv = [tinv[n] + mm1(tinv[n], bd1(pw[n], mask_ss)) for n in range(nu)]
    av = [_mm3(_split2(cat([l_ak[n], l_rk[n]])), _bd_parts(v[units[n]], mask_kv), NN) for n in range(nu)]
    ue = [_split2(cat([b_e[units[n]], k_e[units[n]]])) for n in range(nu)]

    def wide(fn, x):
        return [fn(x[:, :gw]), fn(x[:, gw:])]

    def bd1w(x):
        return jnp.concatenate(wide(lambda h_: bd1(h_, mask_kv), x), axis=1)

    def bd3w(x):
        parts = wide(lambda h_: _bd_parts(h_, mask_kv), x)
        return tuple(jnp.concatenate([parts[0][q], parts[1][q]], axis=1) for q in range(2))

    rhs = [jnp.concatenate([a_t[units[n]], av[n][:c]], axis=1) for n in range(nu)]
    x0 = [mm1(tinv[n], bd1w(rhs[n])) for n in range(nu)]
    resid = [rhs[n] - (x0[n] - _mm3(_split2(l_ab[n]), bd3w(x0[n]), NN)) for n in range(nu)]
    sol = [x0[n] + mm1(tinv[n], bd1w(resid[n])) for n in range(nu)]
    lift = [_mm3(_split2(l_rb[n]), bd3w(sol[n]), NN) for n in range(nu)]
    lhs_s = [_split2(cat([sol[n][:, :gw], r_t[units[n]] + lift[n][:, :gw]])) for n in range(nu)]
    u_loc = [sol[n][:, gw:] for n in range(nu)]
    o_loc = [av[n][c:] + lift[n][:, gw:] for n in range(nu)]

    zr = lax.broadcasted_iota(I32, (gw, gw), 0) >> int(math.log2(HEAD_DIM))
    zmask = zr == (lax.broadcasted_iota(I32, (gw, gw), 1) >> int(math.log2(HEAD_DIM)))
    z = [z_scr[i] for i in range(ng)]
    o_rows = []
    for j in range(n_chunks):
        o_grp = []
        for i in range(ng):
            n = j * ng + i
            rs_, s = units[n]
            sz = _mm3(lhs_s[n], _split2(z[i]), NT)
            u = sz[:c] + u_loc[n]
            o_grp.append(sz[c:] + o_loc[n])
            upd = _mm3(_split2(cat([u, v[rs_, s]])), ue[n], TN)
            z[i] = z[i] * g_end[j * c:j * c + 1, s] + jnp.where(zmask, upd, 0.0)
        o_rows.append(jnp.concatenate(o_grp, axis=1))
    for i in range(ng):
        z_scr[i] = z[i]

    o = cat(o_rows)
    inv_hd = 1.0 / HEAD_DIM
    dlt = o - _gsum(o, gm_ref) * inv_hd
    var = _gsum(dlt * dlt, gm_ref) * inv_hd
    on = dlt * lax.rsqrt(var + RWKV_GN_EPS) * lnw_ref[...] + lnb_ref[...]
    bonus = _gsum(r * kf * rk_ref[...], gm_ref) * v
    y_ref[0] = (on + bonus) * g

    @pl.when(t == nt - 1)
    def _():
        for h in range(N_HEADS):
            i, blk = head_block(h)
            st_ref[0, h] = z_scr[i, blk, blk]


def _rwkv(p, shift0, s0, prm, gmat, chunk, chunks_per_step):
    b, t, _ = p.shape
    row = lambda n: _const_spec((1, n))
    rows = chunk * chunks_per_step
    return pl.pallas_call(
        functools.partial(_rwkv_kernel, c=chunk),
        grid=(b, t // rows),
        in_specs=[pl.BlockSpec((1, rows, RWKV_COLS), lambda i, j: (i, j, 0)),
                  pl.BlockSpec((1, 1, RWKV_COLS), lambda i, j: (i, 0, 0)),
                  pl.BlockSpec((1, N_HEADS, HEAD_DIM, HEAD_DIM), lambda i, j: (i, 0, 0, 0)),
                  row(RWKV_COLS), row(WIDTH), _const_spec((DECAY_LORA, WIDTH)), row(WIDTH),
                  _const_spec((ICLR_LORA, WIDTH)), _const_spec((GATE_LORA, WIDTH)),
                  row(WIDTH), row(WIDTH), row(WIDTH), row(WIDTH), row(WIDTH), _const_spec((WIDTH, WIDTH))],
        out_specs=[pl.BlockSpec((1, rows, WIDTH), lambda i, j: (i, j, 0)),
                   pl.BlockSpec((1, N_HEADS, HEAD_DIM, HEAD_DIM), lambda i, j: (i, 0, 0, 0)),
                   pl.BlockSpec((1, 1, RWKV_COLS), lambda i, j: (i, 0, 0))],
        out_shape=[jax.ShapeDtypeStruct((b, t, WIDTH), F32),
                   jax.ShapeDtypeStruct((b, N_HEADS, HEAD_DIM, HEAD_DIM), F32),
                   jax.ShapeDtypeStruct((b, 1, RWKV_COLS), F32)],
        scratch_shapes=[pltpu.VMEM((N_HEADS // HEADS_PER_GROUP, HEADS_PER_GROUP * HEAD_DIM,
                                    HEADS_PER_GROUP * HEAD_DIM), F32),
                        pltpu.VMEM((1, RWKV_COLS), F32)],
        compiler_params=_cparams(("parallel", "arbitrary")),
        name="rwkv7_mix",
    )(p, shift0, s0, prm["mu"], prm["w0"], prm["wb"], prm["a0"], prm["ab"], prm["gb"], prm["kk"], prm["ka"],
      prm["rk"], prm["lnw"], prm["lnb"], gmat)


def _merge_kernel(x_ref, ya_ref, yb_ref, gate_ref, mod_ref, g2_ref, woa_ref, wob_ref, wo_ref, wrh_ref, wrl_ref,
                  *rest):
    x1_ref, h2_ref, lg_ref = rest[-3:]
    bb, tt, d = x_ref.shape
    m = bb * tt
    gate = gate_ref[...].reshape(m, GATE_COLS)
    merged = (gate[:, 0:d] * _dot(ya_ref[...].reshape(m, WIDTH), woa_ref[...])
              + gate[:, d:2 * d] * _dot(yb_ref[...].reshape(m, WIDTH), wob_ref[...]))
    x1 = x_ref[...] + mod_ref[:, 2:3, :] * _dot(merged, wo_ref[...]).reshape(bb, tt, d)
    x1_ref[...] = x1
    ms = jnp.mean(x1 * x1, axis=-1, keepdims=True)
    h2 = x1 * lax.rsqrt(ms + RMS_EPS) * g2_ref[...]
    h2 = (h2 * (1.0 + mod_ref[:, 4:5, :]) + mod_ref[:, 3:4, :]).reshape(m, d)
    h2_ref[...] = _pack_bf16_pairs(h2)
    lg_ref[...] = _mm3((wrh_ref[...], wrl_ref[...]), _split2(h2), NT)


def _merge(x, ya, yb, gate, mod, w, n_total, row_offset, shared=None):
    b, t, d = x.shape
    bb, tt = _token_blocks(b, t)
    nt = t // tt
    m = bb * tt
    off = row_offset // m

    def tok(cols):
        return pl.BlockSpec((bb, tt, cols), lambda i, j: (i, j, 0))

    in_specs = [tok(d), tok(WIDTH), tok(WIDTH), tok(GATE_COLS),
                pl.BlockSpec((bb, 6, d), lambda i, j: (i, 0, 0)),
                _const_spec((1, d)), _const_spec((WIDTH, d)), _const_spec((WIDTH, d)), _const_spec((d, d)),
                _const_spec((N_EXPERTS, d)), _const_spec((N_EXPERTS, d))]
    args = [x, ya, yb, gate, mod.reshape(b, 6, d), w["g2"], w["w_oa"], w["w_ob"], w["w_o"], w["wr_hi"], w["wr_lo"]]
    aliases = {}
    if shared is not None:
        aliases = {len(args): 1, len(args) + 1: 2}
        in_specs += [pl.BlockSpec(memory_space=pl.ANY), pl.BlockSpec(memory_space=pl.ANY)]
        args += list(shared)
    return pl.pallas_call(
        _merge_kernel,
        grid=(b // bb, nt),
        in_specs=in_specs,
        out_specs=[tok(d), pl.BlockSpec((m, d // 2), lambda i, j: (off + i * nt + j, 0)),
                   pl.BlockSpec((N_EXPERTS, m), lambda i, j: (0, off + i * nt + j))],
        out_shape=[jax.ShapeDtypeStruct((b, t, d), F32), jax.ShapeDtypeStruct((n_total, d // 2), I32),
                   jax.ShapeDtypeStruct((N_EXPERTS, n_total), F32)],
        input_output_aliases=aliases,
        compiler_params=_cparams(("parallel", "parallel")),
        name="merge_norm2_router",
    )(*args)


def _route_kernel(lg_ref, bias_ref, idx_ref, wt_ref, rank_ref, cnt_ref, carry):
    @pl.when(pl.program_id(0) == 0)
    def _():
        carry[...] = jnp.zeros(carry.shape, F32)

    tm = lg_ref.shape[1]
    scores = _sigmoid(lg_ref[...])
    sel = scores + bias_ref[...]
    row = lax.broadcasted_iota(I32, (N_EXPERTS, tm), 0)
    neg_inf = -jnp.inf

    def first_argmax(vals, rows):
        mx = jnp.max(vals, axis=0, keepdims=True)
        return mx, jnp.min(jnp.where(vals == mx, rows, N_EXPERTS), axis=0, keepdims=True)

    gslices = [slice(g * EXPERTS_PER_GROUP, (g + 1) * EXPERTS_PER_GROUP) for g in range(N_GROUPS)]
    gs = []
    row_g = lax.broadcasted_iota(I32, (EXPERTS_PER_GROUP, tm), 0)
    for sl in gslices:
        m1, i1 = first_argmax(sel[sl], row_g)
        m2 = jnp.max(jnp.where(row_g == i1, neg_inf, sel[sl]), axis=0, keepdims=True)
        gs.append(m1 + m2)
    kept = []
    for g in range(N_GROUPS):
        beaten = jnp.zeros((1, tm), I32)
        for o in range(N_GROUPS):
            if o != g:
                wins = (gs[o] >= gs[g]) if o < g else (gs[o] > gs[g])
                beaten = beaten + wins.astype(I32)
        kept.append(jnp.where(beaten < TOPK_GROUPS, sel[gslices[g]], neg_inf))
    cur = jnp.concatenate(kept, axis=0)

    idxs, ws = [], []
    picked = jnp.zeros((N_EXPERTS, tm), F32)
    for _ in range(TOP_K):
        _, ik = first_argmax(cur, row)
        hit = row == ik
        idxs.append(ik)
        ws.append(jnp.sum(jnp.where(hit, scores, 0.0), axis=0, keepdims=True))
        cur = jnp.where(hit, neg_inf, cur)
        picked = jnp.where(hit, 1.0, picked)
    wsum = ws[0]
    for k in range(1, TOP_K):
        wsum = wsum + ws[k]

    r = lax.broadcasted_iota(I32, (tm, tm), 0)
    c = lax.broadcasted_iota(I32, (tm, tm), 1)
    before = jnp.dot(picked.astype(BF16), (r < c).astype(BF16), preferred_element_type=F32) + carry[...]
    carry[...] = carry[...] + jnp.sum(picked, axis=1, keepdims=True)
    cnt_ref[...] = carry[...]

    kk = lax.broadcasted_iota(I32, (TOP_K, tm), 0)
    idx_o = jnp.zeros((TOP_K, tm), I32)
    wt_o = jnp.zeros((TOP_K, tm), F32)
    rank_o = jnp.zeros((TOP_K, tm), F32)
    for k in range(TOP_K):
        rk = jnp.sum(jnp.where(row == idxs[k], before, 0.0), axis=0, keepdims=True)
        idx_o = jnp.where(kk == k, idxs[k], idx_o)
        wt_o = jnp.where(kk == k, ws[k] / wsum * ROUTED_SCALE, wt_o)
        rank_o = jnp.where(kk == k, rk, rank_o)
    idx_ref[...] = idx_o
    wt_ref[...] = wt_o
    rank_ref[...] = rank_o.astype(I32)


def _route(logits_t, bias_col, tm):
    n = logits_t.shape[1]
    tokk = pl.BlockSpec((TOP_K, tm), lambda i: (0, i))
    return pl.pallas_call(
        _route_kernel,
        grid=(n // tm,),
        in_specs=[pl.BlockSpec((N_EXPERTS, tm), lambda i: (0, i)), _const_spec((N_EXPERTS, 1))],
        out_specs=[tokk, tokk, tokk, _const_spec((N_EXPERTS, 1))],
        out_shape=[jax.ShapeDtypeStruct((TOP_K, n), I32), jax.ShapeDtypeStruct((TOP_K, n), F32),
                   jax.ShapeDtypeStruct((TOP_K, n), I32), jax.ShapeDtypeStruct((N_EXPERTS, 1), F32)],
        scratch_shapes=[pltpu.VMEM((N_EXPERTS, 1), F32)],
        compiler_params=_cparams(("arbitrary",)),
        name="route_topk",
    )(logits_t, bias_col)


def _plan_kernel(cnt_ref, start_ref, be_ref, valid_ref, nu_ref, *, blk):
    cnt = cnt_ref[...]
    padded = jnp.ceil(cnt * (1.0 / blk)) * blk
    e_r = lax.broadcasted_iota(I32, (N_EXPERTS, N_EXPERTS), 0)
    e_c = lax.broadcasted_iota(I32, (N_EXPERTS, N_EXPERTS), 1)
    incl = (e_r <= e_c).astype(BF16)
    ph, pm, plo = _split3(jnp.broadcast_to(padded, (8, N_EXPERTS)))
    d2 = functools.partial(jnp.dot, preferred_element_type=F32)
    pad_end = (d2(ph, incl) + (d2(pm, incl) + d2(plo, incl)))[0:1, :]
    pad_start = pad_end - padded
    start_ref[...] = pad_start.astype(I32)
    total = jnp.max(pad_end, axis=-1, keepdims=True)
    nu_ref[...] = jnp.broadcast_to(total * (1.0 / blk), (1, N_EXPERTS)).astype(I32)
    nb = be_ref.shape[0]
    first = (lax.broadcasted_iota(I32, (nb, N_EXPERTS), 0) * blk).astype(F32)
    lane = lax.broadcasted_iota(I32, (nb, N_EXPERTS), 1)
    inside = jnp.logical_and(pad_start <= first, first < pad_end)
    be_ref[...] = jnp.sum(jnp.where(inside, lane, 0), axis=-1, keepdims=True)
    rows = jnp.minimum(pad_start + cnt - first, float(blk))
    valid_ref[...] = jnp.sum(jnp.where(inside, rows, 0.0), axis=-1, keepdims=True).astype(I32)


def _plan(counts, n_blocks, blk):
    return pl.pallas_call(
        functools.partial(_plan_kernel, blk=blk),
        out_shape=[jax.ShapeDtypeStruct((1, N_EXPERTS), I32), jax.ShapeDtypeStruct((n_blocks, 1), I32),
                   jax.ShapeDtypeStruct((n_blocks, 1), I32), jax.ShapeDtypeStruct((1, N_EXPERTS), I32)],
        compiler_params=pltpu.CompilerParams(vmem_limit_bytes=VMEM_LIMIT),
        name="dispatch_plan",
    )(counts)


def _dest_kernel(idx_ref, rank_ref, start_ref, dest_ref):
    tm = idx_ref.shape[1]
    row = lax.broadcasted_iota(I32, (N_EXPERTS, tm), 0)
    kk = lax.broadcasted_iota(I32, (TOP_K, tm), 0)
    idx = idx_ref[...]
    base = jnp.zeros((TOP_K, tm), I32)
    for k in range(TOP_K):
        bk = jnp.sum(jnp.where(row == idx[k:k + 1, :], start_ref[...], 0), axis=0, keepdims=True)
        base = jnp.where(kk == k, bk, base)
    dest_ref[...] = base + rank_ref[...]


def _dest(idx, rank, pad_start_col, tm):
    n = idx.shape[1]
    tokk = pl.BlockSpec((TOP_K, tm), lambda i: (0, i))
    return pl.pallas_call(
        _dest_kernel,
        grid=(n // tm,),
        in_specs=[tokk, tokk, _const_spec((N_EXPERTS, 1))],
        out_specs=tokk,
        out_shape=jax.ShapeDtypeStruct((TOP_K, n), I32),
        compiler_params=_cparams(("parallel",)),
        name="dispatch_dest",
    )(idx, rank, pad_start_col)


def _dispatch(h2p, dest_t, n_slots):
    n, wp = h2p.shape
    half = wp // SC_ROW_SPLIT
    window = SC_SCATTER_WINDOW
    mesh = plsc.VectorSubcoreMesh(core_axis_name="core", subcore_axis_name="subcore")
    out = jax.ShapeDtypeStruct((n_slots, half), h2p.dtype)

    @functools.partial(pl.kernel, out_type=[out] * SC_ROW_SPLIT, mesh=mesh, scratch_types=[])
    def scatter(rows_hbm, idx_hbm, *outs):
        for c, out_hbm in enumerate(outs):
            def body(rows_vmem, idx_vmem, out_hbm=out_hbm):
                for k in range(TOP_K):
                    pltpu.sync_copy(rows_vmem, out_hbm.at[idx_vmem.at[k]])

            pltpu.emit_pipeline(
                body, grid=(n // window,),
                in_specs=[pl.BlockSpec((window, half), index_map=lambda i, c=c: (i, c)),
                          pl.BlockSpec((TOP_K, window), index_map=lambda i: (0, i))],
                out_specs=[], core_axis_name=("core", "subcore"), dimension_semantics=(pltpu.PARALLEL,),
            )(rows_hbm, idx_hbm)

    return scatter(h2p, dest_t)


def _expert_kernel(be_ref, valid_ref, nu_ref, xa_ref, xb_ref, wg_ref, wu_ref, wd_ref, *rest):
    y_refs, (wg_b, wu_b, wd_b) = rest[:SC_COL_CHUNKS], rest[SC_COL_CHUNKS:]
    i = pl.program_id(0)
    nv = valid_ref[i]
    new_expert = jnp.logical_or(i == 0, be_ref[i] != be_ref[jnp.maximum(i - 1, 0)])

    @pl.when(jnp.logical_and(nv > 0, new_expert))
    def _():
        wg_b[...] = wg_ref[0].astype(BF16)
        wu_b[...] = wu_ref[0].astype(BF16)
        wd_b[...] = wd_ref[0].astype(BF16)

    @pl.when(nv > 0)
    def _():
        blk = xa_ref.shape[0]
        rows = lax.broadcasted_iota(I32, (blk, 1), 0)
        packed = jnp.concatenate([xa_ref[...], xb_ref[...]], axis=1)
        x = _unpack_bf16_pairs(jnp.where(rows < nv, packed, 0))
        hg = jnp.dot(x, wg_b[...], preferred_element_type=F32)
        hu = jnp.dot(x, wu_b[...], preferred_element_type=F32)
        y = jnp.dot((_silu(hg) * hu).astype(BF16), wd_b[...], preferred_element_type=F32)
        cw = y.shape[1] // SC_COL_CHUNKS
        for c, y_ref in enumerate(y_refs):
            y_ref[...] = y[:, c * cw:(c + 1) * cw]


def _experts(xs, block_e, valid, n_used, w_eg, w_eu, w_ed, blk):
    xa, xb = xs
    n_slots, packed = xa.shape
    d = w_eg.shape[1]
    n_blocks = n_slots // blk

    def row_blk(i, be, valid, nu):
        return (jnp.minimum(i, nu[0] - 1), 0)

    def w_blk(i, be, valid, nu):
        return (be[i], 0, 0)

    return pl.pallas_call(
        _expert_kernel,
        grid_spec=pltpu.PrefetchScalarGridSpec(
            num_scalar_prefetch=3,
            grid=(n_blocks,),
            in_specs=[pl.BlockSpec((blk, packed), row_blk), pl.BlockSpec((blk, packed), row_blk),
                      pl.BlockSpec((1, d, D_EXPERT), w_blk), pl.BlockSpec((1, d, D_EXPERT), w_blk),
                      pl.BlockSpec((1, D_EXPERT, d), w_blk)],
            out_specs=[pl.BlockSpec((blk, d // SC_COL_CHUNKS), row_blk)] * SC_COL_CHUNKS,
            scratch_shapes=[pltpu.VMEM((d, D_EXPERT), BF16), pltpu.VMEM((d, D_EXPERT), BF16),
                            pltpu.VMEM((D_EXPERT, d), BF16)]),
        out_shape=[jax.ShapeDtypeStruct((n_slots, d // SC_COL_CHUNKS), F32)] * SC_COL_CHUNKS,
        compiler_params=_cparams(("arbitrary",)),
        name="moe_experts",
    )(block_e, valid, n_used, xa, xb, w_eg, w_eu, w_ed)


def _combine_gather(ys, dest_t):
    k, n = dest_t.shape
    cw = ys[0].shape[1]
    window = SC_SCATTER_WINDOW
    mesh = plsc.VectorSubcoreMesh(core_axis_name="core", subcore_axis_name="subcore")

    @functools.partial(pl.kernel, out_type=jax.ShapeDtypeStruct((k * n, cw * len(ys)), ys[0].dtype), mesh=mesh,
                       scratch_types=[])
    def gather(*refs):
        y_refs, idx_hbm, out_hbm = refs[:len(ys)], refs[len(ys)], refs[len(ys) + 1]
        for c, y_hbm in enumerate(y_refs):
            def body(idx_vmem, out_vmem, y_hbm=y_hbm):
                pltpu.sync_copy(y_hbm.at[idx_vmem.at[0]], out_vmem)

            pltpu.emit_pipeline(
                body, grid=(k * n // window,),
                in_specs=[pl.BlockSpec((1, window), index_map=lambda i: (0, i))],
                out_specs=[pl.BlockSpec((window, cw), index_map=lambda i, c=c: (i, c))],
                core_axis_name=("core", "subcore"), dimension_semantics=(pltpu.PARALLEL,),
            )(idx_hbm, out_hbm)

    return gather(*ys, dest_t.reshape(1, k * n)).reshape(k, n, cw * len(ys))


def _final_kernel(x1_ref, h2_ref, wt_ref, mod_ref, wsg_ref, wsu_ref, wsd_ref, yg_ref, o_ref):
    bb, tt, d = x1_ref.shape
    hb = _unpack_bf16_pairs(h2_ref[...])
    hg = jnp.dot(hb, wsg_ref[...], preferred_element_type=F32)
    hu = jnp.dot(hb, wsu_ref[...], preferred_element_type=F32)
    ffn = _dot(_silu(hg) * hu, wsd_ref[...])
    wt = wt_ref[...]
    for k in range(TOP_K):
        ffn = ffn + wt[:, k:k + 1] * yg_ref[k]
    o_ref[...] = x1_ref[...] + mod_ref[:, 5:6, :] * ffn.reshape(bb, tt, d)


def _final(x1, h2_all, wts_all, y_tok, mod, w, row_offset):
    b, t, d = x1.shape
    bb, tt = _token_blocks(b, t)
    nt = t // tt
    m = bb * tt
    off = row_offset // m

    def flat_idx(i, j):
        return off + i * nt + j

    return pl.pallas_call(
        _final_kernel,
        grid=(b // bb, nt),
        in_specs=[pl.BlockSpec((bb, tt, d), lambda i, j: (i, j, 0)),
                  pl.BlockSpec((m, h2_all.shape[1]), lambda i, j: (flat_idx(i, j), 0)),
                  pl.BlockSpec((m, TOP_K), lambda i, j: (flat_idx(i, j), 0)),
                  pl.BlockSpec((bb, 6, d), lambda i, j: (i, 0, 0)),
                  _const_spec((d, D_EXPERT)), _const_spec((d, D_EXPERT)), _const_spec((D_EXPERT, d)),
                  pl.BlockSpec((TOP_K, m, d), lambda i, j: (0, flat_idx(i, j), 0))],
        out_specs=pl.BlockSpec((bb, tt, d), lambda i, j: (i, j, 0)),
        out_shape=jax.ShapeDtypeStruct((b, t, d), F32),
        compiler_params=_cparams(("parallel", "parallel")),
        name="moe_combine_final",
    )(x1, h2_all, wts_all, mod.reshape(b, 6, d), w["w_sg"], w["w_su"], w["w_sd"], y_tok)


def _moe_routed(h2_all, logits_all, w, blk=256, tm=256):
    n = h2_all.shape[0]
    n_blocks = (n * TOP_K + N_EXPERTS * (blk - 1)) // blk + 1
    n_blocks = (n_blocks + 7) // 8 * 8
    idx, wts_t, rank, counts = _route(logits_all, w["router_bias"], tm)
    pad_start, block_e, valid, n_used = _plan(counts.reshape(1, N_EXPERTS), n_blocks, blk)
    block_e = block_e.reshape(n_blocks)
    valid = valid.reshape(n_blocks)
    n_used = n_used[0, 0:1]
    dest_t = _dest(idx, rank, pad_start.reshape(N_EXPERTS, 1), tm)
    xs = _dispatch(h2_all, dest_t, n_blocks * blk)
    ys = _experts(xs, block_e, valid, n_used, w["w_eg"], w["w_eu"], w["w_ed"], blk)
    return _combine_gather(ys, dest_t), jnp.transpose(wts_t)


def _prep(raw):
    p = {k: v[0] for k, v in raw.items()}
    w_in = p["w_in"]
    o_fox = RWKV_COLS
    o_fl = o_fox + FOX_MAIN_COLS
    o_gate = o_fl + N_HEADS
    row = lambda a: a.reshape(1, -1)
    return dict(
        w_ada=p["w_ada"], b_ada=p["b_ada"],
        g1=row(p["norm1_g"]), g2=row(p["norm2_g"]),
        wr=w_in[:, :o_fox].astype(BF16),
        wf=w_in[:, o_fox:o_fl].astype(BF16),
        wfl=jnp.pad(w_in[:, o_fl:o_gate], ((0, 0), (0, LANES - N_HEADS))).astype(BF16),
        wg=w_in[:, o_gate:].astype(BF16),
        qn=row(jnp.tile(p["fox_q_norm"], N_HEADS)), kn=row(jnp.tile(p["fox_k_norm"], N_HEADS)),
        fb=row(p["fox_f_bias"]),
        gmat=_group_ones(),
        rwkv=dict(mu=row(p["rwkv_mu"]), w0=row(p["rwkv_w0"]), wb=p["rwkv_w_lora_b"], a0=row(p["rwkv_a0"]),
                  ab=p["rwkv_a_lora_b"], gb=p["rwkv_g_lora_b"], kk=row(p["rwkv_k_k"]), ka=row(p["rwkv_k_a"]),
                  rk=row(p["rwkv_r_k"]), lnw=row(p["rwkv_ln_w"]), lnb=row(p["rwkv_ln_b"])),
        w_oa=p["w_out_rwkv"].astype(BF16), w_ob=p["w_out_fox"].astype(BF16), w_o=p["w_out"].astype(BF16),
        wr_hi=p["w_router"].T.astype(BF16),
        wr_lo=(p["w_router"] - p["w_router"].astype(BF16).astype(F32)).T.astype(BF16),
        router_bias=p["router_bias"].reshape(N_EXPERTS, 1),
        w_eg=p["w_exp_gate"], w_eu=p["w_exp_up"], w_ed=p["w_exp_down"],
        w_sg=p["w_sh_gate"].astype(BF16), w_su=p["w_sh_up"].astype(BF16), w_sd=p["w_sh_down"].astype(BF16),
    )


def _token_blocks(b, t):
    if t >= 256:
        return 1, 256
    bb = max(1, min(b, 256 // t))
    while b % bb:
        bb -= 1
    return bb, t


def _mix_path(x, mod, shift0, wkv0, past_k, past_v, past_logf, w):
    b, t, d = x.shape
    bb, tt = _token_blocks(b, t)
    n_past = past_k.shape[1]
    if n_past:
        f_past = _past_cumsum(past_logf)
        init = f_past[:, n_past - 1:n_past, :]
        past = (past_k, past_v, jnp.swapaxes(f_past, 1, 2))
    else:
        init = jnp.zeros((b, 1, N_HEADS), F32)
        past = None
    pr, q, k, v, sg, logf, f_new, gate = _inproj(x, mod.reshape(b, 6, d), w["g1"], w["wr"], w["wf"], w["wfl"],
                                                 w["wg"], w["qn"], w["kn"], w["fb"], w["gmat"], init, bb, tt)
    y_fox = _fox_attention(q, f_new, sg, k, v, jnp.swapaxes(f_new, 1, 2), past=past, tq=min(t, 512),
                           tk_past=min(max(n_past, 1), 512))
    chunk = min(t, RWKV_CHUNK)
    y_rwkv, wkv_new, shift_new = _rwkv(pr, shift0.reshape(b, 1, RWKV_COLS), wkv0, w["rwkv"], w["gmat"],
                                       chunk, max(1, min(RWKV_CHUNKS_PER_STEP, t // chunk)))
    return y_rwkv, y_fox, gate, wkv_new, shift_new, k, v, logf


def _layer(paths, w):
    n_b = [p[0].shape[0] for p in paths]
    mod_all = _ada(jnp.concatenate([p[1] for p in paths], axis=0), w["w_ada"], w["b_ada"])
    mods, o = [], 0
    for nb in n_b:
        mods.append(mod_all[o:o + nb])
        o += nb
    n_total = sum(p[0].shape[0] * p[0].shape[1] for p in paths)
    mixed, x1s = [], []
    shared, row = None, 0
    for (x, _, shift0, wkv0, pk, pv, plf), mod in zip(paths, mods):
        ya, yb, gate, wkv_new, shift_new, k, v, logf = _mix_path(x, mod, shift0, wkv0, pk, pv, plf, w)
        x1, h2_all, lg_all = _merge(x, ya, yb, gate, mod, w, n_total, row, shared)
        shared = (h2_all, lg_all)
        row += x.shape[0] * x.shape[1]
        mixed.append((wkv_new, shift_new, k, v, logf))
        x1s.append(x1)
    y_tok, wts = _moe_routed(h2_all, lg_all, w)
    outs, row = [], 0
    for x1, mod, st in zip(x1s, mods, mixed):
        y = _final(x1, h2_all, wts, y_tok, mod, w, row)
        row += x1.shape[0] * x1.shape[1]
        outs.append((y,) + st)
    return outs


def kernel(x_prompt, x_sample, c_prompt, c_sample, state_rwkv_wkv, state_rwkv_shift, cache_fox_k, cache_fox_v,
           cache_fox_logf, w_ada, b_ada, norm1_g, norm2_g, w_in, rwkv_mu, rwkv_w0, rwkv_w_lora_b, rwkv_a0,
           rwkv_a_lora_b, rwkv_g_lora_b, rwkv_k_k, rwkv_k_a, rwkv_r_k, rwkv_ln_w, rwkv_ln_b, fox_q_norm,
           fox_k_norm, fox_f_bias, w_out_rwkv, w_out_fox, w_out, w_router, router_bias, w_exp_gate, w_exp_up,
           w_exp_down, w_sh_gate, w_sh_up, w_sh_down):
    raw = dict(w_ada=w_ada, b_ada=b_ada, norm1_g=norm1_g, norm2_g=norm2_g, w_in=w_in, rwkv_mu=rwkv_mu,
               rwkv_w0=rwkv_w0, rwkv_w_lora_b=rwkv_w_lora_b, rwkv_a0=rwkv_a0, rwkv_a_lora_b=rwkv_a_lora_b,
               rwkv_g_lora_b=rwkv_g_lora_b, rwkv_k_k=rwkv_k_k, rwkv_k_a=rwkv_k_a, rwkv_r_k=rwkv_r_k,
               rwkv_ln_w=rwkv_ln_w, rwkv_ln_b=rwkv_ln_b, fox_q_norm=fox_q_norm, fox_k_norm=fox_k_norm,
               fox_f_bias=fox_f_bias, w_out_rwkv=w_out_rwkv, w_out_fox=w_out_fox, w_out=w_out,
               w_router=w_router, router_bias=router_bias, w_exp_gate=w_exp_gate, w_exp_up=w_exp_up,
               w_exp_down=w_exp_down, w_sh_gate=w_sh_gate, w_sh_up=w_sh_up, w_sh_down=w_sh_down)
    assert w_in.shape[0] == 1, "single-layer stack"
    w = _prep(raw)
    bp, tp, _ = x_prompt.shape
    bs, ts, _ = x_sample.shape
    n_past = cache_fox_k.shape[2]
    prompt = (x_prompt, c_prompt, jnp.zeros((bp, RWKV_COLS), F32),
              jnp.zeros((bp, N_HEADS, HEAD_DIM, HEAD_DIM), F32),
              jnp.zeros((bp, 0, WIDTH), F32), jnp.zeros((bp, 0, WIDTH), F32), jnp.zeros((bp, 0, N_HEADS), F32))
    sample = (x_sample, c_sample, state_rwkv_shift[0], state_rwkv_wkv[0],
              cache_fox_k[0].reshape(bs, n_past, WIDTH), cache_fox_v[0].reshape(bs, n_past, WIDTH),
              cache_fox_logf[0])
    (yp, wkv_p, sh_p, k_p, v_p, lf_p), (ysm, wkv_s, sh_s, k_s, v_s, lf_s) = _layer([prompt, sample], w)

    def heads(a):
        return a.reshape((1,) + a.shape[:2] + (N_HEADS, HEAD_DIM))

    return (yp, ysm,
            wkv_p[None], sh_p.reshape(1, bp, RWKV_COLS), heads(k_p), heads(v_p), lf_p[None],
            wkv_s[None], sh_s.reshape(1, bs, RWKV_COLS), heads(k_s), heads(v_s), lf_s[None])
```

```python
import functools
import math

import jax
import jax.numpy as jnp
from jax import lax
from jax.experimental import pallas as pl
from jax.experimental.pallas import tpu as pltpu
from jax.experimental.pallas import tpu_sc as plsc

F32 = jnp.float32
BF16 = jnp.bfloat16
I32 = jnp.int32

D_MODEL = 1024
N_HEADS = 8
HEAD_DIM = 64
WIDTH = N_HEADS * HEAD_DIM
HEADS_PER_GROUP = 4
RWKV_CHUNK = 64
RWKV_CHUNKS_PER_STEP = 4
SC_SCATTER_WINDOW = 128
SC_ROW_SPLIT = 2
DECAY_LORA = 64
ICLR_LORA = 64
GATE_LORA = 128
RWKV_COLS = 3 * WIDTH + DECAY_LORA + ICLR_LORA + GATE_LORA
FOX_MAIN_COLS = 4 * WIDTH
GATE_COLS = 2 * D_MODEL
RWKV_GN_EPS = HEAD_DIM * 1e-5
L2_EPS = 1e-12
RMS_EPS = 1e-6
N_EXPERTS = 256
TOP_K = 8
N_GROUPS = 8
TOPK_GROUPS = 4
EXPERTS_PER_GROUP = N_EXPERTS // N_GROUPS
D_EXPERT = 256
ROUTED_SCALE = 2.5

LANES = 128
VMEM_LIMIT = 56 * 1024 * 1024
NEG_BIG = -1e30

NN = (((1,), (0,)), ((), ()))
NT = (((1,), (1,)), ((), ()))
TN = (((0,), (0,)), ((), ()))


def _cparams(sem):
    return pltpu.CompilerParams(dimension_semantics=sem, vmem_limit_bytes=VMEM_LIMIT)


def _dot(a, b, dims=NN):
    return lax.dot_general(a.astype(BF16), b.astype(BF16), dims, preferred_element_type=F32)


def _split2(a):
    hi = a.astype(BF16)
    lo = (a - hi.astype(F32)).astype(BF16)
    return hi, lo


def _split3(a):
    hi = a.astype(BF16)
    r1 = a - hi.astype(F32)
    mid = r1.astype(BF16)
    lo = (r1 - mid.astype(F32)).astype(BF16)
    return hi, mid, lo


def _dot3(a, b, dims=NN):
    ah, al = _split2(a)
    bh, bl = _split2(b)
    d = functools.partial(lax.dot_general, dimension_numbers=dims, preferred_element_type=F32)
    return d(ah, bh) + (d(ah, bl) + d(al, bh))


def _mm3(a, b, dims):
    d = functools.partial(lax.dot_general, dimension_numbers=dims, preferred_element_type=F32)
    return d(a[0], b[0]) + (d(a[0], b[1]) + d(a[1], b[0]))


def _bd_parts(x, mask):
    out = []
    for part in _split2(x):
        tiled = jnp.concatenate([part] * HEADS_PER_GROUP, axis=0)
        out.append(jnp.where(mask, tiled, jnp.zeros_like(tiled)))
    return tuple(out)


def _dot_exact_rhs(a_exact, b, dims=NN):
    ab = a_exact.astype(BF16)
    bh, bm, bl = _split3(b)
    d = functools.partial(lax.dot_general, dimension_numbers=dims, preferred_element_type=F32)
    return d(ab, bh) + (d(ab, bm) + d(ab, bl))


def _gsum(x, g_ref):
    hi, mid, lo = _split3(x)
    g = g_ref[...]
    d = functools.partial(jnp.dot, preferred_element_type=F32)
    return d(hi, g) + (d(mid, g) + d(lo, g))


def _sigmoid(x):
    return 1.0 / (1.0 + jnp.exp(-x))


def _softplus(x):
    return jnp.maximum(x, 0.0) + jnp.log1p(jnp.exp(-jnp.abs(x)))


def _silu(x):
    return x * _sigmoid(x)


def _pack_bf16_pairs(x):
    w = x.shape[1] // 2
    bits = lax.bitcast_convert_type(x.astype(BF16).astype(F32), I32)
    return lax.shift_right_logical(bits[:, :w], 16) | (bits[:, w:] & -65536)


def _unpack_bf16_pairs(p):
    lo = lax.bitcast_convert_type(lax.shift_left(p, 16), F32)
    hi = lax.bitcast_convert_type(p & -65536, F32)
    return jnp.concatenate([lo, hi], axis=1).astype(BF16)


def _group_ones():
    h = jnp.arange(WIDTH, dtype=I32) // HEAD_DIM
    return (h[:, None] == h[None, :]).astype(BF16)


def _ada_kernel(c_ref, w_ref, b_ref, o_ref):
    o_ref[...] = _dot(_silu(c_ref[...]), w_ref[...]) + b_ref[...]


def _ada(c, w_ada, b_ada):
    nb = c.shape[0]
    n_out = w_ada.shape[1]
    blk = D_MODEL
    return pl.pallas_call(
        _ada_kernel,
        grid=(n_out // blk,),
        in_specs=[pl.BlockSpec((nb, D_MODEL), lambda j: (0, 0)),
                  pl.BlockSpec((D_MODEL, blk), lambda j: (0, j)),
                  pl.BlockSpec((1, blk), lambda j: (0, j))],
        out_specs=pl.BlockSpec((nb, blk), lambda j: (0, j)),
        out_shape=jax.ShapeDtypeStruct((nb, n_out), F32),
        compiler_params=_cparams(("parallel",)),
        name="ada_mod",
    )(c, w_ada, b_ada.reshape(1, n_out))


def _inproj_kernel(x_ref, mod_ref, g1_ref, wr_ref, wf_ref, wfl_ref, wg_ref, qn_ref, kn_ref, fb_ref, gm_ref, f0_ref,
                   pr_ref, q_ref, k_ref, v_ref, sg_ref, lf_ref, cf_ref, gate_ref, carry):
    bb, tt, d = x_ref.shape
    m = bb * tt
    x = x_ref[...]
    ms = jnp.mean(x * x, axis=-1, keepdims=True)
    h = x * lax.rsqrt(ms + RMS_EPS) * g1_ref[...]
    h = h * (1.0 + mod_ref[:, 1:2, :]) + mod_ref[:, 0:1, :]
    hb = h.reshape(m, d).astype(BF16)

    pr_ref[...] = jnp.dot(hb, wr_ref[...], preferred_element_type=F32).reshape(bb, tt, RWKV_COLS)

    f = jnp.dot(hb, wf_ref[...], preferred_element_type=F32)
    q = f[:, 0:WIDTH]
    k = f[:, WIDTH:2 * WIDTH]
    v = f[:, 2 * WIDTH:3 * WIDTH]
    og = f[:, 3 * WIDTH:4 * WIDTH]
    inv_hd = 1.0 / HEAD_DIM
    q = q * lax.rsqrt(_gsum(q * q, gm_ref) * inv_hd + RMS_EPS) * qn_ref[...]
    k = k * lax.rsqrt(_gsum(k * k, gm_ref) * inv_hd + RMS_EPS) * kn_ref[...]
    q_ref[...] = (q * (HEAD_DIM ** -0.5)).astype(BF16).reshape(bb, tt, WIDTH)
    k_ref[...] = k.reshape(bb, tt, WIDTH)
    v_ref[...] = v.reshape(bb, tt, WIDTH)
    sg_ref[...] = _sigmoid(og).reshape(bb, tt, WIDTH)

    fl = jnp.dot(hb, wfl_ref[...], preferred_element_type=F32)[:, 0:N_HEADS] + fb_ref[...]
    lf = -_softplus(-fl)
    lf_ref[...] = lf.reshape(bb, tt, N_HEADS)

    @pl.when(pl.program_id(1) == 0)
    def _():
        carry[...] = f0_ref[...]

    r = lax.broadcasted_iota(I32, (m, m), 0)
    c = lax.broadcasted_iota(I32, (m, m), 1)
    tri = jnp.logical_and(r // tt == c // tt, r >= c).astype(F32)
    cf = _dot_exact_rhs(tri, lf).reshape(bb, tt, N_HEADS) + carry[...]
    cf_ref[...] = cf
    carry[...] = cf[:, tt - 1:tt, :]

    gate_ref[...] =_sigmoid(jnp.dot(hb, wg_ref[...], preferred_element_type=F32)).reshape(bb, tt, GATE_COLS)


def _const_spec(shape):
    nd = len(shape)
    return pl.BlockSpec(shape, lambda *_: (0,) * nd)


def _inproj(x, mod, g1, wr, wf, wfl, wg, qn, kn, fb, gmat, f0, bb, tt):
    b, t, d = x.shape
    grid = (b // bb, t // tt)

    def tok(cols):
        return pl.BlockSpec((bb, tt, cols), lambda i, j: (i, j, 0))

    out_cols = [(RWKV_COLS, F32), (WIDTH, BF16), (WIDTH, F32), (WIDTH, F32), (WIDTH, F32), (N_HEADS, F32),
                (N_HEADS, F32), (GATE_COLS, F32)]
    return pl.pallas_call(
        _inproj_kernel,
        grid=grid,
        in_specs=[tok(d),
                  pl.BlockSpec((bb, 6, d), lambda i, j: (i, 0, 0)),
                  _const_spec((1, d)),
                  _const_spec(wr.shape), _const_spec(wf.shape), _const_spec(wfl.shape), _const_spec(wg.shape),
                  _const_spec((1, WIDTH)), _const_spec((1, WIDTH)), _const_spec((1, N_HEADS)),
                  _const_spec((WIDTH, WIDTH)),
                  pl.BlockSpec((bb, 1, N_HEADS), lambda i, j: (i, 0, 0))],
        out_specs=[tok(c) for c, _ in out_cols],
        out_shape=[jax.ShapeDtypeStruct((b, t, c), dt) for c, dt in out_cols],
        scratch_shapes=[pltpu.VMEM((bb, 1, N_HEADS), F32)],
        compiler_params=_cparams(("parallel", "arbitrary")),
        name="norm1_inproj",
    )(x, mod, g1, wr, wf, wfl, wg, qn, kn, fb, gmat, f0)


def _past_cumsum_kernel(x_ref, o_ref):
    x = x_ref[0]
    rows = x.shape[0]
    li = lax.broadcasted_iota(I32, (LANES, LANES), 0)
    lj = lax.broadcasted_iota(I32, (LANES, LANES), 1)
    same_head = (li % N_HEADS) == (lj % N_HEADS)
    within = jnp.logical_and(same_head, li // N_HEADS <= lj // N_HEADS).astype(BF16)
    xh, xm, xl = _split3(x)
    d2 = functools.partial(jnp.dot, preferred_element_type=F32)
    in_row = d2(xh, within) + (d2(xm, within) + d2(xl, within))
    sh = same_head.astype(BF16)
    row_tot = d2(xh, sh) + (d2(xm, sh) + d2(xl, sh))
    ri = lax.broadcasted_iota(I32, (rows, rows), 0)
    ci = lax.broadcasted_iota(I32, (rows, rows), 1)
    o_ref[0] = in_row + _dot_exact_rhs((ri > ci).astype(F32), row_tot)


def _past_cumsum(past_logf):
    b, p, h = past_logf.shape
    rows = p * h // LANES
    flat = past_logf.reshape(b, rows, LANES)
    out = pl.pallas_call(
        _past_cumsum_kernel,
        grid=(b,),
        in_specs=[pl.BlockSpec((1, rows, LANES), lambda i: (i, 0, 0))],
        out_specs=pl.BlockSpec((1, rows, LANES), lambda i: (i, 0, 0)),
        out_shape=jax.ShapeDtypeStruct((b, rows, LANES), F32),
        compiler_params=_cparams(("parallel",)),
        name="cache_logf_cumsum",
    )(flat)
    return out.reshape(b, p, h)


def _fox_kernel(*refs, n_past_blocks, tq):
    if n_past_blocks:
        (q_ref, fq_ref, sg_ref, kp_ref, vp_ref, fkp_ref, kn_ref, vn_ref, fkn_ref,
         o_ref, m_scr, l_scr, acc_scr) = refs
    else:
        q_ref, fq_ref, sg_ref, kn_ref, vn_ref, fkn_ref, o_ref, m_scr, l_scr, acc_scr = refs
    qi = pl.program_id(1)
    ki = pl.program_id(2)
    nk = pl.num_programs(2)

    @pl.when(ki == 0)
    def _():
        m_scr[...] = jnp.full(m_scr.shape, NEG_BIG, F32)
        l_scr[...] = jnp.zeros(l_scr.shape, F32)
        acc_scr[...] = jnp.zeros(acc_scr.shape, F32)

    lane_a = lax.broadcasted_iota(I32, (tq, LANES), 1) < HEAD_DIM

    def step(k_ref, v_ref, fk_ref, diag):
        tk = k_ref.shape[1]
        if diag:
            rq = lax.broadcasted_iota(I32, (tq, tk), 0)
            ck = lax.broadcasted_iota(I32, (tq, tk), 1)
            visible = ck <= rq
        fq_all = fq_ref[0]
        pairs = range(N_HEADS // 2)
        cols = [slice(j * LANES, (j + 1) * LANES) for j in pairs]
        scores = []
        for j in pairs:
            qj = q_ref[0, :, cols[j]]
            kb = k_ref[0, :, cols[j]].astype(BF16)
            for hh in range(2):
                h = 2 * j + hh
                qm = jnp.where(lane_a if hh == 0 else jnp.logical_not(lane_a), qj, jnp.zeros_like(qj))
                s = lax.dot_general(qm, kb, NT, preferred_element_type=F32)
                s = s + fq_all[:, h:h + 1] - fk_ref[0, h:h + 1, :]
                if diag:
                    s = jnp.where(visible, s, NEG_BIG)
                scores.append(s)
        alphas, probs = [], []
        for h in range(N_HEADS):
            m_old = m_scr[h]
            m_new = jnp.maximum(m_old, jnp.max(scores[h], axis=-1, keepdims=True))
            alpha = jnp.exp(m_old - m_new)
            p = jnp.exp(scores[h] - m_new)
            l_scr[h] = alpha * l_scr[h] + jnp.sum(p, axis=-1, keepdims=True)
            m_scr[h] = m_new
            alphas.append(alpha)
            probs.append(p.astype(BF16))
        for j in pairs:
            vb = v_ref[0, :, cols[j]].astype(BF16)
            pv0 = jnp.dot(probs[2 * j], vb, preferred_element_type=F32)
            pv1 = jnp.dot(probs[2 * j + 1], vb, preferred_element_type=F32)
            acc_scr[:, cols[j]] = (acc_scr[:, cols[j]] * jnp.where(lane_a, alphas[2 * j], alphas[2 * j + 1])
                                   + jnp.where(lane_a, pv0, pv1))

    if n_past_blocks:
        @pl.when(ki < n_past_blocks)
        def _():
            step(kp_ref, vp_ref, fkp_ref, False)

    kn = ki - n_past_blocks

    @pl.when(jnp.logical_and(kn >= 0, kn < qi))
    def _():
        step(kn_ref, vn_ref, fkn_ref, False)

    @pl.when(kn == qi)
    def _():
        step(kn_ref, vn_ref, fkn_ref, True)

    @pl.when(ki == nk - 1)
    def _():
        for j in range(N_HEADS // 2):
            cols = slice(j * LANES, (j + 1) * LANES)
            l = jnp.where(lane_a, l_scr[2 * j], l_scr[2 * j + 1])
            o_ref[0, :, cols] = acc_scr[:, cols] / l * sg_ref[0, :, cols]


def _fox_attention(q, fq, sg, k_new, v_new, fk_new_t, past=None, tq=512, tk_past=512):
    b, t, _ = q.shape
    nq = t // tq
    n_past_blocks = 0 if past is None else past[0].shape[1] // tk_past
    nk = n_past_blocks + nq

    def new_idx(i, qi, ki):
        return jnp.clip(ki - n_past_blocks, 0, qi)

    in_specs = [pl.BlockSpec((1, tq, WIDTH), lambda i, qi, ki: (i, qi, 0)),
                pl.BlockSpec((1, tq, N_HEADS), lambda i, qi, ki: (i, qi, 0)),
                pl.BlockSpec((1, tq, WIDTH), lambda i, qi, ki: (i, qi, 0))]
    args = [q, fq, sg]
    if n_past_blocks:
        def past_idx(i, qi, ki):
            return jnp.minimum(ki, n_past_blocks - 1)
        in_specs += [pl.BlockSpec((1, tk_past, WIDTH), lambda i, qi, ki: (i, past_idx(i, qi, ki), 0)),
                     pl.BlockSpec((1, tk_past, WIDTH), lambda i, qi, ki: (i, past_idx(i, qi, ki), 0)),
                     pl.BlockSpec((1, N_HEADS, tk_past), lambda i, qi, ki: (i, 0, past_idx(i, qi, ki)))]
        args += list(past)
    in_specs += [pl.BlockSpec((1, tq, WIDTH), lambda i, qi, ki: (i, new_idx(i, qi, ki), 0)),
                 pl.BlockSpec((1, tq, WIDTH), lambda i, qi, ki: (i, new_idx(i, qi, ki), 0)),
                 pl.BlockSpec((1, N_HEADS, tq), lambda i, qi, ki: (i, 0, new_idx(i, qi, ki)))]
    args += [k_new, v_new, fk_new_t]
    return pl.pallas_call(
        functools.partial(_fox_kernel, n_past_blocks=n_past_blocks, tq=tq),
        grid=(b, nq, nk),
        in_specs=in_specs,
        out_specs=pl.BlockSpec((1, tq, WIDTH), lambda i, qi, ki: (i, qi, 0)),
        out_shape=jax.ShapeDtypeStruct((b, t, WIDTH), F32),
        scratch_shapes=[pltpu.VMEM((N_HEADS, tq, 1), F32), pltpu.VMEM((N_HEADS, tq, 1), F32),
                        pltpu.VMEM((tq, WIDTH), F32)],
        compiler_params=_cparams(("parallel", "parallel", "arbitrary")),
        name="fox_attention",
    )(*args)


def _rwkv_kernel(p_ref, sh0_ref, s0_ref, mu_ref, w0_ref, wb_ref, a0_ref, ab_ref, gb_ref, kk_ref, ka_ref, rk_ref,
                 lnw_ref, lnb_ref, gm_ref, y_ref, st_ref, sht_ref, z_scr, prev_scr, *, c):
    t = pl.program_id(1)
    nt = pl.num_programs(1)
    n_rows = p_ref.shape[1]
    n_chunks = n_rows // c

    def head_block(h):
        lo = (h % HEADS_PER_GROUP) * HEAD_DIM
        return h // HEADS_PER_GROUP, slice(lo, lo + HEAD_DIM)

    @pl.when(t == 0)
    def _():
        z_scr[...] = jnp.zeros(z_scr.shape, F32)
        for h in range(N_HEADS):
            i, blk = head_block(h)
            z_scr[i, blk, blk] = s0_ref[0, h]
        prev_scr[...] = sh0_ref[0]

    p = p_ref[0]
    row = lax.broadcasted_iota(I32, p.shape, 0)
    prev = jnp.where(row == 0, prev_scr[...], pltpu.roll(p, 1, 0))
    last = p[n_rows - 1:n_rows, :]
    prev_scr[...] = last
    sht_ref[0] = last

    pm = p + (prev - p) * mu_ref[...]
    r = pm[:, 0:WIDTH]
    k = pm[:, WIDTH:2 * WIDTH]
    v = pm[:, 2 * WIDTH:3 * WIDTH]
    o1 = 3 * WIDTH
    wd = pm[:, o1:o1 + DECAY_LORA]
    ad = pm[:, o1 + DECAY_LORA:o1 + DECAY_LORA + ICLR_LORA]
    gd = pm[:, o1 + DECAY_LORA + ICLR_LORA:RWKV_COLS]

    w = -_softplus(-(w0_ref[...] + _dot(jnp.tanh(wd), wb_ref[...]))) - 0.5
    lw = -jnp.exp(w)
    a = _sigmoid(a0_ref[...] + _dot(ad, ab_ref[...]))
    g = _dot(_sigmoid(gd), gb_ref[...])
    kk = k * kk_ref[...]
    kk = kk / jnp.maximum(jnp.sqrt(_gsum(kk * kk, gm_ref)), L2_EPS)
    kf = k * (1.0 + (a - 1.0) * ka_ref[...])

    ri = lax.broadcasted_iota(I32, (n_rows, n_rows), 0)
    ci = lax.broadcasted_iota(I32, (n_rows, n_rows), 1)
    same_chunk = (ri // c) == (ci // c)
    cum = _dot_exact_rhs(jnp.logical_and(same_chunk, ri >= ci).astype(F32), lw)
    cum_last = _dot_exact_rhs(same_chunk.astype(F32), lw)
    r_t = r * jnp.exp(cum)
    a_t = -kk * jnp.exp(cum - lw)
    inv = jnp.exp(-cum)
    b_t = kk * a * inv
    k_t = kf * inv
    to_end = jnp.exp(cum_last - cum)
    b_e = kk * a * to_end
    k_e = kf * to_end
    g_end = jnp.exp(cum_last)

    hg = HEADS_PER_GROUP
    gw = hg * HEAD_DIM
    log_c = int(math.log2(c))
    t_idx = lax.broadcasted_iota(I32, (c, hg * c), 0)
    s_idx = lax.broadcasted_iota(I32, (c, hg * c), 1) & (c - 1)
    strict = s_idx < t_idx
    lower = s_idx <= t_idx
    eye = (s_idx == t_idx).astype(F32)
    rb = lax.broadcasted_iota(I32, (hg * c, gw), 0) >> log_c
    mask_kv = rb == (lax.broadcasted_iota(I32, (hg * c, gw), 1) >> int(math.log2(HEAD_DIM)))
    rs = lax.broadcasted_iota(I32, (hg * c, hg * c), 0) >> log_c
    mask_ss = rs == (lax.broadcasted_iota(I32, (hg * c, hg * c), 1) >> log_c)
    ng = N_HEADS // hg
    cat = functools.partial(jnp.concatenate, axis=0)
    units = [(slice(j * c, (j + 1) * c), slice(i * gw, (i + 1) * gw)) for j in range(n_chunks) for i in range(ng)]
    nu = len(units)

    ar = [_split2(cat([a_t[rs_, s], r_t[rs_, s]])) for rs_, s in units]
    ab = [_mm3(ar[n], _bd_parts(b_t[units[n]], mask_kv), NT) for n in range(nu)]
    ak = [_mm3(ar[n], _bd_parts(k_t[units[n]], mask_kv), NT) for n in range(nu)]
    l_ab = [jnp.where(strict, m[:c], 0.0) for m in ab]
    l_rb = [jnp.where(lower, m[c:], 0.0) for m in ab]
    l_ak = [jnp.where(strict, m[:c], 0.0) for m in ak]
    l_rk = [jnp.where(lower, m[c:], 0.0) for m in ak]
    def mm1(a, b_bd):
        return jnp.dot(a.astype(BF16), b_bd, preferred_element_type=F32)

    def bd1(x, mask):
        tiled = jnp.concatenate([x.astype(BF16)] * hg, axis=0)
        return jnp.where(mask, tiled, jnp.zeros_like(tiled))

    tinv = [eye + m for m in l_ab]
    pw = [mm1(m, bd1(m, mask_ss)) for m in l_ab]
    for _ in range(1, log_c - 1):
        res = [mm1(cat([tinv[n], pw[n]]), bd1(pw[n], mask_ss)) for n in range(nu)]
        tinv = [tinv[n] + res[n][:c] for n in range(nu)]
        pw = [m[c:] for m in res]
    tinv = [tinv[n] + mm1(tinv[n], bd1(pw[n], mask_ss)) for n in range(nu)]
    av = [_mm3(_split2(cat([l_ak[n], l_rk[n]])), _bd_parts(v[units[n]], mask_kv), NN) for n in range(nu)]
    ue = [_split2(cat([b_e[units[n]], k_e[units[n]]])) for n in range(nu)]

    def wide(fn, x):
        return [fn(x[:, :gw]), fn(x[:, gw:])]

    def bd1w(x):
        return jnp.concatenate(wide(lambda h_: bd1(h_, mask_kv), x), axis=1)

    def bd3w(x):
        parts = wide(lambda h_: _bd_parts(h_, mask_kv), x)
        return tuple(jnp.concatenate([parts[0][q], parts[1][q]], axis=1) for q in range(2))

    rhs = [jnp.concatenate([a_t[units[n]], av[n][:c]], axis=1) for n in range(nu)]
    x0 = [mm1(tinv[n], bd1w(rhs[n])) for n in range(nu)]
    resid = [rhs[n] - (x0[n] - _mm3(_split2(l_ab[n]), bd3w(x0[n]), NN)) for n in range(nu)]
    sol = [x0[n] + mm1(tinv[n], bd1w(resid[n])) for n in range(nu)]
    lift = [_mm3(_split2(l_rb[n]), bd3w(sol[n]), NN) for n in range(nu)]
    lhs_s = [_split2(cat([sol[n][:, :gw], r_t[units[n]] + lift[n][:, :gw]])) for n in range(nu)]
    u_loc = [sol[n][:, gw:] for n in range(nu)]
    o_loc = [av[n][c:] + lift[n][:, gw:] for n in range(nu)]

    zr = lax.broadcasted_iota(I32, (gw, gw), 0) >> int(math.log2(HEAD_DIM))
    zmask = zr == (lax.broadcasted_iota(I32, (gw, gw), 1) >> int(math.log2(HEAD_DIM)))
    z = [z_scr[i] for i in range(ng)]
    o_rows = []
    for j in range(n_chunks):
        o_grp = []
        for i in range(ng):
            n = j * ng + i
            rs_, s = units[n]
            sz = _mm3(lhs_s[n], _split2(z[i]), NT)
            u = sz[:c] + u_loc[n]
            o_grp.append(sz[c:] + o_loc[n])
            upd = _mm3(_split2(cat([u, v[rs_, s]])), ue[n], TN)
            z[i] = z[i] * g_end[j * c:j * c + 1, s] + jnp.where(zmask, upd, 0.0)
        o_rows.append(jnp.concatenate(o_grp, axis=1))
    for i in range(ng):
        z_scr[i] = z[i]

    o = cat(o_rows)
    inv_hd = 1.0 / HEAD_DIM
    dlt = o - _gsum(o, gm_ref) * inv_hd
    var = _gsum(dlt * dlt, gm_ref) * inv_hd
    on = dlt * lax.rsqrt(var + RWKV_GN_EPS) * lnw_ref[...] + lnb_ref[...]
    bonus = _gsum(r * kf * rk_ref[...], gm_ref) * v
    y_ref[0] = (on + bonus) * g

    @pl.when(t == nt - 1)
    def _():
        for h in range(N_HEADS):
            i, blk = head_block(h)
            st_ref[0, h] = z_scr[i, blk, blk]


def _rwkv(p, shift0, s0, prm, gmat, chunk, chunks_per_step):
    b, t, _ = p.shape
    row = lambda n: _const_spec((1, n))
    rows = chunk * chunks_per_step
    return pl.pallas_call(
        functools.partial(_rwkv_kernel, c=chunk),
        grid=(b, t // rows),
        in_specs=[pl.BlockSpec((1, rows, RWKV_COLS), lambda i, j: (i, j, 0)),
                  pl.BlockSpec((1, 1, RWKV_COLS), lambda i, j: (i, 0, 0)),
                  pl.BlockSpec((1, N_HEADS, HEAD_DIM, HEAD_DIM), lambda i, j: (i, 0, 0, 0)),
                  row(RWKV_COLS), row(WIDTH), _const_spec((DECAY_LORA, WIDTH)), row(WIDTH),
                  _const_spec((ICLR_LORA, WIDTH)), _const_spec((GATE_LORA, WIDTH)),
                  row(WIDTH), row(WIDTH), row(WIDTH), row(WIDTH), row(WIDTH), _const_spec((WIDTH, WIDTH))],
        out_specs=[pl.BlockSpec((1, rows, WIDTH), lambda i, j: (i, j, 0)),
                   pl.BlockSpec((1, N_HEADS, HEAD_DIM, HEAD_DIM), lambda i, j: (i, 0, 0, 0)),
                   pl.BlockSpec((1, 1, RWKV_COLS), lambda i, j: (i, 0, 0))],
        out_shape=[jax.ShapeDtypeStruct((b, t, WIDTH), F32),
                   jax.ShapeDtypeStruct((b, N_HEADS, HEAD_DIM, HEAD_DIM), F32),
                   jax.ShapeDtypeStruct((b, 1, RWKV_COLS), F32)],
        scratch_shapes=[pltpu.VMEM((N_HEADS // HEADS_PER_GROUP, HEADS_PER_GROUP * HEAD_DIM,
                                    HEADS_PER_GROUP * HEAD_DIM), F32),
                        pltpu.VMEM((1, RWKV_COLS), F32)],
        compiler_params=_cparams(("parallel", "arbitrary")),
        name="rwkv7_mix",
    )(p, shift0, s0, prm["mu"], prm["w0"], prm["wb"], prm["a0"], prm["ab"], prm["gb"], prm["kk"], prm["ka"],
      prm["rk"], prm["lnw"], prm["lnb"], gmat)


def _merge_kernel(x_ref, ya_ref, yb_ref, gate_ref, mod_ref, g2_ref, woa_ref, wob_ref, wo_ref, wrh_ref, wrl_ref,
                  *rest):
    x1_ref, h2_ref, lg_ref = rest[-3:]
    bb, tt, d = x_ref.shape
    m = bb * tt
    gate = gate_ref[...].reshape(m, GATE_COLS)
    merged = (gate[:, 0:d] * _dot(ya_ref[...].reshape(m, WIDTH), woa_ref[...])
              + gate[:, d:2 * d] * _dot(yb_ref[...].reshape(m, WIDTH), wob_ref[...]))
    x1 = x_ref[...] + mod_ref[:, 2:3, :] * _dot(merged, wo_ref[...]).reshape(bb, tt, d)
    x1_ref[...] = x1
    ms = jnp.mean(x1 * x1, axis=-1, keepdims=True)
    h2 = x1 * lax.rsqrt(ms + RMS_EPS) * g2_ref[...]
    h2 = (h2 * (1.0 + mod_ref[:, 4:5, :]) + mod_ref[:, 3:4, :]).reshape(m, d)
    h2_ref[...] = _pack_bf16_pairs(h2)
    lg_ref[...] = _mm3((wrh_ref[...], wrl_ref[...]), _split2(h2), NT)


def _merge(x, ya, yb, gate, mod, w, n_total, row_offset, shared=None):
    b, t, d = x.shape
    bb, tt = _token_blocks(b, t)
    nt = t // tt
    m = bb * tt
    off = row_offset // m

    def tok(cols):
        return pl.BlockSpec((bb, tt, cols), lambda i, j: (i, j, 0))

    in_specs = [tok(d), tok(WIDTH), tok(WIDTH), tok(GATE_COLS),
                pl.BlockSpec((bb, 6, d), lambda i, j: (i, 0, 0)),
                _const_spec((1, d)), _const_spec((WIDTH, d)), _const_spec((WIDTH, d)), _const_spec((d, d)),
                _const_spec((N_EXPERTS, d)), _const_spec((N_EXPERTS, d))]
    args = [x, ya, yb, gate, mod.reshape(b, 6, d), w["g2"], w["w_oa"], w["w_ob"], w["w_o"], w["wr_hi"], w["wr_lo"]]
    aliases = {}
    if shared is not None:
        aliases = {len(args): 1, len(args) + 1: 2}
        in_specs += [pl.BlockSpec(memory_space=pl.ANY), pl.BlockSpec(memory_space=pl.ANY)]
        args += list(shared)
    return pl.pallas_call(
        _merge_kernel,
        grid=(b // bb, nt),
        in_specs=in_specs,
        out_specs=[tok(d), pl.BlockSpec((m, d // 2), lambda i, j: (off + i * nt + j, 0)),
                   pl.BlockSpec((N_EXPERTS, m), lambda i, j: (0, off + i * nt + j))],
        out_shape=[jax.ShapeDtypeStruct((b, t, d), F32), jax.ShapeDtypeStruct((n_total, d // 2), I32),
                   jax.ShapeDtypeStruct((N_EXPERTS, n_total), F32)],
        input_output_aliases=aliases,
        compiler_params=_cparams(("parallel", "parallel")),
        name="merge_norm2_router",
    )(*args)


def _route_kernel(lg_ref, bias_ref, idx_ref, wt_ref, rank_ref, cnt_ref, carry):
    @pl.when(pl.program_id(0) == 0)
    def _():
        carry[...] = jnp.zeros(carry.shape, F32)

    tm = lg_ref.shape[1]
    scores = _sigmoid(lg_ref[...])
    sel = scores + bias_ref[...]
    row = lax.broadcasted_iota(I32, (N_EXPERTS, tm), 0)
    neg_inf = -jnp.inf

    def first_argmax(vals, rows):
        mx = jnp.max(vals, axis=0, keepdims=True)
        return mx, jnp.min(jnp.where(vals == mx, rows, N_EXPERTS), axis=0, keepdims=True)

    gslices = [slice(g * EXPERTS_PER_GROUP, (g + 1) * EXPERTS_PER_GROUP) for g in range(N_GROUPS)]
    gs = []
    row_g = lax.broadcasted_iota(I32, (EXPERTS_PER_GROUP, tm), 0)
    for sl in gslices:
        m1, i1 = first_argmax(sel[sl], row_g)
        m2 = jnp.max(jnp.where(row_g == i1, neg_inf, sel[sl]), axis=0, keepdims=True)
        gs.append(m1 + m2)
    kept = []
    for g in range(N_GROUPS):
        beaten = jnp.zeros((1, tm), I32)
        for o in range(N_GROUPS):
            if o != g:
                wins = (gs[o] >= gs[g]) if o < g else (gs[o] > gs[g])
                beaten = beaten + wins.astype(I32)
        kept.append(jnp.where(beaten < TOPK_GROUPS, sel[gslices[g]], neg_inf))
    cur = jnp.concatenate(kept, axis=0)

    idxs, ws = [], []
    picked = jnp.zeros((N_EXPERTS, tm), F32)
    for _ in range(TOP_K):
        _, ik = first_argmax(cur, row)
        hit = row == ik
        idxs.append(ik)
        ws.append(jnp.sum(jnp.where(hit, scores, 0.0), axis=0, keepdims=True))
        cur = jnp.where(hit, neg_inf, cur)
        picked = jnp.where(hit, 1.0, picked)
    wsum = ws[0]
    for k in range(1, TOP_K):
        wsum = wsum + ws[k]

    r = lax.broadcasted_iota(I32, (tm, tm), 0)
    c = lax.broadcasted_iota(I32, (tm, tm), 1)
    before = jnp.dot(picked.astype(BF16), (r < c).astype(BF16), preferred_element_type=F32) + carry[...]
    carry[...] = carry[...] + jnp.sum(picked, axis=1, keepdims=True)
    cnt_ref[...] = carry[...]

    kk = lax.broadcasted_iota(I32, (TOP_K, tm), 0)
    idx_o = jnp.zeros((TOP_K, tm), I32)
    wt_o = jnp.zeros((TOP_K, tm), F32)
    rank_o = jnp.zeros((TOP_K, tm), F32)
    for k in range(TOP_K):
        rk = jnp.sum(jnp.where(row == idxs[k], before, 0.0), axis=0, keepdims=True)
        idx_o = jnp.where(kk == k, idxs[k], idx_o)
        wt_o = jnp.where(kk == k, ws[k] / wsum * ROUTED_SCALE, wt_o)
        rank_o = jnp.where(kk == k, rk, rank_o)
    idx_ref[...] = idx_o
    wt_ref[...] = wt_o
    rank_ref[...] = rank_o.astype(I32)


def _route(logits_t, bias_col, tm):
    n = logits_t.shape[1]
    tokk = pl.BlockSpec((TOP_K, tm), lambda i: (0, i))
    return pl.pallas_call(
        _route_kernel,
        grid=(n // tm,),
        in_specs=[pl.BlockSpec((N_EXPERTS, tm), lambda i: (0, i)), _const_spec((N_EXPERTS, 1))],
        out_specs=[tokk, tokk, tokk, _const_spec((N_EXPERTS, 1))],
        out_shape=[jax.ShapeDtypeStruct((TOP_K, n), I32), jax.ShapeDtypeStruct((TOP_K, n), F32),
                   jax.ShapeDtypeStruct((TOP_K, n), I32), jax.ShapeDtypeStruct((N_EXPERTS, 1), F32)],
        scratch_shapes=[pltpu.VMEM((N_EXPERTS, 1), F32)],
        compiler_params=_cparams(("arbitrary",)),
        name="route_topk",
    )(logits_t, bias_col)


def _plan_kernel(cnt_ref, start_ref, be_ref, valid_ref, nu_ref, *, blk):
    cnt = cnt_ref[...]
    padded = jnp.ceil(cnt * (1.0 / blk)) * blk
    e_r = lax.broadcasted_iota(I32, (N_EXPERTS, N_EXPERTS), 0)
    e_c = lax.broadcasted_iota(I32, (N_EXPERTS, N_EXPERTS), 1)
    incl = (e_r <= e_c).astype(BF16)
    ph, pm, plo = _split3(jnp.broadcast_to(padded, (8, N_EXPERTS)))
    d2 = functools.partial(jnp.dot, preferred_element_type=F32)
    pad_end = (d2(ph, incl) + (d2(pm, incl) + d2(plo, incl)))[0:1, :]
    pad_start = pad_end - padded
    start_ref[...] = pad_start.astype(I32)
    total = jnp.max(pad_end, axis=-1, keepdims=True)
    nu_ref[...] = jnp.broadcast_to(total * (1.0 / blk), (1, N_EXPERTS)).astype(I32)
    nb = be_ref.shape[0]
    first = (lax.broadcasted_iota(I32, (nb, N_EXPERTS), 0) * blk).astype(F32)
    lane = lax.broadcasted_iota(I32, (nb, N_EXPERTS), 1)
    inside = jnp.logical_and(pad_start <= first, first < pad_end)
    be_ref[...] = jnp.sum(jnp.where(inside, lane, 0), axis=-1, keepdims=True)
    rows = jnp.minimum(pad_start + cnt - first, float(blk))
    valid_ref[...] = jnp.sum(jnp.where(inside, rows, 0.0), axis=-1, keepdims=True).astype(I32)


def _plan(counts, n_blocks, blk):
    return pl.pallas_call(
        functools.partial(_plan_kernel, blk=blk),
        out_shape=[jax.ShapeDtypeStruct((1, N_EXPERTS), I32), jax.ShapeDtypeStruct((n_blocks, 1), I32),
                   jax.ShapeDtypeStruct((n_blocks, 1), I32), jax.ShapeDtypeStruct((1, N_EXPERTS), I32)],
        compiler_params=pltpu.CompilerParams(vmem_limit_bytes=VMEM_LIMIT),
        name="dispatch_plan",
    )(counts)


def _dest_kernel(idx_ref, rank_ref, start_ref, dest_ref):
    tm = idx_ref.shape[1]
    row = lax.broadcasted_iota(I32, (N_EXPERTS, tm), 0)
    kk = lax.broadcasted_iota(I32, (TOP_K, tm), 0)
    idx = idx_ref[...]
    base = jnp.zeros((TOP_K, tm), I32)
    for k in range(TOP_K):
        bk = jnp.sum(jnp.where(row == idx[k:k + 1, :], start_ref[...], 0), axis=0, keepdims=True)
        base = jnp.where(kk == k, bk, base)
    dest_ref[...] = base + rank_ref[...]


def _dest(idx, rank, pad_start_col, tm):
    n = idx.shape[1]
    tokk = pl.BlockSpec((TOP_K, tm), lambda i: (0, i))
    return pl.pallas_call(
        _dest_kernel,
        grid=(n // tm,),
        in_specs=[tokk, tokk, _const_spec((N_EXPERTS, 1))],
        out_specs=tokk,
        out_shape=jax.ShapeDtypeStruct((TOP_K, n), I32),
        compiler_params=_cparams(("parallel",)),
        name="dispatch_dest",
    )(idx, rank, pad_start_col)


def _dispatch(h2p, dest_t, n_slots):
    n, wp = h2p.shape
    half = wp // SC_ROW_SPLIT
    window = SC_SCATTER_WINDOW
    mesh = plsc.VectorSubcoreMesh(core_axis_name="core", subcore_axis_name="subcore")
    out = jax.ShapeDtypeStruct((n_slots, half), h2p.dtype)

    @functools.partial(pl.kernel, out_type=[out] * SC_ROW_SPLIT, mesh=mesh, scratch_types=[])
    def scatter(rows_hbm, idx_hbm, *outs):
        for c, out_hbm in enumerate(outs):
            def body(rows_vmem, idx_vmem, out_hbm=out_hbm):
                for k in range(TOP_K):
                    pltpu.sync_copy(rows_vmem, out_hbm.at[idx_vmem.at[k]])

            pltpu.emit_pipeline(
                body, grid=(n // window,),
                in_specs=[pl.BlockSpec((window, half), index_map=lambda i, c=c: (i, c)),
                          pl.BlockSpec((TOP_K, window), index_map=lambda i: (0, i))],
                out_specs=[], core_axis_name=("core", "subcore"), dimension_semantics=(pltpu.PARALLEL,),
            )(rows_hbm, idx_hbm)

    return scatter(h2p, dest_t)


def _expert_kernel(be_ref, valid_ref, nu_ref, xa_ref, xb_ref, wg_ref, wu_ref, wd_ref, *rest):
    y_refs, (wg_b, wu_b, wd_b) = rest[:SC_ROW_SPLIT], rest[SC_ROW_SPLIT:]
    i = pl.program_id(0)
    nv = valid_ref[i]
    new_expert = jnp.logical_or(i == 0, be_ref[i] != be_ref[jnp.maximum(i - 1, 0)])

    @pl.when(jnp.logical_and(nv > 0, new_expert))
    def _():
        wg_b[...] = wg_ref[0].astype(BF16)
        wu_b[...] = wu_ref[0].astype(BF16)
        wd_b[...] = wd_ref[0].astype(BF16)

    @pl.when(nv > 0)
    def _():
        blk = xa_ref.shape[0]
        rows = lax.broadcasted_iota(I32, (blk, 1), 0)
        packed = jnp.concatenate([xa_ref[...], xb_ref[...]], axis=1)
        x = _unpack_bf16_pairs(jnp.where(rows < nv, packed, 0))
        hg = jnp.dot(x, wg_b[...], preferred_element_type=F32)
        hu = jnp.dot(x, wu_b[...], preferred_element_type=F32)
        y = _pack_bf16_pairs(jnp.dot((_silu(hg) * hu).astype(BF16), wd_b[...], preferred_element_type=F32))
        cw = y.shape[1] // SC_ROW_SPLIT
        for c, y_ref in enumerate(y_refs):
            y_ref[...] = y[:, c * cw:(c + 1) * cw]


def _experts(xs, block_e, valid, n_used, w_eg, w_eu, w_ed, blk):
    xa, xb = xs
    n_slots, packed = xa.shape
    d = w_eg.shape[1]
    n_blocks = n_slots // blk

    def row_blk(i, be, valid, nu):
        return (jnp.minimum(i, nu[0] - 1), 0)

    def w_blk(i, be, valid, nu):
        return (be[i], 0, 0)

    return pl.pallas_call(
        _expert_kernel,
        grid_spec=pltpu.PrefetchScalarGridSpec(
            num_scalar_prefetch=3,
            grid=(n_blocks,),
            in_specs=[pl.BlockSpec((blk, packed), row_blk), pl.BlockSpec((blk, packed), row_blk),
                      pl.BlockSpec((1, d, D_EXPERT), w_blk), pl.BlockSpec((1, d, D_EXPERT), w_blk),
                      pl.BlockSpec((1, D_EXPERT, d), w_blk)],
            out_specs=[pl.BlockSpec((blk, packed), row_blk)] * SC_ROW_SPLIT,
            scratch_shapes=[pltpu.VMEM((d, D_EXPERT), BF16), pltpu.VMEM((d, D_EXPERT), BF16),
                            pltpu.VMEM((D_EXPERT, d), BF16)]),
        out_shape=[jax.ShapeDtypeStruct((n_slots, packed), I32)] * SC_ROW_SPLIT,
        compiler_params=_cparams(("arbitrary",)),
        name="moe_experts",
    )(block_e, valid, n_used, xa, xb, w_eg, w_eu, w_ed)


def _combine_gather(ys, dest_t):
    k, n = dest_t.shape
    cw = ys[0].shape[1]
    window = SC_SCATTER_WINDOW
    mesh = plsc.VectorSubcoreMesh(core_axis_name="core", subcore_axis_name="subcore")

    @functools.partial(pl.kernel, out_type=jax.ShapeDtypeStruct((k * n, cw * len(ys)), ys[0].dtype), mesh=mesh,
                       scratch_types=[])
    def gather(*refs):
        y_refs, idx_hbm, out_hbm = refs[:len(ys)], refs[len(ys)], refs[len(ys) + 1]
        for c, y_hbm in enumerate(y_refs):
            def body(idx_vmem, out_vmem, y_hbm=y_hbm):
                pltpu.sync_copy(y_hbm.at[idx_vmem.at[0]], out_vmem)

            pltpu.emit_pipeline(
                body, grid=(k * n // window,),
                in_specs=[pl.BlockSpec((1, window), index_map=lambda i: (0, i))],
                out_specs=[pl.BlockSpec((window, cw), index_map=lambda i, c=c: (i, c))],
                core_axis_name=("core", "subcore"), dimension_semantics=(pltpu.PARALLEL,),
            )(idx_hbm, out_hbm)

    return gather(*ys, dest_t.reshape(1, k * n)).reshape(k, n, cw * len(ys))


def _final_kernel(x1_ref, h2_ref, wt_ref, mod_ref, wsg_ref, wsu_ref, wsd_ref, yg_ref, o_ref):
    bb, tt, d = x1_ref.shape
    hb = _unpack_bf16_pairs(h2_ref[...])
    hg = jnp.dot(hb, wsg_ref[...], preferred_element_type=F32)
    hu = jnp.dot(hb, wsu_ref[...], preferred_element_type=F32)
    ffn = _dot(_silu(hg) * hu, wsd_ref[...])
    wt = wt_ref[...]
    for k in range(TOP_K):
        ffn = ffn + wt[:, k:k + 1] * _unpack_bf16_pairs(yg_ref[k]).astype(F32)
    o_ref[...] = x1_ref[...] + mod_ref[:, 5:6, :] * ffn.reshape(bb, tt, d)


def _final(x1, h2_all, wts_all, y_tok, mod, w, row_offset):
    b, t, d = x1.shape
    bb, tt = _token_blocks(b, t)
    nt = t // tt
    m = bb * tt
    off = row_offset // m

    def flat_idx(i, j):
        return off + i * nt + j

    return pl.pallas_call(
        _final_kernel,
        grid=(b // bb, nt),
        in_specs=[pl.BlockSpec((bb, tt, d), lambda i, j: (i, j, 0)),
                  pl.BlockSpec((m, h2_all.shape[1]), lambda i, j: (flat_idx(i, j), 0)),
                  pl.BlockSpec((m, TOP_K), lambda i, j: (flat_idx(i, j), 0)),
                  pl.BlockSpec((bb, 6, d), lambda i, j: (i, 0, 0)),
                  _const_spec((d, D_EXPERT)), _const_spec((d, D_EXPERT)), _const_spec((D_EXPERT, d)),
                  pl.BlockSpec((TOP_K, m, y_tok.shape[2]), lambda i, j: (0, flat_idx(i, j), 0))],
        out_specs=pl.BlockSpec((bb, tt, d), lambda i, j: (i, j, 0)),
        out_shape=jax.ShapeDtypeStruct((b, t, d), F32),
        compiler_params=_cparams(("parallel", "parallel")),
        name="moe_combine_final",
    )(x1, h2_all, wts_all, mod.reshape(b, 6, d), w["w_sg"], w["w_su"], w["w_sd"], y_tok)


def _moe_routed(h2_all, logits_all, w, blk=256, tm=256):
    n = h2_all.shape[0]
    n_blocks = (n * TOP_K + N_EXPERTS * (blk - 1)) // blk + 1
    n_blocks = (n_blocks + 7) // 8 * 8
    idx, wts_t, rank, counts = _route(logits_all, w["router_bias"], tm)
    pad_start, block_e, valid, n_used = _plan(counts.reshape(1, N_EXPERTS), n_blocks, blk)
    block_e = block_e.reshape(n_blocks)
    valid = valid.reshape(n_blocks)
    n_used = n_used[0, 0:1]
    dest_t = _dest(idx, rank, pad_start.reshape(N_EXPERTS, 1), tm)
    xs = _dispatch(h2_all, dest_t, n_blocks * blk)
    ys = _experts(xs, block_e, valid, n_used, w["w_eg"], w["w_eu"], w["w_ed"], blk)
    return _combine_gather(ys, dest_t), jnp.transpose(wts_t)


def _prep(raw):
    p = {k: v[0] for k, v in raw.items()}
    w_in = p["w_in"]
    o_fox = RWKV_COLS
    o_fl = o_fox + FOX_MAIN_COLS
    o_gate = o_fl + N_HEADS
    row = lambda a: a.reshape(1, -1)
    return dict(
        w_ada=p["w_ada"], b_ada=p["b_ada"],
        g1=row(p["norm1_g"]), g2=row(p["norm2_g"]),
        wr=w_in[:, :o_fox].astype(BF16),
        wf=w_in[:, o_fox:o_fl].astype(BF16),
        wfl=jnp.pad(w_in[:, o_fl:o_gate], ((0, 0), (0, LANES - N_HEADS))).astype(BF16),
        wg=w_in[:, o_gate:].astype(BF16),
        qn=row(jnp.tile(p["fox_q_norm"], N_HEADS)), kn=row(jnp.tile(p["fox_k_norm"], N_HEADS)),
        fb=row(p["fox_f_bias"]),
        gmat=_group_ones(),
        rwkv=dict(mu=row(p["rwkv_mu"]), w0=row(p["rwkv_w0"]), wb=p["rwkv_w_lora_b"], a0=row(p["rwkv_a0"]),
                  ab=p["rwkv_a_lora_b"], gb=p["rwkv_g_lora_b"], kk=row(p["rwkv_k_k"]), ka=row(p["rwkv_k_a"]),
                  rk=row(p["rwkv_r_k"]), lnw=row(p["rwkv_ln_w"]), lnb=row(p["rwkv_ln_b"])),
        w_oa=p["w_out_rwkv"].astype(BF16), w_ob=p["w_out_fox"].astype(BF16), w_o=p["w_out"].astype(BF16),
        wr_hi=p["w_router"].T.astype(BF16),
        wr_lo=(p["w_router"] - p["w_router"].astype(BF16).astype(F32)).T.astype(BF16),
        router_bias=p["router_bias"].reshape(N_EXPERTS, 1),
        w_eg=p["w_exp_gate"], w_eu=p["w_exp_up"], w_ed=p["w_exp_down"],
        w_sg=p["w_sh_gate"].astype(BF16), w_su=p["w_sh_up"].astype(BF16), w_sd=p["w_sh_down"].astype(BF16),
    )


def _token_blocks(b, t):
    if t >= 256:
        return 1, 256
    bb = max(1, min(b, 256 // t))
    while b % bb:
        bb -= 1
    return bb, t


def _mix_path(x, mod, shift0, wkv0, past_k, past_v, past_logf, w):
    b, t, d = x.shape
    bb, tt = _token_blocks(b, t)
    n_past = past_k.shape[1]
    if n_past:
        f_past = _past_cumsum(past_logf)
        init = f_past[:, n_past - 1:n_past, :]
        past = (past_k, past_v, jnp.swapaxes(f_past, 1, 2))
    else:
        init = jnp.zeros((b, 1, N_HEADS), F32)
        past = None
    pr, q, k, v, sg, logf, f_new, gate = _inproj(x, mod.reshape(b, 6, d), w["g1"], w["wr"], w["wf"], w["wfl"],
                                                 w["wg"], w["qn"], w["kn"], w["fb"], w["gmat"], init, bb, tt)
    y_fox = _fox_attention(q, f_new, sg, k, v, jnp.swapaxes(f_new, 1, 2), past=past, tq=min(t, 512),
                           tk_past=min(max(n_past, 1), 512))
    chunk = min(t, RWKV_CHUNK)
    y_rwkv, wkv_new, shift_new = _rwkv(pr, shift0.reshape(b, 1, RWKV_COLS), wkv0, w["rwkv"], w["gmat"],
                                       chunk, max(1, min(RWKV_CHUNKS_PER_STEP, t // chunk)))
    return y_rwkv, y_fox, gate, wkv_new, shift_new, k, v, logf


def _layer(paths, w):
    n_b = [p[0].shape[0] for p in paths]
    mod_all = _ada(jnp.concatenate([p[1] for p in paths], axis=0), w["w_ada"], w["b_ada"])
    mods, o = [], 0
    for nb in n_b:
        mods.append(mod_all[o:o + nb])
        o += nb
    n_total = sum(p[0].shape[0] * p[0].shape[1] for p in paths)
    mixed, x1s = [], []
    shared, row = None, 0
    for (x, _, shift0, wkv0, pk, pv, plf), mod in zip(paths, mods):
        ya, yb, gate, wkv_new, shift_new, k, v, logf = _mix_path(x, mod, shift0, wkv0, pk, pv, plf, w)
        x1, h2_all, lg_all = _merge(x, ya, yb, gate, mod, w, n_total, row, shared)
        shared = (h2_all, lg_all)
        row += x.shape[0] * x.shape[1]
        mixed.append((wkv_new, shift_new, k, v, logf))
        x1s.append(x1)
    y_tok, wts = _moe_routed(h2_all, lg_all, w)
    outs, row = [], 0
    for x1, mod, st in zip(x1s, mods, mixed):
        y = _final(x1, h2_all, wts, y_tok, mod, w, row)
        row += x1.shape[0] * x1.shape[1]
        outs.append((y,) + st)
    return outs


def kernel(x_prompt, x_sample, c_prompt, c_sample, state_rwkv_wkv, state_rwkv_shift, cache_fox_k, cache_fox_v,
           cache_fox_logf, w_ada, b_ada, norm1_g, norm2_g, w_in, rwkv_mu, rwkv_w0, rwkv_w_lora_b, rwkv_a0,
           rwkv_a_lora_b, rwkv_g_lora_b, rwkv_k_k, rwkv_k_a, rwkv_r_k, rwkv_ln_w, rwkv_ln_b, fox_q_norm,
           fox_k_norm, fox_f_bias, w_out_rwkv, w_out_fox, w_out, w_router, router_bias, w_exp_gate, w_exp_up,
           w_exp_down, w_sh_gate, w_sh_up, w_sh_down):
    raw = dict(w_ada=w_ada, b_ada=b_ada, norm1_g=norm1_g, norm2_g=norm2_g, w_in=w_in, rwkv_mu=rwkv_mu,
               rwkv_w0=rwkv_w0, rwkv_w_lora_b=rwkv_w_lora_b, rwkv_a0=rwkv_a0, rwkv_a_lora_b=rwkv_a_lora_b,
               rwkv_g_lora_b=rwkv_g_lora_b, rwkv_k_k=rwkv_k_k, rwkv_k_a=rwkv_k_a, rwkv_r_k=rwkv_r_k,
               rwkv_ln_w=rwkv_ln_w, rwkv_ln_b=rwkv_ln_b, fox_q_norm=fox_q_norm, fox_k_norm=fox_k_norm,
               fox_f_bias=fox_f_bias, w_out_rwkv=w_out_rwkv, w_out_fox=w_out_fox, w_out=w_out,
               w_router=w_router, router_bias=router_bias, w_exp_gate=w_exp_gate, w_exp_up=w_exp_up,
               w_exp_down=w_exp_down, w_sh_gate=w_sh_gate, w_sh_up=w_sh_up, w_sh_down=w_sh_down)
    assert w_in.shape[0] == 1, "single-layer stack"
    w = _prep(raw)
    bp, tp, _ = x_prompt.shape
    bs, ts, _ = x_sample.shape
    n_past = cache_fox_k.shape[2]
    prompt = (x_prompt, c_prompt, jnp.zeros((bp, RWKV_COLS), F32),
              jnp.zeros((bp, N_HEADS, HEAD_DIM, HEAD_DIM), F32),
              jnp.zeros((bp, 0, WIDTH), F32), jnp.zeros((bp, 0, WIDTH), F32), jnp.zeros((bp, 0, N_HEADS), F32))
    sample = (x_sample, c_sample, state_rwkv_shift[0], state_rwkv_wkv[0],
              cache_fox_k[0].reshape(bs, n_past, WIDTH), cache_fox_v[0].reshape(bs, n_past, WIDTH),
              cache_fox_logf[0])
    (yp, wkv_p, sh_p, k_p, v_p, lf_p), (ysm, wkv_s, sh_s, k_s, v_s, lf_s) = _layer([prompt, sample], w)

    def heads(a):
        return a.reshape((1,) + a.shape[:2] + (N_HEADS, HEAD_DIM))

    return (yp, ysm,
            wkv_p[None], sh_p.reshape(1, bp, RWKV_COLS), heads(k_p), heads(v_p), lf_p[None],
            wkv_s[None], sh_s.reshape(1, bs, RWKV_COLS), heads(k_s), heads(v_s), lf_s[None])
```

```python
import functools
import math

import jax
import jax.numpy as jnp
from jax import lax
from jax.experimental import pallas as pl
from jax.experimental.pallas import tpu as pltpu
from jax.experimental.pallas import tpu_sc as plsc

F32 = jnp.float32
BF16 = jnp.bfloat16
I32 = jnp.int32

D_MODEL = 1024
N_HEADS = 8
HEAD_DIM = 64
WIDTH = N_HEADS * HEAD_DIM
HEADS_PER_GROUP = 4
RWKV_CHUNK = 64
RWKV_CHUNKS_PER_STEP = 4
EXPERT_BLOCK_ROWS = 512
SC_SCATTER_WINDOW = 128
SC_ROW_SPLIT = 2
DECAY_LORA = 64
ICLR_LORA = 64
GATE_LORA = 128
RWKV_COLS = 3 * WIDTH + DECAY_LORA + ICLR_LORA + GATE_LORA
FOX_MAIN_COLS = 4 * WIDTH
GATE_COLS = 2 * D_MODEL
RWKV_GN_EPS = HEAD_DIM * 1e-5
L2_EPS = 1e-12
RMS_EPS = 1e-6
N_EXPERTS = 256
TOP_K = 8
N_GROUPS = 8
TOPK_GROUPS = 4
EXPERTS_PER_GROUP = N_EXPERTS // N_GROUPS
D_EXPERT = 256
ROUTED_SCALE = 2.5

LANES = 128
VMEM_LIMIT = 56 * 1024 * 1024
NEG_BIG = -1e30

NN = (((1,), (0,)), ((), ()))
NT = (((1,), (1,)), ((), ()))
TN = (((0,), (0,)), ((), ()))


def _cparams(sem):
    return pltpu.CompilerParams(dimension_semantics=sem, vmem_limit_bytes=VMEM_LIMIT)


def _dot(a, b, dims=NN):
    return lax.dot_general(a.astype(BF16), b.astype(BF16), dims, preferred_element_type=F32)


def _split2(a):
    hi = a.astype(BF16)
    lo = (a - hi.astype(F32)).astype(BF16)
    return hi, lo


def _split3(a):
    hi = a.astype(BF16)
    r1 = a - hi.astype(F32)
    mid = r1.astype(BF16)
    lo = (r1 - mid.astype(F32)).astype(BF16)
    return hi, mid, lo


def _dot3(a, b, dims=NN):
    ah, al = _split2(a)
    bh, bl = _split2(b)
    d = functools.partial(lax.dot_general, dimension_numbers=dims, preferred_element_type=F32)
    return d(ah, bh) + (d(ah, bl) + d(al, bh))


def _mm3(a, b, dims):
    d = functools.partial(lax.dot_general, dimension_numbers=dims, preferred_element_type=F32)
    return d(a[0], b[0]) + (d(a[0], b[1]) + d(a[1], b[0]))


def _bd_parts(x, mask):
    out = []
    for part in _split2(x):
        tiled = jnp.concatenate([part] * HEADS_PER_GROUP, axis=0)
        out.append(jnp.where(mask, tiled, jnp.zeros_like(tiled)))
    return tuple(out)


def _dot_exact_rhs(a_exact, b, dims=NN):
    ab = a_exact.astype(BF16)
    bh, bm, bl = _split3(b)
    d = functools.partial(lax.dot_general, dimension_numbers=dims, preferred_element_type=F32)
    return d(ab, bh) + (d(ab, bm) + d(ab, bl))


def _gsum(x, g_ref):
    hi, mid, lo = _split3(x)
    g = g_ref[...]
    d = functools.partial(jnp.dot, preferred_element_type=F32)
    return d(hi, g) + (d(mid, g) + d(lo, g))


def _sigmoid(x):
    return 1.0 / (1.0 + jnp.exp(-x))


def _softplus(x):
    return jnp.maximum(x, 0.0) + jnp.log1p(jnp.exp(-jnp.abs(x)))


def _silu(x):
    return x * _sigmoid(x)


def _pack_bf16_pairs(x):
    w = x.shape[1] // 2
    bits = lax.bitcast_convert_type(x.astype(BF16).astype(F32), I32)
    return lax.shift_right_logical(bits[:, :w], 16) | (bits[:, w:] & -65536)


def _unpack_bf16_pairs(p):
    lo = lax.bitcast_convert_type(lax.shift_left(p, 16), F32)
    hi = lax.bitcast_convert_type(p & -65536, F32)
    return jnp.concatenate([lo, hi], axis=1).astype(BF16)


def _group_ones():
    h = jnp.arange(WIDTH, dtype=I32) // HEAD_DIM
    return (h[:, None] == h[None, :]).astype(BF16)


def _ada_kernel(c_ref, w_ref, b_ref, o_ref):
    o_ref[...] = _dot(_silu(c_ref[...]), w_ref[...]) + b_ref[...]


def _ada(c, w_ada, b_ada):
    nb = c.shape[0]
    n_out = w_ada.shape[1]
    blk = D_MODEL
    return pl.pallas_call(
        _ada_kernel,
        grid=(n_out // blk,),
        in_specs=[pl.BlockSpec((nb, D_MODEL), lambda j: (0, 0)),
                  pl.BlockSpec((D_MODEL, blk), lambda j: (0, j)),
                  pl.BlockSpec((1, blk), lambda j: (0, j))],
        out_specs=pl.BlockSpec((nb, blk), lambda j: (0, j)),
        out_shape=jax.ShapeDtypeStruct((nb, n_out), F32),
        compiler_params=_cparams(("parallel",)),
        name="ada_mod",
    )(c, w_ada, b_ada.reshape(1, n_out))


def _inproj_kernel(x_ref, mod_ref, g1_ref, wr_ref, wf_ref, wfl_ref, wg_ref, qn_ref, kn_ref, fb_ref, gm_ref, f0_ref,
                   pr_ref, q_ref, k_ref, v_ref, sg_ref, lf_ref, cf_ref, gate_ref, carry):
    bb, tt, d = x_ref.shape
    m = bb * tt
    x = x_ref[...]
    ms = jnp.mean(x * x, axis=-1, keepdims=True)
    h = x * lax.rsqrt(ms + RMS_EPS) * g1_ref[...]
    h = h * (1.0 + mod_ref[:, 1:2, :]) + mod_ref[:, 0:1, :]
    hb = h.reshape(m, d).astype(BF16)

    pr_ref[...] = jnp.dot(hb, wr_ref[...], preferred_element_type=F32).reshape(bb, tt, RWKV_COLS)

    f = jnp.dot(hb, wf_ref[...], preferred_element_type=F32)
    q = f[:, 0:WIDTH]
    k = f[:, WIDTH:2 * WIDTH]
    v = f[:, 2 * WIDTH:3 * WIDTH]
    og = f[:, 3 * WIDTH:4 * WIDTH]
    inv_hd = 1.0 / HEAD_DIM
    q = q * lax.rsqrt(_gsum(q * q, gm_ref) * inv_hd + RMS_EPS) * qn_ref[...]
    k = k * lax.rsqrt(_gsum(k * k, gm_ref) * inv_hd + RMS_EPS) * kn_ref[...]
    q_ref[...] = (q * (HEAD_DIM ** -0.5)).astype(BF16).reshape(bb, tt, WIDTH)
    k_ref[...] = k.reshape(bb, tt, WIDTH)
    v_ref[...] = v.reshape(bb, tt, WIDTH)
    sg_ref[...] = _sigmoid(og).reshape(bb, tt, WIDTH)

    fl = jnp.dot(hb, wfl_ref[...], preferred_element_type=F32)[:, 0:N_HEADS] + fb_ref[...]
    lf = -_softplus(-fl)
    lf_ref[...] = lf.reshape(bb, tt, N_HEADS)

    @pl.when(pl.program_id(1) == 0)
    def _():
        carry[...] = f0_ref[...]

    r = lax.broadcasted_iota(I32, (m, m), 0)
    c = lax.broadcasted_iota(I32, (m, m), 1)
    tri = jnp.logical_and(r // tt == c // tt, r >= c).astype(F32)
    cf = _dot_exact_rhs(tri, lf).reshape(bb, tt, N_HEADS) + carry[...]
    cf_ref[...] = cf
    carry[...] = cf[:, tt - 1:tt, :]

    gate_ref[...] =_sigmoid(jnp.dot(hb, wg_ref[...], preferred_element_type=F32)).reshape(bb, tt, GATE_COLS)


def _const_spec(shape):
    nd = len(shape)
    return pl.BlockSpec(shape, lambda *_: (0,) * nd)


def _inproj(x, mod, g1, wr, wf, wfl, wg, qn, kn, fb, gmat, f0, bb, tt):
    b, t, d = x.shape
    grid = (b // bb, t // tt)

    def tok(cols):
        return pl.BlockSpec((bb, tt, cols), lambda i, j: (i, j, 0))

    out_cols = [(RWKV_COLS, F32), (WIDTH, BF16), (WIDTH, F32), (WIDTH, F32), (WIDTH, F32), (N_HEADS, F32),
                (N_HEADS, F32), (GATE_COLS, F32)]
    return pl.pallas_call(
        _inproj_kernel,
        grid=grid,
        in_specs=[tok(d),
                  pl.BlockSpec((bb, 6, d), lambda i, j: (i, 0, 0)),
                  _const_spec((1, d)),
                  _const_spec(wr.shape), _const_spec(wf.shape), _const_spec(wfl.shape), _const_spec(wg.shape),
                  _const_spec((1, WIDTH)), _const_spec((1, WIDTH)), _const_spec((1, N_HEADS)),
                  _const_spec((WIDTH, WIDTH)),
                  pl.BlockSpec((bb, 1, N_HEADS), lambda i, j: (i, 0, 0))],
        out_specs=[tok(c) for c, _ in out_cols],
        out_shape=[jax.ShapeDtypeStruct((b, t, c), dt) for c, dt in out_cols],
        scratch_shapes=[pltpu.VMEM((bb, 1, N_HEADS), F32)],
        compiler_params=_cparams(("parallel", "arbitrary")),
        name="norm1_inproj",
    )(x, mod, g1, wr, wf, wfl, wg, qn, kn, fb, gmat, f0)


def _past_cumsum_kernel(x_ref, o_ref):
    x = x_ref[0]
    rows = x.shape[0]
    li = lax.broadcasted_iota(I32, (LANES, LANES), 0)
    lj = lax.broadcasted_iota(I32, (LANES, LANES), 1)
    same_head = (li % N_HEADS) == (lj % N_HEADS)
    within = jnp.logical_and(same_head, li // N_HEADS <= lj // N_HEADS).astype(BF16)
    xh, xm, xl = _split3(x)
    d2 = functools.partial(jnp.dot, preferred_element_type=F32)
    in_row = d2(xh, within) + (d2(xm, within) + d2(xl, within))
    sh = same_head.astype(BF16)
    row_tot = d2(xh, sh) + (d2(xm, sh) + d2(xl, sh))
    ri = lax.broadcasted_iota(I32, (rows, rows), 0)
    ci = lax.broadcasted_iota(I32, (rows, rows), 1)
    o_ref[0] = in_row + _dot_exact_rhs((ri > ci).astype(F32), row_tot)


def _past_cumsum(past_logf):
    b, p, h = past_logf.shape
    rows = p * h // LANES
    flat = past_logf.reshape(b, rows, LANES)
    out = pl.pallas_call(
        _past_cumsum_kernel,
        grid=(b,),
        in_specs=[pl.BlockSpec((1, rows, LANES), lambda i: (i, 0, 0))],
        out_specs=pl.BlockSpec((1, rows, LANES), lambda i: (i, 0, 0)),
        out_shape=jax.ShapeDtypeStruct((b, rows, LANES), F32),
        compiler_params=_cparams(("parallel",)),
        name="cache_logf_cumsum",
    )(flat)
    return out.reshape(b, p, h)


def _fox_kernel(*refs, n_past_blocks, tq):
    if n_past_blocks:
        (q_ref, fq_ref, sg_ref, kp_ref, vp_ref, fkp_ref, kn_ref, vn_ref, fkn_ref,
         o_ref, m_scr, l_scr, acc_scr) = refs
    else:
        q_ref, fq_ref, sg_ref, kn_ref, vn_ref, fkn_ref, o_ref, m_scr, l_scr, acc_scr = refs
    qi = pl.program_id(1)
    ki = pl.program_id(2)
    nk = pl.num_programs(2)

    @pl.when(ki == 0)
    def _():
        m_scr[...] = jnp.full(m_scr.shape, NEG_BIG, F32)
        l_scr[...] = jnp.zeros(l_scr.shape, F32)
        acc_scr[...] = jnp.zeros(acc_scr.shape, F32)

    lane_a = lax.broadcasted_iota(I32, (tq, LANES), 1) < HEAD_DIM

    def step(k_ref, v_ref, fk_ref, diag):
        tk = k_ref.shape[1]
        if diag:
            rq = lax.broadcasted_iota(I32, (tq, tk), 0)
            ck = lax.broadcasted_iota(I32, (tq, tk), 1)
            visible = ck <= rq
        fq_all = fq_ref[0]
        pairs = range(N_HEADS // 2)
        cols = [slice(j * LANES, (j + 1) * LANES) for j in pairs]
        scores = []
        for j in pairs:
            qj = q_ref[0, :, cols[j]]
            kb = k_ref[0, :, cols[j]].astype(BF16)
            for hh in range(2):
                h = 2 * j + hh
                qm = jnp.where(lane_a if hh == 0 else jnp.logical_not(lane_a), qj, jnp.zeros_like(qj))
                s = lax.dot_general(qm, kb, NT, preferred_element_type=F32)
                s = s + fq_all[:, h:h + 1] - fk_ref[0, h:h + 1, :]
                if diag:
                    s = jnp.where(visible, s, NEG_BIG)
                scores.append(s)
        alphas, probs = [], []
        for h in range(N_HEADS):
            m_old = m_scr[h]
            m_new = jnp.maximum(m_old, jnp.max(scores[h], axis=-1, keepdims=True))
            alpha = jnp.exp(m_old - m_new)
            p = jnp.exp(scores[h] - m_new)
            l_scr[h] = alpha * l_scr[h] + jnp.sum(p, axis=-1, keepdims=True)
            m_scr[h] = m_new
            alphas.append(alpha)
            probs.append(p.astype(BF16))
        for j in pairs:
            vb = v_ref[0, :, cols[j]].astype(BF16)
            pv0 = jnp.dot(probs[2 * j], vb, preferred_element_type=F32)
            pv1 = jnp.dot(probs[2 * j + 1], vb, preferred_element_type=F32)
            acc_scr[:, cols[j]] = (acc_scr[:, cols[j]] * jnp.where(lane_a, alphas[2 * j], alphas[2 * j + 1])
                                   + jnp.where(lane_a, pv0, pv1))

    if n_past_blocks:
        @pl.when(ki < n_past_blocks)
        def _():
            step(kp_ref, vp_ref, fkp_ref, False)

    kn = ki - n_past_blocks

    @pl.when(jnp.logical_and(kn >= 0, kn < qi))
    def _():
        step(kn_ref, vn_ref, fkn_ref, False)

    @pl.when(kn == qi)
    def _():
        step(kn_ref, vn_ref, fkn_ref, True)

    @pl.when(ki == nk - 1)
    def _():
        for j in range(N_HEADS // 2):
            cols = slice(j * LANES, (j + 1) * LANES)
            l = jnp.where(lane_a, l_scr[2 * j], l_scr[2 * j + 1])
            o_ref[0, :, cols] = acc_scr[:, cols] / l * sg_ref[0, :, cols]


def _fox_attention(q, fq, sg, k_new, v_new, fk_new_t, past=None, tq=512, tk_past=512):
    b, t, _ = q.shape
    nq = t // tq
    n_past_blocks = 0 if past is None else past[0].shape[1] // tk_past
    nk = n_past_blocks + nq

    def new_idx(i, qi, ki):
        return jnp.clip(ki - n_past_blocks, 0, qi)

    in_specs = [pl.BlockSpec((1, tq, WIDTH), lambda i, qi, ki: (i, qi, 0)),
                pl.BlockSpec((1, tq, N_HEADS), lambda i, qi, ki: (i, qi, 0)),
                pl.BlockSpec((1, tq, WIDTH), lambda i, qi, ki: (i, qi, 0))]
    args = [q, fq, sg]
    if n_past_blocks:
        def past_idx(i, qi, ki):
            return jnp.minimum(ki, n_past_blocks - 1)
        in_specs += [pl.BlockSpec((1, tk_past, WIDTH), lambda i, qi, ki: (i, past_idx(i, qi, ki), 0)),
                     pl.BlockSpec((1, tk_past, WIDTH), lambda i, qi, ki: (i, past_idx(i, qi, ki), 0)),
                     pl.BlockSpec((1, N_HEADS, tk_past), lambda i, qi, ki: (i, 0, past_idx(i, qi, ki)))]
        args += list(past)
    in_specs += [pl.BlockSpec((1, tq, WIDTH), lambda i, qi, ki: (i, new_idx(i, qi, ki), 0)),
                 pl.BlockSpec((1, tq, WIDTH), lambda i, qi, ki: (i, new_idx(i, qi, ki), 0)),
                 pl.BlockSpec((1, N_HEADS, tq), lambda i, qi, ki: (i, 0, new_idx(i, qi, ki)))]
    args += [k_new, v_new, fk_new_t]
    return pl.pallas_call(
        functools.partial(_fox_kernel, n_past_blocks=n_past_blocks, tq=tq),
        grid=(b, nq, nk),
        in_specs=in_specs,
        out_specs=pl.BlockSpec((1, tq, WIDTH), lambda i, qi, ki: (i, qi, 0)),
        out_shape=jax.ShapeDtypeStruct((b, t, WIDTH), F32),
        scratch_shapes=[pltpu.VMEM((N_HEADS, tq, 1), F32), pltpu.VMEM((N_HEADS, tq, 1), F32),
                        pltpu.VMEM((tq, WIDTH), F32)],
        compiler_params=_cparams(("parallel", "parallel", "arbitrary")),
        name="fox_attention",
    )(*args)


def _rwkv_kernel(p_ref, sh0_ref, s0_ref, mu_ref, w0_ref, wb_ref, a0_ref, ab_ref, gb_ref, kk_ref, ka_ref, rk_ref,
                 lnw_ref, lnb_ref, gm_ref, y_ref, st_ref, sht_ref, z_scr, prev_scr, *, c):
    t = pl.program_id(1)
    nt = pl.num_programs(1)
    n_rows = p_ref.shape[1]
    n_chunks = n_rows // c

    def head_block(h):
        lo = (h % HEADS_PER_GROUP) * HEAD_DIM
        return h // HEADS_PER_GROUP, slice(lo, lo + HEAD_DIM)

    @pl.when(t == 0)
    def _():
        z_scr[...] = jnp.zeros(z_scr.shape, F32)
        for h in range(N_HEADS):
            i, blk = head_block(h)
            z_scr[i, blk, blk] = s0_ref[0, h]
        prev_scr[...] = sh0_ref[0]

    p = p_ref[0]
    row = lax.broadcasted_iota(I32, p.shape, 0)
    prev = jnp.where(row == 0, prev_scr[...], pltpu.roll(p, 1, 0))
    last = p[n_rows - 1:n_rows, :]
    prev_scr[...] = last
    sht_ref[0] = last

    pm = p + (prev - p) * mu_ref[...]
    r = pm[:, 0:WIDTH]
    k = pm[:, WIDTH:2 * WIDTH]
    v = pm[:, 2 * WIDTH:3 * WIDTH]
    o1 = 3 * WIDTH
    wd = pm[:, o1:o1 + DECAY_LORA]
    ad = pm[:, o1 + DECAY_LORA:o1 + DECAY_LORA + ICLR_LORA]
    gd = pm[:, o1 + DECAY_LORA + ICLR_LORA:RWKV_COLS]

    w = -_softplus(-(w0_ref[...] + _dot(jnp.tanh(wd), wb_ref[...]))) - 0.5
    lw = -jnp.exp(w)
    a = _sigmoid(a0_ref[...] + _dot(ad, ab_ref[...]))
    g = _dot(_sigmoid(gd), gb_ref[...])
    kk = k * kk_ref[...]
    kk = kk / jnp.maximum(jnp.sqrt(_gsum(kk * kk, gm_ref)), L2_EPS)
    kf = k * (1.0 + (a - 1.0) * ka_ref[...])

    ri = lax.broadcasted_iota(I32, (n_rows, n_rows), 0)
    ci = lax.broadcasted_iota(I32, (n_rows, n_rows), 1)
    same_chunk = (ri // c) == (ci // c)
    cum = _dot_exact_rhs(jnp.logical_and(same_chunk, ri >= ci).astype(F32), lw)
    cum_last = _dot_exact_rhs(same_chunk.astype(F32), lw)
    r_t = r * jnp.exp(cum)
    a_t = -kk * jnp.exp(cum - lw)
    inv = jnp.exp(-cum)
    b_t = kk * a * inv
    k_t = kf * inv
    to_end = jnp.exp(cum_last - cum)
    b_e = kk * a * to_end
    k_e = kf * to_end
    g_end = jnp.exp(cum_last)

    hg = HEADS_PER_GROUP
    gw = hg * HEAD_DIM
    log_c = int(math.log2(c))
    t_idx = lax.broadcasted_iota(I32, (c, hg * c), 0)
    s_idx = lax.broadcasted_iota(I32, (c, hg * c), 1) & (c - 1)
    strict = s_idx < t_idx
    lower = s_idx <= t_idx
    eye = (s_idx == t_idx).astype(F32)
    rb = lax.broadcasted_iota(I32, (hg * c, gw), 0) >> log_c
    mask_kv = rb == (lax.broadcasted_iota(I32, (hg * c, gw), 1) >> int(math.log2(HEAD_DIM)))
    rs = lax.broadcasted_iota(I32, (hg * c, hg * c), 0) >> log_c
    mask_ss = rs == (lax.broadcasted_iota(I32, (hg * c, hg * c), 1) >> log_c)
    ng = N_HEADS // hg
    cat = functools.partial(jnp.concatenate, axis=0)
    units = [(slice(j * c, (j + 1) * c), slice(i * gw, (i + 1) * gw)) for j in range(n_chunks) for i in range(ng)]
    nu = len(units)

    ar = [_split2(cat([a_t[rs_, s], r_t[rs_, s]])) for rs_, s in units]
    ab = [_mm3(ar[n], _bd_parts(b_t[units[n]], mask_kv), NT) for n in range(nu)]
    ak = [_mm3(ar[n], _bd_parts(k_t[units[n]], mask_kv), NT) for n in range(nu)]
    l_ab = [jnp.where(strict, m[:c], 0.0) for m in ab]
    l_rb = [jnp.where(lower, m[c:], 0.0) for m in ab]
    l_ak = [jnp.where(strict, m[:c], 0.0) for m in ak]
    l_rk = [jnp.where(lower, m[c:], 0.0) for m in ak]
    def mm1(a, b_bd):
        return jnp.dot(a.astype(BF16), b_bd, preferred_element_type=F32)

    def bd1(x, mask):
        tiled = jnp.concatenate([x.astype(BF16)] * hg, axis=0)
        return jnp.where(mask, tiled, jnp.zeros_like(tiled))

    tinv = [eye + m for m in l_ab]
    pw = [mm1(m, bd1(m, mask_ss)) for m in l_ab]
    for _ in range(1, log_c - 1):
        res = [mm1(cat([tinv[n], pw[n]]), bd1(pw[n], mask_ss)) for n in range(nu)]
        tinv = [tinv[n] + res[n][:c] for n in range(nu)]
        pw = [m[c:] for m in res]
    tinv = [tinv[n] + mm1(tinv[n], bd1(pw[n], mask_ss)) for n in range(nu)]
    av = [_mm3(_split2(cat([l_ak[n], l_rk[n]])), _bd_parts(v[units[n]], mask_kv), NN) for n in range(nu)]
    ue = [_split2(cat([b_e[units[n]], k_e[units[n]]])) for n in range(nu)]

    def wide(fn, x):
        return [fn(x[:, :gw]), fn(x[:, gw:])]

    def bd1w(x):
        return jnp.concatenate(wide(lambda h_: bd1(h_, mask_kv), x), axis=1)

    def bd3w(x):
        parts = wide(lambda h_: _bd_parts(h_, mask_kv), x)
        return tuple(jnp.concatenate([parts[0][q], parts[1][q]], axis=1) for q in range(2))

    rhs = [jnp.concatenate([a_t[units[n]], av[n][:c]], axis=1) for n in range(nu)]
    x0 = [mm1(tinv[n], bd1w(rhs[n])) for n in range(nu)]
    resid = [rhs[n] - (x0[n] - _mm3(_split2(l_ab[n]), bd3w(x0[n]), NN)) for n in range(nu)]
    sol = [x0[n] + mm1(tinv[n], bd1w(resid[n])) for n in range(nu)]
    lift = [_mm3(_split2(l_rb[n]), bd3w(sol[n]), NN) for n in range(nu)]
    lhs_s = [_split2(cat([sol[n][:, :gw], r_t[units[n]] + lift[n][:, :gw]])) for n in range(nu)]
    u_loc = [sol[n][:, gw:] for n in range(nu)]
    o_loc = [av[n][c:] + lift[n][:, gw:] for n in range(nu)]

    zr = lax.broadcasted_iota(I32, (gw, gw), 0) >> int(math.log2(HEAD_DIM))
    zmask = zr == (lax.broadcasted_iota(I32, (gw, gw), 1) >> int(math.log2(HEAD_DIM)))
    z = [z_scr[i] for i in range(ng)]
    o_rows = []
    for j in range(n_chunks):
        o_grp = []
        for i in range(ng):
            n = j * ng + i
            rs_, s = units[n]
            sz = _mm3(lhs_s[n], _split2(z[i]), NT)
            u = sz[:c] + u_loc[n]
            o_grp.append(sz[c:] + o_loc[n])
            upd = _mm3(_split2(cat([u, v[rs_, s]])), ue[n], TN)
            z[i] = z[i] * g_end[j * c:j * c + 1, s] + jnp.where(zmask, upd, 0.0)
        o_rows.append(jnp.concatenate(o_grp, axis=1))
    for i in range(ng):
        z_scr[i] = z[i]

    o = cat(o_rows)
    inv_hd = 1.0 / HEAD_DIM
    dlt = o - _gsum(o, gm_ref) * inv_hd
    var = _gsum(dlt * dlt, gm_ref) * inv_hd
    on = dlt * lax.rsqrt(var + RWKV_GN_EPS) * lnw_ref[...] + lnb_ref[...]
    bonus = _gsum(r * kf * rk_ref[...], gm_ref) * v
    y_ref[0] = (on + bonus) * g

    @pl.when(t == nt - 1)
    def _():
        for h in range(N_HEADS):
            i, blk = head_block(h)
            st_ref[0, h] = z_scr[i, blk, blk]


def _rwkv(p, shift0, s0, prm, gmat, chunk, chunks_per_step):
    b, t, _ = p.shape
    row = lambda n: _const_spec((1, n))
    rows = chunk * chunks_per_step
    return pl.pallas_call(
        functools.partial(_rwkv_kernel, c=chunk),
        grid=(b, t // rows),
        in_specs=[pl.BlockSpec((1, rows, RWKV_COLS), lambda i, j: (i, j, 0)),
                  pl.BlockSpec((1, 1, RWKV_COLS), lambda i, j: (i, 0, 0)),
                  pl.BlockSpec((1, N_HEADS, HEAD_DIM, HEAD_DIM), lambda i, j: (i, 0, 0, 0)),
                  row(RWKV_COLS), row(WIDTH), _const_spec((DECAY_LORA, WIDTH)), row(WIDTH),
                  _const_spec((ICLR_LORA, WIDTH)), _const_spec((GATE_LORA, WIDTH)),
                  row(WIDTH), row(WIDTH), row(WIDTH), row(WIDTH), row(WIDTH), _const_spec((WIDTH, WIDTH))],
        out_specs=[pl.BlockSpec((1, rows, WIDTH), lambda i, j: (i, j, 0)),
                   pl.BlockSpec((1, N_HEADS, HEAD_DIM, HEAD_DIM), lambda i, j: (i, 0, 0, 0)),
                   pl.BlockSpec((1, 1, RWKV_COLS), lambda i, j: (i, 0, 0))],
        out_shape=[jax.ShapeDtypeStruct((b, t, WIDTH), F32),
                   jax.ShapeDtypeStruct((b, N_HEADS, HEAD_DIM, HEAD_DIM), F32),
                   jax.ShapeDtypeStruct((b, 1, RWKV_COLS), F32)],
        scratch_shapes=[pltpu.VMEM((N_HEADS // HEADS_PER_GROUP, HEADS_PER_GROUP * HEAD_DIM,
                                    HEADS_PER_GROUP * HEAD_DIM), F32),
                        pltpu.VMEM((1, RWKV_COLS), F32)],
        compiler_params=_cparams(("parallel", "arbitrary")),
        name="rwkv7_mix",
    )(p, shift0, s0, prm["mu"], prm["w0"], prm["wb"], prm["a0"], prm["ab"], prm["gb"], prm["kk"], prm["ka"],
      prm["rk"], prm["lnw"], prm["lnb"], gmat)


def _merge_kernel(x_ref, ya_ref, yb_ref, gate_ref, mod_ref, g2_ref, woa_ref, wob_ref, wo_ref, wrh_ref, wrl_ref,
                  *rest):
    x1_ref, h2_ref, lg_ref = rest[-3:]
    bb, tt, d = x_ref.shape
    m = bb * tt
    gate = gate_ref[...].reshape(m, GATE_COLS)
    merged = (gate[:, 0:d] * _dot(ya_ref[...].reshape(m, WIDTH), woa_ref[...])
              + gate[:, d:2 * d] * _dot(yb_ref[...].reshape(m, WIDTH), wob_ref[...]))
    x1 = x_ref[...] + mod_ref[:, 2:3, :] * _dot(merged, wo_ref[...]).reshape(bb, tt, d)
    x1_ref[...] = x1
    ms = jnp.mean(x1 * x1, axis=-1, keepdims=True)
    h2 = x1 * lax.rsqrt(ms + RMS_EPS) * g2_ref[...]
    h2 = (h2 * (1.0 + mod_ref[:, 4:5, :]) + mod_ref[:, 3:4, :]).reshape(m, d)
    h2_ref[...] = _pack_bf16_pairs(h2)
    lg_ref[...] = _mm3((wrh_ref[...], wrl_ref[...]), _split2(h2), NT)


def _merge(x, ya, yb, gate, mod, w, n_total, row_offset, shared=None):
    b, t, d = x.shape
    bb, tt = _token_blocks(b, t)
    nt = t // tt
    m = bb * tt
    off = row_offset // m

    def tok(cols):
        return pl.BlockSpec((bb, tt, cols), lambda i, j: (i, j, 0))

    in_specs = [tok(d), tok(WIDTH), tok(WIDTH), tok(GATE_COLS),
                pl.BlockSpec((bb, 6, d), lambda i, j: (i, 0, 0)),
                _const_spec((1, d)), _const_spec((WIDTH, d)), _const_spec((WIDTH, d)), _const_spec((d, d)),
                _const_spec((N_EXPERTS, d)), _const_spec((N_EXPERTS, d))]
    args = [x, ya, yb, gate, mod.reshape(b, 6, d), w["g2"], w["w_oa"], w["w_ob"], w["w_o"], w["wr_hi"], w["wr_lo"]]
    aliases = {}
    if shared is not None:
        aliases = {len(args): 1, len(args) + 1: 2}
        in_specs += [pl.BlockSpec(memory_space=pl.ANY), pl.BlockSpec(memory_space=pl.ANY)]
        args += list(shared)
    return pl.pallas_call(
        _merge_kernel,
        grid=(b // bb, nt),
        in_specs=in_specs,
        out_specs=[tok(d), pl.BlockSpec((m, d // 2), lambda i, j: (off + i * nt + j, 0)),
                   pl.BlockSpec((N_EXPERTS, m), lambda i, j: (0, off + i * nt + j))],
        out_shape=[jax.ShapeDtypeStruct((b, t, d), F32), jax.ShapeDtypeStruct((n_total, d // 2), I32),
                   jax.ShapeDtypeStruct((N_EXPERTS, n_total), F32)],
        input_output_aliases=aliases,
        compiler_params=_cparams(("parallel", "parallel")),
        name="merge_norm2_router",
    )(*args)


def _route_kernel(lg_ref, bias_ref, idx_ref, wt_ref, rank_ref, cnt_ref, carry):
    @pl.when(pl.program_id(0) == 0)
    def _():
        carry[...] = jnp.zeros(carry.shape, F32)

    tm = lg_ref.shape[1]
    scores = _sigmoid(lg_ref[...])
    sel = scores + bias_ref[...]
    row = lax.broadcasted_iota(I32, (N_EXPERTS, tm), 0)
    neg_inf = -jnp.inf

    def first_argmax(vals, rows):
        mx = jnp.max(vals, axis=0, keepdims=True)
        return mx, jnp.min(jnp.where(vals == mx, rows, N_EXPERTS), axis=0, keepdims=True)

    gslices = [slice(g * EXPERTS_PER_GROUP, (g + 1) * EXPERTS_PER_GROUP) for g in range(N_GROUPS)]
    gs = []
    row_g = lax.broadcasted_iota(I32, (EXPERTS_PER_GROUP, tm), 0)
    for sl in gslices:
        m1, i1 = first_argmax(sel[sl], row_g)
        m2 = jnp.max(jnp.where(row_g == i1, neg_inf, sel[sl]), axis=0, keepdims=True)
        gs.append(m1 + m2)
    kept = []
    for g in range(N_GROUPS):
        beaten = jnp.zeros((1, tm), I32)
        for o in range(N_GROUPS):
            if o != g:
                wins = (gs[o] >= gs[g]) if o < g else (gs[o] > gs[g])
                beaten = beaten + wins.astype(I32)
        kept.append(jnp.where(beaten < TOPK_GROUPS, sel[gslices[g]], neg_inf))
    cur = jnp.concatenate(kept, axis=0)

    idxs, ws = [], []
    picked = jnp.zeros((N_EXPERTS, tm), F32)
    for _ in range(TOP_K):
        _, ik = first_argmax(cur, row)
        hit = row == ik
        idxs.append(ik)
        ws.append(jnp.sum(jnp.where(hit, scores, 0.0), axis=0, keepdims=True))
        cur = jnp.where(hit, neg_inf, cur)
        picked = jnp.where(hit, 1.0, picked)
    wsum = ws[0]
    for k in range(1, TOP_K):
        wsum = wsum + ws[k]

    r = lax.broadcasted_iota(I32, (tm, tm), 0)
    c = lax.broadcasted_iota(I32, (tm, tm), 1)
    before = jnp.dot(picked.astype(BF16), (r < c).astype(BF16), preferred_element_type=F32) + carry[...]
    carry[...] = carry[...] + jnp.sum(picked, axis=1, keepdims=True)
    cnt_ref[...] = carry[...]

    kk = lax.broadcasted_iota(I32, (TOP_K, tm), 0)
    idx_o = jnp.zeros((TOP_K, tm), I32)
    wt_o = jnp.zeros((TOP_K, tm), F32)
    rank_o = jnp.zeros((TOP_K, tm), F32)
    for k in range(TOP_K):
        rk = jnp.sum(jnp.where(row == idxs[k], before, 0.0), axis=0, keepdims=True)
        idx_o = jnp.where(kk == k, idxs[k], idx_o)
        wt_o = jnp.where(kk == k, ws[k] / wsum * ROUTED_SCALE, wt_o)
        rank_o = jnp.where(kk == k, rk, rank_o)
    idx_ref[...] = idx_o
    wt_ref[...] = wt_o
    rank_ref[...] = rank_o.astype(I32)


def _route(logits_t, bias_col, tm):
    n = logits_t.shape[1]
    tokk = pl.BlockSpec((TOP_K, tm), lambda i: (0, i))
    return pl.pallas_call(
        _route_kernel,
        grid=(n // tm,),
        in_specs=[pl.BlockSpec((N_EXPERTS, tm), lambda i: (0, i)), _const_spec((N_EXPERTS, 1))],
        out_specs=[tokk, tokk, tokk, _const_spec((N_EXPERTS, 1))],
        out_shape=[jax.ShapeDtypeStruct((TOP_K, n), I32), jax.ShapeDtypeStruct((TOP_K, n), F32),
                   jax.ShapeDtypeStruct((TOP_K, n), I32), jax.ShapeDtypeStruct((N_EXPERTS, 1), F32)],
        scratch_shapes=[pltpu.VMEM((N_EXPERTS, 1), F32)],
        compiler_params=_cparams(("arbitrary",)),
        name="route_topk",
    )(logits_t, bias_col)


def _plan_kernel(cnt_ref, start_ref, be_ref, valid_ref, nu_ref, *, blk):
    cnt = cnt_ref[...]
    padded = jnp.ceil(cnt * (1.0 / blk)) * blk
    e_r = lax.broadcasted_iota(I32, (N_EXPERTS, N_EXPERTS), 0)
    e_c = lax.broadcasted_iota(I32, (N_EXPERTS, N_EXPERTS), 1)
    incl = (e_r <= e_c).astype(BF16)
    ph, pm, plo = _split3(jnp.broadcast_to(padded, (8, N_EXPERTS)))
    d2 = functools.partial(jnp.dot, preferred_element_type=F32)
    pad_end = (d2(ph, incl) + (d2(pm, incl) + d2(plo, incl)))[0:1, :]
    pad_start = pad_end - padded
    start_ref[...] = pad_start.astype(I32)
    total = jnp.max(pad_end, axis=-1, keepdims=True)
    nu_ref[...] = jnp.broadcast_to(total * (1.0 / blk), (1, N_EXPERTS)).astype(I32)
    nb = be_ref.shape[0]
    first = (lax.broadcasted_iota(I32, (nb, N_EXPERTS), 0) * blk).astype(F32)
    lane = lax.broadcasted_iota(I32, (nb, N_EXPERTS), 1)
    inside = jnp.logical_and(pad_start <= first, first < pad_end)
    be_ref[...] = jnp.sum(jnp.where(inside, lane, 0), axis=-1, keepdims=True)
    rows = jnp.minimum(pad_start + cnt - first, float(blk))
    valid_ref[...] = jnp.sum(jnp.where(inside, rows, 0.0), axis=-1, keepdims=True).astype(I32)


def _plan(counts, n_blocks, blk):
    return pl.pallas_call(
        functools.partial(_plan_kernel, blk=blk),
        out_shape=[jax.ShapeDtypeStruct((1, N_EXPERTS), I32), jax.ShapeDtypeStruct((n_blocks, 1), I32),
                   jax.ShapeDtypeStruct((n_blocks, 1), I32), jax.ShapeDtypeStruct((1, N_EXPERTS), I32)],
        compiler_params=pltpu.CompilerParams(vmem_limit_bytes=VMEM_LIMIT),
        name="dispatch_plan",
    )(counts)


def _dest_kernel(idx_ref, rank_ref, start_ref, dest_ref):
    tm = idx_ref.shape[1]
    row = lax.broadcasted_iota(I32, (N_EXPERTS, tm), 0)
    kk = lax.broadcasted_iota(I32, (TOP_K, tm), 0)
    idx = idx_ref[...]
    base = jnp.zeros((TOP_K, tm), I32)
    for k in range(TOP_K):
        bk = jnp.sum(jnp.where(row == idx[k:k + 1, :], start_ref[...], 0), axis=0, keepdims=True)
        base = jnp.where(kk == k, bk, base)
    dest_ref[...] = base + rank_ref[...]


def _dest(idx, rank, pad_start_col, tm):
    n = idx.shape[1]
    tokk = pl.BlockSpec((TOP_K, tm), lambda i: (0, i))
    return pl.pallas_call(
        _dest_kernel,
        grid=(n // tm,),
        in_specs=[tokk, tokk, _const_spec((N_EXPERTS, 1))],
        out_specs=tokk,
        out_shape=jax.ShapeDtypeStruct((TOP_K, n), I32),
        compiler_params=_cparams(("parallel",)),
        name="dispatch_dest",
    )(idx, rank, pad_start_col)


def _dispatch(h2p, dest_t, n_slots):
    n, wp = h2p.shape
    half = wp // SC_ROW_SPLIT
    window = SC_SCATTER_WINDOW
    mesh = plsc.VectorSubcoreMesh(core_axis_name="core", subcore_axis_name="subcore")
    out = jax.ShapeDtypeStruct((n_slots, half), h2p.dtype)

    @functools.partial(pl.kernel, out_type=[out] * SC_ROW_SPLIT, mesh=mesh, scratch_types=[])
    def scatter(rows_hbm, idx_hbm, *outs):
        for c, out_hbm in enumerate(outs):
            def body(rows_vmem, idx_vmem, out_hbm=out_hbm):
                for k in range(TOP_K):
                    pltpu.sync_copy(rows_vmem, out_hbm.at[idx_vmem.at[k]])

            pltpu.emit_pipeline(
                body, grid=(n // window,),
                in_specs=[pl.BlockSpec((window, half), index_map=lambda i, c=c: (i, c)),
                          pl.BlockSpec((TOP_K, window), index_map=lambda i: (0, i))],
                out_specs=[], core_axis_name=("core", "subcore"), dimension_semantics=(pltpu.PARALLEL,),
            )(rows_hbm, idx_hbm)

    return scatter(h2p, dest_t)


def _expert_kernel(be_ref, valid_ref, nu_ref, xa_ref, xb_ref, wg_ref, wu_ref, wd_ref, *rest):
    y_refs, (wg_b, wu_b, wd_b) = rest[:SC_ROW_SPLIT], rest[SC_ROW_SPLIT:]
    i = pl.program_id(0)
    nv = valid_ref[i]
    new_expert = jnp.logical_or(i == 0, be_ref[i] != be_ref[jnp.maximum(i - 1, 0)])

    @pl.when(jnp.logical_and(nv > 0, new_expert))
    def _():
        wg_b[...] = wg_ref[0].astype(BF16)
        wu_b[...] = wu_ref[0].astype(BF16)
        wd_b[...] = wd_ref[0].astype(BF16)

    @pl.when(nv > 0)
    def _():
        blk = xa_ref.shape[0]
        rows = lax.broadcasted_iota(I32, (blk, 1), 0)
        packed = jnp.concatenate([xa_ref[...], xb_ref[...]], axis=1)
        x = _unpack_bf16_pairs(jnp.where(rows < nv, packed, 0))
        hg = jnp.dot(x, wg_b[...], preferred_element_type=F32)
        hu = jnp.dot(x, wu_b[...], preferred_element_type=F32)
        y = _pack_bf16_pairs(jnp.dot((_silu(hg) * hu).astype(BF16), wd_b[...], preferred_element_type=F32))
        cw = y.shape[1] // SC_ROW_SPLIT
        for c, y_ref in enumerate(y_refs):
            y_ref[...] = y[:, c * cw:(c + 1) * cw]


def _experts(xs, block_e, valid, n_used, w_eg, w_eu, w_ed, blk):
    xa, xb = xs
    n_slots, packed = xa.shape
    d = w_eg.shape[1]
    n_blocks = n_slots // blk

    def row_blk(i, be, valid, nu):
        return (jnp.minimum(i, nu[0] - 1), 0)

    def w_blk(i, be, valid, nu):
        return (be[i], 0, 0)

    return pl.pallas_call(
        _expert_kernel,
        grid_spec=pltpu.PrefetchScalarGridSpec(
            num_scalar_prefetch=3,
            grid=(n_blocks,),
            in_specs=[pl.BlockSpec((blk, packed), row_blk), pl.BlockSpec((blk, packed), row_blk),
                      pl.BlockSpec((1, d, D_EXPERT), w_blk), pl.BlockSpec((1, d, D_EXPERT), w_blk),
                      pl.BlockSpec((1, D_EXPERT, d), w_blk)],
            out_specs=[pl.BlockSpec((blk, packed), row_blk)] * SC_ROW_SPLIT,
            scratch_shapes=[pltpu.VMEM((d, D_EXPERT), BF16), pltpu.VMEM((d, D_EXPERT), BF16),
                            pltpu.VMEM((D_EXPERT, d), BF16)]),
        out_shape=[jax.ShapeDtypeStruct((n_slots, packed), I32)] * SC_ROW_SPLIT,
        compiler_params=_cparams(("arbitrary",)),
        name="moe_experts",
    )(block_e, valid, n_used, xa, xb, w_eg, w_eu, w_ed)


def _combine_gather(ys, dest_t):
    k, n = dest_t.shape
    cw = ys[0].shape[1]
    window = SC_SCATTER_WINDOW
    mesh = plsc.VectorSubcoreMesh(core_axis_name="core", subcore_axis_name="subcore")

    @functools.partial(pl.kernel, out_type=jax.ShapeDtypeStruct((k * n, cw * len(ys)), ys[0].dtype), mesh=mesh,
                       scratch_types=[])
    def gather(*refs):
        y_refs, idx_hbm, out_hbm = refs[:len(ys)], refs[len(ys)], refs[len(ys) + 1]
        for c, y_hbm in enumerate(y_refs):
            def body(idx_vmem, out_vmem, y_hbm=y_hbm):
                pltpu.sync_copy(y_hbm.at[idx_vmem.at[0]], out_vmem)

            pltpu.emit_pipeline(
                body, grid=(k * n // window,),
                in_specs=[pl.BlockSpec((1, window), index_map=lambda i: (0, i))],
                out_specs=[pl.BlockSpec((window, cw), index_map=lambda i, c=c: (i, c))],
                core_axis_name=("core", "subcore"), dimension_semantics=(pltpu.PARALLEL,),
            )(idx_hbm, out_hbm)

    return gather(*ys, dest_t.reshape(1, k * n)).reshape(k, n, cw * len(ys))


def _final_kernel(x1_ref, h2_ref, wt_ref, mod_ref, wsg_ref, wsu_ref, wsd_ref, yg_ref, o_ref):
    bb, tt, d = x1_ref.shape
    hb = _unpack_bf16_pairs(h2_ref[...])
    hg = jnp.dot(hb, wsg_ref[...], preferred_element_type=F32)
    hu = jnp.dot(hb, wsu_ref[...], preferred_element_type=F32)
    ffn = _dot(_silu(hg) * hu, wsd_ref[...])
    wt = wt_ref[...]
    for k in range(TOP_K):
        ffn = ffn + wt[:, k:k + 1] * _unpack_bf16_pairs(yg_ref[k]).astype(F32)
    o_ref[...] = x1_ref[...] + mod_ref[:, 5:6, :] * ffn.reshape(bb, tt, d)


def _final(x1, h2_all, wts_all, y_tok, mod, w, row_offset):
    b, t, d = x1.shape
    bb, tt = _token_blocks(b, t)
    nt = t // tt
    m = bb * tt
    off = row_offset // m

    def flat_idx(i, j):
        return off + i * nt + j

    return pl.pallas_call(
        _final_kernel,
        grid=(b // bb, nt),
        in_specs=[pl.BlockSpec((bb, tt, d), lambda i, j: (i, j, 0)),
                  pl.BlockSpec((m, h2_all.shape[1]), lambda i, j: (flat_idx(i, j), 0)),
                  pl.BlockSpec((m, TOP_K), lambda i, j: (flat_idx(i, j), 0)),
                  pl.BlockSpec((bb, 6, d), lambda i, j: (i, 0, 0)),
                  _const_spec((d, D_EXPERT)), _const_spec((d, D_EXPERT)), _const_spec((D_EXPERT, d)),
                  pl.BlockSpec((TOP_K, m, y_tok.shape[2]), lambda i, j: (0, flat_idx(i, j), 0))],
        out_specs=pl.BlockSpec((bb, tt, d), lambda i, j: (i, j, 0)),
        out_shape=jax.ShapeDtypeStruct((b, t, d), F32),
        compiler_params=_cparams(("parallel", "parallel")),
        name="moe_combine_final",
    )(x1, h2_all, wts_all, mod.reshape(b, 6, d), w["w_sg"], w["w_su"], w["w_sd"], y_tok)


def _moe_routed(h2_all, logits_all, w, blk=EXPERT_BLOCK_ROWS, tm=256):
    n = h2_all.shape[0]
    n_blocks = (n * TOP_K + N_EXPERTS * (blk - 1)) // blk + 1
    n_blocks = (n_blocks + 7) // 8 * 8
    idx, wts_t, rank, counts = _route(logits_all, w["router_bias"], tm)
    pad_start, block_e, valid, n_used = _plan(counts.reshape(1, N_EXPERTS), n_blocks, blk)
    block_e = block_e.reshape(n_blocks)
    valid = valid.reshape(n_blocks)
    n_used = n_used[0, 0:1]
    dest_t = _dest(idx, rank, pad_start.reshape(N_EXPERTS, 1), tm)
    xs = _dispatch(h2_all, dest_t, n_blocks * blk)
    ys = _experts(xs, block_e, valid, n_used, w["w_eg"], w["w_eu"], w["w_ed"], blk)
    return _combine_gather(ys, dest_t), jnp.transpose(wts_t)


def _prep(raw):
    p = {k: v[0] for k, v in raw.items()}
    w_in = p["w_in"]
    o_fox = RWKV_COLS
    o_fl = o_fox + FOX_MAIN_COLS
    o_gate = o_fl + N_HEADS
    row = lambda a: a.reshape(1, -1)
    return dict(
        w_ada=p["w_ada"], b_ada=p["b_ada"],
        g1=row(p["norm1_g"]), g2=row(p["norm2_g"]),
        wr=w_in[:, :o_fox].astype(BF16),
        wf=w_in[:, o_fox:o_fl].astype(BF16),
        wfl=jnp.pad(w_in[:, o_fl:o_gate], ((0, 0), (0, LANES - N_HEADS))).astype(BF16),
        wg=w_in[:, o_gate:].astype(BF16),
        qn=row(jnp.tile(p["fox_q_norm"], N_HEADS)), kn=row(jnp.tile(p["fox_k_norm"], N_HEADS)),
        fb=row(p["fox_f_bias"]),
        gmat=_group_ones(),
        rwkv=dict(mu=row(p["rwkv_mu"]), w0=row(p["rwkv_w0"]), wb=p["rwkv_w_lora_b"], a0=row(p["rwkv_a0"]),
                  ab=p["rwkv_a_lora_b"], gb=p["rwkv_g_lora_b"], kk=row(p["rwkv_k_k"]), ka=row(p["rwkv_k_a"]),
                  rk=row(p["rwkv_r_k"]), lnw=row(p["rwkv_ln_w"]), lnb=row(p["rwkv_ln_b"])),
        w_oa=p["w_out_rwkv"].astype(BF16), w_ob=p["w_out_fox"].astype(BF16), w_o=p["w_out"].astype(BF16),
        wr_hi=p["w_router"].T.astype(BF16),
        wr_lo=(p["w_router"] - p["w_router"].astype(BF16).astype(F32)).T.astype(BF16),
        router_bias=p["router_bias"].reshape(N_EXPERTS, 1),
        w_eg=p["w_exp_gate"], w_eu=p["w_exp_up"], w_ed=p["w_exp_down"],
        w_sg=p["w_sh_gate"].astype(BF16), w_su=p["w_sh_up"].astype(BF16), w_sd=p["w_sh_down"].astype(BF16),
    )


def _token_blocks(b, t):
    if t >= 256:
        return 1, 256
    bb = max(1, min(b, 256 // t))
    while b % bb:
        bb -= 1
    return bb, t


def _mix_path(x, mod, shift0, wkv0, past_k, past_v, past_logf, w):
    b, t, d = x.shape
    bb, tt = _token_blocks(b, t)
    n_past = past_k.shape[1]
    if n_past:
        f_past = _past_cumsum(past_logf)
        init = f_past[:, n_past - 1:n_past, :]
        past = (past_k, past_v, jnp.swapaxes(f_past, 1, 2))
    else:
        init = jnp.zeros((b, 1, N_HEADS), F32)
        past = None
    pr, q, k, v, sg, logf, f_new, gate = _inproj(x, mod.reshape(b, 6, d), w["g1"], w["wr"], w["wf"], w["wfl"],
                                                 w["wg"], w["qn"], w["kn"], w["fb"], w["gmat"], init, bb, tt)
    y_fox = _fox_attention(q, f_new, sg, k, v, jnp.swapaxes(f_new, 1, 2), past=past, tq=min(t, 512),
                           tk_past=min(max(n_past, 1), 512))
    chunk = min(t, RWKV_CHUNK)
    y_rwkv, wkv_new, shift_new = _rwkv(pr, shift0.reshape(b, 1, RWKV_COLS), wkv0, w["rwkv"], w["gmat"],
                                       chunk, max(1, min(RWKV_CHUNKS_PER_STEP, t // chunk)))
    return y_rwkv, y_fox, gate, wkv_new, shift_new, k, v, logf


def _layer(paths, w):
    n_b = [p[0].shape[0] for p in paths]
    mod_all = _ada(jnp.concatenate([p[1] for p in paths], axis=0), w["w_ada"], w["b_ada"])
    mods, o = [], 0
    for nb in n_b:
        mods.append(mod_all[o:o + nb])
        o += nb
    n_total = sum(p[0].shape[0] * p[0].shape[1] for p in paths)
    mixed, x1s = [], []
    shared, row = None, 0
    for (x, _, shift0, wkv0, pk, pv, plf), mod in zip(paths, mods):
        ya, yb, gate, wkv_new, shift_new, k, v, logf = _mix_path(x, mod, shift0, wkv0, pk, pv, plf, w)
        x1, h2_all, lg_all = _merge(x, ya, yb, gate, mod, w, n_total, row, shared)
        shared = (h2_all, lg_all)
        row += x.shape[0] * x.shape[1]
        mixed.append((wkv_new, shift_new, k, v, logf))
        x1s.append(x1)
    y_tok, wts = _moe_routed(h2_all, lg_all, w)
    outs, row = [], 0
    for x1, mod, st in zip(x1s, mods, mixed):
        y = _final(x1, h2_all, wts, y_tok, mod, w, row)
        row += x1.shape[0] * x1.shape[1]
        outs.append((y,) + st)
    return outs


def kernel(x_prompt, x_sample, c_prompt, c_sample, state_rwkv_wkv, state_rwkv_shift, cache_fox_k, cache_fox_v,
           cache_fox_logf, w_ada, b_ada, norm1_g, norm2_g, w_in, rwkv_mu, rwkv_w0, rwkv_w_lora_b, rwkv_a0,
           rwkv_a_lora_b, rwkv_g_lora_b, rwkv_k_k, rwkv_k_a, rwkv_r_k, rwkv_ln_w, rwkv_ln_b, fox_q_norm,
           fox_k_norm, fox_f_bias, w_out_rwkv, w_out_fox, w_out, w_router, router_bias, w_exp_gate, w_exp_up,
           w_exp_down, w_sh_gate, w_sh_up, w_sh_down):
    raw = dict(w_ada=w_ada, b_ada=b_ada, norm1_g=norm1_g, norm2_g=norm2_g, w_in=w_in, rwkv_mu=rwkv_mu,
               rwkv_w0=rwkv_w0, rwkv_w_lora_b=rwkv_w_lora_b, rwkv_a0=rwkv_a0, rwkv_a_lora_b=rwkv_a_lora_b,
               rwkv_g_lora_b=rwkv_g_lora_b, rwkv_k_k=rwkv_k_k, rwkv_k_a=rwkv_k_a, rwkv_r_k=rwkv_r_k,
               rwkv_ln_w=rwkv_ln_w, rwkv_ln_b=rwkv_ln_b, fox_q_norm=fox_q_norm, fox_k_norm=fox_k_norm,
               fox_f_bias=fox_f_bias, w_out_rwkv=w_out_rwkv, w_out_fox=w_out_fox, w_out=w_out,
               w_router=w_router, router_bias=router_bias, w_exp_gate=w_exp_gate, w_exp_up=w_exp_up,
               w_exp_down=w_exp_down, w_sh_gate=w_sh_gate, w_sh_up=w_sh_up, w_sh_down=w_sh_down)
    assert w_in.shape[0] == 1, "single-layer stack"
    w = _prep(raw)
    bp, tp, _ = x_prompt.shape
    bs, ts, _ = x_sample.shape
    n_past = cache_fox_k.shape[2]
    prompt = (x_prompt, c_prompt, jnp.zeros((bp, RWKV_COLS), F32),
              jnp.zeros((bp, N_HEADS, HEAD_DIM, HEAD_DIM), F32),
              jnp.zeros((bp, 0, WIDTH), F32), jnp.zeros((bp, 0, WIDTH), F32), jnp.zeros((bp, 0, N_HEADS), F32))
    sample = (x_sample, c_sample, state_rwkv_shift[0], state_rwkv_wkv[0],
              cache_fox_k[0].reshape(bs, n_past, WIDTH), cache_fox_v[0].reshape(bs, n_past, WIDTH),
              cache_fox_logf[0])
    (yp, wkv_p, sh_p, k_p, v_p, lf_p), (ysm, wkv_s, sh_s, k_s, v_s, lf_s) = _layer([prompt, sample], w)

    def heads(a):
        return a.reshape((1,) + a.shape[:2] + (N_HEADS, HEAD_DIM))

    return (yp, ysm,
            wkv_p[None], sh_p.reshape(1, bp, RWKV_COLS), heads(k_p), heads(v_p), lf_p[None],
            wkv_s[None], sh_s.reshape(1, bs, RWKV_COLS), heads(k_s), heads(v_s), lf_s[None])
```

```python
import functools
import math

import jax
import jax.numpy as jnp
from jax import lax
from jax.experimental import pallas as pl
from jax.experimental.pallas import tpu as pltpu
from jax.experimental.pallas import tpu_sc as plsc

F32 = jnp.float32
BF16 = jnp.bfloat16
I32 = jnp.int32

D_MODEL = 1024
N_HEADS = 8
HEAD_DIM = 64
WIDTH = N_HEADS * HEAD_DIM
HEADS_PER_GROUP = 4
RWKV_CHUNK = 64
RWKV_CHUNKS_PER_STEP = 4
EXPERT_BLOCK_ROWS = 512
SC_SCATTER_WINDOW = 128
SC_ROW_SPLIT = 2
DECAY_LORA = 64
ICLR_LORA = 64
GATE_LORA = 128
RWKV_COLS = 3 * WIDTH + DECAY_LORA + ICLR_LORA + GATE_LORA
FOX_MAIN_COLS = 4 * WIDTH
GATE_COLS = 2 * D_MODEL
RWKV_GN_EPS = HEAD_DIM * 1e-5
L2_EPS = 1e-12
RMS_EPS = 1e-6
N_EXPERTS = 256
TOP_K = 8
N_GROUPS = 8
TOPK_GROUPS = 4
EXPERTS_PER_GROUP = N_EXPERTS // N_GROUPS
D_EXPERT = 256
ROUTED_SCALE = 2.5

LANES = 128
VMEM_LIMIT = 56 * 1024 * 1024
NEG_BIG = -1e30

NN = (((1,), (0,)), ((), ()))
NT = (((1,), (1,)), ((), ()))
TN = (((0,), (0,)), ((), ()))


def _cparams(sem):
    return pltpu.CompilerParams(dimension_semantics=sem, vmem_limit_bytes=VMEM_LIMIT)


def _dot(a, b, dims=NN):
    return lax.dot_general(a.astype(BF16), b.astype(BF16), dims, preferred_element_type=F32)


def _split2(a):
    hi = a.astype(BF16)
    lo = (a - hi.astype(F32)).astype(BF16)
    return hi, lo


def _split3(a):
    hi = a.astype(BF16)
    r1 = a - hi.astype(F32)
    mid = r1.astype(BF16)
    lo = (r1 - mid.astype(F32)).astype(BF16)
    return hi, mid, lo


def _mm3(a, b, dims):
    d = functools.partial(lax.dot_general, dimension_numbers=dims, preferred_element_type=F32)
    return d(a[0], b[0]) + (d(a[0], b[1]) + d(a[1], b[0]))


def _dot_exact_rhs(a_exact, b, dims=NN):
    ab = a_exact.astype(BF16)
    bh, bm, bl = _split3(b)
    d = functools.partial(lax.dot_general, dimension_numbers=dims, preferred_element_type=F32)
    return d(ab, bh) + (d(ab, bm) + d(ab, bl))


def _gsum(x, g_ref):
    hi, mid, lo = _split3(x)
    g = g_ref[...]
    d = functools.partial(jnp.dot, preferred_element_type=F32)
    return d(hi, g) + (d(mid, g) + d(lo, g))


def _sigmoid(x):
    return 1.0 / (1.0 + jnp.exp(-x))


def _softplus(x):
    return jnp.maximum(x, 0.0) + jnp.log1p(jnp.exp(-jnp.abs(x)))


def _silu(x):
    return x * _sigmoid(x)


def _pack_bf16_pairs(x):
    w = x.shape[1] // 2
    bits = lax.bitcast_convert_type(x.astype(BF16).astype(F32), I32)
    return lax.shift_right_logical(bits[:, :w], 16) | (bits[:, w:] & -65536)


def _unpack_bf16_pairs(p):
    lo = lax.bitcast_convert_type(lax.shift_left(p, 16), F32)
    hi = lax.bitcast_convert_type(p & -65536, F32)
    return jnp.concatenate([lo, hi], axis=1).astype(BF16)


def _group_ones():
    h = jnp.arange(WIDTH, dtype=I32) // HEAD_DIM
    return (h[:, None] == h[None, :]).astype(BF16)


def _ada_kernel(c_ref, w_ref, b_ref, o_ref):
    o_ref[...] = _dot(_silu(c_ref[...]), w_ref[...]) + b_ref[...]


def _ada(c, w_ada, b_ada):
    nb = c.shape[0]
    n_out = w_ada.shape[1]
    blk = D_MODEL
    return pl.pallas_call(
        _ada_kernel,
        grid=(n_out // blk,),
        in_specs=[pl.BlockSpec((nb, D_MODEL), lambda j: (0, 0)),
                  pl.BlockSpec((D_MODEL, blk), lambda j: (0, j)),
                  pl.BlockSpec((1, blk), lambda j: (0, j))],
        out_specs=pl.BlockSpec((nb, blk), lambda j: (0, j)),
        out_shape=jax.ShapeDtypeStruct((nb, n_out), F32),
        compiler_params=_cparams(("parallel",)),
        name="ada_mod",
    )(c, w_ada, b_ada.reshape(1, n_out))


def _inproj_kernel(x_ref, mod_ref, g1_ref, wr_ref, wf_ref, wfl_ref, wg_ref, qn_ref, kn_ref, fb_ref, gm_ref, f0_ref,
                   pr_ref, q_ref, k_ref, v_ref, sg_ref, lf_ref, cf_ref, gate_ref, carry):
    bb, tt, d = x_ref.shape
    m = bb * tt
    x = x_ref[...]
    ms = jnp.mean(x * x, axis=-1, keepdims=True)
    h = x * lax.rsqrt(ms + RMS_EPS) * g1_ref[...]
    h = h * (1.0 + mod_ref[:, 1:2, :]) + mod_ref[:, 0:1, :]
    hb = h.reshape(m, d).astype(BF16)

    pr_ref[...] = jnp.dot(hb, wr_ref[...], preferred_element_type=F32).reshape(bb, tt, RWKV_COLS)

    f = jnp.dot(hb, wf_ref[...], preferred_element_type=F32)
    q = f[:, 0:WIDTH]
    k = f[:, WIDTH:2 * WIDTH]
    v = f[:, 2 * WIDTH:3 * WIDTH]
    og = f[:, 3 * WIDTH:4 * WIDTH]
    inv_hd = 1.0 / HEAD_DIM
    q = q * lax.rsqrt(_gsum(q * q, gm_ref) * inv_hd + RMS_EPS) * qn_ref[...]
    k = k * lax.rsqrt(_gsum(k * k, gm_ref) * inv_hd + RMS_EPS) * kn_ref[...]
    q_ref[...] = (q * (HEAD_DIM ** -0.5)).astype(BF16).reshape(bb, tt, WIDTH)
    k_ref[...] = k.reshape(bb, tt, WIDTH)
    v_ref[...] = v.reshape(bb, tt, WIDTH)
    sg_ref[...] = _sigmoid(og).reshape(bb, tt, WIDTH)

    fl = jnp.dot(hb, wfl_ref[...], preferred_element_type=F32)[:, 0:N_HEADS] + fb_ref[...]
    lf = -_softplus(-fl)
    lf_ref[...] = lf.reshape(bb, tt, N_HEADS)

    @pl.when(pl.program_id(1) == 0)
    def _():
        carry[...] = f0_ref[...]

    r = lax.broadcasted_iota(I32, (m, m), 0)
    c = lax.broadcasted_iota(I32, (m, m), 1)
    tri = jnp.logical_and(r // tt == c // tt, r >= c).astype(F32)
    cf = _dot_exact_rhs(tri, lf).reshape(bb, tt, N_HEADS) + carry[...]
    cf_ref[...] = cf
    carry[...] = cf[:, tt - 1:tt, :]

    gate_ref[...] =_sigmoid(jnp.dot(hb, wg_ref[...], preferred_element_type=F32)).reshape(bb, tt, GATE_COLS)


def _const_spec(shape):
    nd = len(shape)
    return pl.BlockSpec(shape, lambda *_: (0,) * nd)


def _inproj(x, mod, g1, wr, wf, wfl, wg, qn, kn, fb, gmat, f0, bb, tt):
    b, t, d = x.shape
    grid = (b // bb, t // tt)

    def tok(cols):
        return pl.BlockSpec((bb, tt, cols), lambda i, j: (i, j, 0))

    out_cols = [(RWKV_COLS, F32), (WIDTH, BF16), (WIDTH, F32), (WIDTH, F32), (WIDTH, F32), (N_HEADS, F32),
                (N_HEADS, F32), (GATE_COLS, F32)]
    return pl.pallas_call(
        _inproj_kernel,
        grid=grid,
        in_specs=[tok(d),
                  pl.BlockSpec((bb, 6, d), lambda i, j: (i, 0, 0)),
                  _const_spec((1, d)),
                  _const_spec(wr.shape), _const_spec(wf.shape), _const_spec(wfl.shape), _const_spec(wg.shape),
                  _const_spec((1, WIDTH)), _const_spec((1, WIDTH)), _const_spec((1, N_HEADS)),
                  _const_spec((WIDTH, WIDTH)),
                  pl.BlockSpec((bb, 1, N_HEADS), lambda i, j: (i, 0, 0))],
        out_specs=[tok(c) for c, _ in out_cols],
        out_shape=[jax.ShapeDtypeStruct((b, t, c), dt) for c, dt in out_cols],
        scratch_shapes=[pltpu.VMEM((bb, 1, N_HEADS), F32)],
        compiler_params=_cparams(("parallel", "arbitrary")),
        name="norm1_inproj",
    )(x, mod, g1, wr, wf, wfl, wg, qn, kn, fb, gmat, f0)


def _past_cumsum_kernel(x_ref, o_ref):
    x = x_ref[0]
    rows = x.shape[0]
    li = lax.broadcasted_iota(I32, (LANES, LANES), 0)
    lj = lax.broadcasted_iota(I32, (LANES, LANES), 1)
    same_head = (li % N_HEADS) == (lj % N_HEADS)
    within = jnp.logical_and(same_head, li // N_HEADS <= lj // N_HEADS).astype(BF16)
    xh, xm, xl = _split3(x)
    d2 = functools.partial(jnp.dot, preferred_element_type=F32)
    in_row = d2(xh, within) + (d2(xm, within) + d2(xl, within))
    sh = same_head.astype(BF16)
    row_tot = d2(xh, sh) + (d2(xm, sh) + d2(xl, sh))
    ri = lax.broadcasted_iota(I32, (rows, rows), 0)
    ci = lax.broadcasted_iota(I32, (rows, rows), 1)
    o_ref[0] = in_row + _dot_exact_rhs((ri > ci).astype(F32), row_tot)


def _past_cumsum(past_logf):
    b, p, h = past_logf.shape
    rows = p * h // LANES
    flat = past_logf.reshape(b, rows, LANES)
    out = pl.pallas_call(
        _past_cumsum_kernel,
        grid=(b,),
        in_specs=[pl.BlockSpec((1, rows, LANES), lambda i: (i, 0, 0))],
        out_specs=pl.BlockSpec((1, rows, LANES), lambda i: (i, 0, 0)),
        out_shape=jax.ShapeDtypeStruct((b, rows, LANES), F32),
        compiler_params=_cparams(("parallel",)),
        name="cache_logf_cumsum",
    )(flat)
    return out.reshape(b, p, h)


def _fox_kernel(*refs, n_past_blocks, tq):
    if n_past_blocks:
        (q_ref, fq_ref, sg_ref, kp_ref, vp_ref, fkp_ref, kn_ref, vn_ref, fkn_ref,
         o_ref, m_scr, l_scr, acc_scr) = refs
    else:
        q_ref, fq_ref, sg_ref, kn_ref, vn_ref, fkn_ref, o_ref, m_scr, l_scr, acc_scr = refs
    qi = pl.program_id(1)
    ki = pl.program_id(2)
    nk = pl.num_programs(2)

    @pl.when(ki == 0)
    def _():
        m_scr[...] = jnp.full(m_scr.shape, NEG_BIG, F32)
        l_scr[...] = jnp.zeros(l_scr.shape, F32)
        acc_scr[...] = jnp.zeros(acc_scr.shape, F32)

    lane_a = lax.broadcasted_iota(I32, (tq, LANES), 1) < HEAD_DIM

    def step(k_ref, v_ref, fk_ref, diag):
        tk = k_ref.shape[1]
        if diag:
            rq = lax.broadcasted_iota(I32, (tq, tk), 0)
            ck = lax.broadcasted_iota(I32, (tq, tk), 1)
            visible = ck <= rq
        fq_all = fq_ref[0]
        pairs = range(N_HEADS // 2)
        cols = [slice(j * LANES, (j + 1) * LANES) for j in pairs]
        scores = []
        for j in pairs:
            qj = q_ref[0, :, cols[j]]
            kb = k_ref[0, :, cols[j]].astype(BF16)
            for hh in range(2):
                h = 2 * j + hh
                qm = jnp.where(lane_a if hh == 0 else jnp.logical_not(lane_a), qj, jnp.zeros_like(qj))
                s = lax.dot_general(qm, kb, NT, preferred_element_type=F32)
                s = s + fq_all[:, h:h + 1] - fk_ref[0, h:h + 1, :]
                if diag:
                    s = jnp.where(visible, s, NEG_BIG)
                scores.append(s)
        alphas, probs = [], []
        for h in range(N_HEADS):
            m_old = m_scr[h]
            m_new = jnp.maximum(m_old, jnp.max(scores[h], axis=-1, keepdims=True))
            alpha = jnp.exp(m_old - m_new)
            p = jnp.exp(scores[h] - m_new)
            l_scr[h] = alpha * l_scr[h] + jnp.sum(p, axis=-1, keepdims=True)
            m_scr[h] = m_new
            alphas.append(alpha)
            probs.append(p.astype(BF16))
        for j in pairs:
            vb = v_ref[0, :, cols[j]].astype(BF16)
            pv0 = jnp.dot(probs[2 * j], vb, preferred_element_type=F32)
            pv1 = jnp.dot(probs[2 * j + 1], vb, preferred_element_type=F32)
            acc_scr[:, cols[j]] = (acc_scr[:, cols[j]] * jnp.where(lane_a, alphas[2 * j], alphas[2 * j + 1])
                                   + jnp.where(lane_a, pv0, pv1))

    if n_past_blocks:
        @pl.when(ki < n_past_blocks)
        def _():
            step(kp_ref, vp_ref, fkp_ref, False)

    kn = ki - n_past_blocks

    @pl.when(jnp.logical_and(kn >= 0, kn < qi))
    def _():
        step(kn_ref, vn_ref, fkn_ref, False)

    @pl.when(kn == qi)
    def _():
        step(kn_ref, vn_ref, fkn_ref, True)

    @pl.when(ki == nk - 1)
    def _():
        for j in range(N_HEADS // 2):
            cols = slice(j * LANES, (j + 1) * LANES)
            l = jnp.where(lane_a, l_scr[2 * j], l_scr[2 * j + 1])
            o_ref[0, :, cols] = acc_scr[:, cols] / l * sg_ref[0, :, cols]


def _fox_attention(q, fq, sg, k_new, v_new, fk_new_t, past=None, tq=512, tk_past=512):
    b, t, _ = q.shape
    nq = t // tq
    n_past_blocks = 0 if past is None else past[0].shape[1] // tk_past
    nk = n_past_blocks + nq

    def new_idx(i, qi, ki):
        return jnp.clip(ki - n_past_blocks, 0, qi)

    in_specs = [pl.BlockSpec((1, tq, WIDTH), lambda i, qi, ki: (i, qi, 0)),
                pl.BlockSpec((1, tq, N_HEADS), lambda i, qi, ki: (i, qi, 0)),
                pl.BlockSpec((1, tq, WIDTH), lambda i, qi, ki: (i, qi, 0))]
    args = [q, fq, sg]
    if n_past_blocks:
        def past_idx(i, qi, ki):
            return jnp.minimum(ki, n_past_blocks - 1)
        in_specs += [pl.BlockSpec((1, tk_past, WIDTH), lambda i, qi, ki: (i, past_idx(i, qi, ki), 0)),
                     pl.BlockSpec((1, tk_past, WIDTH), lambda i, qi, ki: (i, past_idx(i, qi, ki), 0)),
                     pl.BlockSpec((1, N_HEADS, tk_past), lambda i, qi, ki: (i, 0, past_idx(i, qi, ki)))]
        args += list(past)
    in_specs += [pl.BlockSpec((1, tq, WIDTH), lambda i, qi, ki: (i, new_idx(i, qi, ki), 0)),
                 pl.BlockSpec((1, tq, WIDTH), lambda i, qi, ki: (i, new_idx(i, qi, ki), 0)),
                 pl.BlockSpec((1, N_HEADS, tq), lambda i, qi, ki: (i, 0, new_idx(i, qi, ki)))]
    args += [k_new, v_new, fk_new_t]
    return pl.pallas_call(
        functools.partial(_fox_kernel, n_past_blocks=n_past_blocks, tq=tq),
        grid=(b, nq, nk),
        in_specs=in_specs,
        out_specs=pl.BlockSpec((1, tq, WIDTH), lambda i, qi, ki: (i, qi, 0)),
        out_shape=jax.ShapeDtypeStruct((b, t, WIDTH), F32),
        scratch_shapes=[pltpu.VMEM((N_HEADS, tq, 1), F32), pltpu.VMEM((N_HEADS, tq, 1), F32),
                        pltpu.VMEM((tq, WIDTH), F32)],
        compiler_params=_cparams(("parallel", "parallel", "arbitrary")),
        name="fox_attention",
    )(*args)


def _rwkv_kernel(p_ref, sh0_ref, s0_ref, mu_ref, w0_ref, wb_ref, a0_ref, ab_ref, gb_ref, kk_ref, ka_ref, rk_ref,
                 lnw_ref, lnb_ref, gm_ref, y_ref, st_ref, sht_ref, z_scr, prev_scr, *, c):
    t = pl.program_id(1)
    nt = pl.num_programs(1)
    n_rows = p_ref.shape[1]
    n_chunks = n_rows // c

    def head_block(h):
        lo = (h % HEADS_PER_GROUP) * HEAD_DIM
        return h // HEADS_PER_GROUP, slice(lo, lo + HEAD_DIM)

    @pl.when(t == 0)
    def _():
        z_scr[...] = jnp.zeros(z_scr.shape, F32)
        for h in range(N_HEADS):
            i, blk = head_block(h)
            z_scr[i, blk, blk] = s0_ref[0, h]
        prev_scr[...] = sh0_ref[0]

    p = p_ref[0]
    row = lax.broadcasted_iota(I32, p.shape, 0)
    prev = jnp.where(row == 0, prev_scr[...], pltpu.roll(p, 1, 0))
    last = p[n_rows - 1:n_rows, :]
    prev_scr[...] = last
    sht_ref[0] = last

    pm = p + (prev - p) * mu_ref[...]
    r = pm[:, 0:WIDTH]
    k = pm[:, WIDTH:2 * WIDTH]
    v = pm[:, 2 * WIDTH:3 * WIDTH]
    o1 = 3 * WIDTH
    wd = pm[:, o1:o1 + DECAY_LORA]
    ad = pm[:, o1 + DECAY_LORA:o1 + DECAY_LORA + ICLR_LORA]
    gd = pm[:, o1 + DECAY_LORA + ICLR_LORA:RWKV_COLS]

    w = -_softplus(-(w0_ref[...] + _dot(jnp.tanh(wd), wb_ref[...]))) - 0.5
    lw = -jnp.exp(w)
    a = _sigmoid(a0_ref[...] + _dot(ad, ab_ref[...]))
    g = _dot(_sigmoid(gd), gb_ref[...])
    kk = k * kk_ref[...]
    kk = kk / jnp.maximum(jnp.sqrt(_gsum(kk * kk, gm_ref)), L2_EPS)
    kf = k * (1.0 + (a - 1.0) * ka_ref[...])

    ri = lax.broadcasted_iota(I32, (n_rows, n_rows), 0)
    ci = lax.broadcasted_iota(I32, (n_rows, n_rows), 1)
    same_chunk = (ri // c) == (ci // c)
    cum = _dot_exact_rhs(jnp.logical_and(same_chunk, ri >= ci).astype(F32), lw)
    cum_last = _dot_exact_rhs(same_chunk.astype(F32), lw)
    r_t = r * jnp.exp(cum)
    a_t = -kk * jnp.exp(cum - lw)
    inv = jnp.exp(-cum)
    b_t = kk * a * inv
    k_t = kf * inv
    to_end = jnp.exp(cum_last - cum)
    b_e = kk * a * to_end
    k_e = kf * to_end
    g_end = jnp.exp(cum_last)

    hg = HEADS_PER_GROUP
    gw = hg * HEAD_DIM
    log_c = int(math.log2(c))
    t_idx = lax.broadcasted_iota(I32, (c, hg * c), 0)
    s_idx = lax.broadcasted_iota(I32, (c, hg * c), 1) & (c - 1)
    strict = s_idx < t_idx
    lower = s_idx <= t_idx
    eye = (s_idx == t_idx).astype(F32)
    rb = lax.broadcasted_iota(I32, (hg * c, gw), 0) >> log_c
    mask_kv = rb == (lax.broadcasted_iota(I32, (hg * c, gw), 1) >> int(math.log2(HEAD_DIM)))
    rs = lax.broadcasted_iota(I32, (hg * c, hg * c), 0) >> log_c
    mask_ss = rs == (lax.broadcasted_iota(I32, (hg * c, hg * c), 1) >> log_c)
    ng = N_HEADS // hg
    cat = functools.partial(jnp.concatenate, axis=0)
    units = [(slice(j * c, (j + 1) * c), slice(i * gw, (i + 1) * gw)) for j in range(n_chunks) for i in range(ng)]
    nu = len(units)

    def mm1(a, b_bd):
        return jnp.dot(a.astype(BF16), b_bd, preferred_element_type=F32)

    def dg(a, b, dims):
        return lax.dot_general(a.astype(BF16), b.astype(BF16), dims, preferred_element_type=F32)

    def bd1(x, mask):
        tiled = jnp.concatenate([x.astype(BF16)] * hg, axis=0)
        return jnp.where(mask, tiled, jnp.zeros_like(tiled))

    ar = [cat([a_t[rs_, s], r_t[rs_, s]]).astype(BF16) for rs_, s in units]
    ab = [dg(ar[n], bd1(b_t[units[n]], mask_kv), NT) for n in range(nu)]
    ak = [dg(ar[n], bd1(k_t[units[n]], mask_kv), NT) for n in range(nu)]
    l_ab = [jnp.where(strict, m[:c], 0.0) for m in ab]
    l_rb = [jnp.where(lower, m[c:], 0.0) for m in ab]
    l_ak = [jnp.where(strict, m[:c], 0.0) for m in ak]
    l_rk = [jnp.where(lower, m[c:], 0.0) for m in ak]
    tinv = [eye + m for m in l_ab]
    pw = [mm1(m, bd1(m, mask_ss)) for m in l_ab]
    for _ in range(1, log_c - 1):
        res = [mm1(cat([tinv[n], pw[n]]), bd1(pw[n], mask_ss)) for n in range(nu)]
        tinv = [tinv[n] + res[n][:c] for n in range(nu)]
        pw = [m[c:] for m in res]
    tinv = [tinv[n] + mm1(tinv[n], bd1(pw[n], mask_ss)) for n in range(nu)]
    av = [mm1(cat([l_ak[n], l_rk[n]]), bd1(v[units[n]], mask_kv)) for n in range(nu)]
    ue = [cat([b_e[units[n]], k_e[units[n]]]).astype(BF16) for n in range(nu)]

    def wide(fn, x):
        return [fn(x[:, :gw]), fn(x[:, gw:])]

    def bd1w(x):
        return jnp.concatenate(wide(lambda h_: bd1(h_, mask_kv), x), axis=1)

    rhs = [jnp.concatenate([a_t[units[n]], av[n][:c]], axis=1) for n in range(nu)]
    x0 = [mm1(tinv[n], bd1w(rhs[n])) for n in range(nu)]
    resid = [rhs[n] - (x0[n] - mm1(l_ab[n], bd1w(x0[n]))) for n in range(nu)]
    sol = [x0[n] + mm1(tinv[n], bd1w(resid[n])) for n in range(nu)]
    lift = [mm1(l_rb[n], bd1w(sol[n])) for n in range(nu)]
    lhs_s = [cat([sol[n][:, :gw], r_t[units[n]] + lift[n][:, :gw]]).astype(BF16) for n in range(nu)]
    u_loc = [sol[n][:, gw:] for n in range(nu)]
    o_loc = [av[n][c:] + lift[n][:, gw:] for n in range(nu)]

    zr = lax.broadcasted_iota(I32, (gw, gw), 0) >> int(math.log2(HEAD_DIM))
    zmask = zr == (lax.broadcasted_iota(I32, (gw, gw), 1) >> int(math.log2(HEAD_DIM)))
    z = [z_scr[i] for i in range(ng)]
    o_rows = []
    for j in range(n_chunks):
        o_grp = []
        for i in range(ng):
            n = j * ng + i
            rs_, s = units[n]
            sz = dg(lhs_s[n], z[i], NT)
            u = sz[:c] + u_loc[n]
            o_grp.append(sz[c:] + o_loc[n])
            upd = dg(cat([u, v[rs_, s]]), ue[n], TN)
            z[i] = z[i] * g_end[j * c:j * c + 1, s] + jnp.where(zmask, upd, 0.0)
        o_rows.append(jnp.concatenate(o_grp, axis=1))
    for i in range(ng):
        z_scr[i] = z[i]

    o = cat(o_rows)
    inv_hd = 1.0 / HEAD_DIM
    dlt = o - _gsum(o, gm_ref) * inv_hd
    var = _gsum(dlt * dlt, gm_ref) * inv_hd
    on = dlt * lax.rsqrt(var + RWKV_GN_EPS) * lnw_ref[...] + lnb_ref[...]
    bonus = _gsum(r * kf * rk_ref[...], gm_ref) * v
    y_ref[0] = (on + bonus) * g

    @pl.when(t == nt - 1)
    def _():
        for h in range(N_HEADS):
            i, blk = head_block(h)
            st_ref[0, h] = z_scr[i, blk, blk]


def _rwkv(p, shift0, s0, prm, gmat, chunk, chunks_per_step):
    b, t, _ = p.shape
    row = lambda n: _const_spec((1, n))
    rows = chunk * chunks_per_step
    return pl.pallas_call(
        functools.partial(_rwkv_kernel, c=chunk),
        grid=(b, t // rows),
        in_specs=[pl.BlockSpec((1, rows, RWKV_COLS), lambda i, j: (i, j, 0)),
                  pl.BlockSpec((1, 1, RWKV_COLS), lambda i, j: (i, 0, 0)),
                  pl.BlockSpec((1, N_HEADS, HEAD_DIM, HEAD_DIM), lambda i, j: (i, 0, 0, 0)),
                  row(RWKV_COLS), row(WIDTH), _const_spec((DECAY_LORA, WIDTH)), row(WIDTH),
                  _const_spec((ICLR_LORA, WIDTH)), _const_spec((GATE_LORA, WIDTH)),
                  row(WIDTH), row(WIDTH), row(WIDTH), row(WIDTH), row(WIDTH), _const_spec((WIDTH, WIDTH))],
        out_specs=[pl.BlockSpec((1, rows, WIDTH), lambda i, j: (i, j, 0)),
                   pl.BlockSpec((1, N_HEADS, HEAD_DIM, HEAD_DIM), lambda i, j: (i, 0, 0, 0)),
                   pl.BlockSpec((1, 1, RWKV_COLS), lambda i, j: (i, 0, 0))],
        out_shape=[jax.ShapeDtypeStruct((b, t, WIDTH), F32),
                   jax.ShapeDtypeStruct((b, N_HEADS, HEAD_DIM, HEAD_DIM), F32),
                   jax.ShapeDtypeStruct((b, 1, RWKV_COLS), F32)],
        scratch_shapes=[pltpu.VMEM((N_HEADS // HEADS_PER_GROUP, HEADS_PER_GROUP * HEAD_DIM,
                                    HEADS_PER_GROUP * HEAD_DIM), F32),
                        pltpu.VMEM((1, RWKV_COLS), F32)],
        compiler_params=_cparams(("parallel", "arbitrary")),
        name="rwkv7_mix",
    )(p, shift0, s0, prm["mu"], prm["w0"], prm["wb"], prm["a0"], prm["ab"], prm["gb"], prm["kk"], prm["ka"],
      prm["rk"], prm["lnw"], prm["lnb"], gmat)


def _merge_kernel(x_ref, ya_ref, yb_ref, gate_ref, mod_ref, g2_ref, woa_ref, wob_ref, wo_ref, wrh_ref, wrl_ref,
                  *rest):
    x1_ref, h2_ref, lg_ref = rest[-3:]
    bb, tt, d = x_ref.shape
    m = bb * tt
    gate = gate_ref[...].reshape(m, GATE_COLS)
    merged = (gate[:, 0:d] * _dot(ya_ref[...].reshape(m, WIDTH), woa_ref[...])
              + gate[:, d:2 * d] * _dot(yb_ref[...].reshape(m, WIDTH), wob_ref[...]))
    x1 = x_ref[...] + mod_ref[:, 2:3, :] * _dot(merged, wo_ref[...]).reshape(bb, tt, d)
    x1_ref[...] = x1
    ms = jnp.mean(x1 * x1, axis=-1, keepdims=True)
    h2 = x1 * lax.rsqrt(ms + RMS_EPS) * g2_ref[...]
    h2 = (h2 * (1.0 + mod_ref[:, 4:5, :]) + mod_ref[:, 3:4, :]).reshape(m, d)
    h2_ref[...] = _pack_bf16_pairs(h2)
    lg_ref[...] = _mm3((wrh_ref[...], wrl_ref[...]), _split2(h2), NT)


def _merge(x, ya, yb, gate, mod, w, n_total, row_offset, shared=None):
    b, t, d = x.shape
    bb, tt = _token_blocks(b, t)
    nt = t // tt
    m = bb * tt
    off = row_offset // m

    def tok(cols):
        return pl.BlockSpec((bb, tt, cols), lambda i, j: (i, j, 0))

    in_specs = [tok(d), tok(WIDTH), tok(WIDTH), tok(GATE_COLS),
                pl.BlockSpec((bb, 6, d), lambda i, j: (i, 0, 0)),
                _const_spec((1, d)), _const_spec((WIDTH, d)), _const_spec((WIDTH, d)), _const_spec((d, d)),
                _const_spec((N_EXPERTS, d)), _const_spec((N_EXPERTS, d))]
    args = [x, ya, yb, gate, mod.reshape(b, 6, d), w["g2"], w["w_oa"], w["w_ob"], w["w_o"], w["wr_hi"], w["wr_lo"]]
    aliases = {}
    if shared is not None:
        aliases = {len(args): 1, len(args) + 1: 2}
        in_specs += [pl.BlockSpec(memory_space=pl.ANY), pl.BlockSpec(memory_space=pl.ANY)]
        args += list(shared)
    return pl.pallas_call(
        _merge_kernel,
        grid=(b // bb, nt),
        in_specs=in_specs,
        out_specs=[tok(d), pl.BlockSpec((m, d // 2), lambda i, j: (off + i * nt + j, 0)),
                   pl.BlockSpec((N_EXPERTS, m), lambda i, j: (0, off + i * nt + j))],
        out_shape=[jax.ShapeDtypeStruct((b, t, d), F32), jax.ShapeDtypeStruct((n_total, d // 2), I32),
                   jax.ShapeDtypeStruct((N_EXPERTS, n_total), F32)],
        input_output_aliases=aliases,
        compiler_params=_cparams(("parallel", "parallel")),
        name="merge_norm2_router",
    )(*args)


def _route_kernel(lg_ref, bias_ref, idx_ref, wt_ref, rank_ref, cnt_ref, carry):
    @pl.when(pl.program_id(0) == 0)
    def _():
        carry[...] = jnp.zeros(carry.shape, F32)

    tm = lg_ref.shape[1]
    scores = _sigmoid(lg_ref[...])
    sel = scores + bias_ref[...]
    row = lax.broadcasted_iota(I32, (N_EXPERTS, tm), 0)
    neg_inf = -jnp.inf

    def first_argmax(vals, rows):
        mx = jnp.max(vals, axis=0, keepdims=True)
        return mx, jnp.min(jnp.where(vals == mx, rows, N_EXPERTS), axis=0, keepdims=True)

    gslices = [slice(g * EXPERTS_PER_GROUP, (g + 1) * EXPERTS_PER_GROUP) for g in range(N_GROUPS)]
    gs = []
    row_g = lax.broadcasted_iota(I32, (EXPERTS_PER_GROUP, tm), 0)
    for sl in gslices:
        m1, i1 = first_argmax(sel[sl], row_g)
        m2 = jnp.max(jnp.where(row_g == i1, neg_inf, sel[sl]), axis=0, keepdims=True)
        gs.append(m1 + m2)
    kept = []
    for g in range(N_GROUPS):
        beaten = jnp.zeros((1, tm), I32)
        for o in range(N_GROUPS):
            if o != g:
                wins = (gs[o] >= gs[g]) if o < g else (gs[o] > gs[g])
                beaten = beaten + wins.astype(I32)
        kept.append(jnp.where(beaten < TOPK_GROUPS, sel[gslices[g]], neg_inf))
    cur = jnp.concatenate(kept, axis=0)

    idxs, ws = [], []
    picked = jnp.zeros((N_EXPERTS, tm), F32)
    for _ in range(TOP_K):
        _, ik = first_argmax(cur, row)
        hit = row == ik
        idxs.append(ik)
        ws.append(jnp.sum(jnp.where(hit, scores, 0.0), axis=0, keepdims=True))
        cur = jnp.where(hit, neg_inf, cur)
        picked = jnp.where(hit, 1.0, picked)
    wsum = ws[0]
    for k in range(1, TOP_K):
        wsum = wsum + ws[k]

    r = lax.broadcasted_iota(I32, (tm, tm), 0)
    c = lax.broadcasted_iota(I32, (tm, tm), 1)
    before = jnp.dot(picked.astype(BF16), (r < c).astype(BF16), preferred_element_type=F32) + carry[...]
    carry[...] = carry[...] + jnp.sum(picked, axis=1, keepdims=True)
    cnt_ref[...] = carry[...]

    kk = lax.broadcasted_iota(I32, (TOP_K, tm), 0)
    idx_o = jnp.zeros((TOP_K, tm), I32)
    wt_o = jnp.zeros((TOP_K, tm), F32)
    rank_o = jnp.zeros((TOP_K, tm), F32)
    for k in range(TOP_K):
        rk = jnp.sum(jnp.where(row == idxs[k], before, 0.0), axis=0, keepdims=True)
        idx_o = jnp.where(kk == k, idxs[k], idx_o)
        wt_o = jnp.where(kk == k, ws[k] / wsum * ROUTED_SCALE, wt_o)
        rank_o = jnp.where(kk == k, rk, rank_o)
    idx_ref[...] = idx_o
    wt_ref[...] = wt_o
    rank_ref[...] = rank_o.astype(I32)


def _route(logits_t, bias_col, tm):
    n = logits_t.shape[1]
    tokk = pl.BlockSpec((TOP_K, tm), lambda i: (0, i))
    return pl.pallas_call(
        _route_kernel,
        grid=(n // tm,),
        in_specs=[pl.BlockSpec((N_EXPERTS, tm), lambda i: (0, i)), _const_spec((N_EXPERTS, 1))],
        out_specs=[tokk, tokk, tokk, _const_spec((N_EXPERTS, 1))],
        out_shape=[jax.ShapeDtypeStruct((TOP_K, n), I32), jax.ShapeDtypeStruct((TOP_K, n), F32),
                   jax.ShapeDtypeStruct((TOP_K, n), I32), jax.ShapeDtypeStruct((N_EXPERTS, 1), F32)],
        scratch_shapes=[pltpu.VMEM((N_EXPERTS, 1), F32)],
        compiler_params=_cparams(("arbitrary",)),
        name="route_topk",
    )(logits_t, bias_col)


def _plan_kernel(cnt_ref, start_ref, be_ref, valid_ref, nu_ref, *, blk):
    cnt = cnt_ref[...]
    padded = jnp.ceil(cnt * (1.0 / blk)) * blk
    e_r = lax.broadcasted_iota(I32, (N_EXPERTS, N_EXPERTS), 0)
    e_c = lax.broadcasted_iota(I32, (N_EXPERTS, N_EXPERTS), 1)
    incl = (e_r <= e_c).astype(BF16)
    ph, pm, plo = _split3(jnp.broadcast_to(padded, (8, N_EXPERTS)))
    d2 = functools.partial(jnp.dot, preferred_element_type=F32)
    pad_end = (d2(ph, incl) + (d2(pm, incl) + d2(plo, incl)))[0:1, :]
    pad_start = pad_end - padded
    start_ref[...] = pad_start.astype(I32)
    total = jnp.max(pad_end, axis=-1, keepdims=True)
    nu_ref[...] = jnp.broadcast_to(total * (1.0 / blk), (1, N_EXPERTS)).astype(I32)
    nb = be_ref.shape[0]
    first = (lax.broadcasted_iota(I32, (nb, N_EXPERTS), 0) * blk).astype(F32)
    lane = lax.broadcasted_iota(I32, (nb, N_EXPERTS), 1)
    inside = jnp.logical_and(pad_start <= first, first < pad_end)
    be_ref[...] = jnp.sum(jnp.where(inside, lane, 0), axis=-1, keepdims=True)
    rows = jnp.minimum(pad_start + cnt - first, float(blk))
    valid_ref[...] = jnp.sum(jnp.where(inside, rows, 0.0), axis=-1, keepdims=True).astype(I32)


def _plan(counts, n_blocks, blk):
    return pl.pallas_call(
        functools.partial(_plan_kernel, blk=blk),
        out_shape=[jax.ShapeDtypeStruct((1, N_EXPERTS), I32), jax.ShapeDtypeStruct((n_blocks, 1), I32),
                   jax.ShapeDtypeStruct((n_blocks, 1), I32), jax.ShapeDtypeStruct((1, N_EXPERTS), I32)],
        compiler_params=pltpu.CompilerParams(vmem_limit_bytes=VMEM_LIMIT),
        name="dispatch_plan",
    )(counts)


def _dest_kernel(idx_ref, rank_ref, start_ref, dest_ref):
    tm = idx_ref.shape[1]
    row = lax.broadcasted_iota(I32, (N_EXPERTS, tm), 0)
    kk = lax.broadcasted_iota(I32, (TOP_K, tm), 0)
    idx = idx_ref[...]
    base = jnp.zeros((TOP_K, tm), I32)
    for k in range(TOP_K):
        bk = jnp.sum(jnp.where(row == idx[k:k + 1, :], start_ref[...], 0), axis=0, keepdims=True)
        base = jnp.where(kk == k, bk, base)
    dest_ref[...] = base + rank_ref[...]


def _dest(idx, rank, pad_start_col, tm):
    n = idx.shape[1]
    tokk = pl.BlockSpec((TOP_K, tm), lambda i: (0, i))
    return pl.pallas_call(
        _dest_kernel,
        grid=(n // tm,),
        in_specs=[tokk, tokk, _const_spec((N_EXPERTS, 1))],
        out_specs=tokk,
        out_shape=jax.ShapeDtypeStruct((TOP_K, n), I32),
        compiler_params=_cparams(("parallel",)),
        name="dispatch_dest",
    )(idx, rank, pad_start_col)


def _dispatch(h2p, dest_t, n_slots):
    n, wp = h2p.shape
    half = wp // SC_ROW_SPLIT
    window = SC_SCATTER_WINDOW
    mesh = plsc.VectorSubcoreMesh(core_axis_name="core", subcore_axis_name="subcore")
    out = jax.ShapeDtypeStruct((n_slots, half), h2p.dtype)

    @functools.partial(pl.kernel, out_type=[out] * SC_ROW_SPLIT, mesh=mesh, scratch_types=[])
    def scatter(rows_hbm, idx_hbm, *outs):
        for c, out_hbm in enumerate(outs):
            def body(rows_vmem, idx_vmem, out_hbm=out_hbm):
                for k in range(TOP_K):
                    pltpu.sync_copy(rows_vmem, out_hbm.at[idx_vmem.at[k]])

            pltpu.emit_pipeline(
                body, grid=(n // window,),
                in_specs=[pl.BlockSpec((window, half), index_map=lambda i, c=c: (i, c)),
                          pl.BlockSpec((TOP_K, window), index_map=lambda i: (0, i))],
                out_specs=[], core_axis_name=("core", "subcore"), dimension_semantics=(pltpu.PARALLEL,),
            )(rows_hbm, idx_hbm)

    return scatter(h2p, dest_t)


def _expert_kernel(be_ref, valid_ref, nu_ref, xa_ref, xb_ref, wg_ref, wu_ref, wd_ref, *rest):
    y_refs, (wg_b, wu_b, wd_b) = rest[:SC_ROW_SPLIT], rest[SC_ROW_SPLIT:]
    i = pl.program_id(0)
    nv = valid_ref[i]
    new_expert = jnp.logical_or(i == 0, be_ref[i] != be_ref[jnp.maximum(i - 1, 0)])

    @pl.when(jnp.logical_and(nv > 0, new_expert))
    def _():
        wg_b[...] = wg_ref[0].astype(BF16)
        wu_b[...] = wu_ref[0].astype(BF16)
        wd_b[...] = wd_ref[0].astype(BF16)

    @pl.when(nv > 0)
    def _():
        blk = xa_ref.shape[0]
        rows = lax.broadcasted_iota(I32, (blk, 1), 0)
        packed = jnp.concatenate([xa_ref[...], xb_ref[...]], axis=1)
        x = _unpack_bf16_pairs(jnp.where(rows < nv, packed, 0))
        hg = jnp.dot(x, wg_b[...], preferred_element_type=F32)
        hu = jnp.dot(x, wu_b[...], preferred_element_type=F32)
        y = _pack_bf16_pairs(jnp.dot((_silu(hg) * hu).astype(BF16), wd_b[...], preferred_element_type=F32))
        cw = y.shape[1] // SC_ROW_SPLIT
        for c, y_ref in enumerate(y_refs):
            y_ref[...] = y[:, c * cw:(c + 1) * cw]


def _experts(xs, block_e, valid, n_used, w_eg, w_eu, w_ed, blk):
    xa, xb = xs
    n_slots, packed = xa.shape
    d = w_eg.shape[1]
    n_blocks = n_slots // blk

    def row_blk(i, be, valid, nu):
        return (jnp.minimum(i, nu[0] - 1), 0)

    def w_blk(i, be, valid, nu):
        return (be[i], 0, 0)

    return pl.pallas_call(
        _expert_kernel,
        grid_spec=pltpu.PrefetchScalarGridSpec(
            num_scalar_prefetch=3,
            grid=(n_blocks,),
            in_specs=[pl.BlockSpec((blk, packed), row_blk), pl.BlockSpec((blk, packed), row_blk),
                      pl.BlockSpec((1, d, D_EXPERT), w_blk), pl.BlockSpec((1, d, D_EXPERT), w_blk),
                      pl.BlockSpec((1, D_EXPERT, d), w_blk)],
            out_specs=[pl.BlockSpec((blk, packed), row_blk)] * SC_ROW_SPLIT,
            scratch_shapes=[pltpu.VMEM((d, D_EXPERT), BF16), pltpu.VMEM((d, D_EXPERT), BF16),
                            pltpu.VMEM((D_EXPERT, d), BF16)]),
        out_shape=[jax.ShapeDtypeStruct((n_slots, packed), I32)] * SC_ROW_SPLIT,
        compiler_params=_cparams(("arbitrary",)),
        name="moe_experts",
    )(block_e, valid, n_used, xa, xb, w_eg, w_eu, w_ed)


def _combine_gather(ys, dest_t):
    k, n = dest_t.shape
    cw = ys[0].shape[1]
    window = SC_SCATTER_WINDOW
    mesh = plsc.VectorSubcoreMesh(core_axis_name="core", subcore_axis_name="subcore")

    @functools.partial(pl.kernel, out_type=jax.ShapeDtypeStruct((k * n, cw * len(ys)), ys[0].dtype), mesh=mesh,
                       scratch_types=[])
    def gather(*refs):
        y_refs, idx_hbm, out_hbm = refs[:len(ys)], refs[len(ys)], refs[len(ys) + 1]
        for c, y_hbm in enumerate(y_refs):
            def body(idx_vmem, out_vmem, y_hbm=y_hbm):
                pltpu.sync_copy(y_hbm.at[idx_vmem.at[0]], out_vmem)

            pltpu.emit_pipeline(
                body, grid=(k * n // window,),
                in_specs=[pl.BlockSpec((1, window), index_map=lambda i: (0, i))],
                out_specs=[pl.BlockSpec((window, cw), index_map=lambda i, c=c: (i, c))],
                core_axis_name=("core", "subcore"), dimension_semantics=(pltpu.PARALLEL,),
            )(idx_hbm, out_hbm)

    return gather(*ys, dest_t.reshape(1, k * n)).reshape(k, n, cw * len(ys))


def _final_kernel(x1_ref, h2_ref, wt_ref, mod_ref, wsg_ref, wsu_ref, wsd_ref, yg_ref, o_ref):
    bb, tt, d = x1_ref.shape
    hb = _unpack_bf16_pairs(h2_ref[...])
    hg = jnp.dot(hb, wsg_ref[...], preferred_element_type=F32)
    hu = jnp.dot(hb, wsu_ref[...], preferred_element_type=F32)
    ffn = _dot(_silu(hg) * hu, wsd_ref[...])
    wt = wt_ref[...]
    for k in range(TOP_K):
        ffn = ffn + wt[:, k:k + 1] * _unpack_bf16_pairs(yg_ref[k]).astype(F32)
    o_ref[...] = x1_ref[...] + mod_ref[:, 5:6, :] * ffn.reshape(bb, tt, d)


def _final(x1, h2_all, wts_all, y_tok, mod, w, row_offset):
    b, t, d = x1.shape
    bb, tt = _token_blocks(b, t)
    nt = t // tt
    m = bb * tt
    off = row_offset // m

    def flat_idx(i, j):
        return off + i * nt + j

    return pl.pallas_call(
        _final_kernel,
        grid=(b // bb, nt),
        in_specs=[pl.BlockSpec((bb, tt, d), lambda i, j: (i, j, 0)),
                  pl.BlockSpec((m, h2_all.shape[1]), lambda i, j: (flat_idx(i, j), 0)),
                  pl.BlockSpec((m, TOP_K), lambda i, j: (flat_idx(i, j), 0)),
                  pl.BlockSpec((bb, 6, d), lambda i, j: (i, 0, 0)),
                  _const_spec((d, D_EXPERT)), _const_spec((d, D_EXPERT)), _const_spec((D_EXPERT, d)),
                  pl.BlockSpec((TOP_K, m, y_tok.shape[2]), lambda i, j: (0, flat_idx(i, j), 0))],
        out_specs=pl.BlockSpec((bb, tt, d), lambda i, j: (i, j, 0)),
        out_shape=jax.ShapeDtypeStruct((b, t, d), F32),
        compiler_params=_cparams(("parallel", "parallel")),
        name="moe_combine_final",
    )(x1, h2_all, wts_all, mod.reshape(b, 6, d), w["w_sg"], w["w_su"], w["w_sd"], y_tok)


def _moe_routed(h2_all, logits_all, w, blk=EXPERT_BLOCK_ROWS, tm=256):
    n = h2_all.shape[0]
    n_blocks = (n * TOP_K + N_EXPERTS * (blk - 1)) // blk + 1
    n_blocks = (n_blocks + 7) // 8 * 8
    idx, wts_t, rank, counts = _route(logits_all, w["router_bias"], tm)
    pad_start, block_e, valid, n_used = _plan(counts.reshape(1, N_EXPERTS), n_blocks, blk)
    block_e = block_e.reshape(n_blocks)
    valid = valid.reshape(n_blocks)
    n_used = n_used[0, 0:1]
    dest_t = _dest(idx, rank, pad_start.reshape(N_EXPERTS, 1), tm)
    xs = _dispatch(h2_all, dest_t, n_blocks * blk)
    ys = _experts(xs, block_e, valid, n_used, w["w_eg"], w["w_eu"], w["w_ed"], blk)
    return _combine_gather(ys, dest_t), jnp.transpose(wts_t)


def _prep(raw):
    p = {k: v[0] for k, v in raw.items()}
    w_in = p["w_in"]
    o_fox = RWKV_COLS
    o_fl = o_fox + FOX_MAIN_COLS
    o_gate = o_fl + N_HEADS
    row = lambda a: a.reshape(1, -1)
    return dict(
        w_ada=p["w_ada"], b_ada=p["b_ada"],
        g1=row(p["norm1_g"]), g2=row(p["norm2_g"]),
        wr=w_in[:, :o_fox].astype(BF16),
        wf=w_in[:, o_fox:o_fl].astype(BF16),
        wfl=jnp.pad(w_in[:, o_fl:o_gate], ((0, 0), (0, LANES - N_HEADS))).astype(BF16),
        wg=w_in[:, o_gate:].astype(BF16),
        qn=row(jnp.tile(p["fox_q_norm"], N_HEADS)), kn=row(jnp.tile(p["fox_k_norm"], N_HEADS)),
        fb=row(p["fox_f_bias"]),
        gmat=_group_ones(),
        rwkv=dict(mu=row(p["rwkv_mu"]), w0=row(p["rwkv_w0"]), wb=p["rwkv_w_lora_b"], a0=row(p["rwkv_a0"]),
                  ab=p["rwkv_a_lora_b"], gb=p["rwkv_g_lora_b"], kk=row(p["rwkv_k_k"]), ka=row(p["rwkv_k_a"]),
                  rk=row(p["rwkv_r_k"]), lnw=row(p["rwkv_ln_w"]), lnb=row(p["rwkv_ln_b"])),
        w_oa=p["w_out_rwkv"].astype(BF16), w_ob=p["w_out_fox"].astype(BF16), w_o=p["w_out"].astype(BF16),
        wr_hi=p["w_router"].T.astype(BF16),
        wr_lo=(p["w_router"] - p["w_router"].astype(BF16).astype(F32)).T.astype(BF16),
        router_bias=p["router_bias"].reshape(N_EXPERTS, 1),
        w_eg=p["w_exp_gate"], w_eu=p["w_exp_up"], w_ed=p["w_exp_down"],
        w_sg=p["w_sh_gate"].astype(BF16), w_su=p["w_sh_up"].astype(BF16), w_sd=p["w_sh_down"].astype(BF16),
    )


def _token_blocks(b, t):
    if t >= 256:
        return 1, 256
    bb = max(1, min(b, 256 // t))
    while b % bb:
        bb -= 1
    return bb, t


def _mix_path(x, mod, shift0, wkv0, past_k, past_v, past_logf, w):
    b, t, d = x.shape
    bb, tt = _token_blocks(b, t)
    n_past = past_k.shape[1]
    if n_past:
        f_past = _past_cumsum(past_logf)
        init = f_past[:, n_past - 1:n_past, :]
        past = (past_k, past_v, jnp.swapaxes(f_past, 1, 2))
    else:
        init = jnp.zeros((b, 1, N_HEADS), F32)
        past = None
    pr, q, k, v, sg, logf, f_new, gate = _inproj(x, mod.reshape(b, 6, d), w["g1"], w["wr"], w["wf"], w["wfl"],
                                                 w["wg"], w["qn"], w["kn"], w["fb"], w["gmat"], init, bb, tt)
    y_fox = _fox_attention(q, f_new, sg, k, v, jnp.swapaxes(f_new, 1, 2), past=past, tq=min(t, 512),
                           tk_past=min(max(n_past, 1), 512))
    chunk = min(t, RWKV_CHUNK)
    y_rwkv, wkv_new, shift_new = _rwkv(pr, shift0.reshape(b, 1, RWKV_COLS), wkv0, w["rwkv"], w["gmat"],
                                       chunk, max(1, min(RWKV_CHUNKS_PER_STEP, t // chunk)))
    return y_rwkv, y_fox, gate, wkv_new, shift_new, k, v, logf


def _layer(paths, w):
    n_b = [p[0].shape[0] for p in paths]
    mod_all = _ada(jnp.concatenate([p[1] for p in paths], axis=0), w["w_ada"], w["b_ada"])
    mods, o = [], 0
    for nb in n_b:
        mods.append(mod_all[o:o + nb])
        o += nb
    n_total = sum(p[0].shape[0] * p[0].shape[1] for p in paths)
    mixed, x1s = [], []
    shared, row = None, 0
    for (x, _, shift0, wkv0, pk, pv, plf), mod in zip(paths, mods):
        ya, yb, gate, wkv_new, shift_new, k, v, logf = _mix_path(x, mod, shift0, wkv0, pk, pv, plf, w)
        x1, h2_all, lg_all = _merge(x, ya, yb, gate, mod, w, n_total, row, shared)
        shared = (h2_all, lg_all)
        row += x.shape[0] * x.shape[1]
        mixed.append((wkv_new, shift_new, k, v, logf))
        x1s.append(x1)
    y_tok, wts = _moe_routed(h2_all, lg_all, w)
    outs, row = [], 0
    for x1, mod, st in zip(x1s, mods, mixed):
        y = _final(x1, h2_all, wts, y_tok, mod, w, row)
        row += x1.shape[0] * x1.shape[1]
        outs.append((y,) + st)
    return outs


def kernel(x_prompt, x_sample, c_prompt, c_sample, state_rwkv_wkv, state_rwkv_shift, cache_fox_k, cache_fox_v,
           cache_fox_logf, w_ada, b_ada, norm1_g, norm2_g, w_in, rwkv_mu, rwkv_w0, rwkv_w_lora_b, rwkv_a0,
           rwkv_a_lora_b, rwkv_g_lora_b, rwkv_k_k, rwkv_k_a, rwkv_r_k, rwkv_ln_w, rwkv_ln_b, fox_q_norm,
           fox_k_norm, fox_f_bias, w_out_rwkv, w_out_fox, w_out, w_router, router_bias, w_exp_gate, w_exp_up,
           w_exp_down, w_sh_gate, w_sh_up, w_sh_down):
    raw = dict(w_ada=w_ada, b_ada=b_ada, norm1_g=norm1_g, norm2_g=norm2_g, w_in=w_in, rwkv_mu=rwkv_mu,
               rwkv_w0=rwkv_w0, rwkv_w_lora_b=rwkv_w_lora_b, rwkv_a0=rwkv_a0, rwkv_a_lora_b=rwkv_a_lora_b,
               rwkv_g_lora_b=rwkv_g_lora_b, rwkv_k_k=rwkv_k_k, rwkv_k_a=rwkv_k_a, rwkv_r_k=rwkv_r_k,
               rwkv_ln_w=rwkv_ln_w, rwkv_ln_b=rwkv_ln_b, fox_q_norm=fox_q_norm, fox_k_norm=fox_k_norm,
               fox_f_bias=fox_f_bias, w_out_rwkv=w_out_rwkv, w_out_fox=w_out_fox, w_out=w_out,
               w_router=w_router, router_bias=router_bias, w_exp_gate=w_exp_gate, w_exp_up=w_exp_up,
               w_exp_down=w_exp_down, w_sh_gate=w_sh_gate, w_sh_up=w_sh_up, w_sh_down=w_sh_down)
    assert w_in.shape[0] == 1, "single-layer stack"
    w = _prep(raw)
    bp, tp, _ = x_prompt.shape
    bs, ts, _ = x_sample.shape
    n_past = cache_fox_k.shape[2]
    prompt = (x_prompt, c_prompt, jnp.zeros((bp, RWKV_COLS), F32),
              jnp.zeros((bp, N_HEADS, HEAD_DIM, HEAD_DIM), F32),
              jnp.zeros((bp, 0, WIDTH), F32), jnp.zeros((bp, 0, WIDTH), F32), jnp.zeros((bp, 0, N_HEADS), F32))
    sample = (x_sample, c_sample, state_rwkv_shift[0], state_rwkv_wkv[0],
              cache_fox_k[0].reshape(bs, n_past, WIDTH), cache_fox_v[0].reshape(bs, n_past, WIDTH),
              cache_fox_logf[0])
    (yp, wkv_p, sh_p, k_p, v_p, lf_p), (ysm, wkv_s, sh_s, k_s, v_s, lf_s) = _layer([prompt, sample], w)

    def heads(a):
        return a.reshape((1,) + a.shape[:2] + (N_HEADS, HEAD_DIM))

    return (yp, ysm,
            wkv_p[None], sh_p.reshape(1, bp, RWKV_COLS), heads(k_p), heads(v_p), lf_p[None],
            wkv_s[None], sh_s.reshape(1, bs, RWKV_COLS), heads(k_s), heads(v_s), lf_s[None])
```

```python
import functools
import math

import jax
import jax.numpy as jnp
from jax import lax
from jax.experimental import pallas as pl
from jax.experimental.pallas import tpu as pltpu
from jax.experimental.pallas import tpu_sc as plsc

F32 = jnp.float32
BF16 = jnp.bfloat16
I32 = jnp.int32

D_MODEL = 1024
N_HEADS = 8
HEAD_DIM = 64
WIDTH = N_HEADS * HEAD_DIM
HEADS_PER_GROUP = 4
RWKV_CHUNK = 64
RWKV_CHUNKS_PER_STEP = 4
MOE_TOKEN_ROWS = 512
EXPERT_BLOCK_ROWS = 512
SC_SCATTER_WINDOW = 128
SC_ROW_SPLIT = 2
DECAY_LORA = 64
ICLR_LORA = 64
GATE_LORA = 128
RWKV_COLS = 3 * WIDTH + DECAY_LORA + ICLR_LORA + GATE_LORA
FOX_MAIN_COLS = 4 * WIDTH
GATE_COLS = 2 * D_MODEL
RWKV_GN_EPS = HEAD_DIM * 1e-5
L2_EPS = 1e-12
RMS_EPS = 1e-6
N_EXPERTS = 256
TOP_K = 8
N_GROUPS = 8
TOPK_GROUPS = 4
EXPERTS_PER_GROUP = N_EXPERTS // N_GROUPS
D_EXPERT = 256
ROUTED_SCALE = 2.5

LANES = 128
VMEM_LIMIT = 56 * 1024 * 1024
NEG_BIG = -1e30

NN = (((1,), (0,)), ((), ()))
NT = (((1,), (1,)), ((), ()))
TN = (((0,), (0,)), ((), ()))


def _cparams(sem):
    return pltpu.CompilerParams(dimension_semantics=sem, vmem_limit_bytes=VMEM_LIMIT)


def _dot(a, b, dims=NN):
    return lax.dot_general(a.astype(BF16), b.astype(BF16), dims, preferred_element_type=F32)


def _split2(a):
    hi = a.astype(BF16)
    lo = (a - hi.astype(F32)).astype(BF16)
    return hi, lo


def _split3(a):
    hi = a.astype(BF16)
    r1 = a - hi.astype(F32)
    mid = r1.astype(BF16)
    lo = (r1 - mid.astype(F32)).astype(BF16)
    return hi, mid, lo


def _mm3(a, b, dims):
    d = functools.partial(lax.dot_general, dimension_numbers=dims, preferred_element_type=F32)
    return d(a[0], b[0]) + (d(a[0], b[1]) + d(a[1], b[0]))


def _dot_exact_rhs(a_exact, b, dims=NN):
    ab = a_exact.astype(BF16)
    bh, bm, bl = _split3(b)
    d = functools.partial(lax.dot_general, dimension_numbers=dims, preferred_element_type=F32)
    return d(ab, bh) + (d(ab, bm) + d(ab, bl))


def _gsum(x, g_ref):
    hi, mid, lo = _split3(x)
    g = g_ref[...]
    d = functools.partial(jnp.dot, preferred_element_type=F32)
    return d(hi, g) + (d(mid, g) + d(lo, g))


def _sigmoid(x):
    return 1.0 / (1.0 + jnp.exp(-x))


def _softplus(x):
    return jnp.maximum(x, 0.0) + jnp.log1p(jnp.exp(-jnp.abs(x)))


def _silu(x):
    return x * _sigmoid(x)


def _pack_bf16_pairs(x):
    w = x.shape[1] // 2
    bits = lax.bitcast_convert_type(x.astype(BF16).astype(F32), I32)
    return lax.shift_right_logical(bits[:, :w], 16) | (bits[:, w:] & -65536)


def _unpack_bf16_pairs(p):
    lo = lax.bitcast_convert_type(lax.shift_left(p, 16), F32)
    hi = lax.bitcast_convert_type(p & -65536, F32)
    return jnp.concatenate([lo, hi], axis=1).astype(BF16)


def _group_ones():
    h = jnp.arange(WIDTH, dtype=I32) // HEAD_DIM
    return (h[:, None] == h[None, :]).astype(BF16)


def _ada_kernel(c_ref, w_ref, b_ref, o_ref):
    o_ref[...] = _dot(_silu(c_ref[...]), w_ref[...]) + b_ref[...]


def _ada(c, w_ada, b_ada):
    nb = c.shape[0]
    n_out = w_ada.shape[1]
    blk = D_MODEL
    return pl.pallas_call(
        _ada_kernel,
        grid=(n_out // blk,),
        in_specs=[pl.BlockSpec((nb, D_MODEL), lambda j: (0, 0)),
                  pl.BlockSpec((D_MODEL, blk), lambda j: (0, j)),
                  pl.BlockSpec((1, blk), lambda j: (0, j))],
        out_specs=pl.BlockSpec((nb, blk), lambda j: (0, j)),
        out_shape=jax.ShapeDtypeStruct((nb, n_out), F32),
        compiler_params=_cparams(("parallel",)),
        name="ada_mod",
    )(c, w_ada, b_ada.reshape(1, n_out))


def _inproj_kernel(x_ref, mod_ref, g1_ref, wr_ref, wf_ref, wfl_ref, wg_ref, qn_ref, kn_ref, fb_ref, gm_ref, f0_ref,
                   pr_ref, q_ref, k_ref, v_ref, sg_ref, lf_ref, cf_ref, gate_ref, carry):
    bb, tt, d = x_ref.shape
    m = bb * tt
    x = x_ref[...]
    ms = jnp.mean(x * x, axis=-1, keepdims=True)
    h = x * lax.rsqrt(ms + RMS_EPS) * g1_ref[...]
    h = h * (1.0 + mod_ref[:, 1:2, :]) + mod_ref[:, 0:1, :]
    hb = h.reshape(m, d).astype(BF16)

    pr_ref[...] = jnp.dot(hb, wr_ref[...], preferred_element_type=F32).reshape(bb, tt, RWKV_COLS)

    f = jnp.dot(hb, wf_ref[...], preferred_element_type=F32)
    q = f[:, 0:WIDTH]
    k = f[:, WIDTH:2 * WIDTH]
    v = f[:, 2 * WIDTH:3 * WIDTH]
    og = f[:, 3 * WIDTH:4 * WIDTH]
    inv_hd = 1.0 / HEAD_DIM
    q = q * lax.rsqrt(_gsum(q * q, gm_ref) * inv_hd + RMS_EPS) * qn_ref[...]
    k = k * lax.rsqrt(_gsum(k * k, gm_ref) * inv_hd + RMS_EPS) * kn_ref[...]
    q_ref[...] = (q * (HEAD_DIM ** -0.5)).astype(BF16).reshape(bb, tt, WIDTH)
    k_ref[...] = k.reshape(bb, tt, WIDTH)
    v_ref[...] = v.reshape(bb, tt, WIDTH)
    sg_ref[...] = _sigmoid(og).reshape(bb, tt, WIDTH)

    fl = jnp.dot(hb, wfl_ref[...], preferred_element_type=F32)[:, 0:N_HEADS] + fb_ref[...]
    lf = -_softplus(-fl)
    lf_ref[...] = lf.reshape(bb, tt, N_HEADS)

    @pl.when(pl.program_id(1) == 0)
    def _():
        carry[...] = f0_ref[...]

    r = lax.broadcasted_iota(I32, (m, m), 0)
    c = lax.broadcasted_iota(I32, (m, m), 1)
    tri = jnp.logical_and(r // tt == c // tt, r >= c).astype(F32)
    cf = _dot_exact_rhs(tri, lf).reshape(bb, tt, N_HEADS) + carry[...]
    cf_ref[...] = cf
    carry[...] = cf[:, tt - 1:tt, :]

    gate_ref[...] =_sigmoid(jnp.dot(hb, wg_ref[...], preferred_element_type=F32)).reshape(bb, tt, GATE_COLS)


def _const_spec(shape):
    nd = len(shape)
    return pl.BlockSpec(shape, lambda *_: (0,) * nd)


def _inproj(x, mod, g1, wr, wf, wfl, wg, qn, kn, fb, gmat, f0, bb, tt):
    b, t, d = x.shape
    grid = (b // bb, t // tt)

    def tok(cols):
        return pl.BlockSpec((bb, tt, cols), lambda i, j: (i, j, 0))

    out_cols = [(RWKV_COLS, F32), (WIDTH, BF16), (WIDTH, F32), (WIDTH, F32), (WIDTH, F32), (N_HEADS, F32),
                (N_HEADS, F32), (GATE_COLS, F32)]
    return pl.pallas_call(
        _inproj_kernel,
        grid=grid,
        in_specs=[tok(d),
                  pl.BlockSpec((bb, 6, d), lambda i, j: (i, 0, 0)),
                  _const_spec((1, d)),
                  _const_spec(wr.shape), _const_spec(wf.shape), _const_spec(wfl.shape), _const_spec(wg.shape),
                  _const_spec((1, WIDTH)), _const_spec((1, WIDTH)), _const_spec((1, N_HEADS)),
                  _const_spec((WIDTH, WIDTH)),
                  pl.BlockSpec((bb, 1, N_HEADS), lambda i, j: (i, 0, 0))],
        out_specs=[tok(c) for c, _ in out_cols],
        out_shape=[jax.ShapeDtypeStruct((b, t, c), dt) for c, dt in out_cols],
        scratch_shapes=[pltpu.VMEM((bb, 1, N_HEADS), F32)],
        compiler_params=_cparams(("parallel", "arbitrary")),
        name="norm1_inproj",
    )(x, mod, g1, wr, wf, wfl, wg, qn, kn, fb, gmat, f0)


def _past_cumsum_kernel(x_ref, o_ref):
    x = x_ref[0]
    rows = x.shape[0]
    li = lax.broadcasted_iota(I32, (LANES, LANES), 0)
    lj = lax.broadcasted_iota(I32, (LANES, LANES), 1)
    same_head = (li % N_HEADS) == (lj % N_HEADS)
    within = jnp.logical_and(same_head, li // N_HEADS <= lj // N_HEADS).astype(BF16)
    xh, xm, xl = _split3(x)
    d2 = functools.partial(jnp.dot, preferred_element_type=F32)
    in_row = d2(xh, within) + (d2(xm, within) + d2(xl, within))
    sh = same_head.astype(BF16)
    row_tot = d2(xh, sh) + (d2(xm, sh) + d2(xl, sh))
    ri = lax.broadcasted_iota(I32, (rows, rows), 0)
    ci = lax.broadcasted_iota(I32, (rows, rows), 1)
    o_ref[0] = in_row + _dot_exact_rhs((ri > ci).astype(F32), row_tot)


def _past_cumsum(past_logf):
    b, p, h = past_logf.shape
    rows = p * h // LANES
    flat = past_logf.reshape(b, rows, LANES)
    out = pl.pallas_call(
        _past_cumsum_kernel,
        grid=(b,),
        in_specs=[pl.BlockSpec((1, rows, LANES), lambda i: (i, 0, 0))],
        out_specs=pl.BlockSpec((1, rows, LANES), lambda i: (i, 0, 0)),
        out_shape=jax.ShapeDtypeStruct((b, rows, LANES), F32),
        compiler_params=_cparams(("parallel",)),
        name="cache_logf_cumsum",
    )(flat)
    return out.reshape(b, p, h)


def _fox_kernel(*refs, n_past_blocks, tq):
    if n_past_blocks:
        (q_ref, fq_ref, sg_ref, kp_ref, vp_ref, fkp_ref, kn_ref, vn_ref, fkn_ref,
         o_ref, m_scr, l_scr, acc_scr) = refs
    else:
        q_ref, fq_ref, sg_ref, kn_ref, vn_ref, fkn_ref, o_ref, m_scr, l_scr, acc_scr = refs
    qi = pl.program_id(1)
    ki = pl.program_id(2)
    nk = pl.num_programs(2)

    @pl.when(ki == 0)
    def _():
        m_scr[...] = jnp.full(m_scr.shape, NEG_BIG, F32)
        l_scr[...] = jnp.zeros(l_scr.shape, F32)
        acc_scr[...] = jnp.zeros(acc_scr.shape, F32)

    lane_a = lax.broadcasted_iota(I32, (tq, LANES), 1) < HEAD_DIM

    def step(k_ref, v_ref, fk_ref, diag):
        tk = k_ref.shape[1]
        if diag:
            rq = lax.broadcasted_iota(I32, (tq, tk), 0)
            ck = lax.broadcasted_iota(I32, (tq, tk), 1)
            visible = ck <= rq
        fq_all = fq_ref[0]
        pairs = range(N_HEADS // 2)
        cols = [slice(j * LANES, (j + 1) * LANES) for j in pairs]
        scores = []
        for j in pairs:
            qj = q_ref[0, :, cols[j]]
            kb = k_ref[0, :, cols[j]].astype(BF16)
            for hh in range(2):
                h = 2 * j + hh
                qm = jnp.where(lane_a if hh == 0 else jnp.logical_not(lane_a), qj, jnp.zeros_like(qj))
                s = lax.dot_general(qm, kb, NT, preferred_element_type=F32)
                s = s + fq_all[:, h:h + 1] - fk_ref[0, h:h + 1, :]
                if diag:
                    s = jnp.where(visible, s, NEG_BIG)
                scores.append(s)
        alphas, probs = [], []
        for h in range(N_HEADS):
            m_old = m_scr[h]
            m_new = jnp.maximum(m_old, jnp.max(scores[h], axis=-1, keepdims=True))
            alpha = jnp.exp(m_old - m_new)
            p = jnp.exp(scores[h] - m_new)
            l_scr[h] = alpha * l_scr[h] + jnp.sum(p, axis=-1, keepdims=True)
            m_scr[h] = m_new
            alphas.append(alpha)
            probs.append(p.astype(BF16))
        for j in pairs:
            vb = v_ref[0, :, cols[j]].astype(BF16)
            pv0 = jnp.dot(probs[2 * j], vb, preferred_element_type=F32)
            pv1 = jnp.dot(probs[2 * j + 1], vb, preferred_element_type=F32)
            acc_scr[:, cols[j]] = (acc_scr[:, cols[j]] * jnp.where(lane_a, alphas[2 * j], alphas[2 * j + 1])
                                   + jnp.where(lane_a, pv0, pv1))

    if n_past_blocks:
        @pl.when(ki < n_past_blocks)
        def _():
            step(kp_ref, vp_ref, fkp_ref, False)

    kn = ki - n_past_blocks

    @pl.when(jnp.logical_and(kn >= 0, kn < qi))
    def _():
        step(kn_ref, vn_ref, fkn_ref, False)

    @pl.when(kn == qi)
    def _():
        step(kn_ref, vn_ref, fkn_ref, True)

    @pl.when(ki == nk - 1)
    def _():
        for j in range(N_HEADS // 2):
            cols = slice(j * LANES, (j + 1) * LANES)
            l = jnp.where(lane_a, l_scr[2 * j], l_scr[2 * j + 1])
            o_ref[0, :, cols] = acc_scr[:, cols] / l * sg_ref[0, :, cols]


def _fox_attention(q, fq, sg, k_new, v_new, fk_new_t, past=None, tq=512, tk_past=512):
    b, t, _ = q.shape
    nq = t // tq
    n_past_blocks = 0 if past is None else past[0].shape[1] // tk_past
    nk = n_past_blocks + nq

    def new_idx(i, qi, ki):
        return jnp.clip(ki - n_past_blocks, 0, qi)

    in_specs = [pl.BlockSpec((1, tq, WIDTH), lambda i, qi, ki: (i, qi, 0)),
                pl.BlockSpec((1, tq, N_HEADS), lambda i, qi, ki: (i, qi, 0)),
                pl.BlockSpec((1, tq, WIDTH), lambda i, qi, ki: (i, qi, 0))]
    args = [q, fq, sg]
    if n_past_blocks:
        def past_idx(i, qi, ki):
            return jnp.minimum(ki, n_past_blocks - 1)
        in_specs += [pl.BlockSpec((1, tk_past, WIDTH), lambda i, qi, ki: (i, past_idx(i, qi, ki), 0)),
                     pl.BlockSpec((1, tk_past, WIDTH), lambda i, qi, ki: (i, past_idx(i, qi, ki), 0)),
                     pl.BlockSpec((1, N_HEADS, tk_past), lambda i, qi, ki: (i, 0, past_idx(i, qi, ki)))]
        args += list(past)
    in_specs += [pl.BlockSpec((1, tq, WIDTH), lambda i, qi, ki: (i, new_idx(i, qi, ki), 0)),
                 pl.BlockSpec((1, tq, WIDTH), lambda i, qi, ki: (i, new_idx(i, qi, ki), 0)),
                 pl.BlockSpec((1, N_HEADS, tq), lambda i, qi, ki: (i, 0, new_idx(i, qi, ki)))]
    args += [k_new, v_new, fk_new_t]
    return pl.pallas_call(
        functools.partial(_fox_kernel, n_past_blocks=n_past_blocks, tq=tq),
        grid=(b, nq, nk),
        in_specs=in_specs,
        out_specs=pl.BlockSpec((1, tq, WIDTH), lambda i, qi, ki: (i, qi, 0)),
        out_shape=jax.ShapeDtypeStruct((b, t, WIDTH), F32),
        scratch_shapes=[pltpu.VMEM((N_HEADS, tq, 1), F32), pltpu.VMEM((N_HEADS, tq, 1), F32),
                        pltpu.VMEM((tq, WIDTH), F32)],
        compiler_params=_cparams(("parallel", "parallel", "arbitrary")),
        name="fox_attention",
    )(*args)


def _rwkv_kernel(p_ref, sh0_ref, s0_ref, mu_ref, w0_ref, wb_ref, a0_ref, ab_ref, gb_ref, kk_ref, ka_ref, rk_ref,
                 lnw_ref, lnb_ref, gm_ref, y_ref, st_ref, sht_ref, z_scr, prev_scr, *, c):
    t = pl.program_id(1)
    nt = pl.num_programs(1)
    n_rows = p_ref.shape[1]
    n_chunks = n_rows // c

    def head_block(h):
        lo = (h % HEADS_PER_GROUP) * HEAD_DIM
        return h // HEADS_PER_GROUP, slice(lo, lo + HEAD_DIM)

    @pl.when(t == 0)
    def _():
        z_scr[...] = jnp.zeros(z_scr.shape, F32)
        for h in range(N_HEADS):
            i, blk = head_block(h)
            z_scr[i, blk, blk] = s0_ref[0, h]
        prev_scr[...] = sh0_ref[0]

    p = p_ref[0]
    row = lax.broadcasted_iota(I32, p.shape, 0)
    prev = jnp.where(row == 0, prev_scr[...], pltpu.roll(p, 1, 0))
    last = p[n_rows - 1:n_rows, :]
    prev_scr[...] = last
    sht_ref[0] = last

    pm = p + (prev - p) * mu_ref[...]
    r = pm[:, 0:WIDTH]
    k = pm[:, WIDTH:2 * WIDTH]
    v = pm[:, 2 * WIDTH:3 * WIDTH]
    o1 = 3 * WIDTH
    wd = pm[:, o1:o1 + DECAY_LORA]
    ad = pm[:, o1 + DECAY_LORA:o1 + DECAY_LORA + ICLR_LORA]
    gd = pm[:, o1 + DECAY_LORA + ICLR_LORA:RWKV_COLS]

    w = -_softplus(-(w0_ref[...] + _dot(jnp.tanh(wd), wb_ref[...]))) - 0.5
    lw = -jnp.exp(w)
    a = _sigmoid(a0_ref[...] + _dot(ad, ab_ref[...]))
    g = _dot(_sigmoid(gd), gb_ref[...])
    kk = k * kk_ref[...]
    kk = kk / jnp.maximum(jnp.sqrt(_gsum(kk * kk, gm_ref)), L2_EPS)
    kf = k * (1.0 + (a - 1.0) * ka_ref[...])

    ri = lax.broadcasted_iota(I32, (n_rows, n_rows), 0)
    ci = lax.broadcasted_iota(I32, (n_rows, n_rows), 1)
    same_chunk = (ri // c) == (ci // c)
    cum = _dot_exact_rhs(jnp.logical_and(same_chunk, ri >= ci).astype(F32), lw)
    cum_last = _dot_exact_rhs(same_chunk.astype(F32), lw)
    r_t = r * jnp.exp(cum)
    a_t = -kk * jnp.exp(cum - lw)
    inv = jnp.exp(-cum)
    b_t = kk * a * inv
    k_t = kf * inv
    to_end = jnp.exp(cum_last - cum)
    b_e = kk * a * to_end
    k_e = kf * to_end
    g_end = jnp.exp(cum_last)

    hg = HEADS_PER_GROUP
    gw = hg * HEAD_DIM
    log_c = int(math.log2(c))
    t_idx = lax.broadcasted_iota(I32, (c, hg * c), 0)
    s_idx = lax.broadcasted_iota(I32, (c, hg * c), 1) & (c - 1)
    strict = s_idx < t_idx
    lower = s_idx <= t_idx
    eye = (s_idx == t_idx).astype(F32)
    rb = lax.broadcasted_iota(I32, (hg * c, gw), 0) >> log_c
    mask_kv = rb == (lax.broadcasted_iota(I32, (hg * c, gw), 1) >> int(math.log2(HEAD_DIM)))
    rs = lax.broadcasted_iota(I32, (hg * c, hg * c), 0) >> log_c
    mask_ss = rs == (lax.broadcasted_iota(I32, (hg * c, hg * c), 1) >> log_c)
    ng = N_HEADS // hg
    cat = functools.partial(jnp.concatenate, axis=0)
    units = [(slice(j * c, (j + 1) * c), slice(i * gw, (i + 1) * gw)) for j in range(n_chunks) for i in range(ng)]
    nu = len(units)

    def mm1(a, b_bd):
        return jnp.dot(a.astype(BF16), b_bd, preferred_element_type=F32)

    def dg(a, b, dims):
        return lax.dot_general(a.astype(BF16), b.astype(BF16), dims, preferred_element_type=F32)

    def bd1(x, mask):
        tiled = jnp.concatenate([x.astype(BF16)] * hg, axis=0)
        return jnp.where(mask, tiled, jnp.zeros_like(tiled))

    ar = [cat([a_t[rs_, s], r_t[rs_, s]]).astype(BF16) for rs_, s in units]
    ab = [dg(ar[n], bd1(b_t[units[n]], mask_kv), NT) for n in range(nu)]
    ak = [dg(ar[n], bd1(k_t[units[n]], mask_kv), NT) for n in range(nu)]
    l_ab = [jnp.where(strict, m[:c], 0.0) for m in ab]
    l_rb = [jnp.where(lower, m[c:], 0.0) for m in ab]
    l_ak = [jnp.where(strict, m[:c], 0.0) for m in ak]
    l_rk = [jnp.where(lower, m[c:], 0.0) for m in ak]
    tinv = [eye + m for m in l_ab]
    pw = [mm1(m, bd1(m, mask_ss)) for m in l_ab]
    for _ in range(1, log_c - 1):
        res = [mm1(cat([tinv[n], pw[n]]), bd1(pw[n], mask_ss)) for n in range(nu)]
        tinv = [tinv[n] + res[n][:c] for n in range(nu)]
        pw = [m[c:] for m in res]
    tinv = [tinv[n] + mm1(tinv[n], bd1(pw[n], mask_ss)) for n in range(nu)]
    av = [mm1(cat([l_ak[n], l_rk[n]]), bd1(v[units[n]], mask_kv)) for n in range(nu)]
    ue = [cat([b_e[units[n]], k_e[units[n]]]).astype(BF16) for n in range(nu)]

    def wide(fn, x):
        return [fn(x[:, :gw]), fn(x[:, gw:])]

    def bd1w(x):
        return jnp.concatenate(wide(lambda h_: bd1(h_, mask_kv), x), axis=1)

    rhs = [jnp.concatenate([a_t[units[n]], av[n][:c]], axis=1) for n in range(nu)]
    x0 = [mm1(tinv[n], bd1w(rhs[n])) for n in range(nu)]
    resid = [rhs[n] - (x0[n] - mm1(l_ab[n], bd1w(x0[n]))) for n in range(nu)]
    sol = [x0[n] + mm1(tinv[n], bd1w(resid[n])) for n in range(nu)]
    lift = [mm1(l_rb[n], bd1w(sol[n])) for n in range(nu)]
    lhs_s = [cat([sol[n][:, :gw], r_t[units[n]] + lift[n][:, :gw]]).astype(BF16) for n in range(nu)]
    u_loc = [sol[n][:, gw:] for n in range(nu)]
    o_loc = [av[n][c:] + lift[n][:, gw:] for n in range(nu)]

    zr = lax.broadcasted_iota(I32, (gw, gw), 0) >> int(math.log2(HEAD_DIM))
    zmask = zr == (lax.broadcasted_iota(I32, (gw, gw), 1) >> int(math.log2(HEAD_DIM)))
    z = [z_scr[i] for i in range(ng)]
    o_rows = []
    for j in range(n_chunks):
        o_grp = []
        for i in range(ng):
            n = j * ng + i
            rs_, s = units[n]
            sz = dg(lhs_s[n], z[i], NT)
            u = sz[:c] + u_loc[n]
            o_grp.append(sz[c:] + o_loc[n])
            upd = dg(cat([u, v[rs_, s]]), ue[n], TN)
            z[i] = z[i] * g_end[j * c:j * c + 1, s] + jnp.where(zmask, upd, 0.0)
        o_rows.append(jnp.concatenate(o_grp, axis=1))
    for i in range(ng):
        z_scr[i] = z[i]

    o = cat(o_rows)
    inv_hd = 1.0 / HEAD_DIM
    dlt = o - _gsum(o, gm_ref) * inv_hd
    var = _gsum(dlt * dlt, gm_ref) * inv_hd
    on = dlt * lax.rsqrt(var + RWKV_GN_EPS) * lnw_ref[...] + lnb_ref[...]
    bonus = _gsum(r * kf * rk_ref[...], gm_ref) * v
    y_ref[0] = (on + bonus) * g

    @pl.when(t == nt - 1)
    def _():
        for h in range(N_HEADS):
            i, blk = head_block(h)
            st_ref[0, h] = z_scr[i, blk, blk]


def _rwkv(p, shift0, s0, prm, gmat, chunk, chunks_per_step):
    b, t, _ = p.shape
    row = lambda n: _const_spec((1, n))
    rows = chunk * chunks_per_step
    return pl.pallas_call(
        functools.partial(_rwkv_kernel, c=chunk),
        grid=(b, t // rows),
        in_specs=[pl.BlockSpec((1, rows, RWKV_COLS), lambda i, j: (i, j, 0)),
                  pl.BlockSpec((1, 1, RWKV_COLS), lambda i, j: (i, 0, 0)),
                  pl.BlockSpec((1, N_HEADS, HEAD_DIM, HEAD_DIM), lambda i, j: (i, 0, 0, 0)),
                  row(RWKV_COLS), row(WIDTH), _const_spec((DECAY_LORA, WIDTH)), row(WIDTH),
                  _const_spec((ICLR_LORA, WIDTH)), _const_spec((GATE_LORA, WIDTH)),
                  row(WIDTH), row(WIDTH), row(WIDTH), row(WIDTH), row(WIDTH), _const_spec((WIDTH, WIDTH))],
        out_specs=[pl.BlockSpec((1, rows, WIDTH), lambda i, j: (i, j, 0)),
                   pl.BlockSpec((1, N_HEADS, HEAD_DIM, HEAD_DIM), lambda i, j: (i, 0, 0, 0)),
                   pl.BlockSpec((1, 1, RWKV_COLS), lambda i, j: (i, 0, 0))],
        out_shape=[jax.ShapeDtypeStruct((b, t, WIDTH), F32),
                   jax.ShapeDtypeStruct((b, N_HEADS, HEAD_DIM, HEAD_DIM), F32),
                   jax.ShapeDtypeStruct((b, 1, RWKV_COLS), F32)],
        scratch_shapes=[pltpu.VMEM((N_HEADS // HEADS_PER_GROUP, HEADS_PER_GROUP * HEAD_DIM,
                                    HEADS_PER_GROUP * HEAD_DIM), F32),
                        pltpu.VMEM((1, RWKV_COLS), F32)],
        compiler_params=_cparams(("parallel", "arbitrary")),
        name="rwkv7_mix",
    )(p, shift0, s0, prm["mu"], prm["w0"], prm["wb"], prm["a0"], prm["ab"], prm["gb"], prm["kk"], prm["ka"],
      prm["rk"], prm["lnw"], prm["lnb"], gmat)


def _merge_kernel(x_ref, ya_ref, yb_ref, gate_ref, mod_ref, g2_ref, woa_ref, wob_ref, wo_ref, wrh_ref, wrl_ref,
                  *rest):
    x1_ref, h2_ref, lg_ref = rest[-3:]
    bb, tt, d = x_ref.shape
    m = bb * tt
    gate = gate_ref[...].reshape(m, GATE_COLS)
    merged = (gate[:, 0:d] * _dot(ya_ref[...].reshape(m, WIDTH), woa_ref[...])
              + gate[:, d:2 * d] * _dot(yb_ref[...].reshape(m, WIDTH), wob_ref[...]))
    x1 = x_ref[...] + mod_ref[:, 2:3, :] * _dot(merged, wo_ref[...]).reshape(bb, tt, d)
    x1_ref[...] = x1
    ms = jnp.mean(x1 * x1, axis=-1, keepdims=True)
    h2 = x1 * lax.rsqrt(ms + RMS_EPS) * g2_ref[...]
    h2 = (h2 * (1.0 + mod_ref[:, 4:5, :]) + mod_ref[:, 3:4, :]).reshape(m, d)
    h2_ref[...] = _pack_bf16_pairs(h2)
    lg_ref[...] = _mm3((wrh_ref[...], wrl_ref[...]), _split2(h2), NT)


def _merge(x, ya, yb, gate, mod, w, n_total, row_offset, shared=None):
    b, t, d = x.shape
    bb, tt = _token_blocks(b, t, MOE_TOKEN_ROWS)
    nt = t // tt
    m = bb * tt
    off = row_offset // m

    def tok(cols):
        return pl.BlockSpec((bb, tt, cols), lambda i, j: (i, j, 0))

    in_specs = [tok(d), tok(WIDTH), tok(WIDTH), tok(GATE_COLS),
                pl.BlockSpec((bb, 6, d), lambda i, j: (i, 0, 0)),
                _const_spec((1, d)), _const_spec((WIDTH, d)), _const_spec((WIDTH, d)), _const_spec((d, d)),
                _const_spec((N_EXPERTS, d)), _const_spec((N_EXPERTS, d))]
    args = [x, ya, yb, gate, mod.reshape(b, 6, d), w["g2"], w["w_oa"], w["w_ob"], w["w_o"], w["wr_hi"], w["wr_lo"]]
    aliases = {}
    if shared is not None:
        aliases = {len(args): 1, len(args) + 1: 2}
        in_specs += [pl.BlockSpec(memory_space=pl.ANY), pl.BlockSpec(memory_space=pl.ANY)]
        args += list(shared)
    return pl.pallas_call(
        _merge_kernel,
        grid=(b // bb, nt),
        in_specs=in_specs,
        out_specs=[tok(d), pl.BlockSpec((m, d // 2), lambda i, j: (off + i * nt + j, 0)),
                   pl.BlockSpec((N_EXPERTS, m), lambda i, j: (0, off + i * nt + j))],
        out_shape=[jax.ShapeDtypeStruct((b, t, d), F32), jax.ShapeDtypeStruct((n_total, d // 2), I32),
                   jax.ShapeDtypeStruct((N_EXPERTS, n_total), F32)],
        input_output_aliases=aliases,
        compiler_params=_cparams(("parallel", "parallel")),
        name="merge_norm2_router",
    )(*args)


def _route_kernel(lg_ref, bias_ref, idx_ref, wt_ref, rank_ref, cnt_ref, carry):
    @pl.when(pl.program_id(0) == 0)
    def _():
        carry[...] = jnp.zeros(carry.shape, F32)

    tm = lg_ref.shape[1]
    scores = _sigmoid(lg_ref[...])
    sel = scores + bias_ref[...]
    row = lax.broadcasted_iota(I32, (N_EXPERTS, tm), 0)
    neg_inf = -jnp.inf

    def first_argmax(vals, rows):
        mx = jnp.max(vals, axis=0, keepdims=True)
        return mx, jnp.min(jnp.where(vals == mx, rows, N_EXPERTS), axis=0, keepdims=True)

    gslices = [slice(g * EXPERTS_PER_GROUP, (g + 1) * EXPERTS_PER_GROUP) for g in range(N_GROUPS)]
    gs = []
    row_g = lax.broadcasted_iota(I32, (EXPERTS_PER_GROUP, tm), 0)
    for sl in gslices:
        m1, i1 = first_argmax(sel[sl], row_g)
        m2 = jnp.max(jnp.where(row_g == i1, neg_inf, sel[sl]), axis=0, keepdims=True)
        gs.append(m1 + m2)
    kept = []
    for g in range(N_GROUPS):
        beaten = jnp.zeros((1, tm), I32)
        for o in range(N_GROUPS):
            if o != g:
                wins = (gs[o] >= gs[g]) if o < g else (gs[o] > gs[g])
                beaten = beaten + wins.astype(I32)
        kept.append(jnp.where(beaten < TOPK_GROUPS, sel[gslices[g]], neg_inf))
    cur = jnp.concatenate(kept, axis=0)

    idxs, ws = [], []
    picked = jnp.zeros((N_EXPERTS, tm), F32)
    for _ in range(TOP_K):
        _, ik = first_argmax(cur, row)
        hit = row == ik
        idxs.append(ik)
        ws.append(jnp.sum(jnp.where(hit, scores, 0.0), axis=0, keepdims=True))
        cur = jnp.where(hit, neg_inf, cur)
        picked = jnp.where(hit, 1.0, picked)
    wsum = ws[0]
    for k in range(1, TOP_K):
        wsum = wsum + ws[k]

    r = lax.broadcasted_iota(I32, (tm, tm), 0)
    c = lax.broadcasted_iota(I32, (tm, tm), 1)
    before = jnp.dot(picked.astype(BF16), (r < c).astype(BF16), preferred_element_type=F32) + carry[...]
    carry[...] = carry[...] + jnp.sum(picked, axis=1, keepdims=True)
    cnt_ref[...] = carry[...]

    kk = lax.broadcasted_iota(I32, (TOP_K, tm), 0)
    idx_o = jnp.zeros((TOP_K, tm), I32)
    wt_o = jnp.zeros((TOP_K, tm), F32)
    rank_o = jnp.zeros((TOP_K, tm), F32)
    for k in range(TOP_K):
        rk = jnp.sum(jnp.where(row == idxs[k], before, 0.0), axis=0, keepdims=True)
        idx_o = jnp.where(kk == k, idxs[k], idx_o)
        wt_o = jnp.where(kk == k, ws[k] / wsum * ROUTED_SCALE, wt_o)
        rank_o = jnp.where(kk == k, rk, rank_o)
    idx_ref[...] = idx_o
    wt_ref[...] = wt_o
    rank_ref[...] = rank_o.astype(I32)


def _route(logits_t, bias_col, tm):
    n = logits_t.shape[1]
    tokk = pl.BlockSpec((TOP_K, tm), lambda i: (0, i))
    return pl.pallas_call(
        _route_kernel,
        grid=(n // tm,),
        in_specs=[pl.BlockSpec((N_EXPERTS, tm), lambda i: (0, i)), _const_spec((N_EXPERTS, 1))],
        out_specs=[tokk, tokk, tokk, _const_spec((N_EXPERTS, 1))],
        out_shape=[jax.ShapeDtypeStruct((TOP_K, n), I32), jax.ShapeDtypeStruct((TOP_K, n), F32),
                   jax.ShapeDtypeStruct((TOP_K, n), I32), jax.ShapeDtypeStruct((N_EXPERTS, 1), F32)],
        scratch_shapes=[pltpu.VMEM((N_EXPERTS, 1), F32)],
        compiler_params=_cparams(("arbitrary",)),
        name="route_topk",
    )(logits_t, bias_col)


def _plan_kernel(cnt_ref, start_ref, be_ref, valid_ref, nu_ref, *, blk):
    cnt = cnt_ref[...]
    padded = jnp.ceil(cnt * (1.0 / blk)) * blk
    e_r = lax.broadcasted_iota(I32, (N_EXPERTS, N_EXPERTS), 0)
    e_c = lax.broadcasted_iota(I32, (N_EXPERTS, N_EXPERTS), 1)
    incl = (e_r <= e_c).astype(BF16)
    ph, pm, plo = _split3(jnp.broadcast_to(padded, (8, N_EXPERTS)))
    d2 = functools.partial(jnp.dot, preferred_element_type=F32)
    pad_end = (d2(ph, incl) + (d2(pm, incl) + d2(plo, incl)))[0:1, :]
    pad_start = pad_end - padded
    start_ref[...] = pad_start.astype(I32)
    total = jnp.max(pad_end, axis=-1, keepdims=True)
    nu_ref[...] = jnp.broadcast_to(total * (1.0 / blk), (1, N_EXPERTS)).astype(I32)
    nb = be_ref.shape[0]
    first = (lax.broadcasted_iota(I32, (nb, N_EXPERTS), 0) * blk).astype(F32)
    lane = lax.broadcasted_iota(I32, (nb, N_EXPERTS), 1)
    inside = jnp.logical_and(pad_start <= first, first < pad_end)
    be_ref[...] = jnp.sum(jnp.where(inside, lane, 0), axis=-1, keepdims=True)
    rows = jnp.minimum(pad_start + cnt - first, float(blk))
    valid_ref[...] = jnp.sum(jnp.where(inside, rows, 0.0), axis=-1, keepdims=True).astype(I32)


def _plan(counts, n_blocks, blk):
    return pl.pallas_call(
        functools.partial(_plan_kernel, blk=blk),
        out_shape=[jax.ShapeDtypeStruct((1, N_EXPERTS), I32), jax.ShapeDtypeStruct((n_blocks, 1), I32),
                   jax.ShapeDtypeStruct((n_blocks, 1), I32), jax.ShapeDtypeStruct((1, N_EXPERTS), I32)],
        compiler_params=pltpu.CompilerParams(vmem_limit_bytes=VMEM_LIMIT),
        name="dispatch_plan",
    )(counts)


def _dest_kernel(idx_ref, rank_ref, start_ref, dest_ref):
    tm = idx_ref.shape[1]
    row = lax.broadcasted_iota(I32, (N_EXPERTS, tm), 0)
    kk = lax.broadcasted_iota(I32, (TOP_K, tm), 0)
    idx = idx_ref[...]
    base = jnp.zeros((TOP_K, tm), I32)
    for k in range(TOP_K):
        bk = jnp.sum(jnp.where(row == idx[k:k + 1, :], start_ref[...], 0), axis=0, keepdims=True)
        base = jnp.where(kk == k, bk, base)
    dest_ref[...] = base + rank_ref[...]


def _dest(idx, rank, pad_start_col, tm):
    n = idx.shape[1]
    tokk = pl.BlockSpec((TOP_K, tm), lambda i: (0, i))
    return pl.pallas_call(
        _dest_kernel,
        grid=(n // tm,),
        in_specs=[tokk, tokk, _const_spec((N_EXPERTS, 1))],
        out_specs=tokk,
        out_shape=jax.ShapeDtypeStruct((TOP_K, n), I32),
        compiler_params=_cparams(("parallel",)),
        name="dispatch_dest",
    )(idx, rank, pad_start_col)


def _dispatch(h2p, dest_t, n_slots):
    n, wp = h2p.shape
    half = wp // SC_ROW_SPLIT
    window = SC_SCATTER_WINDOW
    mesh = plsc.VectorSubcoreMesh(core_axis_name="core", subcore_axis_name="subcore")
    out = jax.ShapeDtypeStruct((n_slots, half), h2p.dtype)

    @functools.partial(pl.kernel, out_type=[out] * SC_ROW_SPLIT, mesh=mesh, scratch_types=[])
    def scatter(rows_hbm, idx_hbm, *outs):
        for c, out_hbm in enumerate(outs):
            def body(rows_vmem, idx_vmem, out_hbm=out_hbm):
                for k in range(TOP_K):
                    pltpu.sync_copy(rows_vmem, out_hbm.at[idx_vmem.at[k]])

            pltpu.emit_pipeline(
                body, grid=(n // window,),
                in_specs=[pl.BlockSpec((window, half), index_map=lambda i, c=c: (i, c)),
                          pl.BlockSpec((TOP_K, window), index_map=lambda i: (0, i))],
                out_specs=[], core_axis_name=("core", "subcore"), dimension_semantics=(pltpu.PARALLEL,),
            )(rows_hbm, idx_hbm)

    return scatter(h2p, dest_t)


def _expert_kernel(be_ref, valid_ref, nu_ref, xa_ref, xb_ref, wg_ref, wu_ref, wd_ref, *rest):
    y_refs, (wg_b, wu_b, wd_b) = rest[:SC_ROW_SPLIT], rest[SC_ROW_SPLIT:]
    i = pl.program_id(0)
    nv = valid_ref[i]
    new_expert = jnp.logical_or(i == 0, be_ref[i] != be_ref[jnp.maximum(i - 1, 0)])

    @pl.when(jnp.logical_and(nv > 0, new_expert))
    def _():
        wg_b[...] = wg_ref[0].astype(BF16)
        wu_b[...] = wu_ref[0].astype(BF16)
        wd_b[...] = wd_ref[0].astype(BF16)

    @pl.when(nv > 0)
    def _():
        blk = xa_ref.shape[0]
        rows = lax.broadcasted_iota(I32, (blk, 1), 0)
        packed = jnp.concatenate([xa_ref[...], xb_ref[...]], axis=1)
        x = _unpack_bf16_pairs(jnp.where(rows < nv, packed, 0))
        hg = jnp.dot(x, wg_b[...], preferred_element_type=F32)
        hu = jnp.dot(x, wu_b[...], preferred_element_type=F32)
        y = _pack_bf16_pairs(jnp.dot((_silu(hg) * hu).astype(BF16), wd_b[...], preferred_element_type=F32))
        cw = y.shape[1] // SC_ROW_SPLIT
        for c, y_ref in enumerate(y_refs):
            y_ref[...] = y[:, c * cw:(c + 1) * cw]


def _experts(xs, block_e, valid, n_used, w_eg, w_eu, w_ed, blk):
    xa, xb = xs
    n_slots, packed = xa.shape
    d = w_eg.shape[1]
    n_blocks = n_slots // blk

    def row_blk(i, be, valid, nu):
        return (jnp.minimum(i, nu[0] - 1), 0)

    def w_blk(i, be, valid, nu):
        return (be[i], 0, 0)

    return pl.pallas_call(
        _expert_kernel,
        grid_spec=pltpu.PrefetchScalarGridSpec(
            num_scalar_prefetch=3,
            grid=(n_blocks,),
            in_specs=[pl.BlockSpec((blk, packed), row_blk), pl.BlockSpec((blk, packed), row_blk),
                      pl.BlockSpec((1, d, D_EXPERT), w_blk), pl.BlockSpec((1, d, D_EXPERT), w_blk),
                      pl.BlockSpec((1, D_EXPERT, d), w_blk)],
            out_specs=[pl.BlockSpec((blk, packed), row_blk)] * SC_ROW_SPLIT,
            scratch_shapes=[pltpu.VMEM((d, D_EXPERT), BF16), pltpu.VMEM((d, D_EXPERT), BF16),
                            pltpu.VMEM((D_EXPERT, d), BF16)]),
        out_shape=[jax.ShapeDtypeStruct((n_slots, packed), I32)] * SC_ROW_SPLIT,
        compiler_params=_cparams(("arbitrary",)),
        name="moe_experts",
    )(block_e, valid, n_used, xa, xb, w_eg, w_eu, w_ed)


def _combine_gather(ys, dest_t):
    k, n = dest_t.shape
    cw = ys[0].shape[1]
    window = SC_SCATTER_WINDOW
    mesh = plsc.VectorSubcoreMesh(core_axis_name="core", subcore_axis_name="subcore")

    @functools.partial(pl.kernel, out_type=jax.ShapeDtypeStruct((k * n, cw * len(ys)), ys[0].dtype), mesh=mesh,
                       scratch_types=[])
    def gather(*refs):
        y_refs, idx_hbm, out_hbm = refs[:len(ys)], refs[len(ys)], refs[len(ys) + 1]
        for c, y_hbm in enumerate(y_refs):
            def body(idx_vmem, out_vmem, y_hbm=y_hbm):
                pltpu.sync_copy(y_hbm.at[idx_vmem.at[0]], out_vmem)

            pltpu.emit_pipeline(
                body, grid=(k * n // window,),
                in_specs=[pl.BlockSpec((1, window), index_map=lambda i: (0, i))],
                out_specs=[pl.BlockSpec((window, cw), index_map=lambda i, c=c: (i, c))],
                core_axis_name=("core", "subcore"), dimension_semantics=(pltpu.PARALLEL,),
            )(idx_hbm, out_hbm)

    return gather(*ys, dest_t.reshape(1, k * n)).reshape(k, n, cw * len(ys))


def _final_kernel(x1_ref, h2_ref, wt_ref, mod_ref, wsg_ref, wsu_ref, wsd_ref, yg_ref, o_ref):
    bb, tt, d = x1_ref.shape
    hb = _unpack_bf16_pairs(h2_ref[...])
    hg = jnp.dot(hb, wsg_ref[...], preferred_element_type=F32)
    hu = jnp.dot(hb, wsu_ref[...], preferred_element_type=F32)
    ffn = _dot(_silu(hg) * hu, wsd_ref[...])
    wt = wt_ref[...]
    for k in range(TOP_K):
        ffn = ffn + wt[:, k:k + 1] * _unpack_bf16_pairs(yg_ref[k]).astype(F32)
    o_ref[...] = x1_ref[...] + mod_ref[:, 5:6, :] * ffn.reshape(bb, tt, d)


def _final(x1, h2_all, wts_all, y_tok, mod, w, row_offset):
    b, t, d = x1.shape
    bb, tt = _token_blocks(b, t, MOE_TOKEN_ROWS)
    nt = t // tt
    m = bb * tt
    off = row_offset // m

    def flat_idx(i, j):
        return off + i * nt + j

    return pl.pallas_call(
        _final_kernel,
        grid=(b // bb, nt),
        in_specs=[pl.BlockSpec((bb, tt, d), lambda i, j: (i, j, 0)),
                  pl.BlockSpec((m, h2_all.shape[1]), lambda i, j: (flat_idx(i, j), 0)),
                  pl.BlockSpec((m, TOP_K), lambda i, j: (flat_idx(i, j), 0)),
                  pl.BlockSpec((bb, 6, d), lambda i, j: (i, 0, 0)),
                  _const_spec((d, D_EXPERT)), _const_spec((d, D_EXPERT)), _const_spec((D_EXPERT, d)),
                  pl.BlockSpec((TOP_K, m, y_tok.shape[2]), lambda i, j: (0, flat_idx(i, j), 0))],
        out_specs=pl.BlockSpec((bb, tt, d), lambda i, j: (i, j, 0)),
        out_shape=jax.ShapeDtypeStruct((b, t, d), F32),
        compiler_params=_cparams(("parallel", "parallel")),
        name="moe_combine_final",
    )(x1, h2_all, wts_all, mod.reshape(b, 6, d), w["w_sg"], w["w_su"], w["w_sd"], y_tok)


def _largest_tile(n, candidates):
    return next(c for c in candidates if n % c == 0)


def _moe_routed(h2_all, logits_all, w, blk=EXPERT_BLOCK_ROWS):
    n = h2_all.shape[0]
    n_blocks = (n * TOP_K + N_EXPERTS * (blk - 1)) // blk + 1
    n_blocks = (n_blocks + 7) // 8 * 8
    idx, wts_t, rank, counts = _route(logits_all, w["router_bias"], _largest_tile(n, (512, 256)))
    pad_start, block_e, valid, n_used = _plan(counts.reshape(1, N_EXPERTS), n_blocks, blk)
    block_e = block_e.reshape(n_blocks)
    valid = valid.reshape(n_blocks)
    n_used = n_used[0, 0:1]
    dest_t = _dest(idx, rank, pad_start.reshape(N_EXPERTS, 1), _largest_tile(n, (1024, 512, 256)))
    xs = _dispatch(h2_all, dest_t, n_blocks * blk)
    ys = _experts(xs, block_e, valid, n_used, w["w_eg"], w["w_eu"], w["w_ed"], blk)
    return _combine_gather(ys, dest_t), jnp.transpose(wts_t)


def _prep(raw):
    p = {k: v[0] for k, v in raw.items()}
    w_in = p["w_in"]
    o_fox = RWKV_COLS
    o_fl = o_fox + FOX_MAIN_COLS
    o_gate = o_fl + N_HEADS
    row = lambda a: a.reshape(1, -1)
    return dict(
        w_ada=p["w_ada"], b_ada=p["b_ada"],
        g1=row(p["norm1_g"]), g2=row(p["norm2_g"]),
        wr=w_in[:, :o_fox].astype(BF16),
        wf=w_in[:, o_fox:o_fl].astype(BF16),
        wfl=jnp.pad(w_in[:, o_fl:o_gate], ((0, 0), (0, LANES - N_HEADS))).astype(BF16),
        wg=w_in[:, o_gate:].astype(BF16),
        qn=row(jnp.tile(p["fox_q_norm"], N_HEADS)), kn=row(jnp.tile(p["fox_k_norm"], N_HEADS)),
        fb=row(p["fox_f_bias"]),
        gmat=_group_ones(),
        rwkv=dict(mu=row(p["rwkv_mu"]), w0=row(p["rwkv_w0"]), wb=p["rwkv_w_lora_b"], a0=row(p["rwkv_a0"]),
                  ab=p["rwkv_a_lora_b"], gb=p["rwkv_g_lora_b"], kk=row(p["rwkv_k_k"]), ka=row(p["rwkv_k_a"]),
                  rk=row(p["rwkv_r_k"]), lnw=row(p["rwkv_ln_w"]), lnb=row(p["rwkv_ln_b"])),
        w_oa=p["w_out_rwkv"].astype(BF16), w_ob=p["w_out_fox"].astype(BF16), w_o=p["w_out"].astype(BF16),
        wr_hi=p["w_router"].T.astype(BF16),
        wr_lo=(p["w_router"] - p["w_router"].astype(BF16).astype(F32)).T.astype(BF16),
        router_bias=p["router_bias"].reshape(N_EXPERTS, 1),
        w_eg=p["w_exp_gate"], w_eu=p["w_exp_up"], w_ed=p["w_exp_down"],
        w_sg=p["w_sh_gate"].astype(BF16), w_su=p["w_sh_up"].astype(BF16), w_sd=p["w_sh_down"].astype(BF16),
    )


def _token_blocks(b, t, rows=256):
    if t >= rows:
        return 1, rows
    bb = max(1, min(b, 256 // t))
    while b % bb:
        bb -= 1
    return bb, t


def _mix_path(x, mod, shift0, wkv0, past_k, past_v, past_logf, w):
    b, t, d = x.shape
    bb, tt = _token_blocks(b, t)
    n_past = past_k.shape[1]
    if n_past:
        f_past = _past_cumsum(past_logf)
        init = f_past[:, n_past - 1:n_past, :]
        past = (past_k, past_v, jnp.swapaxes(f_past, 1, 2))
    else:
        init = jnp.zeros((b, 1, N_HEADS), F32)
        past = None
    pr, q, k, v, sg, logf, f_new, gate = _inproj(x, mod.reshape(b, 6, d), w["g1"], w["wr"], w["wf"], w["wfl"],
                                                 w["wg"], w["qn"], w["kn"], w["fb"], w["gmat"], init, bb, tt)
    y_fox = _fox_attention(q, f_new, sg, k, v, jnp.swapaxes(f_new, 1, 2), past=past, tq=min(t, 512),
                           tk_past=min(max(n_past, 1), 512))
    chunk = min(t, RWKV_CHUNK)
    y_rwkv, wkv_new, shift_new = _rwkv(pr, shift0.reshape(b, 1, RWKV_COLS), wkv0, w["rwkv"], w["gmat"],
                                       chunk, max(1, min(RWKV_CHUNKS_PER_STEP, t // chunk)))
    return y_rwkv, y_fox, gate, wkv_new, shift_new, k, v, logf


def _layer(paths, w):
    n_b = [p[0].shape[0] for p in paths]
    mod_all = _ada(jnp.concatenate([p[1] for p in paths], axis=0), w["w_ada"], w["b_ada"])
    mods, o = [], 0
    for nb in n_b:
        mods.append(mod_all[o:o + nb])
        o += nb
    n_total = sum(p[0].shape[0] * p[0].shape[1] for p in paths)
    mixed, x1s = [], []
    shared, row = None, 0
    for (x, _, shift0, wkv0, pk, pv, plf), mod in zip(paths, mods):
        ya, yb, gate, wkv_new, shift_new, k, v, logf = _mix_path(x, mod, shift0, wkv0, pk, pv, plf, w)
        x1, h2_all, lg_all = _merge(x, ya, yb, gate, mod, w, n_total, row, shared)
        shared = (h2_all, lg_all)
        row += x.shape[0] * x.shape[1]
        mixed.append((wkv_new, shift_new, k, v, logf))
        x1s.append(x1)
    y_tok, wts = _moe_routed(h2_all, lg_all, w)
    outs, row = [], 0
    for x1, mod, st in zip(x1s, mods, mixed):
        y = _final(x1, h2_all, wts, y_tok, mod, w, row)
        row += x1.shape[0] * x1.shape[1]
        outs.append((y,) + st)
    return outs


def kernel(x_prompt, x_sample, c_prompt, c_sample, state_rwkv_wkv, state_rwkv_shift, cache_fox_k, cache_fox_v,
           cache_fox_logf, w_ada, b_ada, norm1_g, norm2_g, w_in, rwkv_mu, rwkv_w0, rwkv_w_lora_b, rwkv_a0,
           rwkv_a_lora_b, rwkv_g_lora_b, rwkv_k_k, rwkv_k_a, rwkv_r_k, rwkv_ln_w, rwkv_ln_b, fox_q_norm,
           fox_k_norm, fox_f_bias, w_out_rwkv, w_out_fox, w_out, w_router, router_bias, w_exp_gate, w_exp_up,
           w_exp_down, w_sh_gate, w_sh_up, w_sh_down):
    raw = dict(w_ada=w_ada, b_ada=b_ada, norm1_g=norm1_g, norm2_g=norm2_g, w_in=w_in, rwkv_mu=rwkv_mu,
               rwkv_w0=rwkv_w0, rwkv_w_lora_b=rwkv_w_lora_b, rwkv_a0=rwkv_a0, rwkv_a_lora_b=rwkv_a_lora_b,
               rwkv_g_lora_b=rwkv_g_lora_b, rwkv_k_k=rwkv_k_k, rwkv_k_a=rwkv_k_a, rwkv_r_k=rwkv_r_k,
               rwkv_ln_w=rwkv_ln_w, rwkv_ln_b=rwkv_ln_b, fox_q_norm=fox_q_norm, fox_k_norm=fox_k_norm,
               fox_f_bias=fox_f_bias, w_out_rwkv=w_out_rwkv, w_out_fox=w_out_fox, w_out=w_out,
               w_router=w_router, router_bias=router_bias, w_exp_gate=w_exp_gate, w_exp_up=w_exp_up,
               w_exp_down=w_exp_down, w_sh_gate=w_sh_gate, w_sh_up=w_sh_up, w_sh_down=w_sh_down)
    assert w_in.shape[0] == 1, "single-layer stack"
    w = _prep(raw)
    bp, tp, _ = x_prompt.shape
    bs, ts, _ = x_sample.shape
    n_past = cache_fox_k.shape[2]
    prompt = (x_prompt, c_prompt, jnp.zeros((bp, RWKV_COLS), F32),
              jnp.zeros((bp, N_HEADS, HEAD_DIM, HEAD_DIM), F32),
              jnp.zeros((bp, 0, WIDTH), F32), jnp.zeros((bp, 0, WIDTH), F32), jnp.zeros((bp, 0, N_HEADS), F32))
    sample = (x_sample, c_sample, state_rwkv_shift[0], state_rwkv_wkv[0],
              cache_fox_k[0].reshape(bs, n_past, WIDTH), cache_fox_v[0].reshape(bs, n_past, WIDTH),
              cache_fox_logf[0])
    (yp, wkv_p, sh_p, k_p, v_p, lf_p), (ysm, wkv_s, sh_s, k_s, v_s, lf_s) = _layer([prompt, sample], w)

    def heads(a):
        return a.reshape((1,) + a.shape[:2] + (N_HEADS, HEAD_DIM))

    return (yp, ysm,
            wkv_p[None], sh_p.reshape(1, bp, RWKV_COLS), heads(k_p), heads(v_p), lf_p[None],
            wkv_s[None], sh_s.reshape(1, bs, RWKV_COLS), heads(k_s), heads(v_s), lf_s[None])
```

```python
import functools
import math

import jax
import jax.numpy as jnp
from jax import lax
from jax.experimental import pallas as pl
from jax.experimental.pallas import tpu as pltpu
from jax.experimental.pallas import tpu_sc as plsc

F32 = jnp.float32
BF16 = jnp.bfloat16
I32 = jnp.int32

D_MODEL = 1024
N_HEADS = 8
HEAD_DIM = 64
WIDTH = N_HEADS * HEAD_DIM
HEADS_PER_GROUP = 4
RWKV_CHUNK = 64
RWKV_CHUNKS_PER_STEP = 4
INPROJ_TOKEN_ROWS = 512
MOE_TOKEN_ROWS = 512
EXPERT_BLOCK_ROWS = 512
SC_SCATTER_WINDOW = 128
SC_ROW_SPLIT = 2
DECAY_LORA = 64
ICLR_LORA = 64
GATE_LORA = 128
RWKV_COLS = 3 * WIDTH + DECAY_LORA + ICLR_LORA + GATE_LORA
FOX_MAIN_COLS = 4 * WIDTH
GATE_COLS = 2 * D_MODEL
RWKV_GN_EPS = HEAD_DIM * 1e-5
L2_EPS = 1e-12
RMS_EPS = 1e-6
N_EXPERTS = 256
TOP_K = 8
N_GROUPS = 8
TOPK_GROUPS = 4
EXPERTS_PER_GROUP = N_EXPERTS // N_GROUPS
D_EXPERT = 256
ROUTED_SCALE = 2.5

LANES = 128
VMEM_LIMIT = 56 * 1024 * 1024
NEG_BIG = -1e30

NN = (((1,), (0,)), ((), ()))
NT = (((1,), (1,)), ((), ()))
TN = (((0,), (0,)), ((), ()))


def _cparams(sem):
    return pltpu.CompilerParams(dimension_semantics=sem, vmem_limit_bytes=VMEM_LIMIT)


def _dot(a, b, dims=NN):
    return lax.dot_general(a.astype(BF16), b.astype(BF16), dims, preferred_element_type=F32)


def _split2(a):
    hi = a.astype(BF16)
    lo = (a - hi.astype(F32)).astype(BF16)
    return hi, lo


def _split3(a):
    hi = a.astype(BF16)
    r1 = a - hi.astype(F32)
    mid = r1.astype(BF16)
    lo = (r1 - mid.astype(F32)).astype(BF16)
    return hi, mid, lo


def _mm3(a, b, dims):
    d = functools.partial(lax.dot_general, dimension_numbers=dims, preferred_element_type=F32)
    return d(a[0], b[0]) + (d(a[0], b[1]) + d(a[1], b[0]))


def _dot_exact_rhs(a_exact, b, dims=NN):
    ab = a_exact.astype(BF16)
    bh, bm, bl = _split3(b)
    d = functools.partial(lax.dot_general, dimension_numbers=dims, preferred_element_type=F32)
    return d(ab, bh) + (d(ab, bm) + d(ab, bl))


def _gsum(x, g_ref):
    hi, mid, lo = _split3(x)
    g = g_ref[...]
    d = functools.partial(jnp.dot, preferred_element_type=F32)
    return d(hi, g) + (d(mid, g) + d(lo, g))


def _sigmoid(x):
    return 1.0 / (1.0 + jnp.exp(-x))


def _softplus(x):
    return jnp.maximum(x, 0.0) + jnp.log1p(jnp.exp(-jnp.abs(x)))


def _silu(x):
    return x * _sigmoid(x)


def _pack_bf16_pairs(x):
    w = x.shape[1] // 2
    bits = lax.bitcast_convert_type(x.astype(BF16).astype(F32), I32)
    return lax.shift_right_logical(bits[:, :w], 16) | (bits[:, w:] & -65536)


def _unpack_bf16_pairs(p):
    lo = lax.bitcast_convert_type(lax.shift_left(p, 16), F32)
    hi = lax.bitcast_convert_type(p & -65536, F32)
    return jnp.concatenate([lo, hi], axis=1).astype(BF16)


def _group_ones():
    h = jnp.arange(WIDTH, dtype=I32) // HEAD_DIM
    return (h[:, None] == h[None, :]).astype(BF16)


def _ada_kernel(c_ref, w_ref, b_ref, o_ref):
    o_ref[...] = _dot(_silu(c_ref[...]), w_ref[...]) + b_ref[...]


def _ada(c, w_ada, b_ada):
    nb = c.shape[0]
    n_out = w_ada.shape[1]
    blk = D_MODEL
    return pl.pallas_call(
        _ada_kernel,
        grid=(n_out // blk,),
        in_specs=[pl.BlockSpec((nb, D_MODEL), lambda j: (0, 0)),
                  pl.BlockSpec((D_MODEL, blk), lambda j: (0, j)),
                  pl.BlockSpec((1, blk), lambda j: (0, j))],
        out_specs=pl.BlockSpec((nb, blk), lambda j: (0, j)),
        out_shape=jax.ShapeDtypeStruct((nb, n_out), F32),
        compiler_params=_cparams(("parallel",)),
        name="ada_mod",
    )(c, w_ada, b_ada.reshape(1, n_out))


def _inproj_kernel(x_ref, mod_ref, g1_ref, wr_ref, wf_ref, wfl_ref, wg_ref, qn_ref, kn_ref, fb_ref, gm_ref, f0_ref,
                   pr_ref, q_ref, k_ref, v_ref, sg_ref, lf_ref, cf_ref, gate_ref, carry):
    bb, tt, d = x_ref.shape
    m = bb * tt
    x = x_ref[...]
    ms = jnp.mean(x * x, axis=-1, keepdims=True)
    h = x * lax.rsqrt(ms + RMS_EPS) * g1_ref[...]
    h = h * (1.0 + mod_ref[:, 1:2, :]) + mod_ref[:, 0:1, :]
    hb = h.reshape(m, d).astype(BF16)

    pr_ref[...] = jnp.dot(hb, wr_ref[...], preferred_element_type=F32).reshape(bb, tt, RWKV_COLS)

    f = jnp.dot(hb, wf_ref[...], preferred_element_type=F32)
    q = f[:, 0:WIDTH]
    k = f[:, WIDTH:2 * WIDTH]
    v = f[:, 2 * WIDTH:3 * WIDTH]
    og = f[:, 3 * WIDTH:4 * WIDTH]
    inv_hd = 1.0 / HEAD_DIM
    q = q * lax.rsqrt(_gsum(q * q, gm_ref) * inv_hd + RMS_EPS) * qn_ref[...]
    k = k * lax.rsqrt(_gsum(k * k, gm_ref) * inv_hd + RMS_EPS) * kn_ref[...]
    q_ref[...] = (q * (HEAD_DIM ** -0.5)).astype(BF16).reshape(bb, tt, WIDTH)
    k_ref[...] = k.reshape(bb, tt, WIDTH)
    v_ref[...] = v.reshape(bb, tt, WIDTH)
    sg_ref[...] = _sigmoid(og).reshape(bb, tt, WIDTH)

    fl = jnp.dot(hb, wfl_ref[...], preferred_element_type=F32)[:, 0:N_HEADS] + fb_ref[...]
    lf = -_softplus(-fl)
    lf_ref[...] = lf.reshape(bb, tt, N_HEADS)

    @pl.when(pl.program_id(1) == 0)
    def _():
        carry[...] = f0_ref[...]

    r = lax.broadcasted_iota(I32, (m, m), 0)
    c = lax.broadcasted_iota(I32, (m, m), 1)
    tri = jnp.logical_and(r // tt == c // tt, r >= c).astype(F32)
    cf = _dot_exact_rhs(tri, lf).reshape(bb, tt, N_HEADS) + carry[...]
    cf_ref[...] = cf
    carry[...] = cf[:, tt - 1:tt, :]

    gate_ref[...] =_sigmoid(jnp.dot(hb, wg_ref[...], preferred_element_type=F32)).reshape(bb, tt, GATE_COLS)


def _const_spec(shape, single_buffer=False):
    nd = len(shape)
    if single_buffer:
        return pl.BlockSpec(shape, lambda *_: (0,) * nd, pipeline_mode=pl.Buffered(1))
    return pl.BlockSpec(shape, lambda *_: (0,) * nd)


def _inproj(x, mod, g1, wr, wf, wfl, wg, qn, kn, fb, gmat, f0, bb, tt):
    b, t, d = x.shape
    grid = (b // bb, t // tt)

    def tok(cols):
        return pl.BlockSpec((bb, tt, cols), lambda i, j: (i, j, 0))

    out_cols = [(RWKV_COLS, F32), (WIDTH, BF16), (WIDTH, F32), (WIDTH, F32), (WIDTH, F32), (N_HEADS, F32),
                (N_HEADS, F32), (GATE_COLS, F32)]
    return pl.pallas_call(
        _inproj_kernel,
        grid=grid,
        in_specs=[tok(d),
                  pl.BlockSpec((bb, 6, d), lambda i, j: (i, 0, 0)),
                  _const_spec((1, d)),
                  _const_spec(wr.shape, True), _const_spec(wf.shape, True), _const_spec(wfl.shape),
                  _const_spec(wg.shape, True),
                  _const_spec((1, WIDTH)), _const_spec((1, WIDTH)), _const_spec((1, N_HEADS)),
                  _const_spec((WIDTH, WIDTH)),
                  pl.BlockSpec((bb, 1, N_HEADS), lambda i, j: (i, 0, 0))],
        out_specs=[tok(c) for c, _ in out_cols],
        out_shape=[jax.ShapeDtypeStruct((b, t, c), dt) for c, dt in out_cols],
        scratch_shapes=[pltpu.VMEM((bb, 1, N_HEADS), F32)],
        compiler_params=_cparams(("parallel", "arbitrary")),
        name="norm1_inproj",
    )(x, mod, g1, wr, wf, wfl, wg, qn, kn, fb, gmat, f0)


def _past_cumsum_kernel(x_ref, o_ref):
    x = x_ref[0]
    rows = x.shape[0]
    li = lax.broadcasted_iota(I32, (LANES, LANES), 0)
    lj = lax.broadcasted_iota(I32, (LANES, LANES), 1)
    same_head = (li % N_HEADS) == (lj % N_HEADS)
    within = jnp.logical_and(same_head, li // N_HEADS <= lj // N_HEADS).astype(BF16)
    xh, xm, xl = _split3(x)
    d2 = functools.partial(jnp.dot, preferred_element_type=F32)
    in_row = d2(xh, within) + (d2(xm, within) + d2(xl, within))
    sh = same_head.astype(BF16)
    row_tot = d2(xh, sh) + (d2(xm, sh) + d2(xl, sh))
    ri = lax.broadcasted_iota(I32, (rows, rows), 0)
    ci = lax.broadcasted_iota(I32, (rows, rows), 1)
    o_ref[0] = in_row + _dot_exact_rhs((ri > ci).astype(F32), row_tot)


def _past_cumsum(past_logf):
    b, p, h = past_logf.shape
    rows = p * h // LANES
    flat = past_logf.reshape(b, rows, LANES)
    out = pl.pallas_call(
        _past_cumsum_kernel,
        grid=(b,),
        in_specs=[pl.BlockSpec((1, rows, LANES), lambda i: (i, 0, 0))],
        out_specs=pl.BlockSpec((1, rows, LANES), lambda i: (i, 0, 0)),
        out_shape=jax.ShapeDtypeStruct((b, rows, LANES), F32),
        compiler_params=_cparams(("parallel",)),
        name="cache_logf_cumsum",
    )(flat)
    return out.reshape(b, p, h)


def _fox_kernel(*refs, n_past_blocks, tq):
    if n_past_blocks:
        (q_ref, fq_ref, sg_ref, kp_ref, vp_ref, fkp_ref, kn_ref, vn_ref, fkn_ref,
         o_ref, m_scr, l_scr, acc_scr) = refs
    else:
        q_ref, fq_ref, sg_ref, kn_ref, vn_ref, fkn_ref, o_ref, m_scr, l_scr, acc_scr = refs
    qi = pl.program_id(1)
    ki = pl.program_id(2)
    nk = pl.num_programs(2)

    @pl.when(ki == 0)
    def _():
        m_scr[...] = jnp.full(m_scr.shape, NEG_BIG, F32)
        l_scr[...] = jnp.zeros(l_scr.shape, F32)
        acc_scr[...] = jnp.zeros(acc_scr.shape, F32)

    lane_a = lax.broadcasted_iota(I32, (tq, LANES), 1) < HEAD_DIM

    def step(k_ref, v_ref, fk_ref, diag):
        tk = k_ref.shape[1]
        if diag:
            rq = lax.broadcasted_iota(I32, (tq, tk), 0)
            ck = lax.broadcasted_iota(I32, (tq, tk), 1)
            visible = ck <= rq
        fq_all = fq_ref[0]
        pairs = range(N_HEADS // 2)
        cols = [slice(j * LANES, (j + 1) * LANES) for j in pairs]
        scores = []
        for j in pairs:
            qj = q_ref[0, :, cols[j]]
            kb = k_ref[0, :, cols[j]].astype(BF16)
            for hh in range(2):
                h = 2 * j + hh
                qm = jnp.where(lane_a if hh == 0 else jnp.logical_not(lane_a), qj, jnp.zeros_like(qj))
                s = lax.dot_general(qm, kb, NT, preferred_element_type=F32)
                s = s + fq_all[:, h:h + 1] - fk_ref[0, h:h + 1, :]
                if diag:
                    s = jnp.where(visible, s, NEG_BIG)
                scores.append(s)
        alphas, probs = [], []
        for h in range(N_HEADS):
            m_old = m_scr[h]
            m_new = jnp.maximum(m_old, jnp.max(scores[h], axis=-1, keepdims=True))
            alpha = jnp.exp(m_old - m_new)
            p = jnp.exp(scores[h] - m_new)
            l_scr[h] = alpha * l_scr[h] + jnp.sum(p, axis=-1, keepdims=True)
            m_scr[h] = m_new
            alphas.append(alpha)
            probs.append(p.astype(BF16))
        for j in pairs:
            vb = v_ref[0, :, cols[j]].astype(BF16)
            pv0 = jnp.dot(probs[2 * j], vb, preferred_element_type=F32)
            pv1 = jnp.dot(probs[2 * j + 1], vb, preferred_element_type=F32)
            acc_scr[:, cols[j]] = (acc_scr[:, cols[j]] * jnp.where(lane_a, alphas[2 * j], alphas[2 * j + 1])
                                   + jnp.where(lane_a, pv0, pv1))

    if n_past_blocks:
        @pl.when(ki < n_past_blocks)
        def _():
            step(kp_ref, vp_ref, fkp_ref, False)

    kn = ki - n_past_blocks

    @pl.when(jnp.logical_and(kn >= 0, kn < qi))
    def _():
        step(kn_ref, vn_ref, fkn_ref, False)

    @pl.when(kn == qi)
    def _():
        step(kn_ref, vn_ref, fkn_ref, True)

    @pl.when(ki == nk - 1)
    def _():
        for j in range(N_HEADS // 2):
            cols = slice(j * LANES, (j + 1) * LANES)
            l = jnp.where(lane_a, l_scr[2 * j], l_scr[2 * j + 1])
            o_ref[0, :, cols] = acc_scr[:, cols] / l * sg_ref[0, :, cols]


def _fox_attention(q, fq, sg, k_new, v_new, fk_new_t, past=None, tq=512, tk_past=512):
    b, t, _ = q.shape
    nq = t // tq
    n_past_blocks = 0 if past is None else past[0].shape[1] // tk_past
    nk = n_past_blocks + nq

    def new_idx(i, qi, ki):
        return jnp.clip(ki - n_past_blocks, 0, qi)

    in_specs = [pl.BlockSpec((1, tq, WIDTH), lambda i, qi, ki: (i, qi, 0)),
                pl.BlockSpec((1, tq, N_HEADS), lambda i, qi, ki: (i, qi, 0)),
                pl.BlockSpec((1, tq, WIDTH), lambda i, qi, ki: (i, qi, 0))]
    args = [q, fq, sg]
    if n_past_blocks:
        def past_idx(i, qi, ki):
            return jnp.minimum(ki, n_past_blocks - 1)
        in_specs += [pl.BlockSpec((1, tk_past, WIDTH), lambda i, qi, ki: (i, past_idx(i, qi, ki), 0)),
                     pl.BlockSpec((1, tk_past, WIDTH), lambda i, qi, ki: (i, past_idx(i, qi, ki), 0)),
                     pl.BlockSpec((1, N_HEADS, tk_past), lambda i, qi, ki: (i, 0, past_idx(i, qi, ki)))]
        args += list(past)
    in_specs += [pl.BlockSpec((1, tq, WIDTH), lambda i, qi, ki: (i, new_idx(i, qi, ki), 0)),
                 pl.BlockSpec((1, tq, WIDTH), lambda i, qi, ki: (i, new_idx(i, qi, ki), 0)),
                 pl.BlockSpec((1, N_HEADS, tq), lambda i, qi, ki: (i, 0, new_idx(i, qi, ki)))]
    args += [k_new, v_new, fk_new_t]
    return pl.pallas_call(
        functools.partial(_fox_kernel, n_past_blocks=n_past_blocks, tq=tq),
        grid=(b, nq, nk),
        in_specs=in_specs,
        out_specs=pl.BlockSpec((1, tq, WIDTH), lambda i, qi, ki: (i, qi, 0)),
        out_shape=jax.ShapeDtypeStruct((b, t, WIDTH), F32),
        scratch_shapes=[pltpu.VMEM((N_HEADS, tq, 1), F32), pltpu.VMEM((N_HEADS, tq, 1), F32),
                        pltpu.VMEM((tq, WIDTH), F32)],
        compiler_params=_cparams(("parallel", "parallel", "arbitrary")),
        name="fox_attention",
    )(*args)


def _rwkv_kernel(p_ref, sh0_ref, s0_ref, mu_ref, w0_ref, wb_ref, a0_ref, ab_ref, gb_ref, kk_ref, ka_ref, rk_ref,
                 lnw_ref, lnb_ref, gm_ref, y_ref, st_ref, sht_ref, z_scr, prev_scr, *, c):
    t = pl.program_id(1)
    nt = pl.num_programs(1)
    n_rows = p_ref.shape[1]
    n_chunks = n_rows // c

    def head_block(h):
        lo = (h % HEADS_PER_GROUP) * HEAD_DIM
        return h // HEADS_PER_GROUP, slice(lo, lo + HEAD_DIM)

    @pl.when(t == 0)
    def _():
        z_scr[...] = jnp.zeros(z_scr.shape, F32)
        for h in range(N_HEADS):
            i, blk = head_block(h)
            z_scr[i, blk, blk] = s0_ref[0, h]
        prev_scr[...] = sh0_ref[0]

    p = p_ref[0]
    row = lax.broadcasted_iota(I32, p.shape, 0)
    prev = jnp.where(row == 0, prev_scr[...], pltpu.roll(p, 1, 0))
    last = p[n_rows - 1:n_rows, :]
    prev_scr[...] = last
    sht_ref[0] = last

    pm = p + (prev - p) * mu_ref[...]
    r = pm[:, 0:WIDTH]
    k = pm[:, WIDTH:2 * WIDTH]
    v = pm[:, 2 * WIDTH:3 * WIDTH]
    o1 = 3 * WIDTH
    wd = pm[:, o1:o1 + DECAY_LORA]
    ad = pm[:, o1 + DECAY_LORA:o1 + DECAY_LORA + ICLR_LORA]
    gd = pm[:, o1 + DECAY_LORA + ICLR_LORA:RWKV_COLS]

    w = -_softplus(-(w0_ref[...] + _dot(jnp.tanh(wd), wb_ref[...]))) - 0.5
    lw = -jnp.exp(w)
    a = _sigmoid(a0_ref[...] + _dot(ad, ab_ref[...]))
    g = _dot(_sigmoid(gd), gb_ref[...])
    kk = k * kk_ref[...]
    kk = kk / jnp.maximum(jnp.sqrt(_gsum(kk * kk, gm_ref)), L2_EPS)
    kf = k * (1.0 + (a - 1.0) * ka_ref[...])

    ri = lax.broadcasted_iota(I32, (n_rows, n_rows), 0)
    ci = lax.broadcasted_iota(I32, (n_rows, n_rows), 1)
    same_chunk = (ri // c) == (ci // c)
    cum = _dot_exact_rhs(jnp.logical_and(same_chunk, ri >= ci).astype(F32), lw)
    cum_last = _dot_exact_rhs(same_chunk.astype(F32), lw)
    r_t = r * jnp.exp(cum)
    a_t = -kk * jnp.exp(cum - lw)
    inv = jnp.exp(-cum)
    b_t = kk * a * inv
    k_t = kf * inv
    to_end = jnp.exp(cum_last - cum)
    b_e = kk * a * to_end
    k_e = kf * to_end
    g_end = jnp.exp(cum_last)

    hg = HEADS_PER_GROUP
    gw = hg * HEAD_DIM
    log_c = int(math.log2(c))
    t_idx = lax.broadcasted_iota(I32, (c, hg * c), 0)
    s_idx = lax.broadcasted_iota(I32, (c, hg * c), 1) & (c - 1)
    strict = s_idx < t_idx
    lower = s_idx <= t_idx
    eye = (s_idx == t_idx).astype(F32)
    rb = lax.broadcasted_iota(I32, (hg * c, gw), 0) >> log_c
    mask_kv = rb == (lax.broadcasted_iota(I32, (hg * c, gw), 1) >> int(math.log2(HEAD_DIM)))
    rs = lax.broadcasted_iota(I32, (hg * c, hg * c), 0) >> log_c
    mask_ss = rs == (lax.broadcasted_iota(I32, (hg * c, hg * c), 1) >> log_c)
    ng = N_HEADS // hg
    cat = functools.partial(jnp.concatenate, axis=0)
    units = [(slice(j * c, (j + 1) * c), slice(i * gw, (i + 1) * gw)) for j in range(n_chunks) for i in range(ng)]
    nu = len(units)

    def mm1(a, b_bd):
        return jnp.dot(a.astype(BF16), b_bd, preferred_element_type=F32)

    def dg(a, b, dims):
        return lax.dot_general(a.astype(BF16), b.astype(BF16), dims, preferred_element_type=F32)

    def bd1(x, mask):
        tiled = jnp.concatenate([x.astype(BF16)] * hg, axis=0)
        return jnp.where(mask, tiled, jnp.zeros_like(tiled))

    ar = [cat([a_t[rs_, s], r_t[rs_, s]]).astype(BF16) for rs_, s in units]
    ab = [dg(ar[n], bd1(b_t[units[n]], mask_kv), NT) for n in range(nu)]
    ak = [dg(ar[n], bd1(k_t[units[n]], mask_kv), NT) for n in range(nu)]
    l_ab = [jnp.where(strict, m[:c], 0.0) for m in ab]
    l_rb = [jnp.where(lower, m[c:], 0.0) for m in ab]
    l_ak = [jnp.where(strict, m[:c], 0.0) for m in ak]
    l_rk = [jnp.where(lower, m[c:], 0.0) for m in ak]
    tinv = [eye + m for m in l_ab]
    pw = [mm1(m, bd1(m, mask_ss)) for m in l_ab]
    for _ in range(1, log_c - 1):
        res = [mm1(cat([tinv[n], pw[n]]), bd1(pw[n], mask_ss)) for n in range(nu)]
        tinv = [tinv[n] + res[n][:c] for n in range(nu)]
        pw = [m[c:] for m in res]
    tinv = [tinv[n] + mm1(tinv[n], bd1(pw[n], mask_ss)) for n in range(nu)]
    av = [mm1(cat([l_ak[n], l_rk[n]]), bd1(v[units[n]], mask_kv)) for n in range(nu)]
    ue = [cat([b_e[units[n]], k_e[units[n]]]).astype(BF16) for n in range(nu)]

    def wide(fn, x):
        return [fn(x[:, :gw]), fn(x[:, gw:])]

    def bd1w(x):
        return jnp.concatenate(wide(lambda h_: bd1(h_, mask_kv), x), axis=1)

    rhs = [jnp.concatenate([a_t[units[n]], av[n][:c]], axis=1) for n in range(nu)]
    x0 = [mm1(tinv[n], bd1w(rhs[n])) for n in range(nu)]
    resid = [rhs[n] - (x0[n] - mm1(l_ab[n], bd1w(x0[n]))) for n in range(nu)]
    sol = [x0[n] + mm1(tinv[n], bd1w(resid[n])) for n in range(nu)]
    lift = [mm1(l_rb[n], bd1w(sol[n])) for n in range(nu)]
    lhs_s = [cat([sol[n][:, :gw], r_t[units[n]] + lift[n][:, :gw]]).astype(BF16) for n in range(nu)]
    u_loc = [sol[n][:, gw:] for n in range(nu)]
    o_loc = [av[n][c:] + lift[n][:, gw:] for n in range(nu)]

    zr = lax.broadcasted_iota(I32, (gw, gw), 0) >> int(math.log2(HEAD_DIM))
    zmask = zr == (lax.broadcasted_iota(I32, (gw, gw), 1) >> int(math.log2(HEAD_DIM)))
    z = [z_scr[i] for i in range(ng)]
    o_rows = []
    for j in range(n_chunks):
        o_grp = []
        for i in range(ng):
            n = j * ng + i
            rs_, s = units[n]
            sz = dg(lhs_s[n], z[i], NT)
            u = sz[:c] + u_loc[n]
            o_grp.append(sz[c:] + o_loc[n])
            upd = dg(cat([u, v[rs_, s]]), ue[n], TN)
            z[i] = z[i] * g_end[j * c:j * c + 1, s] + jnp.where(zmask, upd, 0.0)
        o_rows.append(jnp.concatenate(o_grp, axis=1))
    for i in range(ng):
        z_scr[i] = z[i]

    o = cat(o_rows)
    inv_hd = 1.0 / HEAD_DIM
    dlt = o - _gsum(o, gm_ref) * inv_hd
    var = _gsum(dlt * dlt, gm_ref) * inv_hd
    on = dlt * lax.rsqrt(var + RWKV_GN_EPS) * lnw_ref[...] + lnb_ref[...]
    bonus = _gsum(r * kf * rk_ref[...], gm_ref) * v
    y_ref[0] = (on + bonus) * g

    @pl.when(t == nt - 1)
    def _():
        for h in range(N_HEADS):
            i, blk = head_block(h)
            st_ref[0, h] = z_scr[i, blk, blk]


def _rwkv(p, shift0, s0, prm, gmat, chunk, chunks_per_step):
    b, t, _ = p.shape
    row = lambda n: _const_spec((1, n))
    rows = chunk * chunks_per_step
    return pl.pallas_call(
        functools.partial(_rwkv_kernel, c=chunk),
        grid=(b, t // rows),
        in_specs=[pl.BlockSpec((1, rows, RWKV_COLS), lambda i, j: (i, j, 0)),
                  pl.BlockSpec((1, 1, RWKV_COLS), lambda i, j: (i, 0, 0)),
                  pl.BlockSpec((1, N_HEADS, HEAD_DIM, HEAD_DIM), lambda i, j: (i, 0, 0, 0)),
                  row(RWKV_COLS), row(WIDTH), _const_spec((DECAY_LORA, WIDTH)), row(WIDTH),
                  _const_spec((ICLR_LORA, WIDTH)), _const_spec((GATE_LORA, WIDTH)),
                  row(WIDTH), row(WIDTH), row(WIDTH), row(WIDTH), row(WIDTH), _const_spec((WIDTH, WIDTH))],
        out_specs=[pl.BlockSpec((1, rows, WIDTH), lambda i, j: (i, j, 0)),
                   pl.BlockSpec((1, N_HEADS, HEAD_DIM, HEAD_DIM), lambda i, j: (i, 0, 0, 0)),
                   pl.BlockSpec((1, 1, RWKV_COLS), lambda i, j: (i, 0, 0))],
        out_shape=[jax.ShapeDtypeStruct((b, t, WIDTH), F32),
                   jax.ShapeDtypeStruct((b, N_HEADS, HEAD_DIM, HEAD_DIM), F32),
                   jax.ShapeDtypeStruct((b, 1, RWKV_COLS), F32)],
        scratch_shapes=[pltpu.VMEM((N_HEADS // HEADS_PER_GROUP, HEADS_PER_GROUP * HEAD_DIM,
                                    HEADS_PER_GROUP * HEAD_DIM), F32),
                        pltpu.VMEM((1, RWKV_COLS), F32)],
        compiler_params=_cparams(("parallel", "arbitrary")),
        name="rwkv7_mix",
    )(p, shift0, s0, prm["mu"], prm["w0"], prm["wb"], prm["a0"], prm["ab"], prm["gb"], prm["kk"], prm["ka"],
      prm["rk"], prm["lnw"], prm["lnb"], gmat)


def _merge_kernel(x_ref, ya_ref, yb_ref, gate_ref, mod_ref, g2_ref, woa_ref, wob_ref, wo_ref, wrh_ref, wrl_ref,
                  *rest):
    x1_ref, h2_ref, lg_ref = rest[-3:]
    bb, tt, d = x_ref.shape
    m = bb * tt
    gate = gate_ref[...].reshape(m, GATE_COLS)
    merged = (gate[:, 0:d] * _dot(ya_ref[...].reshape(m, WIDTH), woa_ref[...])
              + gate[:, d:2 * d] * _dot(yb_ref[...].reshape(m, WIDTH), wob_ref[...]))
    x1 = x_ref[...] + mod_ref[:, 2:3, :] * _dot(merged, wo_ref[...]).reshape(bb, tt, d)
    x1_ref[...] = x1
    ms = jnp.mean(x1 * x1, axis=-1, keepdims=True)
    h2 = x1 * lax.rsqrt(ms + RMS_EPS) * g2_ref[...]
    h2 = (h2 * (1.0 + mod_ref[:, 4:5, :]) + mod_ref[:, 3:4, :]).reshape(m, d)
    h2_ref[...] = _pack_bf16_pairs(h2)
    lg_ref[...] = _mm3((wrh_ref[...], wrl_ref[...]), _split2(h2), NT)


def _merge(x, ya, yb, gate, mod, w, n_total, row_offset, shared=None):
    b, t, d = x.shape
    bb, tt = _token_blocks(b, t, MOE_TOKEN_ROWS)
    nt = t // tt
    m = bb * tt
    off = row_offset // m

    def tok(cols):
        return pl.BlockSpec((bb, tt, cols), lambda i, j: (i, j, 0))

    in_specs = [tok(d), tok(WIDTH), tok(WIDTH), tok(GATE_COLS),
                pl.BlockSpec((bb, 6, d), lambda i, j: (i, 0, 0)),
                _const_spec((1, d)), _const_spec((WIDTH, d)), _const_spec((WIDTH, d)), _const_spec((d, d)),
                _const_spec((N_EXPERTS, d)), _const_spec((N_EXPERTS, d))]
    args = [x, ya, yb, gate, mod.reshape(b, 6, d), w["g2"], w["w_oa"], w["w_ob"], w["w_o"], w["wr_hi"], w["wr_lo"]]
    aliases = {}
    if shared is not None:
        aliases = {len(args): 1, len(args) + 1: 2}
        in_specs += [pl.BlockSpec(memory_space=pl.ANY), pl.BlockSpec(memory_space=pl.ANY)]
        args += list(shared)
    return pl.pallas_call(
        _merge_kernel,
        grid=(b // bb, nt),
        in_specs=in_specs,
        out_specs=[tok(d), pl.BlockSpec((m, d // 2), lambda i, j: (off + i * nt + j, 0)),
                   pl.BlockSpec((N_EXPERTS, m), lambda i, j: (0, off + i * nt + j))],
        out_shape=[jax.ShapeDtypeStruct((b, t, d), F32), jax.ShapeDtypeStruct((n_total, d // 2), I32),
                   jax.ShapeDtypeStruct((N_EXPERTS, n_total), F32)],
        input_output_aliases=aliases,
        compiler_params=_cparams(("parallel", "parallel")),
        name="merge_norm2_router",
    )(*args)


def _route_kernel(lg_ref, bias_ref, idx_ref, wt_ref, rank_ref, cnt_ref, carry):
    @pl.when(pl.program_id(0) == 0)
    def _():
        carry[...] = jnp.zeros(carry.shape, F32)

    tm = lg_ref.shape[1]
    scores = _sigmoid(lg_ref[...])
    sel = scores + bias_ref[...]
    row = lax.broadcasted_iota(I32, (N_EXPERTS, tm), 0)
    neg_inf = -jnp.inf

    def first_argmax(vals, rows):
        mx = jnp.max(vals, axis=0, keepdims=True)
        return mx, jnp.min(jnp.where(vals == mx, rows, N_EXPERTS), axis=0, keepdims=True)

    gslices = [slice(g * EXPERTS_PER_GROUP, (g + 1) * EXPERTS_PER_GROUP) for g in range(N_GROUPS)]
    gs = []
    row_g = lax.broadcasted_iota(I32, (EXPERTS_PER_GROUP, tm), 0)
    for sl in gslices:
        m1, i1 = first_argmax(sel[sl], row_g)
        m2 = jnp.max(jnp.where(row_g == i1, neg_inf, sel[sl]), axis=0, keepdims=True)
        gs.append(m1 + m2)
    kept = []
    for g in range(N_GROUPS):
        beaten = jnp.zeros((1, tm), I32)
        for o in range(N_GROUPS):
            if o != g:
                wins = (gs[o] >= gs[g]) if o < g else (gs[o] > gs[g])
                beaten = beaten + wins.astype(I32)
        kept.append(jnp.where(beaten < TOPK_GROUPS, sel[gslices[g]], neg_inf))
    cur = jnp.concatenate(kept, axis=0)

    idxs, ws = [], []
    picked = jnp.zeros((N_EXPERTS, tm), F32)
    for _ in range(TOP_K):
        _, ik = first_argmax(cur, row)
        hit = row == ik
        idxs.append(ik)
        ws.append(jnp.sum(jnp.where(hit, scores, 0.0), axis=0, keepdims=True))
        cur = jnp.where(hit, neg_inf, cur)
        picked = jnp.where(hit, 1.0, picked)
    wsum = ws[0]
    for k in range(1, TOP_K):
        wsum = wsum + ws[k]

    r = lax.broadcasted_iota(I32, (tm, tm), 0)
    c = lax.broadcasted_iota(I32, (tm, tm), 1)
    before = jnp.dot(picked.astype(BF16), (r < c).astype(BF16), preferred_element_type=F32) + carry[...]
    carry[...] = carry[...] + jnp.sum(picked, axis=1, keepdims=True)
    cnt_ref[...] = carry[...]

    kk = lax.broadcasted_iota(I32, (TOP_K, tm), 0)
    idx_o = jnp.zeros((TOP_K, tm), I32)
    wt_o = jnp.zeros((TOP_K, tm), F32)
    rank_o = jnp.zeros((TOP_K, tm), F32)
    for k in range(TOP_K):
        rk = jnp.sum(jnp.where(row == idxs[k], before, 0.0), axis=0, keepdims=True)
        idx_o = jnp.where(kk == k, idxs[k], idx_o)
        wt_o = jnp.where(kk == k, ws[k] / wsum * ROUTED_SCALE, wt_o)
        rank_o = jnp.where(kk == k, rk, rank_o)
    idx_ref[...] = idx_o
    wt_ref[...] = wt_o
    rank_ref[...] = rank_o.astype(I32)


def _route(logits_t, bias_col, tm):
    n = logits_t.shape[1]
    tokk = pl.BlockSpec((TOP_K, tm), lambda i: (0, i))
    return pl.pallas_call(
        _route_kernel,
        grid=(n // tm,),
        in_specs=[pl.BlockSpec((N_EXPERTS, tm), lambda i: (0, i)), _const_spec((N_EXPERTS, 1))],
        out_specs=[tokk, tokk, tokk, _const_spec((N_EXPERTS, 1))],
        out_shape=[jax.ShapeDtypeStruct((TOP_K, n), I32), jax.ShapeDtypeStruct((TOP_K, n), F32),
                   jax.ShapeDtypeStruct((TOP_K, n), I32), jax.ShapeDtypeStruct((N_EXPERTS, 1), F32)],
        scratch_shapes=[pltpu.VMEM((N_EXPERTS, 1), F32)],
        compiler_params=_cparams(("arbitrary",)),
        name="route_topk",
    )(logits_t, bias_col)


def _plan_kernel(cnt_ref, start_ref, be_ref, valid_ref, nu_ref, *, blk):
    cnt = cnt_ref[...]
    padded = jnp.ceil(cnt * (1.0 / blk)) * blk
    e_r = lax.broadcasted_iota(I32, (N_EXPERTS, N_EXPERTS), 0)
    e_c = lax.broadcasted_iota(I32, (N_EXPERTS, N_EXPERTS), 1)
    incl = (e_r <= e_c).astype(BF16)
    ph, pm, plo = _split3(jnp.broadcast_to(padded, (8, N_EXPERTS)))
    d2 = functools.partial(jnp.dot, preferred_element_type=F32)
    pad_end = (d2(ph, incl) + (d2(pm, incl) + d2(plo, incl)))[0:1, :]
    pad_start = pad_end - padded
    start_ref[...] = pad_start.astype(I32)
    total = jnp.max(pad_end, axis=-1, keepdims=True)
    nu_ref[...] = jnp.broadcast_to(total * (1.0 / blk), (1, N_EXPERTS)).astype(I32)
    nb = be_ref.shape[0]
    first = (lax.broadcasted_iota(I32, (nb, N_EXPERTS), 0) * blk).astype(F32)
    lane = lax.broadcasted_iota(I32, (nb, N_EXPERTS), 1)
    inside = jnp.logical_and(pad_start <= first, first < pad_end)
    be_ref[...] = jnp.sum(jnp.where(inside, lane, 0), axis=-1, keepdims=True)
    rows = jnp.minimum(pad_start + cnt - first, float(blk))
    valid_ref[...] = jnp.sum(jnp.where(inside, rows, 0.0), axis=-1, keepdims=True).astype(I32)


def _plan(counts, n_blocks, blk):
    return pl.pallas_call(
        functools.partial(_plan_kernel, blk=blk),
        out_shape=[jax.ShapeDtypeStruct((1, N_EXPERTS), I32), jax.ShapeDtypeStruct((n_blocks, 1), I32),
                   jax.ShapeDtypeStruct((n_blocks, 1), I32), jax.ShapeDtypeStruct((1, N_EXPERTS), I32)],
        compiler_params=pltpu.CompilerParams(vmem_limit_bytes=VMEM_LIMIT),
        name="dispatch_plan",
    )(counts)


def _dest_kernel(idx_ref, rank_ref, start_ref, dest_ref):
    tm = idx_ref.shape[1]
    row = lax.broadcasted_iota(I32, (N_EXPERTS, tm), 0)
    kk = lax.broadcasted_iota(I32, (TOP_K, tm), 0)
    idx = idx_ref[...]
    base = jnp.zeros((TOP_K, tm), I32)
    for k in range(TOP_K):
        bk = jnp.sum(jnp.where(row == idx[k:k + 1, :], start_ref[...], 0), axis=0, keepdims=True)
        base = jnp.where(kk == k, bk, base)
    dest_ref[...] = base + rank_ref[...]


def _dest(idx, rank, pad_start_col, tm):
    n = idx.shape[1]
    tokk = pl.BlockSpec((TOP_K, tm), lambda i: (0, i))
    return pl.pallas_call(
        _dest_kernel,
        grid=(n // tm,),
        in_specs=[tokk, tokk, _const_spec((N_EXPERTS, 1))],
        out_specs=tokk,
        out_shape=jax.ShapeDtypeStruct((TOP_K, n), I32),
        compiler_params=_cparams(("parallel",)),
        name="dispatch_dest",
    )(idx, rank, pad_start_col)


def _dispatch(h2p, dest_t, n_slots):
    n, wp = h2p.shape
    half = wp // SC_ROW_SPLIT
    window = SC_SCATTER_WINDOW
    mesh = plsc.VectorSubcoreMesh(core_axis_name="core", subcore_axis_name="subcore")
    out = jax.ShapeDtypeStruct((n_slots, half), h2p.dtype)

    @functools.partial(pl.kernel, out_type=[out] * SC_ROW_SPLIT, mesh=mesh, scratch_types=[])
    def scatter(rows_hbm, idx_hbm, *outs):
        for c, out_hbm in enumerate(outs):
            def body(rows_vmem, idx_vmem, out_hbm=out_hbm):
                for k in range(TOP_K):
                    pltpu.sync_copy(rows_vmem, out_hbm.at[idx_vmem.at[k]])

            pltpu.emit_pipeline(
                body, grid=(n // window,),
                in_specs=[pl.BlockSpec((window, half), index_map=lambda i, c=c: (i, c)),
                          pl.BlockSpec((TOP_K, window), index_map=lambda i: (0, i))],
                out_specs=[], core_axis_name=("core", "subcore"), dimension_semantics=(pltpu.PARALLEL,),
            )(rows_hbm, idx_hbm)

    return scatter(h2p, dest_t)


def _expert_kernel(be_ref, valid_ref, nu_ref, xa_ref, xb_ref, wg_ref, wu_ref, wd_ref, *rest):
    y_refs, (wg_b, wu_b, wd_b) = rest[:SC_ROW_SPLIT], rest[SC_ROW_SPLIT:]
    i = pl.program_id(0)
    nv = valid_ref[i]
    new_expert = jnp.logical_or(i == 0, be_ref[i] != be_ref[jnp.maximum(i - 1, 0)])

    @pl.when(jnp.logical_and(nv > 0, new_expert))
    def _():
        wg_b[...] = wg_ref[0].astype(BF16)
        wu_b[...] = wu_ref[0].astype(BF16)
        wd_b[...] = wd_ref[0].astype(BF16)

    @pl.when(nv > 0)
    def _():
        blk = xa_ref.shape[0]
        rows = lax.broadcasted_iota(I32, (blk, 1), 0)
        packed = jnp.concatenate([xa_ref[...], xb_ref[...]], axis=1)
        x = _unpack_bf16_pairs(jnp.where(rows < nv, packed, 0))
        hg = jnp.dot(x, wg_b[...], preferred_element_type=F32)
        hu = jnp.dot(x, wu_b[...], preferred_element_type=F32)
        y = _pack_bf16_pairs(jnp.dot((_silu(hg) * hu).astype(BF16), wd_b[...], preferred_element_type=F32))
        cw = y.shape[1] // SC_ROW_SPLIT
        for c, y_ref in enumerate(y_refs):
            y_ref[...] = y[:, c * cw:(c + 1) * cw]


def _experts(xs, block_e, valid, n_used, w_eg, w_eu, w_ed, blk):
    xa, xb = xs
    n_slots, packed = xa.shape
    d = w_eg.shape[1]
    n_blocks = n_slots // blk

    def row_blk(i, be, valid, nu):
        return (jnp.minimum(i, nu[0] - 1), 0)

    def w_blk(i, be, valid, nu):
        return (be[i], 0, 0)

    return pl.pallas_call(
        _expert_kernel,
        grid_spec=pltpu.PrefetchScalarGridSpec(
            num_scalar_prefetch=3,
            grid=(n_blocks,),
            in_specs=[pl.BlockSpec((blk, packed), row_blk), pl.BlockSpec((blk, packed), row_blk),
                      pl.BlockSpec((1, d, D_EXPERT), w_blk), pl.BlockSpec((1, d, D_EXPERT), w_blk),
                      pl.BlockSpec((1, D_EXPERT, d), w_blk)],
            out_specs=[pl.BlockSpec((blk, packed), row_blk)] * SC_ROW_SPLIT,
            scratch_shapes=[pltpu.VMEM((d, D_EXPERT), BF16), pltpu.VMEM((d, D_EXPERT), BF16),
                            pltpu.VMEM((D_EXPERT, d), BF16)]),
        out_shape=[jax.ShapeDtypeStruct((n_slots, packed), I32)] * SC_ROW_SPLIT,
        compiler_params=_cparams(("arbitrary",)),
        name="moe_experts",
    )(block_e, valid, n_used, xa, xb, w_eg, w_eu, w_ed)


def _combine_gather(ys, dest_t):
    k, n = dest_t.shape
    cw = ys[0].shape[1]
    window = SC_SCATTER_WINDOW
    mesh = plsc.VectorSubcoreMesh(core_axis_name="core", subcore_axis_name="subcore")

    @functools.partial(pl.kernel, out_type=jax.ShapeDtypeStruct((k * n, cw * len(ys)), ys[0].dtype), mesh=mesh,
                       scratch_types=[])
    def gather(*refs):
        y_refs, idx_hbm, out_hbm = refs[:len(ys)], refs[len(ys)], refs[len(ys) + 1]
        for c, y_hbm in enumerate(y_refs):
            def body(idx_vmem, out_vmem, y_hbm=y_hbm):
                pltpu.sync_copy(y_hbm.at[idx_vmem.at[0]], out_vmem)

            pltpu.emit_pipeline(
                body, grid=(k * n // window,),
                in_specs=[pl.BlockSpec((1, window), index_map=lambda i: (0, i))],
                out_specs=[pl.BlockSpec((window, cw), index_map=lambda i, c=c: (i, c))],
                core_axis_name=("core", "subcore"), dimension_semantics=(pltpu.PARALLEL,),
            )(idx_hbm, out_hbm)

    return gather(*ys, dest_t.reshape(1, k * n)).reshape(k, n, cw * len(ys))


def _final_kernel(x1_ref, h2_ref, wt_ref, mod_ref, wsg_ref, wsu_ref, wsd_ref, yg_ref, o_ref):
    bb, tt, d = x1_ref.shape
    hb = _unpack_bf16_pairs(h2_ref[...])
    hg = jnp.dot(hb, wsg_ref[...], preferred_element_type=F32)
    hu = jnp.dot(hb, wsu_ref[...], preferred_element_type=F32)
    ffn = _dot(_silu(hg) * hu, wsd_ref[...])
    wt = wt_ref[...]
    for k in range(TOP_K):
        ffn = ffn + wt[:, k:k + 1] * _unpack_bf16_pairs(yg_ref[k]).astype(F32)
    o_ref[...] = x1_ref[...] + mod_ref[:, 5:6, :] * ffn.reshape(bb, tt, d)


def _final(x1, h2_all, wts_all, y_tok, mod, w, row_offset):
    b, t, d = x1.shape
    bb, tt = _token_blocks(b, t, MOE_TOKEN_ROWS)
    nt = t // tt
    m = bb * tt
    off = row_offset // m

    def flat_idx(i, j):
        return off + i * nt + j

    return pl.pallas_call(
        _final_kernel,
        grid=(b // bb, nt),
        in_specs=[pl.BlockSpec((bb, tt, d), lambda i, j: (i, j, 0)),
                  pl.BlockSpec((m, h2_all.shape[1]), lambda i, j: (flat_idx(i, j), 0)),
                  pl.BlockSpec((m, TOP_K), lambda i, j: (flat_idx(i, j), 0)),
                  pl.BlockSpec((bb, 6, d), lambda i, j: (i, 0, 0)),
                  _const_spec((d, D_EXPERT)), _const_spec((d, D_EXPERT)), _const_spec((D_EXPERT, d)),
                  pl.BlockSpec((TOP_K, m, y_tok.shape[2]), lambda i, j: (0, flat_idx(i, j), 0))],
        out_specs=pl.BlockSpec((bb, tt, d), lambda i, j: (i, j, 0)),
        out_shape=jax.ShapeDtypeStruct((b, t, d), F32),
        compiler_params=_cparams(("parallel", "parallel")),
        name="moe_combine_final",
    )(x1, h2_all, wts_all, mod.reshape(b, 6, d), w["w_sg"], w["w_su"], w["w_sd"], y_tok)


def _largest_tile(n, candidates):
    return next(c for c in candidates if n % c == 0)


def _moe_routed(h2_all, logits_all, w, blk=EXPERT_BLOCK_ROWS):
    n = h2_all.shape[0]
    n_blocks = (n * TOP_K + N_EXPERTS * (blk - 1)) // blk + 1
    n_blocks = (n_blocks + 7) // 8 * 8
    idx, wts_t, rank, counts = _route(logits_all, w["router_bias"], _largest_tile(n, (512, 256)))
    pad_start, block_e, valid, n_used = _plan(counts.reshape(1, N_EXPERTS), n_blocks, blk)
    block_e = block_e.reshape(n_blocks)
    valid = valid.reshape(n_blocks)
    n_used = n_used[0, 0:1]
    dest_t = _dest(idx, rank, pad_start.reshape(N_EXPERTS, 1), _largest_tile(n, (1024, 512, 256)))
    xs = _dispatch(h2_all, dest_t, n_blocks * blk)
    ys = _experts(xs, block_e, valid, n_used, w["w_eg"], w["w_eu"], w["w_ed"], blk)
    return _combine_gather(ys, dest_t), jnp.transpose(wts_t)


def _prep(raw):
    p = {k: v[0] for k, v in raw.items()}
    w_in = p["w_in"]
    o_fox = RWKV_COLS
    o_fl = o_fox + FOX_MAIN_COLS
    o_gate = o_fl + N_HEADS
    row = lambda a: a.reshape(1, -1)
    return dict(
        w_ada=p["w_ada"], b_ada=p["b_ada"],
        g1=row(p["norm1_g"]), g2=row(p["norm2_g"]),
        wr=w_in[:, :o_fox].astype(BF16),
        wf=w_in[:, o_fox:o_fl].astype(BF16),
        wfl=jnp.pad(w_in[:, o_fl:o_gate], ((0, 0), (0, LANES - N_HEADS))).astype(BF16),
        wg=w_in[:, o_gate:].astype(BF16),
        qn=row(jnp.tile(p["fox_q_norm"], N_HEADS)), kn=row(jnp.tile(p["fox_k_norm"], N_HEADS)),
        fb=row(p["fox_f_bias"]),
        gmat=_group_ones(),
        rwkv=dict(mu=row(p["rwkv_mu"]), w0=row(p["rwkv_w0"]), wb=p["rwkv_w_lora_b"], a0=row(p["rwkv_a0"]),
                  ab=p["rwkv_a_lora_b"], gb=p["rwkv_g_lora_b"], kk=row(p["rwkv_k_k"]), ka=row(p["rwkv_k_a"]),
                  rk=row(p["rwkv_r_k"]), lnw=row(p["rwkv_ln_w"]), lnb=row(p["rwkv_ln_b"])),
        w_oa=p["w_out_rwkv"].astype(BF16), w_ob=p["w_out_fox"].astype(BF16), w_o=p["w_out"].astype(BF16),
        wr_hi=p["w_router"].T.astype(BF16),
        wr_lo=(p["w_router"] - p["w_router"].astype(BF16).astype(F32)).T.astype(BF16),
        router_bias=p["router_bias"].reshape(N_EXPERTS, 1),
        w_eg=p["w_exp_gate"], w_eu=p["w_exp_up"], w_ed=p["w_exp_down"],
        w_sg=p["w_sh_gate"].astype(BF16), w_su=p["w_sh_up"].astype(BF16), w_sd=p["w_sh_down"].astype(BF16),
    )


def _token_blocks(b, t, rows=256):
    if t >= rows:
        return 1, rows
    bb = max(1, min(b, 256 // t))
    while b % bb:
        bb -= 1
    return bb, t


def _mix_path(x, mod, shift0, wkv0, past_k, past_v, past_logf, w):
    b, t, d = x.shape
    bb, tt = _token_blocks(b, t, INPROJ_TOKEN_ROWS)
    n_past = past_k.shape[1]
    if n_past:
        f_past = _past_cumsum(past_logf)
        init = f_past[:, n_past - 1:n_past, :]
        past = (past_k, past_v, jnp.swapaxes(f_past, 1, 2))
    else:
        init = jnp.zeros((b, 1, N_HEADS), F32)
        past = None
    pr, q, k, v, sg, logf, f_new, gate = _inproj(x, mod.reshape(b, 6, d), w["g1"], w["wr"], w["wf"], w["wfl"],
                                                 w["wg"], w["qn"], w["kn"], w["fb"], w["gmat"], init, bb, tt)
    y_fox = _fox_attention(q, f_new, sg, k, v, jnp.swapaxes(f_new, 1, 2), past=past, tq=min(t, 512),
                           tk_past=min(max(n_past, 1), 512))
    chunk = min(t, RWKV_CHUNK)
    y_rwkv, wkv_new, shift_new = _rwkv(pr, shift0.reshape(b, 1, RWKV_COLS), wkv0, w["rwkv"], w["gmat"],
                                       chunk, max(1, min(RWKV_CHUNKS_PER_STEP, t // chunk)))
    return y_rwkv, y_fox, gate, wkv_new, shift_new, k, v, logf


def _layer(paths, w):
    n_b = [p[0].shape[0] for p in paths]
    mod_all = _ada(jnp.concatenate([p[1] for p in paths], axis=0), w["w_ada"], w["b_ada"])
    mods, o = [], 0
    for nb in n_b:
        mods.append(mod_all[o:o + nb])
        o += nb
    n_total = sum(p[0].shape[0] * p[0].shape[1] for p in paths)
    mixed, x1s = [], []
    shared, row = None, 0
    for (x, _, shift0, wkv0, pk, pv, plf), mod in zip(paths, mods):
        ya, yb, gate, wkv_new, shift_new, k, v, logf = _mix_path(x, mod, shift0, wkv0, pk, pv, plf, w)
        x1, h2_all, lg_all = _merge(x, ya, yb, gate, mod, w, n_total, row, shared)
        shared = (h2_all, lg_all)
        row += x.shape[0] * x.shape[1]
        mixed.append((wkv_new, shift_new, k, v, logf))
        x1s.append(x1)
    y_tok, wts = _moe_routed(h2_all, lg_all, w)
    outs, row = [], 0
    for x1, mod, st in zip(x1s, mods, mixed):
        y = _final(x1, h2_all, wts, y_tok, mod, w, row)
        row += x1.shape[0] * x1.shape[1]
        outs.append((y,) + st)
    return outs


def kernel(x_prompt, x_sample, c_prompt, c_sample, state_rwkv_wkv, state_rwkv_shift, cache_fox_k, cache_fox_v,
           cache_fox_logf, w_ada, b_ada, norm1_g, norm2_g, w_in, rwkv_mu, rwkv_w0, rwkv_w_lora_b, rwkv_a0,
           rwkv_a_lora_b, rwkv_g_lora_b, rwkv_k_k, rwkv_k_a, rwkv_r_k, rwkv_ln_w, rwkv_ln_b, fox_q_norm,
           fox_k_norm, fox_f_bias, w_out_rwkv, w_out_fox, w_out, w_router, router_bias, w_exp_gate, w_exp_up,
           w_exp_down, w_sh_gate, w_sh_up, w_sh_down):
    raw = dict(w_ada=w_ada, b_ada=b_ada, norm1_g=norm1_g, norm2_g=norm2_g, w_in=w_in, rwkv_mu=rwkv_mu,
               rwkv_w0=rwkv_w0, rwkv_w_lora_b=rwkv_w_lora_b, rwkv_a0=rwkv_a0, rwkv_a_lora_b=rwkv_a_lora_b,
               rwkv_g_lora_b=rwkv_g_lora_b, rwkv_k_k=rwkv_k_k, rwkv_k_a=rwkv_k_a, rwkv_r_k=rwkv_r_k,
               rwkv_ln_w=rwkv_ln_w, rwkv_ln_b=rwkv_ln_b, fox_q_norm=fox_q_norm, fox_k_norm=fox_k_norm,
               fox_f_bias=fox_f_bias, w_out_rwkv=w_out_rwkv, w_out_fox=w_out_fox, w_out=w_out,
               w_router=w_router, router_bias=router_bias, w_exp_gate=w_exp_gate, w_exp_up=w_exp_up,
               w_exp_down=w_exp_down, w_sh_gate=w_sh_gate, w_sh_up=w_sh_up, w_sh_down=w_sh_down)
    assert w_in.shape[0] == 1, "single-layer stack"
    w = _prep(raw)
    bp, tp, _ = x_prompt.shape
    bs, ts, _ = x_sample.shape
    n_past = cache_fox_k.shape[2]
    prompt = (x_prompt, c_prompt, jnp.zeros((bp, RWKV_COLS), F32),
              jnp.zeros((bp, N_HEADS, HEAD_DIM, HEAD_DIM), F32),
              jnp.zeros((bp, 0, WIDTH), F32), jnp.zeros((bp, 0, WIDTH), F32), jnp.zeros((bp, 0, N_HEADS), F32))
    sample = (x_sample, c_sample, state_rwkv_shift[0], state_rwkv_wkv[0],
              cache_fox_k[0].reshape(bs, n_past, WIDTH), cache_fox_v[0].reshape(bs, n_past, WIDTH),
              cache_fox_logf[0])
    (yp, wkv_p, sh_p, k_p, v_p, lf_p), (ysm, wkv_s, sh_s, k_s, v_s, lf_s) = _layer([prompt, sample], w)

    def heads(a):
        return a.reshape((1,) + a.shape[:2] + (N_HEADS, HEAD_DIM))

    return (yp, ysm,
            wkv_p[None], sh_p.reshape(1, bp, RWKV_COLS), heads(k_p), heads(v_p), lf_p[None],
            wkv_s[None], sh_s.reshape(1, bs, RWKV_COLS), heads(k_s), heads(v_s), lf_s[None])
```

```python
import functools
import math

import jax
import jax.numpy as jnp
from jax import lax
from jax.experimental import pallas as pl
from jax.experimental.pallas import tpu as pltpu
from jax.experimental.pallas import tpu_sc as plsc

F32 = jnp.float32
BF16 = jnp.bfloat16
I32 = jnp.int32

D_MODEL = 1024
N_HEADS = 8
HEAD_DIM = 64
WIDTH = N_HEADS * HEAD_DIM
HEADS_PER_GROUP = 4
RWKV_CHUNK = 64
RWKV_CHUNKS_PER_STEP = 4
FINAL_PARTS = 2
MOE_TOKEN_ROWS = 512
EXPERT_BLOCK_ROWS = 512
SC_SCATTER_WINDOW = 128
SC_ROW_SPLIT = 2
DECAY_LORA = 64
ICLR_LORA = 64
GATE_LORA = 128
RWKV_COLS = 3 * WIDTH + DECAY_LORA + ICLR_LORA + GATE_LORA
FOX_MAIN_COLS = 4 * WIDTH
GATE_COLS = 2 * D_MODEL
RWKV_GN_EPS = HEAD_DIM * 1e-5
L2_EPS = 1e-12
RMS_EPS = 1e-6
N_EXPERTS = 256
TOP_K = 8
N_GROUPS = 8
TOPK_GROUPS = 4
EXPERTS_PER_GROUP = N_EXPERTS // N_GROUPS
D_EXPERT = 256
ROUTED_SCALE = 2.5

LANES = 128
VMEM_LIMIT = 56 * 1024 * 1024
NEG_BIG = -1e30

NN = (((1,), (0,)), ((), ()))
NT = (((1,), (1,)), ((), ()))
TN = (((0,), (0,)), ((), ()))


def _cparams(sem):
    return pltpu.CompilerParams(dimension_semantics=sem, vmem_limit_bytes=VMEM_LIMIT)


def _dot(a, b, dims=NN):
    return lax.dot_general(a.astype(BF16), b.astype(BF16), dims, preferred_element_type=F32)


def _split2(a):
    hi = a.astype(BF16)
    lo = (a - hi.astype(F32)).astype(BF16)
    return hi, lo


def _split3(a):
    hi = a.astype(BF16)
    r1 = a - hi.astype(F32)
    mid = r1.astype(BF16)
    lo = (r1 - mid.astype(F32)).astype(BF16)
    return hi, mid, lo


def _mm3(a, b, dims):
    d = functools.partial(lax.dot_general, dimension_numbers=dims, preferred_element_type=F32)
    return d(a[0], b[0]) + (d(a[0], b[1]) + d(a[1], b[0]))


def _dot_exact_rhs(a_exact, b, dims=NN):
    ab = a_exact.astype(BF16)
    bh, bm, bl = _split3(b)
    d = functools.partial(lax.dot_general, dimension_numbers=dims, preferred_element_type=F32)
    return d(ab, bh) + (d(ab, bm) + d(ab, bl))


def _gsum(x, g_ref):
    hi, mid, lo = _split3(x)
    g = g_ref[...]
    d = functools.partial(jnp.dot, preferred_element_type=F32)
    return d(hi, g) + (d(mid, g) + d(lo, g))


def _sigmoid(x):
    return 1.0 / (1.0 + jnp.exp(-x))


def _softplus(x):
    return jnp.maximum(x, 0.0) + jnp.log1p(jnp.exp(-jnp.abs(x)))


def _silu(x):
    return x * _sigmoid(x)


def _pack_bf16_pairs(x):
    w = x.shape[1] // 2
    bits = lax.bitcast_convert_type(x.astype(BF16).astype(F32), I32)
    return lax.shift_right_logical(bits[:, :w], 16) | (bits[:, w:] & -65536)


def _unpack_bf16_pairs(p):
    lo = lax.bitcast_convert_type(lax.shift_left(p, 16), F32)
    hi = lax.bitcast_convert_type(p & -65536, F32)
    return jnp.concatenate([lo, hi], axis=1).astype(BF16)


def _group_ones():
    h = jnp.arange(WIDTH, dtype=I32) // HEAD_DIM
    return (h[:, None] == h[None, :]).astype(BF16)


def _ada_kernel(c_ref, w_ref, b_ref, o_ref):
    o_ref[...] = _dot(_silu(c_ref[...]), w_ref[...]) + b_ref[...]


def _ada(c, w_ada, b_ada):
    nb = c.shape[0]
    n_out = w_ada.shape[1]
    blk = D_MODEL
    return pl.pallas_call(
        _ada_kernel,
        grid=(n_out // blk,),
        in_specs=[pl.BlockSpec((nb, D_MODEL), lambda j: (0, 0)),
                  pl.BlockSpec((D_MODEL, blk), lambda j: (0, j)),
                  pl.BlockSpec((1, blk), lambda j: (0, j))],
        out_specs=pl.BlockSpec((nb, blk), lambda j: (0, j)),
        out_shape=jax.ShapeDtypeStruct((nb, n_out), F32),
        compiler_params=_cparams(("parallel",)),
        name="ada_mod",
    )(c, w_ada, b_ada.reshape(1, n_out))


def _inproj_kernel(x_ref, mod_ref, g1_ref, wr_ref, wf_ref, wfl_ref, wg_ref, qn_ref, kn_ref, fb_ref, gm_ref, f0_ref,
                   pr_ref, q_ref, k_ref, v_ref, sg_ref, lf_ref, cf_ref, gate_ref, carry):
    bb, tt, d = x_ref.shape
    m = bb * tt
    x = x_ref[...]
    ms = jnp.mean(x * x, axis=-1, keepdims=True)
    h = x * lax.rsqrt(ms + RMS_EPS) * g1_ref[...]
    h = h * (1.0 + mod_ref[:, 1:2, :]) + mod_ref[:, 0:1, :]
    hb = h.reshape(m, d).astype(BF16)

    pr_ref[...] = jnp.dot(hb, wr_ref[...], preferred_element_type=F32).reshape(bb, tt, RWKV_COLS)

    f = jnp.dot(hb, wf_ref[...], preferred_element_type=F32)
    q = f[:, 0:WIDTH]
    k = f[:, WIDTH:2 * WIDTH]
    v = f[:, 2 * WIDTH:3 * WIDTH]
    og = f[:, 3 * WIDTH:4 * WIDTH]
    inv_hd = 1.0 / HEAD_DIM
    q = q * lax.rsqrt(_gsum(q * q, gm_ref) * inv_hd + RMS_EPS) * qn_ref[...]
    k = k * lax.rsqrt(_gsum(k * k, gm_ref) * inv_hd + RMS_EPS) * kn_ref[...]
    q_ref[...] = (q * (HEAD_DIM ** -0.5)).astype(BF16).reshape(bb, tt, WIDTH)
    k_ref[...] = k.reshape(bb, tt, WIDTH)
    v_ref[...] = v.reshape(bb, tt, WIDTH)
    sg_ref[...] = _sigmoid(og).reshape(bb, tt, WIDTH)

    fl = jnp.dot(hb, wfl_ref[...], preferred_element_type=F32)[:, 0:N_HEADS] + fb_ref[...]
    lf = -_softplus(-fl)
    lf_ref[...] = lf.reshape(bb, tt, N_HEADS)

    @pl.when(pl.program_id(1) == 0)
    def _():
        carry[...] = f0_ref[...]

    r = lax.broadcasted_iota(I32, (m, m), 0)
    c = lax.broadcasted_iota(I32, (m, m), 1)
    tri = jnp.logical_and(r // tt == c // tt, r >= c).astype(F32)
    cf = _dot_exact_rhs(tri, lf).reshape(bb, tt, N_HEADS) + carry[...]
    cf_ref[...] = cf
    carry[...] = cf[:, tt - 1:tt, :]

    gate_ref[...] =_sigmoid(jnp.dot(hb, wg_ref[...], preferred_element_type=F32)).reshape(bb, tt, GATE_COLS)


def _const_spec(shape):
    nd = len(shape)
    return pl.BlockSpec(shape, lambda *_: (0,) * nd)


def _inproj(x, mod, g1, wr, wf, wfl, wg, qn, kn, fb, gmat, f0, bb, tt):
    b, t, d = x.shape
    grid = (b // bb, t // tt)

    def tok(cols):
        return pl.BlockSpec((bb, tt, cols), lambda i, j: (i, j, 0))

    out_cols = [(RWKV_COLS, F32), (WIDTH, BF16), (WIDTH, F32), (WIDTH, F32), (WIDTH, F32), (N_HEADS, F32),
                (N_HEADS, F32), (GATE_COLS, F32)]
    return pl.pallas_call(
        _inproj_kernel,
        grid=grid,
        in_specs=[tok(d),
                  pl.BlockSpec((bb, 6, d), lambda i, j: (i, 0, 0)),
                  _const_spec((1, d)),
                  _const_spec(wr.shape), _const_spec(wf.shape), _const_spec(wfl.shape), _const_spec(wg.shape),
                  _const_spec((1, WIDTH)), _const_spec((1, WIDTH)), _const_spec((1, N_HEADS)),
                  _const_spec((WIDTH, WIDTH)),
                  pl.BlockSpec((bb, 1, N_HEADS), lambda i, j: (i, 0, 0))],
        out_specs=[tok(c) for c, _ in out_cols],
        out_shape=[jax.ShapeDtypeStruct((b, t, c), dt) for c, dt in out_cols],
        scratch_shapes=[pltpu.VMEM((bb, 1, N_HEADS), F32)],
        compiler_params=_cparams(("parallel", "arbitrary")),
        name="norm1_inproj",
    )(x, mod, g1, wr, wf, wfl, wg, qn, kn, fb, gmat, f0)


def _past_cumsum_kernel(x_ref, o_ref):
    x = x_ref[0]
    rows = x.shape[0]
    li = lax.broadcasted_iota(I32, (LANES, LANES), 0)
    lj = lax.broadcasted_iota(I32, (LANES, LANES), 1)
    same_head = (li % N_HEADS) == (lj % N_HEADS)
    within = jnp.logical_and(same_head, li // N_HEADS <= lj // N_HEADS).astype(BF16)
    xh, xm, xl = _split3(x)
    d2 = functools.partial(jnp.dot, preferred_element_type=F32)
    in_row = d2(xh, within) + (d2(xm, within) + d2(xl, within))
    sh = same_head.astype(BF16)
    row_tot = d2(xh, sh) + (d2(xm, sh) + d2(xl, sh))
    ri = lax.broadcasted_iota(I32, (rows, rows), 0)
    ci = lax.broadcasted_iota(I32, (rows, rows), 1)
    o_ref[0] = in_row + _dot_exact_rhs((ri > ci).astype(F32), row_tot)


def _past_cumsum(past_logf):
    b, p, h = past_logf.shape
    rows = p * h // LANES
    flat = past_logf.reshape(b, rows, LANES)
    out = pl.pallas_call(
        _past_cumsum_kernel,
        grid=(b,),
        in_specs=[pl.BlockSpec((1, rows, LANES), lambda i: (i, 0, 0))],
        out_specs=pl.BlockSpec((1, rows, LANES), lambda i: (i, 0, 0)),
        out_shape=jax.ShapeDtypeStruct((b, rows, LANES), F32),
        compiler_params=_cparams(("parallel",)),
        name="cache_logf_cumsum",
    )(flat)
    return out.reshape(b, p, h)


def _fox_kernel(*refs, n_past_blocks, tq):
    if n_past_blocks:
        (q_ref, fq_ref, sg_ref, kp_ref, vp_ref, fkp_ref, kn_ref, vn_ref, fkn_ref,
         o_ref, m_scr, l_scr, acc_scr) = refs
    else:
        q_ref, fq_ref, sg_ref, kn_ref, vn_ref, fkn_ref, o_ref, m_scr, l_scr, acc_scr = refs
    qi = pl.program_id(1)
    ki = pl.program_id(2)
    nk = pl.num_programs(2)

    @pl.when(ki == 0)
    def _():
        m_scr[...] = jnp.full(m_scr.shape, NEG_BIG, F32)
        l_scr[...] = jnp.zeros(l_scr.shape, F32)
        acc_scr[...] = jnp.zeros(acc_scr.shape, F32)

    lane_a = lax.broadcasted_iota(I32, (tq, LANES), 1) < HEAD_DIM

    def step(k_ref, v_ref, fk_ref, diag):
        tk = k_ref.shape[1]
        if diag:
            rq = lax.broadcasted_iota(I32, (tq, tk), 0)
            ck = lax.broadcasted_iota(I32, (tq, tk), 1)
            visible = ck <= rq
        fq_all = fq_ref[0]
        pairs = range(N_HEADS // 2)
        cols = [slice(j * LANES, (j + 1) * LANES) for j in pairs]
        scores = []
        for j in pairs:
            qj = q_ref[0, :, cols[j]]
            kb = k_ref[0, :, cols[j]].astype(BF16)
            for hh in range(2):
                h = 2 * j + hh
                qm = jnp.where(lane_a if hh == 0 else jnp.logical_not(lane_a), qj, jnp.zeros_like(qj))
                s = lax.dot_general(qm, kb, NT, preferred_element_type=F32)
                s = s + fq_all[:, h:h + 1] - fk_ref[0, h:h + 1, :]
                if diag:
                    s = jnp.where(visible, s, NEG_BIG)
                scores.append(s)
        alphas, probs = [], []
        for h in range(N_HEADS):
            m_old = m_scr[h]
            m_new = jnp.maximum(m_old, jnp.max(scores[h], axis=-1, keepdims=True))
            alpha = jnp.exp(m_old - m_new)
            p = jnp.exp(scores[h] - m_new)
            l_scr[h] = alpha * l_scr[h] + jnp.sum(p, axis=-1, keepdims=True)
            m_scr[h] = m_new
            alphas.append(alpha)
            probs.append(p.astype(BF16))
        for j in pairs:
            vb = v_ref[0, :, cols[j]].astype(BF16)
            pv0 = jnp.dot(probs[2 * j], vb, preferred_element_type=F32)
            pv1 = jnp.dot(probs[2 * j + 1], vb, preferred_element_type=F32)
            acc_scr[:, cols[j]] = (acc_scr[:, cols[j]] * jnp.where(lane_a, alphas[2 * j], alphas[2 * j + 1])
                                   + jnp.where(lane_a, pv0, pv1))

    if n_past_blocks:
        @pl.when(ki < n_past_blocks)
        def _():
            step(kp_ref, vp_ref, fkp_ref, False)

    kn = ki - n_past_blocks

    @pl.when(jnp.logical_and(kn >= 0, kn < qi))
    def _():
        step(kn_ref, vn_ref, fkn_ref, False)

    @pl.when(kn == qi)
    def _():
        step(kn_ref, vn_ref, fkn_ref, True)

    @pl.when(ki == nk - 1)
    def _():
        for j in range(N_HEADS // 2):
            cols = slice(j * LANES, (j + 1) * LANES)
            l = jnp.where(lane_a, l_scr[2 * j], l_scr[2 * j + 1])
            o_ref[0, :, cols] = acc_scr[:, cols] / l * sg_ref[0, :, cols]


def _fox_attention(q, fq, sg, k_new, v_new, fk_new_t, past=None, tq=512, tk_past=512):
    b, t, _ = q.shape
    nq = t // tq
    n_past_blocks = 0 if past is None else past[0].shape[1] // tk_past
    nk = n_past_blocks + nq

    def new_idx(i, qi, ki):
        return jnp.clip(ki - n_past_blocks, 0, qi)

    in_specs = [pl.BlockSpec((1, tq, WIDTH), lambda i, qi, ki: (i, qi, 0)),
                pl.BlockSpec((1, tq, N_HEADS), lambda i, qi, ki: (i, qi, 0)),
                pl.BlockSpec((1, tq, WIDTH), lambda i, qi, ki: (i, qi, 0))]
    args = [q, fq, sg]
    if n_past_blocks:
        def past_idx(i, qi, ki):
            return jnp.minimum(ki, n_past_blocks - 1)
        in_specs += [pl.BlockSpec((1, tk_past, WIDTH), lambda i, qi, ki: (i, past_idx(i, qi, ki), 0)),
                     pl.BlockSpec((1, tk_past, WIDTH), lambda i, qi, ki: (i, past_idx(i, qi, ki), 0)),
                     pl.BlockSpec((1, N_HEADS, tk_past), lambda i, qi, ki: (i, 0, past_idx(i, qi, ki)))]
        args += list(past)
    in_specs += [pl.BlockSpec((1, tq, WIDTH), lambda i, qi, ki: (i, new_idx(i, qi, ki), 0)),
                 pl.BlockSpec((1, tq, WIDTH), lambda i, qi, ki: (i, new_idx(i, qi, ki), 0)),
                 pl.BlockSpec((1, N_HEADS, tq), lambda i, qi, ki: (i, 0, new_idx(i, qi, ki)))]
    args += [k_new, v_new, fk_new_t]
    return pl.pallas_call(
        functools.partial(_fox_kernel, n_past_blocks=n_past_blocks, tq=tq),
        grid=(b, nq, nk),
        in_specs=in_specs,
        out_specs=pl.BlockSpec((1, tq, WIDTH), lambda i, qi, ki: (i, qi, 0)),
        out_shape=jax.ShapeDtypeStruct((b, t, WIDTH), F32),
        scratch_shapes=[pltpu.VMEM((N_HEADS, tq, 1), F32), pltpu.VMEM((N_HEADS, tq, 1), F32),
                        pltpu.VMEM((tq, WIDTH), F32)],
        compiler_params=_cparams(("parallel", "parallel", "arbitrary")),
        name="fox_attention",
    )(*args)


def _rwkv_kernel(p_ref, sh0_ref, s0_ref, mu_ref, w0_ref, wb_ref, a0_ref, ab_ref, gb_ref, kk_ref, ka_ref, rk_ref,
                 lnw_ref, lnb_ref, gm_ref, y_ref, st_ref, sht_ref, z_scr, prev_scr, *, c):
    t = pl.program_id(1)
    nt = pl.num_programs(1)
    n_rows = p_ref.shape[1]
    n_chunks = n_rows // c

    def head_block(h):
        lo = (h % HEADS_PER_GROUP) * HEAD_DIM
        return h // HEADS_PER_GROUP, slice(lo, lo + HEAD_DIM)

    @pl.when(t == 0)
    def _():
        z_scr[...] = jnp.zeros(z_scr.shape, F32)
        for h in range(N_HEADS):
            i, blk = head_block(h)
            z_scr[i, blk, blk] = s0_ref[0, h]
        prev_scr[...] = sh0_ref[0]

    p = p_ref[0]
    row = lax.broadcasted_iota(I32, p.shape, 0)
    prev = jnp.where(row == 0, prev_scr[...], pltpu.roll(p, 1, 0))
    last = p[n_rows - 1:n_rows, :]
    prev_scr[...] = last
    sht_ref[0] = last

    pm = p + (prev - p) * mu_ref[...]
    r = pm[:, 0:WIDTH]
    k = pm[:, WIDTH:2 * WIDTH]
    v = pm[:, 2 * WIDTH:3 * WIDTH]
    o1 = 3 * WIDTH
    wd = pm[:, o1:o1 + DECAY_LORA]
    ad = pm[:, o1 + DECAY_LORA:o1 + DECAY_LORA + ICLR_LORA]
    gd = pm[:, o1 + DECAY_LORA + ICLR_LORA:RWKV_COLS]

    w = -_softplus(-(w0_ref[...] + _dot(jnp.tanh(wd), wb_ref[...]))) - 0.5
    lw = -jnp.exp(w)
    a = _sigmoid(a0_ref[...] + _dot(ad, ab_ref[...]))
    g = _dot(_sigmoid(gd), gb_ref[...])
    kk = k * kk_ref[...]
    kk = kk / jnp.maximum(jnp.sqrt(_gsum(kk * kk, gm_ref)), L2_EPS)
    kf = k * (1.0 + (a - 1.0) * ka_ref[...])

    ri = lax.broadcasted_iota(I32, (n_rows, n_rows), 0)
    ci = lax.broadcasted_iota(I32, (n_rows, n_rows), 1)
    same_chunk = (ri // c) == (ci // c)
    cum = _dot_exact_rhs(jnp.logical_and(same_chunk, ri >= ci).astype(F32), lw)
    cum_last = _dot_exact_rhs(same_chunk.astype(F32), lw)
    r_t = r * jnp.exp(cum)
    a_t = -kk * jnp.exp(cum - lw)
    inv = jnp.exp(-cum)
    b_t = kk * a * inv
    k_t = kf * inv
    to_end = jnp.exp(cum_last - cum)
    b_e = kk * a * to_end
    k_e = kf * to_end
    g_end = jnp.exp(cum_last)

    hg = HEADS_PER_GROUP
    gw = hg * HEAD_DIM
    log_c = int(math.log2(c))
    t_idx = lax.broadcasted_iota(I32, (c, hg * c), 0)
    s_idx = lax.broadcasted_iota(I32, (c, hg * c), 1) & (c - 1)
    strict = s_idx < t_idx
    lower = s_idx <= t_idx
    eye = (s_idx == t_idx).astype(F32)
    rb = lax.broadcasted_iota(I32, (hg * c, gw), 0) >> log_c
    mask_kv = rb == (lax.broadcasted_iota(I32, (hg * c, gw), 1) >> int(math.log2(HEAD_DIM)))
    rs = lax.broadcasted_iota(I32, (hg * c, hg * c), 0) >> log_c
    mask_ss = rs == (lax.broadcasted_iota(I32, (hg * c, hg * c), 1) >> log_c)
    ng = N_HEADS // hg
    cat = functools.partial(jnp.concatenate, axis=0)
    units = [(slice(j * c, (j + 1) * c), slice(i * gw, (i + 1) * gw)) for j in range(n_chunks) for i in range(ng)]
    nu = len(units)

    def mm1(a, b_bd):
        return jnp.dot(a.astype(BF16), b_bd, preferred_element_type=F32)

    def dg(a, b, dims):
        return lax.dot_general(a.astype(BF16), b.astype(BF16), dims, preferred_element_type=F32)

    def bd1(x, mask):
        tiled = jnp.concatenate([x.astype(BF16)] * hg, axis=0)
        return jnp.where(mask, tiled, jnp.zeros_like(tiled))

    ar = [cat([a_t[rs_, s], r_t[rs_, s]]).astype(BF16) for rs_, s in units]
    ab = [dg(ar[n], bd1(b_t[units[n]], mask_kv), NT) for n in range(nu)]
    ak = [dg(ar[n], bd1(k_t[units[n]], mask_kv), NT) for n in range(nu)]
    l_ab = [jnp.where(strict, m[:c], 0.0) for m in ab]
    l_rb = [jnp.where(lower, m[c:], 0.0) for m in ab]
    l_ak = [jnp.where(strict, m[:c], 0.0) for m in ak]
    l_rk = [jnp.where(lower, m[c:], 0.0) for m in ak]
    tinv = [eye + m for m in l_ab]
    pw = [mm1(m, bd1(m, mask_ss)) for m in l_ab]
    for _ in range(1, log_c - 1):
        res = [mm1(cat([tinv[n], pw[n]]), bd1(pw[n], mask_ss)) for n in range(nu)]
        tinv = [tinv[n] + res[n][:c] for n in range(nu)]
        pw = [m[c:] for m in res]
    tinv = [tinv[n] + mm1(tinv[n], bd1(pw[n], mask_ss)) for n in range(nu)]
    av = [mm1(cat([l_ak[n], l_rk[n]]), bd1(v[units[n]], mask_kv)) for n in range(nu)]
    ue = [cat([b_e[units[n]], k_e[units[n]]]).astype(BF16) for n in range(nu)]

    def wide(fn, x):
        return [fn(x[:, :gw]), fn(x[:, gw:])]

    def bd1w(x):
        return jnp.concatenate(wide(lambda h_: bd1(h_, mask_kv), x), axis=1)

    rhs = [jnp.concatenate([a_t[units[n]], av[n][:c]], axis=1) for n in range(nu)]
    x0 = [mm1(tinv[n], bd1w(rhs[n])) for n in range(nu)]
    resid = [rhs[n] - (x0[n] - mm1(l_ab[n], bd1w(x0[n]))) for n in range(nu)]
    sol = [x0[n] + mm1(tinv[n], bd1w(resid[n])) for n in range(nu)]
    lift = [mm1(l_rb[n], bd1w(sol[n])) for n in range(nu)]
    lhs_s = [cat([sol[n][:, :gw], r_t[units[n]] + lift[n][:, :gw]]).astype(BF16) for n in range(nu)]
    u_loc = [sol[n][:, gw:] for n in range(nu)]
    o_loc = [av[n][c:] + lift[n][:, gw:] for n in range(nu)]

    zr = lax.broadcasted_iota(I32, (gw, gw), 0) >> int(math.log2(HEAD_DIM))
    zmask = zr == (lax.broadcasted_iota(I32, (gw, gw), 1) >> int(math.log2(HEAD_DIM)))
    z = [z_scr[i] for i in range(ng)]
    o_rows = []
    for j in range(n_chunks):
        o_grp = []
        for i in range(ng):
            n = j * ng + i
            rs_, s = units[n]
            sz = dg(lhs_s[n], z[i], NT)
            u = sz[:c] + u_loc[n]
            o_grp.append(sz[c:] + o_loc[n])
            upd = dg(cat([u, v[rs_, s]]), ue[n], TN)
            z[i] = z[i] * g_end[j * c:j * c + 1, s] + jnp.where(zmask, upd, 0.0)
        o_rows.append(jnp.concatenate(o_grp, axis=1))
    for i in range(ng):
        z_scr[i] = z[i]

    o = cat(o_rows)
    inv_hd = 1.0 / HEAD_DIM
    dlt = o - _gsum(o, gm_ref) * inv_hd
    var = _gsum(dlt * dlt, gm_ref) * inv_hd
    on = dlt * lax.rsqrt(var + RWKV_GN_EPS) * lnw_ref[...] + lnb_ref[...]
    bonus = _gsum(r * kf * rk_ref[...], gm_ref) * v
    y_ref[0] = (on + bonus) * g

    @pl.when(t == nt - 1)
    def _():
        for h in range(N_HEADS):
            i, blk = head_block(h)
            st_ref[0, h] = z_scr[i, blk, blk]


def _rwkv(p, shift0, s0, prm, gmat, chunk, chunks_per_step):
    b, t, _ = p.shape
    row = lambda n: _const_spec((1, n))
    rows = chunk * chunks_per_step
    return pl.pallas_call(
        functools.partial(_rwkv_kernel, c=chunk),
        grid=(b, t // rows),
        in_specs=[pl.BlockSpec((1, rows, RWKV_COLS), lambda i, j: (i, j, 0)),
                  pl.BlockSpec((1, 1, RWKV_COLS), lambda i, j: (i, 0, 0)),
                  pl.BlockSpec((1, N_HEADS, HEAD_DIM, HEAD_DIM), lambda i, j: (i, 0, 0, 0)),
                  row(RWKV_COLS), row(WIDTH), _const_spec((DECAY_LORA, WIDTH)), row(WIDTH),
                  _const_spec((ICLR_LORA, WIDTH)), _const_spec((GATE_LORA, WIDTH)),
                  row(WIDTH), row(WIDTH), row(WIDTH), row(WIDTH), row(WIDTH), _const_spec((WIDTH, WIDTH))],
        out_specs=[pl.BlockSpec((1, rows, WIDTH), lambda i, j: (i, j, 0)),
                   pl.BlockSpec((1, N_HEADS, HEAD_DIM, HEAD_DIM), lambda i, j: (i, 0, 0, 0)),
                   pl.BlockSpec((1, 1, RWKV_COLS), lambda i, j: (i, 0, 0))],
        out_shape=[jax.ShapeDtypeStruct((b, t, WIDTH), F32),
                   jax.ShapeDtypeStruct((b, N_HEADS, HEAD_DIM, HEAD_DIM), F32),
                   jax.ShapeDtypeStruct((b, 1, RWKV_COLS), F32)],
        scratch_shapes=[pltpu.VMEM((N_HEADS // HEADS_PER_GROUP, HEADS_PER_GROUP * HEAD_DIM,
                                    HEADS_PER_GROUP * HEAD_DIM), F32),
                        pltpu.VMEM((1, RWKV_COLS), F32)],
        compiler_params=_cparams(("parallel", "arbitrary")),
        name="rwkv7_mix",
    )(p, shift0, s0, prm["mu"], prm["w0"], prm["wb"], prm["a0"], prm["ab"], prm["gb"], prm["kk"], prm["ka"],
      prm["rk"], prm["lnw"], prm["lnb"], gmat)


def _merge_kernel(x_ref, ya_ref, yb_ref, gate_ref, mod_ref, g2_ref, woa_ref, wob_ref, wo_ref, wrh_ref, wrl_ref,
                  *rest):
    x1_ref, h2_ref, lg_ref = rest[-3:]
    bb, tt, d = x_ref.shape
    m = bb * tt
    gate = gate_ref[...].reshape(m, GATE_COLS)
    merged = (gate[:, 0:d] * _dot(ya_ref[...].reshape(m, WIDTH), woa_ref[...])
              + gate[:, d:2 * d] * _dot(yb_ref[...].reshape(m, WIDTH), wob_ref[...]))
    x1 = x_ref[...] + mod_ref[:, 2:3, :] * _dot(merged, wo_ref[...]).reshape(bb, tt, d)
    x1_ref[...] = x1
    ms = jnp.mean(x1 * x1, axis=-1, keepdims=True)
    h2 = x1 * lax.rsqrt(ms + RMS_EPS) * g2_ref[...]
    h2 = (h2 * (1.0 + mod_ref[:, 4:5, :]) + mod_ref[:, 3:4, :]).reshape(m, d)
    h2_ref[...] = _pack_bf16_pairs(h2)
    lg_ref[...] = _mm3((wrh_ref[...], wrl_ref[...]), _split2(h2), NT)


def _merge(x, ya, yb, gate, mod, w, n_total, row_offset, shared=None):
    b, t, d = x.shape
    bb, tt = _token_blocks(b, t, MOE_TOKEN_ROWS)
    nt = t // tt
    m = bb * tt
    off = row_offset // m

    def tok(cols):
        return pl.BlockSpec((bb, tt, cols), lambda i, j: (i, j, 0))

    in_specs = [tok(d), tok(WIDTH), tok(WIDTH), tok(GATE_COLS),
                pl.BlockSpec((bb, 6, d), lambda i, j: (i, 0, 0)),
                _const_spec((1, d)), _const_spec((WIDTH, d)), _const_spec((WIDTH, d)), _const_spec((d, d)),
                _const_spec((N_EXPERTS, d)), _const_spec((N_EXPERTS, d))]
    args = [x, ya, yb, gate, mod.reshape(b, 6, d), w["g2"], w["w_oa"], w["w_ob"], w["w_o"], w["wr_hi"], w["wr_lo"]]
    aliases = {}
    if shared is not None:
        aliases = {len(args): 1, len(args) + 1: 2}
        in_specs += [pl.BlockSpec(memory_space=pl.ANY), pl.BlockSpec(memory_space=pl.ANY)]
        args += list(shared)
    return pl.pallas_call(
        _merge_kernel,
        grid=(b // bb, nt),
        in_specs=in_specs,
        out_specs=[tok(d), pl.BlockSpec((m, d // 2), lambda i, j: (off + i * nt + j, 0)),
                   pl.BlockSpec((N_EXPERTS, m), lambda i, j: (0, off + i * nt + j))],
        out_shape=[jax.ShapeDtypeStruct((b, t, d), F32), jax.ShapeDtypeStruct((n_total, d // 2), I32),
                   jax.ShapeDtypeStruct((N_EXPERTS, n_total), F32)],
        input_output_aliases=aliases,
        compiler_params=_cparams(("parallel", "parallel")),
        name="merge_norm2_router",
    )(*args)


def _route_kernel(lg_ref, bias_ref, idx_ref, wt_ref, rank_ref, cnt_ref, carry):
    @pl.when(pl.program_id(0) == 0)
    def _():
        carry[...] = jnp.zeros(carry.shape, F32)

    tm = lg_ref.shape[1]
    scores = _sigmoid(lg_ref[...])
    sel = scores + bias_ref[...]
    row = lax.broadcasted_iota(I32, (N_EXPERTS, tm), 0)
    neg_inf = -jnp.inf

    def first_argmax(vals, rows):
        mx = jnp.max(vals, axis=0, keepdims=True)
        return mx, jnp.min(jnp.where(vals == mx, rows, N_EXPERTS), axis=0, keepdims=True)

    gslices = [slice(g * EXPERTS_PER_GROUP, (g + 1) * EXPERTS_PER_GROUP) for g in range(N_GROUPS)]
    gs = []
    row_g = lax.broadcasted_iota(I32, (EXPERTS_PER_GROUP, tm), 0)
    for sl in gslices:
        m1, i1 = first_argmax(sel[sl], row_g)
        m2 = jnp.max(jnp.where(row_g == i1, neg_inf, sel[sl]), axis=0, keepdims=True)
        gs.append(m1 + m2)
    kept = []
    for g in range(N_GROUPS):
        beaten = jnp.zeros((1, tm), I32)
        for o in range(N_GROUPS):
            if o != g:
                wins = (gs[o] >= gs[g]) if o < g else (gs[o] > gs[g])
                beaten = beaten + wins.astype(I32)
        kept.append(jnp.where(beaten < TOPK_GROUPS, sel[gslices[g]], neg_inf))
    cur = jnp.concatenate(kept, axis=0)

    idxs, ws = [], []
    picked = jnp.zeros((N_EXPERTS, tm), F32)
    for _ in range(TOP_K):
        _, ik = first_argmax(cur, row)
        hit = row == ik
        idxs.append(ik)
        ws.append(jnp.sum(jnp.where(hit, scores, 0.0), axis=0, keepdims=True))
        cur = jnp.where(hit, neg_inf, cur)
        picked = jnp.where(hit, 1.0, picked)
    wsum = ws[0]
    for k in range(1, TOP_K):
        wsum = wsum + ws[k]

    r = lax.broadcasted_iota(I32, (tm, tm), 0)
    c = lax.broadcasted_iota(I32, (tm, tm), 1)
    before = jnp.dot(picked.astype(BF16), (r < c).astype(BF16), preferred_element_type=F32) + carry[...]
    carry[...] = carry[...] + jnp.sum(picked, axis=1, keepdims=True)
    cnt_ref[...] = carry[...]

    kk = lax.broadcasted_iota(I32, (TOP_K, tm), 0)
    idx_o = jnp.zeros((TOP_K, tm), I32)
    wt_o = jnp.zeros((TOP_K, tm), F32)
    rank_o = jnp.zeros((TOP_K, tm), F32)
    for k in range(TOP_K):
        rk = jnp.sum(jnp.where(row == idxs[k], before, 0.0), axis=0, keepdims=True)
        idx_o = jnp.where(kk == k, idxs[k], idx_o)
        wt_o = jnp.where(kk == k, ws[k] / wsum * ROUTED_SCALE, wt_o)
        rank_o = jnp.where(kk == k, rk, rank_o)
    idx_ref[...] = idx_o
    wt_ref[...] = wt_o
    rank_ref[...] = rank_o.astype(I32)


def _route(logits_t, bias_col, tm):
    n = logits_t.shape[1]
    tokk = pl.BlockSpec((TOP_K, tm), lambda i: (0, i))
    return pl.pallas_call(
        _route_kernel,
        grid=(n // tm,),
        in_specs=[pl.BlockSpec((N_EXPERTS, tm), lambda i: (0, i)), _const_spec((N_EXPERTS, 1))],
        out_specs=[tokk, tokk, tokk, _const_spec((N_EXPERTS, 1))],
        out_shape=[jax.ShapeDtypeStruct((TOP_K, n), I32), jax.ShapeDtypeStruct((TOP_K, n), F32),
                   jax.ShapeDtypeStruct((TOP_K, n), I32), jax.ShapeDtypeStruct((N_EXPERTS, 1), F32)],
        scratch_shapes=[pltpu.VMEM((N_EXPERTS, 1), F32)],
        compiler_params=_cparams(("arbitrary",)),
        name="route_topk",
    )(logits_t, bias_col)


def _plan_kernel(cnt_ref, start_ref, be_ref, valid_ref, nu_ref, *, blk):
    cnt = cnt_ref[...]
    padded = jnp.ceil(cnt * (1.0 / blk)) * blk
    e_r = lax.broadcasted_iota(I32, (N_EXPERTS, N_EXPERTS), 0)
    e_c = lax.broadcasted_iota(I32, (N_EXPERTS, N_EXPERTS), 1)
    incl = (e_r <= e_c).astype(BF16)
    ph, pm, plo = _split3(jnp.broadcast_to(padded, (8, N_EXPERTS)))
    d2 = functools.partial(jnp.dot, preferred_element_type=F32)
    pad_end = (d2(ph, incl) + (d2(pm, incl) + d2(plo, incl)))[0:1, :]
    pad_start = pad_end - padded
    start_ref[...] = pad_start.astype(I32)
    total = jnp.max(pad_end, axis=-1, keepdims=True)
    nu_ref[...] = jnp.broadcast_to(total * (1.0 / blk), (1, N_EXPERTS)).astype(I32)
    nb = be_ref.shape[0]
    first = (lax.broadcasted_iota(I32, (nb, N_EXPERTS), 0) * blk).astype(F32)
    lane = lax.broadcasted_iota(I32, (nb, N_EXPERTS), 1)
    inside = jnp.logical_and(pad_start <= first, first < pad_end)
    be_ref[...] = jnp.sum(jnp.where(inside, lane, 0), axis=-1, keepdims=True)
    rows = jnp.minimum(pad_start + cnt - first, float(blk))
    valid_ref[...] = jnp.sum(jnp.where(inside, rows, 0.0), axis=-1, keepdims=True).astype(I32)


def _plan(counts, n_blocks, blk):
    return pl.pallas_call(
        functools.partial(_plan_kernel, blk=blk),
        out_shape=[jax.ShapeDtypeStruct((1, N_EXPERTS), I32), jax.ShapeDtypeStruct((n_blocks, 1), I32),
                   jax.ShapeDtypeStruct((n_blocks, 1), I32), jax.ShapeDtypeStruct((1, N_EXPERTS), I32)],
        compiler_params=pltpu.CompilerParams(vmem_limit_bytes=VMEM_LIMIT),
        name="dispatch_plan",
    )(counts)


def _dest_kernel(idx_ref, rank_ref, start_ref, dest_ref):
    tm = idx_ref.shape[1]
    row = lax.broadcasted_iota(I32, (N_EXPERTS, tm), 0)
    kk = lax.broadcasted_iota(I32, (TOP_K, tm), 0)
    idx = idx_ref[...]
    base = jnp.zeros((TOP_K, tm), I32)
    for k in range(TOP_K):
        bk = jnp.sum(jnp.where(row == idx[k:k + 1, :], start_ref[...], 0), axis=0, keepdims=True)
        base = jnp.where(kk == k, bk, base)
    dest_ref[...] = base + rank_ref[...]


def _dest(idx, rank, pad_start_col, tm):
    n = idx.shape[1]
    tokk = pl.BlockSpec((TOP_K, tm), lambda i: (0, i))
    return pl.pallas_call(
        _dest_kernel,
        grid=(n // tm,),
        in_specs=[tokk, tokk, _const_spec((N_EXPERTS, 1))],
        out_specs=tokk,
        out_shape=jax.ShapeDtypeStruct((TOP_K, n), I32),
        compiler_params=_cparams(("parallel",)),
        name="dispatch_dest",
    )(idx, rank, pad_start_col)


def _dispatch(h2p, dest_t, n_slots):
    n, wp = h2p.shape
    half = wp // SC_ROW_SPLIT
    window = SC_SCATTER_WINDOW
    mesh = plsc.VectorSubcoreMesh(core_axis_name="core", subcore_axis_name="subcore")
    out = jax.ShapeDtypeStruct((n_slots, half), h2p.dtype)

    @functools.partial(pl.kernel, out_type=[out] * SC_ROW_SPLIT, mesh=mesh, scratch_types=[])
    def scatter(rows_hbm, idx_hbm, *outs):
        for c, out_hbm in enumerate(outs):
            def body(rows_vmem, idx_vmem, out_hbm=out_hbm):
                for k in range(TOP_K):
                    pltpu.sync_copy(rows_vmem, out_hbm.at[idx_vmem.at[k]])

            pltpu.emit_pipeline(
                body, grid=(n // window,),
                in_specs=[pl.BlockSpec((window, half), index_map=lambda i, c=c: (i, c)),
                          pl.BlockSpec((TOP_K, window), index_map=lambda i: (0, i))],
                out_specs=[], core_axis_name=("core", "subcore"), dimension_semantics=(pltpu.PARALLEL,),
            )(rows_hbm, idx_hbm)

    return scatter(h2p, dest_t)


def _expert_kernel(be_ref, valid_ref, nu_ref, xa_ref, xb_ref, wg_ref, wu_ref, wd_ref, *rest):
    y_refs, (wg_b, wu_b, wd_b) = rest[:SC_ROW_SPLIT], rest[SC_ROW_SPLIT:]
    i = pl.program_id(0)
    nv = valid_ref[i]
    new_expert = jnp.logical_or(i == 0, be_ref[i] != be_ref[jnp.maximum(i - 1, 0)])

    @pl.when(jnp.logical_and(nv > 0, new_expert))
    def _():
        wg_b[...] = wg_ref[0].astype(BF16)
        wu_b[...] = wu_ref[0].astype(BF16)
        wd_b[...] = wd_ref[0].astype(BF16)

    @pl.when(nv > 0)
    def _():
        blk = xa_ref.shape[0]
        rows = lax.broadcasted_iota(I32, (blk, 1), 0)
        packed = jnp.concatenate([xa_ref[...], xb_ref[...]], axis=1)
        x = _unpack_bf16_pairs(jnp.where(rows < nv, packed, 0))
        hg = jnp.dot(x, wg_b[...], preferred_element_type=F32)
        hu = jnp.dot(x, wu_b[...], preferred_element_type=F32)
        y = _pack_bf16_pairs(jnp.dot((_silu(hg) * hu).astype(BF16), wd_b[...], preferred_element_type=F32))
        cw = y.shape[1] // SC_ROW_SPLIT
        for c, y_ref in enumerate(y_refs):
            y_ref[...] = y[:, c * cw:(c + 1) * cw]


def _experts(xs, block_e, valid, n_used, w_eg, w_eu, w_ed, blk):
    xa, xb = xs
    n_slots, packed = xa.shape
    d = w_eg.shape[1]
    n_blocks = n_slots // blk

    def row_blk(i, be, valid, nu):
        return (jnp.minimum(i, nu[0] - 1), 0)

    def w_blk(i, be, valid, nu):
        return (be[i], 0, 0)

    return pl.pallas_call(
        _expert_kernel,
        grid_spec=pltpu.PrefetchScalarGridSpec(
            num_scalar_prefetch=3,
            grid=(n_blocks,),
            in_specs=[pl.BlockSpec((blk, packed), row_blk), pl.BlockSpec((blk, packed), row_blk),
                      pl.BlockSpec((1, d, D_EXPERT), w_blk), pl.BlockSpec((1, d, D_EXPERT), w_blk),
                      pl.BlockSpec((1, D_EXPERT, d), w_blk)],
            out_specs=[pl.BlockSpec((blk, packed), row_blk)] * SC_ROW_SPLIT,
            scratch_shapes=[pltpu.VMEM((d, D_EXPERT), BF16), pltpu.VMEM((d, D_EXPERT), BF16),
                            pltpu.VMEM((D_EXPERT, d), BF16)]),
        out_shape=[jax.ShapeDtypeStruct((n_slots, packed), I32)] * SC_ROW_SPLIT,
        compiler_params=_cparams(("arbitrary",)),
        name="moe_experts",
    )(block_e, valid, n_used, xa, xb, w_eg, w_eu, w_ed)


def _combine_gather(ys, dest_t):
    k, n = dest_t.shape
    cw = ys[0].shape[1]
    window = SC_SCATTER_WINDOW
    mesh = plsc.VectorSubcoreMesh(core_axis_name="core", subcore_axis_name="subcore")

    @functools.partial(pl.kernel, out_type=jax.ShapeDtypeStruct((k * n, cw * len(ys)), ys[0].dtype), mesh=mesh,
                       scratch_types=[])
    def gather(*refs):
        y_refs, idx_hbm, out_hbm = refs[:len(ys)], refs[len(ys)], refs[len(ys) + 1]
        for c, y_hbm in enumerate(y_refs):
            def body(idx_vmem, out_vmem, y_hbm=y_hbm):
                pltpu.sync_copy(y_hbm.at[idx_vmem.at[0]], out_vmem)

            pltpu.emit_pipeline(
                body, grid=(k * n // window,),
                in_specs=[pl.BlockSpec((1, window), index_map=lambda i: (0, i))],
                out_specs=[pl.BlockSpec((window, cw), index_map=lambda i, c=c: (i, c))],
                core_axis_name=("core", "subcore"), dimension_semantics=(pltpu.PARALLEL,),
            )(idx_hbm, out_hbm)

    return gather(*ys, dest_t.reshape(1, k * n)).reshape(k, n, cw * len(ys))


def _final_kernel(x1_ref, h2_ref, wt_ref, mod_ref, wsg_ref, wsu_ref, wsd_ref, yg_ref, *rest):
    o_ref = rest[-1]
    bb, tt, d = x1_ref.shape
    hb = _unpack_bf16_pairs(h2_ref[...])
    hg = jnp.dot(hb, wsg_ref[...], preferred_element_type=F32)
    hu = jnp.dot(hb, wsu_ref[...], preferred_element_type=F32)
    ffn = _dot(_silu(hg) * hu, wsd_ref[...])
    wt = wt_ref[...]
    for k in range(TOP_K):
        ffn = ffn + wt[:, k:k + 1] * _unpack_bf16_pairs(yg_ref[k]).astype(F32)
    o_ref[...] = x1_ref[...] + mod_ref[:, 5:6, :] * ffn.reshape(bb, tt, d)


def _final(x1, h2_all, wts_all, y_part, mod, w, row_offset, batch_lo, n_batch, prev=None):
    b, t, d = x1.shape
    bb, tt = _token_blocks(n_batch, t, MOE_TOKEN_ROWS)
    nt = t // tt
    m = bb * tt
    off = (row_offset + batch_lo * t) // m
    blo = batch_lo // bb

    def flat_idx(i, j):
        return off + i * nt + j

    in_specs = [pl.BlockSpec((bb, tt, d), lambda i, j: (blo + i, j, 0)),
                pl.BlockSpec((m, h2_all.shape[1]), lambda i, j: (flat_idx(i, j), 0)),
                pl.BlockSpec((m, TOP_K), lambda i, j: (flat_idx(i, j), 0)),
                pl.BlockSpec((bb, 6, d), lambda i, j: (blo + i, 0, 0)),
                _const_spec((d, D_EXPERT)), _const_spec((d, D_EXPERT)), _const_spec((D_EXPERT, d)),
                pl.BlockSpec((TOP_K, m, y_part.shape[2]), lambda i, j: (0, i * nt + j, 0))]
    args = [x1, h2_all, wts_all, mod.reshape(b, 6, d), w["w_sg"], w["w_su"], w["w_sd"], y_part]
    aliases = {}
    if prev is not None:
        aliases = {len(args): 0}
        in_specs.append(pl.BlockSpec(memory_space=pl.ANY))
        args.append(prev)
    return pl.pallas_call(
        _final_kernel,
        grid=(n_batch // bb, nt),
        in_specs=in_specs,
        out_specs=pl.BlockSpec((bb, tt, d), lambda i, j: (blo + i, j, 0)),
        out_shape=jax.ShapeDtypeStruct((b, t, d), F32),
        input_output_aliases=aliases,
        compiler_params=_cparams(("parallel", "parallel")),
        name="moe_combine_final",
    )(*args)


def _largest_tile(n, candidates):
    return next(c for c in candidates if n % c == 0)


def _moe_routed(h2_all, logits_all, w, blk=EXPERT_BLOCK_ROWS):
    n = h2_all.shape[0]
    n_blocks = (n * TOP_K + N_EXPERTS * (blk - 1)) // blk + 1
    n_blocks = (n_blocks + 7) // 8 * 8
    idx, wts_t, rank, counts = _route(logits_all, w["router_bias"], _largest_tile(n, (512, 256)))
    pad_start, block_e, valid, n_used = _plan(counts.reshape(1, N_EXPERTS), n_blocks, blk)
    block_e = block_e.reshape(n_blocks)
    valid = valid.reshape(n_blocks)
    n_used = n_used[0, 0:1]
    dest_t = _dest(idx, rank, pad_start.reshape(N_EXPERTS, 1), _largest_tile(n, (1024, 512, 256)))
    xs = _dispatch(h2_all, dest_t, n_blocks * blk)
    ys = _experts(xs, block_e, valid, n_used, w["w_eg"], w["w_eu"], w["w_ed"], blk)
    return ys, dest_t, jnp.transpose(wts_t)


def _prep(raw):
    p = {k: v[0] for k, v in raw.items()}
    w_in = p["w_in"]
    o_fox = RWKV_COLS
    o_fl = o_fox + FOX_MAIN_COLS
    o_gate = o_fl + N_HEADS
    row = lambda a: a.reshape(1, -1)
    return dict(
        w_ada=p["w_ada"], b_ada=p["b_ada"],
        g1=row(p["norm1_g"]), g2=row(p["norm2_g"]),
        wr=w_in[:, :o_fox].astype(BF16),
        wf=w_in[:, o_fox:o_fl].astype(BF16),
        wfl=jnp.pad(w_in[:, o_fl:o_gate], ((0, 0), (0, LANES - N_HEADS))).astype(BF16),
        wg=w_in[:, o_gate:].astype(BF16),
        qn=row(jnp.tile(p["fox_q_norm"], N_HEADS)), kn=row(jnp.tile(p["fox_k_norm"], N_HEADS)),
        fb=row(p["fox_f_bias"]),
        gmat=_group_ones(),
        rwkv=dict(mu=row(p["rwkv_mu"]), w0=row(p["rwkv_w0"]), wb=p["rwkv_w_lora_b"], a0=row(p["rwkv_a0"]),
                  ab=p["rwkv_a_lora_b"], gb=p["rwkv_g_lora_b"], kk=row(p["rwkv_k_k"]), ka=row(p["rwkv_k_a"]),
                  rk=row(p["rwkv_r_k"]), lnw=row(p["rwkv_ln_w"]), lnb=row(p["rwkv_ln_b"])),
        w_oa=p["w_out_rwkv"].astype(BF16), w_ob=p["w_out_fox"].astype(BF16), w_o=p["w_out"].astype(BF16),
        wr_hi=p["w_router"].T.astype(BF16),
        wr_lo=(p["w_router"] - p["w_router"].astype(BF16).astype(F32)).T.astype(BF16),
        router_bias=p["router_bias"].reshape(N_EXPERTS, 1),
        w_eg=p["w_exp_gate"], w_eu=p["w_exp_up"], w_ed=p["w_exp_down"],
        w_sg=p["w_sh_gate"].astype(BF16), w_su=p["w_sh_up"].astype(BF16), w_sd=p["w_sh_down"].astype(BF16),
    )


def _token_blocks(b, t, rows=256):
    if t >= rows:
        return 1, rows
    bb = max(1, min(b, 256 // t))
    while b % bb:
        bb -= 1
    return bb, t


def _mix_path(x, mod, shift0, wkv0, past_k, past_v, past_logf, w):
    b, t, d = x.shape
    bb, tt = _token_blocks(b, t)
    n_past = past_k.shape[1]
    if n_past:
        f_past = _past_cumsum(past_logf)
        init = f_past[:, n_past - 1:n_past, :]
        past = (past_k, past_v, jnp.swapaxes(f_past, 1, 2))
    else:
        init = jnp.zeros((b, 1, N_HEADS), F32)
        past = None
    pr, q, k, v, sg, logf, f_new, gate = _inproj(x, mod.reshape(b, 6, d), w["g1"], w["wr"], w["wf"], w["wfl"],
                                                 w["wg"], w["qn"], w["kn"], w["fb"], w["gmat"], init, bb, tt)
    y_fox = _fox_attention(q, f_new, sg, k, v, jnp.swapaxes(f_new, 1, 2), past=past, tq=min(t, 512),
                           tk_past=min(max(n_past, 1), 512))
    chunk = min(t, RWKV_CHUNK)
    y_rwkv, wkv_new, shift_new = _rwkv(pr, shift0.reshape(b, 1, RWKV_COLS), wkv0, w["rwkv"], w["gmat"],
                                       chunk, max(1, min(RWKV_CHUNKS_PER_STEP, t // chunk)))
    return y_rwkv, y_fox, gate, wkv_new, shift_new, k, v, logf


def _layer(paths, w):
    n_b = [p[0].shape[0] for p in paths]
    mod_all = _ada(jnp.concatenate([p[1] for p in paths], axis=0), w["w_ada"], w["b_ada"])
    mods, o = [], 0
    for nb in n_b:
        mods.append(mod_all[o:o + nb])
        o += nb
    n_total = sum(p[0].shape[0] * p[0].shape[1] for p in paths)
    mixed, x1s = [], []
    shared, row = None, 0
    for (x, _, shift0, wkv0, pk, pv, plf), mod in zip(paths, mods):
        ya, yb, gate, wkv_new, shift_new, k, v, logf = _mix_path(x, mod, shift0, wkv0, pk, pv, plf, w)
        x1, h2_all, lg_all = _merge(x, ya, yb, gate, mod, w, n_total, row, shared)
        shared = (h2_all, lg_all)
        row += x.shape[0] * x.shape[1]
        mixed.append((wkv_new, shift_new, k, v, logf))
        x1s.append(x1)
    ys, dest_t, wts = _moe_routed(h2_all, lg_all, w)
    outs, row = [], 0
    for x1, mod, st in zip(x1s, mods, mixed):
        b, t, _ = x1.shape
        n_parts = FINAL_PARTS if (b % FINAL_PARTS == 0 and t >= MOE_TOKEN_ROWS) else 1
        pb = b // n_parts
        y = None
        for part in range(n_parts):
            r0 = row + part * pb * t
            y_part = _combine_gather(ys, dest_t[:, r0:r0 + pb * t])
            y = _final(x1, h2_all, wts, y_part, mod, w, row, part * pb, pb, y)
        row += b * t
        outs.append((y,) + st)
    return outs


def kernel(x_prompt, x_sample, c_prompt, c_sample, state_rwkv_wkv, state_rwkv_shift, cache_fox_k, cache_fox_v,
           cache_fox_logf, w_ada, b_ada, norm1_g, norm2_g, w_in, rwkv_mu, rwkv_w0, rwkv_w_lora_b, rwkv_a0,
           rwkv_a_lora_b, rwkv_g_lora_b, rwkv_k_k, rwkv_k_a, rwkv_r_k, rwkv_ln_w, rwkv_ln_b, fox_q_norm,
           fox_k_norm, fox_f_bias, w_out_rwkv, w_out_fox, w_out, w_router, router_bias, w_exp_gate, w_exp_up,
           w_exp_down, w_sh_gate, w_sh_up, w_sh_down):
    raw = dict(w_ada=w_ada, b_ada=b_ada, norm1_g=norm1_g, norm2_g=norm2_g, w_in=w_in, rwkv_mu=rwkv_mu,
               rwkv_w0=rwkv_w0, rwkv_w_lora_b=rwkv_w_lora_b, rwkv_a0=rwkv_a0, rwkv_a_lora_b=rwkv_a_lora_b,
               rwkv_g_lora_b=rwkv_g_lora_b, rwkv_k_k=rwkv_k_k, rwkv_k_a=rwkv_k_a, rwkv_r_k=rwkv_r_k,
               rwkv_ln_w=rwkv_ln_w, rwkv_ln_b=rwkv_ln_b, fox_q_norm=fox_q_norm, fox_k_norm=fox_k_norm,
               fox_f_bias=fox_f_bias, w_out_rwkv=w_out_rwkv, w_out_fox=w_out_fox, w_out=w_out,
               w_router=w_router, router_bias=router_bias, w_exp_gate=w_exp_gate, w_exp_up=w_exp_up,
               w_exp_down=w_exp_down, w_sh_gate=w_sh_gate, w_sh_up=w_sh_up, w_sh_down=w_sh_down)
    assert w_in.shape[0] == 1, "single-layer stack"
    w = _prep(raw)
    bp, tp, _ = x_prompt.shape
    bs, ts, _ = x_sample.shape
    n_past = cache_fox_k.shape[2]
    prompt = (x_prompt, c_prompt, jnp.zeros((bp, RWKV_COLS), F32),
              jnp.zeros((bp, N_HEADS, HEAD_DIM, HEAD_DIM), F32),
              jnp.zeros((bp, 0, WIDTH), F32), jnp.zeros((bp, 0, WIDTH), F32), jnp.zeros((bp, 0, N_HEADS), F32))
    sample = (x_sample, c_sample, state_rwkv_shift[0], state_rwkv_wkv[0],
              cache_fox_k[0].reshape(bs, n_past, WIDTH), cache_fox_v[0].reshape(bs, n_past, WIDTH),
              cache_fox_logf[0])
    (yp, wkv_p, sh_p, k_p, v_p, lf_p), (ysm, wkv_s, sh_s, k_s, v_s, lf_s) = _layer([prompt, sample], w)

    def heads(a):
        return a.reshape((1,) + a.shape[:2] + (N_HEADS, HEAD_DIM))

    return (yp, ysm,
            wkv_p[None], sh_p.reshape(1, bp, RWKV_COLS), heads(k_p), heads(v_p), lf_p[None],
            wkv_s[None], sh_s.reshape(1, bs, RWKV_COLS), heads(k_s), heads(v_s), lf_s[None])
```

```python
import functools
import math

import jax
import jax.numpy as jnp
from jax import lax
from jax.experimental import pallas as pl
from jax.experimental.pallas import tpu as pltpu
from jax.experimental.pallas import tpu_sc as plsc

F32 = jnp.float32
BF16 = jnp.bfloat16
I32 = jnp.int32

D_MODEL = 1024
N_HEADS = 8
HEAD_DIM = 64
WIDTH = N_HEADS * HEAD_DIM
HEADS_PER_GROUP = 4
RWKV_CHUNK = 64
RWKV_CHUNKS_PER_STEP = 8
MOE_TOKEN_ROWS = 512
EXPERT_BLOCK_ROWS = 512
SC_SCATTER_WINDOW = 128
SC_ROW_SPLIT = 2
DECAY_LORA = 64
ICLR_LORA = 64
GATE_LORA = 128
RWKV_COLS = 3 * WIDTH + DECAY_LORA + ICLR_LORA + GATE_LORA
FOX_MAIN_COLS = 4 * WIDTH
GATE_COLS = 2 * D_MODEL
RWKV_GN_EPS = HEAD_DIM * 1e-5
L2_EPS = 1e-12
RMS_EPS = 1e-6
N_EXPERTS = 256
TOP_K = 8
N_GROUPS = 8
TOPK_GROUPS = 4
EXPERTS_PER_GROUP = N_EXPERTS // N_GROUPS
D_EXPERT = 256
ROUTED_SCALE = 2.5

LANES = 128
VMEM_LIMIT = 56 * 1024 * 1024
NEG_BIG = -1e30

NN = (((1,), (0,)), ((), ()))
NT = (((1,), (1,)), ((), ()))
TN = (((0,), (0,)), ((), ()))


def _cparams(sem):
    return pltpu.CompilerParams(dimension_semantics=sem, vmem_limit_bytes=VMEM_LIMIT)


def _dot(a, b, dims=NN):
    return lax.dot_general(a.astype(BF16), b.astype(BF16), dims, preferred_element_type=F32)


def _split2(a):
    hi = a.astype(BF16)
    lo = (a - hi.astype(F32)).astype(BF16)
    return hi, lo


def _split3(a):
    hi = a.astype(BF16)
    r1 = a - hi.astype(F32)
    mid = r1.astype(BF16)
    lo = (r1 - mid.astype(F32)).astype(BF16)
    return hi, mid, lo


def _mm3(a, b, dims):
    d = functools.partial(lax.dot_general, dimension_numbers=dims, preferred_element_type=F32)
    return d(a[0], b[0]) + (d(a[0], b[1]) + d(a[1], b[0]))


def _dot_exact_rhs(a_exact, b, dims=NN):
    ab = a_exact.astype(BF16)
    bh, bm, bl = _split3(b)
    d = functools.partial(lax.dot_general, dimension_numbers=dims, preferred_element_type=F32)
    return d(ab, bh) + (d(ab, bm) + d(ab, bl))


def _gsum(x, g_ref):
    hi, mid, lo = _split3(x)
    g = g_ref[...]
    d = functools.partial(jnp.dot, preferred_element_type=F32)
    return d(hi, g) + (d(mid, g) + d(lo, g))


def _sigmoid(x):
    return 1.0 / (1.0 + jnp.exp(-x))


def _softplus(x):
    return jnp.maximum(x, 0.0) + jnp.log1p(jnp.exp(-jnp.abs(x)))


def _silu(x):
    return x * _sigmoid(x)


def _pack_bf16_pairs(x):
    w = x.shape[1] // 2
    bits = lax.bitcast_convert_type(x.astype(BF16).astype(F32), I32)
    return lax.shift_right_logical(bits[:, :w], 16) | (bits[:, w:] & -65536)


def _unpack_bf16_pairs(p):
    lo = lax.bitcast_convert_type(lax.shift_left(p, 16), F32)
    hi = lax.bitcast_convert_type(p & -65536, F32)
    return jnp.concatenate([lo, hi], axis=1).astype(BF16)


def _group_ones():
    h = jnp.arange(WIDTH, dtype=I32) // HEAD_DIM
    return (h[:, None] == h[None, :]).astype(BF16)


def _ada_kernel(c_ref, w_ref, b_ref, o_ref):
    o_ref[...] = _dot(_silu(c_ref[...]), w_ref[...]) + b_ref[...]


def _ada(c, w_ada, b_ada):
    nb = c.shape[0]
    n_out = w_ada.shape[1]
    blk = D_MODEL
    return pl.pallas_call(
        _ada_kernel,
        grid=(n_out // blk,),
        in_specs=[pl.BlockSpec((nb, D_MODEL), lambda j: (0, 0)),
                  pl.BlockSpec((D_MODEL, blk), lambda j: (0, j)),
                  pl.BlockSpec((1, blk), lambda j: (0, j))],
        out_specs=pl.BlockSpec((nb, blk), lambda j: (0, j)),
        out_shape=jax.ShapeDtypeStruct((nb, n_out), F32),
        compiler_params=_cparams(("parallel",)),
        name="ada_mod",
    )(c, w_ada, b_ada.reshape(1, n_out))


def _inproj_kernel(x_ref, mod_ref, g1_ref, wr_ref, wf_ref, wfl_ref, wg_ref, qn_ref, kn_ref, fb_ref, gm_ref, f0_ref,
                   pr_ref, q_ref, k_ref, v_ref, sg_ref, lf_ref, cf_ref, gate_ref, carry):
    bb, tt, d = x_ref.shape
    m = bb * tt
    x = x_ref[...]
    ms = jnp.mean(x * x, axis=-1, keepdims=True)
    h = x * lax.rsqrt(ms + RMS_EPS) * g1_ref[...]
    h = h * (1.0 + mod_ref[:, 1:2, :]) + mod_ref[:, 0:1, :]
    hb = h.reshape(m, d).astype(BF16)

    pr_ref[...] = jnp.dot(hb, wr_ref[...], preferred_element_type=F32).reshape(bb, tt, RWKV_COLS)

    f = jnp.dot(hb, wf_ref[...], preferred_element_type=F32)
    q = f[:, 0:WIDTH]
    k = f[:, WIDTH:2 * WIDTH]
    v = f[:, 2 * WIDTH:3 * WIDTH]
    og = f[:, 3 * WIDTH:4 * WIDTH]
    inv_hd = 1.0 / HEAD_DIM
    q = q * lax.rsqrt(_gsum(q * q, gm_ref) * inv_hd + RMS_EPS) * qn_ref[...]
    k = k * lax.rsqrt(_gsum(k * k, gm_ref) * inv_hd + RMS_EPS) * kn_ref[...]
    q_ref[...] = (q * (HEAD_DIM ** -0.5)).astype(BF16).reshape(bb, tt, WIDTH)
    k_ref[...] = k.reshape(bb, tt, WIDTH)
    v_ref[...] = v.reshape(bb, tt, WIDTH)
    sg_ref[...] = _sigmoid(og).reshape(bb, tt, WIDTH)

    fl = jnp.dot(hb, wfl_ref[...], preferred_element_type=F32)[:, 0:N_HEADS] + fb_ref[...]
    lf = -_softplus(-fl)
    lf_ref[...] = lf.reshape(bb, tt, N_HEADS)

    @pl.when(pl.program_id(1) == 0)
    def _():
        carry[...] = f0_ref[...]

    r = lax.broadcasted_iota(I32, (m, m), 0)
    c = lax.broadcasted_iota(I32, (m, m), 1)
    tri = jnp.logical_and(r // tt == c // tt, r >= c).astype(F32)
    cf = _dot_exact_rhs(tri, lf).reshape(bb, tt, N_HEADS) + carry[...]
    cf_ref[...] = cf
    carry[...] = cf[:, tt - 1:tt, :]

    gate_ref[...] =_sigmoid(jnp.dot(hb, wg_ref[...], preferred_element_type=F32)).reshape(bb, tt, GATE_COLS)


def _const_spec(shape):
    nd = len(shape)
    return pl.BlockSpec(shape, lambda *_: (0,) * nd)


def _inproj(x, mod, g1, wr, wf, wfl, wg, qn, kn, fb, gmat, f0, bb, tt):
    b, t, d = x.shape
    grid = (b // bb, t // tt)

    def tok(cols):
        return pl.BlockSpec((bb, tt, cols), lambda i, j: (i, j, 0))

    out_cols = [(RWKV_COLS, F32), (WIDTH, BF16), (WIDTH, F32), (WIDTH, F32), (WIDTH, F32), (N_HEADS, F32),
                (N_HEADS, F32), (GATE_COLS, F32)]
    return pl.pallas_call(
        _inproj_kernel,
        grid=grid,
        in_specs=[tok(d),
                  pl.BlockSpec((bb, 6, d), lambda i, j: (i, 0, 0)),
                  _const_spec((1, d)),
                  _const_spec(wr.shape), _const_spec(wf.shape), _const_spec(wfl.shape), _const_spec(wg.shape),
                  _const_spec((1, WIDTH)), _const_spec((1, WIDTH)), _const_spec((1, N_HEADS)),
                  _const_spec((WIDTH, WIDTH)),
                  pl.BlockSpec((bb, 1, N_HEADS), lambda i, j: (i, 0, 0))],
        out_specs=[tok(c) for c, _ in out_cols],
        out_shape=[jax.ShapeDtypeStruct((b, t, c), dt) for c, dt in out_cols],
        scratch_shapes=[pltpu.VMEM((bb, 1, N_HEADS), F32)],
        compiler_params=_cparams(("parallel", "arbitrary")),
        name="norm1_inproj",
    )(x, mod, g1, wr, wf, wfl, wg, qn, kn, fb, gmat, f0)


def _past_cumsum_kernel(x_ref, o_ref):
    x = x_ref[0]
    rows = x.shape[0]
    li = lax.broadcasted_iota(I32, (LANES, LANES), 0)
    lj = lax.broadcasted_iota(I32, (LANES, LANES), 1)
    same_head = (li % N_HEADS) == (lj % N_HEADS)
    within = jnp.logical_and(same_head, li // N_HEADS <= lj // N_HEADS).astype(BF16)
    xh, xm, xl = _split3(x)
    d2 = functools.partial(jnp.dot, preferred_element_type=F32)
    in_row = d2(xh, within) + (d2(xm, within) + d2(xl, within))
    sh = same_head.astype(BF16)
    row_tot = d2(xh, sh) + (d2(xm, sh) + d2(xl, sh))
    ri = lax.broadcasted_iota(I32, (rows, rows), 0)
    ci = lax.broadcasted_iota(I32, (rows, rows), 1)
    o_ref[0] = in_row + _dot_exact_rhs((ri > ci).astype(F32), row_tot)


def _past_cumsum(past_logf):
    b, p, h = past_logf.shape
    rows = p * h // LANES
    flat = past_logf.reshape(b, rows, LANES)
    out = pl.pallas_call(
        _past_cumsum_kernel,
        grid=(b,),
        in_specs=[pl.BlockSpec((1, rows, LANES), lambda i: (i, 0, 0))],
        out_specs=pl.BlockSpec((1, rows, LANES), lambda i: (i, 0, 0)),
        out_shape=jax.ShapeDtypeStruct((b, rows, LANES), F32),
        compiler_params=_cparams(("parallel",)),
        name="cache_logf_cumsum",
    )(flat)
    return out.reshape(b, p, h)


def _fox_kernel(*refs, n_past_blocks, tq):
    if n_past_blocks:
        (q_ref, fq_ref, sg_ref, kp_ref, vp_ref, fkp_ref, kn_ref, vn_ref, fkn_ref,
         o_ref, m_scr, l_scr, acc_scr) = refs
    else:
        q_ref, fq_ref, sg_ref, kn_ref, vn_ref, fkn_ref, o_ref, m_scr, l_scr, acc_scr = refs
    qi = pl.program_id(1)
    ki = pl.program_id(2)
    nk = pl.num_programs(2)

    @pl.when(ki == 0)
    def _():
        m_scr[...] = jnp.full(m_scr.shape, NEG_BIG, F32)
        l_scr[...] = jnp.zeros(l_scr.shape, F32)
        acc_scr[...] = jnp.zeros(acc_scr.shape, F32)

    lane_a = lax.broadcasted_iota(I32, (tq, LANES), 1) < HEAD_DIM

    def step(k_ref, v_ref, fk_ref, diag):
        tk = k_ref.shape[1]
        if diag:
            rq = lax.broadcasted_iota(I32, (tq, tk), 0)
            ck = lax.broadcasted_iota(I32, (tq, tk), 1)
            visible = ck <= rq
        fq_all = fq_ref[0]
        pairs = range(N_HEADS // 2)
        cols = [slice(j * LANES, (j + 1) * LANES) for j in pairs]
        scores = []
        for j in pairs:
            qj = q_ref[0, :, cols[j]]
            kb = k_ref[0, :, cols[j]].astype(BF16)
            for hh in range(2):
                h = 2 * j + hh
                qm = jnp.where(lane_a if hh == 0 else jnp.logical_not(lane_a), qj, jnp.zeros_like(qj))
                s = lax.dot_general(qm, kb, NT, preferred_element_type=F32)
                s = s + fq_all[:, h:h + 1] - fk_ref[0, h:h + 1, :]
                if diag:
                    s = jnp.where(visible, s, NEG_BIG)
                scores.append(s)
        alphas, probs = [], []
        for h in range(N_HEADS):
            m_old = m_scr[h]
            m_new = jnp.maximum(m_old, jnp.max(scores[h], axis=-1, keepdims=True))
            alpha = jnp.exp(m_old - m_new)
            p = jnp.exp(scores[h] - m_new)
            l_scr[h] = alpha * l_scr[h] + jnp.sum(p, axis=-1, keepdims=True)
            m_scr[h] = m_new
            alphas.append(alpha)
            probs.append(p.astype(BF16))
        for j in pairs:
            vb = v_ref[0, :, cols[j]].astype(BF16)
            pv0 = jnp.dot(probs[2 * j], vb, preferred_element_type=F32)
            pv1 = jnp.dot(probs[2 * j + 1], vb, preferred_element_type=F32)
            acc_scr[:, cols[j]] = (acc_scr[:, cols[j]] * jnp.where(lane_a, alphas[2 * j], alphas[2 * j + 1])
                                   + jnp.where(lane_a, pv0, pv1))

    if n_past_blocks:
        @pl.when(ki < n_past_blocks)
        def _():
            step(kp_ref, vp_ref, fkp_ref, False)

    kn = ki - n_past_blocks

    @pl.when(jnp.logical_and(kn >= 0, kn < qi))
    def _():
        step(kn_ref, vn_ref, fkn_ref, False)

    @pl.when(kn == qi)
    def _():
        step(kn_ref, vn_ref, fkn_ref, True)

    @pl.when(ki == nk - 1)
    def _():
        for j in range(N_HEADS // 2):
            cols = slice(j * LANES, (j + 1) * LANES)
            l = jnp.where(lane_a, l_scr[2 * j], l_scr[2 * j + 1])
            o_ref[0, :, cols] = acc_scr[:, cols] / l * sg_ref[0, :, cols]


def _fox_attention(q, fq, sg, k_new, v_new, fk_new_t, past=None, tq=512, tk_past=512):
    b, t, _ = q.shape
    nq = t // tq
    n_past_blocks = 0 if past is None else past[0].shape[1] // tk_past
    nk = n_past_blocks + nq

    def new_idx(i, qi, ki):
        return jnp.clip(ki - n_past_blocks, 0, qi)

    in_specs = [pl.BlockSpec((1, tq, WIDTH), lambda i, qi, ki: (i, qi, 0)),
                pl.BlockSpec((1, tq, N_HEADS), lambda i, qi, ki: (i, qi, 0)),
                pl.BlockSpec((1, tq, WIDTH), lambda i, qi, ki: (i, qi, 0))]
    args = [q, fq, sg]
    if n_past_blocks:
        def past_idx(i, qi, ki):
            return jnp.minimum(ki, n_past_blocks - 1)
        in_specs += [pl.BlockSpec((1, tk_past, WIDTH), lambda i, qi, ki: (i, past_idx(i, qi, ki), 0)),
                     pl.BlockSpec((1, tk_past, WIDTH), lambda i, qi, ki: (i, past_idx(i, qi, ki), 0)),
                     pl.BlockSpec((1, N_HEADS, tk_past), lambda i, qi, ki: (i, 0, past_idx(i, qi, ki)))]
        args += list(past)
    in_specs += [pl.BlockSpec((1, tq, WIDTH), lambda i, qi, ki: (i, new_idx(i, qi, ki), 0)),
                 pl.BlockSpec((1, tq, WIDTH), lambda i, qi, ki: (i, new_idx(i, qi, ki), 0)),
                 pl.BlockSpec((1, N_HEADS, tq), lambda i, qi, ki: (i, 0, new_idx(i, qi, ki)))]
    args += [k_new, v_new, fk_new_t]
    return pl.pallas_call(
        functools.partial(_fox_kernel, n_past_blocks=n_past_blocks, tq=tq),
        grid=(b, nq, nk),
        in_specs=in_specs,
        out_specs=pl.BlockSpec((1, tq, WIDTH), lambda i, qi, ki: (i, qi, 0)),
        out_shape=jax.ShapeDtypeStruct((b, t, WIDTH), F32),
        scratch_shapes=[pltpu.VMEM((N_HEADS, tq, 1), F32), pltpu.VMEM((N_HEADS, tq, 1), F32),
                        pltpu.VMEM((tq, WIDTH), F32)],
        compiler_params=_cparams(("parallel", "parallel", "arbitrary")),
        name="fox_attention",
    )(*args)


def _rwkv_kernel(p_ref, sh0_ref, s0_ref, mu_ref, w0_ref, wb_ref, a0_ref, ab_ref, gb_ref, kk_ref, ka_ref, rk_ref,
                 lnw_ref, lnb_ref, gm_ref, y_ref, st_ref, sht_ref, z_scr, prev_scr, *, c):
    t = pl.program_id(1)
    nt = pl.num_programs(1)
    n_rows = p_ref.shape[1]
    n_chunks = n_rows // c

    def head_block(h):
        lo = (h % HEADS_PER_GROUP) * HEAD_DIM
        return h // HEADS_PER_GROUP, slice(lo, lo + HEAD_DIM)

    @pl.when(t == 0)
    def _():
        z_scr[...] = jnp.zeros(z_scr.shape, F32)
        for h in range(N_HEADS):
            i, blk = head_block(h)
            z_scr[i, blk, blk] = s0_ref[0, h]
        prev_scr[...] = sh0_ref[0]

    p = p_ref[0]
    row = lax.broadcasted_iota(I32, p.shape, 0)
    prev = jnp.where(row == 0, prev_scr[...], pltpu.roll(p, 1, 0))
    last = p[n_rows - 1:n_rows, :]
    prev_scr[...] = last
    sht_ref[0] = last

    pm = p + (prev - p) * mu_ref[...]
    r = pm[:, 0:WIDTH]
    k = pm[:, WIDTH:2 * WIDTH]
    v = pm[:, 2 * WIDTH:3 * WIDTH]
    o1 = 3 * WIDTH
    wd = pm[:, o1:o1 + DECAY_LORA]
    ad = pm[:, o1 + DECAY_LORA:o1 + DECAY_LORA + ICLR_LORA]
    gd = pm[:, o1 + DECAY_LORA + ICLR_LORA:RWKV_COLS]

    w = -_softplus(-(w0_ref[...] + _dot(jnp.tanh(wd), wb_ref[...]))) - 0.5
    lw = -jnp.exp(w)
    a = _sigmoid(a0_ref[...] + _dot(ad, ab_ref[...]))
    g = _dot(_sigmoid(gd), gb_ref[...])
    kk = k * kk_ref[...]
    kk = kk / jnp.maximum(jnp.sqrt(_gsum(kk * kk, gm_ref)), L2_EPS)
    kf = k * (1.0 + (a - 1.0) * ka_ref[...])

    ri = lax.broadcasted_iota(I32, (n_rows, n_rows), 0)
    ci = lax.broadcasted_iota(I32, (n_rows, n_rows), 1)
    same_chunk = (ri // c) == (ci // c)
    cum = _dot_exact_rhs(jnp.logical_and(same_chunk, ri >= ci).astype(F32), lw)
    cum_last = _dot_exact_rhs(same_chunk.astype(F32), lw)
    r_t = r * jnp.exp(cum)
    a_t = -kk * jnp.exp(cum - lw)
    inv = jnp.exp(-cum)
    b_t = kk * a * inv
    k_t = kf * inv
    to_end = jnp.exp(cum_last - cum)
    b_e = kk * a * to_end
    k_e = kf * to_end
    g_end = jnp.exp(cum_last)

    hg = HEADS_PER_GROUP
    gw = hg * HEAD_DIM
    log_c = int(math.log2(c))
    t_idx = lax.broadcasted_iota(I32, (c, hg * c), 0)
    s_idx = lax.broadcasted_iota(I32, (c, hg * c), 1) & (c - 1)
    strict = s_idx < t_idx
    lower = s_idx <= t_idx
    eye = (s_idx == t_idx).astype(F32)
    rb = lax.broadcasted_iota(I32, (hg * c, gw), 0) >> log_c
    mask_kv = rb == (lax.broadcasted_iota(I32, (hg * c, gw), 1) >> int(math.log2(HEAD_DIM)))
    rs = lax.broadcasted_iota(I32, (hg * c, hg * c), 0) >> log_c
    mask_ss = rs == (lax.broadcasted_iota(I32, (hg * c, hg * c), 1) >> log_c)
    ng = N_HEADS // hg
    cat = functools.partial(jnp.concatenate, axis=0)
    units = [(slice(j * c, (j + 1) * c), slice(i * gw, (i + 1) * gw)) for j in range(n_chunks) for i in range(ng)]
    nu = len(units)

    def mm1(a, b_bd):
        return jnp.dot(a.astype(BF16), b_bd, preferred_element_type=F32)

    def dg(a, b, dims):
        return lax.dot_general(a.astype(BF16), b.astype(BF16), dims, preferred_element_type=F32)

    def bd1(x, mask):
        tiled = jnp.concatenate([x.astype(BF16)] * hg, axis=0)
        return jnp.where(mask, tiled, jnp.zeros_like(tiled))

    ar = [cat([a_t[rs_, s], r_t[rs_, s]]).astype(BF16) for rs_, s in units]
    ab = [dg(ar[n], bd1(b_t[units[n]], mask_kv), NT) for n in range(nu)]
    ak = [dg(ar[n], bd1(k_t[units[n]], mask_kv), NT) for n in range(nu)]
    l_ab = [jnp.where(strict, m[:c], 0.0) for m in ab]
    l_rb = [jnp.where(lower, m[c:], 0.0) for m in ab]
    l_ak = [jnp.where(strict, m[:c], 0.0) for m in ak]
    l_rk = [jnp.where(lower, m[c:], 0.0) for m in ak]
    tinv = [eye + m for m in l_ab]
    pw = [mm1(m, bd1(m, mask_ss)) for m in l_ab]
    for _ in range(1, log_c - 1):
        res = [mm1(cat([tinv[n], pw[n]]), bd1(pw[n], mask_ss)) for n in range(nu)]
        tinv = [tinv[n] + res[n][:c] for n in range(nu)]
        pw = [m[c:] for m in res]
    tinv = [tinv[n] + mm1(tinv[n], bd1(pw[n], mask_ss)) for n in range(nu)]
    av = [mm1(cat([l_ak[n], l_rk[n]]), bd1(v[units[n]], mask_kv)) for n in range(nu)]
    ue = [cat([b_e[units[n]], k_e[units[n]]]).astype(BF16) for n in range(nu)]

    def wide(fn, x):
        return [fn(x[:, :gw]), fn(x[:, gw:])]

    def bd1w(x):
        return jnp.concatenate(wide(lambda h_: bd1(h_, mask_kv), x), axis=1)

    rhs = [jnp.concatenate([a_t[units[n]], av[n][:c]], axis=1) for n in range(nu)]
    x0 = [mm1(tinv[n], bd1w(rhs[n])) for n in range(nu)]
    resid = [rhs[n] - (x0[n] - mm1(l_ab[n], bd1w(x0[n]))) for n in range(nu)]
    sol = [x0[n] + mm1(tinv[n], bd1w(resid[n])) for n in range(nu)]
    lift = [mm1(l_rb[n], bd1w(sol[n])) for n in range(nu)]
    lhs_s = [cat([sol[n][:, :gw], r_t[units[n]] + lift[n][:, :gw]]).astype(BF16) for n in range(nu)]
    u_loc = [sol[n][:, gw:] for n in range(nu)]
    o_loc = [av[n][c:] + lift[n][:, gw:] for n in range(nu)]

    zr = lax.broadcasted_iota(I32, (gw, gw), 0) >> int(math.log2(HEAD_DIM))
    zmask = zr == (lax.broadcasted_iota(I32, (gw, gw), 1) >> int(math.log2(HEAD_DIM)))
    z = [z_scr[i] for i in range(ng)]
    o_rows = []
    for j in range(n_chunks):
        o_grp = []
        for i in range(ng):
            n = j * ng + i
            rs_, s = units[n]
            sz = dg(lhs_s[n], z[i], NT)
            u = sz[:c] + u_loc[n]
            o_grp.append(sz[c:] + o_loc[n])
            upd = dg(cat([u, v[rs_, s]]), ue[n], TN)
            z[i] = z[i] * g_end[j * c:j * c + 1, s] + jnp.where(zmask, upd, 0.0)
        o_rows.append(jnp.concatenate(o_grp, axis=1))
    for i in range(ng):
        z_scr[i] = z[i]

    o = cat(o_rows)
    inv_hd = 1.0 / HEAD_DIM
    dlt = o - _gsum(o, gm_ref) * inv_hd
    var = _gsum(dlt * dlt, gm_ref) * inv_hd
    on = dlt * lax.rsqrt(var + RWKV_GN_EPS) * lnw_ref[...] + lnb_ref[...]
    bonus = _gsum(r * kf * rk_ref[...], gm_ref) * v
    y_ref[0] = (on + bonus) * g

    @pl.when(t == nt - 1)
    def _():
        for h in range(N_HEADS):
            i, blk = head_block(h)
            st_ref[0, h] = z_scr[i, blk, blk]


def _rwkv(p, shift0, s0, prm, gmat, chunk, chunks_per_step):
    b, t, _ = p.shape
    row = lambda n: _const_spec((1, n))
    rows = chunk * chunks_per_step
    return pl.pallas_call(
        functools.partial(_rwkv_kernel, c=chunk),
        grid=(b, t // rows),
        in_specs=[pl.BlockSpec((1, rows, RWKV_COLS), lambda i, j: (i, j, 0)),
                  pl.BlockSpec((1, 1, RWKV_COLS), lambda i, j: (i, 0, 0)),
                  pl.BlockSpec((1, N_HEADS, HEAD_DIM, HEAD_DIM), lambda i, j: (i, 0, 0, 0)),
                  row(RWKV_COLS), row(WIDTH), _const_spec((DECAY_LORA, WIDTH)), row(WIDTH),
                  _const_spec((ICLR_LORA, WIDTH)), _const_spec((GATE_LORA, WIDTH)),
                  row(WIDTH), row(WIDTH), row(WIDTH), row(WIDTH), row(WIDTH), _const_spec((WIDTH, WIDTH))],
        out_specs=[pl.BlockSpec((1, rows, WIDTH), lambda i, j: (i, j, 0)),
                   pl.BlockSpec((1, N_HEADS, HEAD_DIM, HEAD_DIM), lambda i, j: (i, 0, 0, 0)),
                   pl.BlockSpec((1, 1, RWKV_COLS), lambda i, j: (i, 0, 0))],
        out_shape=[jax.ShapeDtypeStruct((b, t, WIDTH), F32),
                   jax.ShapeDtypeStruct((b, N_HEADS, HEAD_DIM, HEAD_DIM), F32),
                   jax.ShapeDtypeStruct((b, 1, RWKV_COLS), F32)],
        scratch_shapes=[pltpu.VMEM((N_HEADS // HEADS_PER_GROUP, HEADS_PER_GROUP * HEAD_DIM,
                                    HEADS_PER_GROUP * HEAD_DIM), F32),
                        pltpu.VMEM((1, RWKV_COLS), F32)],
        compiler_params=_cparams(("parallel", "arbitrary")),
        name="rwkv7_mix",
    )(p, shift0, s0, prm["mu"], prm["w0"], prm["wb"], prm["a0"], prm["ab"], prm["gb"], prm["kk"], prm["ka"],
      prm["rk"], prm["lnw"], prm["lnb"], gmat)


def _merge_kernel(x_ref, ya_ref, yb_ref, gate_ref, mod_ref, g2_ref, woa_ref, wob_ref, wo_ref, wrh_ref, wrl_ref,
                  *rest):
    x1_ref, h2_ref, lg_ref = rest[-3:]
    bb, tt, d = x_ref.shape
    m = bb * tt
    gate = gate_ref[...].reshape(m, GATE_COLS)
    merged = (gate[:, 0:d] * _dot(ya_ref[...].reshape(m, WIDTH), woa_ref[...])
              + gate[:, d:2 * d] * _dot(yb_ref[...].reshape(m, WIDTH), wob_ref[...]))
    x1 = x_ref[...] + mod_ref[:, 2:3, :] * _dot(merged, wo_ref[...]).reshape(bb, tt, d)
    x1_ref[...] = x1
    ms = jnp.mean(x1 * x1, axis=-1, keepdims=True)
    h2 = x1 * lax.rsqrt(ms + RMS_EPS) * g2_ref[...]
    h2 = (h2 * (1.0 + mod_ref[:, 4:5, :]) + mod_ref[:, 3:4, :]).reshape(m, d)
    h2_ref[...] = _pack_bf16_pairs(h2)
    lg_ref[...] = _mm3((wrh_ref[...], wrl_ref[...]), _split2(h2), NT)


def _merge(x, ya, yb, gate, mod, w, n_total, row_offset, shared=None):
    b, t, d = x.shape
    bb, tt = _token_blocks(b, t, MOE_TOKEN_ROWS)
    nt = t // tt
    m = bb * tt
    off = row_offset // m

    def tok(cols):
        return pl.BlockSpec((bb, tt, cols), lambda i, j: (i, j, 0))

    in_specs = [tok(d), tok(WIDTH), tok(WIDTH), tok(GATE_COLS),
                pl.BlockSpec((bb, 6, d), lambda i, j: (i, 0, 0)),
                _const_spec((1, d)), _const_spec((WIDTH, d)), _const_spec((WIDTH, d)), _const_spec((d, d)),
                _const_spec((N_EXPERTS, d)), _const_spec((N_EXPERTS, d))]
    args = [x, ya, yb, gate, mod.reshape(b, 6, d), w["g2"], w["w_oa"], w["w_ob"], w["w_o"], w["wr_hi"], w["wr_lo"]]
    aliases = {}
    if shared is not None:
        aliases = {len(args): 1, len(args) + 1: 2}
        in_specs += [pl.BlockSpec(memory_space=pl.ANY), pl.BlockSpec(memory_space=pl.ANY)]
        args += list(shared)
    return pl.pallas_call(
        _merge_kernel,
        grid=(b // bb, nt),
        in_specs=in_specs,
        out_specs=[tok(d), pl.BlockSpec((m, d // 2), lambda i, j: (off + i * nt + j, 0)),
                   pl.BlockSpec((N_EXPERTS, m), lambda i, j: (0, off + i * nt + j))],
        out_shape=[jax.ShapeDtypeStruct((b, t, d), F32), jax.ShapeDtypeStruct((n_total, d // 2), I32),
                   jax.ShapeDtypeStruct((N_EXPERTS, n_total), F32)],
        input_output_aliases=aliases,
        compiler_params=_cparams(("parallel", "parallel")),
        name="merge_norm2_router",
    )(*args)


def _route_kernel(lg_ref, bias_ref, idx_ref, wt_ref, rank_ref, cnt_ref, carry):
    @pl.when(pl.program_id(0) == 0)
    def _():
        carry[...] = jnp.zeros(carry.shape, F32)

    tm = lg_ref.shape[1]
    scores = _sigmoid(lg_ref[...])
    sel = scores + bias_ref[...]
    row = lax.broadcasted_iota(I32, (N_EXPERTS, tm), 0)
    neg_inf = -jnp.inf

    def first_argmax(vals, rows):
        mx = jnp.max(vals, axis=0, keepdims=True)
        return mx, jnp.min(jnp.where(vals == mx, rows, N_EXPERTS), axis=0, keepdims=True)

    gslices = [slice(g * EXPERTS_PER_GROUP, (g + 1) * EXPERTS_PER_GROUP) for g in range(N_GROUPS)]
    gs = []
    row_g = lax.broadcasted_iota(I32, (EXPERTS_PER_GROUP, tm), 0)
    for sl in gslices:
        m1, i1 = first_argmax(sel[sl], row_g)
        m2 = jnp.max(jnp.where(row_g == i1, neg_inf, sel[sl]), axis=0, keepdims=True)
        gs.append(m1 + m2)
    kept = []
    for g in range(N_GROUPS):
        beaten = jnp.zeros((1, tm), I32)
        for o in range(N_GROUPS):
            if o != g:
                wins = (gs[o] >= gs[g]) if o < g else (gs[o] > gs[g])
                beaten = beaten + wins.astype(I32)
        kept.append(jnp.where(beaten < TOPK_GROUPS, sel[gslices[g]], neg_inf))
    cur = jnp.concatenate(kept, axis=0)

    idxs, ws = [], []
    picked = jnp.zeros((N_EXPERTS, tm), F32)
    for _ in range(TOP_K):
        _, ik = first_argmax(cur, row)
        hit = row == ik
        idxs.append(ik)
        ws.append(jnp.sum(jnp.where(hit, scores, 0.0), axis=0, keepdims=True))
        cur = jnp.where(hit, neg_inf, cur)
        picked = jnp.where(hit, 1.0, picked)
    wsum = ws[0]
    for k in range(1, TOP_K):
        wsum = wsum + ws[k]

    r = lax.broadcasted_iota(I32, (tm, tm), 0)
    c = lax.broadcasted_iota(I32, (tm, tm), 1)
    before = jnp.dot(picked.astype(BF16), (r < c).astype(BF16), preferred_element_type=F32) + carry[...]
    carry[...] = carry[...] + jnp.sum(picked, axis=1, keepdims=True)
    cnt_ref[...] = carry[...]

    kk = lax.broadcasted_iota(I32, (TOP_K, tm), 0)
    idx_o = jnp.zeros((TOP_K, tm), I32)
    wt_o = jnp.zeros((TOP_K, tm), F32)
    rank_o = jnp.zeros((TOP_K, tm), F32)
    for k in range(TOP_K):
        rk = jnp.sum(jnp.where(row == idxs[k], before, 0.0), axis=0, keepdims=True)
        idx_o = jnp.where(kk == k, idxs[k], idx_o)
        wt_o = jnp.where(kk == k, ws[k] / wsum * ROUTED_SCALE, wt_o)
        rank_o = jnp.where(kk == k, rk, rank_o)
    idx_ref[...] = idx_o
    wt_ref[...] = wt_o
    rank_ref[...] = rank_o.astype(I32)


def _route(logits_t, bias_col, tm):
    n = logits_t.shape[1]
    tokk = pl.BlockSpec((TOP_K, tm), lambda i: (0, i))
    return pl.pallas_call(
        _route_kernel,
        grid=(n // tm,),
        in_specs=[pl.BlockSpec((N_EXPERTS, tm), lambda i: (0, i)), _const_spec((N_EXPERTS, 1))],
        out_specs=[tokk, tokk, tokk, _const_spec((N_EXPERTS, 1))],
        out_shape=[jax.ShapeDtypeStruct((TOP_K, n), I32), jax.ShapeDtypeStruct((TOP_K, n), F32),
                   jax.ShapeDtypeStruct((TOP_K, n), I32), jax.ShapeDtypeStruct((N_EXPERTS, 1), F32)],
        scratch_shapes=[pltpu.VMEM((N_EXPERTS, 1), F32)],
        compiler_params=_cparams(("arbitrary",)),
        name="route_topk",
    )(logits_t, bias_col)


def _plan_kernel(cnt_ref, start_ref, be_ref, valid_ref, nu_ref, *, blk):
    cnt = cnt_ref[...]
    padded = jnp.ceil(cnt * (1.0 / blk)) * blk
    e_r = lax.broadcasted_iota(I32, (N_EXPERTS, N_EXPERTS), 0)
    e_c = lax.broadcasted_iota(I32, (N_EXPERTS, N_EXPERTS), 1)
    incl = (e_r <= e_c).astype(BF16)
    ph, pm, plo = _split3(jnp.broadcast_to(padded, (8, N_EXPERTS)))
    d2 = functools.partial(jnp.dot, preferred_element_type=F32)
    pad_end = (d2(ph, incl) + (d2(pm, incl) + d2(plo, incl)))[0:1, :]
    pad_start = pad_end - padded
    start_ref[...] = pad_start.astype(I32)
    total = jnp.max(pad_end, axis=-1, keepdims=True)
    nu_ref[...] = jnp.broadcast_to(total * (1.0 / blk), (1, N_EXPERTS)).astype(I32)
    nb = be_ref.shape[0]
    first = (lax.broadcasted_iota(I32, (nb, N_EXPERTS), 0) * blk).astype(F32)
    lane = lax.broadcasted_iota(I32, (nb, N_EXPERTS), 1)
    inside = jnp.logical_and(pad_start <= first, first < pad_end)
    be_ref[...] = jnp.sum(jnp.where(inside, lane, 0), axis=-1, keepdims=True)
    rows = jnp.minimum(pad_start + cnt - first, float(blk))
    valid_ref[...] = jnp.sum(jnp.where(inside, rows, 0.0), axis=-1, keepdims=True).astype(I32)


def _plan(counts, n_blocks, blk):
    return pl.pallas_call(
        functools.partial(_plan_kernel, blk=blk),
        out_shape=[jax.ShapeDtypeStruct((1, N_EXPERTS), I32), jax.ShapeDtypeStruct((n_blocks, 1), I32),
                   jax.ShapeDtypeStruct((n_blocks, 1), I32), jax.ShapeDtypeStruct((1, N_EXPERTS), I32)],
        compiler_params=pltpu.CompilerParams(vmem_limit_bytes=VMEM_LIMIT),
        name="dispatch_plan",
    )(counts)


def _dest_kernel(idx_ref, rank_ref, start_ref, dest_ref):
    tm = idx_ref.shape[1]
    row = lax.broadcasted_iota(I32, (N_EXPERTS, tm), 0)
    kk = lax.broadcasted_iota(I32, (TOP_K, tm), 0)
    idx = idx_ref[...]
    base = jnp.zeros((TOP_K, tm), I32)
    for k in range(TOP_K):
        bk = jnp.sum(jnp.where(row == idx[k:k + 1, :], start_ref[...], 0), axis=0, keepdims=True)
        base = jnp.where(kk == k, bk, base)
    dest_ref[...] = base + rank_ref[...]


def _dest(idx, rank, pad_start_col, tm):
    n = idx.shape[1]
    tokk = pl.BlockSpec((TOP_K, tm), lambda i: (0, i))
    return pl.pallas_call(
        _dest_kernel,
        grid=(n // tm,),
        in_specs=[tokk, tokk, _const_spec((N_EXPERTS, 1))],
        out_specs=tokk,
        out_shape=jax.ShapeDtypeStruct((TOP_K, n), I32),
        compiler_params=_cparams(("parallel",)),
        name="dispatch_dest",
    )(idx, rank, pad_start_col)


def _dispatch(h2p, dest_t, n_slots):
    n, wp = h2p.shape
    half = wp // SC_ROW_SPLIT
    window = SC_SCATTER_WINDOW
    mesh = plsc.VectorSubcoreMesh(core_axis_name="core", subcore_axis_name="subcore")
    out = jax.ShapeDtypeStruct((n_slots, half), h2p.dtype)

    @functools.partial(pl.kernel, out_type=[out] * SC_ROW_SPLIT, mesh=mesh, scratch_types=[])
    def scatter(rows_hbm, idx_hbm, *outs):
        for c, out_hbm in enumerate(outs):
            def body(rows_vmem, idx_vmem, out_hbm=out_hbm):
                for k in range(TOP_K):
                    pltpu.sync_copy(rows_vmem, out_hbm.at[idx_vmem.at[k]])

            pltpu.emit_pipeline(
                body, grid=(n // window,),
                in_specs=[pl.BlockSpec((window, half), index_map=lambda i, c=c: (i, c)),
                          pl.BlockSpec((TOP_K, window), index_map=lambda i: (0, i))],
                out_specs=[], core_axis_name=("core", "subcore"), dimension_semantics=(pltpu.PARALLEL,),
            )(rows_hbm, idx_hbm)

    return scatter(h2p, dest_t)


def _expert_kernel(be_ref, valid_ref, nu_ref, xa_ref, xb_ref, wg_ref, wu_ref, wd_ref, *rest):
    y_refs, (wg_b, wu_b, wd_b) = rest[:SC_ROW_SPLIT], rest[SC_ROW_SPLIT:]
    i = pl.program_id(0)
    nv = valid_ref[i]
    new_expert = jnp.logical_or(i == 0, be_ref[i] != be_ref[jnp.maximum(i - 1, 0)])

    @pl.when(jnp.logical_and(nv > 0, new_expert))
    def _():
        wg_b[...] = wg_ref[0].astype(BF16)
        wu_b[...] = wu_ref[0].astype(BF16)
        wd_b[...] = wd_ref[0].astype(BF16)

    @pl.when(nv > 0)
    def _():
        blk = xa_ref.shape[0]
        rows = lax.broadcasted_iota(I32, (blk, 1), 0)
        packed = jnp.concatenate([xa_ref[...], xb_ref[...]], axis=1)
        x = _unpack_bf16_pairs(jnp.where(rows < nv, packed, 0))
        hg = jnp.dot(x, wg_b[...], preferred_element_type=F32)
        hu = jnp.dot(x, wu_b[...], preferred_element_type=F32)
        y = _pack_bf16_pairs(jnp.dot((_silu(hg) * hu).astype(BF16), wd_b[...], preferred_element_type=F32))
        cw = y.shape[1] // SC_ROW_SPLIT
        for c, y_ref in enumerate(y_refs):
            y_ref[...] = y[:, c * cw:(c + 1) * cw]


def _experts(xs, block_e, valid, n_used, w_eg, w_eu, w_ed, blk):
    xa, xb = xs
    n_slots, packed = xa.shape
    d = w_eg.shape[1]
    n_blocks = n_slots // blk

    def row_blk(i, be, valid, nu):
        return (jnp.minimum(i, nu[0] - 1), 0)

    def w_blk(i, be, valid, nu):
        return (be[i], 0, 0)

    return pl.pallas_call(
        _expert_kernel,
        grid_spec=pltpu.PrefetchScalarGridSpec(
            num_scalar_prefetch=3,
            grid=(n_blocks,),
            in_specs=[pl.BlockSpec((blk, packed), row_blk), pl.BlockSpec((blk, packed), row_blk),
                      pl.BlockSpec((1, d, D_EXPERT), w_blk), pl.BlockSpec((1, d, D_EXPERT), w_blk),
                      pl.BlockSpec((1, D_EXPERT, d), w_blk)],
            out_specs=[pl.BlockSpec((blk, packed), row_blk)] * SC_ROW_SPLIT,
            scratch_shapes=[pltpu.VMEM((d, D_EXPERT), BF16), pltpu.VMEM((d, D_EXPERT), BF16),
                            pltpu.VMEM((D_EXPERT, d), BF16)]),
        out_shape=[jax.ShapeDtypeStruct((n_slots, packed), I32)] * SC_ROW_SPLIT,
        compiler_params=_cparams(("arbitrary",)),
        name="moe_experts",
    )(block_e, valid, n_used, xa, xb, w_eg, w_eu, w_ed)


def _combine_gather(ys, dest_t):
    k, n = dest_t.shape
    cw = ys[0].shape[1]
    window = SC_SCATTER_WINDOW
    mesh = plsc.VectorSubcoreMesh(core_axis_name="core", subcore_axis_name="subcore")

    @functools.partial(pl.kernel, out_type=jax.ShapeDtypeStruct((k * n, cw * len(ys)), ys[0].dtype), mesh=mesh,
                       scratch_types=[])
    def gather(*refs):
        y_refs, idx_hbm, out_hbm = refs[:len(ys)], refs[len(ys)], refs[len(ys) + 1]
        for c, y_hbm in enumerate(y_refs):
            def body(idx_vmem, out_vmem, y_hbm=y_hbm):
                pltpu.sync_copy(y_hbm.at[idx_vmem.at[0]], out_vmem)

            pltpu.emit_pipeline(
                body, grid=(k * n // window,),
                in_specs=[pl.BlockSpec((1, window), index_map=lambda i: (0, i))],
                out_specs=[pl.BlockSpec((window, cw), index_map=lambda i, c=c: (i, c))],
                core_axis_name=("core", "subcore"), dimension_semantics=(pltpu.PARALLEL,),
            )(idx_hbm, out_hbm)

    return gather(*ys, dest_t.reshape(1, k * n)).reshape(k, n, cw * len(ys))


def _final_kernel(x1_ref, h2_ref, wt_ref, mod_ref, wsg_ref, wsu_ref, wsd_ref, yg_ref, o_ref):
    bb, tt, d = x1_ref.shape
    hb = _unpack_bf16_pairs(h2_ref[...])
    hg = jnp.dot(hb, wsg_ref[...], preferred_element_type=F32)
    hu = jnp.dot(hb, wsu_ref[...], preferred_element_type=F32)
    ffn = _dot(_silu(hg) * hu, wsd_ref[...])
    wt = wt_ref[...]
    for k in range(TOP_K):
        ffn = ffn + wt[:, k:k + 1] * _unpack_bf16_pairs(yg_ref[k]).astype(F32)
    o_ref[...] = x1_ref[...] + mod_ref[:, 5:6, :] * ffn.reshape(bb, tt, d)


def _final(x1, h2_all, wts_all, y_tok, mod, w, row_offset):
    b, t, d = x1.shape
    bb, tt = _token_blocks(b, t, MOE_TOKEN_ROWS)
    nt = t // tt
    m = bb * tt
    off = row_offset // m

    def flat_idx(i, j):
        return off + i * nt + j

    return pl.pallas_call(
        _final_kernel,
        grid=(b // bb, nt),
        in_specs=[pl.BlockSpec((bb, tt, d), lambda i, j: (i, j, 0)),
                  pl.BlockSpec((m, h2_all.shape[1]), lambda i, j: (flat_idx(i, j), 0)),
                  pl.BlockSpec((m, TOP_K), lambda i, j: (flat_idx(i, j), 0)),
                  pl.BlockSpec((bb, 6, d), lambda i, j: (i, 0, 0)),
                  _const_spec((d, D_EXPERT)), _const_spec((d, D_EXPERT)), _const_spec((D_EXPERT, d)),
                  pl.BlockSpec((TOP_K, m, y_tok.shape[2]), lambda i, j: (0, flat_idx(i, j), 0))],
        out_specs=pl.BlockSpec((bb, tt, d), lambda i, j: (i, j, 0)),
        out_shape=jax.ShapeDtypeStruct((b, t, d), F32),
        compiler_params=_cparams(("parallel", "parallel")),
        name="moe_combine_final",
    )(x1, h2_all, wts_all, mod.reshape(b, 6, d), w["w_sg"], w["w_su"], w["w_sd"], y_tok)


def _largest_tile(n, candidates):
    return next(c for c in candidates if n % c == 0)


def _moe_routed(h2_all, logits_all, w, blk=EXPERT_BLOCK_ROWS):
    n = h2_all.shape[0]
    n_blocks = (n * TOP_K + N_EXPERTS * (blk - 1)) // blk + 1
    n_blocks = (n_blocks + 7) // 8 * 8
    idx, wts_t, rank, counts = _route(logits_all, w["router_bias"], _largest_tile(n, (512, 256)))
    pad_start, block_e, valid, n_used = _plan(counts.reshape(1, N_EXPERTS), n_blocks, blk)
    block_e = block_e.reshape(n_blocks)
    valid = valid.reshape(n_blocks)
    n_used = n_used[0, 0:1]
    dest_t = _dest(idx, rank, pad_start.reshape(N_EXPERTS, 1), _largest_tile(n, (1024, 512, 256)))
    xs = _dispatch(h2_all, dest_t, n_blocks * blk)
    ys = _experts(xs, block_e, valid, n_used, w["w_eg"], w["w_eu"], w["w_ed"], blk)
    return _combine_gather(ys, dest_t), jnp.transpose(wts_t)


def _prep(raw):
    p = {k: v[0] for k, v in raw.items()}
    w_in = p["w_in"]
    o_fox = RWKV_COLS
    o_fl = o_fox + FOX_MAIN_COLS
    o_gate = o_fl + N_HEADS
    row = lambda a: a.reshape(1, -1)
    return dict(
        w_ada=p["w_ada"], b_ada=p["b_ada"],
        g1=row(p["norm1_g"]), g2=row(p["norm2_g"]),
        wr=w_in[:, :o_fox].astype(BF16),
        wf=w_in[:, o_fox:o_fl].astype(BF16),
        wfl=jnp.pad(w_in[:, o_fl:o_gate], ((0, 0), (0, LANES - N_HEADS))).astype(BF16),
        wg=w_in[:, o_gate:].astype(BF16),
        qn=row(jnp.tile(p["fox_q_norm"], N_HEADS)), kn=row(jnp.tile(p["fox_k_norm"], N_HEADS)),
        fb=row(p["fox_f_bias"]),
        gmat=_group_ones(),
        rwkv=dict(mu=row(p["rwkv_mu"]), w0=row(p["rwkv_w0"]), wb=p["rwkv_w_lora_b"], a0=row(p["rwkv_a0"]),
                  ab=p["rwkv_a_lora_b"], gb=p["rwkv_g_lora_b"], kk=row(p["rwkv_k_k"]), ka=row(p["rwkv_k_a"]),
                  rk=row(p["rwkv_r_k"]), lnw=row(p["rwkv_ln_w"]), lnb=row(p["rwkv_ln_b"])),
        w_oa=p["w_out_rwkv"].astype(BF16), w_ob=p["w_out_fox"].astype(BF16), w_o=p["w_out"].astype(BF16),
        wr_hi=p["w_router"].T.astype(BF16),
        wr_lo=(p["w_router"] - p["w_router"].astype(BF16).astype(F32)).T.astype(BF16),
        router_bias=p["router_bias"].reshape(N_EXPERTS, 1),
        w_eg=p["w_exp_gate"], w_eu=p["w_exp_up"], w_ed=p["w_exp_down"],
        w_sg=p["w_sh_gate"].astype(BF16), w_su=p["w_sh_up"].astype(BF16), w_sd=p["w_sh_down"].astype(BF16),
    )


def _token_blocks(b, t, rows=256):
    if t >= rows:
        return 1, rows
    bb = max(1, min(b, 256 // t))
    while b % bb:
        bb -= 1
    return bb, t


def _mix_path(x, mod, shift0, wkv0, past_k, past_v, past_logf, w):
    b, t, d = x.shape
    bb, tt = _token_blocks(b, t)
    n_past = past_k.shape[1]
    if n_past:
        f_past = _past_cumsum(past_logf)
        init = f_past[:, n_past - 1:n_past, :]
        past = (past_k, past_v, jnp.swapaxes(f_past, 1, 2))
    else:
        init = jnp.zeros((b, 1, N_HEADS), F32)
        past = None
    pr, q, k, v, sg, logf, f_new, gate = _inproj(x, mod.reshape(b, 6, d), w["g1"], w["wr"], w["wf"], w["wfl"],
                                                 w["wg"], w["qn"], w["kn"], w["fb"], w["gmat"], init, bb, tt)
    y_fox = _fox_attention(q, f_new, sg, k, v, jnp.swapaxes(f_new, 1, 2), past=past, tq=min(t, 512),
                           tk_past=min(max(n_past, 1), 512))
    chunk = min(t, RWKV_CHUNK)
    y_rwkv, wkv_new, shift_new = _rwkv(pr, shift0.reshape(b, 1, RWKV_COLS), wkv0, w["rwkv"], w["gmat"],
                                       chunk, max(1, min(RWKV_CHUNKS_PER_STEP, t // chunk)))
    return y_rwkv, y_fox, gate, wkv_new, shift_new, k, v, logf


def _layer(paths, w):
    n_b = [p[0].shape[0] for p in paths]
    mod_all = _ada(jnp.concatenate([p[1] for p in paths], axis=0), w["w_ada"], w["b_ada"])
    mods, o = [], 0
    for nb in n_b:
        mods.append(mod_all[o:o + nb])
        o += nb
    n_total = sum(p[0].shape[0] * p[0].shape[1] for p in paths)
    mixed, x1s = [], []
    shared, row = None, 0
    for (x, _, shift0, wkv0, pk, pv, plf), mod in zip(paths, mods):
        ya, yb, gate, wkv_new, shift_new, k, v, logf = _mix_path(x, mod, shift0, wkv0, pk, pv, plf, w)
        x1, h2_all, lg_all = _merge(x, ya, yb, gate, mod, w, n_total, row, shared)
        shared = (h2_all, lg_all)
        row += x.shape[0] * x.shape[1]
        mixed.append((wkv_new, shift_new, k, v, logf))
        x1s.append(x1)
    y_tok, wts = _moe_routed(h2_all, lg_all, w)
    outs, row = [], 0
    for x1, mod, st in zip(x1s, mods, mixed):
        y = _final(x1, h2_all, wts, y_tok, mod, w, row)
        row += x1.shape[0] * x1.shape[1]
        outs.append((y,) + st)
    return outs


def kernel(x_prompt, x_sample, c_prompt, c_sample, state_rwkv_wkv, state_rwkv_shift, cache_fox_k, cache_fox_v,
           cache_fox_logf, w_ada, b_ada, norm1_g, norm2_g, w_in, rwkv_mu, rwkv_w0, rwkv_w_lora_b, rwkv_a0,
           rwkv_a_lora_b, rwkv_g_lora_b, rwkv_k_k, rwkv_k_a, rwkv_r_k, rwkv_ln_w, rwkv_ln_b, fox_q_norm,
           fox_k_norm, fox_f_bias, w_out_rwkv, w_out_fox, w_out, w_router, router_bias, w_exp_gate, w_exp_up,
           w_exp_down, w_sh_gate, w_sh_up, w_sh_down):
    raw = dict(w_ada=w_ada, b_ada=b_ada, norm1_g=norm1_g, norm2_g=norm2_g, w_in=w_in, rwkv_mu=rwkv_mu,
               rwkv_w0=rwkv_w0, rwkv_w_lora_b=rwkv_w_lora_b, rwkv_a0=rwkv_a0, rwkv_a_lora_b=rwkv_a_lora_b,
               rwkv_g_lora_b=rwkv_g_lora_b, rwkv_k_k=rwkv_k_k, rwkv_k_a=rwkv_k_a, rwkv_r_k=rwkv_r_k,
               rwkv_ln_w=rwkv_ln_w, rwkv_ln_b=rwkv_ln_b, fox_q_norm=fox_q_norm, fox_k_norm=fox_k_norm,
               fox_f_bias=fox_f_bias, w_out_rwkv=w_out_rwkv, w_out_fox=w_out_fox, w_out=w_out,
               w_router=w_router, router_bias=router_bias, w_exp_gate=w_exp_gate, w_exp_up=w_exp_up,
               w_exp_down=w_exp_down, w_sh_gate=w_sh_gate, w_sh_up=w_sh_up, w_sh_down=w_sh_down)
    assert w_in.shape[0] == 1, "single-layer stack"
    w = _prep(raw)
    bp, tp, _ = x_prompt.shape
    bs, ts, _ = x_sample.shape
    n_past = cache_fox_k.shape[2]
    prompt = (x_prompt, c_prompt, jnp.zeros((bp, RWKV_COLS), F32),
              jnp.zeros((bp, N_HEADS, HEAD_DIM, HEAD_DIM), F32),
              jnp.zeros((bp, 0, WIDTH), F32), jnp.zeros((bp, 0, WIDTH), F32), jnp.zeros((bp, 0, N_HEADS), F32))
    sample = (x_sample, c_sample, state_rwkv_shift[0], state_rwkv_wkv[0],
              cache_fox_k[0].reshape(bs, n_past, WIDTH), cache_fox_v[0].reshape(bs, n_past, WIDTH),
              cache_fox_logf[0])
    (yp, wkv_p, sh_p, k_p, v_p, lf_p), (ysm, wkv_s, sh_s, k_s, v_s, lf_s) = _layer([prompt, sample], w)

    def heads(a):
        return a.reshape((1,) + a.shape[:2] + (N_HEADS, HEAD_DIM))

    return (yp, ysm,
            wkv_p[None], sh_p.reshape(1, bp, RWKV_COLS), heads(k_p), heads(v_p), lf_p[None],
            wkv_s[None], sh_s.reshape(1, bs, RWKV_COLS), heads(k_s), heads(v_s), lf_s[None])
```

```python
import functools
import math

import jax
import jax.numpy as jnp
from jax import lax
from jax.experimental import pallas as pl
from jax.experimental.pallas import tpu as pltpu
from jax.experimental.pallas import tpu_sc as plsc

F32 = jnp.float32
BF16 = jnp.bfloat16
I32 = jnp.int32

D_MODEL = 1024
N_HEADS = 8
HEAD_DIM = 64
WIDTH = N_HEADS * HEAD_DIM
HEADS_PER_GROUP = 4
RWKV_CHUNK = 64
RWKV_CHUNKS_PER_STEP = 4
MOE_TOKEN_ROWS = 512
EXPERT_BLOCK_ROWS = 512
SC_SCATTER_WINDOW = 128
SC_ROW_SPLIT = 2
DECAY_LORA = 64
ICLR_LORA = 64
GATE_LORA = 128
RWKV_COLS = 3 * WIDTH + DECAY_LORA + ICLR_LORA + GATE_LORA
FOX_MAIN_COLS = 4 * WIDTH
GATE_COLS = 2 * D_MODEL
RWKV_GN_EPS = HEAD_DIM * 1e-5
L2_EPS = 1e-12
RMS_EPS = 1e-6
N_EXPERTS = 256
TOP_K = 8
N_GROUPS = 8
TOPK_GROUPS = 4
EXPERTS_PER_GROUP = N_EXPERTS // N_GROUPS
D_EXPERT = 256
ROUTED_SCALE = 2.5

LANES = 128
VMEM_LIMIT = 56 * 1024 * 1024
NEG_BIG = -1e30

NN = (((1,), (0,)), ((), ()))
NT = (((1,), (1,)), ((), ()))
TN = (((0,), (0,)), ((), ()))


def _cparams(sem):
    return pltpu.CompilerParams(dimension_semantics=sem, vmem_limit_bytes=VMEM_LIMIT)


def _dot(a, b, dims=NN):
    return lax.dot_general(a.astype(BF16), b.astype(BF16), dims, preferred_element_type=F32)


def _split2(a):
    hi = a.astype(BF16)
    lo = (a - hi.astype(F32)).astype(BF16)
    return hi, lo


def _split3(a):
    hi = a.astype(BF16)
    r1 = a - hi.astype(F32)
    mid = r1.astype(BF16)
    lo = (r1 - mid.astype(F32)).astype(BF16)
    return hi, mid, lo


def _mm3(a, b, dims):
    d = functools.partial(lax.dot_general, dimension_numbers=dims, preferred_element_type=F32)
    return d(a[0], b[0]) + (d(a[0], b[1]) + d(a[1], b[0]))


def _dot_exact_rhs(a_exact, b, dims=NN):
    ab = a_exact.astype(BF16)
    bh, bm, bl = _split3(b)
    d = functools.partial(lax.dot_general, dimension_numbers=dims, preferred_element_type=F32)
    return d(ab, bh) + (d(ab, bm) + d(ab, bl))


def _gsum(x, g_ref):
    half = HEADS_PER_GROUP * HEAD_DIM
    hi, lo = _split2(x)
    g = g_ref[0:half, 0:half]
    d = functools.partial(jnp.dot, preferred_element_type=F32)
    return jnp.concatenate([d(hi[:, s], g) + d(lo[:, s], g) for s in (slice(0, half), slice(half, 2 * half))],
                           axis=1)


def _sigmoid(x):
    return 1.0 / (1.0 + jnp.exp(-x))


def _softplus(x):
    return jnp.maximum(x, 0.0) + jnp.log1p(jnp.exp(-jnp.abs(x)))


def _silu(x):
    return x * _sigmoid(x)


def _pack_bf16_pairs(x):
    w = x.shape[1] // 2
    bits = lax.bitcast_convert_type(x.astype(BF16).astype(F32), I32)
    return lax.shift_right_logical(bits[:, :w], 16) | (bits[:, w:] & -65536)


def _unpack_bf16_pairs(p):
    lo = lax.bitcast_convert_type(lax.shift_left(p, 16), F32)
    hi = lax.bitcast_convert_type(p & -65536, F32)
    return jnp.concatenate([lo, hi], axis=1).astype(BF16)


def _group_ones():
    h = jnp.arange(WIDTH, dtype=I32) // HEAD_DIM
    return (h[:, None] == h[None, :]).astype(BF16)


def _ada_kernel(c_ref, w_ref, b_ref, o_ref):
    o_ref[...] = _dot(_silu(c_ref[...]), w_ref[...]) + b_ref[...]


def _ada(c, w_ada, b_ada):
    nb = c.shape[0]
    n_out = w_ada.shape[1]
    blk = D_MODEL
    return pl.pallas_call(
        _ada_kernel,
        grid=(n_out // blk,),
        in_specs=[pl.BlockSpec((nb, D_MODEL), lambda j: (0, 0)),
                  pl.BlockSpec((D_MODEL, blk), lambda j: (0, j)),
                  pl.BlockSpec((1, blk), lambda j: (0, j))],
        out_specs=pl.BlockSpec((nb, blk), lambda j: (0, j)),
        out_shape=jax.ShapeDtypeStruct((nb, n_out), F32),
        compiler_params=_cparams(("parallel",)),
        name="ada_mod",
    )(c, w_ada, b_ada.reshape(1, n_out))


def _inproj_kernel(x_ref, mod_ref, g1_ref, wr_ref, wf_ref, wfl_ref, wg_ref, qn_ref, kn_ref, fb_ref, gm_ref, f0_ref,
                   pr_ref, q_ref, k_ref, v_ref, sg_ref, lf_ref, cf_ref, gate_ref, carry):
    bb, tt, d = x_ref.shape
    m = bb * tt
    x = x_ref[...]
    ms = jnp.mean(x * x, axis=-1, keepdims=True)
    h = x * lax.rsqrt(ms + RMS_EPS) * g1_ref[...]
    h = h * (1.0 + mod_ref[:, 1:2, :]) + mod_ref[:, 0:1, :]
    hb = h.reshape(m, d).astype(BF16)

    pr_ref[...] = jnp.dot(hb, wr_ref[...], preferred_element_type=F32).reshape(bb, tt, RWKV_COLS)

    f = jnp.dot(hb, wf_ref[...], preferred_element_type=F32)
    q = f[:, 0:WIDTH]
    k = f[:, WIDTH:2 * WIDTH]
    v = f[:, 2 * WIDTH:3 * WIDTH]
    og = f[:, 3 * WIDTH:4 * WIDTH]
    inv_hd = 1.0 / HEAD_DIM
    q = q * lax.rsqrt(_gsum(q * q, gm_ref) * inv_hd + RMS_EPS) * qn_ref[...]
    k = k * lax.rsqrt(_gsum(k * k, gm_ref) * inv_hd + RMS_EPS) * kn_ref[...]
    q_ref[...] = (q * (HEAD_DIM ** -0.5)).astype(BF16).reshape(bb, tt, WIDTH)
    k_ref[...] = k.reshape(bb, tt, WIDTH)
    v_ref[...] = v.reshape(bb, tt, WIDTH)
    sg_ref[...] = _sigmoid(og).reshape(bb, tt, WIDTH)

    fl = jnp.dot(hb, wfl_ref[...], preferred_element_type=F32)[:, 0:N_HEADS] + fb_ref[...]
    lf = -_softplus(-fl)
    lf_ref[...] = lf.reshape(bb, tt, N_HEADS)

    @pl.when(pl.program_id(1) == 0)
    def _():
        carry[...] = f0_ref[...]

    r = lax.broadcasted_iota(I32, (m, m), 0)
    c = lax.broadcasted_iota(I32, (m, m), 1)
    tri = jnp.logical_and(r // tt == c // tt, r >= c).astype(F32)
    cf = _dot_exact_rhs(tri, lf).reshape(bb, tt, N_HEADS) + carry[...]
    cf_ref[...] = cf
    carry[...] = cf[:, tt - 1:tt, :]

    gate_ref[...] =_sigmoid(jnp.dot(hb, wg_ref[...], preferred_element_type=F32)).reshape(bb, tt, GATE_COLS)


def _const_spec(shape):
    nd = len(shape)
    return pl.BlockSpec(shape, lambda *_: (0,) * nd)


def _inproj(x, mod, g1, wr, wf, wfl, wg, qn, kn, fb, gmat, f0, bb, tt):
    b, t, d = x.shape
    grid = (b // bb, t // tt)

    def tok(cols):
        return pl.BlockSpec((bb, tt, cols), lambda i, j: (i, j, 0))

    out_cols = [(RWKV_COLS, F32), (WIDTH, BF16), (WIDTH, F32), (WIDTH, F32), (WIDTH, F32), (N_HEADS, F32),
                (N_HEADS, F32), (GATE_COLS, F32)]
    return pl.pallas_call(
        _inproj_kernel,
        grid=grid,
        in_specs=[tok(d),
                  pl.BlockSpec((bb, 6, d), lambda i, j: (i, 0, 0)),
                  _const_spec((1, d)),
                  _const_spec(wr.shape), _const_spec(wf.shape), _const_spec(wfl.shape), _const_spec(wg.shape),
                  _const_spec((1, WIDTH)), _const_spec((1, WIDTH)), _const_spec((1, N_HEADS)),
                  _const_spec((WIDTH, WIDTH)),
                  pl.BlockSpec((bb, 1, N_HEADS), lambda i, j: (i, 0, 0))],
        out_specs=[tok(c) for c, _ in out_cols],
        out_shape=[jax.ShapeDtypeStruct((b, t, c), dt) for c, dt in out_cols],
        scratch_shapes=[pltpu.VMEM((bb, 1, N_HEADS), F32)],
        compiler_params=_cparams(("parallel", "arbitrary")),
        name="norm1_inproj",
    )(x, mod, g1, wr, wf, wfl, wg, qn, kn, fb, gmat, f0)


def _past_cumsum_kernel(x_ref, o_ref):
    x = x_ref[0]
    rows = x.shape[0]
    li = lax.broadcasted_iota(I32, (LANES, LANES), 0)
    lj = lax.broadcasted_iota(I32, (LANES, LANES), 1)
    same_head = (li % N_HEADS) == (lj % N_HEADS)
    within = jnp.logical_and(same_head, li // N_HEADS <= lj // N_HEADS).astype(BF16)
    xh, xm, xl = _split3(x)
    d2 = functools.partial(jnp.dot, preferred_element_type=F32)
    in_row = d2(xh, within) + (d2(xm, within) + d2(xl, within))
    sh = same_head.astype(BF16)
    row_tot = d2(xh, sh) + (d2(xm, sh) + d2(xl, sh))
    ri = lax.broadcasted_iota(I32, (rows, rows), 0)
    ci = lax.broadcasted_iota(I32, (rows, rows), 1)
    o_ref[0] = in_row + _dot_exact_rhs((ri > ci).astype(F32), row_tot)


def _past_cumsum(past_logf):
    b, p, h = past_logf.shape
    rows = p * h // LANES
    flat = past_logf.reshape(b, rows, LANES)
    out = pl.pallas_call(
        _past_cumsum_kernel,
        grid=(b,),
        in_specs=[pl.BlockSpec((1, rows, LANES), lambda i: (i, 0, 0))],
        out_specs=pl.BlockSpec((1, rows, LANES), lambda i: (i, 0, 0)),
        out_shape=jax.ShapeDtypeStruct((b, rows, LANES), F32),
        compiler_params=_cparams(("parallel",)),
        name="cache_logf_cumsum",
    )(flat)
    return out.reshape(b, p, h)


def _fox_kernel(*refs, n_past_blocks, tq):
    if n_past_blocks:
        (q_ref, fq_ref, sg_ref, kp_ref, vp_ref, fkp_ref, kn_ref, vn_ref, fkn_ref,
         o_ref, m_scr, l_scr, acc_scr) = refs
    else:
        q_ref, fq_ref, sg_ref, kn_ref, vn_ref, fkn_ref, o_ref, m_scr, l_scr, acc_scr = refs
    qi = pl.program_id(1)
    ki = pl.program_id(2)
    nk = pl.num_programs(2)

    @pl.when(ki == 0)
    def _():
        m_scr[...] = jnp.full(m_scr.shape, NEG_BIG, F32)
        l_scr[...] = jnp.zeros(l_scr.shape, F32)
        acc_scr[...] = jnp.zeros(acc_scr.shape, F32)

    lane_a = lax.broadcasted_iota(I32, (tq, LANES), 1) < HEAD_DIM

    def step(k_ref, v_ref, fk_ref, diag):
        tk = k_ref.shape[1]
        if diag:
            rq = lax.broadcasted_iota(I32, (tq, tk), 0)
            ck = lax.broadcasted_iota(I32, (tq, tk), 1)
            visible = ck <= rq
        fq_all = fq_ref[0]
        pairs = range(N_HEADS // 2)
        cols = [slice(j * LANES, (j + 1) * LANES) for j in pairs]
        scores = []
        for j in pairs:
            qj = q_ref[0, :, cols[j]]
            kb = k_ref[0, :, cols[j]].astype(BF16)
            for hh in range(2):
                h = 2 * j + hh
                qm = jnp.where(lane_a if hh == 0 else jnp.logical_not(lane_a), qj, jnp.zeros_like(qj))
                s = lax.dot_general(qm, kb, NT, preferred_element_type=F32)
                s = s + fq_all[:, h:h + 1] - fk_ref[0, h:h + 1, :]
                if diag:
                    s = jnp.where(visible, s, NEG_BIG)
                scores.append(s)
        alphas, probs = [], []
        for h in range(N_HEADS):
            m_old = m_scr[h]
            m_new = jnp.maximum(m_old, jnp.max(scores[h], axis=-1, keepdims=True))
            alpha = jnp.exp(m_old - m_new)
            p = jnp.exp(scores[h] - m_new)
            l_scr[h] = alpha * l_scr[h] + jnp.sum(p, axis=-1, keepdims=True)
            m_scr[h] = m_new
            alphas.append(alpha)
            probs.append(p.astype(BF16))
        for j in pairs:
            vb = v_ref[0, :, cols[j]].astype(BF16)
            pv0 = jnp.dot(probs[2 * j], vb, preferred_element_type=F32)
            pv1 = jnp.dot(probs[2 * j + 1], vb, preferred_element_type=F32)
            acc_scr[:, cols[j]] = (acc_scr[:, cols[j]] * jnp.where(lane_a, alphas[2 * j], alphas[2 * j + 1])
                                   + jnp.where(lane_a, pv0, pv1))

    if n_past_blocks:
        @pl.when(ki < n_past_blocks)
        def _():
            step(kp_ref, vp_ref, fkp_ref, False)

    kn = ki - n_past_blocks

    @pl.when(jnp.logical_and(kn >= 0, kn < qi))
    def _():
        step(kn_ref, vn_ref, fkn_ref, False)

    @pl.when(kn == qi)
    def _():
        step(kn_ref, vn_ref, fkn_ref, True)

    @pl.when(ki == nk - 1)
    def _():
        for j in range(N_HEADS // 2):
            cols = slice(j * LANES, (j + 1) * LANES)
            l = jnp.where(lane_a, l_scr[2 * j], l_scr[2 * j + 1])
            o_ref[0, :, cols] = acc_scr[:, cols] / l * sg_ref[0, :, cols]


def _fox_attention(q, fq, sg, k_new, v_new, fk_new_t, past=None, tq=512, tk_past=512):
    b, t, _ = q.shape
    nq = t // tq
    n_past_blocks = 0 if past is None else past[0].shape[1] // tk_past
    nk = n_past_blocks + nq

    def new_idx(i, qi, ki):
        return jnp.clip(ki - n_past_blocks, 0, qi)

    in_specs = [pl.BlockSpec((1, tq, WIDTH), lambda i, qi, ki: (i, qi, 0)),
                pl.BlockSpec((1, tq, N_HEADS), lambda i, qi, ki: (i, qi, 0)),
                pl.BlockSpec((1, tq, WIDTH), lambda i, qi, ki: (i, qi, 0))]
    args = [q, fq, sg]
    if n_past_blocks:
        def past_idx(i, qi, ki):
            return jnp.minimum(ki, n_past_blocks - 1)
        in_specs += [pl.BlockSpec((1, tk_past, WIDTH), lambda i, qi, ki: (i, past_idx(i, qi, ki), 0)),
                     pl.BlockSpec((1, tk_past, WIDTH), lambda i, qi, ki: (i, past_idx(i, qi, ki), 0)),
                     pl.BlockSpec((1, N_HEADS, tk_past), lambda i, qi, ki: (i, 0, past_idx(i, qi, ki)))]
        args += list(past)
    in_specs += [pl.BlockSpec((1, tq, WIDTH), lambda i, qi, ki: (i, new_idx(i, qi, ki), 0)),
                 pl.BlockSpec((1, tq, WIDTH), lambda i, qi, ki: (i, new_idx(i, qi, ki), 0)),
                 pl.BlockSpec((1, N_HEADS, tq), lambda i, qi, ki: (i, 0, new_idx(i, qi, ki)))]
    args += [k_new, v_new, fk_new_t]
    return pl.pallas_call(
        functools.partial(_fox_kernel, n_past_blocks=n_past_blocks, tq=tq),
        grid=(b, nq, nk),
        in_specs=in_specs,
        out_specs=pl.BlockSpec((1, tq, WIDTH), lambda i, qi, ki: (i, qi, 0)),
        out_shape=jax.ShapeDtypeStruct((b, t, WIDTH), F32),
        scratch_shapes=[pltpu.VMEM((N_HEADS, tq, 1), F32), pltpu.VMEM((N_HEADS, tq, 1), F32),
                        pltpu.VMEM((tq, WIDTH), F32)],
        compiler_params=_cparams(("parallel", "parallel", "arbitrary")),
        name="fox_attention",
    )(*args)


def _rwkv_kernel(p_ref, sh0_ref, s0_ref, mu_ref, w0_ref, wb_ref, a0_ref, ab_ref, gb_ref, kk_ref, ka_ref, rk_ref,
                 lnw_ref, lnb_ref, gm_ref, y_ref, st_ref, sht_ref, z_scr, prev_scr, *, c):
    t = pl.program_id(1)
    nt = pl.num_programs(1)
    n_rows = p_ref.shape[1]
    n_chunks = n_rows // c

    def head_block(h):
        lo = (h % HEADS_PER_GROUP) * HEAD_DIM
        return h // HEADS_PER_GROUP, slice(lo, lo + HEAD_DIM)

    @pl.when(t == 0)
    def _():
        z_scr[...] = jnp.zeros(z_scr.shape, F32)
        for h in range(N_HEADS):
            i, blk = head_block(h)
            z_scr[i, blk, blk] = s0_ref[0, h]
        prev_scr[...] = sh0_ref[0]

    p = p_ref[0]
    row = lax.broadcasted_iota(I32, p.shape, 0)
    prev = jnp.where(row == 0, prev_scr[...], pltpu.roll(p, 1, 0))
    last = p[n_rows - 1:n_rows, :]
    prev_scr[...] = last
    sht_ref[0] = last

    pm = p + (prev - p) * mu_ref[...]
    r = pm[:, 0:WIDTH]
    k = pm[:, WIDTH:2 * WIDTH]
    v = pm[:, 2 * WIDTH:3 * WIDTH]
    o1 = 3 * WIDTH
    wd = pm[:, o1:o1 + DECAY_LORA]
    ad = pm[:, o1 + DECAY_LORA:o1 + DECAY_LORA + ICLR_LORA]
    gd = pm[:, o1 + DECAY_LORA + ICLR_LORA:RWKV_COLS]

    w = -_softplus(-(w0_ref[...] + _dot(jnp.tanh(wd), wb_ref[...]))) - 0.5
    lw = -jnp.exp(w)
    a = _sigmoid(a0_ref[...] + _dot(ad, ab_ref[...]))
    g = _dot(_sigmoid(gd), gb_ref[...])
    kk = k * kk_ref[...]
    kk = kk / jnp.maximum(jnp.sqrt(_gsum(kk * kk, gm_ref)), L2_EPS)
    kf = k * (1.0 + (a - 1.0) * ka_ref[...])

    ri = lax.broadcasted_iota(I32, (n_rows, n_rows), 0)
    ci = lax.broadcasted_iota(I32, (n_rows, n_rows), 1)
    same_chunk = (ri // c) == (ci // c)
    cum = _dot_exact_rhs(jnp.logical_and(same_chunk, ri >= ci).astype(F32), lw)
    cum_last = _dot_exact_rhs(same_chunk.astype(F32), lw)
    r_t = r * jnp.exp(cum)
    a_t = -kk * jnp.exp(cum - lw)
    inv = jnp.exp(-cum)
    b_t = kk * a * inv
    k_t = kf * inv
    to_end = jnp.exp(cum_last - cum)
    b_e = kk * a * to_end
    k_e = kf * to_end
    g_end = jnp.exp(cum_last)

    hg = HEADS_PER_GROUP
    gw = hg * HEAD_DIM
    log_c = int(math.log2(c))
    t_idx = lax.broadcasted_iota(I32, (c, hg * c), 0)
    s_idx = lax.broadcasted_iota(I32, (c, hg * c), 1) & (c - 1)
    strict = s_idx < t_idx
    lower = s_idx <= t_idx
    eye = (s_idx == t_idx).astype(F32)
    rb = lax.broadcasted_iota(I32, (hg * c, gw), 0) >> log_c
    mask_kv = rb == (lax.broadcasted_iota(I32, (hg * c, gw), 1) >> int(math.log2(HEAD_DIM)))
    rs = lax.broadcasted_iota(I32, (hg * c, hg * c), 0) >> log_c
    mask_ss = rs == (lax.broadcasted_iota(I32, (hg * c, hg * c), 1) >> log_c)
    ng = N_HEADS // hg
    cat = functools.partial(jnp.concatenate, axis=0)
    units = [(slice(j * c, (j + 1) * c), slice(i * gw, (i + 1) * gw)) for j in range(n_chunks) for i in range(ng)]
    nu = len(units)

    def mm1(a, b_bd):
        return jnp.dot(a.astype(BF16), b_bd, preferred_element_type=F32)

    def dg(a, b, dims):
        return lax.dot_general(a.astype(BF16), b.astype(BF16), dims, preferred_element_type=F32)

    def bd1(x, mask):
        tiled = jnp.concatenate([x.astype(BF16)] * hg, axis=0)
        return jnp.where(mask, tiled, jnp.zeros_like(tiled))

    ar = [cat([a_t[rs_, s], r_t[rs_, s]]).astype(BF16) for rs_, s in units]
    ab = [dg(ar[n], bd1(b_t[units[n]], mask_kv), NT) for n in range(nu)]
    ak = [dg(ar[n], bd1(k_t[units[n]], mask_kv), NT) for n in range(nu)]
    l_ab = [jnp.where(strict, m[:c], 0.0) for m in ab]
    l_rb = [jnp.where(lower, m[c:], 0.0) for m in ab]
    l_ak = [jnp.where(strict, m[:c], 0.0) for m in ak]
    l_rk = [jnp.where(lower, m[c:], 0.0) for m in ak]
    tinv = [eye + m for m in l_ab]
    pw = [mm1(m, bd1(m, mask_ss)) for m in l_ab]
    for _ in range(1, log_c - 1):
        res = [mm1(cat([tinv[n], pw[n]]), bd1(pw[n], mask_ss)) for n in range(nu)]
        tinv = [tinv[n] + res[n][:c] for n in range(nu)]
        pw = [m[c:] for m in res]
    tinv = [tinv[n] + mm1(tinv[n], bd1(pw[n], mask_ss)) for n in range(nu)]
    av = [mm1(cat([l_ak[n], l_rk[n]]), bd1(v[units[n]], mask_kv)) for n in range(nu)]
    ue = [cat([b_e[units[n]], k_e[units[n]]]).astype(BF16) for n in range(nu)]

    def wide(fn, x):
        return [fn(x[:, :gw]), fn(x[:, gw:])]

    def bd1w(x):
        return jnp.concatenate(wide(lambda h_: bd1(h_, mask_kv), x), axis=1)

    rhs = [jnp.concatenate([a_t[units[n]], av[n][:c]], axis=1) for n in range(nu)]
    x0 = [mm1(tinv[n], bd1w(rhs[n])) for n in range(nu)]
    resid = [rhs[n] - (x0[n] - mm1(l_ab[n], bd1w(x0[n]))) for n in range(nu)]
    sol = [x0[n] + mm1(tinv[n], bd1w(resid[n])) for n in range(nu)]
    lift = [mm1(l_rb[n], bd1w(sol[n])) for n in range(nu)]
    lhs_s = [cat([sol[n][:, :gw], r_t[units[n]] + lift[n][:, :gw]]).astype(BF16) for n in range(nu)]
    u_loc = [sol[n][:, gw:] for n in range(nu)]
    o_loc = [av[n][c:] + lift[n][:, gw:] for n in range(nu)]

    zr = lax.broadcasted_iota(I32, (gw, gw), 0) >> int(math.log2(HEAD_DIM))
    zmask = zr == (lax.broadcasted_iota(I32, (gw, gw), 1) >> int(math.log2(HEAD_DIM)))
    z = [z_scr[i] for i in range(ng)]
    o_rows = []
    for j in range(n_chunks):
        o_grp = []
        for i in range(ng):
            n = j * ng + i
            rs_, s = units[n]
            sz = dg(lhs_s[n], z[i], NT)
            u = sz[:c] + u_loc[n]
            o_grp.append(sz[c:] + o_loc[n])
            upd = dg(cat([u, v[rs_, s]]), ue[n], TN)
            z[i] = z[i] * g_end[j * c:j * c + 1, s] + jnp.where(zmask, upd, 0.0)
        o_rows.append(jnp.concatenate(o_grp, axis=1))
    for i in range(ng):
        z_scr[i] = z[i]

    o = cat(o_rows)
    inv_hd = 1.0 / HEAD_DIM
    dlt = o - _gsum(o, gm_ref) * inv_hd
    var = _gsum(dlt * dlt, gm_ref) * inv_hd
    on = dlt * lax.rsqrt(var + RWKV_GN_EPS) * lnw_ref[...] + lnb_ref[...]
    bonus = _gsum(r * kf * rk_ref[...], gm_ref) * v
    y_ref[0] = (on + bonus) * g

    @pl.when(t == nt - 1)
    def _():
        for h in range(N_HEADS):
            i, blk = head_block(h)
            st_ref[0, h] = z_scr[i, blk, blk]


def _rwkv(p, shift0, s0, prm, gmat, chunk, chunks_per_step):
    b, t, _ = p.shape
    row = lambda n: _const_spec((1, n))
    rows = chunk * chunks_per_step
    return pl.pallas_call(
        functools.partial(_rwkv_kernel, c=chunk),
        grid=(b, t // rows),
        in_specs=[pl.BlockSpec((1, rows, RWKV_COLS), lambda i, j: (i, j, 0)),
                  pl.BlockSpec((1, 1, RWKV_COLS), lambda i, j: (i, 0, 0)),
                  pl.BlockSpec((1, N_HEADS, HEAD_DIM, HEAD_DIM), lambda i, j: (i, 0, 0, 0)),
                  row(RWKV_COLS), row(WIDTH), _const_spec((DECAY_LORA, WIDTH)), row(WIDTH),
                  _const_spec((ICLR_LORA, WIDTH)), _const_spec((GATE_LORA, WIDTH)),
                  row(WIDTH), row(WIDTH), row(WIDTH), row(WIDTH), row(WIDTH), _const_spec((WIDTH, WIDTH))],
        out_specs=[pl.BlockSpec((1, rows, WIDTH), lambda i, j: (i, j, 0)),
                   pl.BlockSpec((1, N_HEADS, HEAD_DIM, HEAD_DIM), lambda i, j: (i, 0, 0, 0)),
                   pl.BlockSpec((1, 1, RWKV_COLS), lambda i, j: (i, 0, 0))],
        out_shape=[jax.ShapeDtypeStruct((b, t, WIDTH), F32),
                   jax.ShapeDtypeStruct((b, N_HEADS, HEAD_DIM, HEAD_DIM), F32),
                   jax.ShapeDtypeStruct((b, 1, RWKV_COLS), F32)],
        scratch_shapes=[pltpu.VMEM((N_HEADS // HEADS_PER_GROUP, HEADS_PER_GROUP * HEAD_DIM,
                                    HEADS_PER_GROUP * HEAD_DIM), F32),
                        pltpu.VMEM((1, RWKV_COLS), F32)],
        compiler_params=_cparams(("parallel", "arbitrary")),
        name="rwkv7_mix",
    )(p, shift0, s0, prm["mu"], prm["w0"], prm["wb"], prm["a0"], prm["ab"], prm["gb"], prm["kk"], prm["ka"],
      prm["rk"], prm["lnw"], prm["lnb"], gmat)


def _merge_kernel(x_ref, ya_ref, yb_ref, gate_ref, mod_ref, g2_ref, woa_ref, wob_ref, wo_ref, wrh_ref, wrl_ref,
                  *rest):
    x1_ref, h2_ref, lg_ref = rest[-3:]
    bb, tt, d = x_ref.shape
    m = bb * tt
    gate = gate_ref[...].reshape(m, GATE_COLS)
    merged = (gate[:, 0:d] * _dot(ya_ref[...].reshape(m, WIDTH), woa_ref[...])
              + gate[:, d:2 * d] * _dot(yb_ref[...].reshape(m, WIDTH), wob_ref[...]))
    x1 = x_ref[...] + mod_ref[:, 2:3, :] * _dot(merged, wo_ref[...]).reshape(bb, tt, d)
    x1_ref[...] = x1
    ms = jnp.mean(x1 * x1, axis=-1, keepdims=True)
    h2 = x1 * lax.rsqrt(ms + RMS_EPS) * g2_ref[...]
    h2 = (h2 * (1.0 + mod_ref[:, 4:5, :]) + mod_ref[:, 3:4, :]).reshape(m, d)
    h2_ref[...] = _pack_bf16_pairs(h2)
    lg_ref[...] = _mm3((wrh_ref[...], wrl_ref[...]), _split2(h2), NT)


def _merge(x, ya, yb, gate, mod, w, n_total, row_offset, shared=None):
    b, t, d = x.shape
    bb, tt = _token_blocks(b, t, MOE_TOKEN_ROWS)
    nt = t // tt
    m = bb * tt
    off = row_offset // m

    def tok(cols):
        return pl.BlockSpec((bb, tt, cols), lambda i, j: (i, j, 0))

    in_specs = [tok(d), tok(WIDTH), tok(WIDTH), tok(GATE_COLS),
                pl.BlockSpec((bb, 6, d), lambda i, j: (i, 0, 0)),
                _const_spec((1, d)), _const_spec((WIDTH, d)), _const_spec((WIDTH, d)), _const_spec((d, d)),
                _const_spec((N_EXPERTS, d)), _const_spec((N_EXPERTS, d))]
    args = [x, ya, yb, gate, mod.reshape(b, 6, d), w["g2"], w["w_oa"], w["w_ob"], w["w_o"], w["wr_hi"], w["wr_lo"]]
    aliases = {}
    if shared is not None:
        aliases = {len(args): 1, len(args) + 1: 2}
        in_specs += [pl.BlockSpec(memory_space=pl.ANY), pl.BlockSpec(memory_space=pl.ANY)]
        args += list(shared)
    return pl.pallas_call(
        _merge_kernel,
        grid=(b // bb, nt),
        in_specs=in_specs,
        out_specs=[tok(d), pl.BlockSpec((m, d // 2), lambda i, j: (off + i * nt + j, 0)),
                   pl.BlockSpec((N_EXPERTS, m), lambda i, j: (0, off + i * nt + j))],
        out_shape=[jax.ShapeDtypeStruct((b, t, d), F32), jax.ShapeDtypeStruct((n_total, d // 2), I32),
                   jax.ShapeDtypeStruct((N_EXPERTS, n_total), F32)],
        input_output_aliases=aliases,
        compiler_params=_cparams(("parallel", "parallel")),
        name="merge_norm2_router",
    )(*args)


def _route_kernel(lg_ref, bias_ref, idx_ref, wt_ref, rank_ref, cnt_ref, carry):
    @pl.when(pl.program_id(0) == 0)
    def _():
        carry[...] = jnp.zeros(carry.shape, F32)

    tm = lg_ref.shape[1]
    scores = _sigmoid(lg_ref[...])
    sel = scores + bias_ref[...]
    row = lax.broadcasted_iota(I32, (N_EXPERTS, tm), 0)
    neg_inf = -jnp.inf

    def first_argmax(vals, rows):
        mx = jnp.max(vals, axis=0, keepdims=True)
        return mx, jnp.min(jnp.where(vals == mx, rows, N_EXPERTS), axis=0, keepdims=True)

    gslices = [slice(g * EXPERTS_PER_GROUP, (g + 1) * EXPERTS_PER_GROUP) for g in range(N_GROUPS)]
    gs = []
    row_g = lax.broadcasted_iota(I32, (EXPERTS_PER_GROUP, tm), 0)
    for sl in gslices:
        m1, i1 = first_argmax(sel[sl], row_g)
        m2 = jnp.max(jnp.where(row_g == i1, neg_inf, sel[sl]), axis=0, keepdims=True)
        gs.append(m1 + m2)
    kept = []
    for g in range(N_GROUPS):
        beaten = jnp.zeros((1, tm), I32)
        for o in range(N_GROUPS):
            if o != g:
                wins = (gs[o] >= gs[g]) if o < g else (gs[o] > gs[g])
                beaten = beaten + wins.astype(I32)
        kept.append(jnp.where(beaten < TOPK_GROUPS, sel[gslices[g]], neg_inf))
    cur = jnp.concatenate(kept, axis=0)

    idxs, ws = [], []
    picked = jnp.zeros((N_EXPERTS, tm), F32)
    for _ in range(TOP_K):
        _, ik = first_argmax(cur, row)
        hit = row == ik
        idxs.append(ik)
        ws.append(jnp.sum(jnp.where(hit, scores, 0.0), axis=0, keepdims=True))
        cur = jnp.where(hit, neg_inf, cur)
        picked = jnp.where(hit, 1.0, picked)
    wsum = ws[0]
    for k in range(1, TOP_K):
        wsum = wsum + ws[k]

    r = lax.broadcasted_iota(I32, (tm, tm), 0)
    c = lax.broadcasted_iota(I32, (tm, tm), 1)
    before = jnp.dot(picked.astype(BF16), (r < c).astype(BF16), preferred_element_type=F32) + carry[...]
    carry[...] = carry[...] + jnp.sum(picked, axis=1, keepdims=True)
    cnt_ref[...] = carry[...]

    kk = lax.broadcasted_iota(I32, (TOP_K, tm), 0)
    idx_o = jnp.zeros((TOP_K, tm), I32)
    wt_o = jnp.zeros((TOP_K, tm), F32)
    rank_o = jnp.zeros((TOP_K, tm), F32)
    for k in range(TOP_K):
        rk = jnp.sum(jnp.where(row == idxs[k], before, 0.0), axis=0, keepdims=True)
        idx_o = jnp.where(kk == k, idxs[k], idx_o)
        wt_o = jnp.where(kk == k, ws[k] / wsum * ROUTED_SCALE, wt_o)
        rank_o = jnp.where(kk == k, rk, rank_o)
    idx_ref[...] = idx_o
    wt_ref[...] = wt_o
    rank_ref[...] = rank_o.astype(I32)


def _route(logits_t, bias_col, tm):
    n = logits_t.shape[1]
    tokk = pl.BlockSpec((TOP_K, tm), lambda i: (0, i))
    return pl.pallas_call(
        _route_kernel,
        grid=(n // tm,),
        in_specs=[pl.BlockSpec((N_EXPERTS, tm), lambda i: (0, i)), _const_spec((N_EXPERTS, 1))],
        out_specs=[tokk, tokk, tokk, _const_spec((N_EXPERTS, 1))],
        out_shape=[jax.ShapeDtypeStruct((TOP_K, n), I32), jax.ShapeDtypeStruct((TOP_K, n), F32),
                   jax.ShapeDtypeStruct((TOP_K, n), I32), jax.ShapeDtypeStruct((N_EXPERTS, 1), F32)],
        scratch_shapes=[pltpu.VMEM((N_EXPERTS, 1), F32)],
        compiler_params=_cparams(("arbitrary",)),
        name="route_topk",
    )(logits_t, bias_col)


def _plan_kernel(cnt_ref, start_ref, be_ref, valid_ref, nu_ref, *, blk):
    cnt = cnt_ref[...]
    padded = jnp.ceil(cnt * (1.0 / blk)) * blk
    e_r = lax.broadcasted_iota(I32, (N_EXPERTS, N_EXPERTS), 0)
    e_c = lax.broadcasted_iota(I32, (N_EXPERTS, N_EXPERTS), 1)
    incl = (e_r <= e_c).astype(BF16)
    ph, pm, plo = _split3(jnp.broadcast_to(padded, (8, N_EXPERTS)))
    d2 = functools.partial(jnp.dot, preferred_element_type=F32)
    pad_end = (d2(ph, incl) + (d2(pm, incl) + d2(plo, incl)))[0:1, :]
    pad_start = pad_end - padded
    start_ref[...] = pad_start.astype(I32)
    total = jnp.max(pad_end, axis=-1, keepdims=True)
    nu_ref[...] = jnp.broadcast_to(total * (1.0 / blk), (1, N_EXPERTS)).astype(I32)
    nb = be_ref.shape[0]
    first = (lax.broadcasted_iota(I32, (nb, N_EXPERTS), 0) * blk).astype(F32)
    lane = lax.broadcasted_iota(I32, (nb, N_EXPERTS), 1)
    inside = jnp.logical_and(pad_start <= first, first < pad_end)
    be_ref[...] = jnp.sum(jnp.where(inside, lane, 0), axis=-1, keepdims=True)
    rows = jnp.minimum(pad_start + cnt - first, float(blk))
    valid_ref[...] = jnp.sum(jnp.where(inside, rows, 0.0), axis=-1, keepdims=True).astype(I32)


def _plan(counts, n_blocks, blk):
    return pl.pallas_call(
        functools.partial(_plan_kernel, blk=blk),
        out_shape=[jax.ShapeDtypeStruct((1, N_EXPERTS), I32), jax.ShapeDtypeStruct((n_blocks, 1), I32),
                   jax.ShapeDtypeStruct((n_blocks, 1), I32), jax.ShapeDtypeStruct((1, N_EXPERTS), I32)],
        compiler_params=pltpu.CompilerParams(vmem_limit_bytes=VMEM_LIMIT),
        name="dispatch_plan",
    )(counts)


def _dest_kernel(idx_ref, rank_ref, start_ref, dest_ref):
    tm = idx_ref.shape[1]
    row = lax.broadcasted_iota(I32, (N_EXPERTS, tm), 0)
    kk = lax.broadcasted_iota(I32, (TOP_K, tm), 0)
    idx = idx_ref[...]
    base = jnp.zeros((TOP_K, tm), I32)
    for k in range(TOP_K):
        bk = jnp.sum(jnp.where(row == idx[k:k + 1, :], start_ref[...], 0), axis=0, keepdims=True)
        base = jnp.where(kk == k, bk, base)
    dest_ref[...] = base + rank_ref[...]


def _dest(idx, rank, pad_start_col, tm):
    n = idx.shape[1]
    tokk = pl.BlockSpec((TOP_K, tm), lambda i: (0, i))
    return pl.pallas_call(
        _dest_kernel,
        grid=(n // tm,),
        in_specs=[tokk, tokk, _const_spec((N_EXPERTS, 1))],
        out_specs=tokk,
        out_shape=jax.ShapeDtypeStruct((TOP_K, n), I32),
        compiler_params=_cparams(("parallel",)),
        name="dispatch_dest",
    )(idx, rank, pad_start_col)


def _dispatch(h2p, dest_t, n_slots):
    n, wp = h2p.shape
    half = wp // SC_ROW_SPLIT
    window = SC_SCATTER_WINDOW
    mesh = plsc.VectorSubcoreMesh(core_axis_name="core", subcore_axis_name="subcore")
    out = jax.ShapeDtypeStruct((n_slots, half), h2p.dtype)

    @functools.partial(pl.kernel, out_type=[out] * SC_ROW_SPLIT, mesh=mesh, scratch_types=[])
    def scatter(rows_hbm, idx_hbm, *outs):
        for c, out_hbm in enumerate(outs):
            def body(rows_vmem, idx_vmem, out_hbm=out_hbm):
                for k in range(TOP_K):
                    pltpu.sync_copy(rows_vmem, out_hbm.at[idx_vmem.at[k]])

            pltpu.emit_pipeline(
                body, grid=(n // window,),
                in_specs=[pl.BlockSpec((window, half), index_map=lambda i, c=c: (i, c)),
                          pl.BlockSpec((TOP_K, window), index_map=lambda i: (0, i))],
                out_specs=[], core_axis_name=("core", "subcore"), dimension_semantics=(pltpu.PARALLEL,),
            )(rows_hbm, idx_hbm)

    return scatter(h2p, dest_t)


def _expert_kernel(be_ref, valid_ref, nu_ref, xa_ref, xb_ref, wg_ref, wu_ref, wd_ref, *rest):
    y_refs, (wg_b, wu_b, wd_b) = rest[:SC_ROW_SPLIT], rest[SC_ROW_SPLIT:]
    i = pl.program_id(0)
    nv = valid_ref[i]
    new_expert = jnp.logical_or(i == 0, be_ref[i] != be_ref[jnp.maximum(i - 1, 0)])

    @pl.when(jnp.logical_and(nv > 0, new_expert))
    def _():
        wg_b[...] = wg_ref[0].astype(BF16)
        wu_b[...] = wu_ref[0].astype(BF16)
        wd_b[...] = wd_ref[0].astype(BF16)

    @pl.when(nv > 0)
    def _():
        blk = xa_ref.shape[0]
        rows = lax.broadcasted_iota(I32, (blk, 1), 0)
        packed = jnp.concatenate([xa_ref[...], xb_ref[...]], axis=1)
        x = _unpack_bf16_pairs(jnp.where(rows < nv, packed, 0))
        hg = jnp.dot(x, wg_b[...], preferred_element_type=F32)
        hu = jnp.dot(x, wu_b[...], preferred_element_type=F32)
        y = _pack_bf16_pairs(jnp.dot((_silu(hg) * hu).astype(BF16), wd_b[...], preferred_element_type=F32))
        cw = y.shape[1] // SC_ROW_SPLIT
        for c, y_ref in enumerate(y_refs):
            y_ref[...] = y[:, c * cw:(c + 1) * cw]


def _experts(xs, block_e, valid, n_used, w_eg, w_eu, w_ed, blk):
    xa, xb = xs
    n_slots, packed = xa.shape
    d = w_eg.shape[1]
    n_blocks = n_slots // blk

    def row_blk(i, be, valid, nu):
        return (jnp.minimum(i, nu[0] - 1), 0)

    def w_blk(i, be, valid, nu):
        return (be[i], 0, 0)

    return pl.pallas_call(
        _expert_kernel,
        grid_spec=pltpu.PrefetchScalarGridSpec(
            num_scalar_prefetch=3,
            grid=(n_blocks,),
            in_specs=[pl.BlockSpec((blk, packed), row_blk), pl.BlockSpec((blk, packed), row_blk),
                      pl.BlockSpec((1, d, D_EXPERT), w_blk), pl.BlockSpec((1, d, D_EXPERT), w_blk),
                      pl.BlockSpec((1, D_EXPERT, d), w_blk)],
            out_specs=[pl.BlockSpec((blk, packed), row_blk)] * SC_ROW_SPLIT,
            scratch_shapes=[pltpu.VMEM((d, D_EXPERT), BF16), pltpu.VMEM((d, D_EXPERT), BF16),
                            pltpu.VMEM((D_EXPERT, d), BF16)]),
        out_shape=[jax.ShapeDtypeStruct((n_slots, packed), I32)] * SC_ROW_SPLIT,
        compiler_params=_cparams(("arbitrary",)),
        name="moe_experts",
    )(block_e, valid, n_used, xa, xb, w_eg, w_eu, w_ed)


def _combine_gather(ys, dest_t):
    k, n = dest_t.shape
    cw = ys[0].shape[1]
    window = SC_SCATTER_WINDOW
    mesh = plsc.VectorSubcoreMesh(core_axis_name="core", subcore_axis_name="subcore")

    @functools.partial(pl.kernel, out_type=jax.ShapeDtypeStruct((k * n, cw * len(ys)), ys[0].dtype), mesh=mesh,
                       scratch_types=[])
    def gather(*refs):
        y_refs, idx_hbm, out_hbm = refs[:len(ys)], refs[len(ys)], refs[len(ys) + 1]
        for c, y_hbm in enumerate(y_refs):
            def body(idx_vmem, out_vmem, y_hbm=y_hbm):
                pltpu.sync_copy(y_hbm.at[idx_vmem.at[0]], out_vmem)

            pltpu.emit_pipeline(
                body, grid=(k * n // window,),
                in_specs=[pl.BlockSpec((1, window), index_map=lambda i: (0, i))],
                out_specs=[pl.BlockSpec((window, cw), index_map=lambda i, c=c: (i, c))],
                core_axis_name=("core", "subcore"), dimension_semantics=(pltpu.PARALLEL,),
            )(idx_hbm, out_hbm)

    return gather(*ys, dest_t.reshape(1, k * n)).reshape(k, n, cw * len(ys))


def _final_kernel(x1_ref, h2_ref, wt_ref, mod_ref, wsg_ref, wsu_ref, wsd_ref, yg_ref, o_ref):
    bb, tt, d = x1_ref.shape
    hb = _unpack_bf16_pairs(h2_ref[...])
    hg = jnp.dot(hb, wsg_ref[...], preferred_element_type=F32)
    hu = jnp.dot(hb, wsu_ref[...], preferred_element_type=F32)
    ffn = _dot(_silu(hg) * hu, wsd_ref[...])
    wt = wt_ref[...]
    for k in range(TOP_K):
        ffn = ffn + wt[:, k:k + 1] * _unpack_bf16_pairs(yg_ref[k]).astype(F32)
    o_ref[...] = x1_ref[...] + mod_ref[:, 5:6, :] * ffn.reshape(bb, tt, d)


def _final(x1, h2_all, wts_all, y_tok, mod, w, row_offset):
    b, t, d = x1.shape
    bb, tt = _token_blocks(b, t, MOE_TOKEN_ROWS)
    nt = t // tt
    m = bb * tt
    off = row_offset // m

    def flat_idx(i, j):
        return off + i * nt + j

    return pl.pallas_call(
        _final_kernel,
        grid=(b // bb, nt),
        in_specs=[pl.BlockSpec((bb, tt, d), lambda i, j: (i, j, 0)),
                  pl.BlockSpec((m, h2_all.shape[1]), lambda i, j: (flat_idx(i, j), 0)),
                  pl.BlockSpec((m, TOP_K), lambda i, j: (flat_idx(i, j), 0)),
                  pl.BlockSpec((bb, 6, d), lambda i, j: (i, 0, 0)),
                  _const_spec((d, D_EXPERT)), _const_spec((d, D_EXPERT)), _const_spec((D_EXPERT, d)),
                  pl.BlockSpec((TOP_K, m, y_tok.shape[2]), lambda i, j: (0, flat_idx(i, j), 0))],
        out_specs=pl.BlockSpec((bb, tt, d), lambda i, j: (i, j, 0)),
        out_shape=jax.ShapeDtypeStruct((b, t, d), F32),
        compiler_params=_cparams(("parallel", "parallel")),
        name="moe_combine_final",
    )(x1, h2_all, wts_all, mod.reshape(b, 6, d), w["w_sg"], w["w_su"], w["w_sd"], y_tok)


def _largest_tile(n, candidates):
    return next(c for c in candidates if n % c == 0)


def _moe_routed(h2_all, logits_all, w, blk=EXPERT_BLOCK_ROWS):
    n = h2_all.shape[0]
    n_blocks = (n * TOP_K + N_EXPERTS * (blk - 1)) // blk + 1
    n_blocks = (n_blocks + 7) // 8 * 8
    idx, wts_t, rank, counts = _route(logits_all, w["router_bias"], _largest_tile(n, (512, 256)))
    pad_start, block_e, valid, n_used = _plan(counts.reshape(1, N_EXPERTS), n_blocks, blk)
    block_e = block_e.reshape(n_blocks)
    valid = valid.reshape(n_blocks)
    n_used = n_used[0, 0:1]
    dest_t = _dest(idx, rank, pad_start.reshape(N_EXPERTS, 1), _largest_tile(n, (1024, 512, 256)))
    xs = _dispatch(h2_all, dest_t, n_blocks * blk)
    ys = _experts(xs, block_e, valid, n_used, w["w_eg"], w["w_eu"], w["w_ed"], blk)
    return _combine_gather(ys, dest_t), jnp.transpose(wts_t)


def _prep(raw):
    p = {k: v[0] for k, v in raw.items()}
    w_in = p["w_in"]
    o_fox = RWKV_COLS
    o_fl = o_fox + FOX_MAIN_COLS
    o_gate = o_fl + N_HEADS
    row = lambda a: a.reshape(1, -1)
    return dict(
        w_ada=p["w_ada"], b_ada=p["b_ada"],
        g1=row(p["norm1_g"]), g2=row(p["norm2_g"]),
        wr=w_in[:, :o_fox].astype(BF16),
        wf=w_in[:, o_fox:o_fl].astype(BF16),
        wfl=jnp.pad(w_in[:, o_fl:o_gate], ((0, 0), (0, LANES - N_HEADS))).astype(BF16),
        wg=w_in[:, o_gate:].astype(BF16),
        qn=row(jnp.tile(p["fox_q_norm"], N_HEADS)), kn=row(jnp.tile(p["fox_k_norm"], N_HEADS)),
        fb=row(p["fox_f_bias"]),
        gmat=_group_ones(),
        rwkv=dict(mu=row(p["rwkv_mu"]), w0=row(p["rwkv_w0"]), wb=p["rwkv_w_lora_b"], a0=row(p["rwkv_a0"]),
                  ab=p["rwkv_a_lora_b"], gb=p["rwkv_g_lora_b"], kk=row(p["rwkv_k_k"]), ka=row(p["rwkv_k_a"]),
                  rk=row(p["rwkv_r_k"]), lnw=row(p["rwkv_ln_w"]), lnb=row(p["rwkv_ln_b"])),
        w_oa=p["w_out_rwkv"].astype(BF16), w_ob=p["w_out_fox"].astype(BF16), w_o=p["w_out"].astype(BF16),
        wr_hi=p["w_router"].T.astype(BF16),
        wr_lo=(p["w_router"] - p["w_router"].astype(BF16).astype(F32)).T.astype(BF16),
        router_bias=p["router_bias"].reshape(N_EXPERTS, 1),
        w_eg=p["w_exp_gate"], w_eu=p["w_exp_up"], w_ed=p["w_exp_down"],
        w_sg=p["w_sh_gate"].astype(BF16), w_su=p["w_sh_up"].astype(BF16), w_sd=p["w_sh_down"].astype(BF16),
    )


def _token_blocks(b, t, rows=256):
    if t >= rows:
        return 1, rows
    bb = max(1, min(b, 256 // t))
    while b % bb:
        bb -= 1
    return bb, t


def _mix_path(x, mod, shift0, wkv0, past_k, past_v, past_logf, w):
    b, t, d = x.shape
    bb, tt = _token_blocks(b, t)
    n_past = past_k.shape[1]
    if n_past:
        f_past = _past_cumsum(past_logf)
        init = f_past[:, n_past - 1:n_past, :]
        past = (past_k, past_v, jnp.swapaxes(f_past, 1, 2))
    else:
        init = jnp.zeros((b, 1, N_HEADS), F32)
        past = None
    pr, q, k, v, sg, logf, f_new, gate = _inproj(x, mod.reshape(b, 6, d), w["g1"], w["wr"], w["wf"], w["wfl"],
                                                 w["wg"], w["qn"], w["kn"], w["fb"], w["gmat"], init, bb, tt)
    y_fox = _fox_attention(q, f_new, sg, k, v, jnp.swapaxes(f_new, 1, 2), past=past, tq=min(t, 512),
                           tk_past=min(max(n_past, 1), 512))
    chunk = min(t, RWKV_CHUNK)
    y_rwkv, wkv_new, shift_new = _rwkv(pr, shift0.reshape(b, 1, RWKV_COLS), wkv0, w["rwkv"], w["gmat"],
                                       chunk, max(1, min(RWKV_CHUNKS_PER_STEP, t // chunk)))
    return y_rwkv, y_fox, gate, wkv_new, shift_new, k, v, logf


def _layer(paths, w):
    n_b = [p[0].shape[0] for p in paths]
    mod_all = _ada(jnp.concatenate([p[1] for p in paths], axis=0), w["w_ada"], w["b_ada"])
    mods, o = [], 0
    for nb in n_b:
        mods.append(mod_all[o:o + nb])
        o += nb
    n_total = sum(p[0].shape[0] * p[0].shape[1] for p in paths)
    mixed, x1s = [], []
    shared, row = None, 0
    for (x, _, shift0, wkv0, pk, pv, plf), mod in zip(paths, mods):
        ya, yb, gate, wkv_new, shift_new, k, v, logf = _mix_path(x, mod, shift0, wkv0, pk, pv, plf, w)
        x1, h2_all, lg_all = _merge(x, ya, yb, gate, mod, w, n_total, row, shared)
        shared = (h2_all, lg_all)
        row += x.shape[0] * x.shape[1]
        mixed.append((wkv_new, shift_new, k, v, logf))
        x1s.append(x1)
    y_tok, wts = _moe_routed(h2_all, lg_all, w)
    outs, row = [], 0
    for x1, mod, st in zip(x1s, mods, mixed):
        y = _final(x1, h2_all, wts, y_tok, mod, w, row)
        row += x1.shape[0] * x1.shape[1]
        outs.append((y,) + st)
    return outs


def kernel(x_prompt, x_sample, c_prompt, c_sample, state_rwkv_wkv, state_rwkv_shift, cache_fox_k, cache_fox_v,
           cache_fox_logf, w_ada, b_ada, norm1_g, norm2_g, w_in, rwkv_mu, rwkv_w0, rwkv_w_lora_b, rwkv_a0,
           rwkv_a_lora_b, rwkv_g_lora_b, rwkv_k_k, rwkv_k_a, rwkv_r_k, rwkv_ln_w, rwkv_ln_b, fox_q_norm,
           fox_k_norm, fox_f_bias, w_out_rwkv, w_out_fox, w_out, w_router, router_bias, w_exp_gate, w_exp_up,
           w_exp_down, w_sh_gate, w_sh_up, w_sh_down):
    raw = dict(w_ada=w_ada, b_ada=b_ada, norm1_g=norm1_g, norm2_g=norm2_g, w_in=w_in, rwkv_mu=rwkv_mu,
               rwkv_w0=rwkv_w0, rwkv_w_lora_b=rwkv_w_lora_b, rwkv_a0=rwkv_a0, rwkv_a_lora_b=rwkv_a_lora_b,
               rwkv_g_lora_b=rwkv_g_lora_b, rwkv_k_k=rwkv_k_k, rwkv_k_a=rwkv_k_a, rwkv_r_k=rwkv_r_k,
               rwkv_ln_w=rwkv_ln_w, rwkv_ln_b=rwkv_ln_b, fox_q_norm=fox_q_norm, fox_k_norm=fox_k_norm,
               fox_f_bias=fox_f_bias, w_out_rwkv=w_out_rwkv, w_out_fox=w_out_fox, w_out=w_out,
               w_router=w_router, router_bias=router_bias, w_exp_gate=w_exp_gate, w_exp_up=w_exp_up,
               w_exp_down=w_exp_down, w_sh_gate=w_sh_gate, w_sh_up=w_sh_up, w_sh_down=w_sh_down)
    assert w_in.shape[0] == 1, "single-layer stack"
    w = _prep(raw)
    bp, tp, _ = x_prompt.shape
    bs, ts, _ = x_sample.shape
    n_past = cache_fox_k.shape[2]
    prompt = (x_prompt, c_prompt, jnp.zeros((bp, RWKV_COLS), F32),
              jnp.zeros((bp, N_HEADS, HEAD_DIM, HEAD_DIM), F32),
              jnp.zeros((bp, 0, WIDTH), F32), jnp.zeros((bp, 0, WIDTH), F32), jnp.zeros((bp, 0, N_HEADS), F32))
    sample = (x_sample, c_sample, state_rwkv_shift[0], state_rwkv_wkv[0],
              cache_fox_k[0].reshape(bs, n_past, WIDTH), cache_fox_v[0].reshape(bs, n_past, WIDTH),
              cache_fox_logf[0])
    (yp, wkv_p, sh_p, k_p, v_p, lf_p), (ysm, wkv_s, sh_s, k_s, v_s, lf_s) = _layer([prompt, sample], w)

    def heads(a):
        return a.reshape((1,) + a.shape[:2] + (N_HEADS, HEAD_DIM))

    return (yp, ysm,
            wkv_p[None], sh_p.reshape(1, bp, RWKV_COLS), heads(k_p), heads(v_p), lf_p[None],
            wkv_s[None], sh_s.reshape(1, bs, RWKV_COLS), heads(k_s), heads(v_s), lf_s[None])
```

```python
import functools
import math

import jax
import jax.numpy as jnp
from jax import lax
from jax.experimental import pallas as pl
from jax.experimental.pallas import tpu as pltpu
from jax.experimental.pallas import tpu_sc as plsc

F32 = jnp.float32
BF16 = jnp.bfloat16
I32 = jnp.int32

D_MODEL = 1024
N_HEADS = 8
HEAD_DIM = 64
WIDTH = N_HEADS * HEAD_DIM
HEADS_PER_GROUP = 4
FOX_HEAD_GROUP = 4
FOX_GROUP_MIN_TILE = 256 * 256
RWKV_CHUNK = 64
RWKV_CHUNKS_PER_STEP = 4
MOE_TOKEN_ROWS = 512
EXPERT_BLOCK_ROWS = 512
SC_SCATTER_WINDOW = 128
SC_ROW_SPLIT = 2
DECAY_LORA = 64
ICLR_LORA = 64
GATE_LORA = 128
RWKV_COLS = 3 * WIDTH + DECAY_LORA + ICLR_LORA + GATE_LORA
FOX_MAIN_COLS = 4 * WIDTH
GATE_COLS = 2 * D_MODEL
RWKV_GN_EPS = HEAD_DIM * 1e-5
L2_EPS = 1e-12
RMS_EPS = 1e-6
N_EXPERTS = 256
TOP_K = 8
N_GROUPS = 8
TOPK_GROUPS = 4
EXPERTS_PER_GROUP = N_EXPERTS // N_GROUPS
D_EXPERT = 256
ROUTED_SCALE = 2.5

LANES = 128
VMEM_LIMIT = 56 * 1024 * 1024
NEG_BIG = -1e30

NN = (((1,), (0,)), ((), ()))
NT = (((1,), (1,)), ((), ()))
TN = (((0,), (0,)), ((), ()))


def _cparams(sem):
    return pltpu.CompilerParams(dimension_semantics=sem, vmem_limit_bytes=VMEM_LIMIT)


def _dot(a, b, dims=NN):
    return lax.dot_general(a.astype(BF16), b.astype(BF16), dims, preferred_element_type=F32)


def _split2(a):
    hi = a.astype(BF16)
    lo = (a - hi.astype(F32)).astype(BF16)
    return hi, lo


def _split3(a):
    hi = a.astype(BF16)
    r1 = a - hi.astype(F32)
    mid = r1.astype(BF16)
    lo = (r1 - mid.astype(F32)).astype(BF16)
    return hi, mid, lo


def _mm3(a, b, dims):
    d = functools.partial(lax.dot_general, dimension_numbers=dims, preferred_element_type=F32)
    return d(a[0], b[0]) + (d(a[0], b[1]) + d(a[1], b[0]))


def _dot_exact_rhs(a_exact, b, dims=NN):
    ab = a_exact.astype(BF16)
    bh, bm, bl = _split3(b)
    d = functools.partial(lax.dot_general, dimension_numbers=dims, preferred_element_type=F32)
    return d(ab, bh) + (d(ab, bm) + d(ab, bl))


def _gsum(x, g_ref):
    half = HEADS_PER_GROUP * HEAD_DIM
    hi, lo = _split2(x)
    g = g_ref[0:half, 0:half]
    d = functools.partial(jnp.dot, preferred_element_type=F32)
    return jnp.concatenate([d(hi[:, s], g) + d(lo[:, s], g) for s in (slice(0, half), slice(half, 2 * half))],
                           axis=1)


def _sigmoid(x):
    return 1.0 / (1.0 + jnp.exp(-x))


def _softplus(x):
    return jnp.maximum(x, 0.0) + jnp.log1p(jnp.exp(-jnp.abs(x)))


def _silu(x):
    return x * _sigmoid(x)


def _pack_bf16_pairs(x):
    w = x.shape[1] // 2
    bits = lax.bitcast_convert_type(x.astype(BF16).astype(F32), I32)
    return lax.shift_right_logical(bits[:, :w], 16) | (bits[:, w:] & -65536)


def _unpack_bf16_pairs(p):
    lo = lax.bitcast_convert_type(lax.shift_left(p, 16), F32)
    hi = lax.bitcast_convert_type(p & -65536, F32)
    return jnp.concatenate([lo, hi], axis=1).astype(BF16)


def _group_ones():
    h = jnp.arange(WIDTH, dtype=I32) // HEAD_DIM
    return (h[:, None] == h[None, :]).astype(BF16)


def _ada_kernel(c_ref, w_ref, b_ref, o_ref):
    o_ref[...] = _dot(_silu(c_ref[...]), w_ref[...]) + b_ref[...]


def _ada(c, w_ada, b_ada):
    nb = c.shape[0]
    n_out = w_ada.shape[1]
    blk = D_MODEL
    return pl.pallas_call(
        _ada_kernel,
        grid=(n_out // blk,),
        in_specs=[pl.BlockSpec((nb, D_MODEL), lambda j: (0, 0)),
                  pl.BlockSpec((D_MODEL, blk), lambda j: (0, j)),
                  pl.BlockSpec((1, blk), lambda j: (0, j))],
        out_specs=pl.BlockSpec((nb, blk), lambda j: (0, j)),
        out_shape=jax.ShapeDtypeStruct((nb, n_out), F32),
        compiler_params=_cparams(("parallel",)),
        name="ada_mod",
    )(c, w_ada, b_ada.reshape(1, n_out))


def _inproj_kernel(x_ref, mod_ref, g1_ref, wr_ref, wf_ref, wfl_ref, wg_ref, qn_ref, kn_ref, fb_ref, gm_ref, f0_ref,
                   pr_ref, q_ref, k_ref, v_ref, sg_ref, lf_ref, cf_ref, gate_ref, carry):
    bb, tt, d = x_ref.shape
    m = bb * tt
    x = x_ref[...]
    ms = jnp.mean(x * x, axis=-1, keepdims=True)
    h = x * lax.rsqrt(ms + RMS_EPS) * g1_ref[...]
    h = h * (1.0 + mod_ref[:, 1:2, :]) + mod_ref[:, 0:1, :]
    hb = h.reshape(m, d).astype(BF16)

    pr_ref[...] = jnp.dot(hb, wr_ref[...], preferred_element_type=F32).reshape(bb, tt, RWKV_COLS)

    f = jnp.dot(hb, wf_ref[...], preferred_element_type=F32)
    q = f[:, 0:WIDTH]
    k = f[:, WIDTH:2 * WIDTH]
    v = f[:, 2 * WIDTH:3 * WIDTH]
    og = f[:, 3 * WIDTH:4 * WIDTH]
    inv_hd = 1.0 / HEAD_DIM
    q = q * lax.rsqrt(_gsum(q * q, gm_ref) * inv_hd + RMS_EPS) * qn_ref[...]
    k = k * lax.rsqrt(_gsum(k * k, gm_ref) * inv_hd + RMS_EPS) * kn_ref[...]
    q_ref[...] = (q * (HEAD_DIM ** -0.5)).astype(BF16).reshape(bb, tt, WIDTH)
    k_ref[...] = k.reshape(bb, tt, WIDTH)
    v_ref[...] = v.reshape(bb, tt, WIDTH)
    sg_ref[...] = _sigmoid(og).reshape(bb, tt, WIDTH)

    fl = jnp.dot(hb, wfl_ref[...], preferred_element_type=F32)[:, 0:N_HEADS] + fb_ref[...]
    lf = -_softplus(-fl)
    lf_ref[...] = lf.reshape(bb, tt, N_HEADS)

    @pl.when(pl.program_id(1) == 0)
    def _():
        carry[...] = f0_ref[...]

    r = lax.broadcasted_iota(I32, (m, m), 0)
    c = lax.broadcasted_iota(I32, (m, m), 1)
    tri = jnp.logical_and(r // tt == c // tt, r >= c).astype(F32)
    cf = _dot_exact_rhs(tri, lf).reshape(bb, tt, N_HEADS) + carry[...]
    cf_ref[...] = cf
    carry[...] = cf[:, tt - 1:tt, :]

    gate_ref[...] =_sigmoid(jnp.dot(hb, wg_ref[...], preferred_element_type=F32)).reshape(bb, tt, GATE_COLS)


def _const_spec(shape):
    nd = len(shape)
    return pl.BlockSpec(shape, lambda *_: (0,) * nd)


def _inproj(x, mod, g1, wr, wf, wfl, wg, qn, kn, fb, gmat, f0, bb, tt):
    b, t, d = x.shape
    grid = (b // bb, t // tt)

    def tok(cols):
        return pl.BlockSpec((bb, tt, cols), lambda i, j: (i, j, 0))

    out_cols = [(RWKV_COLS, F32), (WIDTH, BF16), (WIDTH, F32), (WIDTH, F32), (WIDTH, F32), (N_HEADS, F32),
                (N_HEADS, F32), (GATE_COLS, F32)]
    return pl.pallas_call(
        _inproj_kernel,
        grid=grid,
        in_specs=[tok(d),
                  pl.BlockSpec((bb, 6, d), lambda i, j: (i, 0, 0)),
                  _const_spec((1, d)),
                  _const_spec(wr.shape), _const_spec(wf.shape), _const_spec(wfl.shape), _const_spec(wg.shape),
                  _const_spec((1, WIDTH)), _const_spec((1, WIDTH)), _const_spec((1, N_HEADS)),
                  _const_spec((WIDTH, WIDTH)),
                  pl.BlockSpec((bb, 1, N_HEADS), lambda i, j: (i, 0, 0))],
        out_specs=[tok(c) for c, _ in out_cols],
        out_shape=[jax.ShapeDtypeStruct((b, t, c), dt) for c, dt in out_cols],
        scratch_shapes=[pltpu.VMEM((bb, 1, N_HEADS), F32)],
        compiler_params=_cparams(("parallel", "arbitrary")),
        name="norm1_inproj",
    )(x, mod, g1, wr, wf, wfl, wg, qn, kn, fb, gmat, f0)


def _past_cumsum_kernel(x_ref, o_ref):
    x = x_ref[0]
    rows = x.shape[0]
    li = lax.broadcasted_iota(I32, (LANES, LANES), 0)
    lj = lax.broadcasted_iota(I32, (LANES, LANES), 1)
    same_head = (li % N_HEADS) == (lj % N_HEADS)
    within = jnp.logical_and(same_head, li // N_HEADS <= lj // N_HEADS).astype(BF16)
    xh, xm, xl = _split3(x)
    d2 = functools.partial(jnp.dot, preferred_element_type=F32)
    in_row = d2(xh, within) + (d2(xm, within) + d2(xl, within))
    sh = same_head.astype(BF16)
    row_tot = d2(xh, sh) + (d2(xm, sh) + d2(xl, sh))
    ri = lax.broadcasted_iota(I32, (rows, rows), 0)
    ci = lax.broadcasted_iota(I32, (rows, rows), 1)
    o_ref[0] = in_row + _dot_exact_rhs((ri > ci).astype(F32), row_tot)


def _past_cumsum(past_logf):
    b, p, h = past_logf.shape
    rows = p * h // LANES
    flat = past_logf.reshape(b, rows, LANES)
    out = pl.pallas_call(
        _past_cumsum_kernel,
        grid=(b,),
        in_specs=[pl.BlockSpec((1, rows, LANES), lambda i: (i, 0, 0))],
        out_specs=pl.BlockSpec((1, rows, LANES), lambda i: (i, 0, 0)),
        out_shape=jax.ShapeDtypeStruct((b, rows, LANES), F32),
        compiler_params=_cparams(("parallel",)),
        name="cache_logf_cumsum",
    )(flat)
    return out.reshape(b, p, h)


def _fox_kernel(*refs, n_past_blocks, tq):
    if n_past_blocks:
        (q_ref, fq_ref, sg_ref, kp_ref, vp_ref, fkp_ref, kn_ref, vn_ref, fkn_ref,
         o_ref, m_scr, l_scr, acc_scr) = refs
    else:
        q_ref, fq_ref, sg_ref, kn_ref, vn_ref, fkn_ref, o_ref, m_scr, l_scr, acc_scr = refs
    qi = pl.program_id(1)
    ki = pl.program_id(2)
    nk = pl.num_programs(2)

    @pl.when(ki == 0)
    def _():
        m_scr[...] = jnp.full(m_scr.shape, NEG_BIG, F32)
        l_scr[...] = jnp.zeros(l_scr.shape, F32)
        acc_scr[...] = jnp.zeros(acc_scr.shape, F32)

    lane_a = lax.broadcasted_iota(I32, (tq, LANES), 1) < HEAD_DIM

    def step(k_ref, v_ref, fk_ref, diag):
        tk = k_ref.shape[1]
        if diag:
            rq = lax.broadcasted_iota(I32, (tq, tk), 0)
            ck = lax.broadcasted_iota(I32, (tq, tk), 1)
            visible = ck <= rq
        fq_all = fq_ref[0]
        n_pairs = N_HEADS // 2
        cols = [slice(j * LANES, (j + 1) * LANES) for j in range(n_pairs)]
        group = FOX_HEAD_GROUP if tq * tk >= FOX_GROUP_MIN_TILE else N_HEADS
        for g0 in range(0, n_pairs, group // 2):
            pairs = range(g0, g0 + group // 2)
            scores = {}
            for j in pairs:
                qj = q_ref[0, :, cols[j]]
                kb = k_ref[0, :, cols[j]].astype(BF16)
                for hh in range(2):
                    h = 2 * j + hh
                    qm = jnp.where(lane_a if hh == 0 else jnp.logical_not(lane_a), qj, jnp.zeros_like(qj))
                    s = lax.dot_general(qm, kb, NT, preferred_element_type=F32)
                    s = s + fq_all[:, h:h + 1] - fk_ref[0, h:h + 1, :]
                    if diag:
                        s = jnp.where(visible, s, NEG_BIG)
                    scores[h] = s
            alphas, probs = {}, {}
            for h in scores:
                m_old = m_scr[h]
                m_new = jnp.maximum(m_old, jnp.max(scores[h], axis=-1, keepdims=True))
                alphas[h] = jnp.exp(m_old - m_new)
                p = jnp.exp(scores[h] - m_new)
                l_scr[h] = alphas[h] * l_scr[h] + jnp.sum(p, axis=-1, keepdims=True)
                m_scr[h] = m_new
                probs[h] = p.astype(BF16)
            for j in pairs:
                vb = v_ref[0, :, cols[j]].astype(BF16)
                pv0 = jnp.dot(probs[2 * j], vb, preferred_element_type=F32)
                pv1 = jnp.dot(probs[2 * j + 1], vb, preferred_element_type=F32)
                acc_scr[:, cols[j]] = (acc_scr[:, cols[j]] * jnp.where(lane_a, alphas[2 * j], alphas[2 * j + 1])
                                       + jnp.where(lane_a, pv0, pv1))

    if n_past_blocks:
        @pl.when(ki < n_past_blocks)
        def _():
            step(kp_ref, vp_ref, fkp_ref, False)

    kn = ki - n_past_blocks

    @pl.when(jnp.logical_and(kn >= 0, kn < qi))
    def _():
        step(kn_ref, vn_ref, fkn_ref, False)

    @pl.when(kn == qi)
    def _():
        step(kn_ref, vn_ref, fkn_ref, True)

    @pl.when(ki == nk - 1)
    def _():
        for j in range(N_HEADS // 2):
            cols = slice(j * LANES, (j + 1) * LANES)
            l = jnp.where(lane_a, l_scr[2 * j], l_scr[2 * j + 1])
            o_ref[0, :, cols] = acc_scr[:, cols] / l * sg_ref[0, :, cols]


def _fox_attention(q, fq, sg, k_new, v_new, fk_new_t, past=None, tq=512, tk_past=512):
    b, t, _ = q.shape
    nq = t // tq
    n_past_blocks = 0 if past is None else past[0].shape[1] // tk_past
    nk = n_past_blocks + nq

    def new_idx(i, qi, ki):
        return jnp.clip(ki - n_past_blocks, 0, qi)

    in_specs = [pl.BlockSpec((1, tq, WIDTH), lambda i, qi, ki: (i, qi, 0)),
                pl.BlockSpec((1, tq, N_HEADS), lambda i, qi, ki: (i, qi, 0)),
                pl.BlockSpec((1, tq, WIDTH), lambda i, qi, ki: (i, qi, 0))]
    args = [q, fq, sg]
    if n_past_blocks:
        def past_idx(i, qi, ki):
            return jnp.minimum(ki, n_past_blocks - 1)
        in_specs += [pl.BlockSpec((1, tk_past, WIDTH), lambda i, qi, ki: (i, past_idx(i, qi, ki), 0)),
                     pl.BlockSpec((1, tk_past, WIDTH), lambda i, qi, ki: (i, past_idx(i, qi, ki), 0)),
                     pl.BlockSpec((1, N_HEADS, tk_past), lambda i, qi, ki: (i, 0, past_idx(i, qi, ki)))]
        args += list(past)
    in_specs += [pl.BlockSpec((1, tq, WIDTH), lambda i, qi, ki: (i, new_idx(i, qi, ki), 0)),
                 pl.BlockSpec((1, tq, WIDTH), lambda i, qi, ki: (i, new_idx(i, qi, ki), 0)),
                 pl.BlockSpec((1, N_HEADS, tq), lambda i, qi, ki: (i, 0, new_idx(i, qi, ki)))]
    args += [k_new, v_new, fk_new_t]
    return pl.pallas_call(
        functools.partial(_fox_kernel, n_past_blocks=n_past_blocks, tq=tq),
        grid=(b, nq, nk),
        in_specs=in_specs,
        out_specs=pl.BlockSpec((1, tq, WIDTH), lambda i, qi, ki: (i, qi, 0)),
        out_shape=jax.ShapeDtypeStruct((b, t, WIDTH), F32),
        scratch_shapes=[pltpu.VMEM((N_HEADS, tq, 1), F32), pltpu.VMEM((N_HEADS, tq, 1), F32),
                        pltpu.VMEM((tq, WIDTH), F32)],
        compiler_params=_cparams(("parallel", "parallel", "arbitrary")),
        name="fox_attention",
    )(*args)


def _rwkv_kernel(p_ref, sh0_ref, s0_ref, mu_ref, w0_ref, wb_ref, a0_ref, ab_ref, gb_ref, kk_ref, ka_ref, rk_ref,
                 lnw_ref, lnb_ref, gm_ref, y_ref, st_ref, sht_ref, z_scr, prev_scr, *, c):
    t = pl.program_id(1)
    nt = pl.num_programs(1)
    n_rows = p_ref.shape[1]
    n_chunks = n_rows // c

    def head_block(h):
        lo = (h % HEADS_PER_GROUP) * HEAD_DIM
        return h // HEADS_PER_GROUP, slice(lo, lo + HEAD_DIM)

    @pl.when(t == 0)
    def _():
        z_scr[...] = jnp.zeros(z_scr.shape, F32)
        for h in range(N_HEADS):
            i, blk = head_block(h)
            z_scr[i, blk, blk] = s0_ref[0, h]
        prev_scr[...] = sh0_ref[0]

    p = p_ref[0]
    row = lax.broadcasted_iota(I32, p.shape, 0)
    prev = jnp.where(row == 0, prev_scr[...], pltpu.roll(p, 1, 0))
    last = p[n_rows - 1:n_rows, :]
    prev_scr[...] = last
    sht_ref[0] = last

    pm = p + (prev - p) * mu_ref[...]
    r = pm[:, 0:WIDTH]
    k = pm[:, WIDTH:2 * WIDTH]
    v = pm[:, 2 * WIDTH:3 * WIDTH]
    o1 = 3 * WIDTH
    wd = pm[:, o1:o1 + DECAY_LORA]
    ad = pm[:, o1 + DECAY_LORA:o1 + DECAY_LORA + ICLR_LORA]
    gd = pm[:, o1 + DECAY_LORA + ICLR_LORA:RWKV_COLS]

    w = -_softplus(-(w0_ref[...] + _dot(jnp.tanh(wd), wb_ref[...]))) - 0.5
    lw = -jnp.exp(w)
    a = _sigmoid(a0_ref[...] + _dot(ad, ab_ref[...]))
    g = _dot(_sigmoid(gd), gb_ref[...])
    kk = k * kk_ref[...]
    kk = kk / jnp.maximum(jnp.sqrt(_gsum(kk * kk, gm_ref)), L2_EPS)
    kf = k * (1.0 + (a - 1.0) * ka_ref[...])

    ri = lax.broadcasted_iota(I32, (n_rows, n_rows), 0)
    ci = lax.broadcasted_iota(I32, (n_rows, n_rows), 1)
    same_chunk = (ri // c) == (ci // c)
    cum = _dot_exact_rhs(jnp.logical_and(same_chunk, ri >= ci).astype(F32), lw)
    cum_last = _dot_exact_rhs(same_chunk.astype(F32), lw)
    r_t = r * jnp.exp(cum)
    a_t = -kk * jnp.exp(cum - lw)
    inv = jnp.exp(-cum)
    b_t = kk * a * inv
    k_t = kf * inv
    to_end = jnp.exp(cum_last - cum)
    b_e = kk * a * to_end
    k_e = kf * to_end
    g_end = jnp.exp(cum_last)

    hg = HEADS_PER_GROUP
    gw = hg * HEAD_DIM
    log_c = int(math.log2(c))
    t_idx = lax.broadcasted_iota(I32, (c, hg * c), 0)
    s_idx = lax.broadcasted_iota(I32, (c, hg * c), 1) & (c - 1)
    strict = s_idx < t_idx
    lower = s_idx <= t_idx
    eye = (s_idx == t_idx).astype(F32)
    rb = lax.broadcasted_iota(I32, (hg * c, gw), 0) >> log_c
    mask_kv = rb == (lax.broadcasted_iota(I32, (hg * c, gw), 1) >> int(math.log2(HEAD_DIM)))
    rs = lax.broadcasted_iota(I32, (hg * c, hg * c), 0) >> log_c
    mask_ss = rs == (lax.broadcasted_iota(I32, (hg * c, hg * c), 1) >> log_c)
    ng = N_HEADS // hg
    cat = functools.partial(jnp.concatenate, axis=0)
    units = [(slice(j * c, (j + 1) * c), slice(i * gw, (i + 1) * gw)) for j in range(n_chunks) for i in range(ng)]
    nu = len(units)

    def mm1(a, b_bd):
        return jnp.dot(a.astype(BF16), b_bd, preferred_element_type=F32)

    def dg(a, b, dims):
        return lax.dot_general(a.astype(BF16), b.astype(BF16), dims, preferred_element_type=F32)

    def bd1(x, mask):
        tiled = jnp.concatenate([x.astype(BF16)] * hg, axis=0)
        return jnp.where(mask, tiled, jnp.zeros_like(tiled))

    ar = [cat([a_t[rs_, s], r_t[rs_, s]]).astype(BF16) for rs_, s in units]
    ab = [dg(ar[n], bd1(b_t[units[n]], mask_kv), NT) for n in range(nu)]
    ak = [dg(ar[n], bd1(k_t[units[n]], mask_kv), NT) for n in range(nu)]
    l_ab = [jnp.where(strict, m[:c], 0.0) for m in ab]
    l_rb = [jnp.where(lower, m[c:], 0.0) for m in ab]
    l_ak = [jnp.where(strict, m[:c], 0.0) for m in ak]
    l_rk = [jnp.where(lower, m[c:], 0.0) for m in ak]
    tinv = [eye + m for m in l_ab]
    pw = [mm1(m, bd1(m, mask_ss)) for m in l_ab]
    for _ in range(1, log_c - 1):
        res = [mm1(cat([tinv[n], pw[n]]), bd1(pw[n], mask_ss)) for n in range(nu)]
        tinv = [tinv[n] + res[n][:c] for n in range(nu)]
        pw = [m[c:] for m in res]
    tinv = [tinv[n] + mm1(tinv[n], bd1(pw[n], mask_ss)) for n in range(nu)]
    av = [mm1(cat([l_ak[n], l_rk[n]]), bd1(v[units[n]], mask_kv)) for n in range(nu)]
    ue = [cat([b_e[units[n]], k_e[units[n]]]).astype(BF16) for n in range(nu)]

    def wide(fn, x):
        return [fn(x[:, :gw]), fn(x[:, gw:])]

    def bd1w(x):
        return jnp.concatenate(wide(lambda h_: bd1(h_, mask_kv), x), axis=1)

    rhs = [jnp.concatenate([a_t[units[n]], av[n][:c]], axis=1) for n in range(nu)]
    x0 = [mm1(tinv[n], bd1w(rhs[n])) for n in range(nu)]
    resid = [rhs[n] - (x0[n] - mm1(l_ab[n], bd1w(x0[n]))) for n in range(nu)]
    sol = [x0[n] + mm1(tinv[n], bd1w(resid[n])) for n in range(nu)]
    lift = [mm1(l_rb[n], bd1w(sol[n])) for n in range(nu)]
    lhs_s = [cat([sol[n][:, :gw], r_t[units[n]] + lift[n][:, :gw]]).astype(BF16) for n in range(nu)]
    u_loc = [sol[n][:, gw:] for n in range(nu)]
    o_loc = [av[n][c:] + lift[n][:, gw:] for n in range(nu)]

    zr = lax.broadcasted_iota(I32, (gw, gw), 0) >> int(math.log2(HEAD_DIM))
    zmask = zr == (lax.broadcasted_iota(I32, (gw, gw), 1) >> int(math.log2(HEAD_DIM)))
    z = [z_scr[i] for i in range(ng)]
    o_rows = []
    for j in range(n_chunks):
        o_grp = []
        for i in range(ng):
            n = j * ng + i
            rs_, s = units[n]
            sz = dg(lhs_s[n], z[i], NT)
            u = sz[:c] + u_loc[n]
            o_grp.append(sz[c:] + o_loc[n])
            upd = dg(cat([u, v[rs_, s]]), ue[n], TN)
            z[i] = z[i] * g_end[j * c:j * c + 1, s] + jnp.where(zmask, upd, 0.0)
        o_rows.append(jnp.concatenate(o_grp, axis=1))
    for i in range(ng):
        z_scr[i] = z[i]

    o = cat(o_rows)
    inv_hd = 1.0 / HEAD_DIM
    dlt = o - _gsum(o, gm_ref) * inv_hd
    var = _gsum(dlt * dlt, gm_ref) * inv_hd
    on = dlt * lax.rsqrt(var + RWKV_GN_EPS) * lnw_ref[...] + lnb_ref[...]
    bonus = _gsum(r * kf * rk_ref[...], gm_ref) * v
    y_ref[0] = (on + bonus) * g

    @pl.when(t == nt - 1)
    def _():
        for h in range(N_HEADS):
            i, blk = head_block(h)
            st_ref[0, h] = z_scr[i, blk, blk]


def _rwkv(p, shift0, s0, prm, gmat, chunk, chunks_per_step):
    b, t, _ = p.shape
    row = lambda n: _const_spec((1, n))
    rows = chunk * chunks_per_step
    return pl.pallas_call(
        functools.partial(_rwkv_kernel, c=chunk),
        grid=(b, t // rows),
        in_specs=[pl.BlockSpec((1, rows, RWKV_COLS), lambda i, j: (i, j, 0)),
                  pl.BlockSpec((1, 1, RWKV_COLS), lambda i, j: (i, 0, 0)),
                  pl.BlockSpec((1, N_HEADS, HEAD_DIM, HEAD_DIM), lambda i, j: (i, 0, 0, 0)),
                  row(RWKV_COLS), row(WIDTH), _const_spec((DECAY_LORA, WIDTH)), row(WIDTH),
                  _const_spec((ICLR_LORA, WIDTH)), _const_spec((GATE_LORA, WIDTH)),
                  row(WIDTH), row(WIDTH), row(WIDTH), row(WIDTH), row(WIDTH), _const_spec((WIDTH, WIDTH))],
        out_specs=[pl.BlockSpec((1, rows, WIDTH), lambda i, j: (i, j, 0)),
                   pl.BlockSpec((1, N_HEADS, HEAD_DIM, HEAD_DIM), lambda i, j: (i, 0, 0, 0)),
                   pl.BlockSpec((1, 1, RWKV_COLS), lambda i, j: (i, 0, 0))],
        out_shape=[jax.ShapeDtypeStruct((b, t, WIDTH), F32),
                   jax.ShapeDtypeStruct((b, N_HEADS, HEAD_DIM, HEAD_DIM), F32),
                   jax.ShapeDtypeStruct((b, 1, RWKV_COLS), F32)],
        scratch_shapes=[pltpu.VMEM((N_HEADS // HEADS_PER_GROUP, HEADS_PER_GROUP * HEAD_DIM,
                                    HEADS_PER_GROUP * HEAD_DIM), F32),
                        pltpu.VMEM((1, RWKV_COLS), F32)],
        compiler_params=_cparams(("parallel", "arbitrary")),
        name="rwkv7_mix",
    )(p, shift0, s0, prm["mu"], prm["w0"], prm["wb"], prm["a0"], prm["ab"], prm["gb"], prm["kk"], prm["ka"],
      prm["rk"], prm["lnw"], prm["lnb"], gmat)


def _merge_kernel(x_ref, ya_ref, yb_ref, gate_ref, mod_ref, g2_ref, woa_ref, wob_ref, wo_ref, wrh_ref, wrl_ref,
                  *rest):
    x1_ref, h2_ref, lg_ref = rest[-3:]
    bb, tt, d = x_ref.shape
    m = bb * tt
    gate = gate_ref[...].reshape(m, GATE_COLS)
    merged = (gate[:, 0:d] * _dot(ya_ref[...].reshape(m, WIDTH), woa_ref[...])
              + gate[:, d:2 * d] * _dot(yb_ref[...].reshape(m, WIDTH), wob_ref[...]))
    x1 = x_ref[...] + mod_ref[:, 2:3, :] * _dot(merged, wo_ref[...]).reshape(bb, tt, d)
    x1_ref[...] = x1
    ms = jnp.mean(x1 * x1, axis=-1, keepdims=True)
    h2 = x1 * lax.rsqrt(ms + RMS_EPS) * g2_ref[...]
    h2 = (h2 * (1.0 + mod_ref[:, 4:5, :]) + mod_ref[:, 3:4, :]).reshape(m, d)
    h2_ref[...] = _pack_bf16_pairs(h2)
    lg_ref[...] = _mm3((wrh_ref[...], wrl_ref[...]), _split2(h2), NT)


def _merge(x, ya, yb, gate, mod, w, n_total, row_offset, shared=None):
    b, t, d = x.shape
    bb, tt = _token_blocks(b, t, MOE_TOKEN_ROWS)
    nt = t // tt
    m = bb * tt
    off = row_offset // m

    def tok(cols):
        return pl.BlockSpec((bb, tt, cols), lambda i, j: (i, j, 0))

    in_specs = [tok(d), tok(WIDTH), tok(WIDTH), tok(GATE_COLS),
                pl.BlockSpec((bb, 6, d), lambda i, j: (i, 0, 0)),
                _const_spec((1, d)), _const_spec((WIDTH, d)), _const_spec((WIDTH, d)), _const_spec((d, d)),
                _const_spec((N_EXPERTS, d)), _const_spec((N_EXPERTS, d))]
    args = [x, ya, yb, gate, mod.reshape(b, 6, d), w["g2"], w["w_oa"], w["w_ob"], w["w_o"], w["wr_hi"], w["wr_lo"]]
    aliases = {}
    if shared is not None:
        aliases = {len(args): 1, len(args) + 1: 2}
        in_specs += [pl.BlockSpec(memory_space=pl.ANY), pl.BlockSpec(memory_space=pl.ANY)]
        args += list(shared)
    return pl.pallas_call(
        _merge_kernel,
        grid=(b // bb, nt),
        in_specs=in_specs,
        out_specs=[tok(d), pl.BlockSpec((m, d // 2), lambda i, j: (off + i * nt + j, 0)),
                   pl.BlockSpec((N_EXPERTS, m), lambda i, j: (0, off + i * nt + j))],
        out_shape=[jax.ShapeDtypeStruct((b, t, d), F32), jax.ShapeDtypeStruct((n_total, d // 2), I32),
                   jax.ShapeDtypeStruct((N_EXPERTS, n_total), F32)],
        input_output_aliases=aliases,
        compiler_params=_cparams(("parallel", "parallel")),
        name="merge_norm2_router",
    )(*args)


def _route_kernel(lg_ref, bias_ref, idx_ref, wt_ref, rank_ref, cnt_ref, carry):
    @pl.when(pl.program_id(0) == 0)
    def _():
        carry[...] = jnp.zeros(carry.shape, F32)

    tm = lg_ref.shape[1]
    scores = _sigmoid(lg_ref[...])
    sel = scores + bias_ref[...]
    row = lax.broadcasted_iota(I32, (N_EXPERTS, tm), 0)
    neg_inf = -jnp.inf

    def first_argmax(vals, rows):
        mx = jnp.max(vals, axis=0, keepdims=True)
        return mx, jnp.min(jnp.where(vals == mx, rows, N_EXPERTS), axis=0, keepdims=True)

    gslices = [slice(g * EXPERTS_PER_GROUP, (g + 1) * EXPERTS_PER_GROUP) for g in range(N_GROUPS)]
    gs = []
    row_g = lax.broadcasted_iota(I32, (EXPERTS_PER_GROUP, tm), 0)
    for sl in gslices:
        m1, i1 = first_argmax(sel[sl], row_g)
        m2 = jnp.max(jnp.where(row_g == i1, neg_inf, sel[sl]), axis=0, keepdims=True)
        gs.append(m1 + m2)
    kept = []
    for g in range(N_GROUPS):
        beaten = jnp.zeros((1, tm), I32)
        for o in range(N_GROUPS):
            if o != g:
                wins = (gs[o] >= gs[g]) if o < g else (gs[o] > gs[g])
                beaten = beaten + wins.astype(I32)
        kept.append(jnp.where(beaten < TOPK_GROUPS, sel[gslices[g]], neg_inf))
    cur = jnp.concatenate(kept, axis=0)

    idxs, ws = [], []
    picked = jnp.zeros((N_EXPERTS, tm), F32)
    for _ in range(TOP_K):
        _, ik = first_argmax(cur, row)
        hit = row == ik
        idxs.append(ik)
        ws.append(jnp.sum(jnp.where(hit, scores, 0.0), axis=0, keepdims=True))
        cur = jnp.where(hit, neg_inf, cur)
        picked = jnp.where(hit, 1.0, picked)
    wsum = ws[0]
    for k in range(1, TOP_K):
        wsum = wsum + ws[k]

    r = lax.broadcasted_iota(I32, (tm, tm), 0)
    c = lax.broadcasted_iota(I32, (tm, tm), 1)
    before = jnp.dot(picked.astype(BF16), (r < c).astype(BF16), preferred_element_type=F32) + carry[...]
    carry[...] = carry[...] + jnp.sum(picked, axis=1, keepdims=True)
    cnt_ref[...] = carry[...]

    kk = lax.broadcasted_iota(I32, (TOP_K, tm), 0)
    idx_o = jnp.zeros((TOP_K, tm), I32)
    wt_o = jnp.zeros((TOP_K, tm), F32)
    rank_o = jnp.zeros((TOP_K, tm), F32)
    for k in range(TOP_K):
        rk = jnp.sum(jnp.where(row == idxs[k], before, 0.0), axis=0, keepdims=True)
        idx_o = jnp.where(kk == k, idxs[k], idx_o)
        wt_o = jnp.where(kk == k, ws[k] / wsum * ROUTED_SCALE, wt_o)
        rank_o = jnp.where(kk == k, rk, rank_o)
    idx_ref[...] = idx_o
    wt_ref[...] = wt_o
    rank_ref[...] = rank_o.astype(I32)


def _route(logits_t, bias_col, tm):
    n = logits_t.shape[1]
    tokk = pl.BlockSpec((TOP_K, tm), lambda i: (0, i))
    return pl.pallas_call(
        _route_kernel,
        grid=(n // tm,),
        in_specs=[pl.BlockSpec((N_EXPERTS, tm), lambda i: (0, i)), _const_spec((N_EXPERTS, 1))],
        out_specs=[tokk, tokk, tokk, _const_spec((N_EXPERTS, 1))],
        out_shape=[jax.ShapeDtypeStruct((TOP_K, n), I32), jax.ShapeDtypeStruct((TOP_K, n), F32),
                   jax.ShapeDtypeStruct((TOP_K, n), I32), jax.ShapeDtypeStruct((N_EXPERTS, 1), F32)],
        scratch_shapes=[pltpu.VMEM((N_EXPERTS, 1), F32)],
        compiler_params=_cparams(("arbitrary",)),
        name="route_topk",
    )(logits_t, bias_col)


def _plan_kernel(cnt_ref, start_ref, be_ref, valid_ref, nu_ref, *, blk):
    cnt = cnt_ref[...]
    padded = jnp.ceil(cnt * (1.0 / blk)) * blk
    e_r = lax.broadcasted_iota(I32, (N_EXPERTS, N_EXPERTS), 0)
    e_c = lax.broadcasted_iota(I32, (N_EXPERTS, N_EXPERTS), 1)
    incl = (e_r <= e_c).astype(BF16)
    ph, pm, plo = _split3(jnp.broadcast_to(padded, (8, N_EXPERTS)))
    d2 = functools.partial(jnp.dot, preferred_element_type=F32)
    pad_end = (d2(ph, incl) + (d2(pm, incl) + d2(plo, incl)))[0:1, :]
    pad_start = pad_end - padded
    start_ref[...] = pad_start.astype(I32)
    total = jnp.max(pad_end, axis=-1, keepdims=True)
    nu_ref[...] = jnp.broadcast_to(total * (1.0 / blk), (1, N_EXPERTS)).astype(I32)
    nb = be_ref.shape[0]
    first = (lax.broadcasted_iota(I32, (nb, N_EXPERTS), 0) * blk).astype(F32)
    lane = lax.broadcasted_iota(I32, (nb, N_EXPERTS), 1)
    inside = jnp.logical_and(pad_start <= first, first < pad_end)
    be_ref[...] = jnp.sum(jnp.where(inside, lane, 0), axis=-1, keepdims=True)
    rows = jnp.minimum(pad_start + cnt - first, float(blk))
    valid_ref[...] = jnp.sum(jnp.where(inside, rows, 0.0), axis=-1, keepdims=True).astype(I32)


def _plan(counts, n_blocks, blk):
    return pl.pallas_call(
        functools.partial(_plan_kernel, blk=blk),
        out_shape=[jax.ShapeDtypeStruct((1, N_EXPERTS), I32), jax.ShapeDtypeStruct((n_blocks, 1), I32),
                   jax.ShapeDtypeStruct((n_blocks, 1), I32), jax.ShapeDtypeStruct((1, N_EXPERTS), I32)],
        compiler_params=pltpu.CompilerParams(vmem_limit_bytes=VMEM_LIMIT),
        name="dispatch_plan",
    )(counts)


def _dest_kernel(idx_ref, rank_ref, start_ref, dest_ref):
    tm = idx_ref.shape[1]
    row = lax.broadcasted_iota(I32, (N_EXPERTS, tm), 0)
    kk = lax.broadcasted_iota(I32, (TOP_K, tm), 0)
    idx = idx_ref[...]
    base = jnp.zeros((TOP_K, tm), I32)
    for k in range(TOP_K):
        bk = jnp.sum(jnp.where(row == idx[k:k + 1, :], start_ref[...], 0), axis=0, keepdims=True)
        base = jnp.where(kk == k, bk, base)
    dest_ref[...] = base + rank_ref[...]


def _dest(idx, rank, pad_start_col, tm):
    n = idx.shape[1]
    tokk = pl.BlockSpec((TOP_K, tm), lambda i: (0, i))
    return pl.pallas_call(
        _dest_kernel,
        grid=(n // tm,),
        in_specs=[tokk, tokk, _const_spec((N_EXPERTS, 1))],
        out_specs=tokk,
        out_shape=jax.ShapeDtypeStruct((TOP_K, n), I32),
        compiler_params=_cparams(("parallel",)),
        name="dispatch_dest",
    )(idx, rank, pad_start_col)


def _dispatch(h2p, dest_t, n_slots):
    n, wp = h2p.shape
    half = wp // SC_ROW_SPLIT
    window = SC_SCATTER_WINDOW
    mesh = plsc.VectorSubcoreMesh(core_axis_name="core", subcore_axis_name="subcore")
    out = jax.ShapeDtypeStruct((n_slots, half), h2p.dtype)

    @functools.partial(pl.kernel, out_type=[out] * SC_ROW_SPLIT, mesh=mesh, scratch_types=[])
    def scatter(rows_hbm, idx_hbm, *outs):
        for c, out_hbm in enumerate(outs):
            def body(rows_vmem, idx_vmem, out_hbm=out_hbm):
                for k in range(TOP_K):
                    pltpu.sync_copy(rows_vmem, out_hbm.at[idx_vmem.at[k]])

            pltpu.emit_pipeline(
                body, grid=(n // window,),
                in_specs=[pl.BlockSpec((window, half), index_map=lambda i, c=c: (i, c)),
                          pl.BlockSpec((TOP_K, window), index_map=lambda i: (0, i))],
                out_specs=[], core_axis_name=("core", "subcore"), dimension_semantics=(pltpu.PARALLEL,),
            )(rows_hbm, idx_hbm)

    return scatter(h2p, dest_t)


def _expert_kernel(be_ref, valid_ref, nu_ref, xa_ref, xb_ref, wg_ref, wu_ref, wd_ref, *rest):
    y_refs, (wg_b, wu_b, wd_b) = rest[:SC_ROW_SPLIT], rest[SC_ROW_SPLIT:]
    i = pl.program_id(0)
    nv = valid_ref[i]
    new_expert = jnp.logical_or(i == 0, be_ref[i] != be_ref[jnp.maximum(i - 1, 0)])

    @pl.when(jnp.logical_and(nv > 0, new_expert))
    def _():
        wg_b[...] = wg_ref[0].astype(BF16)
        wu_b[...] = wu_ref[0].astype(BF16)
        wd_b[...] = wd_ref[0].astype(BF16)

    @pl.when(nv > 0)
    def _():
        blk = xa_ref.shape[0]
        rows = lax.broadcasted_iota(I32, (blk, 1), 0)
        packed = jnp.concatenate([xa_ref[...], xb_ref[...]], axis=1)
        x = _unpack_bf16_pairs(jnp.where(rows < nv, packed, 0))
        hg = jnp.dot(x, wg_b[...], preferred_element_type=F32)
        hu = jnp.dot(x, wu_b[...], preferred_element_type=F32)
        y = _pack_bf16_pairs(jnp.dot((_silu(hg) * hu).astype(BF16), wd_b[...], preferred_element_type=F32))
        cw = y.shape[1] // SC_ROW_SPLIT
        for c, y_ref in enumerate(y_refs):
            y_ref[...] = y[:, c * cw:(c + 1) * cw]


def _experts(xs, block_e, valid, n_used, w_eg, w_eu, w_ed, blk):
    xa, xb = xs
    n_slots, packed = xa.shape
    d = w_eg.shape[1]
    n_blocks = n_slots // blk

    def row_blk(i, be, valid, nu):
        return (jnp.minimum(i, nu[0] - 1), 0)

    def w_blk(i, be, valid, nu):
        return (be[i], 0, 0)

    return pl.pallas_call(
        _expert_kernel,
        grid_spec=pltpu.PrefetchScalarGridSpec(
            num_scalar_prefetch=3,
            grid=(n_blocks,),
            in_specs=[pl.BlockSpec((blk, packed), row_blk), pl.BlockSpec((blk, packed), row_blk),
                      pl.BlockSpec((1, d, D_EXPERT), w_blk), pl.BlockSpec((1, d, D_EXPERT), w_blk),
                      pl.BlockSpec((1, D_EXPERT, d), w_blk)],
            out_specs=[pl.BlockSpec((blk, packed), row_blk)] * SC_ROW_SPLIT,
            scratch_shapes=[pltpu.VMEM((d, D_EXPERT), BF16), pltpu.VMEM((d, D_EXPERT), BF16),
                            pltpu.VMEM((D_EXPERT, d), BF16)]),
        out_shape=[jax.ShapeDtypeStruct((n_slots, packed), I32)] * SC_ROW_SPLIT,
        compiler_params=_cparams(("arbitrary",)),
        name="moe_experts",
    )(block_e, valid, n_used, xa, xb, w_eg, w_eu, w_ed)


def _combine_gather(ys, dest_t):
    k, n = dest_t.shape
    cw = ys[0].shape[1]
    window = SC_SCATTER_WINDOW
    mesh = plsc.VectorSubcoreMesh(core_axis_name="core", subcore_axis_name="subcore")

    @functools.partial(pl.kernel, out_type=jax.ShapeDtypeStruct((k * n, cw * len(ys)), ys[0].dtype), mesh=mesh,
                       scratch_types=[])
    def gather(*refs):
        y_refs, idx_hbm, out_hbm = refs[:len(ys)], refs[len(ys)], refs[len(ys) + 1]
        for c, y_hbm in enumerate(y_refs):
            def body(idx_vmem, out_vmem, y_hbm=y_hbm):
                pltpu.sync_copy(y_hbm.at[idx_vmem.at[0]], out_vmem)

            pltpu.emit_pipeline(
                body, grid=(k * n // window,),
                in_specs=[pl.BlockSpec((1, window), index_map=lambda i: (0, i))],
                out_specs=[pl.BlockSpec((window, cw), index_map=lambda i, c=c: (i, c))],
                core_axis_name=("core", "subcore"), dimension_semantics=(pltpu.PARALLEL,),
            )(idx_hbm, out_hbm)

    return gather(*ys, dest_t.reshape(1, k * n)).reshape(k, n, cw * len(ys))


def _final_kernel(x1_ref, h2_ref, wt_ref, mod_ref, wsg_ref, wsu_ref, wsd_ref, yg_ref, o_ref):
    bb, tt, d = x1_ref.shape
    hb = _unpack_bf16_pairs(h2_ref[...])
    hg = jnp.dot(hb, wsg_ref[...], preferred_element_type=F32)
    hu = jnp.dot(hb, wsu_ref[...], preferred_element_type=F32)
    ffn = _dot(_silu(hg) * hu, wsd_ref[...])
    wt = wt_ref[...]
    for k in range(TOP_K):
        ffn = ffn + wt[:, k:k + 1] * _unpack_bf16_pairs(yg_ref[k]).astype(F32)
    o_ref[...] = x1_ref[...] + mod_ref[:, 5:6, :] * ffn.reshape(bb, tt, d)


def _final(x1, h2_all, wts_all, y_tok, mod, w, row_offset):
    b, t, d = x1.shape
    bb, tt = _token_blocks(b, t, MOE_TOKEN_ROWS)
    nt = t // tt
    m = bb * tt
    off = row_offset // m

    def flat_idx(i, j):
        return off + i * nt + j

    return pl.pallas_call(
        _final_kernel,
        grid=(b // bb, nt),
        in_specs=[pl.BlockSpec((bb, tt, d), lambda i, j: (i, j, 0)),
                  pl.BlockSpec((m, h2_all.shape[1]), lambda i, j: (flat_idx(i, j), 0)),
                  pl.BlockSpec((m, TOP_K), lambda i, j: (flat_idx(i, j), 0)),
                  pl.BlockSpec((bb, 6, d), lambda i, j: (i, 0, 0)),
                  _const_spec((d, D_EXPERT)), _const_spec((d, D_EXPERT)), _const_spec((D_EXPERT, d)),
                  pl.BlockSpec((TOP_K, m, y_tok.shape[2]), lambda i, j: (0, flat_idx(i, j), 0))],
        out_specs=pl.BlockSpec((bb, tt, d), lambda i, j: (i, j, 0)),
        out_shape=jax.ShapeDtypeStruct((b, t, d), F32),
        compiler_params=_cparams(("parallel", "parallel")),
        name="moe_combine_final",
    )(x1, h2_all, wts_all, mod.reshape(b, 6, d), w["w_sg"], w["w_su"], w["w_sd"], y_tok)


def _largest_tile(n, candidates):
    return next(c for c in candidates if n % c == 0)


def _moe_routed(h2_all, logits_all, w, blk=EXPERT_BLOCK_ROWS):
    n = h2_all.shape[0]
    n_blocks = (n * TOP_K + N_EXPERTS * (blk - 1)) // blk + 1
    n_blocks = (n_blocks + 7) // 8 * 8
    idx, wts_t, rank, counts = _route(logits_all, w["router_bias"], _largest_tile(n, (512, 256)))
    pad_start, block_e, valid, n_used = _plan(counts.reshape(1, N_EXPERTS), n_blocks, blk)
    block_e = block_e.reshape(n_blocks)
    valid = valid.reshape(n_blocks)
    n_used = n_used[0, 0:1]
    dest_t = _dest(idx, rank, pad_start.reshape(N_EXPERTS, 1), _largest_tile(n, (1024, 512, 256)))
    xs = _dispatch(h2_all, dest_t, n_blocks * blk)
    ys = _experts(xs, block_e, valid, n_used, w["w_eg"], w["w_eu"], w["w_ed"], blk)
    return _combine_gather(ys, dest_t), jnp.transpose(wts_t)


def _prep(raw):
    p = {k: v[0] for k, v in raw.items()}
    w_in = p["w_in"]
    o_fox = RWKV_COLS
    o_fl = o_fox + FOX_MAIN_COLS
    o_gate = o_fl + N_HEADS
    row = lambda a: a.reshape(1, -1)
    return dict(
        w_ada=p["w_ada"], b_ada=p["b_ada"],
        g1=row(p["norm1_g"]), g2=row(p["norm2_g"]),
        wr=w_in[:, :o_fox].astype(BF16),
        wf=w_in[:, o_fox:o_fl].astype(BF16),
        wfl=jnp.pad(w_in[:, o_fl:o_gate], ((0, 0), (0, LANES - N_HEADS))).astype(BF16),
        wg=w_in[:, o_gate:].astype(BF16),
        qn=row(jnp.tile(p["fox_q_norm"], N_HEADS)), kn=row(jnp.tile(p["fox_k_norm"], N_HEADS)),
        fb=row(p["fox_f_bias"]),
        gmat=_group_ones(),
        rwkv=dict(mu=row(p["rwkv_mu"]), w0=row(p["rwkv_w0"]), wb=p["rwkv_w_lora_b"], a0=row(p["rwkv_a0"]),
                  ab=p["rwkv_a_lora_b"], gb=p["rwkv_g_lora_b"], kk=row(p["rwkv_k_k"]), ka=row(p["rwkv_k_a"]),
                  rk=row(p["rwkv_r_k"]), lnw=row(p["rwkv_ln_w"]), lnb=row(p["rwkv_ln_b"])),
        w_oa=p["w_out_rwkv"].astype(BF16), w_ob=p["w_out_fox"].astype(BF16), w_o=p["w_out"].astype(BF16),
        wr_hi=p["w_router"].T.astype(BF16),
        wr_lo=(p["w_router"] - p["w_router"].astype(BF16).astype(F32)).T.astype(BF16),
        router_bias=p["router_bias"].reshape(N_EXPERTS, 1),
        w_eg=p["w_exp_gate"], w_eu=p["w_exp_up"], w_ed=p["w_exp_down"],
        w_sg=p["w_sh_gate"].astype(BF16), w_su=p["w_sh_up"].astype(BF16), w_sd=p["w_sh_down"].astype(BF16),
    )


def _token_blocks(b, t, rows=256):
    if t >= rows:
        return 1, rows
    bb = max(1, min(b, 256 // t))
    while b % bb:
        bb -= 1
    return bb, t


def _mix_path(x, mod, shift0, wkv0, past_k, past_v, past_logf, w):
    b, t, d = x.shape
    bb, tt = _token_blocks(b, t)
    n_past = past_k.shape[1]
    if n_past:
        f_past = _past_cumsum(past_logf)
        init = f_past[:, n_past - 1:n_past, :]
        past = (past_k, past_v, jnp.swapaxes(f_past, 1, 2))
    else:
        init = jnp.zeros((b, 1, N_HEADS), F32)
        past = None
    pr, q, k, v, sg, logf, f_new, gate = _inproj(x, mod.reshape(b, 6, d), w["g1"], w["wr"], w["wf"], w["wfl"],
                                                 w["wg"], w["qn"], w["kn"], w["fb"], w["gmat"], init, bb, tt)
    y_fox = _fox_attention(q, f_new, sg, k, v, jnp.swapaxes(f_new, 1, 2), past=past, tq=min(t, 512),
                           tk_past=min(max(n_past, 1), 512))
    chunk = min(t, RWKV_CHUNK)
    y_rwkv, wkv_new, shift_new = _rwkv(pr, shift0.reshape(b, 1, RWKV_COLS), wkv0, w["rwkv"], w["gmat"],
                                       chunk, max(1, min(RWKV_CHUNKS_PER_STEP, t // chunk)))
    return y_rwkv, y_fox, gate, wkv_new, shift_new, k, v, logf


def _layer(paths, w):
    n_b = [p[0].shape[0] for p in paths]
    mod_all = _ada(jnp.concatenate([p[1] for p in paths], axis=0), w["w_ada"], w["b_ada"])
    mods, o = [], 0
    for nb in n_b:
        mods.append(mod_all[o:o + nb])
        o += nb
    n_total = sum(p[0].shape[0] * p[0].shape[1] for p in paths)
    mixed, x1s = [], []
    shared, row = None, 0
    for (x, _, shift0, wkv0, pk, pv, plf), mod in zip(paths, mods):
        ya, yb, gate, wkv_new, shift_new, k, v, logf = _mix_path(x, mod, shift0, wkv0, pk, pv, plf, w)
        x1, h2_all, lg_all = _merge(x, ya, yb, gate, mod, w, n_total, row, shared)
        shared = (h2_all, lg_all)
        row += x.shape[0] * x.shape[1]
        mixed.append((wkv_new, shift_new, k, v, logf))
        x1s.append(x1)
    y_tok, wts = _moe_routed(h2_all, lg_all, w)
    outs, row = [], 0
    for x1, mod, st in zip(x1s, mods, mixed):
        y = _final(x1, h2_all, wts, y_tok, mod, w, row)
        row += x1.shape[0] * x1.shape[1]
        outs.append((y,) + st)
    return outs


def kernel(x_prompt, x_sample, c_prompt, c_sample, state_rwkv_wkv, state_rwkv_shift, cache_fox_k, cache_fox_v,
           cache_fox_logf, w_ada, b_ada, norm1_g, norm2_g, w_in, rwkv_mu, rwkv_w0, rwkv_w_lora_b, rwkv_a0,
           rwkv_a_lora_b, rwkv_g_lora_b, rwkv_k_k, rwkv_k_a, rwkv_r_k, rwkv_ln_w, rwkv_ln_b, fox_q_norm,
           fox_k_norm, fox_f_bias, w_out_rwkv, w_out_fox, w_out, w_router, router_bias, w_exp_gate, w_exp_up,
           w_exp_down, w_sh_gate, w_sh_up, w_sh_down):
    raw = dict(w_ada=w_ada, b_ada=b_ada, norm1_g=norm1_g, norm2_g=norm2_g, w_in=w_in, rwkv_mu=rwkv_mu,
               rwkv_w0=rwkv_w0, rwkv_w_lora_b=rwkv_w_lora_b, rwkv_a0=rwkv_a0, rwkv_a_lora_b=rwkv_a_lora_b,
               rwkv_g_lora_b=rwkv_g_lora_b, rwkv_k_k=rwkv_k_k, rwkv_k_a=rwkv_k_a, rwkv_r_k=rwkv_r_k,
               rwkv_ln_w=rwkv_ln_w, rwkv_ln_b=rwkv_ln_b, fox_q_norm=fox_q_norm, fox_k_norm=fox_k_norm,
               fox_f_bias=fox_f_bias, w_out_rwkv=w_out_rwkv, w_out_fox=w_out_fox, w_out=w_out,
               w_router=w_router, router_bias=router_bias, w_exp_gate=w_exp_gate, w_exp_up=w_exp_up,
               w_exp_down=w_exp_down, w_sh_gate=w_sh_gate, w_sh_up=w_sh_up, w_sh_down=w_sh_down)
    assert w_in.shape[0] == 1, "single-layer stack"
    w = _prep(raw)
    bp, tp, _ = x_prompt.shape
    bs, ts, _ = x_sample.shape
    n_past = cache_fox_k.shape[2]
    prompt = (x_prompt, c_prompt, jnp.zeros((bp, RWKV_COLS), F32),
              jnp.zeros((bp, N_HEADS, HEAD_DIM, HEAD_DIM), F32),
              jnp.zeros((bp, 0, WIDTH), F32), jnp.zeros((bp, 0, WIDTH), F32), jnp.zeros((bp, 0, N_HEADS), F32))
    sample = (x_sample, c_sample, state_rwkv_shift[0], state_rwkv_wkv[0],
              cache_fox_k[0].reshape(bs, n_past, WIDTH), cache_fox_v[0].reshape(bs, n_past, WIDTH),
              cache_fox_logf[0])
    (yp, wkv_p, sh_p, k_p, v_p, lf_p), (ysm, wkv_s, sh_s, k_s, v_s, lf_s) = _layer([prompt, sample], w)

    def heads(a):
        return a.reshape((1,) + a.shape[:2] + (N_HEADS, HEAD_DIM))

    return (yp, ysm,
            wkv_p[None], sh_p.reshape(1, bp, RWKV_COLS), heads(k_p), heads(v_p), lf_p[None],
            wkv_s[None], sh_s.reshape(1, bs, RWKV_COLS), heads(k_s), heads(v_s), lf_s[None])
```

```python
import functools
import math

import jax
import jax.numpy as jnp
from jax import lax
from jax.experimental import pallas as pl
from jax.experimental.pallas import tpu as pltpu
from jax.experimental.pallas import tpu_sc as plsc

F32 = jnp.float32
BF16 = jnp.bfloat16
I32 = jnp.int32

D_MODEL = 1024
N_HEADS = 8
HEAD_DIM = 64
WIDTH = N_HEADS * HEAD_DIM
HEADS_PER_GROUP = 4
RWKV_CHUNK = 64
RWKV_CHUNKS_PER_STEP = 4
MOE_TOKEN_ROWS = 512
EXPERT_BLOCK_ROWS = 512
SC_SCATTER_WINDOW = 128
SC_ROW_SPLIT = 2
DECAY_LORA = 64
ICLR_LORA = 64
GATE_LORA = 128
RWKV_COLS = 3 * WIDTH + DECAY_LORA + ICLR_LORA + GATE_LORA
FOX_MAIN_COLS = 4 * WIDTH
GATE_COLS = 2 * D_MODEL
RWKV_GN_EPS = HEAD_DIM * 1e-5
L2_EPS = 1e-12
RMS_EPS = 1e-6
N_EXPERTS = 256
TOP_K = 8
N_GROUPS = 8
TOPK_GROUPS = 4
EXPERTS_PER_GROUP = N_EXPERTS // N_GROUPS
D_EXPERT = 256
ROUTED_SCALE = 2.5

LANES = 128
VMEM_LIMIT = 56 * 1024 * 1024
NEG_BIG = -1e30

NN = (((1,), (0,)), ((), ()))
NT = (((1,), (1,)), ((), ()))
TN = (((0,), (0,)), ((), ()))


def _cparams(sem):
    return pltpu.CompilerParams(dimension_semantics=sem, vmem_limit_bytes=VMEM_LIMIT)


def _dot(a, b, dims=NN):
    return lax.dot_general(a.astype(BF16), b.astype(BF16), dims, preferred_element_type=F32)


def _split2(a):
    hi = a.astype(BF16)
    lo = (a - hi.astype(F32)).astype(BF16)
    return hi, lo


def _split3(a):
    hi = a.astype(BF16)
    r1 = a - hi.astype(F32)
    mid = r1.astype(BF16)
    lo = (r1 - mid.astype(F32)).astype(BF16)
    return hi, mid, lo


def _mm3(a, b, dims):
    d = functools.partial(lax.dot_general, dimension_numbers=dims, preferred_element_type=F32)
    return d(a[0], b[0]) + (d(a[0], b[1]) + d(a[1], b[0]))


def _dot_exact_rhs(a_exact, b, dims=NN):
    ab = a_exact.astype(BF16)
    bh, bm, bl = _split3(b)
    d = functools.partial(lax.dot_general, dimension_numbers=dims, preferred_element_type=F32)
    return d(ab, bh) + (d(ab, bm) + d(ab, bl))


def _gsum(x, g_ref):
    half = HEADS_PER_GROUP * HEAD_DIM
    hi, lo = _split2(x)
    g = g_ref[0:half, 0:half]
    d = functools.partial(jnp.dot, preferred_element_type=F32)
    return jnp.concatenate([d(hi[:, s], g) + d(lo[:, s], g) for s in (slice(0, half), slice(half, 2 * half))],
                           axis=1)


def _sigmoid(x):
    return 1.0 / (1.0 + jnp.exp(-x))


def _softplus(x):
    return jnp.maximum(x, 0.0) + jnp.log1p(jnp.exp(-jnp.abs(x)))


def _silu(x):
    return x * _sigmoid(x)


def _pack_bf16_pairs(x):
    w = x.shape[1] // 2
    bits = lax.bitcast_convert_type(x.astype(BF16).astype(F32), I32)
    return lax.shift_right_logical(bits[:, :w], 16) | (bits[:, w:] & -65536)


def _unpack_bf16_pairs(p):
    lo = lax.bitcast_convert_type(lax.shift_left(p, 16), F32)
    hi = lax.bitcast_convert_type(p & -65536, F32)
    return jnp.concatenate([lo, hi], axis=1).astype(BF16)


def _group_ones():
    h = jnp.arange(WIDTH, dtype=I32) // HEAD_DIM
    return (h[:, None] == h[None, :]).astype(BF16)


def _ada_kernel(c_ref, w_ref, b_ref, o_ref):
    o_ref[...] = _dot(_silu(c_ref[...]), w_ref[...]) + b_ref[...]


def _ada(c, w_ada, b_ada):
    nb = c.shape[0]
    n_out = w_ada.shape[1]
    blk = D_MODEL
    return pl.pallas_call(
        _ada_kernel,
        grid=(n_out // blk,),
        in_specs=[pl.BlockSpec((nb, D_MODEL), lambda j: (0, 0)),
                  pl.BlockSpec((D_MODEL, blk), lambda j: (0, j)),
                  pl.BlockSpec((1, blk), lambda j: (0, j))],
        out_specs=pl.BlockSpec((nb, blk), lambda j: (0, j)),
        out_shape=jax.ShapeDtypeStruct((nb, n_out), F32),
        compiler_params=_cparams(("parallel",)),
        name="ada_mod",
    )(c, w_ada, b_ada.reshape(1, n_out))


def _inproj_kernel(x_ref, mod_ref, g1_ref, wr_ref, wf_ref, wfl_ref, wg_ref, qn_ref, kn_ref, fb_ref, gm_ref, f0_ref,
                   pr_ref, q_ref, k_ref, v_ref, sg_ref, lf_ref, cf_ref, gate_ref, carry):
    bb, tt, d = x_ref.shape
    m = bb * tt
    x = x_ref[...]
    ms = jnp.mean(x * x, axis=-1, keepdims=True)
    h = x * lax.rsqrt(ms + RMS_EPS) * g1_ref[...]
    h = h * (1.0 + mod_ref[:, 1:2, :]) + mod_ref[:, 0:1, :]
    hb = h.reshape(m, d).astype(BF16)

    pr_ref[...] = jnp.dot(hb, wr_ref[...], preferred_element_type=F32).reshape(bb, tt, RWKV_COLS)

    f = jnp.dot(hb, wf_ref[...], preferred_element_type=F32)
    q = f[:, 0:WIDTH]
    k = f[:, WIDTH:2 * WIDTH]
    v = f[:, 2 * WIDTH:3 * WIDTH]
    og = f[:, 3 * WIDTH:4 * WIDTH]
    inv_hd = 1.0 / HEAD_DIM
    q = q * lax.rsqrt(_gsum(q * q, gm_ref) * inv_hd + RMS_EPS) * qn_ref[...]
    k = k * lax.rsqrt(_gsum(k * k, gm_ref) * inv_hd + RMS_EPS) * kn_ref[...]
    q_ref[...] = (q * (HEAD_DIM ** -0.5)).astype(BF16).reshape(bb, tt, WIDTH)
    k_ref[...] = k.reshape(bb, tt, WIDTH)
    v_ref[...] = v.reshape(bb, tt, WIDTH)
    sg_ref[...] = _sigmoid(og).reshape(bb, tt, WIDTH)

    fl = jnp.dot(hb, wfl_ref[...], preferred_element_type=F32)[:, 0:N_HEADS] + fb_ref[...]
    lf = -_softplus(-fl)
    lf_ref[...] = lf.reshape(bb, tt, N_HEADS)

    @pl.when(pl.program_id(1) == 0)
    def _():
        carry[...] = f0_ref[...]

    r = lax.broadcasted_iota(I32, (m, m), 0)
    c = lax.broadcasted_iota(I32, (m, m), 1)
    tri = jnp.logical_and(r // tt == c // tt, r >= c).astype(F32)
    cf = _dot_exact_rhs(tri, lf).reshape(bb, tt, N_HEADS) + carry[...]
    cf_ref[...] = cf
    carry[...] = cf[:, tt - 1:tt, :]

    gate_ref[...] =_sigmoid(jnp.dot(hb, wg_ref[...], preferred_element_type=F32)).reshape(bb, tt, GATE_COLS)


def _const_spec(shape, single_buffer=False):
    nd = len(shape)
    if single_buffer:
        return pl.BlockSpec(shape, lambda *_: (0,) * nd, pipeline_mode=pl.Buffered(1))
    return pl.BlockSpec(shape, lambda *_: (0,) * nd)


def _inproj(x, mod, g1, wr, wf, wfl, wg, qn, kn, fb, gmat, f0, bb, tt):
    b, t, d = x.shape
    grid = (b // bb, t // tt)

    def tok(cols):
        return pl.BlockSpec((bb, tt, cols), lambda i, j: (i, j, 0))

    out_cols = [(RWKV_COLS, F32), (WIDTH, BF16), (WIDTH, F32), (WIDTH, F32), (WIDTH, F32), (N_HEADS, F32),
                (N_HEADS, F32), (GATE_COLS, F32)]
    return pl.pallas_call(
        _inproj_kernel,
        grid=grid,
        in_specs=[tok(d),
                  pl.BlockSpec((bb, 6, d), lambda i, j: (i, 0, 0)),
                  _const_spec((1, d)),
                  _const_spec(wr.shape, True), _const_spec(wf.shape, True), _const_spec(wfl.shape),
                  _const_spec(wg.shape, True),
                  _const_spec((1, WIDTH)), _const_spec((1, WIDTH)), _const_spec((1, N_HEADS)),
                  _const_spec((WIDTH, WIDTH)),
                  pl.BlockSpec((bb, 1, N_HEADS), lambda i, j: (i, 0, 0))],
        out_specs=[tok(c) for c, _ in out_cols],
        out_shape=[jax.ShapeDtypeStruct((b, t, c), dt) for c, dt in out_cols],
        scratch_shapes=[pltpu.VMEM((bb, 1, N_HEADS), F32)],
        compiler_params=_cparams(("parallel", "arbitrary")),
        name="norm1_inproj",
    )(x, mod, g1, wr, wf, wfl, wg, qn, kn, fb, gmat, f0)


def _past_cumsum_kernel(x_ref, o_ref):
    x = x_ref[0]
    rows = x.shape[0]
    li = lax.broadcasted_iota(I32, (LANES, LANES), 0)
    lj = lax.broadcasted_iota(I32, (LANES, LANES), 1)
    same_head = (li % N_HEADS) == (lj % N_HEADS)
    within = jnp.logical_and(same_head, li // N_HEADS <= lj // N_HEADS).astype(BF16)
    xh, xm, xl = _split3(x)
    d2 = functools.partial(jnp.dot, preferred_element_type=F32)
    in_row = d2(xh, within) + (d2(xm, within) + d2(xl, within))
    sh = same_head.astype(BF16)
    row_tot = d2(xh, sh) + (d2(xm, sh) + d2(xl, sh))
    ri = lax.broadcasted_iota(I32, (rows, rows), 0)
    ci = lax.broadcasted_iota(I32, (rows, rows), 1)
    o_ref[0] = in_row + _dot_exact_rhs((ri > ci).astype(F32), row_tot)


def _past_cumsum(past_logf):
    b, p, h = past_logf.shape
    rows = p * h // LANES
    flat = past_logf.reshape(b, rows, LANES)
    out = pl.pallas_call(
        _past_cumsum_kernel,
        grid=(b,),
        in_specs=[pl.BlockSpec((1, rows, LANES), lambda i: (i, 0, 0))],
        out_specs=pl.BlockSpec((1, rows, LANES), lambda i: (i, 0, 0)),
        out_shape=jax.ShapeDtypeStruct((b, rows, LANES), F32),
        compiler_params=_cparams(("parallel",)),
        name="cache_logf_cumsum",
    )(flat)
    return out.reshape(b, p, h)


def _fox_kernel(*refs, n_past_blocks, tq):
    if n_past_blocks:
        (q_ref, fq_ref, sg_ref, kp_ref, vp_ref, fkp_ref, kn_ref, vn_ref, fkn_ref,
         o_ref, m_scr, l_scr, acc_scr) = refs
    else:
        q_ref, fq_ref, sg_ref, kn_ref, vn_ref, fkn_ref, o_ref, m_scr, l_scr, acc_scr = refs
    qi = pl.program_id(1)
    ki = pl.program_id(2)
    nk = pl.num_programs(2)

    @pl.when(ki == 0)
    def _():
        m_scr[...] = jnp.full(m_scr.shape, NEG_BIG, F32)
        l_scr[...] = jnp.zeros(l_scr.shape, F32)
        acc_scr[...] = jnp.zeros(acc_scr.shape, F32)

    lane_a = lax.broadcasted_iota(I32, (tq, LANES), 1) < HEAD_DIM

    def step(k_ref, v_ref, fk_ref, diag):
        tk = k_ref.shape[1]
        if diag:
            rq = lax.broadcasted_iota(I32, (tq, tk), 0)
            ck = lax.broadcasted_iota(I32, (tq, tk), 1)
            visible = ck <= rq
        fq_all = fq_ref[0]
        pairs = range(N_HEADS // 2)
        cols = [slice(j * LANES, (j + 1) * LANES) for j in pairs]
        scores = []
        for j in pairs:
            qj = q_ref[0, :, cols[j]]
            kb = k_ref[0, :, cols[j]].astype(BF16)
            for hh in range(2):
                h = 2 * j + hh
                qm = jnp.where(lane_a if hh == 0 else jnp.logical_not(lane_a), qj, jnp.zeros_like(qj))
                s = lax.dot_general(qm, kb, NT, preferred_element_type=F32)
                s = s + fq_all[:, h:h + 1] - fk_ref[0, h:h + 1, :]
                if diag:
                    s = jnp.where(visible, s, NEG_BIG)
                scores.append(s)
        alphas, probs = [], []
        for h in range(N_HEADS):
            m_old = m_scr[h]
            m_new = jnp.maximum(m_old, jnp.max(scores[h], axis=-1, keepdims=True))
            alpha = jnp.exp(m_old - m_new)
            p = jnp.exp(scores[h] - m_new)
            l_scr[h] = alpha * l_scr[h] + jnp.sum(p, axis=-1, keepdims=True)
            m_scr[h] = m_new
            alphas.append(alpha)
            probs.append(p.astype(BF16))
        for j in pairs:
            vb = v_ref[0, :, cols[j]].astype(BF16)
            pv0 = jnp.dot(probs[2 * j], vb, preferred_element_type=F32)
            pv1 = jnp.dot(probs[2 * j + 1], vb, preferred_element_type=F32)
            acc_scr[:, cols[j]] = (acc_scr[:, cols[j]] * jnp.where(lane_a, alphas[2 * j], alphas[2 * j + 1])
                                   + jnp.where(lane_a, pv0, pv1))

    if n_past_blocks:
        @pl.when(ki < n_past_blocks)
        def _():
            step(kp_ref, vp_ref, fkp_ref, False)

    kn = ki - n_past_blocks

    @pl.when(jnp.logical_and(kn >= 0, kn < qi))
    def _():
        step(kn_ref, vn_ref, fkn_ref, False)

    @pl.when(kn == qi)
    def _():
        step(kn_ref, vn_ref, fkn_ref, True)

    @pl.when(ki == nk - 1)
    def _():
        for j in range(N_HEADS // 2):
            cols = slice(j * LANES, (j + 1) * LANES)
            l = jnp.where(lane_a, l_scr[2 * j], l_scr[2 * j + 1])
            o_ref[0, :, cols] = acc_scr[:, cols] / l * sg_ref[0, :, cols]


def _fox_attention(q, fq, sg, k_new, v_new, fk_new_t, past=None, tq=512, tk_past=512):
    b, t, _ = q.shape
    nq = t // tq
    n_past_blocks = 0 if past is None else past[0].shape[1] // tk_past
    nk = n_past_blocks + nq

    def new_idx(i, qi, ki):
        return jnp.clip(ki - n_past_blocks, 0, qi)

    in_specs = [pl.BlockSpec((1, tq, WIDTH), lambda i, qi, ki: (i, qi, 0)),
                pl.BlockSpec((1, tq, N_HEADS), lambda i, qi, ki: (i, qi, 0)),
                pl.BlockSpec((1, tq, WIDTH), lambda i, qi, ki: (i, qi, 0))]
    args = [q, fq, sg]
    if n_past_blocks:
        def past_idx(i, qi, ki):
            return jnp.minimum(ki, n_past_blocks - 1)
        in_specs += [pl.BlockSpec((1, tk_past, WIDTH), lambda i, qi, ki: (i, past_idx(i, qi, ki), 0)),
                     pl.BlockSpec((1, tk_past, WIDTH), lambda i, qi, ki: (i, past_idx(i, qi, ki), 0)),
                     pl.BlockSpec((1, N_HEADS, tk_past), lambda i, qi, ki: (i, 0, past_idx(i, qi, ki)))]
        args += list(past)
    in_specs += [pl.BlockSpec((1, tq, WIDTH), lambda i, qi, ki: (i, new_idx(i, qi, ki), 0)),
                 pl.BlockSpec((1, tq, WIDTH), lambda i, qi, ki: (i, new_idx(i, qi, ki), 0)),
                 pl.BlockSpec((1, N_HEADS, tq), lambda i, qi, ki: (i, 0, new_idx(i, qi, ki)))]
    args += [k_new, v_new, fk_new_t]
    return pl.pallas_call(
        functools.partial(_fox_kernel, n_past_blocks=n_past_blocks, tq=tq),
        grid=(b, nq, nk),
        in_specs=in_specs,
        out_specs=pl.BlockSpec((1, tq, WIDTH), lambda i, qi, ki: (i, qi, 0)),
        out_shape=jax.ShapeDtypeStruct((b, t, WIDTH), F32),
        scratch_shapes=[pltpu.VMEM((N_HEADS, tq, 1), F32), pltpu.VMEM((N_HEADS, tq, 1), F32),
                        pltpu.VMEM((tq, WIDTH), F32)],
        compiler_params=_cparams(("parallel", "parallel", "arbitrary")),
        name="fox_attention",
    )(*args)


def _rwkv_kernel(p_ref, sh0_ref, s0_ref, mu_ref, w0_ref, wb_ref, a0_ref, ab_ref, gb_ref, kk_ref, ka_ref, rk_ref,
                 lnw_ref, lnb_ref, gm_ref, y_ref, st_ref, sht_ref, z_scr, prev_scr, *, c):
    t = pl.program_id(1)
    nt = pl.num_programs(1)
    n_rows = p_ref.shape[1]
    n_chunks = n_rows // c

    def head_block(h):
        lo = (h % HEADS_PER_GROUP) * HEAD_DIM
        return h // HEADS_PER_GROUP, slice(lo, lo + HEAD_DIM)

    @pl.when(t == 0)
    def _():
        z_scr[...] = jnp.zeros(z_scr.shape, F32)
        for h in range(N_HEADS):
            i, blk = head_block(h)
            z_scr[i, blk, blk] = s0_ref[0, h]
        prev_scr[...] = sh0_ref[0]

    p = p_ref[0]
    row = lax.broadcasted_iota(I32, p.shape, 0)
    prev = jnp.where(row == 0, prev_scr[...], pltpu.roll(p, 1, 0))
    last = p[n_rows - 1:n_rows, :]
    prev_scr[...] = last
    sht_ref[0] = last

    pm = p + (prev - p) * mu_ref[...]
    r = pm[:, 0:WIDTH]
    k = pm[:, WIDTH:2 * WIDTH]
    v = pm[:, 2 * WIDTH:3 * WIDTH]
    o1 = 3 * WIDTH
    wd = pm[:, o1:o1 + DECAY_LORA]
    ad = pm[:, o1 + DECAY_LORA:o1 + DECAY_LORA + ICLR_LORA]
    gd = pm[:, o1 + DECAY_LORA + ICLR_LORA:RWKV_COLS]

    w = -_softplus(-(w0_ref[...] + _dot(jnp.tanh(wd), wb_ref[...]))) - 0.5
    lw = -jnp.exp(w)
    a = _sigmoid(a0_ref[...] + _dot(ad, ab_ref[...]))
    g = _dot(_sigmoid(gd), gb_ref[...])
    kk = k * kk_ref[...]
    kk = kk / jnp.maximum(jnp.sqrt(_gsum(kk * kk, gm_ref)), L2_EPS)
    kf = k * (1.0 + (a - 1.0) * ka_ref[...])

    ri = lax.broadcasted_iota(I32, (n_rows, n_rows), 0)
    ci = lax.broadcasted_iota(I32, (n_rows, n_rows), 1)
    same_chunk = (ri // c) == (ci // c)
    cum = _dot_exact_rhs(jnp.logical_and(same_chunk, ri >= ci).astype(F32), lw)
    cum_last = jnp.concatenate([jnp.broadcast_to(cum[(j + 1) * c - 1:(j + 1) * c, :], (c, WIDTH))
                                for j in range(n_chunks)], axis=0)
    r_t = r * jnp.exp(cum)
    a_t = -kk * jnp.exp(cum - lw)
    inv = jnp.exp(-cum)
    b_t = kk * a * inv
    k_t = kf * inv
    to_end = jnp.exp(cum_last - cum)
    b_e = kk * a * to_end
    k_e = kf * to_end
    g_end = jnp.exp(cum_last)

    hg = HEADS_PER_GROUP
    gw = hg * HEAD_DIM
    log_c = int(math.log2(c))
    t_idx = lax.broadcasted_iota(I32, (c, hg * c), 0)
    s_idx = lax.broadcasted_iota(I32, (c, hg * c), 1) & (c - 1)
    strict = s_idx < t_idx
    lower = s_idx <= t_idx
    eye = (s_idx == t_idx).astype(F32)
    rb = lax.broadcasted_iota(I32, (hg * c, gw), 0) >> log_c
    mask_kv = rb == (lax.broadcasted_iota(I32, (hg * c, gw), 1) >> int(math.log2(HEAD_DIM)))
    rs = lax.broadcasted_iota(I32, (hg * c, hg * c), 0) >> log_c
    mask_ss = rs == (lax.broadcasted_iota(I32, (hg * c, hg * c), 1) >> log_c)
    ng = N_HEADS // hg
    cat = functools.partial(jnp.concatenate, axis=0)
    units = [(slice(j * c, (j + 1) * c), slice(i * gw, (i + 1) * gw)) for j in range(n_chunks) for i in range(ng)]
    nu = len(units)

    def mm1(a, b_bd):
        return jnp.dot(a.astype(BF16), b_bd, preferred_element_type=F32)

    def dg(a, b, dims):
        return lax.dot_general(a.astype(BF16), b.astype(BF16), dims, preferred_element_type=F32)

    def bd1(x, mask):
        tiled = jnp.concatenate([x.astype(BF16)] * hg, axis=0)
        return jnp.where(mask, tiled, jnp.zeros_like(tiled))

    ar = [cat([a_t[rs_, s], r_t[rs_, s]]).astype(BF16) for rs_, s in units]
    ab = [dg(ar[n], bd1(b_t[units[n]], mask_kv), NT) for n in range(nu)]
    ak = [dg(ar[n], bd1(k_t[units[n]], mask_kv), NT) for n in range(nu)]
    l_ab = [jnp.where(strict, m[:c], 0.0) for m in ab]
    l_rb = [jnp.where(lower, m[c:], 0.0) for m in ab]
    l_ak = [jnp.where(strict, m[:c], 0.0) for m in ak]
    l_rk = [jnp.where(lower, m[c:], 0.0) for m in ak]
    tinv = [eye + m for m in l_ab]
    pw = [mm1(m, bd1(m, mask_ss)) for m in l_ab]
    for _ in range(1, log_c - 1):
        res = [mm1(cat([tinv[n], pw[n]]), bd1(pw[n], mask_ss)) for n in range(nu)]
        tinv = [tinv[n] + res[n][:c] for n in range(nu)]
        pw = [m[c:] for m in res]
    tinv = [tinv[n] + mm1(tinv[n], bd1(pw[n], mask_ss)) for n in range(nu)]
    av = [mm1(cat([l_ak[n], l_rk[n]]), bd1(v[units[n]], mask_kv)) for n in range(nu)]
    ue = [cat([b_e[units[n]], k_e[units[n]]]).astype(BF16) for n in range(nu)]

    def wide(fn, x):
        return [fn(x[:, :gw]), fn(x[:, gw:])]

    def bd1w(x):
        return jnp.concatenate(wide(lambda h_: bd1(h_, mask_kv), x), axis=1)

    rhs = [jnp.concatenate([a_t[units[n]], av[n][:c]], axis=1) for n in range(nu)]
    x0 = [mm1(tinv[n], bd1w(rhs[n])) for n in range(nu)]
    resid = [rhs[n] - (x0[n] - mm1(l_ab[n], bd1w(x0[n]))) for n in range(nu)]
    sol = [x0[n] + mm1(tinv[n], bd1w(resid[n])) for n in range(nu)]
    lift = [mm1(l_rb[n], bd1w(sol[n])) for n in range(nu)]
    lhs_s = [cat([sol[n][:, :gw], r_t[units[n]] + lift[n][:, :gw]]).astype(BF16) for n in range(nu)]
    u_loc = [sol[n][:, gw:] for n in range(nu)]
    o_loc = [av[n][c:] + lift[n][:, gw:] for n in range(nu)]

    zr = lax.broadcasted_iota(I32, (gw, gw), 0) >> int(math.log2(HEAD_DIM))
    zmask = zr == (lax.broadcasted_iota(I32, (gw, gw), 1) >> int(math.log2(HEAD_DIM)))
    z = [z_scr[i] for i in range(ng)]
    o_rows = []
    for j in range(n_chunks):
        o_grp = []
        for i in range(ng):
            n = j * ng + i
            rs_, s = units[n]
            sz = dg(lhs_s[n], z[i], NT)
            u = sz[:c] + u_loc[n]
            o_grp.append(sz[c:] + o_loc[n])
            upd = dg(cat([u, v[rs_, s]]), ue[n], TN)
            z[i] = z[i] * g_end[j * c:j * c + 1, s] + jnp.where(zmask, upd, 0.0)
        o_rows.append(jnp.concatenate(o_grp, axis=1))
    for i in range(ng):
        z_scr[i] = z[i]

    o = cat(o_rows)
    inv_hd = 1.0 / HEAD_DIM
    dlt = o - _gsum(o, gm_ref) * inv_hd
    var = _gsum(dlt * dlt, gm_ref) * inv_hd
    on = dlt * lax.rsqrt(var + RWKV_GN_EPS) * lnw_ref[...] + lnb_ref[...]
    bonus = _gsum(r * kf * rk_ref[...], gm_ref) * v
    y_ref[0] = (on + bonus) * g

    @pl.when(t == nt - 1)
    def _():
        for h in range(N_HEADS):
            i, blk = head_block(h)
            st_ref[0, h] = z_scr[i, blk, blk]


def _rwkv(p, shift0, s0, prm, gmat, chunk, chunks_per_step):
    b, t, _ = p.shape
    row = lambda n: _const_spec((1, n))
    rows = chunk * chunks_per_step
    return pl.pallas_call(
        functools.partial(_rwkv_kernel, c=chunk),
        grid=(b, t // rows),
        in_specs=[pl.BlockSpec((1, rows, RWKV_COLS), lambda i, j: (i, j, 0)),
                  pl.BlockSpec((1, 1, RWKV_COLS), lambda i, j: (i, 0, 0)),
                  pl.BlockSpec((1, N_HEADS, HEAD_DIM, HEAD_DIM), lambda i, j: (i, 0, 0, 0)),
                  row(RWKV_COLS), row(WIDTH), _const_spec((DECAY_LORA, WIDTH)), row(WIDTH),
                  _const_spec((ICLR_LORA, WIDTH)), _const_spec((GATE_LORA, WIDTH)),
                  row(WIDTH), row(WIDTH), row(WIDTH), row(WIDTH), row(WIDTH), _const_spec((WIDTH, WIDTH))],
        out_specs=[pl.BlockSpec((1, rows, WIDTH), lambda i, j: (i, j, 0)),
                   pl.BlockSpec((1, N_HEADS, HEAD_DIM, HEAD_DIM), lambda i, j: (i, 0, 0, 0)),
                   pl.BlockSpec((1, 1, RWKV_COLS), lambda i, j: (i, 0, 0))],
        out_shape=[jax.ShapeDtypeStruct((b, t, WIDTH), F32),
                   jax.ShapeDtypeStruct((b, N_HEADS, HEAD_DIM, HEAD_DIM), F32),
                   jax.ShapeDtypeStruct((b, 1, RWKV_COLS), F32)],
        scratch_shapes=[pltpu.VMEM((N_HEADS // HEADS_PER_GROUP, HEADS_PER_GROUP * HEAD_DIM,
                                    HEADS_PER_GROUP * HEAD_DIM), F32),
                        pltpu.VMEM((1, RWKV_COLS), F32)],
        compiler_params=_cparams(("parallel", "arbitrary")),
        name="rwkv7_mix",
    )(p, shift0, s0, prm["mu"], prm["w0"], prm["wb"], prm["a0"], prm["ab"], prm["gb"], prm["kk"], prm["ka"],
      prm["rk"], prm["lnw"], prm["lnb"], gmat)


def _merge_kernel(x_ref, ya_ref, yb_ref, gate_ref, mod_ref, g2_ref, woa_ref, wob_ref, wo_ref, wrh_ref, wrl_ref,
                  *rest):
    x1_ref, h2_ref, lg_ref = rest[-3:]
    bb, tt, d = x_ref.shape
    m = bb * tt
    gate = gate_ref[...].reshape(m, GATE_COLS)
    merged = (gate[:, 0:d] * _dot(ya_ref[...].reshape(m, WIDTH), woa_ref[...])
              + gate[:, d:2 * d] * _dot(yb_ref[...].reshape(m, WIDTH), wob_ref[...]))
    x1 = x_ref[...] + mod_ref[:, 2:3, :] * _dot(merged, wo_ref[...]).reshape(bb, tt, d)
    x1_ref[...] = x1
    ms = jnp.mean(x1 * x1, axis=-1, keepdims=True)
    h2 = x1 * lax.rsqrt(ms + RMS_EPS) * g2_ref[...]
    h2 = (h2 * (1.0 + mod_ref[:, 4:5, :]) + mod_ref[:, 3:4, :]).reshape(m, d)
    h2_ref[...] = _pack_bf16_pairs(h2)
    lg_ref[...] = _mm3((wrh_ref[...], wrl_ref[...]), _split2(h2), NT)


def _merge(x, ya, yb, gate, mod, w, n_total, row_offset, shared=None):
    b, t, d = x.shape
    bb, tt = _token_blocks(b, t, MOE_TOKEN_ROWS)
    nt = t // tt
    m = bb * tt
    off = row_offset // m

    def tok(cols):
        return pl.BlockSpec((bb, tt, cols), lambda i, j: (i, j, 0))

    in_specs = [tok(d), tok(WIDTH), tok(WIDTH), tok(GATE_COLS),
                pl.BlockSpec((bb, 6, d), lambda i, j: (i, 0, 0)),
                _const_spec((1, d)), _const_spec((WIDTH, d)), _const_spec((WIDTH, d)), _const_spec((d, d)),
                _const_spec((N_EXPERTS, d)), _const_spec((N_EXPERTS, d))]
    args = [x, ya, yb, gate, mod.reshape(b, 6, d), w["g2"], w["w_oa"], w["w_ob"], w["w_o"], w["wr_hi"], w["wr_lo"]]
    aliases = {}
    if shared is not None:
        aliases = {len(args): 1, len(args) + 1: 2}
        in_specs += [pl.BlockSpec(memory_space=pl.ANY), pl.BlockSpec(memory_space=pl.ANY)]
        args += list(shared)
    return pl.pallas_call(
        _merge_kernel,
        grid=(b // bb, nt),
        in_specs=in_specs,
        out_specs=[tok(d), pl.BlockSpec((m, d // 2), lambda i, j: (off + i * nt + j, 0)),
                   pl.BlockSpec((N_EXPERTS, m), lambda i, j: (0, off + i * nt + j))],
        out_shape=[jax.ShapeDtypeStruct((b, t, d), F32), jax.ShapeDtypeStruct((n_total, d // 2), I32),
                   jax.ShapeDtypeStruct((N_EXPERTS, n_total), F32)],
        input_output_aliases=aliases,
        compiler_params=_cparams(("parallel", "parallel")),
        name="merge_norm2_router",
    )(*args)


def _route_kernel(lg_ref, bias_ref, idx_ref, wt_ref, rank_ref, cnt_ref, carry):
    @pl.when(pl.program_id(0) == 0)
    def _():
        carry[...] = jnp.zeros(carry.shape, F32)

    tm = lg_ref.shape[1]
    scores = _sigmoid(lg_ref[...])
    sel = scores + bias_ref[...]
    row = lax.broadcasted_iota(I32, (N_EXPERTS, tm), 0)
    neg_inf = -jnp.inf

    def first_argmax(vals, rows):
        mx = jnp.max(vals, axis=0, keepdims=True)
        return mx, jnp.min(jnp.where(vals == mx, rows, N_EXPERTS), axis=0, keepdims=True)

    gslices = [slice(g * EXPERTS_PER_GROUP, (g + 1) * EXPERTS_PER_GROUP) for g in range(N_GROUPS)]
    gs = []
    row_g = lax.broadcasted_iota(I32, (EXPERTS_PER_GROUP, tm), 0)
    for sl in gslices:
        m1, i1 = first_argmax(sel[sl], row_g)
        m2 = jnp.max(jnp.where(row_g == i1, neg_inf, sel[sl]), axis=0, keepdims=True)
        gs.append(m1 + m2)
    kept = []
    for g in range(N_GROUPS):
        beaten = jnp.zeros((1, tm), I32)
        for o in range(N_GROUPS):
            if o != g:
                wins = (gs[o] >= gs[g]) if o < g else (gs[o] > gs[g])
                beaten = beaten + wins.astype(I32)
        kept.append(jnp.where(beaten < TOPK_GROUPS, sel[gslices[g]], neg_inf))
    cur = jnp.concatenate(kept, axis=0)

    idxs, ws = [], []
    picked = jnp.zeros((N_EXPERTS, tm), F32)
    for _ in range(TOP_K):
        _, ik = first_argmax(cur, row)
        hit = row == ik
        idxs.append(ik)
        ws.append(jnp.sum(jnp.where(hit, scores, 0.0), axis=0, keepdims=True))
        cur = jnp.where(hit, neg_inf, cur)
        picked = jnp.where(hit, 1.0, picked)
    wsum = ws[0]
    for k in range(1, TOP_K):
        wsum = wsum + ws[k]

    r = lax.broadcasted_iota(I32, (tm, tm), 0)
    c = lax.broadcasted_iota(I32, (tm, tm), 1)
    before = jnp.dot(picked.astype(BF16), (r < c).astype(BF16), preferred_element_type=F32) + carry[...]
    carry[...] = carry[...] + jnp.sum(picked, axis=1, keepdims=True)
    cnt_ref[...] = carry[...]

    kk = lax.broadcasted_iota(I32, (TOP_K, tm), 0)
    idx_o = jnp.zeros((TOP_K, tm), I32)
    wt_o = jnp.zeros((TOP_K, tm), F32)
    rank_o = jnp.zeros((TOP_K, tm), F32)
    for k in range(TOP_K):
        rk = jnp.sum(jnp.where(row == idxs[k], before, 0.0), axis=0, keepdims=True)
        idx_o = jnp.where(kk == k, idxs[k], idx_o)
        wt_o = jnp.where(kk == k, ws[k] / wsum * ROUTED_SCALE, wt_o)
        rank_o = jnp.where(kk == k, rk, rank_o)
    idx_ref[...] = idx_o
    wt_ref[...] = wt_o
    rank_ref[...] = rank_o.astype(I32)


def _route(logits_t, bias_col, tm):
    n = logits_t.shape[1]
    tokk = pl.BlockSpec((TOP_K, tm), lambda i: (0, i))
    return pl.pallas_call(
        _route_kernel,
        grid=(n // tm,),
        in_specs=[pl.BlockSpec((N_EXPERTS, tm), lambda i: (0, i)), _const_spec((N_EXPERTS, 1))],
        out_specs=[tokk, tokk, tokk, _const_spec((N_EXPERTS, 1))],
        out_shape=[jax.ShapeDtypeStruct((TOP_K, n), I32), jax.ShapeDtypeStruct((TOP_K, n), F32),
                   jax.ShapeDtypeStruct((TOP_K, n), I32), jax.ShapeDtypeStruct((N_EXPERTS, 1), F32)],
        scratch_shapes=[pltpu.VMEM((N_EXPERTS, 1), F32)],
        compiler_params=_cparams(("arbitrary",)),
        name="route_topk",
    )(logits_t, bias_col)


def _plan_kernel(cnt_ref, start_ref, be_ref, valid_ref, nu_ref, *, blk):
    cnt = cnt_ref[...]
    padded = jnp.ceil(cnt * (1.0 / blk)) * blk
    e_r = lax.broadcasted_iota(I32, (N_EXPERTS, N_EXPERTS), 0)
    e_c = lax.broadcasted_iota(I32, (N_EXPERTS, N_EXPERTS), 1)
    incl = (e_r <= e_c).astype(BF16)
    ph, pm, plo = _split3(jnp.broadcast_to(padded, (8, N_EXPERTS)))
    d2 = functools.partial(jnp.dot, preferred_element_type=F32)
    pad_end = (d2(ph, incl) + (d2(pm, incl) + d2(plo, incl)))[0:1, :]
    pad_start = pad_end - padded
    start_ref[...] = pad_start.astype(I32)
    total = jnp.max(pad_end, axis=-1, keepdims=True)
    nu_ref[...] = jnp.broadcast_to(total * (1.0 / blk), (1, N_EXPERTS)).astype(I32)
    nb = be_ref.shape[0]
    first = (lax.broadcasted_iota(I32, (nb, N_EXPERTS), 0) * blk).astype(F32)
    lane = lax.broadcasted_iota(I32, (nb, N_EXPERTS), 1)
    inside = jnp.logical_and(pad_start <= first, first < pad_end)
    be_ref[...] = jnp.sum(jnp.where(inside, lane, 0), axis=-1, keepdims=True)
    rows = jnp.minimum(pad_start + cnt - first, float(blk))
    valid_ref[...] = jnp.sum(jnp.where(inside, rows, 0.0), axis=-1, keepdims=True).astype(I32)


def _plan(counts, n_blocks, blk):
    return pl.pallas_call(
        functools.partial(_plan_kernel, blk=blk),
        out_shape=[jax.ShapeDtypeStruct((1, N_EXPERTS), I32), jax.ShapeDtypeStruct((n_blocks, 1), I32),
                   jax.ShapeDtypeStruct((n_blocks, 1), I32), jax.ShapeDtypeStruct((1, N_EXPERTS), I32)],
        compiler_params=pltpu.CompilerParams(vmem_limit_bytes=VMEM_LIMIT),
        name="dispatch_plan",
    )(counts)


def _dest_kernel(idx_ref, rank_ref, start_ref, dest_ref):
    tm = idx_ref.shape[1]
    row = lax.broadcasted_iota(I32, (N_EXPERTS, tm), 0)
    kk = lax.broadcasted_iota(I32, (TOP_K, tm), 0)
    idx = idx_ref[...]
    base = jnp.zeros((TOP_K, tm), I32)
    for k in range(TOP_K):
        bk = jnp.sum(jnp.where(row == idx[k:k + 1, :], start_ref[...], 0), axis=0, keepdims=True)
        base = jnp.where(kk == k, bk, base)
    dest_ref[...] = base + rank_ref[...]


def _dest(idx, rank, pad_start_col, tm):
    n = idx.shape[1]
    tokk = pl.BlockSpec((TOP_K, tm), lambda i: (0, i))
    return pl.pallas_call(
        _dest_kernel,
        grid=(n // tm,),
        in_specs=[tokk, tokk, _const_spec((N_EXPERTS, 1))],
        out_specs=tokk,
        out_shape=jax.ShapeDtypeStruct((TOP_K, n), I32),
        compiler_params=_cparams(("parallel",)),
        name="dispatch_dest",
    )(idx, rank, pad_start_col)


def _dispatch(h2p, dest_t, n_slots):
    n, wp = h2p.shape
    half = wp // SC_ROW_SPLIT
    window = SC_SCATTER_WINDOW
    mesh = plsc.VectorSubcoreMesh(core_axis_name="core", subcore_axis_name="subcore")
    out = jax.ShapeDtypeStruct((n_slots, half), h2p.dtype)

    @functools.partial(pl.kernel, out_type=[out] * SC_ROW_SPLIT, mesh=mesh, scratch_types=[])
    def scatter(rows_hbm, idx_hbm, *outs):
        for c, out_hbm in enumerate(outs):
            def body(rows_vmem, idx_vmem, out_hbm=out_hbm):
                for k in range(TOP_K):
                    pltpu.sync_copy(rows_vmem, out_hbm.at[idx_vmem.at[k]])

            pltpu.emit_pipeline(
                body, grid=(n // window,),
                in_specs=[pl.BlockSpec((window, half), index_map=lambda i, c=c: (i, c)),
                          pl.BlockSpec((TOP_K, window), index_map=lambda i: (0, i))],
                out_specs=[], core_axis_name=("core", "subcore"), dimension_semantics=(pltpu.PARALLEL,),
            )(rows_hbm, idx_hbm)

    return scatter(h2p, dest_t)


def _expert_kernel(be_ref, valid_ref, nu_ref, xa_ref, xb_ref, wg_ref, wu_ref, wd_ref, *rest):
    y_refs, (wg_b, wu_b, wd_b) = rest[:SC_ROW_SPLIT], rest[SC_ROW_SPLIT:]
    i = pl.program_id(0)
    nv = valid_ref[i]
    new_expert = jnp.logical_or(i == 0, be_ref[i] != be_ref[jnp.maximum(i - 1, 0)])

    @pl.when(jnp.logical_and(nv > 0, new_expert))
    def _():
        wg_b[...] = wg_ref[0].astype(BF16)
        wu_b[...] = wu_ref[0].astype(BF16)
        wd_b[...] = wd_ref[0].astype(BF16)

    @pl.when(nv > 0)
    def _():
        blk = xa_ref.shape[0]
        rows = lax.broadcasted_iota(I32, (blk, 1), 0)
        packed = jnp.concatenate([xa_ref[...], xb_ref[...]], axis=1)
        x = _unpack_bf16_pairs(jnp.where(rows < nv, packed, 0))
        hg = jnp.dot(x, wg_b[...], preferred_element_type=F32)
        hu = jnp.dot(x, wu_b[...], preferred_element_type=F32)
        y = _pack_bf16_pairs(jnp.dot((_silu(hg) * hu).astype(BF16), wd_b[...], preferred_element_type=F32))
        cw = y.shape[1] // SC_ROW_SPLIT
        for c, y_ref in enumerate(y_refs):
            y_ref[...] = y[:, c * cw:(c + 1) * cw]


def _experts(xs, block_e, valid, n_used, w_eg, w_eu, w_ed, blk):
    xa, xb = xs
    n_slots, packed = xa.shape
    d = w_eg.shape[1]
    n_blocks = n_slots // blk

    def row_blk(i, be, valid, nu):
        return (jnp.minimum(i, nu[0] - 1), 0)

    def w_blk(i, be, valid, nu):
        return (be[i], 0, 0)

    return pl.pallas_call(
        _expert_kernel,
        grid_spec=pltpu.PrefetchScalarGridSpec(
            num_scalar_prefetch=3,
            grid=(n_blocks,),
            in_specs=[pl.BlockSpec((blk, packed), row_blk), pl.BlockSpec((blk, packed), row_blk),
                      pl.BlockSpec((1, d, D_EXPERT), w_blk), pl.BlockSpec((1, d, D_EXPERT), w_blk),
                      pl.BlockSpec((1, D_EXPERT, d), w_blk)],
            out_specs=[pl.BlockSpec((blk, packed), row_blk)] * SC_ROW_SPLIT,
            scratch_shapes=[pltpu.VMEM((d, D_EXPERT), BF16), pltpu.VMEM((d, D_EXPERT), BF16),
                            pltpu.VMEM((D_EXPERT, d), BF16)]),
        out_shape=[jax.ShapeDtypeStruct((n_slots, packed), I32)] * SC_ROW_SPLIT,
        compiler_params=_cparams(("arbitrary",)),
        name="moe_experts",
    )(block_e, valid, n_used, xa, xb, w_eg, w_eu, w_ed)


def _combine_gather(ys, dest_t):
    k, n = dest_t.shape
    cw = ys[0].shape[1]
    window = SC_SCATTER_WINDOW
    mesh = plsc.VectorSubcoreMesh(core_axis_name="core", subcore_axis_name="subcore")

    @functools.partial(pl.kernel, out_type=jax.ShapeDtypeStruct((k * n, cw * len(ys)), ys[0].dtype), mesh=mesh,
                       scratch_types=[])
    def gather(*refs):
        y_refs, idx_hbm, out_hbm = refs[:len(ys)], refs[len(ys)], refs[len(ys) + 1]
        for c, y_hbm in enumerate(y_refs):
            def body(idx_vmem, out_vmem, y_hbm=y_hbm):
                pltpu.sync_copy(y_hbm.at[idx_vmem.at[0]], out_vmem)

            pltpu.emit_pipeline(
                body, grid=(k * n // window,),
                in_specs=[pl.BlockSpec((1, window), index_map=lambda i: (0, i))],
                out_specs=[pl.BlockSpec((window, cw), index_map=lambda i, c=c: (i, c))],
                core_axis_name=("core", "subcore"), dimension_semantics=(pltpu.PARALLEL,),
            )(idx_hbm, out_hbm)

    return gather(*ys, dest_t.reshape(1, k * n)).reshape(k, n, cw * len(ys))


def _final_kernel(x1_ref, h2_ref, wt_ref, mod_ref, wsg_ref, wsu_ref, wsd_ref, yg_ref, o_ref):
    bb, tt, d = x1_ref.shape
    hb = _unpack_bf16_pairs(h2_ref[...])
    hg = jnp.dot(hb, wsg_ref[...], preferred_element_type=F32)
    hu = jnp.dot(hb, wsu_ref[...], preferred_element_type=F32)
    ffn = _dot(_silu(hg) * hu, wsd_ref[...])
    wt = wt_ref[...]
    for k in range(TOP_K):
        ffn = ffn + wt[:, k:k + 1] * _unpack_bf16_pairs(yg_ref[k]).astype(F32)
    o_ref[...] = x1_ref[...] + mod_ref[:, 5:6, :] * ffn.reshape(bb, tt, d)


def _final(x1, h2_all, wts_all, y_tok, mod, w, row_offset):
    b, t, d = x1.shape
    bb, tt = _token_blocks(b, t, MOE_TOKEN_ROWS)
    nt = t // tt
    m = bb * tt
    off = row_offset // m

    def flat_idx(i, j):
        return off + i * nt + j

    return pl.pallas_call(
        _final_kernel,
        grid=(b // bb, nt),
        in_specs=[pl.BlockSpec((bb, tt, d), lambda i, j: (i, j, 0)),
                  pl.BlockSpec((m, h2_all.shape[1]), lambda i, j: (flat_idx(i, j), 0)),
                  pl.BlockSpec((m, TOP_K), lambda i, j: (flat_idx(i, j), 0)),
                  pl.BlockSpec((bb, 6, d), lambda i, j: (i, 0, 0)),
                  _const_spec((d, D_EXPERT)), _const_spec((d, D_EXPERT)), _const_spec((D_EXPERT, d)),
                  pl.BlockSpec((TOP_K, m, y_tok.shape[2]), lambda i, j: (0, flat_idx(i, j), 0))],
        out_specs=pl.BlockSpec((bb, tt, d), lambda i, j: (i, j, 0)),
        out_shape=jax.ShapeDtypeStruct((b, t, d), F32),
        compiler_params=_cparams(("parallel", "parallel")),
        name="moe_combine_final",
    )(x1, h2_all, wts_all, mod.reshape(b, 6, d), w["w_sg"], w["w_su"], w["w_sd"], y_tok)


def _largest_tile(n, candidates):
    return next(c for c in candidates if n % c == 0)


def _moe_routed(h2_all, logits_all, w, blk=EXPERT_BLOCK_ROWS):
    n = h2_all.shape[0]
    n_blocks = (n * TOP_K + N_EXPERTS * (blk - 1)) // blk + 1
    n_blocks = (n_blocks + 7) // 8 * 8
    idx, wts_t, rank, counts = _route(logits_all, w["router_bias"], _largest_tile(n, (512, 256)))
    pad_start, block_e, valid, n_used = _plan(counts.reshape(1, N_EXPERTS), n_blocks, blk)
    block_e = block_e.reshape(n_blocks)
    valid = valid.reshape(n_blocks)
    n_used = n_used[0, 0:1]
    dest_t = _dest(idx, rank, pad_start.reshape(N_EXPERTS, 1), _largest_tile(n, (1024, 512, 256)))
    xs = _dispatch(h2_all, dest_t, n_blocks * blk)
    ys = _experts(xs, block_e, valid, n_used, w["w_eg"], w["w_eu"], w["w_ed"], blk)
    return _combine_gather(ys, dest_t), jnp.transpose(wts_t)


def _prep(raw):
    p = {k: v[0] for k, v in raw.items()}
    w_in = p["w_in"]
    o_fox = RWKV_COLS
    o_fl = o_fox + FOX_MAIN_COLS
    o_gate = o_fl + N_HEADS
    row = lambda a: a.reshape(1, -1)
    return dict(
        w_ada=p["w_ada"], b_ada=p["b_ada"],
        g1=row(p["norm1_g"]), g2=row(p["norm2_g"]),
        wr=w_in[:, :o_fox].astype(BF16),
        wf=w_in[:, o_fox:o_fl].astype(BF16),
        wfl=jnp.pad(w_in[:, o_fl:o_gate], ((0, 0), (0, LANES - N_HEADS))).astype(BF16),
        wg=w_in[:, o_gate:].astype(BF16),
        qn=row(jnp.tile(p["fox_q_norm"], N_HEADS)), kn=row(jnp.tile(p["fox_k_norm"], N_HEADS)),
        fb=row(p["fox_f_bias"]),
        gmat=_group_ones(),
        rwkv=dict(mu=row(p["rwkv_mu"]), w0=row(p["rwkv_w0"]), wb=p["rwkv_w_lora_b"], a0=row(p["rwkv_a0"]),
                  ab=p["rwkv_a_lora_b"], gb=p["rwkv_g_lora_b"], kk=row(p["rwkv_k_k"]), ka=row(p["rwkv_k_a"]),
                  rk=row(p["rwkv_r_k"]), lnw=row(p["rwkv_ln_w"]), lnb=row(p["rwkv_ln_b"])),
        w_oa=p["w_out_rwkv"].astype(BF16), w_ob=p["w_out_fox"].astype(BF16), w_o=p["w_out"].astype(BF16),
        wr_hi=p["w_router"].T.astype(BF16),
        wr_lo=(p["w_router"] - p["w_router"].astype(BF16).astype(F32)).T.astype(BF16),
        router_bias=p["router_bias"].reshape(N_EXPERTS, 1),
        w_eg=p["w_exp_gate"], w_eu=p["w_exp_up"], w_ed=p["w_exp_down"],
        w_sg=p["w_sh_gate"].astype(BF16), w_su=p["w_sh_up"].astype(BF16), w_sd=p["w_sh_down"].astype(BF16),
    )


def _token_blocks(b, t, rows=256):
    if t >= rows:
        return 1, rows
    bb = max(1, min(b, 256 // t))
    while b % bb:
        bb -= 1
    return bb, t


def _mix_path(x, mod, shift0, wkv0, past_k, past_v, past_logf, w):
    b, t, d = x.shape
    bb, tt = _token_blocks(b, t, MOE_TOKEN_ROWS)
    n_past = past_k.shape[1]
    if n_past:
        f_past = _past_cumsum(past_logf)
        init = f_past[:, n_past - 1:n_past, :]
        past = (past_k, past_v, jnp.swapaxes(f_past, 1, 2))
    else:
        init = jnp.zeros((b, 1, N_HEADS), F32)
        past = None
    pr, q, k, v, sg, logf, f_new, gate = _inproj(x, mod.reshape(b, 6, d), w["g1"], w["wr"], w["wf"], w["wfl"],
                                                 w["wg"], w["qn"], w["kn"], w["fb"], w["gmat"], init, bb, tt)
    y_fox = _fox_attention(q, f_new, sg, k, v, jnp.swapaxes(f_new, 1, 2), past=past, tq=min(t, 512),
                           tk_past=min(max(n_past, 1), 512))
    chunk = min(t, RWKV_CHUNK)
    y_rwkv, wkv_new, shift_new = _rwkv(pr, shift0.reshape(b, 1, RWKV_COLS), wkv0, w["rwkv"], w["gmat"],
                                       chunk, max(1, min(RWKV_CHUNKS_PER_STEP, t // chunk)))
    return y_rwkv, y_fox, gate, wkv_new, shift_new, k, v, logf


def _layer(paths, w):
    n_b = [p[0].shape[0] for p in paths]
    mod_all = _ada(jnp.concatenate([p[1] for p in paths], axis=0), w["w_ada"], w["b_ada"])
    mods, o = [], 0
    for nb in n_b:
        mods.append(mod_all[o:o + nb])
        o += nb
    n_total = sum(p[0].shape[0] * p[0].shape[1] for p in paths)
    mixed, x1s = [], []
    shared, row = None, 0
    for (x, _, shift0, wkv0, pk, pv, plf), mod in zip(paths, mods):
        ya, yb, gate, wkv_new, shift_new, k, v, logf = _mix_path(x, mod, shift0, wkv0, pk, pv, plf, w)
        x1, h2_all, lg_all = _merge(x, ya, yb, gate, mod, w, n_total, row, shared)
        shared = (h2_all, lg_all)
        row += x.shape[0] * x.shape[1]
        mixed.append((wkv_new, shift_new, k, v, logf))
        x1s.append(x1)
    y_tok, wts = _moe_routed(h2_all, lg_all, w)
    outs, row = [], 0
    for x1, mod, st in zip(x1s, mods, mixed):
        y = _final(x1, h2_all, wts, y_tok, mod, w, row)
        row += x1.shape[0] * x1.shape[1]
        outs.append((y,) + st)
    return outs


def kernel(x_prompt, x_sample, c_prompt, c_sample, state_rwkv_wkv, state_rwkv_shift, cache_fox_k, cache_fox_v,
           cache_fox_logf, w_ada, b_ada, norm1_g, norm2_g, w_in, rwkv_mu, rwkv_w0, rwkv_w_lora_b, rwkv_a0,
           rwkv_a_lora_b, rwkv_g_lora_b, rwkv_k_k, rwkv_k_a, rwkv_r_k, rwkv_ln_w, rwkv_ln_b, fox_q_norm,
           fox_k_norm, fox_f_bias, w_out_rwkv, w_out_fox, w_out, w_router, router_bias, w_exp_gate, w_exp_up,
           w_exp_down, w_sh_gate, w_sh_up, w_sh_down):
    raw = dict(w_ada=w_ada, b_ada=b_ada, norm1_g=norm1_g, norm2_g=norm2_g, w_in=w_in, rwkv_mu=rwkv_mu,
               rwkv_w0=rwkv_w0, rwkv_w_lora_b=rwkv_w_lora_b, rwkv_a0=rwkv_a0, rwkv_a_lora_b=rwkv_a_lora_b,
               rwkv_g_lora_b=rwkv_g_lora_b, rwkv_k_k=rwkv_k_k, rwkv_k_a=rwkv_k_a, rwkv_r_k=rwkv_r_k,
               rwkv_ln_w=rwkv_ln_w, rwkv_ln_b=rwkv_ln_b, fox_q_norm=fox_q_norm, fox_k_norm=fox_k_norm,
               fox_f_bias=fox_f_bias, w_out_rwkv=w_out_rwkv, w_out_fox=w_out_fox, w_out=w_out,
               w_router=w_router, router_bias=router_bias, w_exp_gate=w_exp_gate, w_exp_up=w_exp_up,
               w_exp_down=w_exp_down, w_sh_gate=w_sh_gate, w_sh_up=w_sh_up, w_sh_down=w_sh_down)
    assert w_in.shape[0] == 1, "single-layer stack"
    w = _prep(raw)
    bp, tp, _ = x_prompt.shape
    bs, ts, _ = x_sample.shape
    n_past = cache_fox_k.shape[2]
    prompt = (x_prompt, c_prompt, jnp.zeros((bp, RWKV_COLS), F32),
              jnp.zeros((bp, N_HEADS, HEAD_DIM, HEAD_DIM), F32),
              jnp.zeros((bp, 0, WIDTH), F32), jnp.zeros((bp, 0, WIDTH), F32), jnp.zeros((bp, 0, N_HEADS), F32))
    sample = (x_sample, c_sample, state_rwkv_shift[0], state_rwkv_wkv[0],
              cache_fox_k[0].reshape(bs, n_past, WIDTH), cache_fox_v[0].reshape(bs, n_past, WIDTH),
              cache_fox_logf[0])
    (yp, wkv_p, sh_p, k_p, v_p, lf_p), (ysm, wkv_s, sh_s, k_s, v_s, lf_s) = _layer([prompt, sample], w)

    def heads(a):
        return a.reshape((1,) + a.shape[:2] + (N_HEADS, HEAD_DIM))

    return (yp, ysm,
            wkv_p[None], sh_p.reshape(1, bp, RWKV_COLS), heads(k_p), heads(v_p), lf_p[None],
            wkv_s[None], sh_s.reshape(1, bs, RWKV_COLS), heads(k_s), heads(v_s), lf_s[None])
```
